```python
import math
import jax, jax.numpy as jnp
from jax import lax
import numpy as np

D_MODEL = 2048
BATCH = 4
SEQ = 2048
DEPTH = 1
DEC_BATCH = 128
DEC_SEQ = 1
PAST_LEN = 8192
PAGE_SIZE = 128

N_META = 16
HEAD_DIM = 64
N_Q_HEADS = 16
N_KV_HEADS = 4
Q_PER_KV = N_Q_HEADS // N_KV_HEADS
ATTN_WIDTH = N_Q_HEADS * HEAD_DIM
KV_WIDTH = N_KV_HEADS * HEAD_DIM
WINDOW = 128
BLOCK = 128
ROPE_THETA = 10000.0
SSM_WIDTH = D_MODEL // 2
SSM_GROUP = 16
N_SSM_GROUPS = SSM_WIDTH // SSM_GROUP
SSM_STATE = 64
EPS = 1e-6

Q_END = ATTN_WIDTH
K_END = Q_END + KV_WIDTH
V_END = K_END + KV_WIDTH
ZA_END = V_END + ATTN_WIDTH
U_END = ZA_END + SSM_WIDTH
ZS_END = U_END + SSM_WIDTH
GA_END = ZS_END + D_MODEL
IN_WIDTH = GA_END + D_MODEL
IN_SPLITS = (Q_END, K_END, V_END, ZA_END, U_END, ZS_END, GA_END)

kernel_name = "hybrid_swa_sink_s5_meta_decode_step"


def rmsnorm(x, g):
    xf = x.astype(jnp.float32)
    r = lax.rsqrt(jnp.mean(xf * xf, axis=-1, keepdims=True) + EPS)
    return (xf * r * g.astype(jnp.float32)).astype(x.dtype)


def rope(x, pos):
    half = HEAD_DIM // 2
    inv_freq = ROPE_THETA ** (-jnp.arange(half, dtype=jnp.float32) / half)
    ang = pos.astype(jnp.float32)[:, None] * inv_freq[None, :]
    cos = jnp.cos(ang)[:, None, :]
    sin = jnp.sin(ang)[:, None, :]
    xf = x.astype(jnp.float32)
    x1, x2 = xf[..., :half], xf[..., half:]
    return jnp.concatenate([x1 * cos - x2 * sin, x2 * cos + x1 * sin], axis=-1).astype(x.dtype)


def branch_inputs(x, pos, norm_gain, w_in, q_norm_gain, k_norm_gain):
    b, t = x.shape[:2]
    xn = rmsnorm(x, norm_gain)
    h = jnp.einsum('btd,de->bte', xn, w_in)
    q, k, v, z_a, u, z_s, g_a, g_s = jnp.split(h, IN_SPLITS, axis=-1)
    q = rope(rmsnorm(q.reshape(b, t, N_Q_HEADS, HEAD_DIM), q_norm_gain), pos)
    k = rope(rmsnorm(k.reshape(b, t, N_KV_HEADS, HEAD_DIM), k_norm_gain), pos)
    v = v.reshape(b, t, N_KV_HEADS, HEAD_DIM)
    return q, k, v, z_a, u, z_s, g_a, g_s


def sink_attention(q, k, v, mask, sinks):
    s = jnp.einsum('bnqgrd,bnkgd->bngrqk', q, k).astype(jnp.float32) * (HEAD_DIM ** -0.5)
    s = jnp.where(mask[None, :, None, None], s, -jnp.inf)
    sink = sinks.astype(jnp.float32).reshape(1, 1, N_KV_HEADS, Q_PER_KV, 1, 1)
    m = jnp.maximum(jnp.max(s, axis=-1, keepdims=True), sink)
    p = jnp.exp(s - m)
    w = p / (jnp.sum(p, axis=-1, keepdims=True) + jnp.exp(sink - m))
    return jnp.einsum('bngrqk,bnkgd->bnqgrd', w.astype(v.dtype), v)


def prompt_attention(q, k, v, sinks):
    b, length = q.shape[:2]
    pad = BLOCK - N_META
    total = length + pad
    nb = total // BLOCK
    padw = ((0, 0), (pad, 0), (0, 0), (0, 0))
    qb = jnp.pad(q, padw).reshape(b, nb, BLOCK, N_KV_HEADS, Q_PER_KV, HEAD_DIM)

    def band(z):
        zb = jnp.pad(z, padw).reshape(b, nb, BLOCK, N_KV_HEADS, HEAD_DIM)
        prev = jnp.pad(zb[:, :-1], ((0, 0), (1, 0), (0, 0), (0, 0), (0, 0)))
        meta = jnp.broadcast_to(z[:, None, :N_META], (b, nb, N_META, N_KV_HEADS, HEAD_DIM))
        return jnp.concatenate([meta, prev, zb], axis=2)

    kb, vb = band(k), band(v)
    qpos = (jnp.arange(total) - pad).reshape(nb, BLOCK)
    kpos = jnp.arange(nb)[:, None] * BLOCK - BLOCK - pad + jnp.arange(2 * BLOCK)[None, :]
    diff = qpos[:, :, None] - kpos[:, None, :]
    win_ok = (kpos[:, None, :] >= 0) & (diff >= 0) & (diff < WINDOW)
    meta_ok = jnp.arange(N_META)[None, None, :] <= qpos[:, :, None] - WINDOW
    mask = jnp.concatenate([meta_ok, win_ok], axis=-1)
    o = sink_attention(qb, kb, vb, mask, sinks)
    return o.reshape(b, total, ATTN_WIDTH)[:, pad:]


def sample_attention(q, k, v, meta_k, meta_v, win_k, win_v, sinks):
    b, s = q.shape[:2]
    kc = jnp.concatenate([meta_k, win_k, k], axis=1)[:, None]
    vc = jnp.concatenate([meta_v, win_v, v], axis=1)[:, None]
    qpos = PAST_LEN + jnp.arange(s)
    buf_pos = PAST_LEN - WINDOW + jnp.arange(WINDOW)
    d_buf = qpos[:, None] - buf_pos[None, :]
    buf_ok = (buf_pos[None, :] >= 0) & (d_buf < WINDOW)
    d_new = qpos[:, None] - qpos[None, :]
    new_ok = (d_new >= 0) & (d_new < WINDOW)
    meta_ok = jnp.arange(N_META)[None, :] <= qpos[:, None] - WINDOW
    mask = jnp.concatenate([meta_ok, buf_ok, new_ok], axis=-1)[None]
    o = sink_attention(q.reshape(b, 1, s, N_KV_HEADS, Q_PER_KV, HEAD_DIM), kc, vc, mask, sinks)
    return o.reshape(b, s, ATTN_WIDTH)


def ssm_combine(left, right):
    a_l, b_l = left
    a_r, b_r = right
    return a_l * a_r, a_r * b_l + b_r


def s5_branch(u, h0, a_re, a_im, log_dt, b_re, b_im, c_re, c_im, d_skip, w_glu, b_glu):
    f32 = jnp.float32
    b, t = u.shape[:2]
    lam = lax.complex(a_re.astype(f32), a_im.astype(f32))
    dt = jnp.exp(log_dt.astype(f32))[:, None]
    lam_bar = jnp.exp(lam * dt)
    b_bar = ((lam_bar - 1.0) / lam)[..., None] * lax.complex(b_re.astype(f32), b_im.astype(f32))
    c = lax.complex(c_re.astype(f32), c_im.astype(f32))
    uf = u.astype(f32).reshape(b, t, N_SSM_GROUPS, SSM_GROUP)
    bu = jnp.einsum('gph,btgh->btgp', b_bar, uf.astype(jnp.complex64))
    a = jnp.broadcast_to(lam_bar, bu.shape)
    a_cum, h = lax.associative_scan(ssm_combine, (a, bu), axis=1)
    h = h + a_cum * h0[:, None]
    y = jnp.einsum('ghp,btgp->btgh', c, h).real + d_skip.astype(f32).reshape(N_SSM_GROUPS, SSM_GROUP) * uf
    y = jax.nn.gelu(y.reshape(b, t, SSM_WIDTH))
    y = y * jax.nn.sigmoid(y @ w_glu.astype(f32) + b_glu.astype(f32))
    return y.astype(u.dtype), h[:, -1]


def merge(x, o_attn, z_a, y_ssm, z_s, g_a, g_s, w_attn_out, w_ssm_out, w_out):
    br_a = (o_attn * jax.nn.silu(z_a)) @ w_attn_out
    br_s = (y_ssm * jax.nn.silu(z_s)) @ w_ssm_out
    return x + (jax.nn.sigmoid(g_a) * br_a + jax.nn.sigmoid(g_s) * br_s) @ w_out


def setup_inputs(seed: int = 0) -> dict:
    key = jax.random.key(seed)
    ks = jax.random.split(key, 32)
    f32 = jnp.float32
    nrm = lambda k, shape, scale: jax.random.normal(k, shape, f32) * scale
    n_idx = jnp.arange(SSM_STATE, dtype=f32)
    a_re = -0.5 + nrm(ks[0], (DEPTH, N_SSM_GROUPS, SSM_STATE), 0.01)
    a_im = math.pi * n_idx[None, None, :] + nrm(ks[1], (DEPTH, N_SSM_GROUPS, SSM_STATE), 0.01)
    log_dt = jax.random.uniform(ks[2], (DEPTH, N_SSM_GROUPS), f32, math.log(1e-3), math.log(1e-1))
    return {
        "x_prompt": nrm(ks[3], (BATCH, SEQ, D_MODEL), 1.0),
        "x_sample": nrm(ks[4], (DEC_BATCH, DEC_SEQ, D_MODEL), 1.0),
        "cache_win_k": nrm(ks[5], (DEPTH, DEC_BATCH, WINDOW, N_KV_HEADS, HEAD_DIM), 1.0),
        "cache_win_v": nrm(ks[6], (DEPTH, DEC_BATCH, WINDOW, N_KV_HEADS, HEAD_DIM), 1.0),
        "cache_meta_k": nrm(ks[7], (DEPTH, DEC_BATCH, N_META, N_KV_HEADS, HEAD_DIM), 1.0),
        "cache_meta_v": nrm(ks[8], (DEPTH, DEC_BATCH, N_META, N_KV_HEADS, HEAD_DIM), 1.0),
        "state_ssm_re": nrm(ks[9], (DEPTH, DEC_BATCH, N_SSM_GROUPS, SSM_STATE), 0.5),
        "state_ssm_im": nrm(ks[10], (DEPTH, DEC_BATCH, N_SSM_GROUPS, SSM_STATE), 0.5),
        "meta_tokens": nrm(ks[11], (N_META, D_MODEL), 1.0),
        "norm_gain": 1.0 + nrm(ks[12], (DEPTH, D_MODEL), 0.02),
        "w_in": nrm(ks[13], (DEPTH, D_MODEL, IN_WIDTH), D_MODEL ** -0.5),
        "q_norm_gain": 1.0 + nrm(ks[14], (DEPTH, HEAD_DIM), 0.02),
        "k_norm_gain": 1.0 + nrm(ks[15], (DEPTH, HEAD_DIM), 0.02),
        "sinks": nrm(ks[16], (DEPTH, N_Q_HEADS), 0.5),
        "a_re": a_re,
        "a_im": a_im,
        "log_dt": log_dt,
        "b_re": nrm(ks[17], (DEPTH, N_SSM_GROUPS, SSM_STATE, SSM_GROUP), (2.0 * SSM_GROUP) ** -0.5),
        "b_im": nrm(ks[18], (DEPTH, N_SSM_GROUPS, SSM_STATE, SSM_GROUP), (2.0 * SSM_GROUP) ** -0.5),
        "c_re": nrm(ks[19], (DEPTH, N_SSM_GROUPS, SSM_GROUP, SSM_STATE), (2.0 * SSM_STATE) ** -0.5),
        "c_im": nrm(ks[20], (DEPTH, N_SSM_GROUPS, SSM_GROUP, SSM_STATE), (2.0 * SSM_STATE) ** -0.5),
        "d_skip": nrm(ks[21], (DEPTH, SSM_WIDTH), 1.0),
        "w_glu": nrm(ks[22], (DEPTH, SSM_WIDTH, SSM_WIDTH), SSM_WIDTH ** -0.5),
        "b_glu": nrm(ks[23], (DEPTH, SSM_WIDTH), 0.01),
        "w_attn_out": nrm(ks[24], (DEPTH, ATTN_WIDTH, D_MODEL), ATTN_WIDTH ** -0.5),
        "w_ssm_out": nrm(ks[25], (DEPTH, SSM_WIDTH, D_MODEL), SSM_WIDTH ** -0.5),
        "w_out": nrm(ks[26], (DEPTH, D_MODEL, D_MODEL), D_MODEL ** -0.5),
    }


def reference(x_prompt, x_sample, cache_win_k, cache_win_v, cache_meta_k, cache_meta_v,
              state_ssm_re, state_ssm_im, meta_tokens, norm_gain, w_in, q_norm_gain, k_norm_gain,
              sinks, a_re, a_im, log_dt, b_re, b_im, c_re, c_im, d_skip, w_glu, b_glu,
              w_attn_out, w_ssm_out, w_out):
    b_p = x_prompt.shape[0]
    b_s, s_len = x_sample.shape[:2]
    meta = jnp.broadcast_to(meta_tokens.astype(x_prompt.dtype)[None], (b_p, N_META, D_MODEL))
    xp = jnp.concatenate([meta, x_prompt], axis=1)
    xs = x_sample
    pos_p = jnp.arange(xp.shape[1])
    pos_s = PAST_LEN + jnp.arange(s_len)
    p_win_k, p_win_v, p_meta_k, p_meta_v, p_re, p_im = [], [], [], [], [], []
    s_win_k, s_win_v, s_re, s_im = [], [], [], []
    for l in range(DEPTH):
        ssm_w = (a_re[l], a_im[l], log_dt[l], b_re[l], b_im[l], c_re[l], c_im[l], d_skip[l], w_glu[l], b_glu[l])
        out_w = (w_attn_out[l], w_ssm_out[l], w_out[l])
        q, k, v, z_a, u, z_s, g_a, g_s = branch_inputs(xp, pos_p, norm_gain[l], w_in[l], q_norm_gain[l], k_norm_gain[l])
        o_a = prompt_attention(q, k, v, sinks[l])
        h0 = jnp.zeros((b_p, N_SSM_GROUPS, SSM_STATE), jnp.complex64)
        y_s, h_last = s5_branch(u, h0, *ssm_w)
        xp = merge(xp, o_a, z_a, y_s, z_s, g_a, g_s, *out_w)
        p_win_k.append(k[:, -WINDOW:])
        p_win_v.append(v[:, -WINDOW:])
        p_meta_k.append(k[:, :N_META])
        p_meta_v.append(v[:, :N_META])
        p_re.append(h_last.real)
        p_im.append(h_last.imag)
        q, k, v, z_a, u, z_s, g_a, g_s = branch_inputs(xs, pos_s, norm_gain[l], w_in[l], q_norm_gain[l], k_norm_gain[l])
        o_a = sample_attention(q, k, v, cache_meta_k[l], cache_meta_v[l], cache_win_k[l], cache_win_v[l], sinks[l])
        h0 = lax.complex(state_ssm_re[l].astype(jnp.float32), state_ssm_im[l].astype(jnp.float32))
        y_s, h_last = s5_branch(u, h0, *ssm_w)
        xs = merge(xs, o_a, z_a, y_s, z_s, g_a, g_s, *out_w)
        s_win_k.append(jnp.concatenate([cache_win_k[l], k], axis=1)[:, -WINDOW:])
        s_win_v.append(jnp.concatenate([cache_win_v[l], v], axis=1)[:, -WINDOW:])
        s_re.append(h_last.real)
        s_im.append(h_last.imag)
    y_prompt = xp[:, N_META:]
    return (y_prompt, xs, jnp.stack(p_win_k), jnp.stack(p_win_v), jnp.stack(p_meta_k), jnp.stack(p_meta_v),
            jnp.stack(p_re), jnp.stack(p_im), jnp.stack(s_win_k), jnp.stack(s_win_v), jnp.stack(s_re), jnp.stack(s_im))
```

```python
import functools

import jax
import jax.numpy as jnp
from jax import lax
from jax.experimental import pallas as pl
from jax.experimental.pallas import tpu as pltpu

F32 = jnp.float32
BF16 = jnp.bfloat16

D_MODEL = 2048
N_META = 16
HEAD_DIM = 64
N_Q_HEADS = 16
N_KV_HEADS = 4
ATTN_WIDTH = N_Q_HEADS * HEAD_DIM
KV_WIDTH = N_KV_HEADS * HEAD_DIM
WINDOW = 128
BLOCK = 128
ROPE_THETA = 10000.0
SSM_WIDTH = D_MODEL // 2
SSM_GROUP = 16
N_SSM_GROUPS = SSM_WIDTH // SSM_GROUP
SSM_STATE = 64
EPS = 1e-6
PAST_LEN = 8192

LANES = 128
CHUNK = 8
GROUPS_PER_TILE = LANES // SSM_GROUP
N_GROUP_TILES = N_SSM_GROUPS // GROUPS_PER_TILE
CHUNK_WIDTH = CHUNK * LANES
STATE_TILE = GROUPS_PER_TILE * SSM_STATE
SCAN_LEVELS = 8

COL_Q = 0
COL_ZA = 1024
COL_U = 2048
COL_ZS = 3072
COL_GA = 4096
COL_GS = 6144
COL_KV = 8192
IN_WIDTH = 8704
TN = 512
KV_TILE = COL_KV // TN
VMEM_LIMIT = 56 * 1024 * 1024
NEG = -1e30


def _inproj_kernel(x_ref, gain_ref, w_ref, qg_ref, kg_ref, e_ref, cos_ref, sa_ref, sb_ref,
                   o_ref, xn_ref):
    j = pl.program_id(1)

    @pl.when(j == 0)
    def _():
        x = x_ref[...]
        r = lax.rsqrt(jnp.mean(x * x, axis=-1, keepdims=True) + EPS)
        xn_ref[...] = (x * r * gain_ref[...]).astype(BF16)

    acc = jnp.dot(xn_ref[...], w_ref[...], preferred_element_type=F32)

    def head_norm_rope(a, gain, e):
        sq = a * a
        hi = sq.astype(BF16)
        lo = (sq - hi.astype(F32)).astype(BF16)
        ms = (jnp.dot(hi, e, preferred_element_type=F32)
              + jnp.dot(lo, e, preferred_element_type=F32))
        xn = a * lax.rsqrt(ms + EPS) * gain
        cos, sa, sb = cos_ref[...], sa_ref[...], sb_ref[...]
        outs = []
        for c in range(a.shape[1] // LANES):
            xc = xn[:, c * LANES:(c + 1) * LANES]
            outs.append(xc * cos + pltpu.roll(xc, LANES - HEAD_DIM // 2, 1) * sa
                        + pltpu.roll(xc, HEAD_DIM // 2, 1) * sb)
        return jnp.concatenate(outs, axis=1)

    @pl.when(j < COL_ZA // TN)
    def _():
        o_ref[...] = head_norm_rope(acc, qg_ref[...], e_ref[...]) * (HEAD_DIM ** -0.5)

    @pl.when(j == KV_TILE)
    def _():
        k = head_norm_rope(acc[:, :KV_WIDTH], kg_ref[...], e_ref[:KV_WIDTH, :KV_WIDTH])
        o_ref[...] = jnp.concatenate([k, acc[:, KV_WIDTH:]], axis=1)

    @pl.when(jnp.logical_and(j >= COL_ZA // TN, j != KV_TILE))
    def _():
        o_ref[...] = acc


def _inproj(x, gain, w_bf, qg, kg, e, cos, sa, sb, tm):
    m = x.shape[0]
    n_tab = cos.shape[0] // tm
    return pl.pallas_call(
        _inproj_kernel,
        grid=(m // tm, IN_WIDTH // TN),
        in_specs=[
            pl.BlockSpec((tm, D_MODEL), lambda i, j: (i, 0)),
            pl.BlockSpec((1, D_MODEL), lambda i, j: (0, 0)),
            pl.BlockSpec((D_MODEL, TN), lambda i, j: (0, j)),
            pl.BlockSpec((1, TN), lambda i, j: (0, 0)),
            pl.BlockSpec((1, KV_WIDTH), lambda i, j: (0, 0)),
            pl.BlockSpec((TN, TN), lambda i, j: (0, 0)),
            pl.BlockSpec((tm, LANES), lambda i, j: (i % n_tab, 0)),
            pl.BlockSpec((tm, LANES), lambda i, j: (i % n_tab, 0)),
            pl.BlockSpec((tm, LANES), lambda i, j: (i % n_tab, 0)),
        ],
        out_specs=pl.BlockSpec((tm, TN), lambda i, j: (i, j)),
        out_shape=jax.ShapeDtypeStruct((m, IN_WIDTH), F32),
        scratch_shapes=[pltpu.VMEM((tm, D_MODEL), BF16)],
        compiler_params=pltpu.CompilerParams(
            dimension_semantics=("arbitrary", "arbitrary"), vmem_limit_bytes=VMEM_LIMIT),
        name="inproj",
    )(x, gain, w_bf, qg, kg, e, cos, sa, sb)


def _dup_head(x, g, lo):
    r = pltpu.roll(x, HEAD_DIM, 1)
    return jnp.where(lo, x, r) if g % 2 == 0 else jnp.where(lo, r, x)


def _prompt_attn_kernel(sinks_ref, q_ref, kvc_ref, kvp_ref, kvm_ref, o_ref):
    i = pl.program_id(1)
    lo = lax.broadcasted_iota(jnp.int32, (1, LANES), 1) < HEAD_DIM
    rows = 4 * BLOCK
    qi = lax.broadcasted_iota(jnp.int32, (rows, BLOCK), 0) % BLOCK
    kj = lax.broadcasted_iota(jnp.int32, (rows, BLOCK), 1)
    mask_prev = jnp.logical_and(kj > qi, i > 0)
    mask_cur = kj <= qi
    head_row = lax.broadcasted_iota(jnp.int32, (rows, 1), 0) // BLOCK
    nt = (((1,), (1,)), ((), ()))

    for g in range(N_KV_HEADS):
        c = g // 2
        ks = slice(c * LANES, (c + 1) * LANES)
        vs = slice(KV_WIDTH + c * LANES, KV_WIDTH + (c + 1) * LANES)
        k_m = _dup_head(kvm_ref[:, ks], g, lo).astype(BF16)
        k_p = _dup_head(kvp_ref[:, ks], g, lo).astype(BF16)
        k_c = _dup_head(kvc_ref[:, ks], g, lo).astype(BF16)
        v_m = _dup_head(kvm_ref[:, vs], g, lo).astype(BF16)
        v_p = _dup_head(kvp_ref[:, vs], g, lo).astype(BF16)
        v_c = _dup_head(kvc_ref[:, vs], g, lo).astype(BF16)
        qa = q_ref[:, (2 * g) * LANES:(2 * g + 1) * LANES]
        qb = q_ref[:, (2 * g + 1) * LANES:(2 * g + 2) * LANES]
        q4 = jnp.concatenate([jnp.where(lo, qa, 0.0), jnp.where(lo, 0.0, qa),
                              jnp.where(lo, qb, 0.0), jnp.where(lo, 0.0, qb)],
                             axis=0).astype(BF16)
        s_m = lax.dot_general(q4, k_m, nt, preferred_element_type=F32)
        s_p = lax.dot_general(q4, k_p, nt, preferred_element_type=F32)
        s_c = lax.dot_general(q4, k_c, nt, preferred_element_type=F32)
        s_p = jnp.where(mask_prev, s_p, NEG)
        s_c = jnp.where(mask_cur, s_c, NEG)
        sink = jnp.where(head_row == 0, sinks_ref[4 * g],
                         jnp.where(head_row == 1, sinks_ref[4 * g + 1],
                                   jnp.where(head_row == 2, sinks_ref[4 * g + 2],
                                             sinks_ref[4 * g + 3])))
        m = jnp.maximum(jnp.maximum(jnp.max(s_m, axis=-1, keepdims=True),
                                    jnp.max(s_p, axis=-1, keepdims=True)),
                        jnp.maximum(jnp.max(s_c, axis=-1, keepdims=True), sink))
        p_m = jnp.exp(s_m - m)
        p_p = jnp.exp(s_p - m)
        p_c = jnp.exp(s_c - m)
        den = (jnp.sum(p_m, axis=-1, keepdims=True) + jnp.sum(p_p, axis=-1, keepdims=True)
               + jnp.sum(p_c, axis=-1, keepdims=True) + jnp.exp(sink - m))
        inv = 1.0 / den
        o = (jnp.dot((p_m * inv).astype(BF16), v_m, preferred_element_type=F32)
             + jnp.dot((p_p * inv).astype(BF16), v_p, preferred_element_type=F32)
             + jnp.dot((p_c * inv).astype(BF16), v_c, preferred_element_type=F32))
        o_ref[:, (2 * g) * LANES:(2 * g + 1) * LANES] = jnp.where(
            lo, o[0:BLOCK], o[BLOCK:2 * BLOCK])
        o_ref[:, (2 * g + 1) * LANES:(2 * g + 2) * LANES] = jnp.where(
            lo, o[2 * BLOCK:3 * BLOCK], o[3 * BLOCK:4 * BLOCK])


def _prompt_attention(sinks, h_p, h_s, batch, seq):
    nb = seq // BLOCK
    kvb = COL_KV // (2 * KV_WIDTH)
    return pl.pallas_call(
        _prompt_attn_kernel,
        grid=(batch, nb),
        in_specs=[
            pl.BlockSpec(memory_space=pltpu.SMEM),
            pl.BlockSpec((BLOCK, ATTN_WIDTH), lambda b, i: (b * nb + i, 0)),
            pl.BlockSpec((BLOCK, 2 * KV_WIDTH), lambda b, i: (b * nb + i, kvb)),
            pl.BlockSpec((BLOCK, 2 * KV_WIDTH),
                         lambda b, i: (b * nb + jnp.maximum(i - 1, 0), kvb)),
            pl.BlockSpec((N_META, 2 * KV_WIDTH), lambda b, i: (h_s.shape[0] // N_META - 1, kvb)),
        ],
        out_specs=pl.BlockSpec((BLOCK, ATTN_WIDTH), lambda b, i: (b * nb + i, 0)),
        out_shape=jax.ShapeDtypeStruct((batch * seq, ATTN_WIDTH), F32),
        compiler_params=pltpu.CompilerParams(
            dimension_semantics=("arbitrary", "arbitrary"), vmem_limit_bytes=VMEM_LIMIT),
        name="prompt_attn",
    )(sinks, h_p, h_p, h_p, h_s)


def _sample_attn_kernel(qp_ref, knew_ref, vnew_ref, ck_ref, cv_ref, mk_ref, mv_ref, sink_ref,
                        o_ref, ok_ref, ov_ref):
    ok_ref[:, 0:WINDOW - 1, :] = ck_ref[:, 1:WINDOW, :]
    ok_ref[:, WINDOW - 1:WINDOW, :] = knew_ref[...]
    ov_ref[:, 0:WINDOW - 1, :] = cv_ref[:, 1:WINDOW, :]
    ov_ref[:, WINDOW - 1:WINDOW, :] = vnew_ref[...]
    qp = qp_ref[...].astype(BF16)
    k_w = ok_ref[...].astype(BF16)
    v_w = ov_ref[...].astype(BF16)
    k_m = mk_ref[...].astype(BF16)
    v_m = mv_ref[...].astype(BF16)
    s_w = jnp.einsum("bhc,bkc->bhk", qp, k_w, preferred_element_type=F32)
    s_m = jnp.einsum("bhc,bkc->bhk", qp, k_m, preferred_element_type=F32)
    sink = sink_ref[...][None]
    m = jnp.maximum(jnp.maximum(jnp.max(s_w, axis=-1, keepdims=True),
                                jnp.max(s_m, axis=-1, keepdims=True)), sink)
    p_w = jnp.exp(s_w - m)
    p_m = jnp.exp(s_m - m)
    den = (jnp.sum(p_w, axis=-1, keepdims=True) + jnp.sum(p_m, axis=-1, keepdims=True)
           + jnp.exp(sink - m))
    inv = 1.0 / den
    o = (jnp.einsum("bhk,bkc->bhc", (p_w * inv).astype(BF16), v_w, preferred_element_type=F32)
         + jnp.einsum("bhk,bkc->bhc", (p_m * inv).astype(BF16), v_m,
                      preferred_element_type=F32))
    hh = lax.broadcasted_iota(jnp.int32, (1, N_Q_HEADS, KV_WIDTH), 1) // (N_Q_HEADS // N_KV_HEADS)
    cc = lax.broadcasted_iota(jnp.int32, (1, N_Q_HEADS, KV_WIDTH), 2) // HEAD_DIM
    o = jnp.where(hh == cc, o, 0.0)
    o = o[:, :, :LANES] + o[:, :, LANES:]
    o_ref[...] = o + pltpu.roll(o, HEAD_DIM, 2)


def _sample_attention(qp, h_s3, cache_k, cache_v, meta_k, meta_v, sink_col, bt):
    n = cache_k.shape[0]
    kvb = COL_KV // KV_WIDTH
    win_spec = pl.BlockSpec((bt, WINDOW, KV_WIDTH), lambda t: (t, 0, 0))
    meta_spec = pl.BlockSpec((bt, N_META, KV_WIDTH), lambda t: (t, 0, 0))
    return pl.pallas_call(
        _sample_attn_kernel,
        grid=(n // bt,),
        in_specs=[
            pl.BlockSpec((bt, N_Q_HEADS, KV_WIDTH), lambda t: (t, 0, 0)),
            pl.BlockSpec((bt, 1, KV_WIDTH), lambda t: (t, 0, kvb)),
            pl.BlockSpec((bt, 1, KV_WIDTH), lambda t: (t, 0, kvb + 1)),
            win_spec, win_spec, meta_spec, meta_spec,
            pl.BlockSpec((N_Q_HEADS, 1), lambda t: (0, 0)),
        ],
        out_specs=[
            pl.BlockSpec((bt, N_Q_HEADS, LANES), lambda t: (t, 0, 0)),
            win_spec, win_spec,
        ],
        out_shape=[
            jax.ShapeDtypeStruct((n, N_Q_HEADS, LANES), F32),
            jax.ShapeDtypeStruct(cache_k.shape, F32),
            jax.ShapeDtypeStruct(cache_v.shape, F32),
        ],
        compiler_params=pltpu.CompilerParams(
            dimension_semantics=("arbitrary",), vmem_limit_bytes=VMEM_LIMIT),
        name="sample_attn",
    )(qp, h_s3, h_s3, cache_k, cache_v, meta_k, meta_v, sink_col)


def _gelu_tanh(x):
    return 0.5 * x * (1.0 + jnp.tanh(0.7978845608028654 * (x + 0.044715 * (x * x * x))))


def _prompt_ssm_kernel(u_ref, um_ref, t8_ref, win_ref, wst_ref, ar_ref, ai_ref, d_ref,
                       y_ref, sre_ref, sim_ref):
    nc = u_ref.shape[1]
    ucat = jnp.concatenate([u_ref[0, :, j, :] for j in range(CHUNK)], axis=1)
    ub = ucat.astype(BF16)
    win = win_ref[0]
    g = jnp.dot(ub, win, preferred_element_type=F32)
    gr, gi = g[:, :STATE_TILE], g[:, STATE_TILE:]

    um = jnp.concatenate([um_ref[:, j, :] for j in range(CHUNK)], axis=1)
    gm = jnp.dot(um.astype(BF16), win, preferred_element_type=F32)
    a_r, a_i = ar_ref[0, 0:1, :], ai_ref[0, 0:1, :]
    m0r, m0i = gm[0:1, :STATE_TILE], gm[0:1, STATE_TILE:]
    s0r = a_r * m0r - a_i * m0i + gm[1:2, :STATE_TILE]
    s0i = a_r * m0i + a_i * m0r + gm[1:2, STATE_TILE:]

    row = lax.broadcasted_iota(jnp.int32, (nc, 1), 0)
    first = row == 0
    gr = gr + jnp.where(first, a_r * s0r - a_i * s0i, 0.0)
    gi = gi + jnp.where(first, a_r * s0i + a_i * s0r, 0.0)
    for k in range(SCAN_LEVELS):
        d = 1 << k
        if d >= nc:
            break
        k_r, k_i = ar_ref[0, k:k + 1, :], ai_ref[0, k:k + 1, :]
        keep = row >= d
        sh_r = jnp.where(keep, pltpu.roll(gr, d, 0), 0.0)
        sh_i = jnp.where(keep, pltpu.roll(gi, d, 0), 0.0)
        gr, gi = gr + (k_r * sh_r - k_i * sh_i), gi + (k_r * sh_i + k_i * sh_r)

    sre_ref[0, 0] = gr[nc - 1:nc, :]
    sim_ref[0, 0] = gi[nc - 1:nc, :]
    pr = jnp.where(first, s0r, pltpu.roll(gr, 1, 0))
    pi = jnp.where(first, s0i, pltpu.roll(gi, 1, 0))
    sp = jnp.concatenate([pr, pi], axis=1).astype(BF16)
    y2 = (jnp.dot(ub, t8_ref[0], preferred_element_type=F32)
          + jnp.dot(sp, wst_ref[0], preferred_element_type=F32))
    dsk = d_ref[...]
    for j in range(CHUNK):
        sl = slice(j * LANES, (j + 1) * LANES)
        y_ref[0, :, j, :] = _gelu_tanh(y2[:, sl] + dsk * ucat[:, sl])


def _prompt_ssm(h_p4, um3, t8, win, wst, a_r, a_i, d_skip, batch, nc):
    ub = COL_U // LANES
    wspec = pl.BlockSpec((1, CHUNK_WIDTH, CHUNK_WIDTH), lambda gt, b: (gt, 0, 0))
    aspec = pl.BlockSpec((1, SCAN_LEVELS, STATE_TILE), lambda gt, b: (gt, 0, 0))
    sspec = pl.BlockSpec((1, 1, 1, STATE_TILE), lambda gt, b: (b, gt, 0, 0))
    return pl.pallas_call(
        _prompt_ssm_kernel,
        grid=(N_GROUP_TILES, batch),
        in_specs=[
            pl.BlockSpec((1, nc, CHUNK, LANES), lambda gt, b: (b, 0, 0, ub + gt)),
            pl.BlockSpec((CHUNK, CHUNK, LANES), lambda gt, b: (0, 0, gt)),
            wspec, wspec, wspec, aspec, aspec,
            pl.BlockSpec((1, LANES), lambda gt, b: (0, gt)),
        ],
        out_specs=[
            pl.BlockSpec((1, nc, CHUNK, LANES), lambda gt, b: (b, 0, 0, gt)),
            sspec, sspec,
        ],
        out_shape=[
            jax.ShapeDtypeStruct((batch, nc, CHUNK, SSM_WIDTH), F32),
            jax.ShapeDtypeStruct((batch, N_GROUP_TILES, 1, STATE_TILE), F32),
            jax.ShapeDtypeStruct((batch, N_GROUP_TILES, 1, STATE_TILE), F32),
        ],
        compiler_params=pltpu.CompilerParams(
            dimension_semantics=("arbitrary", "arbitrary"), vmem_limit_bytes=VMEM_LIMIT),
        name="prompt_ssm",
    )(h_p4, um3, t8, win, wst, a_r, a_i, d_skip)


def _sample_ssm_kernel(u_ref, hr_ref, hi_ref, b_ref, wc_ref, lr_ref, li_ref, d_ref,
                       y_ref, sre_ref, sim_ref):
    u = u_ref[...]
    bu = jnp.dot(u.astype(BF16), b_ref[0], preferred_element_type=F32)
    l_r, l_i = lr_ref[0], li_ref[0]
    h_r, h_i = hr_ref[...], hi_ref[...]
    n_r = l_r * h_r - l_i * h_i + bu[:, :STATE_TILE]
    n_i = l_r * h_i + l_i * h_r + bu[:, STATE_TILE:]
    sre_ref[...] = n_r
    sim_ref[...] = n_i
    hn = jnp.concatenate([n_r, n_i], axis=1).astype(BF16)
    y = jnp.dot(hn, wc_ref[0], preferred_element_type=F32) + d_ref[...] * u
    y_ref[...] = _gelu_tanh(y)


def _sample_ssm(h_s, h0r, h0i, b7, wc, l_r, l_i, d_skip, n):
    ub = COL_U // LANES
    st = pl.BlockSpec((n, STATE_TILE), lambda gt: (0, gt))
    lam = pl.BlockSpec((1, 1, STATE_TILE), lambda gt: (gt, 0, 0))
    return pl.pallas_call(
        _sample_ssm_kernel,
        grid=(N_GROUP_TILES,),
        in_specs=[
            pl.BlockSpec((n, LANES), lambda gt: (0, ub + gt)),
            st, st,
            pl.BlockSpec((1, LANES, 2 * STATE_TILE), lambda gt: (gt, 0, 0)),
            pl.BlockSpec((1, 2 * STATE_TILE, LANES), lambda gt: (gt, 0, 0)),
            lam, lam,
            pl.BlockSpec((1, LANES), lambda gt: (0, gt)),
        ],
        out_specs=[pl.BlockSpec((n, LANES), lambda gt: (0, gt)), st, st],
        out_shape=[
            jax.ShapeDtypeStruct((n, SSM_WIDTH), F32),
            jax.ShapeDtypeStruct((n, N_SSM_GROUPS * SSM_STATE), F32),
            jax.ShapeDtypeStruct((n, N_SSM_GROUPS * SSM_STATE), F32),
        ],
        compiler_params=pltpu.CompilerParams(
            dimension_semantics=("arbitrary",), vmem_limit_bytes=VMEM_LIMIT),
        name="sample_ssm",
    )(h_s, h0r, h0i, b7, wc, l_r, l_i, d_skip)


def _merge_kernel(o_ref, za_ref, zs_ref, ga_ref, gs_ref, y_ref, x_ref, bglu_ref,
                  wglu_ref, wa_ref, ws_ref, wo_ref, out_ref):
    y = y_ref[...]
    t = jnp.dot(y.astype(BF16), wglu_ref[...], preferred_element_type=F32) + bglu_ref[...]
    y = y * jax.nn.sigmoid(t)
    za = za_ref[...]
    zs = zs_ref[...]
    a_in = (o_ref[...] * (za * jax.nn.sigmoid(za))).astype(BF16)
    s_in = (y * (zs * jax.nn.sigmoid(zs))).astype(BF16)
    br_a = jnp.dot(a_in, wa_ref[...], preferred_element_type=F32)
    br_s = jnp.dot(s_in, ws_ref[...], preferred_element_type=F32)
    mix = (jax.nn.sigmoid(ga_ref[...]) * br_a + jax.nn.sigmoid(gs_ref[...]) * br_s).astype(BF16)
    out_ref[...] = x_ref[...] + jnp.dot(mix, wo_ref[...], preferred_element_type=F32)


def _merge(o, h, y, x, b_glu, w_glu, w_a, w_s, w_o, tm):
    m = x.shape[0]
    const = lambda shape: pl.BlockSpec(shape, lambda i: (0, 0), pipeline_mode=pl.Buffered(1))
    return pl.pallas_call(
        _merge_kernel,
        grid=(m // tm,),
        in_specs=[
            pl.BlockSpec((tm, ATTN_WIDTH), lambda i: (i, 0)),
            pl.BlockSpec((tm, ATTN_WIDTH), lambda i: (i, COL_ZA // ATTN_WIDTH)),
            pl.BlockSpec((tm, SSM_WIDTH), lambda i: (i, COL_ZS // SSM_WIDTH)),
            pl.BlockSpec((tm, D_MODEL), lambda i: (i, COL_GA // D_MODEL)),
            pl.BlockSpec((tm, D_MODEL), lambda i: (i, COL_GS // D_MODEL)),
            pl.BlockSpec((tm, SSM_WIDTH), lambda i: (i, 0)),
            pl.BlockSpec((tm, D_MODEL), lambda i: (i, 0)),
            const((1, SSM_WIDTH)),
            const((SSM_WIDTH, SSM_WIDTH)),
            const((ATTN_WIDTH, D_MODEL)),
            const((SSM_WIDTH, D_MODEL)),
            const((D_MODEL, D_MODEL)),
        ],
        out_specs=pl.BlockSpec((tm, D_MODEL), lambda i: (i, 0)),
        out_shape=jax.ShapeDtypeStruct((m, D_MODEL), F32),
        compiler_params=pltpu.CompilerParams(
            dimension_semantics=("arbitrary",), vmem_limit_bytes=VMEM_LIMIT),
        name="merge",
    )(o, h, h, h, h, y, x, b_glu, w_glu, w_a, w_s, w_o)


def _rope_tables(pos):
    half = HEAD_DIM // 2
    inv_freq = ROPE_THETA ** (-jnp.arange(half, dtype=F32) / half)
    ang = pos.astype(F32)[:, None] * inv_freq[None, :]
    cos, sin = jnp.cos(ang), jnp.sin(ang)
    zero = jnp.zeros_like(sin)
    reps = LANES // HEAD_DIM
    cos_t = jnp.tile(jnp.concatenate([cos, cos], axis=1), (1, reps))
    sin_a = jnp.tile(jnp.concatenate([-sin, zero], axis=1), (1, reps))
    sin_b = jnp.tile(jnp.concatenate([zero, sin], axis=1), (1, reps))
    return cos_t, sin_a, sin_b


def _cmul(ar, ai, br, bi):
    return ar * br - ai * bi, ar * bi + ai * br


def _ssm_tables(a_re, a_im, log_dt, b_re, b_im, c_re, c_im):
    hp = lax.Precision.HIGHEST
    dt = jnp.exp(log_dt)[:, None]
    mag = jnp.exp(a_re * dt)
    lr, li = mag * jnp.cos(a_im * dt), mag * jnp.sin(a_im * dt)
    den = a_re * a_re + a_im * a_im
    fr = ((lr - 1.0) * a_re + li * a_im) / den
    fi = (li * a_re - (lr - 1.0) * a_im) / den
    bbr = fr[..., None] * b_re - fi[..., None] * b_im
    bbi = fr[..., None] * b_im + fi[..., None] * b_re
    pw_r, pw_i = [jnp.ones_like(lr)], [jnp.zeros_like(lr)]
    for _ in range(CHUNK):
        nr, ni = _cmul(pw_r[-1], pw_i[-1], lr, li)
        pw_r.append(nr)
        pw_i.append(ni)
    pw_r, pw_i = jnp.stack(pw_r), jnp.stack(pw_i)
    wr, wi = _cmul(pw_r[:CHUNK, :, :, None], pw_i[:CHUNK, :, :, None], bbr[None], bbi[None])
    kern = (jnp.einsum("gop,ngpi->ngio", c_re, wr, precision=hp)
            - jnp.einsum("gop,ngpi->ngio", c_im, wi, precision=hp))
    eye = jnp.eye(GROUPS_PER_TILE, dtype=F32)
    gt, gp = N_GROUP_TILES, GROUPS_PER_TILE
    jj = jnp.arange(CHUNK)
    diff = jj[None, :] - jj[:, None]
    kt = jnp.where((diff >= 0)[:, :, None, None, None], kern[jnp.clip(diff, 0, CHUNK - 1)], 0.0)
    kt = kt.reshape(CHUNK, CHUNK, gt, gp, SSM_GROUP, SSM_GROUP)
    t8 = (kt[:, :, :, :, :, None, :] * eye[None, None, None, :, None, :, None])
    t8 = t8.transpose(2, 0, 3, 4, 1, 5, 6).reshape(gt, CHUNK_WIDTH, CHUNK_WIDTH)
    w2 = jnp.stack([wr[::-1], wi[::-1]])
    w2 = w2.reshape(2, CHUNK, gt, gp, SSM_STATE, SSM_GROUP)
    win = w2[:, :, :, :, :, :, None] * eye[None, None, None, :, None, None, :]
    win = win.transpose(2, 1, 3, 5, 0, 6, 4).reshape(gt, CHUNK_WIDTH, 2 * STATE_TILE)
    cr, ci = _cmul(c_re[None], c_im[None], pw_r[1:, :, None, :], pw_i[1:, :, None, :])
    cs = jnp.stack([cr, -ci]).reshape(2, CHUNK, gt, gp, SSM_GROUP, SSM_STATE)
    wst = cs[:, :, :, :, :, :, None] * eye[None, None, None, :, None, None, :]
    wst = wst.transpose(2, 0, 3, 5, 1, 6, 4).reshape(gt, 2 * STATE_TILE, CHUNK_WIDTH)
    c0 = jnp.stack([c_re, -c_im]).reshape(2, gt, gp, SSM_GROUP, SSM_STATE)
    wc = c0[:, :, :, :, :, None] * eye[None, None, :, None, None, :]
    wc = wc.transpose(1, 0, 2, 4, 5, 3).reshape(gt, 2 * STATE_TILE, LANES)
    sr, si = [pw_r[CHUNK]], [pw_i[CHUNK]]
    for _ in range(SCAN_LEVELS - 1):
        nr, ni = _cmul(sr[-1], si[-1], sr[-1], si[-1])
        sr.append(nr)
        si.append(ni)
    a_r = jnp.stack(sr).reshape(SCAN_LEVELS, gt, STATE_TILE).transpose(1, 0, 2)
    a_i = jnp.stack(si).reshape(SCAN_LEVELS, gt, STATE_TILE).transpose(1, 0, 2)
    l_r = lr.reshape(gt, 1, STATE_TILE)
    l_i = li.reshape(gt, 1, STATE_TILE)
    return (t8.astype(BF16), win.astype(BF16), wst.astype(BF16), wc.astype(BF16),
            a_r, a_i, l_r, l_i)


def kernel(x_prompt, x_sample, cache_win_k, cache_win_v, cache_meta_k, cache_meta_v,
           state_ssm_re, state_ssm_im, meta_tokens, norm_gain, w_in, q_norm_gain, k_norm_gain,
           sinks, a_re, a_im, log_dt, b_re, b_im, c_re, c_im, d_skip, w_glu, b_glu,
           w_attn_out, w_ssm_out, w_out):
    depth = w_in.shape[0]
    assert depth == 1, "single-layer trunk"
    batch, seq = x_prompt.shape[:2]
    n_s = x_sample.shape[0]
    assert x_sample.shape[1] == 1 and seq % (CHUNK * BLOCK) == 0
    l = 0

    w = w_in[l]
    w_bf = jnp.concatenate([w[:, 0:1024], w[:, 1536:8704], w[:, 1024:1536]], axis=1).astype(BF16)
    qg = jnp.tile(q_norm_gain[l], TN // HEAD_DIM)[None]
    kg = jnp.tile(k_norm_gain[l], KV_WIDTH // HEAD_DIM)[None]
    lane = jnp.arange(TN)
    e = jnp.where((lane[:, None] // HEAD_DIM) == (lane[None, :] // HEAD_DIM),
                  1.0 / HEAD_DIM, 0.0).astype(BF16)
    gain = norm_gain[l][None]

    pos_p = N_META + jnp.arange(seq)
    pos_small = jnp.concatenate([jnp.full((n_s,), PAST_LEN), jnp.arange(N_META)])
    x_small = jnp.concatenate([x_sample[:, 0, :], meta_tokens.astype(x_prompt.dtype)], axis=0)
    xp = x_prompt.reshape(batch * seq, D_MODEL)
    h_p = _inproj(xp, gain, w_bf, qg, kg, e, *_rope_tables(pos_p), tm=1024)
    h_s = _inproj(x_small, gain, w_bf, qg, kg, e, *_rope_tables(pos_small), tm=n_s + N_META)

    t8, win, wst, wc, a_r, a_i, l_r, l_i = _ssm_tables(
        a_re[l], a_im[l], log_dt[l], b_re[l], b_im[l], c_re[l], c_im[l])
    dsk = d_skip[l][None]

    o_p = _prompt_attention(sinks[l], h_p, h_s, batch, seq)
    nc = seq // CHUNK
    um3 = h_s[n_s:, COL_U:COL_U + SSM_WIDTH].reshape(N_META // CHUNK, CHUNK, SSM_WIDTH)
    um3 = jnp.concatenate(
        [um3, jnp.zeros((CHUNK - N_META // CHUNK, CHUNK, SSM_WIDTH), F32)], axis=0)
    y_p, p_re, p_im = _prompt_ssm(h_p.reshape(batch, nc, CHUNK, IN_WIDTH), um3, t8, win, wst,
                                  a_r, a_i, dsk, batch, nc)
    w_glu_bf = w_glu[l].astype(BF16)
    w_a_bf = w_attn_out[l].astype(BF16)
    w_s_bf = w_ssm_out[l].astype(BF16)
    w_o_bf = w_out[l].astype(BF16)
    bglu = b_glu[l][None]
    y_prompt = _merge(o_p, h_p, y_p.reshape(batch * seq, SSM_WIDTH), xp, bglu,
                      w_glu_bf, w_a_bf, w_s_bf, w_o_bf, tm=256).reshape(batch, seq, D_MODEL)

    q_s = h_s[:n_s, COL_Q:COL_Q + ATTN_WIDTH].reshape(n_s, N_Q_HEADS, 1, HEAD_DIM)
    sel = (jnp.arange(N_Q_HEADS)[:, None] // (N_Q_HEADS // N_KV_HEADS)
           == jnp.arange(N_KV_HEADS)[None, :]).astype(F32)
    qp = (q_s * sel[None, :, :, None]).reshape(n_s, N_Q_HEADS, KV_WIDTH)
    o_s, s_win_k, s_win_v = _sample_attention(
        qp, h_s.reshape(n_s + N_META, 1, IN_WIDTH),
        cache_win_k[l].reshape(n_s, WINDOW, KV_WIDTH), cache_win_v[l].reshape(n_s, WINDOW, KV_WIDTH),
        cache_meta_k[l].reshape(n_s, N_META, KV_WIDTH), cache_meta_v[l].reshape(n_s, N_META, KV_WIDTH),
        sinks[l][:, None], bt=8)
    o_s = o_s[:, :, :HEAD_DIM].reshape(n_s, ATTN_WIDTH)
    y_s, s_re, s_im = _sample_ssm(
        h_s, state_ssm_re[l].reshape(n_s, -1), state_ssm_im[l].reshape(n_s, -1),
        win[:, (CHUNK - 1) * LANES:, :], wc, l_r, l_i, dsk, n_s)
    y_sample = _merge(o_s, h_s, y_s, x_sample[:, 0, :], bglu,
                      w_glu_bf, w_a_bf, w_s_bf, w_o_bf, tm=n_s).reshape(n_s, 1, D_MODEL)

    kv_p = h_p[:, COL_KV:].reshape(batch, seq, 2, N_KV_HEADS, HEAD_DIM)
    p_win_k = kv_p[:, seq - WINDOW:, 0][None]
    p_win_v = kv_p[:, seq - WINDOW:, 1][None]
    kv_m = h_s[n_s:, COL_KV:].reshape(N_META, 2, N_KV_HEADS, HEAD_DIM)
    p_meta_k = jnp.broadcast_to(kv_m[None, :, 0], (batch, N_META, N_KV_HEADS, HEAD_DIM))[None]
    p_meta_v = jnp.broadcast_to(kv_m[None, :, 1], (batch, N_META, N_KV_HEADS, HEAD_DIM))[None]
    p_ssm_re = p_re.reshape(batch, N_SSM_GROUPS, SSM_STATE)[None]
    p_ssm_im = p_im.reshape(batch, N_SSM_GROUPS, SSM_STATE)[None]
    kv_shape = (1, n_s, WINDOW, N_KV_HEADS, HEAD_DIM)
    st_shape = (1, n_s, N_SSM_GROUPS, SSM_STATE)
    return (y_prompt, y_sample, p_win_k, p_win_v, p_meta_k, p_meta_v, p_ssm_re, p_ssm_im,
            s_win_k.reshape(kv_shape), s_win_v.reshape(kv_shape),
            s_re.reshape(st_shape), s_im.reshape(st_shape))
```

```python
import functools

import jax
import jax.numpy as jnp
from jax import lax
from jax.experimental import pallas as pl
from jax.experimental.pallas import tpu as pltpu

F32 = jnp.float32
BF16 = jnp.bfloat16

D_MODEL = 2048
N_META = 16
HEAD_DIM = 64
N_Q_HEADS = 16
N_KV_HEADS = 4
ATTN_WIDTH = N_Q_HEADS * HEAD_DIM
KV_WIDTH = N_KV_HEADS * HEAD_DIM
WINDOW = 128
BLOCK = 128
ROPE_THETA = 10000.0
SSM_WIDTH = D_MODEL // 2
SSM_GROUP = 16
N_SSM_GROUPS = SSM_WIDTH // SSM_GROUP
SSM_STATE = 64
EPS = 1e-6
PAST_LEN = 8192

LANES = 128
CHUNK = 8
GROUPS_PER_TILE = LANES // SSM_GROUP
N_GROUP_TILES = N_SSM_GROUPS // GROUPS_PER_TILE
CHUNK_WIDTH = CHUNK * LANES
STATE_TILE = GROUPS_PER_TILE * SSM_STATE
SCAN_LEVELS = 8

COL_Q = 0
COL_ZA = 1024
COL_U = 2048
COL_ZS = 3072
COL_GA = 4096
COL_GS = 6144
COL_KV = 8192
IN_WIDTH = 8704
TN = 512
KV_TILE = COL_KV // TN
VMEM_LIMIT = 56 * 1024 * 1024
NEG = -1e30
NT_DIMS = (((1,), (1,)), ((), ()))


def _cmul(ar, ai, br, bi):
    return ar * br - ai * bi, ar * bi + ai * br


def _inproj_kernel(x_ref, gain_ref, w_ref, qg_ref, kg_ref, e_ref, cos_ref, sa_ref, sb_ref,
                   o_ref, xn_ref):
    j = pl.program_id(1)

    @pl.when(j == 0)
    def _():
        x = x_ref[...]
        r = lax.rsqrt(jnp.mean(x * x, axis=-1, keepdims=True) + EPS)
        xn_ref[...] = (x * r * gain_ref[...]).astype(BF16)

    acc = jnp.dot(xn_ref[...], w_ref[...].astype(BF16), preferred_element_type=F32)

    def head_norm_rope(a, gain, e):
        sq = a * a
        hi = sq.astype(BF16)
        lo = (sq - hi.astype(F32)).astype(BF16)
        ms = (jnp.dot(hi, e, preferred_element_type=F32)
              + jnp.dot(lo, e, preferred_element_type=F32))
        xn = a * lax.rsqrt(ms + EPS) * gain
        cos, sa, sb = cos_ref[...], sa_ref[...], sb_ref[...]
        outs = []
        for c in range(a.shape[1] // LANES):
            xc = xn[:, c * LANES:(c + 1) * LANES]
            outs.append(xc * cos + pltpu.roll(xc, LANES - HEAD_DIM // 2, 1) * sa
                        + pltpu.roll(xc, HEAD_DIM // 2, 1) * sb)
        return jnp.concatenate(outs, axis=1)

    @pl.when(j < COL_ZA // TN)
    def _():
        o_ref[...] = head_norm_rope(acc, qg_ref[...], e_ref[...]) * (HEAD_DIM ** -0.5)

    @pl.when(j == KV_TILE)
    def _():
        k = head_norm_rope(acc[:, :KV_WIDTH], kg_ref[...], e_ref[:KV_WIDTH, :KV_WIDTH])
        o_ref[...] = jnp.concatenate([k, acc[:, KV_WIDTH:]], axis=1)

    @pl.when(jnp.logical_and(j >= COL_ZA // TN, j != KV_TILE))
    def _():
        o_ref[...] = acc


def _w_tile(j):
    n_q = COL_ZA // TN
    return jnp.where(j < n_q, j, jnp.where(j == KV_TILE, n_q, j + 1))


def _inproj(x, gain, w, qg, kg, e, cos, sa, sb, tm):
    m = x.shape[0]
    n_tab = cos.shape[0] // tm
    return pl.pallas_call(
        _inproj_kernel,
        grid=(m // tm, IN_WIDTH // TN),
        in_specs=[
            pl.BlockSpec((tm, D_MODEL), lambda i, j: (i, 0)),
            pl.BlockSpec((1, D_MODEL), lambda i, j: (0, 0)),
            pl.BlockSpec((D_MODEL, TN), lambda i, j: (0, _w_tile(j))),
            pl.BlockSpec((1, TN), lambda i, j: (0, 0)),
            pl.BlockSpec((1, KV_WIDTH), lambda i, j: (0, 0)),
            pl.BlockSpec((TN, TN), lambda i, j: (0, 0)),
            pl.BlockSpec((tm, LANES), lambda i, j: (i % n_tab, 0)),
            pl.BlockSpec((tm, LANES), lambda i, j: (i % n_tab, 0)),
            pl.BlockSpec((tm, LANES), lambda i, j: (i % n_tab, 0)),
        ],
        out_specs=pl.BlockSpec((tm, TN), lambda i, j: (i, j)),
        out_shape=jax.ShapeDtypeStruct((m, IN_WIDTH), F32),
        scratch_shapes=[pltpu.VMEM((tm, D_MODEL), BF16)],
        compiler_params=pltpu.CompilerParams(
            dimension_semantics=("arbitrary", "arbitrary"), vmem_limit_bytes=VMEM_LIMIT),
        name="inproj",
    )(x, gain, w, qg, kg, e, cos, sa, sb)


def _dup_head(x, g, lo):
    r = pltpu.roll(x, HEAD_DIM, 1)
    return jnp.where(lo, x, r) if g % 2 == 0 else jnp.where(lo, r, x)


def _prompt_attn_kernel(sinks_ref, q_ref, kvc_ref, kvp_ref, kvm_ref, o_ref):
    i = pl.program_id(1)
    lo = lax.broadcasted_iota(jnp.int32, (1, LANES), 1) < HEAD_DIM
    rows = 4 * BLOCK
    qi = lax.broadcasted_iota(jnp.int32, (rows, BLOCK), 0) % BLOCK
    kj = lax.broadcasted_iota(jnp.int32, (rows, BLOCK), 1)
    mask_prev = jnp.logical_and(kj > qi, i > 0)
    mask_cur = kj <= qi
    head_row = lax.broadcasted_iota(jnp.int32, (rows, 1), 0) // BLOCK
    nt = (((1,), (1,)), ((), ()))

    for g in range(N_KV_HEADS):
        c = g // 2
        ks = slice(c * LANES, (c + 1) * LANES)
        vs = slice(KV_WIDTH + c * LANES, KV_WIDTH + (c + 1) * LANES)
        k_m = _dup_head(kvm_ref[:, ks], g, lo).astype(BF16)
        k_p = _dup_head(kvp_ref[:, ks], g, lo).astype(BF16)
        k_c = _dup_head(kvc_ref[:, ks], g, lo).astype(BF16)
        v_m = _dup_head(kvm_ref[:, vs], g, lo).astype(BF16)
        v_p = _dup_head(kvp_ref[:, vs], g, lo).astype(BF16)
        v_c = _dup_head(kvc_ref[:, vs], g, lo).astype(BF16)
        qa = q_ref[:, (2 * g) * LANES:(2 * g + 1) * LANES]
        qb = q_ref[:, (2 * g + 1) * LANES:(2 * g + 2) * LANES]
        q4 = jnp.concatenate([jnp.where(lo, qa, 0.0), jnp.where(lo, 0.0, qa),
                              jnp.where(lo, qb, 0.0), jnp.where(lo, 0.0, qb)],
                             axis=0).astype(BF16)
        s_m = lax.dot_general(q4, k_m, nt, preferred_element_type=F32)
        s_p = lax.dot_general(q4, k_p, nt, preferred_element_type=F32)
        s_c = lax.dot_general(q4, k_c, nt, preferred_element_type=F32)
        s_p = jnp.where(mask_prev, s_p, NEG)
        s_c = jnp.where(mask_cur, s_c, NEG)
        sink = jnp.where(head_row == 0, sinks_ref[4 * g],
                         jnp.where(head_row == 1, sinks_ref[4 * g + 1],
                                   jnp.where(head_row == 2, sinks_ref[4 * g + 2],
                                             sinks_ref[4 * g + 3])))
        m = jnp.maximum(jnp.maximum(jnp.max(s_m, axis=-1, keepdims=True),
                                    jnp.max(s_p, axis=-1, keepdims=True)),
                        jnp.maximum(jnp.max(s_c, axis=-1, keepdims=True), sink))
        p_m = jnp.exp(s_m - m)
        p_p = jnp.exp(s_p - m)
        p_c = jnp.exp(s_c - m)
        den = (jnp.sum(p_m, axis=-1, keepdims=True) + jnp.sum(p_p, axis=-1, keepdims=True)
               + jnp.sum(p_c, axis=-1, keepdims=True) + jnp.exp(sink - m))
        inv = 1.0 / den
        o = (jnp.dot((p_m * inv).astype(BF16), v_m, preferred_element_type=F32)
             + jnp.dot((p_p * inv).astype(BF16), v_p, preferred_element_type=F32)
             + jnp.dot((p_c * inv).astype(BF16), v_c, preferred_element_type=F32))
        o_ref[:, (2 * g) * LANES:(2 * g + 1) * LANES] = jnp.where(
            lo, o[0:BLOCK], o[BLOCK:2 * BLOCK])
        o_ref[:, (2 * g + 1) * LANES:(2 * g + 2) * LANES] = jnp.where(
            lo, o[2 * BLOCK:3 * BLOCK], o[3 * BLOCK:4 * BLOCK])


def _prompt_attention(sinks, h_p, h_s, batch, seq):
    nb = seq // BLOCK
    kvb = COL_KV // (2 * KV_WIDTH)
    return pl.pallas_call(
        _prompt_attn_kernel,
        grid=(batch, nb),
        in_specs=[
            pl.BlockSpec(memory_space=pltpu.SMEM),
            pl.BlockSpec((BLOCK, ATTN_WIDTH), lambda b, i: (b * nb + i, 0)),
            pl.BlockSpec((BLOCK, 2 * KV_WIDTH), lambda b, i: (b * nb + i, kvb)),
            pl.BlockSpec((BLOCK, 2 * KV_WIDTH),
                         lambda b, i: (b * nb + jnp.maximum(i - 1, 0), kvb)),
            pl.BlockSpec((N_META, 2 * KV_WIDTH), lambda b, i: (h_s.shape[0] // N_META - 1, kvb)),
        ],
        out_specs=pl.BlockSpec((BLOCK, ATTN_WIDTH), lambda b, i: (b * nb + i, 0)),
        out_shape=jax.ShapeDtypeStruct((batch * seq, ATTN_WIDTH), F32),
        compiler_params=pltpu.CompilerParams(
            dimension_semantics=("arbitrary", "arbitrary"), vmem_limit_bytes=VMEM_LIMIT),
        name="prompt_attn",
    )(sinks, h_p, h_p, h_p, h_s)


def _sample_attn_kernel(qp_ref, kn_ref, vn_ref, ck_ref, cv_ref, mk_ref, mv_ref, sink_ref,
                        o_ref, ok_ref, ov_ref):
    bt = ck_ref.shape[0]
    last = lax.broadcasted_iota(jnp.int32, (1, WINDOW), 1) == WINDOW - 1
    kn = kn_ref[0]
    vn = vn_ref[0]
    for b in range(bt):
        ok_ref[b] = jnp.where(last, kn[:, b:b + 1], pltpu.roll(ck_ref[b], WINDOW - 1, 1))
        ov_ref[b] = jnp.where(last, vn[:, b:b + 1], pltpu.roll(cv_ref[b], WINDOW - 1, 1))
    qp = qp_ref[...].astype(BF16)
    k_w = ok_ref[...].astype(BF16)
    v_w = ov_ref[...].astype(BF16)
    k_m = mk_ref[...].astype(BF16)
    v_m = mv_ref[...].astype(BF16)
    s_w = jnp.einsum("bhc,bcw->bhw", qp, k_w, preferred_element_type=F32)
    s_m = jnp.einsum("bhc,bkc->bhk", qp, k_m, preferred_element_type=F32)
    sink = sink_ref[...][None]
    m = jnp.maximum(jnp.maximum(jnp.max(s_w, axis=-1, keepdims=True),
                                jnp.max(s_m, axis=-1, keepdims=True)), sink)
    p_w = jnp.exp(s_w - m)
    p_m = jnp.exp(s_m - m)
    den = (jnp.sum(p_w, axis=-1, keepdims=True) + jnp.sum(p_m, axis=-1, keepdims=True)
           + jnp.exp(sink - m))
    inv = 1.0 / den
    o = (jnp.einsum("bhw,bcw->bhc", (p_w * inv).astype(BF16), v_w, preferred_element_type=F32)
         + jnp.einsum("bhk,bkc->bhc", (p_m * inv).astype(BF16), v_m,
                      preferred_element_type=F32))
    hh = lax.broadcasted_iota(jnp.int32, (1, N_Q_HEADS, KV_WIDTH), 1) // (N_Q_HEADS // N_KV_HEADS)
    cc = lax.broadcasted_iota(jnp.int32, (1, N_Q_HEADS, KV_WIDTH), 2) // HEAD_DIM
    o = jnp.where(hh == cc, o, 0.0)
    o = o[:, :, :LANES] + o[:, :, LANES:]
    o_ref[...] = o + pltpu.roll(o, HEAD_DIM, 2)


def _sample_attention(qp, kn_t, vn_t, cache_k, cache_v, meta_k, meta_v, sink_col, bt):
    n = cache_k.shape[0]
    win_spec = pl.BlockSpec((bt, KV_WIDTH, WINDOW), lambda t: (t, 0, 0))
    new_spec = pl.BlockSpec((1, KV_WIDTH, bt), lambda t: (t, 0, 0))
    meta_spec = pl.BlockSpec((bt, N_META, KV_WIDTH), lambda t: (t, 0, 0))
    return pl.pallas_call(
        _sample_attn_kernel,
        grid=(n // bt,),
        in_specs=[
            pl.BlockSpec((bt, N_Q_HEADS, KV_WIDTH), lambda t: (t, 0, 0)),
            new_spec, new_spec, win_spec, win_spec, meta_spec, meta_spec,
            pl.BlockSpec((N_Q_HEADS, 1), lambda t: (0, 0)),
        ],
        out_specs=[
            pl.BlockSpec((bt, N_Q_HEADS, LANES), lambda t: (t, 0, 0)),
            win_spec, win_spec,
        ],
        out_shape=[
            jax.ShapeDtypeStruct((n, N_Q_HEADS, LANES), F32),
            jax.ShapeDtypeStruct(cache_k.shape, F32),
            jax.ShapeDtypeStruct(cache_v.shape, F32),
        ],
        compiler_params=pltpu.CompilerParams(
            dimension_semantics=("arbitrary",), vmem_limit_bytes=VMEM_LIMIT),
        name="sample_attn",
    )(qp, kn_t, vn_t, cache_k, cache_v, meta_k, meta_v, sink_col)


def _gelu_tanh(x):
    return 0.5 * x * (1.0 + jnp.tanh(0.7978845608028654 * (x + 0.044715 * (x * x * x))))


def _prompt_ssm_kernel(u_ref, um_ref, t8_ref, win_ref, wst_ref, ar_ref, ai_ref, d_ref,
                       y_ref, sre_ref, sim_ref):
    nc = u_ref.shape[1]
    ucat = jnp.concatenate([u_ref[0, :, j, :] for j in range(CHUNK)], axis=1)
    ub = ucat.astype(BF16)
    win = win_ref[0]
    g = jnp.dot(ub, win, preferred_element_type=F32)
    gr, gi = g[:, :STATE_TILE], g[:, STATE_TILE:]

    um = jnp.concatenate([um_ref[:, j, :] for j in range(CHUNK)], axis=1)
    gm = jnp.dot(um.astype(BF16), win, preferred_element_type=F32)
    a_r, a_i = ar_ref[0, 0:1, :], ai_ref[0, 0:1, :]
    m0r, m0i = gm[0:1, :STATE_TILE], gm[0:1, STATE_TILE:]
    s0r = a_r * m0r - a_i * m0i + gm[1:2, :STATE_TILE]
    s0i = a_r * m0i + a_i * m0r + gm[1:2, STATE_TILE:]

    row = lax.broadcasted_iota(jnp.int32, (nc, 1), 0)
    first = row == 0
    gr = gr + jnp.where(first, a_r * s0r - a_i * s0i, 0.0)
    gi = gi + jnp.where(first, a_r * s0i + a_i * s0r, 0.0)
    for k in range(SCAN_LEVELS):
        d = 1 << k
        if d >= nc:
            break
        k_r, k_i = ar_ref[0, k:k + 1, :], ai_ref[0, k:k + 1, :]
        keep = row >= d
        sh_r = jnp.where(keep, pltpu.roll(gr, d, 0), 0.0)
        sh_i = jnp.where(keep, pltpu.roll(gi, d, 0), 0.0)
        gr, gi = gr + (k_r * sh_r - k_i * sh_i), gi + (k_r * sh_i + k_i * sh_r)

    sre_ref[0, 0] = gr[nc - 1:nc, :]
    sim_ref[0, 0] = gi[nc - 1:nc, :]
    pr = jnp.where(first, s0r, pltpu.roll(gr, 1, 0))
    pi = jnp.where(first, s0i, pltpu.roll(gi, 1, 0))
    sp = jnp.concatenate([pr, pi], axis=1).astype(BF16)
    y2 = (jnp.dot(ub, t8_ref[0], preferred_element_type=F32)
          + lax.dot_general(sp, wst_ref[0], NT_DIMS, preferred_element_type=F32))
    dsk = d_ref[...]
    for j in range(CHUNK):
        sl = slice(j * LANES, (j + 1) * LANES)
        y_ref[0, :, j, :] = _gelu_tanh(y2[:, sl] + dsk * ucat[:, sl])


def _prompt_ssm(h_p4, um3, t8, win, wst, a_r, a_i, d_skip, batch, nc):
    ub = COL_U // LANES
    wspec = pl.BlockSpec((1, CHUNK_WIDTH, CHUNK_WIDTH), lambda gt, b: (gt, 0, 0))
    aspec = pl.BlockSpec((1, SCAN_LEVELS, STATE_TILE), lambda gt, b: (gt, 0, 0))
    sspec = pl.BlockSpec((1, 1, 1, STATE_TILE), lambda gt, b: (b, gt, 0, 0))
    return pl.pallas_call(
        _prompt_ssm_kernel,
        grid=(N_GROUP_TILES, batch),
        in_specs=[
            pl.BlockSpec((1, nc, CHUNK, LANES), lambda gt, b: (b, 0, 0, ub + gt)),
            pl.BlockSpec((CHUNK, CHUNK, LANES), lambda gt, b: (0, 0, gt)),
            wspec, wspec, wspec, aspec, aspec,
            pl.BlockSpec((1, LANES), lambda gt, b: (0, gt)),
        ],
        out_specs=[
            pl.BlockSpec((1, nc, CHUNK, LANES), lambda gt, b: (b, 0, 0, gt)),
            sspec, sspec,
        ],
        out_shape=[
            jax.ShapeDtypeStruct((batch, nc, CHUNK, SSM_WIDTH), F32),
            jax.ShapeDtypeStruct((batch, N_GROUP_TILES, 1, STATE_TILE), F32),
            jax.ShapeDtypeStruct((batch, N_GROUP_TILES, 1, STATE_TILE), F32),
        ],
        compiler_params=pltpu.CompilerParams(
            dimension_semantics=("arbitrary", "arbitrary"), vmem_limit_bytes=VMEM_LIMIT),
        name="prompt_ssm",
    )(h_p4, um3, t8, win, wst, a_r, a_i, d_skip)


def _sample_ssm_kernel(u_ref, hr_ref, hi_ref, b_ref, wc_ref, lr_ref, li_ref, d_ref,
                       y_ref, sre_ref, sim_ref):
    u = u_ref[...]
    bu = jnp.dot(u.astype(BF16), b_ref[0], preferred_element_type=F32)
    l_r, l_i = lr_ref[0], li_ref[0]
    h_r, h_i = hr_ref[...], hi_ref[...]
    n_r = l_r * h_r - l_i * h_i + bu[:, :STATE_TILE]
    n_i = l_r * h_i + l_i * h_r + bu[:, STATE_TILE:]
    sre_ref[...] = n_r
    sim_ref[...] = n_i
    hn = jnp.concatenate([n_r, n_i], axis=1).astype(BF16)
    y = lax.dot_general(hn, wc_ref[0], NT_DIMS, preferred_element_type=F32) + d_ref[...] * u
    y_ref[...] = _gelu_tanh(y)


def _sample_ssm(h_s, h0r, h0i, win, wc, l_r, l_i, d_skip, n):
    ub = COL_U // LANES
    st = pl.BlockSpec((n, STATE_TILE), lambda gt: (0, gt))
    lam = pl.BlockSpec((1, 1, STATE_TILE), lambda gt: (gt, 0, 0))
    return pl.pallas_call(
        _sample_ssm_kernel,
        grid=(N_GROUP_TILES,),
        in_specs=[
            pl.BlockSpec((n, LANES), lambda gt: (0, ub + gt)),
            st, st,
            pl.BlockSpec((1, LANES, 2 * STATE_TILE), lambda gt: (gt, CHUNK - 1, 0)),
            pl.BlockSpec((1, LANES, 2 * STATE_TILE), lambda gt: (gt, 0, 0)),
            lam, lam,
            pl.BlockSpec((1, LANES), lambda gt: (0, gt)),
        ],
        out_specs=[pl.BlockSpec((n, LANES), lambda gt: (0, gt)), st, st],
        out_shape=[
            jax.ShapeDtypeStruct((n, SSM_WIDTH), F32),
            jax.ShapeDtypeStruct((n, N_SSM_GROUPS * SSM_STATE), F32),
            jax.ShapeDtypeStruct((n, N_SSM_GROUPS * SSM_STATE), F32),
        ],
        compiler_params=pltpu.CompilerParams(
            dimension_semantics=("arbitrary",), vmem_limit_bytes=VMEM_LIMIT),
        name="sample_ssm",
    )(h_s, h0r, h0i, win, wc, l_r, l_i, d_skip)


def _merge_kernel(o_ref, za_ref, zs_ref, ga_ref, gs_ref, y_ref, x_ref, bglu_ref,
                  wglu_ref, wa_ref, ws_ref, wo_ref, out_ref):
    y = y_ref[...]
    t = jnp.dot(y.astype(BF16), wglu_ref[...], preferred_element_type=F32) + bglu_ref[...]
    y = y * jax.nn.sigmoid(t)
    za = za_ref[...]
    zs = zs_ref[...]
    a_in = (o_ref[...] * (za * jax.nn.sigmoid(za))).astype(BF16)
    s_in = (y * (zs * jax.nn.sigmoid(zs))).astype(BF16)
    br_a = jnp.dot(a_in, wa_ref[...], preferred_element_type=F32)
    br_s = jnp.dot(s_in, ws_ref[...], preferred_element_type=F32)
    mix = (jax.nn.sigmoid(ga_ref[...]) * br_a + jax.nn.sigmoid(gs_ref[...]) * br_s).astype(BF16)
    out_ref[...] = x_ref[...] + jnp.dot(mix, wo_ref[...], preferred_element_type=F32)


def _merge(o, h, y, x, b_glu, w_glu, w_a, w_s, w_o, tm):
    m = x.shape[0]
    const = lambda shape: pl.BlockSpec(shape, lambda i: (0, 0), pipeline_mode=pl.Buffered(1))
    return pl.pallas_call(
        _merge_kernel,
        grid=(m // tm,),
        in_specs=[
            pl.BlockSpec((tm, ATTN_WIDTH), lambda i: (i, 0)),
            pl.BlockSpec((tm, ATTN_WIDTH), lambda i: (i, COL_ZA // ATTN_WIDTH)),
            pl.BlockSpec((tm, SSM_WIDTH), lambda i: (i, COL_ZS // SSM_WIDTH)),
            pl.BlockSpec((tm, D_MODEL), lambda i: (i, COL_GA // D_MODEL)),
            pl.BlockSpec((tm, D_MODEL), lambda i: (i, COL_GS // D_MODEL)),
            pl.BlockSpec((tm, SSM_WIDTH), lambda i: (i, 0)),
            pl.BlockSpec((tm, D_MODEL), lambda i: (i, 0)),
            const((1, SSM_WIDTH)),
            const((SSM_WIDTH, SSM_WIDTH)),
            const((ATTN_WIDTH, D_MODEL)),
            const((SSM_WIDTH, D_MODEL)),
            const((D_MODEL, D_MODEL)),
        ],
        out_specs=pl.BlockSpec((tm, D_MODEL), lambda i: (i, 0)),
        out_shape=jax.ShapeDtypeStruct((m, D_MODEL), F32),
        compiler_params=pltpu.CompilerParams(
            dimension_semantics=("arbitrary",), vmem_limit_bytes=VMEM_LIMIT),
        name="merge",
    )(o, h, h, h, h, y, x, b_glu, w_glu, w_a, w_s, w_o)


def _rope_tables(pos):
    half = HEAD_DIM // 2
    inv_freq = ROPE_THETA ** (-jnp.arange(half, dtype=F32) / half)
    ang = pos.astype(F32)[:, None] * inv_freq[None, :]
    cos, sin = jnp.cos(ang), jnp.sin(ang)
    zero = jnp.zeros_like(sin)
    reps = LANES // HEAD_DIM
    cos_t = jnp.tile(jnp.concatenate([cos, cos], axis=1), (1, reps))
    sin_a = jnp.tile(jnp.concatenate([-sin, zero], axis=1), (1, reps))
    sin_b = jnp.tile(jnp.concatenate([zero, sin], axis=1), (1, reps))
    return cos_t, sin_a, sin_b


def _split_bf16(x):
    hi = x.astype(BF16)
    return hi, (x - hi.astype(F32)).astype(BF16)


def _dot_nt_f32(a, b):
    ah, al = _split_bf16(a)
    bh, bl = _split_bf16(b)
    dot = lambda x, y: lax.dot_general(x, y, NT_DIMS, preferred_element_type=F32)
    return dot(ah, bh) + dot(ah, bl) + dot(al, bh)


def _ssm_tables_kernel(are_ref, aim_ref, ldt_ref, bre_ref, bim_ref, cre_ref, cim_ref,
                       win_ref, wst_ref, t8_ref, wc_ref, ar_ref, ai_ref, lr_ref, li_ref):
    a_r, a_i = are_ref[0], aim_ref[0]
    dt = jnp.exp(ldt_ref[0])
    mag = jnp.exp(a_r * dt)
    l_r, l_i = mag * jnp.cos(a_i * dt), mag * jnp.sin(a_i * dt)
    den = a_r * a_r + a_i * a_i
    f_r = ((l_r - 1.0) * a_r + l_i * a_i) / den
    f_i = (l_i * a_r - (l_r - 1.0) * a_i) / den
    shape = (LANES, STATE_TILE)
    own = (lax.broadcasted_iota(jnp.int32, shape, 0) // SSM_GROUP
           == lax.broadcasted_iota(jnp.int32, shape, 1) // SSM_STATE)

    def expand(ref):
        x = ref[0]
        return jnp.where(own, jnp.concatenate([x] * (STATE_TILE // LANES), axis=1), 0.0)

    b_r, b_i = expand(bre_ref), expand(bim_ref)
    c_r, c_i = expand(cre_ref), expand(cim_ref)
    bb_r, bb_i = _cmul(f_r, f_i, b_r, b_i)
    wc = jnp.concatenate([c_r, -c_i], axis=1)
    wc_ref[0] = wc.astype(BF16)
    p_r, p_i = jnp.ones_like(l_r), jnp.zeros_like(l_r)
    kern = []
    for n in range(CHUNK):
        x_r, x_i = _cmul(p_r, p_i, bb_r, bb_i)
        x = jnp.concatenate([x_r, x_i], axis=1)
        win_ref[0, (CHUNK - 1 - n) * LANES:(CHUNK - n) * LANES, :] = x.astype(BF16)
        kern.append(_dot_nt_f32(x, wc).astype(BF16))
        p_r, p_i = _cmul(p_r, p_i, l_r, l_i)
        s_r, s_i = _cmul(c_r, c_i, p_r, p_i)
        wst_ref[0, n * LANES:(n + 1) * LANES, :] = jnp.concatenate([s_r, -s_i],
                                                                   axis=1).astype(BF16)
    zero = jnp.zeros((LANES, LANES), BF16)
    for j in range(CHUNK):
        for jp in range(CHUNK):
            t8_ref[0, j * LANES:(j + 1) * LANES, jp * LANES:(jp + 1) * LANES] = (
                kern[jp - j] if jp >= j else zero)
    for k in range(SCAN_LEVELS):
        ar_ref[0, k:k + 1, :] = p_r
        ai_ref[0, k:k + 1, :] = p_i
        p_r, p_i = _cmul(p_r, p_i, p_r, p_i)
    lr_ref[0] = l_r
    li_ref[0] = l_i


def _ssm_tables(a_re, a_im, log_dt, b_re, b_im, c_re, c_im):
    gt = N_GROUP_TILES
    lane_vec = lambda v: v.reshape(gt, 1, STATE_TILE)
    dup = lambda v: jnp.concatenate([v, v], axis=-1)
    rows = GROUPS_PER_TILE * SSM_GROUP
    b_t = lambda v: dup(v.transpose(0, 2, 1).reshape(gt, rows, SSM_STATE))
    c_t = lambda v: dup(v.reshape(gt, rows, SSM_STATE))
    vec = pl.BlockSpec((1, 1, STATE_TILE), lambda g: (g, 0, 0))
    par = pl.BlockSpec((1, rows, LANES), lambda g: (g, 0, 0))
    big = pl.BlockSpec((1, CHUNK_WIDTH, CHUNK_WIDTH), lambda g: (g, 0, 0))
    wcs = pl.BlockSpec((1, LANES, 2 * STATE_TILE), lambda g: (g, 0, 0))
    scan = pl.BlockSpec((1, SCAN_LEVELS, STATE_TILE), lambda g: (g, 0, 0))
    return pl.pallas_call(
        _ssm_tables_kernel,
        grid=(gt,),
        in_specs=[vec, vec, vec, par, par, par, par],
        out_specs=[big, big, big, wcs, scan, scan, vec, vec],
        out_shape=[
            jax.ShapeDtypeStruct((gt, CHUNK_WIDTH, 2 * STATE_TILE), BF16),
            jax.ShapeDtypeStruct((gt, CHUNK_WIDTH, 2 * STATE_TILE), BF16),
            jax.ShapeDtypeStruct((gt, CHUNK_WIDTH, CHUNK_WIDTH), BF16),
            jax.ShapeDtypeStruct((gt, LANES, 2 * STATE_TILE), BF16),
            jax.ShapeDtypeStruct((gt, SCAN_LEVELS, STATE_TILE), F32),
            jax.ShapeDtypeStruct((gt, SCAN_LEVELS, STATE_TILE), F32),
            jax.ShapeDtypeStruct((gt, 1, STATE_TILE), F32),
            jax.ShapeDtypeStruct((gt, 1, STATE_TILE), F32),
        ],
        compiler_params=pltpu.CompilerParams(
            dimension_semantics=("arbitrary",), vmem_limit_bytes=VMEM_LIMIT),
        name="ssm_tables",
    )(lane_vec(a_re), lane_vec(a_im), lane_vec(jnp.repeat(log_dt, SSM_STATE)),
      b_t(b_re), b_t(b_im), c_t(c_re), c_t(c_im))


def kernel(x_prompt, x_sample, cache_win_k, cache_win_v, cache_meta_k, cache_meta_v,
           state_ssm_re, state_ssm_im, meta_tokens, norm_gain, w_in, q_norm_gain, k_norm_gain,
           sinks, a_re, a_im, log_dt, b_re, b_im, c_re, c_im, d_skip, w_glu, b_glu,
           w_attn_out, w_ssm_out, w_out):
    depth = w_in.shape[0]
    assert depth == 1, "single-layer trunk"
    batch, seq = x_prompt.shape[:2]
    n_s = x_sample.shape[0]
    assert x_sample.shape[1] == 1 and seq % (CHUNK * BLOCK) == 0
    l = 0

    w = w_in[l]
    qg = jnp.tile(q_norm_gain[l], TN // HEAD_DIM)[None]
    kg = jnp.tile(k_norm_gain[l], KV_WIDTH // HEAD_DIM)[None]
    lane = jnp.arange(TN)
    e = jnp.where((lane[:, None] // HEAD_DIM) == (lane[None, :] // HEAD_DIM),
                  1.0 / HEAD_DIM, 0.0).astype(BF16)
    gain = norm_gain[l][None]

    pos_p = N_META + jnp.arange(seq)
    pos_small = jnp.concatenate([jnp.full((n_s,), PAST_LEN), jnp.arange(N_META)])
    x_small = jnp.concatenate([x_sample[:, 0, :], meta_tokens.astype(x_prompt.dtype)], axis=0)
    xp = x_prompt.reshape(batch * seq, D_MODEL)
    h_p = _inproj(xp, gain, w, qg, kg, e, *_rope_tables(pos_p), tm=1024)
    h_s = _inproj(x_small, gain, w, qg, kg, e, *_rope_tables(pos_small), tm=n_s + N_META)

    win, wst, t8, wc, a_r, a_i, l_r, l_i = _ssm_tables(
        a_re[l], a_im[l], log_dt[l], b_re[l], b_im[l], c_re[l], c_im[l])
    dsk = d_skip[l][None]

    o_p = _prompt_attention(sinks[l], h_p, h_s, batch, seq)
    nc = seq // CHUNK
    um3 = h_s[n_s:, COL_U:COL_U + SSM_WIDTH].reshape(N_META // CHUNK, CHUNK, SSM_WIDTH)
    um3 = jnp.concatenate(
        [um3, jnp.zeros((CHUNK - N_META // CHUNK, CHUNK, SSM_WIDTH), F32)], axis=0)
    y_p, p_re, p_im = _prompt_ssm(h_p.reshape(batch, nc, CHUNK, IN_WIDTH), um3, t8, win, wst,
                                  a_r, a_i, dsk, batch, nc)
    w_glu_bf = w_glu[l].astype(BF16)
    w_a_bf = w_attn_out[l].astype(BF16)
    w_s_bf = w_ssm_out[l].astype(BF16)
    w_o_bf = w_out[l].astype(BF16)
    bglu = b_glu[l][None]
    y_prompt = _merge(o_p, h_p, y_p.reshape(batch * seq, SSM_WIDTH), xp, bglu,
                      w_glu_bf, w_a_bf, w_s_bf, w_o_bf, tm=256).reshape(batch, seq, D_MODEL)

    q_s = h_s[:n_s, COL_Q:COL_Q + ATTN_WIDTH].reshape(n_s, N_Q_HEADS, 1, HEAD_DIM)
    sel = (jnp.arange(N_Q_HEADS)[:, None] // (N_Q_HEADS // N_KV_HEADS)
           == jnp.arange(N_KV_HEADS)[None, :]).astype(F32)
    qp = (q_s * sel[None, :, :, None]).reshape(n_s, N_Q_HEADS, KV_WIDTH)
    bt = 8
    to_t = lambda c: c.transpose(0, 2, 3, 1).reshape(n_s, KV_WIDTH, WINDOW)
    from_t = lambda c: c.reshape(n_s, N_KV_HEADS, HEAD_DIM, WINDOW).transpose(0, 3, 1, 2)[None]
    new_t = lambda c0: h_s[:n_s, c0:c0 + KV_WIDTH].reshape(n_s // bt, bt, KV_WIDTH).transpose(0, 2, 1)
    o_s, s_win_k, s_win_v = _sample_attention(
        qp, new_t(COL_KV), new_t(COL_KV + KV_WIDTH), to_t(cache_win_k[l]), to_t(cache_win_v[l]),
        cache_meta_k[l].reshape(n_s, N_META, KV_WIDTH), cache_meta_v[l].reshape(n_s, N_META, KV_WIDTH),
        sinks[l][:, None], bt=bt)
    o_s = o_s[:, :, :HEAD_DIM].reshape(n_s, ATTN_WIDTH)
    y_s, s_re, s_im = _sample_ssm(
        h_s, state_ssm_re[l].reshape(n_s, -1), state_ssm_im[l].reshape(n_s, -1),
        win, wc, l_r, l_i, dsk, n_s)
    y_sample = _merge(o_s, h_s, y_s, x_sample[:, 0, :], bglu,
                      w_glu_bf, w_a_bf, w_s_bf, w_o_bf, tm=n_s).reshape(n_s, 1, D_MODEL)

    kv_p = h_p.reshape(batch, seq, IN_WIDTH)[:, seq - WINDOW:, COL_KV:]
    kv_p = kv_p.reshape(batch, WINDOW, 2, N_KV_HEADS, HEAD_DIM)
    p_win_k = kv_p[:, :, 0][None]
    p_win_v = kv_p[:, :, 1][None]
    kv_m = h_s[n_s:, COL_KV:].reshape(N_META, 2, N_KV_HEADS, HEAD_DIM)
    p_meta_k = jnp.broadcast_to(kv_m[None, :, 0], (batch, N_META, N_KV_HEADS, HEAD_DIM))[None]
    p_meta_v = jnp.broadcast_to(kv_m[None, :, 1], (batch, N_META, N_KV_HEADS, HEAD_DIM))[None]
    p_ssm_re = p_re.reshape(batch, N_SSM_GROUPS, SSM_STATE)[None]
    p_ssm_im = p_im.reshape(batch, N_SSM_GROUPS, SSM_STATE)[None]
    st_shape = (1, n_s, N_SSM_GROUPS, SSM_STATE)
    return (y_prompt, y_sample, p_win_k, p_win_v, p_meta_k, p_meta_v, p_ssm_re, p_ssm_im,
            from_t(s_win_k), from_t(s_win_v), s_re.reshape(st_shape), s_im.reshape(st_shape))
```

```python
import functools

import jax
import jax.numpy as jnp
from jax import lax
from jax.experimental import pallas as pl
from jax.experimental.pallas import tpu as pltpu

F32 = jnp.float32
BF16 = jnp.bfloat16

D_MODEL = 2048
N_META = 16
HEAD_DIM = 64
N_Q_HEADS = 16
N_KV_HEADS = 4
ATTN_WIDTH = N_Q_HEADS * HEAD_DIM
KV_WIDTH = N_KV_HEADS * HEAD_DIM
WINDOW = 128
BLOCK = 128
ROPE_THETA = 10000.0
SSM_WIDTH = D_MODEL // 2
SSM_GROUP = 16
N_SSM_GROUPS = SSM_WIDTH // SSM_GROUP
SSM_STATE = 64
EPS = 1e-6
PAST_LEN = 8192

LANES = 128
CHUNK = 8
GROUPS_PER_TILE = LANES // SSM_GROUP
N_GROUP_TILES = N_SSM_GROUPS // GROUPS_PER_TILE
CHUNK_WIDTH = CHUNK * LANES
STATE_TILE = GROUPS_PER_TILE * SSM_STATE
SCAN_LEVELS = 8

COL_Q = 0
COL_ZA = 1024
COL_U = 2048
COL_ZS = 3072
COL_GA = 4096
COL_GS = 6144
COL_KV = 8192
IN_WIDTH = 8704
TN = 512
KV_TILE = COL_KV // TN
VMEM_LIMIT = 56 * 1024 * 1024
NEG = -1e30
NT_DIMS = (((1,), (1,)), ((), ()))


def _cmul(ar, ai, br, bi):
    return ar * br - ai * bi, ar * bi + ai * br


def _inproj_kernel(x_ref, gain_ref, w_ref, qg_ref, kg_ref, e_ref, cos_ref, sa_ref, sb_ref,
                   o_ref, xn_ref):
    j = pl.program_id(1)

    @pl.when(j == 0)
    def _():
        x = x_ref[...]
        r = lax.rsqrt(jnp.mean(x * x, axis=-1, keepdims=True) + EPS)
        xn_ref[...] = (x * r * gain_ref[...]).astype(BF16)

    acc = jnp.dot(xn_ref[...], w_ref[...].astype(BF16), preferred_element_type=F32)

    def head_norm_rope(a, gain, e):
        sq = a * a
        hi = sq.astype(BF16)
        lo = (sq - hi.astype(F32)).astype(BF16)
        ms = (jnp.dot(hi, e, preferred_element_type=F32)
              + jnp.dot(lo, e, preferred_element_type=F32))
        xn = a * lax.rsqrt(ms + EPS) * gain
        cos, sa, sb = cos_ref[...], sa_ref[...], sb_ref[...]
        outs = []
        for c in range(a.shape[1] // LANES):
            xc = xn[:, c * LANES:(c + 1) * LANES]
            outs.append(xc * cos + pltpu.roll(xc, LANES - HEAD_DIM // 2, 1) * sa
                        + pltpu.roll(xc, HEAD_DIM // 2, 1) * sb)
        return jnp.concatenate(outs, axis=1)

    @pl.when(j < COL_ZA // TN)
    def _():
        o_ref[...] = head_norm_rope(acc, qg_ref[...], e_ref[...]) * (HEAD_DIM ** -0.5)

    @pl.when(j == KV_TILE)
    def _():
        k = head_norm_rope(acc[:, :KV_WIDTH], kg_ref[...], e_ref[:KV_WIDTH, :KV_WIDTH])
        o_ref[...] = jnp.concatenate([k, acc[:, KV_WIDTH:]], axis=1)

    @pl.when(jnp.logical_and(j >= COL_ZA // TN, j != KV_TILE))
    def _():
        o_ref[...] = acc


def _w_tile(j):
    n_q = COL_ZA // TN
    return jnp.where(j < n_q, j, jnp.where(j == KV_TILE, n_q, j + 1))


def _inproj(x, gain, w, qg, kg, e, cos, sa, sb, tm):
    m = x.shape[0]
    n_tab = cos.shape[0] // tm
    return pl.pallas_call(
        _inproj_kernel,
        grid=(m // tm, IN_WIDTH // TN),
        in_specs=[
            pl.BlockSpec((tm, D_MODEL), lambda i, j: (i, 0)),
            pl.BlockSpec((1, D_MODEL), lambda i, j: (0, 0)),
            pl.BlockSpec((D_MODEL, TN), lambda i, j: (0, _w_tile(j))),
            pl.BlockSpec((1, TN), lambda i, j: (0, 0)),
            pl.BlockSpec((1, KV_WIDTH), lambda i, j: (0, 0)),
            pl.BlockSpec((TN, TN), lambda i, j: (0, 0)),
            pl.BlockSpec((tm, LANES), lambda i, j: (i % n_tab, 0)),
            pl.BlockSpec((tm, LANES), lambda i, j: (i % n_tab, 0)),
            pl.BlockSpec((tm, LANES), lambda i, j: (i % n_tab, 0)),
        ],
        out_specs=pl.BlockSpec((tm, TN), lambda i, j: (i, j)),
        out_shape=jax.ShapeDtypeStruct((m, IN_WIDTH), F32),
        scratch_shapes=[pltpu.VMEM((tm, D_MODEL), BF16)],
        compiler_params=pltpu.CompilerParams(
            dimension_semantics=("arbitrary", "arbitrary"), vmem_limit_bytes=VMEM_LIMIT),
        name="inproj",
    )(x, gain, w, qg, kg, e, cos, sa, sb)


def _dup_head(x, g, lo):
    r = pltpu.roll(x, HEAD_DIM, 1)
    return jnp.where(lo, x, r) if g % 2 == 0 else jnp.where(lo, r, x)


def _prompt_attn_kernel(sinks_ref, q_ref, kvc_ref, kvp_ref, kvm_ref, vmt_ref, bias_ref, o_ref):
    lo = lax.broadcasted_iota(jnp.int32, (1, LANES), 1) < HEAD_DIM
    head_lane = lax.broadcasted_iota(jnp.int32, (1, 4 * BLOCK), 1) // BLOCK
    bias = bias_ref[0]
    v_t = jnp.concatenate([kvp_ref[:, KV_WIDTH:].T, kvc_ref[:, KV_WIDTH:].T], axis=1)
    for g in range(N_KV_HEADS):
        ks = slice((g // 2) * LANES, (g // 2 + 1) * LANES)
        k_all = jnp.concatenate([_dup_head(kvp_ref[:, ks], g, lo), _dup_head(kvc_ref[:, ks], g, lo),
                                 _dup_head(kvm_ref[:, ks], g, lo)], axis=0).astype(BF16)
        qa = q_ref[:, (2 * g) * LANES:(2 * g + 1) * LANES]
        qb = q_ref[:, (2 * g + 1) * LANES:(2 * g + 2) * LANES]
        q4 = jnp.concatenate([jnp.where(lo, qa, 0.0), jnp.where(lo, 0.0, qa),
                              jnp.where(lo, qb, 0.0), jnp.where(lo, 0.0, qb)],
                             axis=0).astype(BF16)
        s = lax.dot_general(k_all, q4, NT_DIMS, preferred_element_type=F32)
        s_w = s[:2 * BLOCK] + bias
        s_m = s[2 * BLOCK:]
        sink = jnp.where(head_lane == 0, sinks_ref[4 * g],
                         jnp.where(head_lane == 1, sinks_ref[4 * g + 1],
                                   jnp.where(head_lane == 2, sinks_ref[4 * g + 2],
                                             sinks_ref[4 * g + 3])))
        m = jnp.maximum(jnp.maximum(jnp.max(s_w, axis=0, keepdims=True),
                                    jnp.max(s_m, axis=0, keepdims=True)), sink)
        p_w = jnp.exp(s_w - m)
        p_m = jnp.exp(s_m - m)
        den = (jnp.sum(p_w, axis=0, keepdims=True) + jnp.sum(p_m, axis=0, keepdims=True)
               + jnp.exp(sink - m))
        inv = 1.0 / den
        p = jnp.concatenate([p_w * inv, p_m * inv], axis=0).astype(BF16)
        hs = slice(g * HEAD_DIM, (g + 1) * HEAD_DIM)
        vt_g = jnp.concatenate([v_t[hs], vmt_ref[hs, :]], axis=1).astype(BF16)
        o_t = jnp.dot(vt_g, p, preferred_element_type=F32)
        for pair in range(2):
            two = jnp.concatenate([o_t[:, (2 * pair) * BLOCK:(2 * pair + 1) * BLOCK],
                                   o_t[:, (2 * pair + 1) * BLOCK:(2 * pair + 2) * BLOCK]], axis=0)
            o_ref[:, (2 * g + pair) * LANES:(2 * g + pair + 1) * LANES] = two.T


def _prompt_attention(sinks, h_p, h_s, vm_t, batch, seq):
    nb = seq // BLOCK
    kvb = COL_KV // (2 * KV_WIDTH)
    kj = jnp.arange(2 * BLOCK)[:, None]
    qi = jnp.arange(4 * BLOCK)[None, :] % BLOCK
    cur = (kj >= BLOCK) & (kj - BLOCK <= qi)
    prev = (kj < BLOCK) & (kj > qi)
    bias = jnp.where(jnp.stack([cur, cur | prev]), 0.0, NEG).astype(F32)
    return pl.pallas_call(
        _prompt_attn_kernel,
        grid=(batch, nb),
        in_specs=[
            pl.BlockSpec(memory_space=pltpu.SMEM),
            pl.BlockSpec((BLOCK, ATTN_WIDTH), lambda b, i: (b * nb + i, 0)),
            pl.BlockSpec((BLOCK, 2 * KV_WIDTH), lambda b, i: (b * nb + i, kvb)),
            pl.BlockSpec((BLOCK, 2 * KV_WIDTH),
                         lambda b, i: (b * nb + jnp.maximum(i - 1, 0), kvb)),
            pl.BlockSpec((N_META, 2 * KV_WIDTH), lambda b, i: (h_s.shape[0] // N_META - 1, kvb)),
            pl.BlockSpec((KV_WIDTH, N_META), lambda b, i: (0, 0)),
            pl.BlockSpec((1, 2 * BLOCK, 4 * BLOCK), lambda b, i: (jnp.minimum(i, 1), 0, 0)),
        ],
        out_specs=pl.BlockSpec((BLOCK, ATTN_WIDTH), lambda b, i: (b * nb + i, 0)),
        out_shape=jax.ShapeDtypeStruct((batch * seq, ATTN_WIDTH), F32),
        compiler_params=pltpu.CompilerParams(
            dimension_semantics=("arbitrary", "arbitrary"), vmem_limit_bytes=VMEM_LIMIT),
        name="prompt_attn",
    )(sinks, h_p, h_p, h_p, h_s, vm_t, bias)


def _sample_attn_kernel(qp_ref, kn_ref, vn_ref, ck_ref, cv_ref, mk_ref, mv_ref, sink_ref,
                        o_ref, ok_ref, ov_ref):
    bt = ck_ref.shape[0]
    last = lax.broadcasted_iota(jnp.int32, (1, WINDOW), 1) == WINDOW - 1
    kn = kn_ref[0]
    vn = vn_ref[0]
    for b in range(bt):
        ok_ref[b] = jnp.where(last, kn[:, b:b + 1], pltpu.roll(ck_ref[b], WINDOW - 1, 1))
        ov_ref[b] = jnp.where(last, vn[:, b:b + 1], pltpu.roll(cv_ref[b], WINDOW - 1, 1))
    qp = qp_ref[...].astype(BF16)
    k_w = ok_ref[...].astype(BF16)
    v_w = ov_ref[...].astype(BF16)
    k_m = mk_ref[...].astype(BF16)
    v_m = mv_ref[...].astype(BF16)
    s_w = jnp.einsum("bhc,bcw->bhw", qp, k_w, preferred_element_type=F32)
    s_m = jnp.einsum("bhc,bkc->bhk", qp, k_m, preferred_element_type=F32)
    sink = sink_ref[...][None]
    m = jnp.maximum(jnp.maximum(jnp.max(s_w, axis=-1, keepdims=True),
                                jnp.max(s_m, axis=-1, keepdims=True)), sink)
    p_w = jnp.exp(s_w - m)
    p_m = jnp.exp(s_m - m)
    den = (jnp.sum(p_w, axis=-1, keepdims=True) + jnp.sum(p_m, axis=-1, keepdims=True)
           + jnp.exp(sink - m))
    inv = 1.0 / den
    o = (jnp.einsum("bhw,bcw->bhc", (p_w * inv).astype(BF16), v_w, preferred_element_type=F32)
         + jnp.einsum("bhk,bkc->bhc", (p_m * inv).astype(BF16), v_m,
                      preferred_element_type=F32))
    hh = lax.broadcasted_iota(jnp.int32, (1, N_Q_HEADS, KV_WIDTH), 1) // (N_Q_HEADS // N_KV_HEADS)
    cc = lax.broadcasted_iota(jnp.int32, (1, N_Q_HEADS, KV_WIDTH), 2) // HEAD_DIM
    o = jnp.where(hh == cc, o, 0.0)
    o = o[:, :, :LANES] + o[:, :, LANES:]
    o_ref[...] = o + pltpu.roll(o, HEAD_DIM, 2)


def _sample_attention(qp, kn_t, vn_t, cache_k, cache_v, meta_k, meta_v, sink_col, bt):
    n = cache_k.shape[0]
    win_spec = pl.BlockSpec((bt, KV_WIDTH, WINDOW), lambda t: (t, 0, 0))
    new_spec = pl.BlockSpec((1, KV_WIDTH, bt), lambda t: (t, 0, 0))
    meta_spec = pl.BlockSpec((bt, N_META, KV_WIDTH), lambda t: (t, 0, 0))
    return pl.pallas_call(
        _sample_attn_kernel,
        grid=(n // bt,),
        in_specs=[
            pl.BlockSpec((bt, N_Q_HEADS, KV_WIDTH), lambda t: (t, 0, 0)),
            new_spec, new_spec, win_spec, win_spec, meta_spec, meta_spec,
            pl.BlockSpec((N_Q_HEADS, 1), lambda t: (0, 0)),
        ],
        out_specs=[
            pl.BlockSpec((bt, N_Q_HEADS, LANES), lambda t: (t, 0, 0)),
            win_spec, win_spec,
        ],
        out_shape=[
            jax.ShapeDtypeStruct((n, N_Q_HEADS, LANES), F32),
            jax.ShapeDtypeStruct(cache_k.shape, F32),
            jax.ShapeDtypeStruct(cache_v.shape, F32),
        ],
        compiler_params=pltpu.CompilerParams(
            dimension_semantics=("arbitrary",), vmem_limit_bytes=VMEM_LIMIT),
        name="sample_attn",
    )(qp, kn_t, vn_t, cache_k, cache_v, meta_k, meta_v, sink_col)


def _gelu_tanh(x):
    return 0.5 * x * (1.0 + jnp.tanh(0.7978845608028654 * (x + 0.044715 * (x * x * x))))


def _prompt_ssm_kernel(u_ref, um_ref, t8_ref, win_ref, wst_ref, ar_ref, ai_ref, d_ref,
                       y_ref, sre_ref, sim_ref):
    batch = sre_ref.shape[0]
    n_rows = u_ref.shape[0] // CHUNK
    nc = n_rows // batch
    ucat = jnp.concatenate([u_ref[pl.ds(j, n_rows, stride=CHUNK), :] for j in range(CHUNK)],
                           axis=1)
    ub = ucat.astype(BF16)
    win = win_ref[0]
    g = jnp.dot(ub, win, preferred_element_type=F32)

    n_mc = um_ref.shape[0] // CHUNK
    um = jnp.concatenate([um_ref[pl.ds(j, n_mc, stride=CHUNK), :] for j in range(CHUNK)], axis=1)
    gm = jnp.dot(um.astype(BF16), win, preferred_element_type=F32)
    a_r, a_i = ar_ref[0, 0:1, :], ai_ref[0, 0:1, :]
    s0r, s0i = gm[0:1, :STATE_TILE], gm[0:1, STATE_TILE:]
    for c in range(1, n_mc):
        s0r, s0i = (a_r * s0r - a_i * s0i + gm[c:c + 1, :STATE_TILE],
                    a_r * s0i + a_i * s0r + gm[c:c + 1, STATE_TILE:])

    row = lax.broadcasted_iota(jnp.int32, (nc, 1), 0)
    first = row == 0
    prev_r, prev_i = [], []
    for b in range(batch):
        gr = g[b * nc:(b + 1) * nc, :STATE_TILE]
        gi = g[b * nc:(b + 1) * nc, STATE_TILE:]
        gr = gr + jnp.where(first, a_r * s0r - a_i * s0i, 0.0)
        gi = gi + jnp.where(first, a_r * s0i + a_i * s0r, 0.0)
        for k in range(SCAN_LEVELS):
            d = 1 << k
            if d >= nc:
                break
            k_r, k_i = ar_ref[0, k:k + 1, :], ai_ref[0, k:k + 1, :]
            if d < CHUNK:
                keep = row >= d
                sh_r = jnp.where(keep, pltpu.roll(gr, d, 0), 0.0)
                sh_i = jnp.where(keep, pltpu.roll(gi, d, 0), 0.0)
            else:
                zero = jnp.zeros((d, STATE_TILE), F32)
                sh_r = jnp.concatenate([zero, gr[:nc - d]], axis=0)
                sh_i = jnp.concatenate([zero, gi[:nc - d]], axis=0)
            gr, gi = gr + (k_r * sh_r - k_i * sh_i), gi + (k_r * sh_i + k_i * sh_r)
        sre_ref[b, 0] = gr[nc - 1:nc, :]
        sim_ref[b, 0] = gi[nc - 1:nc, :]
        prev_r.append(jnp.where(first, s0r, pltpu.roll(gr, 1, 0)))
        prev_i.append(jnp.where(first, s0i, pltpu.roll(gi, 1, 0)))
    sp = jnp.concatenate([jnp.concatenate(prev_r, axis=0), jnp.concatenate(prev_i, axis=0)],
                         axis=1).astype(BF16)
    y2 = (jnp.dot(ub, t8_ref[0], preferred_element_type=F32)
          + lax.dot_general(sp, wst_ref[0], NT_DIMS, preferred_element_type=F32))
    dsk = d_ref[...]
    for j in range(CHUNK):
        sl = slice(j * LANES, (j + 1) * LANES)
        y_ref[pl.ds(j, n_rows, stride=CHUNK), :] = _gelu_tanh(y2[:, sl] + dsk * ucat[:, sl])


def _prompt_ssm(h_p, h_s, t8, win, wst, a_r, a_i, d_skip, batch):
    ub = COL_U // LANES
    rows = h_p.shape[0]
    wspec = pl.BlockSpec((1, CHUNK_WIDTH, CHUNK_WIDTH), lambda gt: (gt, 0, 0))
    aspec = pl.BlockSpec((1, SCAN_LEVELS, STATE_TILE), lambda gt: (gt, 0, 0))
    sspec = pl.BlockSpec((batch, 1, 1, STATE_TILE), lambda gt: (0, gt, 0, 0))
    return pl.pallas_call(
        _prompt_ssm_kernel,
        grid=(N_GROUP_TILES,),
        in_specs=[
            pl.BlockSpec((rows, LANES), lambda gt: (0, ub + gt)),
            pl.BlockSpec((N_META, LANES), lambda gt: (h_s.shape[0] // N_META - 1, ub + gt)),
            wspec, wspec, wspec, aspec, aspec,
            pl.BlockSpec((1, LANES), lambda gt: (0, gt)),
        ],
        out_specs=[pl.BlockSpec((rows, LANES), lambda gt: (0, gt)), sspec, sspec],
        out_shape=[
            jax.ShapeDtypeStruct((rows, SSM_WIDTH), F32),
            jax.ShapeDtypeStruct((batch, N_GROUP_TILES, 1, STATE_TILE), F32),
            jax.ShapeDtypeStruct((batch, N_GROUP_TILES, 1, STATE_TILE), F32),
        ],
        compiler_params=pltpu.CompilerParams(
            dimension_semantics=("arbitrary",), vmem_limit_bytes=VMEM_LIMIT),
        name="prompt_ssm",
    )(h_p, h_s, t8, win, wst, a_r, a_i, d_skip)


def _sample_ssm_kernel(u_ref, hr_ref, hi_ref, b_ref, wc_ref, lr_ref, li_ref, d_ref,
                       y_ref, sre_ref, sim_ref):
    u = u_ref[...]
    bu = jnp.dot(u.astype(BF16), b_ref[0], preferred_element_type=F32)
    l_r, l_i = lr_ref[0], li_ref[0]
    h_r, h_i = hr_ref[...], hi_ref[...]
    n_r = l_r * h_r - l_i * h_i + bu[:, :STATE_TILE]
    n_i = l_r * h_i + l_i * h_r + bu[:, STATE_TILE:]
    sre_ref[...] = n_r
    sim_ref[...] = n_i
    hn = jnp.concatenate([n_r, n_i], axis=1).astype(BF16)
    y = lax.dot_general(hn, wc_ref[0], NT_DIMS, preferred_element_type=F32) + d_ref[...] * u
    y_ref[...] = _gelu_tanh(y)


def _sample_ssm(h_s, h0r, h0i, win, wc, l_r, l_i, d_skip, n):
    ub = COL_U // LANES
    st = pl.BlockSpec((n, STATE_TILE), lambda gt: (0, gt))
    lam = pl.BlockSpec((1, 1, STATE_TILE), lambda gt: (gt, 0, 0))
    return pl.pallas_call(
        _sample_ssm_kernel,
        grid=(N_GROUP_TILES,),
        in_specs=[
            pl.BlockSpec((n, LANES), lambda gt: (0, ub + gt)),
            st, st,
            pl.BlockSpec((1, LANES, 2 * STATE_TILE), lambda gt: (gt, CHUNK - 1, 0)),
            pl.BlockSpec((1, LANES, 2 * STATE_TILE), lambda gt: (gt, 0, 0)),
            lam, lam,
            pl.BlockSpec((1, LANES), lambda gt: (0, gt)),
        ],
        out_specs=[pl.BlockSpec((n, LANES), lambda gt: (0, gt)), st, st],
        out_shape=[
            jax.ShapeDtypeStruct((n, SSM_WIDTH), F32),
            jax.ShapeDtypeStruct((n, N_SSM_GROUPS * SSM_STATE), F32),
            jax.ShapeDtypeStruct((n, N_SSM_GROUPS * SSM_STATE), F32),
        ],
        compiler_params=pltpu.CompilerParams(
            dimension_semantics=("arbitrary",), vmem_limit_bytes=VMEM_LIMIT),
        name="sample_ssm",
    )(h_s, h0r, h0i, win, wc, l_r, l_i, d_skip)


def _merge_kernel(o_ref, za_ref, zs_ref, ga_ref, gs_ref, y_ref, x_ref, bglu_ref,
                  wglu_ref, wa_ref, ws_ref, wo_ref, out_ref):
    y = y_ref[...]
    t = jnp.dot(y.astype(BF16), wglu_ref[...], preferred_element_type=F32) + bglu_ref[...]
    y = y * jax.nn.sigmoid(t)
    za = za_ref[...]
    zs = zs_ref[...]
    a_in = (o_ref[...] * (za * jax.nn.sigmoid(za))).astype(BF16)
    s_in = (y * (zs * jax.nn.sigmoid(zs))).astype(BF16)
    br_a = jnp.dot(a_in, wa_ref[...], preferred_element_type=F32)
    br_s = jnp.dot(s_in, ws_ref[...], preferred_element_type=F32)
    mix = (jax.nn.sigmoid(ga_ref[...]) * br_a + jax.nn.sigmoid(gs_ref[...]) * br_s).astype(BF16)
    out_ref[...] = x_ref[...] + jnp.dot(mix, wo_ref[...], preferred_element_type=F32)


def _merge(o, h, y, x, b_glu, w_glu, w_a, w_s, w_o, tm):
    m = x.shape[0]
    const = lambda shape: pl.BlockSpec(shape, lambda i: (0, 0), pipeline_mode=pl.Buffered(1))
    return pl.pallas_call(
        _merge_kernel,
        grid=(m // tm,),
        in_specs=[
            pl.BlockSpec((tm, ATTN_WIDTH), lambda i: (i, 0)),
            pl.BlockSpec((tm, ATTN_WIDTH), lambda i: (i, COL_ZA // ATTN_WIDTH)),
            pl.BlockSpec((tm, SSM_WIDTH), lambda i: (i, COL_ZS // SSM_WIDTH)),
            pl.BlockSpec((tm, D_MODEL), lambda i: (i, COL_GA // D_MODEL)),
            pl.BlockSpec((tm, D_MODEL), lambda i: (i, COL_GS // D_MODEL)),
            pl.BlockSpec((tm, SSM_WIDTH), lambda i: (i, 0)),
            pl.BlockSpec((tm, D_MODEL), lambda i: (i, 0)),
            const((1, SSM_WIDTH)),
            const((SSM_WIDTH, SSM_WIDTH)),
            const((ATTN_WIDTH, D_MODEL)),
            const((SSM_WIDTH, D_MODEL)),
            const((D_MODEL, D_MODEL)),
        ],
        out_specs=pl.BlockSpec((tm, D_MODEL), lambda i: (i, 0)),
        out_shape=jax.ShapeDtypeStruct((m, D_MODEL), F32),
        compiler_params=pltpu.CompilerParams(
            dimension_semantics=("arbitrary",), vmem_limit_bytes=VMEM_LIMIT),
        name="merge",
    )(o, h, h, h, h, y, x, b_glu, w_glu, w_a, w_s, w_o)


def _rope_tables(pos):
    half = HEAD_DIM // 2
    inv_freq = ROPE_THETA ** (-jnp.arange(half, dtype=F32) / half)
    ang = pos.astype(F32)[:, None] * inv_freq[None, :]
    cos, sin = jnp.cos(ang), jnp.sin(ang)
    zero = jnp.zeros_like(sin)
    reps = LANES // HEAD_DIM
    cos_t = jnp.tile(jnp.concatenate([cos, cos], axis=1), (1, reps))
    sin_a = jnp.tile(jnp.concatenate([-sin, zero], axis=1), (1, reps))
    sin_b = jnp.tile(jnp.concatenate([zero, sin], axis=1), (1, reps))
    return cos_t, sin_a, sin_b


def _split_bf16(x):
    hi = x.astype(BF16)
    return hi, (x - hi.astype(F32)).astype(BF16)


def _dot_nt_f32(a, b):
    ah, al = _split_bf16(a)
    bh, bl = _split_bf16(b)
    dot = lambda x, y: lax.dot_general(x, y, NT_DIMS, preferred_element_type=F32)
    return dot(ah, bh) + dot(ah, bl) + dot(al, bh)


def _ssm_tables_kernel(are_ref, aim_ref, ldt_ref, bre_ref, bim_ref, cre_ref, cim_ref,
                       win_ref, wst_ref, t8_ref, wc_ref, ar_ref, ai_ref, lr_ref, li_ref):
    a_r, a_i = are_ref[0], aim_ref[0]
    dt = jnp.exp(ldt_ref[0])
    mag = jnp.exp(a_r * dt)
    l_r, l_i = mag * jnp.cos(a_i * dt), mag * jnp.sin(a_i * dt)
    den = a_r * a_r + a_i * a_i
    f_r = ((l_r - 1.0) * a_r + l_i * a_i) / den
    f_i = (l_i * a_r - (l_r - 1.0) * a_i) / den
    shape = (LANES, STATE_TILE)
    own = (lax.broadcasted_iota(jnp.int32, shape, 0) // SSM_GROUP
           == lax.broadcasted_iota(jnp.int32, shape, 1) // SSM_STATE)

    def expand(ref):
        x = ref[0]
        return jnp.where(own, jnp.concatenate([x] * (STATE_TILE // LANES), axis=1), 0.0)

    b_r, b_i = expand(bre_ref), expand(bim_ref)
    c_r, c_i = expand(cre_ref), expand(cim_ref)
    bb_r, bb_i = _cmul(f_r, f_i, b_r, b_i)
    wc = jnp.concatenate([c_r, -c_i], axis=1)
    wc_ref[0] = wc.astype(BF16)
    p_r, p_i = jnp.ones_like(l_r), jnp.zeros_like(l_r)
    kern = []
    for n in range(CHUNK):
        x_r, x_i = _cmul(p_r, p_i, bb_r, bb_i)
        x = jnp.concatenate([x_r, x_i], axis=1)
        win_ref[0, (CHUNK - 1 - n) * LANES:(CHUNK - n) * LANES, :] = x.astype(BF16)
        kern.append(_dot_nt_f32(x, wc).astype(BF16))
        p_r, p_i = _cmul(p_r, p_i, l_r, l_i)
        s_r, s_i = _cmul(c_r, c_i, p_r, p_i)
        wst_ref[0, n * LANES:(n + 1) * LANES, :] = jnp.concatenate([s_r, -s_i],
                                                                   axis=1).astype(BF16)
    zero = jnp.zeros((LANES, LANES), BF16)
    for j in range(CHUNK):
        for jp in range(CHUNK):
            t8_ref[0, j * LANES:(j + 1) * LANES, jp * LANES:(jp + 1) * LANES] = (
                kern[jp - j] if jp >= j else zero)
    for k in range(SCAN_LEVELS):
        ar_ref[0, k:k + 1, :] = p_r
        ai_ref[0, k:k + 1, :] = p_i
        p_r, p_i = _cmul(p_r, p_i, p_r, p_i)
    lr_ref[0] = l_r
    li_ref[0] = l_i


def _ssm_tables(a_re, a_im, log_dt, b_re, b_im, c_re, c_im):
    gt = N_GROUP_TILES
    lane_vec = lambda v: v.reshape(gt, 1, STATE_TILE)
    dup = lambda v: jnp.concatenate([v, v], axis=-1)
    rows = GROUPS_PER_TILE * SSM_GROUP
    b_t = lambda v: dup(v.transpose(0, 2, 1).reshape(gt, rows, SSM_STATE))
    c_t = lambda v: dup(v.reshape(gt, rows, SSM_STATE))
    vec = pl.BlockSpec((1, 1, STATE_TILE), lambda g: (g, 0, 0))
    par = pl.BlockSpec((1, rows, LANES), lambda g: (g, 0, 0))
    big = pl.BlockSpec((1, CHUNK_WIDTH, CHUNK_WIDTH), lambda g: (g, 0, 0))
    wcs = pl.BlockSpec((1, LANES, 2 * STATE_TILE), lambda g: (g, 0, 0))
    scan = pl.BlockSpec((1, SCAN_LEVELS, STATE_TILE), lambda g: (g, 0, 0))
    return pl.pallas_call(
        _ssm_tables_kernel,
        grid=(gt,),
        in_specs=[vec, vec, vec, par, par, par, par],
        out_specs=[big, big, big, wcs, scan, scan, vec, vec],
        out_shape=[
            jax.ShapeDtypeStruct((gt, CHUNK_WIDTH, 2 * STATE_TILE), BF16),
            jax.ShapeDtypeStruct((gt, CHUNK_WIDTH, 2 * STATE_TILE), BF16),
            jax.ShapeDtypeStruct((gt, CHUNK_WIDTH, CHUNK_WIDTH), BF16),
            jax.ShapeDtypeStruct((gt, LANES, 2 * STATE_TILE), BF16),
            jax.ShapeDtypeStruct((gt, SCAN_LEVELS, STATE_TILE), F32),
            jax.ShapeDtypeStruct((gt, SCAN_LEVELS, STATE_TILE), F32),
            jax.ShapeDtypeStruct((gt, 1, STATE_TILE), F32),
            jax.ShapeDtypeStruct((gt, 1, STATE_TILE), F32),
        ],
        compiler_params=pltpu.CompilerParams(
            dimension_semantics=("arbitrary",), vmem_limit_bytes=VMEM_LIMIT),
        name="ssm_tables",
    )(lane_vec(a_re), lane_vec(a_im), lane_vec(jnp.repeat(log_dt, SSM_STATE)),
      b_t(b_re), b_t(b_im), c_t(c_re), c_t(c_im))


def kernel(x_prompt, x_sample, cache_win_k, cache_win_v, cache_meta_k, cache_meta_v,
           state_ssm_re, state_ssm_im, meta_tokens, norm_gain, w_in, q_norm_gain, k_norm_gain,
           sinks, a_re, a_im, log_dt, b_re, b_im, c_re, c_im, d_skip, w_glu, b_glu,
           w_attn_out, w_ssm_out, w_out):
    depth = w_in.shape[0]
    assert depth == 1, "single-layer trunk"
    batch, seq = x_prompt.shape[:2]
    n_s = x_sample.shape[0]
    assert x_sample.shape[1] == 1 and seq % (CHUNK * BLOCK) == 0
    l = 0

    w = w_in[l]
    qg = jnp.tile(q_norm_gain[l], TN // HEAD_DIM)[None]
    kg = jnp.tile(k_norm_gain[l], KV_WIDTH // HEAD_DIM)[None]
    lane = jnp.arange(TN)
    e = jnp.where((lane[:, None] // HEAD_DIM) == (lane[None, :] // HEAD_DIM),
                  1.0 / HEAD_DIM, 0.0).astype(BF16)
    gain = norm_gain[l][None]

    pos_p = N_META + jnp.arange(seq)
    pos_small = jnp.concatenate([jnp.full((n_s,), PAST_LEN), jnp.arange(N_META)])
    x_small = jnp.concatenate([x_sample[:, 0, :], meta_tokens.astype(x_prompt.dtype)], axis=0)
    xp = x_prompt.reshape(batch * seq, D_MODEL)
    h_p = _inproj(xp, gain, w, qg, kg, e, *_rope_tables(pos_p), tm=1024)
    h_s = _inproj(x_small, gain, w, qg, kg, e, *_rope_tables(pos_small), tm=n_s + N_META)

    win, wst, t8, wc, a_r, a_i, l_r, l_i = _ssm_tables(
        a_re[l], a_im[l], log_dt[l], b_re[l], b_im[l], c_re[l], c_im[l])
    dsk = d_skip[l][None]

    vm_t = h_s[n_s:, COL_KV + KV_WIDTH:].T
    o_p = _prompt_attention(sinks[l], h_p, h_s, vm_t, batch, seq)
    y_p, p_re, p_im = _prompt_ssm(h_p, h_s, t8, win, wst, a_r, a_i, dsk, batch)
    w_glu_bf = w_glu[l].astype(BF16)
    w_a_bf = w_attn_out[l].astype(BF16)
    w_s_bf = w_ssm_out[l].astype(BF16)
    w_o_bf = w_out[l].astype(BF16)
    bglu = b_glu[l][None]
    y_prompt = _merge(o_p, h_p, y_p, xp, bglu,
                      w_glu_bf, w_a_bf, w_s_bf, w_o_bf, tm=256).reshape(batch, seq, D_MODEL)

    q_s = h_s[:n_s, COL_Q:COL_Q + ATTN_WIDTH].reshape(n_s, N_Q_HEADS, 1, HEAD_DIM)
    sel = (jnp.arange(N_Q_HEADS)[:, None] // (N_Q_HEADS // N_KV_HEADS)
           == jnp.arange(N_KV_HEADS)[None, :]).astype(F32)
    qp = (q_s * sel[None, :, :, None]).reshape(n_s, N_Q_HEADS, KV_WIDTH)
    bt = 8
    to_t = lambda c: c.transpose(0, 2, 3, 1).reshape(n_s, KV_WIDTH, WINDOW)
    from_t = lambda c: c.reshape(n_s, N_KV_HEADS, HEAD_DIM, WINDOW).transpose(0, 3, 1, 2)[None]
    new_t = lambda c0: h_s[:n_s, c0:c0 + KV_WIDTH].reshape(n_s // bt, bt, KV_WIDTH).transpose(0, 2, 1)
    o_s, s_win_k, s_win_v = _sample_attention(
        qp, new_t(COL_KV), new_t(COL_KV + KV_WIDTH), to_t(cache_win_k[l]), to_t(cache_win_v[l]),
        cache_meta_k[l].reshape(n_s, N_META, KV_WIDTH), cache_meta_v[l].reshape(n_s, N_META, KV_WIDTH),
        sinks[l][:, None], bt=bt)
    o_s = o_s[:, :, :HEAD_DIM].reshape(n_s, ATTN_WIDTH)
    y_s, s_re, s_im = _sample_ssm(
        h_s, state_ssm_re[l].reshape(n_s, -1), state_ssm_im[l].reshape(n_s, -1),
        win, wc, l_r, l_i, dsk, n_s)
    y_sample = _merge(o_s, h_s, y_s, x_sample[:, 0, :], bglu,
                      w_glu_bf, w_a_bf, w_s_bf, w_o_bf, tm=n_s).reshape(n_s, 1, D_MODEL)

    kv_p = h_p.reshape(batch, seq, IN_WIDTH)[:, seq - WINDOW:, COL_KV:]
    kv_p = kv_p.reshape(batch, WINDOW, 2, N_KV_HEADS, HEAD_DIM)
    p_win_k = kv_p[:, :, 0][None]
    p_win_v = kv_p[:, :, 1][None]
    kv_m = h_s[n_s:, COL_KV:].reshape(N_META, 2, N_KV_HEADS, HEAD_DIM)
    p_meta_k = jnp.broadcast_to(kv_m[None, :, 0], (batch, N_META, N_KV_HEADS, HEAD_DIM))[None]
    p_meta_v = jnp.broadcast_to(kv_m[None, :, 1], (batch, N_META, N_KV_HEADS, HEAD_DIM))[None]
    p_ssm_re = p_re.reshape(batch, N_SSM_GROUPS, SSM_STATE)[None]
    p_ssm_im = p_im.reshape(batch, N_SSM_GROUPS, SSM_STATE)[None]
    st_shape = (1, n_s, N_SSM_GROUPS, SSM_STATE)
    return (y_prompt, y_sample, p_win_k, p_win_v, p_meta_k, p_meta_v, p_ssm_re, p_ssm_im,
            from_t(s_win_k), from_t(s_win_v), s_re.reshape(st_shape), s_im.reshape(st_shape))
```

```python
import functools

import jax
import jax.numpy as jnp
import numpy as np
from jax import lax
from jax.experimental import pallas as pl
from jax.experimental.pallas import tpu as pltpu

F32 = jnp.float32
BF16 = jnp.bfloat16

D_MODEL = 2048
N_META = 16
HEAD_DIM = 64
N_Q_HEADS = 16
N_KV_HEADS = 4
ATTN_WIDTH = N_Q_HEADS * HEAD_DIM
KV_WIDTH = N_KV_HEADS * HEAD_DIM
WINDOW = 128
BLOCK = 128
ROPE_THETA = 10000.0
SSM_WIDTH = D_MODEL // 2
SSM_GROUP = 16
N_SSM_GROUPS = SSM_WIDTH // SSM_GROUP
SSM_STATE = 64
EPS = 1e-6
PAST_LEN = 8192

LANES = 128
CHUNK = 8
GROUPS_PER_TILE = LANES // SSM_GROUP
N_GROUP_TILES = N_SSM_GROUPS // GROUPS_PER_TILE
CHUNK_WIDTH = CHUNK * LANES
STATE_TILE = GROUPS_PER_TILE * SSM_STATE
SCAN_LEVELS = 8

COL_Q = 0
COL_ZA = 1024
COL_U = 2048
COL_ZS = 3072
COL_GA = 4096
COL_GS = 6144
COL_KV = 8192
IN_WIDTH = 8704
TN = 512
KV_TILE = COL_KV // TN
VMEM_LIMIT = 56 * 1024 * 1024
NEG = -1e30
NT_DIMS = (((1,), (1,)), ((), ()))


def _cmul(ar, ai, br, bi):
    return ar * br - ai * bi, ar * bi + ai * br


def _rmsnorm_kernel(x_ref, gain_ref, o_ref):
    x = x_ref[...]
    r = lax.rsqrt(jnp.mean(x * x, axis=-1, keepdims=True) + EPS)
    o_ref[...] = (x * r * gain_ref[...]).astype(BF16)


def _rmsnorm(x, gain, tm):
    m = x.shape[0]
    return pl.pallas_call(
        _rmsnorm_kernel,
        grid=(m // tm,),
        in_specs=[pl.BlockSpec((tm, D_MODEL), lambda i: (i, 0)),
                  pl.BlockSpec((1, D_MODEL), lambda i: (0, 0))],
        out_specs=pl.BlockSpec((tm, D_MODEL), lambda i: (i, 0)),
        out_shape=jax.ShapeDtypeStruct((m, D_MODEL), BF16),
        compiler_params=pltpu.CompilerParams(
            dimension_semantics=("arbitrary",), vmem_limit_bytes=VMEM_LIMIT),
        name="rmsnorm",
    )(x, gain)


def _inproj_epilogue(j, acc, o_ref, qg_ref, kg_ref, e_ref, cos_ref, sa_ref, sb_ref):
    def head_norm_rope(a, gain):
        cos, sa, sb = cos_ref[...], sa_ref[...], sb_ref[...]
        outs = []
        for c in range(a.shape[1] // LANES):
            ac = a[:, c * LANES:(c + 1) * LANES]
            sq = ac * ac
            hi = sq.astype(BF16)
            lo = (sq - hi.astype(F32)).astype(BF16)
            ms = jnp.dot(jnp.concatenate([hi, lo], axis=1), e_ref[...],
                         preferred_element_type=F32)
            xc = ac * lax.rsqrt(ms + EPS) * gain[:, c * LANES:(c + 1) * LANES]
            outs.append(xc * cos + pltpu.roll(xc, LANES - HEAD_DIM // 2, 1) * sa
                        + pltpu.roll(xc, HEAD_DIM // 2, 1) * sb)
        return jnp.concatenate(outs, axis=1)

    @pl.when(j < COL_ZA // TN)
    def _():
        o_ref[...] = head_norm_rope(acc, qg_ref[...]) * (HEAD_DIM ** -0.5)

    @pl.when(j == KV_TILE)
    def _():
        k = head_norm_rope(acc[:, :KV_WIDTH], kg_ref[...])
        o_ref[...] = jnp.concatenate([k, acc[:, KV_WIDTH:]], axis=1)

    @pl.when(jnp.logical_and(j >= COL_ZA // TN, j != KV_TILE))
    def _():
        o_ref[...] = acc


def _inproj_kernel(xn_ref, xs_ref, w_ref, qg_ref, kg_ref, e_ref, cos_ref, sa_ref, sb_ref,
                   cos_s_ref, sa_s_ref, sb_s_ref, o_ref, os_ref, wbf_ref):
    j, i = pl.program_id(1), pl.program_id(2)
    tm = o_ref.shape[0]

    @pl.when(i == 0)
    def _():
        wbf_ref[...] = w_ref[...].astype(BF16)

    x = xn_ref[pl.ds(pl.multiple_of(i * tm, tm), tm), :]
    acc = jnp.dot(x, wbf_ref[...], preferred_element_type=F32)
    _inproj_epilogue(j, acc, o_ref, qg_ref, kg_ref, e_ref, cos_ref, sa_ref, sb_ref)

    @pl.when(i == pl.num_programs(2) - 1)
    def _():
        acc_s = jnp.dot(xs_ref[...], wbf_ref[...], preferred_element_type=F32)
        _inproj_epilogue(j, acc_s, os_ref.at[0], qg_ref, kg_ref, e_ref,
                         cos_s_ref, sa_s_ref, sb_s_ref)


def _w_tile(j):
    n_q = COL_ZA // TN
    return jnp.where(j < n_q, j, jnp.where(j == KV_TILE, n_q, j + 1))


def _inproj(xn, xs, w, qg, kg, e, tabs_p, tabs_s, tm, halves):
    m, ms = xn.shape[0], xs.shape[0]
    rows_half = m // halves
    tiles = rows_half // tm
    n_tab = tabs_p[0].shape[0] // tm
    const = lambda shape: pl.BlockSpec(shape, lambda h, j, i: (0, 0))
    tab_p = pl.BlockSpec((tm, LANES), lambda h, j, i: ((h * tiles + i) % n_tab, 0))
    return pl.pallas_call(
        _inproj_kernel,
        grid=(halves, IN_WIDTH // TN, tiles),
        in_specs=[
            pl.BlockSpec((rows_half, D_MODEL), lambda h, j, i: (h, 0),
                         pipeline_mode=pl.Buffered(1)),
            const((ms, D_MODEL)),
            pl.BlockSpec((D_MODEL, TN), lambda h, j, i: (0, _w_tile(j))),
            const((1, TN)), const((1, KV_WIDTH)), const((2 * LANES, LANES)),
            tab_p, tab_p, tab_p,
            const((ms, LANES)), const((ms, LANES)), const((ms, LANES)),
        ],
        out_specs=[pl.BlockSpec((tm, TN), lambda h, j, i: (h * tiles + i, j)),
                   pl.BlockSpec((1, ms, TN), lambda h, j, i: (h, 0, j))],
        out_shape=[jax.ShapeDtypeStruct((m, IN_WIDTH), F32),
                   jax.ShapeDtypeStruct((halves, ms, IN_WIDTH), F32)],
        scratch_shapes=[pltpu.VMEM((D_MODEL, TN), BF16)],
        compiler_params=pltpu.CompilerParams(
            dimension_semantics=("arbitrary", "arbitrary", "arbitrary"),
            vmem_limit_bytes=VMEM_LIMIT),
        name="inproj",
    )(xn, xs, w, qg, kg, e, *tabs_p, *tabs_s)


def _dup_head(x, g, lo):
    r = pltpu.roll(x, HEAD_DIM, 1)
    return jnp.where(lo, x, r) if g % 2 == 0 else jnp.where(lo, r, x)


def _prompt_attn_kernel(sinks_ref, q_ref, kvc_ref, kvp_ref, kvm_ref, vmt_ref, bias_ref, o_ref):
    lo = lax.broadcasted_iota(jnp.int32, (1, LANES), 1) < HEAD_DIM
    head_lane = lax.broadcasted_iota(jnp.int32, (1, 4 * BLOCK), 1) // BLOCK
    bias = bias_ref[0]
    v_t = jnp.concatenate([kvp_ref[:, KV_WIDTH:].T, kvc_ref[:, KV_WIDTH:].T], axis=1)
    for g in range(N_KV_HEADS):
        ks = slice((g // 2) * LANES, (g // 2 + 1) * LANES)
        k_all = jnp.concatenate([_dup_head(kvp_ref[:, ks], g, lo), _dup_head(kvc_ref[:, ks], g, lo),
                                 _dup_head(kvm_ref[:, ks], g, lo)], axis=0).astype(BF16)
        qa = q_ref[:, (2 * g) * LANES:(2 * g + 1) * LANES]
        qb = q_ref[:, (2 * g + 1) * LANES:(2 * g + 2) * LANES]
        q4 = jnp.concatenate([jnp.where(lo, qa, 0.0), jnp.where(lo, 0.0, qa),
                              jnp.where(lo, qb, 0.0), jnp.where(lo, 0.0, qb)],
                             axis=0).astype(BF16)
        s = lax.dot_general(k_all, q4, NT_DIMS, preferred_element_type=F32)
        s_w = s[:2 * BLOCK] + bias
        s_m = s[2 * BLOCK:]
        sink = jnp.where(head_lane == 0, sinks_ref[4 * g],
                         jnp.where(head_lane == 1, sinks_ref[4 * g + 1],
                                   jnp.where(head_lane == 2, sinks_ref[4 * g + 2],
                                             sinks_ref[4 * g + 3])))
        m = jnp.maximum(jnp.maximum(jnp.max(s_w, axis=0, keepdims=True),
                                    jnp.max(s_m, axis=0, keepdims=True)), sink)
        p_w = jnp.exp(s_w - m)
        p_m = jnp.exp(s_m - m)
        den = (jnp.sum(p_w, axis=0, keepdims=True) + jnp.sum(p_m, axis=0, keepdims=True)
               + jnp.exp(sink - m))
        inv = 1.0 / den
        p = jnp.concatenate([p_w * inv, p_m * inv], axis=0).astype(BF16)
        hs = slice(g * HEAD_DIM, (g + 1) * HEAD_DIM)
        vt_g = jnp.concatenate([v_t[hs], vmt_ref[hs, :]], axis=1).astype(BF16)
        o_t = jnp.dot(vt_g, p, preferred_element_type=F32)
        for pair in range(2):
            two = jnp.concatenate([o_t[:, (2 * pair) * BLOCK:(2 * pair + 1) * BLOCK],
                                   o_t[:, (2 * pair + 1) * BLOCK:(2 * pair + 2) * BLOCK]], axis=0)
            o_ref[:, (2 * g + pair) * LANES:(2 * g + pair + 1) * LANES] = two.T


def _prompt_attention(sinks, h_p, h_s, vm_t, batch, seq):
    nb = seq // BLOCK
    kvb = COL_KV // (2 * KV_WIDTH)
    kj = np.arange(2 * BLOCK)[:, None]
    qi = np.arange(4 * BLOCK)[None, :] % BLOCK
    cur = (kj >= BLOCK) & (kj - BLOCK <= qi)
    prev = (kj < BLOCK) & (kj > qi)
    bias = jnp.asarray(np.where(np.stack([cur, cur | prev]), 0.0, NEG), F32)
    return pl.pallas_call(
        _prompt_attn_kernel,
        grid=(batch, nb),
        in_specs=[
            pl.BlockSpec(memory_space=pltpu.SMEM),
            pl.BlockSpec((BLOCK, ATTN_WIDTH), lambda b, i: (b * nb + i, 0)),
            pl.BlockSpec((BLOCK, 2 * KV_WIDTH), lambda b, i: (b * nb + i, kvb)),
            pl.BlockSpec((BLOCK, 2 * KV_WIDTH),
                         lambda b, i: (b * nb + jnp.maximum(i - 1, 0), kvb)),
            pl.BlockSpec((N_META, 2 * KV_WIDTH), lambda b, i: (h_s.shape[0] // N_META - 1, kvb)),
            pl.BlockSpec((KV_WIDTH, N_META), lambda b, i: (0, 0)),
            pl.BlockSpec((1, 2 * BLOCK, 4 * BLOCK), lambda b, i: (jnp.minimum(i, 1), 0, 0)),
        ],
        out_specs=pl.BlockSpec((BLOCK, ATTN_WIDTH), lambda b, i: (b * nb + i, 0)),
        out_shape=jax.ShapeDtypeStruct((batch * seq, ATTN_WIDTH), F32),
        compiler_params=pltpu.CompilerParams(
            dimension_semantics=("arbitrary", "arbitrary"), vmem_limit_bytes=VMEM_LIMIT),
        name="prompt_attn",
    )(sinks, h_p, h_p, h_p, h_s, vm_t, bias)


def _sample_attn_kernel(qp_ref, kn_ref, vn_ref, ck_ref, cv_ref, mk_ref, mv_ref, sink_ref,
                        o_ref, ok_ref, ov_ref):
    bt = ck_ref.shape[0]
    last = lax.broadcasted_iota(jnp.int32, (1, WINDOW), 1) == WINDOW - 1
    kn = kn_ref[0]
    vn = vn_ref[0]
    for b in range(bt):
        ok_ref[b] = jnp.where(last, kn[:, b:b + 1], pltpu.roll(ck_ref[b], WINDOW - 1, 1))
        ov_ref[b] = jnp.where(last, vn[:, b:b + 1], pltpu.roll(cv_ref[b], WINDOW - 1, 1))
    qp = qp_ref[...].astype(BF16)
    k_w = ok_ref[...].astype(BF16)
    v_w = ov_ref[...].astype(BF16)
    k_m = mk_ref[...].astype(BF16)
    v_m = mv_ref[...].astype(BF16)
    s_w = jnp.einsum("bhc,bcw->bhw", qp, k_w, preferred_element_type=F32)
    s_m = jnp.einsum("bhc,bkc->bhk", qp, k_m, preferred_element_type=F32)
    sink = sink_ref[...][None]
    m = jnp.maximum(jnp.maximum(jnp.max(s_w, axis=-1, keepdims=True),
                                jnp.max(s_m, axis=-1, keepdims=True)), sink)
    p_w = jnp.exp(s_w - m)
    p_m = jnp.exp(s_m - m)
    den = (jnp.sum(p_w, axis=-1, keepdims=True) + jnp.sum(p_m, axis=-1, keepdims=True)
           + jnp.exp(sink - m))
    inv = 1.0 / den
    o = (jnp.einsum("bhw,bcw->bhc", (p_w * inv).astype(BF16), v_w, preferred_element_type=F32)
         + jnp.einsum("bhk,bkc->bhc", (p_m * inv).astype(BF16), v_m,
                      preferred_element_type=F32))
    hh = lax.broadcasted_iota(jnp.int32, (1, N_Q_HEADS, KV_WIDTH), 1) // (N_Q_HEADS // N_KV_HEADS)
    cc = lax.broadcasted_iota(jnp.int32, (1, N_Q_HEADS, KV_WIDTH), 2) // HEAD_DIM
    o = jnp.where(hh == cc, o, 0.0)
    o = o[:, :, :LANES] + o[:, :, LANES:]
    o_ref[...] = o + pltpu.roll(o, HEAD_DIM, 2)


def _sample_attention(qp, kn_t, vn_t, cache_k, cache_v, meta_k, meta_v, sink_col, bt):
    n = cache_k.shape[0]
    win_spec = pl.BlockSpec((bt, KV_WIDTH, WINDOW), lambda t: (t, 0, 0))
    new_spec = pl.BlockSpec((1, KV_WIDTH, bt), lambda t: (t, 0, 0))
    meta_spec = pl.BlockSpec((bt, N_META, KV_WIDTH), lambda t: (t, 0, 0))
    return pl.pallas_call(
        _sample_attn_kernel,
        grid=(n // bt,),
        in_specs=[
            pl.BlockSpec((bt, N_Q_HEADS, KV_WIDTH), lambda t: (t, 0, 0)),
            new_spec, new_spec, win_spec, win_spec, meta_spec, meta_spec,
            pl.BlockSpec((N_Q_HEADS, 1), lambda t: (0, 0)),
        ],
        out_specs=[
            pl.BlockSpec((bt, N_Q_HEADS, LANES), lambda t: (t, 0, 0)),
            win_spec, win_spec,
        ],
        out_shape=[
            jax.ShapeDtypeStruct((n, N_Q_HEADS, LANES), F32),
            jax.ShapeDtypeStruct(cache_k.shape, F32),
            jax.ShapeDtypeStruct(cache_v.shape, F32),
        ],
        compiler_params=pltpu.CompilerParams(
            dimension_semantics=("arbitrary",), vmem_limit_bytes=VMEM_LIMIT),
        name="sample_attn",
    )(qp, kn_t, vn_t, cache_k, cache_v, meta_k, meta_v, sink_col)


def _gelu_tanh(x):
    return 0.5 * x * (1.0 + jnp.tanh(0.7978845608028654 * (x + 0.044715 * (x * x * x))))


def _prompt_ssm_kernel(u_ref, um_ref, t8_ref, win_ref, wst_ref, ar_ref, ai_ref, d_ref,
                       y_ref, sre_ref, sim_ref):
    batch = sre_ref.shape[0]
    n_rows = u_ref.shape[0] // CHUNK
    nc = n_rows // batch
    ucat = jnp.concatenate([u_ref[pl.ds(j, n_rows, stride=CHUNK), :] for j in range(CHUNK)],
                           axis=1)
    ub = ucat.astype(BF16)
    win = win_ref[0]
    g = jnp.dot(ub, win, preferred_element_type=F32)

    n_mc = um_ref.shape[0] // CHUNK
    um = jnp.concatenate([um_ref[pl.ds(j, n_mc, stride=CHUNK), :] for j in range(CHUNK)], axis=1)
    gm = jnp.dot(um.astype(BF16), win, preferred_element_type=F32)
    a_r, a_i = ar_ref[0, 0:1, :], ai_ref[0, 0:1, :]
    s0r, s0i = gm[0:1, :STATE_TILE], gm[0:1, STATE_TILE:]
    for c in range(1, n_mc):
        s0r, s0i = (a_r * s0r - a_i * s0i + gm[c:c + 1, :STATE_TILE],
                    a_r * s0i + a_i * s0r + gm[c:c + 1, STATE_TILE:])

    row = lax.broadcasted_iota(jnp.int32, (nc, 1), 0)
    first = row == 0
    prev_r, prev_i = [], []
    for b in range(batch):
        gr = g[b * nc:(b + 1) * nc, :STATE_TILE]
        gi = g[b * nc:(b + 1) * nc, STATE_TILE:]
        gr = gr + jnp.where(first, a_r * s0r - a_i * s0i, 0.0)
        gi = gi + jnp.where(first, a_r * s0i + a_i * s0r, 0.0)
        for k in range(SCAN_LEVELS):
            d = 1 << k
            if d >= nc:
                break
            k_r, k_i = ar_ref[0, k:k + 1, :], ai_ref[0, k:k + 1, :]
            if d < CHUNK:
                keep = row >= d
                sh_r = jnp.where(keep, pltpu.roll(gr, d, 0), 0.0)
                sh_i = jnp.where(keep, pltpu.roll(gi, d, 0), 0.0)
            else:
                zero = jnp.zeros((d, STATE_TILE), F32)
                sh_r = jnp.concatenate([zero, gr[:nc - d]], axis=0)
                sh_i = jnp.concatenate([zero, gi[:nc - d]], axis=0)
            gr, gi = gr + (k_r * sh_r - k_i * sh_i), gi + (k_r * sh_i + k_i * sh_r)
        sre_ref[b, 0] = gr[nc - 1:nc, :]
        sim_ref[b, 0] = gi[nc - 1:nc, :]
        prev_r.append(jnp.where(first, s0r, pltpu.roll(gr, 1, 0)))
        prev_i.append(jnp.where(first, s0i, pltpu.roll(gi, 1, 0)))
    sp = jnp.concatenate([jnp.concatenate(prev_r, axis=0), jnp.concatenate(prev_i, axis=0)],
                         axis=1).astype(BF16)
    y2 = (jnp.dot(ub, t8_ref[0], preferred_element_type=F32)
          + lax.dot_general(sp, wst_ref[0], NT_DIMS, preferred_element_type=F32))
    dsk = d_ref[...]
    for j in range(CHUNK):
        sl = slice(j * LANES, (j + 1) * LANES)
        y_ref[pl.ds(j, n_rows, stride=CHUNK), :] = _gelu_tanh(y2[:, sl] + dsk * ucat[:, sl])


def _prompt_ssm(h_p, h_s, t8, win, wst, a_r, a_i, d_skip, batch):
    ub = COL_U // LANES
    rows = h_p.shape[0]
    wspec = pl.BlockSpec((1, CHUNK_WIDTH, CHUNK_WIDTH), lambda gt: (gt, 0, 0))
    aspec = pl.BlockSpec((1, SCAN_LEVELS, STATE_TILE), lambda gt: (gt, 0, 0))
    sspec = pl.BlockSpec((batch, 1, 1, STATE_TILE), lambda gt: (0, gt, 0, 0))
    return pl.pallas_call(
        _prompt_ssm_kernel,
        grid=(N_GROUP_TILES,),
        in_specs=[
            pl.BlockSpec((rows, LANES), lambda gt: (0, ub + gt)),
            pl.BlockSpec((N_META, LANES), lambda gt: (h_s.shape[0] // N_META - 1, ub + gt)),
            wspec, wspec, wspec, aspec, aspec,
            pl.BlockSpec((1, LANES), lambda gt: (0, gt)),
        ],
        out_specs=[pl.BlockSpec((rows, LANES), lambda gt: (0, gt)), sspec, sspec],
        out_shape=[
            jax.ShapeDtypeStruct((rows, SSM_WIDTH), F32),
            jax.ShapeDtypeStruct((batch, N_GROUP_TILES, 1, STATE_TILE), F32),
            jax.ShapeDtypeStruct((batch, N_GROUP_TILES, 1, STATE_TILE), F32),
        ],
        compiler_params=pltpu.CompilerParams(
            dimension_semantics=("arbitrary",), vmem_limit_bytes=VMEM_LIMIT),
        name="prompt_ssm",
    )(h_p, h_s, t8, win, wst, a_r, a_i, d_skip)


def _sample_ssm_kernel(u_ref, hr_ref, hi_ref, b_ref, wc_ref, lr_ref, li_ref, d_ref,
                       y_ref, sre_ref, sim_ref):
    u = u_ref[...]
    bu = jnp.dot(u.astype(BF16), b_ref[0], preferred_element_type=F32)
    l_r, l_i = lr_ref[0], li_ref[0]
    h_r, h_i = hr_ref[...], hi_ref[...]
    n_r = l_r * h_r - l_i * h_i + bu[:, :STATE_TILE]
    n_i = l_r * h_i + l_i * h_r + bu[:, STATE_TILE:]
    sre_ref[...] = n_r
    sim_ref[...] = n_i
    hn = jnp.concatenate([n_r, n_i], axis=1).astype(BF16)
    y = lax.dot_general(hn, wc_ref[0], NT_DIMS, preferred_element_type=F32) + d_ref[...] * u
    y_ref[...] = _gelu_tanh(y)


def _sample_ssm(h_s, h0r, h0i, win, wc, l_r, l_i, d_skip, n):
    ub = COL_U // LANES
    st = pl.BlockSpec((n, STATE_TILE), lambda gt: (0, gt))
    lam = pl.BlockSpec((1, 1, STATE_TILE), lambda gt: (gt, 0, 0))
    return pl.pallas_call(
        _sample_ssm_kernel,
        grid=(N_GROUP_TILES,),
        in_specs=[
            pl.BlockSpec((n, LANES), lambda gt: (0, ub + gt)),
            st, st,
            pl.BlockSpec((1, LANES, 2 * STATE_TILE), lambda gt: (gt, CHUNK - 1, 0)),
            pl.BlockSpec((1, LANES, 2 * STATE_TILE), lambda gt: (gt, 0, 0)),
            lam, lam,
            pl.BlockSpec((1, LANES), lambda gt: (0, gt)),
        ],
        out_specs=[pl.BlockSpec((n, LANES), lambda gt: (0, gt)), st, st],
        out_shape=[
            jax.ShapeDtypeStruct((n, SSM_WIDTH), F32),
            jax.ShapeDtypeStruct((n, N_SSM_GROUPS * SSM_STATE), F32),
            jax.ShapeDtypeStruct((n, N_SSM_GROUPS * SSM_STATE), F32),
        ],
        compiler_params=pltpu.CompilerParams(
            dimension_semantics=("arbitrary",), vmem_limit_bytes=VMEM_LIMIT),
        name="sample_ssm",
    )(h_s, h0r, h0i, win, wc, l_r, l_i, d_skip)


def _merge_kernel(o_ref, za_ref, zs_ref, ga_ref, gs_ref, y_ref, x_ref, bglu_ref,
                  wglu_ref, wa_ref, ws_ref, wo_ref, out_ref):
    y = y_ref[...]
    t = jnp.dot(y.astype(BF16), wglu_ref[...], preferred_element_type=F32) + bglu_ref[...]
    y = y * jax.nn.sigmoid(t)
    za = za_ref[...]
    zs = zs_ref[...]
    a_in = (o_ref[...] * (za * jax.nn.sigmoid(za))).astype(BF16)
    s_in = (y * (zs * jax.nn.sigmoid(zs))).astype(BF16)
    br_a = jnp.dot(a_in, wa_ref[...], preferred_element_type=F32)
    br_s = jnp.dot(s_in, ws_ref[...], preferred_element_type=F32)
    mix = (jax.nn.sigmoid(ga_ref[...]) * br_a + jax.nn.sigmoid(gs_ref[...]) * br_s).astype(BF16)
    out_ref[...] = x_ref[...] + jnp.dot(mix, wo_ref[...], preferred_element_type=F32)


def _merge(o, h, y, x, b_glu, w_glu, w_a, w_s, w_o, tm):
    m = x.shape[0]
    const = lambda shape: pl.BlockSpec(shape, lambda i: (0, 0), pipeline_mode=pl.Buffered(1))
    return pl.pallas_call(
        _merge_kernel,
        grid=(m // tm,),
        in_specs=[
            pl.BlockSpec((tm, ATTN_WIDTH), lambda i: (i, 0)),
            pl.BlockSpec((tm, ATTN_WIDTH), lambda i: (i, COL_ZA // ATTN_WIDTH)),
            pl.BlockSpec((tm, SSM_WIDTH), lambda i: (i, COL_ZS // SSM_WIDTH)),
            pl.BlockSpec((tm, D_MODEL), lambda i: (i, COL_GA // D_MODEL)),
            pl.BlockSpec((tm, D_MODEL), lambda i: (i, COL_GS // D_MODEL)),
            pl.BlockSpec((tm, SSM_WIDTH), lambda i: (i, 0)),
            pl.BlockSpec((tm, D_MODEL), lambda i: (i, 0)),
            const((1, SSM_WIDTH)),
            const((SSM_WIDTH, SSM_WIDTH)),
            const((ATTN_WIDTH, D_MODEL)),
            const((SSM_WIDTH, D_MODEL)),
            const((D_MODEL, D_MODEL)),
        ],
        out_specs=pl.BlockSpec((tm, D_MODEL), lambda i: (i, 0)),
        out_shape=jax.ShapeDtypeStruct((m, D_MODEL), F32),
        compiler_params=pltpu.CompilerParams(
            dimension_semantics=("arbitrary",), vmem_limit_bytes=VMEM_LIMIT),
        name="merge",
    )(o, h, h, h, h, y, x, b_glu, w_glu, w_a, w_s, w_o)


def _rope_tables(pos):
    half = HEAD_DIM // 2
    inv_freq = (ROPE_THETA ** (-np.arange(half, dtype=np.float32) / half)).astype(np.float32)
    ang = np.asarray(pos, np.float32)[:, None] * inv_freq[None, :]
    cos, sin = np.cos(ang).astype(np.float32), np.sin(ang).astype(np.float32)
    zero = np.zeros_like(sin)
    reps = LANES // HEAD_DIM
    cos_t = np.tile(np.concatenate([cos, cos], axis=1), (1, reps))
    sin_a = np.tile(np.concatenate([-sin, zero], axis=1), (1, reps))
    sin_b = np.tile(np.concatenate([zero, sin], axis=1), (1, reps))
    return jnp.asarray(cos_t), jnp.asarray(sin_a), jnp.asarray(sin_b)


def _split_bf16(x):
    hi = x.astype(BF16)
    return hi, (x - hi.astype(F32)).astype(BF16)


def _dot_nt_f32(a, b):
    ah, al = _split_bf16(a)
    bh, bl = _split_bf16(b)
    dot = lambda x, y: lax.dot_general(x, y, NT_DIMS, preferred_element_type=F32)
    return dot(ah, bh) + dot(ah, bl) + dot(al, bh)


def _ssm_tables_kernel(are_ref, aim_ref, ldt_ref, bre_ref, bim_ref, cre_ref, cim_ref,
                       win_ref, wst_ref, t8_ref, wc_ref, ar_ref, ai_ref, lr_ref, li_ref):
    a_r, a_i = are_ref[0], aim_ref[0]
    dt = jnp.exp(ldt_ref[0])
    mag = jnp.exp(a_r * dt)
    l_r, l_i = mag * jnp.cos(a_i * dt), mag * jnp.sin(a_i * dt)
    den = a_r * a_r + a_i * a_i
    f_r = ((l_r - 1.0) * a_r + l_i * a_i) / den
    f_i = (l_i * a_r - (l_r - 1.0) * a_i) / den
    shape = (LANES, STATE_TILE)
    own = (lax.broadcasted_iota(jnp.int32, shape, 0) // SSM_GROUP
           == lax.broadcasted_iota(jnp.int32, shape, 1) // SSM_STATE)

    def expand(ref):
        x = ref[0]
        return jnp.where(own, jnp.concatenate([x] * (STATE_TILE // LANES), axis=1), 0.0)

    b_r, b_i = expand(bre_ref), expand(bim_ref)
    c_r, c_i = expand(cre_ref), expand(cim_ref)
    bb_r, bb_i = _cmul(f_r, f_i, b_r, b_i)
    wc = jnp.concatenate([c_r, -c_i], axis=1)
    wc_ref[0] = wc.astype(BF16)
    p_r, p_i = jnp.ones_like(l_r), jnp.zeros_like(l_r)
    kern = []
    for n in range(CHUNK):
        x_r, x_i = _cmul(p_r, p_i, bb_r, bb_i)
        x = jnp.concatenate([x_r, x_i], axis=1)
        win_ref[0, (CHUNK - 1 - n) * LANES:(CHUNK - n) * LANES, :] = x.astype(BF16)
        kern.append(_dot_nt_f32(x, wc).astype(BF16))
        p_r, p_i = _cmul(p_r, p_i, l_r, l_i)
        s_r, s_i = _cmul(c_r, c_i, p_r, p_i)
        wst_ref[0, n * LANES:(n + 1) * LANES, :] = jnp.concatenate([s_r, -s_i],
                                                                   axis=1).astype(BF16)
    zero = jnp.zeros((LANES, LANES), BF16)
    for j in range(CHUNK):
        for jp in range(CHUNK):
            t8_ref[0, j * LANES:(j + 1) * LANES, jp * LANES:(jp + 1) * LANES] = (
                kern[jp - j] if jp >= j else zero)
    for k in range(SCAN_LEVELS):
        ar_ref[0, k:k + 1, :] = p_r
        ai_ref[0, k:k + 1, :] = p_i
        p_r, p_i = _cmul(p_r, p_i, p_r, p_i)
    lr_ref[0] = l_r
    li_ref[0] = l_i


def _ssm_tables(a_re, a_im, log_dt, b_re, b_im, c_re, c_im):
    gt = N_GROUP_TILES
    lane_vec = lambda v: v.reshape(gt, 1, STATE_TILE)
    dup = lambda v: jnp.concatenate([v, v], axis=-1)
    rows = GROUPS_PER_TILE * SSM_GROUP
    b_t = lambda v: dup(v.transpose(0, 2, 1).reshape(gt, rows, SSM_STATE))
    c_t = lambda v: dup(v.reshape(gt, rows, SSM_STATE))
    vec = pl.BlockSpec((1, 1, STATE_TILE), lambda g: (g, 0, 0))
    par = pl.BlockSpec((1, rows, LANES), lambda g: (g, 0, 0))
    big = pl.BlockSpec((1, CHUNK_WIDTH, CHUNK_WIDTH), lambda g: (g, 0, 0))
    wcs = pl.BlockSpec((1, LANES, 2 * STATE_TILE), lambda g: (g, 0, 0))
    scan = pl.BlockSpec((1, SCAN_LEVELS, STATE_TILE), lambda g: (g, 0, 0))
    return pl.pallas_call(
        _ssm_tables_kernel,
        grid=(gt,),
        in_specs=[vec, vec, vec, par, par, par, par],
        out_specs=[big, big, big, wcs, scan, scan, vec, vec],
        out_shape=[
            jax.ShapeDtypeStruct((gt, CHUNK_WIDTH, 2 * STATE_TILE), BF16),
            jax.ShapeDtypeStruct((gt, CHUNK_WIDTH, 2 * STATE_TILE), BF16),
            jax.ShapeDtypeStruct((gt, CHUNK_WIDTH, CHUNK_WIDTH), BF16),
            jax.ShapeDtypeStruct((gt, LANES, 2 * STATE_TILE), BF16),
            jax.ShapeDtypeStruct((gt, SCAN_LEVELS, STATE_TILE), F32),
            jax.ShapeDtypeStruct((gt, SCAN_LEVELS, STATE_TILE), F32),
            jax.ShapeDtypeStruct((gt, 1, STATE_TILE), F32),
            jax.ShapeDtypeStruct((gt, 1, STATE_TILE), F32),
        ],
        compiler_params=pltpu.CompilerParams(
            dimension_semantics=("arbitrary",), vmem_limit_bytes=VMEM_LIMIT),
        name="ssm_tables",
    )(lane_vec(a_re), lane_vec(a_im), lane_vec(jnp.repeat(log_dt, SSM_STATE)),
      b_t(b_re), b_t(b_im), c_t(c_re), c_t(c_im))


def kernel(x_prompt, x_sample, cache_win_k, cache_win_v, cache_meta_k, cache_meta_v,
           state_ssm_re, state_ssm_im, meta_tokens, norm_gain, w_in, q_norm_gain, k_norm_gain,
           sinks, a_re, a_im, log_dt, b_re, b_im, c_re, c_im, d_skip, w_glu, b_glu,
           w_attn_out, w_ssm_out, w_out):
    depth = w_in.shape[0]
    assert depth == 1, "single-layer trunk"
    batch, seq = x_prompt.shape[:2]
    n_s = x_sample.shape[0]
    assert x_sample.shape[1] == 1 and seq % (CHUNK * BLOCK) == 0
    l = 0

    w = w_in[l]
    qg = jnp.tile(q_norm_gain[l], TN // HEAD_DIM)[None]
    kg = jnp.tile(k_norm_gain[l], KV_WIDTH // HEAD_DIM)[None]
    lane = np.arange(LANES)
    e128 = np.where((lane[:, None] // HEAD_DIM) == (lane[None, :] // HEAD_DIM),
                    1.0 / HEAD_DIM, 0.0).astype(np.float32)
    e = jnp.asarray(np.concatenate([e128, e128], axis=0), BF16)
    gain = norm_gain[l][None]

    pos_p = N_META + np.arange(seq)
    pos_small = np.concatenate([np.full((n_s,), PAST_LEN), np.arange(N_META)])
    x_small = jnp.concatenate([x_sample[:, 0, :], meta_tokens.astype(x_prompt.dtype)], axis=0)
    xp = x_prompt.reshape(batch * seq, D_MODEL)
    xn_p = _rmsnorm(xp, gain, tm=512)
    xn_s = _rmsnorm(x_small, gain, tm=n_s + N_META)
    h_p, h_s = _inproj(xn_p, xn_s, w, qg, kg, e, _rope_tables(pos_p), _rope_tables(pos_small),
                       tm=1024, halves=2)
    h_s = h_s[0]

    win, wst, t8, wc, a_r, a_i, l_r, l_i = _ssm_tables(
        a_re[l], a_im[l], log_dt[l], b_re[l], b_im[l], c_re[l], c_im[l])
    dsk = d_skip[l][None]

    vm_t = h_s[n_s:, COL_KV + KV_WIDTH:].T
    o_p = _prompt_attention(sinks[l], h_p, h_s, vm_t, batch, seq)
    y_p, p_re, p_im = _prompt_ssm(h_p, h_s, t8, win, wst, a_r, a_i, dsk, batch)
    w_glu_bf = w_glu[l].astype(BF16)
    w_a_bf = w_attn_out[l].astype(BF16)
    w_s_bf = w_ssm_out[l].astype(BF16)
    w_o_bf = w_out[l].astype(BF16)
    bglu = b_glu[l][None]
    y_prompt = _merge(o_p, h_p, y_p, xp, bglu,
                      w_glu_bf, w_a_bf, w_s_bf, w_o_bf, tm=256).reshape(batch, seq, D_MODEL)

    q_s = h_s[:n_s, COL_Q:COL_Q + ATTN_WIDTH].reshape(n_s, N_Q_HEADS, 1, HEAD_DIM)
    sel = (np.arange(N_Q_HEADS)[:, None] // (N_Q_HEADS // N_KV_HEADS)
           == np.arange(N_KV_HEADS)[None, :]).astype(np.float32)
    qp = (q_s * sel[None, :, :, None]).reshape(n_s, N_Q_HEADS, KV_WIDTH)
    bt = 8
    to_t = lambda c: c.transpose(0, 2, 3, 1).reshape(n_s, KV_WIDTH, WINDOW)
    from_t = lambda c: c.reshape(n_s, N_KV_HEADS, HEAD_DIM, WINDOW).transpose(0, 3, 1, 2)[None]
    new_t = lambda c0: h_s[:n_s, c0:c0 + KV_WIDTH].reshape(n_s // bt, bt, KV_WIDTH).transpose(0, 2, 1)
    o_s, s_win_k, s_win_v = _sample_attention(
        qp, new_t(COL_KV), new_t(COL_KV + KV_WIDTH), to_t(cache_win_k[l]), to_t(cache_win_v[l]),
        cache_meta_k[l].reshape(n_s, N_META, KV_WIDTH), cache_meta_v[l].reshape(n_s, N_META, KV_WIDTH),
        sinks[l][:, None], bt=bt)
    o_s = o_s[:, :, :HEAD_DIM].reshape(n_s, ATTN_WIDTH)
    y_s, s_re, s_im = _sample_ssm(
        h_s, state_ssm_re[l].reshape(n_s, -1), state_ssm_im[l].reshape(n_s, -1),
        win, wc, l_r, l_i, dsk, n_s)
    y_sample = _merge(o_s, h_s, y_s, x_sample[:, 0, :], bglu,
                      w_glu_bf, w_a_bf, w_s_bf, w_o_bf, tm=n_s).reshape(n_s, 1, D_MODEL)

    kv_p = h_p.reshape(batch, seq, IN_WIDTH)[:, seq - WINDOW:, COL_KV:]
    kv_p = kv_p.reshape(batch, WINDOW, 2, N_KV_HEADS, HEAD_DIM)
    p_win_k = kv_p[:, :, 0][None]
    p_win_v = kv_p[:, :, 1][None]
    kv_m = h_s[n_s:, COL_KV:].reshape(N_META, 2, N_KV_HEADS, HEAD_DIM)
    p_meta_k = jnp.broadcast_to(kv_m[None, :, 0], (batch, N_META, N_KV_HEADS, HEAD_DIM))[None]
    p_meta_v = jnp.broadcast_to(kv_m[None, :, 1], (batch, N_META, N_KV_HEADS, HEAD_DIM))[None]
    p_ssm_re = p_re.reshape(batch, N_SSM_GROUPS, SSM_STATE)[None]
    p_ssm_im = p_im.reshape(batch, N_SSM_GROUPS, SSM_STATE)[None]
    st_shape = (1, n_s, N_SSM_GROUPS, SSM_STATE)
    return (y_prompt, y_sample, p_win_k, p_win_v, p_meta_k, p_meta_v, p_ssm_re, p_ssm_im,
            from_t(s_win_k), from_t(s_win_v), s_re.reshape(st_shape), s_im.reshape(st_shape))
```

```python
import functools

import jax
import jax.numpy as jnp
import numpy as np
from jax import lax
from jax.experimental import pallas as pl
from jax.experimental.pallas import tpu as pltpu

F32 = jnp.float32
BF16 = jnp.bfloat16

D_MODEL = 2048
N_META = 16
HEAD_DIM = 64
N_Q_HEADS = 16
N_KV_HEADS = 4
ATTN_WIDTH = N_Q_HEADS * HEAD_DIM
KV_WIDTH = N_KV_HEADS * HEAD_DIM
WINDOW = 128
BLOCK = 128
ROPE_THETA = 10000.0
SSM_WIDTH = D_MODEL // 2
SSM_GROUP = 16
N_SSM_GROUPS = SSM_WIDTH // SSM_GROUP
SSM_STATE = 64
EPS = 1e-6
PAST_LEN = 8192

LANES = 128
CHUNK = 8
GROUPS_PER_TILE = LANES // SSM_GROUP
N_GROUP_TILES = N_SSM_GROUPS // GROUPS_PER_TILE
CHUNK_WIDTH = CHUNK * LANES
STATE_TILE = GROUPS_PER_TILE * SSM_STATE
SCAN_LEVELS = 8

COL_Q = 0
COL_ZA = 1024
COL_U = 2048
COL_ZS = 3072
COL_GA = 4096
COL_GS = 6144
COL_KV = 8192
IN_WIDTH = 8704
TN = 512
KV_TILE = COL_KV // TN
VMEM_LIMIT = 56 * 1024 * 1024
NEG = -1e30
NT_DIMS = (((1,), (1,)), ((), ()))


def _cmul(ar, ai, br, bi):
    return ar * br - ai * bi, ar * bi + ai * br


def _rmsnorm_kernel(x_ref, gain_ref, o_ref):
    x = x_ref[...]
    r = lax.rsqrt(jnp.mean(x * x, axis=-1, keepdims=True) + EPS)
    o_ref[...] = (x * r * gain_ref[...]).astype(BF16)


def _rmsnorm(x, gain, tm):
    m = x.shape[0]
    return pl.pallas_call(
        _rmsnorm_kernel,
        grid=(m // tm,),
        in_specs=[pl.BlockSpec((tm, D_MODEL), lambda i: (i, 0)),
                  pl.BlockSpec((1, D_MODEL), lambda i: (0, 0))],
        out_specs=pl.BlockSpec((tm, D_MODEL), lambda i: (i, 0)),
        out_shape=jax.ShapeDtypeStruct((m, D_MODEL), BF16),
        compiler_params=pltpu.CompilerParams(
            dimension_semantics=("arbitrary",), vmem_limit_bytes=VMEM_LIMIT),
        name="rmsnorm",
    )(x, gain)


def _inproj_epilogue(j, o_ref, qg_ref, kg_ref, e_ref, cos_ref, sa_ref, sb_ref):
    def head_norm_rope(n_chunks, gain_ref, scale):
        cos, sa, sb = cos_ref[...], sa_ref[...], sb_ref[...]
        for c in range(n_chunks):
            cs = slice(c * LANES, (c + 1) * LANES)
            ac = o_ref[:, cs]
            sq = ac * ac
            hi = sq.astype(BF16)
            lo = (sq - hi.astype(F32)).astype(BF16)
            ms = jnp.dot(jnp.concatenate([hi, lo], axis=1), e_ref[...],
                         preferred_element_type=F32)
            xc = ac * lax.rsqrt(ms + EPS) * gain_ref[:, cs]
            o_ref[:, cs] = (xc * cos + pltpu.roll(xc, LANES - HEAD_DIM // 2, 1) * sa
                            + pltpu.roll(xc, HEAD_DIM // 2, 1) * sb) * scale

    @pl.when(j < COL_ZA // TN)
    def _():
        head_norm_rope(TN // LANES, qg_ref, HEAD_DIM ** -0.5)

    @pl.when(j == KV_TILE)
    def _():
        head_norm_rope(KV_WIDTH // LANES, kg_ref, 1.0)


def _inproj_kernel(xn_ref, xs_ref, w_ref, qg_ref, kg_ref, e_ref, cos_ref, sa_ref, sb_ref,
                   cos_s_ref, sa_s_ref, sb_s_ref, o_ref, os_ref, wbf_ref):
    h, j, i = pl.program_id(0), pl.program_id(1), pl.program_id(2)
    tm = o_ref.shape[0]

    @pl.when(i == 0)
    def _():
        wbf_ref[...] = w_ref[...].astype(BF16)

    x = xn_ref[pl.ds(pl.multiple_of(i * tm, tm), tm), :]
    o_ref[...] = jnp.dot(x, wbf_ref[...], preferred_element_type=F32)
    _inproj_epilogue(j, o_ref, qg_ref, kg_ref, e_ref, cos_ref, sa_ref, sb_ref)

    @pl.when(jnp.logical_and(h == 0, jnp.logical_and(j == 0, i == 0)))
    def _():
        os_ref[...] = jnp.zeros(os_ref.shape, F32)

    @pl.when(jnp.logical_and(h == pl.num_programs(0) - 1, i == pl.num_programs(2) - 1))
    def _():
        os_ref[...] = jnp.dot(xs_ref[...], wbf_ref[...], preferred_element_type=F32)
        _inproj_epilogue(j, os_ref, qg_ref, kg_ref, e_ref, cos_s_ref, sa_s_ref, sb_s_ref)


def _w_tile(j):
    n_q = COL_ZA // TN
    return jnp.where(j < n_q, j, jnp.where(j == KV_TILE, n_q, j + 1))


def _inproj(xn, xs, w, qg, kg, e, tabs_p, tabs_s, tm, halves):
    m, ms = xn.shape[0], xs.shape[0]
    rows_half = m // halves
    tiles = rows_half // tm
    n_tab = tabs_p[0].shape[0] // tm
    n_col = IN_WIDTH // TN
    const = lambda shape: pl.BlockSpec(shape, lambda h, j, i: (0, 0))
    tab_p = pl.BlockSpec((tm, LANES), lambda h, j, i: ((h * tiles + i) % n_tab, 0))
    return pl.pallas_call(
        _inproj_kernel,
        grid=(halves, IN_WIDTH // TN, tiles),
        in_specs=[
            pl.BlockSpec((rows_half, D_MODEL), lambda h, j, i: (h, 0),
                         pipeline_mode=pl.Buffered(1)),
            const((ms, D_MODEL)),
            pl.BlockSpec((D_MODEL, TN), lambda h, j, i: (0, _w_tile(j))),
            const((1, TN)), const((1, KV_WIDTH)), const((2 * LANES, LANES)),
            tab_p, tab_p, tab_p,
            const((ms, LANES)), const((ms, LANES)), const((ms, LANES)),
        ],
        out_specs=[pl.BlockSpec((tm, TN), lambda h, j, i: (h * tiles + i, j)),
                   pl.BlockSpec((ms, TN),
                                lambda h, j, i: (0, jnp.where(h == halves - 1, j, n_col)))],
        out_shape=[jax.ShapeDtypeStruct((m, IN_WIDTH), F32),
                   jax.ShapeDtypeStruct((ms, IN_WIDTH + TN), F32)],
        scratch_shapes=[pltpu.VMEM((D_MODEL, TN), BF16)],
        compiler_params=pltpu.CompilerParams(
            dimension_semantics=("arbitrary", "arbitrary", "arbitrary"),
            vmem_limit_bytes=VMEM_LIMIT),
        name="inproj",
    )(xn, xs, w, qg, kg, e, *tabs_p, *tabs_s)


def _dup_head(x, g, lo):
    r = pltpu.roll(x, HEAD_DIM, 1)
    return jnp.where(lo, x, r) if g % 2 == 0 else jnp.where(lo, r, x)


def _prompt_attn_kernel(sinks_ref, q_ref, kvc_ref, kvp_ref, kvm_ref, vmt_ref, bias_ref, o_ref):
    lo = lax.broadcasted_iota(jnp.int32, (1, LANES), 1) < HEAD_DIM
    head_lane = lax.broadcasted_iota(jnp.int32, (1, 4 * BLOCK), 1) // BLOCK
    bias = bias_ref[0]
    v_t = jnp.concatenate([kvp_ref[:, KV_WIDTH:].T, kvc_ref[:, KV_WIDTH:].T], axis=1)
    for g in range(N_KV_HEADS):
        ks = slice((g // 2) * LANES, (g // 2 + 1) * LANES)
        k_all = jnp.concatenate([_dup_head(kvp_ref[:, ks], g, lo), _dup_head(kvc_ref[:, ks], g, lo),
                                 _dup_head(kvm_ref[:, ks], g, lo)], axis=0).astype(BF16)
        qa = q_ref[:, (2 * g) * LANES:(2 * g + 1) * LANES]
        qb = q_ref[:, (2 * g + 1) * LANES:(2 * g + 2) * LANES]
        q4 = jnp.concatenate([jnp.where(lo, qa, 0.0), jnp.where(lo, 0.0, qa),
                              jnp.where(lo, qb, 0.0), jnp.where(lo, 0.0, qb)],
                             axis=0).astype(BF16)
        s = lax.dot_general(k_all, q4, NT_DIMS, preferred_element_type=F32)
        s_w = s[:2 * BLOCK] + bias
        s_m = s[2 * BLOCK:]
        sink = jnp.where(head_lane == 0, sinks_ref[4 * g],
                         jnp.where(head_lane == 1, sinks_ref[4 * g + 1],
                                   jnp.where(head_lane == 2, sinks_ref[4 * g + 2],
                                             sinks_ref[4 * g + 3])))
        m = jnp.maximum(jnp.maximum(jnp.max(s_w, axis=0, keepdims=True),
                                    jnp.max(s_m, axis=0, keepdims=True)), sink)
        p_w = jnp.exp(s_w - m)
        p_m = jnp.exp(s_m - m)
        den = (jnp.sum(p_w, axis=0, keepdims=True) + jnp.sum(p_m, axis=0, keepdims=True)
               + jnp.exp(sink - m))
        inv = 1.0 / den
        p = jnp.concatenate([p_w * inv, p_m * inv], axis=0).astype(BF16)
        hs = slice(g * HEAD_DIM, (g + 1) * HEAD_DIM)
        vt_g = jnp.concatenate([v_t[hs], vmt_ref[hs, :]], axis=1).astype(BF16)
        o_t = jnp.dot(vt_g, p, preferred_element_type=F32)
        for pair in range(2):
            two = jnp.concatenate([o_t[:, (2 * pair) * BLOCK:(2 * pair + 1) * BLOCK],
                                   o_t[:, (2 * pair + 1) * BLOCK:(2 * pair + 2) * BLOCK]], axis=0)
            o_ref[:, (2 * g + pair) * LANES:(2 * g + pair + 1) * LANES] = two.T


def _prompt_attention(sinks, h_p, h_s, vm_t, batch, seq):
    nb = seq // BLOCK
    kvb = COL_KV // (2 * KV_WIDTH)
    kj = np.arange(2 * BLOCK)[:, None]
    qi = np.arange(4 * BLOCK)[None, :] % BLOCK
    cur = (kj >= BLOCK) & (kj - BLOCK <= qi)
    prev = (kj < BLOCK) & (kj > qi)
    bias = jnp.asarray(np.where(np.stack([cur, cur | prev]), 0.0, NEG), F32)
    return pl.pallas_call(
        _prompt_attn_kernel,
        grid=(batch, nb),
        in_specs=[
            pl.BlockSpec(memory_space=pltpu.SMEM),
            pl.BlockSpec((BLOCK, ATTN_WIDTH), lambda b, i: (b * nb + i, 0)),
            pl.BlockSpec((BLOCK, 2 * KV_WIDTH), lambda b, i: (b * nb + i, kvb)),
            pl.BlockSpec((BLOCK, 2 * KV_WIDTH),
                         lambda b, i: (b * nb + jnp.maximum(i - 1, 0), kvb)),
            pl.BlockSpec((N_META, 2 * KV_WIDTH), lambda b, i: (h_s.shape[0] // N_META - 1, kvb)),
            pl.BlockSpec((KV_WIDTH, N_META), lambda b, i: (0, 0)),
            pl.BlockSpec((1, 2 * BLOCK, 4 * BLOCK), lambda b, i: (jnp.minimum(i, 1), 0, 0)),
        ],
        out_specs=pl.BlockSpec((BLOCK, ATTN_WIDTH), lambda b, i: (b * nb + i, 0)),
        out_shape=jax.ShapeDtypeStruct((batch * seq, ATTN_WIDTH), F32),
        compiler_params=pltpu.CompilerParams(
            dimension_semantics=("arbitrary", "arbitrary"), vmem_limit_bytes=VMEM_LIMIT),
        name="prompt_attn",
    )(sinks, h_p, h_p, h_p, h_s, vm_t, bias)


def _sample_attn_kernel(qp_ref, kn_ref, vn_ref, ck_ref, cv_ref, mk_ref, mv_ref, sink_ref,
                        o_ref, ok_ref, ov_ref):
    bt = ck_ref.shape[0]
    last = lax.broadcasted_iota(jnp.int32, (1, WINDOW), 1) == WINDOW - 1
    kn = kn_ref[0]
    vn = vn_ref[0]
    for b in range(bt):
        ok_ref[b] = jnp.where(last, kn[:, b:b + 1], pltpu.roll(ck_ref[b], WINDOW - 1, 1))
        ov_ref[b] = jnp.where(last, vn[:, b:b + 1], pltpu.roll(cv_ref[b], WINDOW - 1, 1))
    qp = qp_ref[...].astype(BF16)
    k_w = ok_ref[...].astype(BF16)
    v_w = ov_ref[...].astype(BF16)
    k_m = mk_ref[...].astype(BF16)
    v_m = mv_ref[...].astype(BF16)
    s_w = jnp.einsum("bhc,bcw->bhw", qp, k_w, preferred_element_type=F32)
    s_m = jnp.einsum("bhc,bkc->bhk", qp, k_m, preferred_element_type=F32)
    sink = sink_ref[...][None]
    m = jnp.maximum(jnp.maximum(jnp.max(s_w, axis=-1, keepdims=True),
                                jnp.max(s_m, axis=-1, keepdims=True)), sink)
    p_w = jnp.exp(s_w - m)
    p_m = jnp.exp(s_m - m)
    den = (jnp.sum(p_w, axis=-1, keepdims=True) + jnp.sum(p_m, axis=-1, keepdims=True)
           + jnp.exp(sink - m))
    inv = 1.0 / den
    o = (jnp.einsum("bhw,bcw->bhc", (p_w * inv).astype(BF16), v_w, preferred_element_type=F32)
         + jnp.einsum("bhk,bkc->bhc", (p_m * inv).astype(BF16), v_m,
                      preferred_element_type=F32))
    hh = lax.broadcasted_iota(jnp.int32, (1, N_Q_HEADS, KV_WIDTH), 1) // (N_Q_HEADS // N_KV_HEADS)
    cc = lax.broadcasted_iota(jnp.int32, (1, N_Q_HEADS, KV_WIDTH), 2) // HEAD_DIM
    o = jnp.where(hh == cc, o, 0.0)
    o = o[:, :, :LANES] + o[:, :, LANES:]
    o_ref[...] = o + pltpu.roll(o, HEAD_DIM, 2)


def _sample_attention(qp, kn_t, vn_t, cache_k, cache_v, meta_k, meta_v, sink_col, bt):
    n = cache_k.shape[0]
    win_spec = pl.BlockSpec((bt, KV_WIDTH, WINDOW), lambda t: (t, 0, 0))
    new_spec = pl.BlockSpec((1, KV_WIDTH, bt), lambda t: (t, 0, 0))
    meta_spec = pl.BlockSpec((bt, N_META, KV_WIDTH), lambda t: (t, 0, 0))
    return pl.pallas_call(
        _sample_attn_kernel,
        grid=(n // bt,),
        in_specs=[
            pl.BlockSpec((bt, N_Q_HEADS, KV_WIDTH), lambda t: (t, 0, 0)),
            new_spec, new_spec, win_spec, win_spec, meta_spec, meta_spec,
            pl.BlockSpec((N_Q_HEADS, 1), lambda t: (0, 0)),
        ],
        out_specs=[
            pl.BlockSpec((bt, N_Q_HEADS, LANES), lambda t: (t, 0, 0)),
            win_spec, win_spec,
        ],
        out_shape=[
            jax.ShapeDtypeStruct((n, N_Q_HEADS, LANES), F32),
            jax.ShapeDtypeStruct(cache_k.shape, F32),
            jax.ShapeDtypeStruct(cache_v.shape, F32),
        ],
        compiler_params=pltpu.CompilerParams(
            dimension_semantics=("arbitrary",), vmem_limit_bytes=VMEM_LIMIT),
        name="sample_attn",
    )(qp, kn_t, vn_t, cache_k, cache_v, meta_k, meta_v, sink_col)


def _gelu_tanh(x):
    return 0.5 * x * (1.0 + jnp.tanh(0.7978845608028654 * (x + 0.044715 * (x * x * x))))


def _prompt_ssm_kernel(u_ref, um_ref, t8_ref, win_ref, wst_ref, ar_ref, ai_ref, d_ref,
                       y_ref, sre_ref, sim_ref):
    batch = sre_ref.shape[0]
    n_rows = u_ref.shape[0] // CHUNK
    nc = n_rows // batch
    ucat = jnp.concatenate([u_ref[pl.ds(j, n_rows, stride=CHUNK), :] for j in range(CHUNK)],
                           axis=1)
    ub = ucat.astype(BF16)
    win = win_ref[0]
    g = jnp.dot(ub, win, preferred_element_type=F32)

    n_mc = um_ref.shape[0] // CHUNK
    um = jnp.concatenate([um_ref[pl.ds(j, n_mc, stride=CHUNK), :] for j in range(CHUNK)], axis=1)
    gm = jnp.dot(um.astype(BF16), win, preferred_element_type=F32)
    a_r, a_i = ar_ref[0, 0:1, :], ai_ref[0, 0:1, :]
    s0r, s0i = gm[0:1, :STATE_TILE], gm[0:1, STATE_TILE:]
    for c in range(1, n_mc):
        s0r, s0i = (a_r * s0r - a_i * s0i + gm[c:c + 1, :STATE_TILE],
                    a_r * s0i + a_i * s0r + gm[c:c + 1, STATE_TILE:])

    row = lax.broadcasted_iota(jnp.int32, (nc, 1), 0)
    first = row == 0
    prev_r, prev_i = [], []
    for b in range(batch):
        gr = g[b * nc:(b + 1) * nc, :STATE_TILE]
        gi = g[b * nc:(b + 1) * nc, STATE_TILE:]
        gr = gr + jnp.where(first, a_r * s0r - a_i * s0i, 0.0)
        gi = gi + jnp.where(first, a_r * s0i + a_i * s0r, 0.0)
        for k in range(SCAN_LEVELS):
            d = 1 << k
            if d >= nc:
                break
            k_r, k_i = ar_ref[0, k:k + 1, :], ai_ref[0, k:k + 1, :]
            if d < CHUNK:
                keep = row >= d
                sh_r = jnp.where(keep, pltpu.roll(gr, d, 0), 0.0)
                sh_i = jnp.where(keep, pltpu.roll(gi, d, 0), 0.0)
            else:
                zero = jnp.zeros((d, STATE_TILE), F32)
                sh_r = jnp.concatenate([zero, gr[:nc - d]], axis=0)
                sh_i = jnp.concatenate([zero, gi[:nc - d]], axis=0)
            gr, gi = gr + (k_r * sh_r - k_i * sh_i), gi + (k_r * sh_i + k_i * sh_r)
        sre_ref[b, 0] = gr[nc - 1:nc, :]
        sim_ref[b, 0] = gi[nc - 1:nc, :]
        prev_r.append(jnp.where(first, s0r, pltpu.roll(gr, 1, 0)))
        prev_i.append(jnp.where(first, s0i, pltpu.roll(gi, 1, 0)))
    sp = jnp.concatenate([jnp.concatenate(prev_r, axis=0), jnp.concatenate(prev_i, axis=0)],
                         axis=1).astype(BF16)
    y2 = (jnp.dot(ub, t8_ref[0], preferred_element_type=F32)
          + lax.dot_general(sp, wst_ref[0], NT_DIMS, preferred_element_type=F32))
    dsk = d_ref[...]
    for j in range(CHUNK):
        sl = slice(j * LANES, (j + 1) * LANES)
        y_ref[pl.ds(j, n_rows, stride=CHUNK), :] = _gelu_tanh(y2[:, sl] + dsk * ucat[:, sl])


def _prompt_ssm(h_p, h_s, t8, win, wst, a_r, a_i, d_skip, batch):
    ub = COL_U // LANES
    rows = h_p.shape[0]
    wspec = pl.BlockSpec((1, CHUNK_WIDTH, CHUNK_WIDTH), lambda gt: (gt, 0, 0))
    aspec = pl.BlockSpec((1, SCAN_LEVELS, STATE_TILE), lambda gt: (gt, 0, 0))
    sspec = pl.BlockSpec((batch, 1, 1, STATE_TILE), lambda gt: (0, gt, 0, 0))
    return pl.pallas_call(
        _prompt_ssm_kernel,
        grid=(N_GROUP_TILES,),
        in_specs=[
            pl.BlockSpec((rows, LANES), lambda gt: (0, ub + gt)),
            pl.BlockSpec((N_META, LANES), lambda gt: (h_s.shape[0] // N_META - 1, ub + gt)),
            wspec, wspec, wspec, aspec, aspec,
            pl.BlockSpec((1, LANES), lambda gt: (0, gt)),
        ],
        out_specs=[pl.BlockSpec((rows, LANES), lambda gt: (0, gt)), sspec, sspec],
        out_shape=[
            jax.ShapeDtypeStruct((rows, SSM_WIDTH), F32),
            jax.ShapeDtypeStruct((batch, N_GROUP_TILES, 1, STATE_TILE), F32),
            jax.ShapeDtypeStruct((batch, N_GROUP_TILES, 1, STATE_TILE), F32),
        ],
        compiler_params=pltpu.CompilerParams(
            dimension_semantics=("arbitrary",), vmem_limit_bytes=VMEM_LIMIT),
        name="prompt_ssm",
    )(h_p, h_s, t8, win, wst, a_r, a_i, d_skip)


def _sample_ssm_kernel(u_ref, hr_ref, hi_ref, b_ref, wc_ref, lr_ref, li_ref, d_ref,
                       y_ref, sre_ref, sim_ref):
    u = u_ref[...]
    bu = jnp.dot(u.astype(BF16), b_ref[0], preferred_element_type=F32)
    l_r, l_i = lr_ref[0], li_ref[0]
    h_r, h_i = hr_ref[...], hi_ref[...]
    n_r = l_r * h_r - l_i * h_i + bu[:, :STATE_TILE]
    n_i = l_r * h_i + l_i * h_r + bu[:, STATE_TILE:]
    sre_ref[...] = n_r
    sim_ref[...] = n_i
    hn = jnp.concatenate([n_r, n_i], axis=1).astype(BF16)
    y = lax.dot_general(hn, wc_ref[0], NT_DIMS, preferred_element_type=F32) + d_ref[...] * u
    y_ref[...] = _gelu_tanh(y)


def _sample_ssm(h_s, h0r, h0i, win, wc, l_r, l_i, d_skip, n):
    ub = COL_U // LANES
    st = pl.BlockSpec((n, STATE_TILE), lambda gt: (0, gt))
    lam = pl.BlockSpec((1, 1, STATE_TILE), lambda gt: (gt, 0, 0))
    return pl.pallas_call(
        _sample_ssm_kernel,
        grid=(N_GROUP_TILES,),
        in_specs=[
            pl.BlockSpec((n, LANES), lambda gt: (0, ub + gt)),
            st, st,
            pl.BlockSpec((1, LANES, 2 * STATE_TILE), lambda gt: (gt, CHUNK - 1, 0)),
            pl.BlockSpec((1, LANES, 2 * STATE_TILE), lambda gt: (gt, 0, 0)),
            lam, lam,
            pl.BlockSpec((1, LANES), lambda gt: (0, gt)),
        ],
        out_specs=[pl.BlockSpec((n, LANES), lambda gt: (0, gt)), st, st],
        out_shape=[
            jax.ShapeDtypeStruct((n, SSM_WIDTH), F32),
            jax.ShapeDtypeStruct((n, N_SSM_GROUPS * SSM_STATE), F32),
            jax.ShapeDtypeStruct((n, N_SSM_GROUPS * SSM_STATE), F32),
        ],
        compiler_params=pltpu.CompilerParams(
            dimension_semantics=("arbitrary",), vmem_limit_bytes=VMEM_LIMIT),
        name="sample_ssm",
    )(h_s, h0r, h0i, win, wc, l_r, l_i, d_skip)


def _merge_kernel(o_ref, za_ref, zs_ref, ga_ref, gs_ref, y_ref, x_ref, bglu_ref,
                  wglu_ref, wa_ref, ws_ref, wo_ref, out_ref):
    y = y_ref[...]
    t = jnp.dot(y.astype(BF16), wglu_ref[...], preferred_element_type=F32) + bglu_ref[...]
    y = y * jax.nn.sigmoid(t)
    za = za_ref[...]
    zs = zs_ref[...]
    a_in = (o_ref[...] * (za * jax.nn.sigmoid(za))).astype(BF16)
    s_in = (y * (zs * jax.nn.sigmoid(zs))).astype(BF16)
    br_a = jnp.dot(a_in, wa_ref[...], preferred_element_type=F32)
    br_s = jnp.dot(s_in, ws_ref[...], preferred_element_type=F32)
    mix = (jax.nn.sigmoid(ga_ref[...]) * br_a + jax.nn.sigmoid(gs_ref[...]) * br_s).astype(BF16)
    out_ref[...] = x_ref[...] + jnp.dot(mix, wo_ref[...], preferred_element_type=F32)


def _merge(o, h, y, x, b_glu, w_glu, w_a, w_s, w_o, tm):
    m = x.shape[0]
    const = lambda shape: pl.BlockSpec(shape, lambda i: (0, 0), pipeline_mode=pl.Buffered(1))
    return pl.pallas_call(
        _merge_kernel,
        grid=(m // tm,),
        in_specs=[
            pl.BlockSpec((tm, ATTN_WIDTH), lambda i: (i, 0)),
            pl.BlockSpec((tm, ATTN_WIDTH), lambda i: (i, COL_ZA // ATTN_WIDTH)),
            pl.BlockSpec((tm, SSM_WIDTH), lambda i: (i, COL_ZS // SSM_WIDTH)),
            pl.BlockSpec((tm, D_MODEL), lambda i: (i, COL_GA // D_MODEL)),
            pl.BlockSpec((tm, D_MODEL), lambda i: (i, COL_GS // D_MODEL)),
            pl.BlockSpec((tm, SSM_WIDTH), lambda i: (i, 0)),
            pl.BlockSpec((tm, D_MODEL), lambda i: (i, 0)),
            const((1, SSM_WIDTH)),
            const((SSM_WIDTH, SSM_WIDTH)),
            const((ATTN_WIDTH, D_MODEL)),
            const((SSM_WIDTH, D_MODEL)),
            const((D_MODEL, D_MODEL)),
        ],
        out_specs=pl.BlockSpec((tm, D_MODEL), lambda i: (i, 0)),
        out_shape=jax.ShapeDtypeStruct((m, D_MODEL), F32),
        compiler_params=pltpu.CompilerParams(
            dimension_semantics=("arbitrary",), vmem_limit_bytes=VMEM_LIMIT),
        name="merge",
    )(o, h, h, h, h, y, x, b_glu, w_glu, w_a, w_s, w_o)


def _rope_tables(pos):
    half = HEAD_DIM // 2
    inv_freq = (ROPE_THETA ** (-np.arange(half, dtype=np.float32) / half)).astype(np.float32)
    ang = np.asarray(pos, np.float32)[:, None] * inv_freq[None, :]
    cos, sin = np.cos(ang).astype(np.float32), np.sin(ang).astype(np.float32)
    zero = np.zeros_like(sin)
    reps = LANES // HEAD_DIM
    cos_t = np.tile(np.concatenate([cos, cos], axis=1), (1, reps))
    sin_a = np.tile(np.concatenate([-sin, zero], axis=1), (1, reps))
    sin_b = np.tile(np.concatenate([zero, sin], axis=1), (1, reps))
    return jnp.asarray(cos_t), jnp.asarray(sin_a), jnp.asarray(sin_b)


def _split_bf16(x):
    hi = x.astype(BF16)
    return hi, (x - hi.astype(F32)).astype(BF16)


def _dot_nt_f32(a, b):
    ah, al = _split_bf16(a)
    bh, bl = _split_bf16(b)
    dot = lambda x, y: lax.dot_general(x, y, NT_DIMS, preferred_element_type=F32)
    return dot(ah, bh) + dot(ah, bl) + dot(al, bh)


def _ssm_tables_kernel(are_ref, aim_ref, ldt_ref, bre_ref, bim_ref, cre_ref, cim_ref,
                       win_ref, wst_ref, t8_ref, wc_ref, ar_ref, ai_ref, lr_ref, li_ref):
    a_r, a_i = are_ref[0], aim_ref[0]
    dt = jnp.exp(ldt_ref[0])
    mag = jnp.exp(a_r * dt)
    l_r, l_i = mag * jnp.cos(a_i * dt), mag * jnp.sin(a_i * dt)
    den = a_r * a_r + a_i * a_i
    f_r = ((l_r - 1.0) * a_r + l_i * a_i) / den
    f_i = (l_i * a_r - (l_r - 1.0) * a_i) / den
    shape = (LANES, STATE_TILE)
    own = (lax.broadcasted_iota(jnp.int32, shape, 0) // SSM_GROUP
           == lax.broadcasted_iota(jnp.int32, shape, 1) // SSM_STATE)
    sq = (LANES, LANES)
    same_group = (lax.broadcasted_iota(jnp.int32, sq, 0) // SSM_GROUP
                  == lax.broadcasted_iota(jnp.int32, sq, 1) // SSM_GROUP)
    lo = lax.broadcasted_iota(jnp.int32, (1, LANES), 1) < SSM_STATE
    n_rep = STATE_TILE // LANES

    def expand(x):
        return jnp.where(own, jnp.concatenate([x] * n_rep, axis=1), 0.0)

    def compact(v):
        e = jnp.where(own, jnp.broadcast_to(v, shape), 0.0)
        s = e[:, :LANES]
        for k in range(1, n_rep):
            s = s + e[:, k * LANES:(k + 1) * LANES]
        return s + pltpu.roll(s, SSM_STATE, 1)

    lc_r, lc_i = compact(l_r), compact(l_i)
    b_r, b_i = bre_ref[0], bim_ref[0]
    c_r, c_i = cre_ref[0], cim_ref[0]
    bb_r, bb_i = _cmul(compact(f_r), compact(f_i), b_r, b_i)
    wc_ref[0] = jnp.concatenate([expand(c_r), expand(-c_i)], axis=1).astype(BF16)
    c_cat = jnp.where(lo, c_r, -c_i)
    q_r, q_i = jnp.ones(sq, F32), jnp.zeros(sq, F32)
    kern = []
    for n in range(CHUNK):
        x_r, x_i = _cmul(q_r, q_i, bb_r, bb_i)
        win_ref[0, (CHUNK - 1 - n) * LANES:(CHUNK - n) * LANES, :] = jnp.concatenate(
            [expand(x_r), expand(x_i)], axis=1).astype(BF16)
        k_n = _dot_nt_f32(jnp.where(lo, x_r, x_i), c_cat)
        kern.append(jnp.where(same_group, k_n, 0.0).astype(BF16))
        q_r, q_i = _cmul(q_r, q_i, lc_r, lc_i)
        s_r, s_i = _cmul(c_r, c_i, q_r, q_i)
        wst_ref[0, n * LANES:(n + 1) * LANES, :] = jnp.concatenate(
            [expand(s_r), expand(-s_i)], axis=1).astype(BF16)
    zero = jnp.zeros(sq, BF16)
    for j in range(CHUNK):
        for jp in range(CHUNK):
            t8_ref[0, j * LANES:(j + 1) * LANES, jp * LANES:(jp + 1) * LANES] = (
                kern[jp - j] if jp >= j else zero)
    p_r, p_i = l_r, l_i
    for _ in range(CHUNK.bit_length() - 1):
        p_r, p_i = _cmul(p_r, p_i, p_r, p_i)
    for k in range(SCAN_LEVELS):
        ar_ref[0, k:k + 1, :] = p_r
        ai_ref[0, k:k + 1, :] = p_i
        p_r, p_i = _cmul(p_r, p_i, p_r, p_i)
    lr_ref[0] = l_r
    li_ref[0] = l_i


def _ssm_tables(a_re, a_im, log_dt, b_re, b_im, c_re, c_im):
    gt = N_GROUP_TILES
    lane_vec = lambda v: v.reshape(gt, 1, STATE_TILE)
    dup = lambda v: jnp.concatenate([v, v], axis=-1)
    rows = GROUPS_PER_TILE * SSM_GROUP
    b_t = lambda v: dup(v.transpose(0, 2, 1).reshape(gt, rows, SSM_STATE))
    c_t = lambda v: dup(v.reshape(gt, rows, SSM_STATE))
    vec = pl.BlockSpec((1, 1, STATE_TILE), lambda g: (g, 0, 0))
    par = pl.BlockSpec((1, rows, LANES), lambda g: (g, 0, 0))
    big = pl.BlockSpec((1, CHUNK_WIDTH, CHUNK_WIDTH), lambda g: (g, 0, 0))
    wcs = pl.BlockSpec((1, LANES, 2 * STATE_TILE), lambda g: (g, 0, 0))
    scan = pl.BlockSpec((1, SCAN_LEVELS, STATE_TILE), lambda g: (g, 0, 0))
    return pl.pallas_call(
        _ssm_tables_kernel,
        grid=(gt,),
        in_specs=[vec, vec, vec, par, par, par, par],
        out_specs=[big, big, big, wcs, scan, scan, vec, vec],
        out_shape=[
            jax.ShapeDtypeStruct((gt, CHUNK_WIDTH, 2 * STATE_TILE), BF16),
            jax.ShapeDtypeStruct((gt, CHUNK_WIDTH, 2 * STATE_TILE), BF16),
            jax.ShapeDtypeStruct((gt, CHUNK_WIDTH, CHUNK_WIDTH), BF16),
            jax.ShapeDtypeStruct((gt, LANES, 2 * STATE_TILE), BF16),
            jax.ShapeDtypeStruct((gt, SCAN_LEVELS, STATE_TILE), F32),
            jax.ShapeDtypeStruct((gt, SCAN_LEVELS, STATE_TILE), F32),
            jax.ShapeDtypeStruct((gt, 1, STATE_TILE), F32),
            jax.ShapeDtypeStruct((gt, 1, STATE_TILE), F32),
        ],
        compiler_params=pltpu.CompilerParams(
            dimension_semantics=("arbitrary",), vmem_limit_bytes=VMEM_LIMIT),
        name="ssm_tables",
    )(lane_vec(a_re), lane_vec(a_im), lane_vec(jnp.repeat(log_dt, SSM_STATE)),
      b_t(b_re), b_t(b_im), c_t(c_re), c_t(c_im))


def kernel(x_prompt, x_sample, cache_win_k, cache_win_v, cache_meta_k, cache_meta_v,
           state_ssm_re, state_ssm_im, meta_tokens, norm_gain, w_in, q_norm_gain, k_norm_gain,
           sinks, a_re, a_im, log_dt, b_re, b_im, c_re, c_im, d_skip, w_glu, b_glu,
           w_attn_out, w_ssm_out, w_out):
    depth = w_in.shape[0]
    assert depth == 1, "single-layer trunk"
    batch, seq = x_prompt.shape[:2]
    n_s = x_sample.shape[0]
    assert x_sample.shape[1] == 1 and seq % (CHUNK * BLOCK) == 0
    l = 0

    w = w_in[l]
    qg = jnp.tile(q_norm_gain[l], TN // HEAD_DIM)[None]
    kg = jnp.tile(k_norm_gain[l], KV_WIDTH // HEAD_DIM)[None]
    lane = np.arange(LANES)
    e128 = np.where((lane[:, None] // HEAD_DIM) == (lane[None, :] // HEAD_DIM),
                    1.0 / HEAD_DIM, 0.0).astype(np.float32)
    e = jnp.asarray(np.concatenate([e128, e128], axis=0), BF16)
    gain = norm_gain[l][None]

    pos_p = N_META + np.arange(seq)
    pos_small = np.concatenate([np.full((n_s,), PAST_LEN), np.arange(N_META)])
    x_small = jnp.concatenate([x_sample[:, 0, :], meta_tokens.astype(x_prompt.dtype)], axis=0)
    xp = x_prompt.reshape(batch * seq, D_MODEL)
    xn_p = _rmsnorm(xp, gain, tm=512)
    xn_s = _rmsnorm(x_small, gain, tm=n_s + N_META)
    h_p, h_s = _inproj(xn_p, xn_s, w, qg, kg, e, _rope_tables(pos_p), _rope_tables(pos_small),
                       tm=1024, halves=2)

    win, wst, t8, wc, a_r, a_i, l_r, l_i = _ssm_tables(
        a_re[l], a_im[l], log_dt[l], b_re[l], b_im[l], c_re[l], c_im[l])
    dsk = d_skip[l][None]

    vm_t = h_s[n_s:, COL_KV + KV_WIDTH:IN_WIDTH].T
    o_p = _prompt_attention(sinks[l], h_p, h_s, vm_t, batch, seq)
    y_p, p_re, p_im = _prompt_ssm(h_p, h_s, t8, win, wst, a_r, a_i, dsk, batch)
    w_glu_bf = w_glu[l].astype(BF16)
    w_a_bf = w_attn_out[l].astype(BF16)
    w_s_bf = w_ssm_out[l].astype(BF16)
    w_o_bf = w_out[l].astype(BF16)
    bglu = b_glu[l][None]
    y_prompt = _merge(o_p, h_p, y_p, xp, bglu,
                      w_glu_bf, w_a_bf, w_s_bf, w_o_bf, tm=256).reshape(batch, seq, D_MODEL)

    q_s = h_s[:n_s, COL_Q:COL_Q + ATTN_WIDTH].reshape(n_s, N_Q_HEADS, 1, HEAD_DIM)
    sel = (np.arange(N_Q_HEADS)[:, None] // (N_Q_HEADS // N_KV_HEADS)
           == np.arange(N_KV_HEADS)[None, :]).astype(np.float32)
    qp = (q_s * sel[None, :, :, None]).reshape(n_s, N_Q_HEADS, KV_WIDTH)
    bt = 8
    to_t = lambda c: c.transpose(0, 2, 3, 1).reshape(n_s, KV_WIDTH, WINDOW)
    from_t = lambda c: c.reshape(n_s, N_KV_HEADS, HEAD_DIM, WINDOW).transpose(0, 3, 1, 2)[None]
    new_t = lambda c0: h_s[:n_s, c0:c0 + KV_WIDTH].reshape(n_s // bt, bt, KV_WIDTH).transpose(0, 2, 1)
    o_s, s_win_k, s_win_v = _sample_attention(
        qp, new_t(COL_KV), new_t(COL_KV + KV_WIDTH), to_t(cache_win_k[l]), to_t(cache_win_v[l]),
        cache_meta_k[l].reshape(n_s, N_META, KV_WIDTH), cache_meta_v[l].reshape(n_s, N_META, KV_WIDTH),
        sinks[l][:, None], bt=bt)
    o_s = o_s[:, :, :HEAD_DIM].reshape(n_s, ATTN_WIDTH)
    y_s, s_re, s_im = _sample_ssm(
        h_s, state_ssm_re[l].reshape(n_s, -1), state_ssm_im[l].reshape(n_s, -1),
        win, wc, l_r, l_i, dsk, n_s)
    y_sample = _merge(o_s, h_s, y_s, x_sample[:, 0, :], bglu,
                      w_glu_bf, w_a_bf, w_s_bf, w_o_bf, tm=n_s).reshape(n_s, 1, D_MODEL)

    kv_p = h_p.reshape(batch, seq, IN_WIDTH)[:, seq - WINDOW:, COL_KV:]
    kv_p = kv_p.reshape(batch, WINDOW, 2, N_KV_HEADS, HEAD_DIM)
    p_win_k = kv_p[:, :, 0][None]
    p_win_v = kv_p[:, :, 1][None]
    kv_m = h_s[n_s:, COL_KV:IN_WIDTH].reshape(N_META, 2, N_KV_HEADS, HEAD_DIM)
    p_meta_k = jnp.broadcast_to(kv_m[None, :, 0], (batch, N_META, N_KV_HEADS, HEAD_DIM))[None]
    p_meta_v = jnp.broadcast_to(kv_m[None, :, 1], (batch, N_META, N_KV_HEADS, HEAD_DIM))[None]
    p_ssm_re = p_re.reshape(batch, N_SSM_GROUPS, SSM_STATE)[None]
    p_ssm_im = p_im.reshape(batch, N_SSM_GROUPS, SSM_STATE)[None]
    st_shape = (1, n_s, N_SSM_GROUPS, SSM_STATE)
    return (y_prompt, y_sample, p_win_k, p_win_v, p_meta_k, p_meta_v, p_ssm_re, p_ssm_im,
            from_t(s_win_k), from_t(s_win_v), s_re.reshape(st_shape), s_im.reshape(st_shape))
```

```python
import functools

import jax
import jax.numpy as jnp
import numpy as np
from jax import lax
from jax.experimental import pallas as pl
from jax.experimental.pallas import tpu as pltpu

F32 = jnp.float32
BF16 = jnp.bfloat16

D_MODEL = 2048
N_META = 16
HEAD_DIM = 64
N_Q_HEADS = 16
N_KV_HEADS = 4
ATTN_WIDTH = N_Q_HEADS * HEAD_DIM
KV_WIDTH = N_KV_HEADS * HEAD_DIM
WINDOW = 128
BLOCK = 128
ROPE_THETA = 10000.0
SSM_WIDTH = D_MODEL // 2
SSM_GROUP = 16
N_SSM_GROUPS = SSM_WIDTH // SSM_GROUP
SSM_STATE = 64
EPS = 1e-6
PAST_LEN = 8192

LANES = 128
CHUNK = 8
GROUPS_PER_TILE = LANES // SSM_GROUP
N_GROUP_TILES = N_SSM_GROUPS // GROUPS_PER_TILE
CHUNK_WIDTH = CHUNK * LANES
STATE_TILE = GROUPS_PER_TILE * SSM_STATE
SCAN_LEVELS = 8

COL_Q = 0
COL_ZA = 1024
COL_U = 2048
COL_ZS = 3072
COL_GA = 4096
COL_GS = 6144
COL_KV = 8192
IN_WIDTH = 8704
TN = 512
KV_TILE = COL_KV // TN
VMEM_LIMIT = 56 * 1024 * 1024
NEG = -1e30
NT_DIMS = (((1,), (1,)), ((), ()))


def _cmul(ar, ai, br, bi):
    return ar * br - ai * bi, ar * bi + ai * br


def _rmsnorm_kernel(x_ref, gain_ref, o_ref):
    x = x_ref[...]
    r = lax.rsqrt(jnp.mean(x * x, axis=-1, keepdims=True) + EPS)
    o_ref[...] = (x * r * gain_ref[...]).astype(BF16)


def _rmsnorm(x, gain, tm):
    m = x.shape[0]
    return pl.pallas_call(
        _rmsnorm_kernel,
        grid=(m // tm,),
        in_specs=[pl.BlockSpec((tm, D_MODEL), lambda i: (i, 0)),
                  pl.BlockSpec((1, D_MODEL), lambda i: (0, 0))],
        out_specs=pl.BlockSpec((tm, D_MODEL), lambda i: (i, 0)),
        out_shape=jax.ShapeDtypeStruct((m, D_MODEL), BF16),
        compiler_params=pltpu.CompilerParams(
            dimension_semantics=("arbitrary",), vmem_limit_bytes=VMEM_LIMIT),
        name="rmsnorm",
    )(x, gain)


def _inproj_epilogue(j, o_ref, qg_ref, kg_ref, e_ref, cos_ref, sa_ref, sb_ref):
    def head_norm_rope(n_chunks, gain_ref, scale):
        cos, sa, sb = cos_ref[...], sa_ref[...], sb_ref[...]
        for c in range(n_chunks):
            cs = slice(c * LANES, (c + 1) * LANES)
            ac = o_ref[:, cs]
            sq = ac * ac
            hi = sq.astype(BF16)
            lo = (sq - hi.astype(F32)).astype(BF16)
            ms = jnp.dot(jnp.concatenate([hi, lo], axis=1), e_ref[...],
                         preferred_element_type=F32)
            xc = ac * lax.rsqrt(ms + EPS) * gain_ref[:, cs]
            o_ref[:, cs] = (xc * cos + pltpu.roll(xc, LANES - HEAD_DIM // 2, 1) * sa
                            + pltpu.roll(xc, HEAD_DIM // 2, 1) * sb) * scale

    @pl.when(j < COL_ZA // TN)
    def _():
        head_norm_rope(TN // LANES, qg_ref, HEAD_DIM ** -0.5)

    @pl.when(j == KV_TILE)
    def _():
        head_norm_rope(KV_WIDTH // LANES, kg_ref, 1.0)


def _inproj_kernel(x_ref, gain_ref, xs_ref, w_ref, qg_ref, kg_ref, e_ref, cos_ref, sa_ref, sb_ref,
                   cos_s_ref, sa_s_ref, sb_s_ref, o_ref, os_ref, xn_ref, wbf_ref):
    h, j, i = pl.program_id(0), pl.program_id(1), pl.program_id(2)
    tm = o_ref.shape[0]
    rows = pl.ds(pl.multiple_of(i * tm, tm), tm)

    @pl.when(j == 0)
    def _():
        x = x_ref[...]
        r = lax.rsqrt(jnp.mean(x * x, axis=-1, keepdims=True) + EPS)
        xn_ref[rows, :] = (x * r * gain_ref[...]).astype(BF16)

    @pl.when(i == 0)
    def _():
        wbf_ref[...] = w_ref[...].astype(BF16)

    o_ref[...] = jnp.dot(xn_ref[rows, :], wbf_ref[...], preferred_element_type=F32)
    _inproj_epilogue(j, o_ref, qg_ref, kg_ref, e_ref, cos_ref, sa_ref, sb_ref)

    @pl.when(jnp.logical_and(h == 0, jnp.logical_and(j == 0, i == 0)))
    def _():
        os_ref[...] = jnp.zeros(os_ref.shape, F32)

    @pl.when(jnp.logical_and(h == pl.num_programs(0) - 1, i == pl.num_programs(2) - 1))
    def _():
        os_ref[...] = jnp.dot(xs_ref[...], wbf_ref[...], preferred_element_type=F32)
        _inproj_epilogue(j, os_ref, qg_ref, kg_ref, e_ref, cos_s_ref, sa_s_ref, sb_s_ref)


def _w_tile(j):
    n_q = COL_ZA // TN
    return jnp.where(j < n_q, j, jnp.where(j == KV_TILE, n_q, j + 1))


def _inproj(x, gain, xs, w, qg, kg, e, tabs_p, tabs_s, tm, halves):
    m, ms = x.shape[0], xs.shape[0]
    rows_half = m // halves
    tiles = rows_half // tm
    n_tab = tabs_p[0].shape[0] // tm
    n_col = IN_WIDTH // TN
    const = lambda shape: pl.BlockSpec(shape, lambda h, j, i: (0, 0))
    tab_p = pl.BlockSpec((tm, LANES), lambda h, j, i: ((h * tiles + i) % n_tab, 0))
    return pl.pallas_call(
        _inproj_kernel,
        grid=(halves, IN_WIDTH // TN, tiles),
        in_specs=[
            pl.BlockSpec((tm, D_MODEL),
                         lambda h, j, i: (h * tiles + jnp.where(j == 0, i, tiles - 1), 0)),
            const((1, D_MODEL)),
            const((ms, D_MODEL)),
            pl.BlockSpec((D_MODEL, TN), lambda h, j, i: (0, _w_tile(j))),
            const((1, TN)), const((1, KV_WIDTH)), const((2 * LANES, LANES)),
            tab_p, tab_p, tab_p,
            const((ms, LANES)), const((ms, LANES)), const((ms, LANES)),
        ],
        out_specs=[pl.BlockSpec((tm, TN), lambda h, j, i: (h * tiles + i, j)),
                   pl.BlockSpec((ms, TN),
                                lambda h, j, i: (0, jnp.where(h == halves - 1, j, n_col)))],
        out_shape=[jax.ShapeDtypeStruct((m, IN_WIDTH), F32),
                   jax.ShapeDtypeStruct((ms, IN_WIDTH + TN), F32)],
        scratch_shapes=[pltpu.VMEM((rows_half, D_MODEL), BF16), pltpu.VMEM((D_MODEL, TN), BF16)],
        compiler_params=pltpu.CompilerParams(
            dimension_semantics=("arbitrary", "arbitrary", "arbitrary"),
            vmem_limit_bytes=VMEM_LIMIT),
        name="inproj",
    )(x, gain, xs, w, qg, kg, e, *tabs_p, *tabs_s)


def _dup_head(x, g, lo):
    r = pltpu.roll(x, HEAD_DIM, 1)
    return jnp.where(lo, x, r) if g % 2 == 0 else jnp.where(lo, r, x)


def _prompt_attn_kernel(sinks_ref, q_ref, kvc_ref, kvp_ref, kvm_ref, vmt_ref, bias_ref, o_ref):
    lo = lax.broadcasted_iota(jnp.int32, (1, LANES), 1) < HEAD_DIM
    head_lane = lax.broadcasted_iota(jnp.int32, (1, 4 * BLOCK), 1) // BLOCK
    bias = bias_ref[0]
    v_t = jnp.concatenate([kvp_ref[:, KV_WIDTH:].T, kvc_ref[:, KV_WIDTH:].T], axis=1)
    for g in range(N_KV_HEADS):
        ks = slice((g // 2) * LANES, (g // 2 + 1) * LANES)
        k_all = jnp.concatenate([_dup_head(kvp_ref[:, ks], g, lo), _dup_head(kvc_ref[:, ks], g, lo),
                                 _dup_head(kvm_ref[:, ks], g, lo)], axis=0).astype(BF16)
        qa = q_ref[:, (2 * g) * LANES:(2 * g + 1) * LANES]
        qb = q_ref[:, (2 * g + 1) * LANES:(2 * g + 2) * LANES]
        q4 = jnp.concatenate([jnp.where(lo, qa, 0.0), jnp.where(lo, 0.0, qa),
                              jnp.where(lo, qb, 0.0), jnp.where(lo, 0.0, qb)],
                             axis=0).astype(BF16)
        s = lax.dot_general(k_all, q4, NT_DIMS, preferred_element_type=F32)
        s_w = s[:2 * BLOCK] + bias
        s_m = s[2 * BLOCK:]
        sink = jnp.where(head_lane == 0, sinks_ref[4 * g],
                         jnp.where(head_lane == 1, sinks_ref[4 * g + 1],
                                   jnp.where(head_lane == 2, sinks_ref[4 * g + 2],
                                             sinks_ref[4 * g + 3])))
        m = jnp.maximum(jnp.maximum(jnp.max(s_w, axis=0, keepdims=True),
                                    jnp.max(s_m, axis=0, keepdims=True)), sink)
        p_w = jnp.exp(s_w - m)
        p_m = jnp.exp(s_m - m)
        den = (jnp.sum(p_w, axis=0, keepdims=True) + jnp.sum(p_m, axis=0, keepdims=True)
               + jnp.exp(sink - m))
        inv = 1.0 / den
        p = jnp.concatenate([p_w * inv, p_m * inv], axis=0).astype(BF16)
        hs = slice(g * HEAD_DIM, (g + 1) * HEAD_DIM)
        vt_g = jnp.concatenate([v_t[hs], vmt_ref[hs, :]], axis=1).astype(BF16)
        o_t = jnp.dot(vt_g, p, preferred_element_type=F32)
        for pair in range(2):
            two = jnp.concatenate([o_t[:, (2 * pair) * BLOCK:(2 * pair + 1) * BLOCK],
                                   o_t[:, (2 * pair + 1) * BLOCK:(2 * pair + 2) * BLOCK]], axis=0)
            o_ref[:, (2 * g + pair) * LANES:(2 * g + pair + 1) * LANES] = two.T


def _prompt_attention(sinks, h_p, h_s, vm_t, batch, seq):
    nb = seq // BLOCK
    kvb = COL_KV // (2 * KV_WIDTH)
    kj = np.arange(2 * BLOCK)[:, None]
    qi = np.arange(4 * BLOCK)[None, :] % BLOCK
    cur = (kj >= BLOCK) & (kj - BLOCK <= qi)
    prev = (kj < BLOCK) & (kj > qi)
    bias = jnp.asarray(np.where(np.stack([cur, cur | prev]), 0.0, NEG), F32)
    return pl.pallas_call(
        _prompt_attn_kernel,
        grid=(batch, nb),
        in_specs=[
            pl.BlockSpec(memory_space=pltpu.SMEM),
            pl.BlockSpec((BLOCK, ATTN_WIDTH), lambda b, i: (b * nb + i, 0)),
            pl.BlockSpec((BLOCK, 2 * KV_WIDTH), lambda b, i: (b * nb + i, kvb)),
            pl.BlockSpec((BLOCK, 2 * KV_WIDTH),
                         lambda b, i: (b * nb + jnp.maximum(i - 1, 0), kvb)),
            pl.BlockSpec((N_META, 2 * KV_WIDTH), lambda b, i: (h_s.shape[0] // N_META - 1, kvb)),
            pl.BlockSpec((KV_WIDTH, N_META), lambda b, i: (0, 0)),
            pl.BlockSpec((1, 2 * BLOCK, 4 * BLOCK), lambda b, i: (jnp.minimum(i, 1), 0, 0)),
        ],
        out_specs=pl.BlockSpec((BLOCK, ATTN_WIDTH), lambda b, i: (b * nb + i, 0)),
        out_shape=jax.ShapeDtypeStruct((batch * seq, ATTN_WIDTH), F32),
        compiler_params=pltpu.CompilerParams(
            dimension_semantics=("arbitrary", "arbitrary"), vmem_limit_bytes=VMEM_LIMIT),
        name="prompt_attn",
    )(sinks, h_p, h_p, h_p, h_s, vm_t, bias)


def _sample_attn_kernel(qp_ref, kn_ref, vn_ref, ck_ref, cv_ref, mk_ref, mv_ref, sink_ref,
                        o_ref, ok_ref, ov_ref):
    bt = ck_ref.shape[0]
    last = lax.broadcasted_iota(jnp.int32, (1, WINDOW), 1) == WINDOW - 1
    kn = kn_ref[0]
    vn = vn_ref[0]
    for b in range(bt):
        ok_ref[b] = jnp.where(last, kn[:, b:b + 1], pltpu.roll(ck_ref[b], WINDOW - 1, 1))
        ov_ref[b] = jnp.where(last, vn[:, b:b + 1], pltpu.roll(cv_ref[b], WINDOW - 1, 1))
    qp = qp_ref[...].astype(BF16)
    k_w = ok_ref[...].astype(BF16)
    v_w = ov_ref[...].astype(BF16)
    k_m = mk_ref[...].astype(BF16)
    v_m = mv_ref[...].astype(BF16)
    s_w = jnp.einsum("bhc,bcw->bhw", qp, k_w, preferred_element_type=F32)
    s_m = jnp.einsum("bhc,bkc->bhk", qp, k_m, preferred_element_type=F32)
    sink = sink_ref[...][None]
    m = jnp.maximum(jnp.maximum(jnp.max(s_w, axis=-1, keepdims=True),
                                jnp.max(s_m, axis=-1, keepdims=True)), sink)
    p_w = jnp.exp(s_w - m)
    p_m = jnp.exp(s_m - m)
    den = (jnp.sum(p_w, axis=-1, keepdims=True) + jnp.sum(p_m, axis=-1, keepdims=True)
           + jnp.exp(sink - m))
    inv = 1.0 / den
    o = (jnp.einsum("bhw,bcw->bhc", (p_w * inv).astype(BF16), v_w, preferred_element_type=F32)
         + jnp.einsum("bhk,bkc->bhc", (p_m * inv).astype(BF16), v_m,
                      preferred_element_type=F32))
    hh = lax.broadcasted_iota(jnp.int32, (1, N_Q_HEADS, KV_WIDTH), 1) // (N_Q_HEADS // N_KV_HEADS)
    cc = lax.broadcasted_iota(jnp.int32, (1, N_Q_HEADS, KV_WIDTH), 2) // HEAD_DIM
    o = jnp.where(hh == cc, o, 0.0)
    o = o[:, :, :LANES] + o[:, :, LANES:]
    o_ref[...] = o + pltpu.roll(o, HEAD_DIM, 2)


def _sample_attention(qp, kn_t, vn_t, cache_k, cache_v, meta_k, meta_v, sink_col, bt):
    n = cache_k.shape[0]
    win_spec = pl.BlockSpec((bt, KV_WIDTH, WINDOW), lambda t: (t, 0, 0))
    new_spec = pl.BlockSpec((1, KV_WIDTH, bt), lambda t: (t, 0, 0))
    meta_spec = pl.BlockSpec((bt, N_META, KV_WIDTH), lambda t: (t, 0, 0))
    return pl.pallas_call(
        _sample_attn_kernel,
        grid=(n // bt,),
        in_specs=[
            pl.BlockSpec((bt, N_Q_HEADS, KV_WIDTH), lambda t: (t, 0, 0)),
            new_spec, new_spec, win_spec, win_spec, meta_spec, meta_spec,
            pl.BlockSpec((N_Q_HEADS, 1), lambda t: (0, 0)),
        ],
        out_specs=[
            pl.BlockSpec((bt, N_Q_HEADS, LANES), lambda t: (t, 0, 0)),
            win_spec, win_spec,
        ],
        out_shape=[
            jax.ShapeDtypeStruct((n, N_Q_HEADS, LANES), F32),
            jax.ShapeDtypeStruct(cache_k.shape, F32),
            jax.ShapeDtypeStruct(cache_v.shape, F32),
        ],
        compiler_params=pltpu.CompilerParams(
            dimension_semantics=("arbitrary",), vmem_limit_bytes=VMEM_LIMIT),
        name="sample_attn",
    )(qp, kn_t, vn_t, cache_k, cache_v, meta_k, meta_v, sink_col)


def _gelu_tanh(x):
    return 0.5 * x * (1.0 + jnp.tanh(0.7978845608028654 * (x + 0.044715 * (x * x * x))))


def _prompt_ssm_kernel(u_ref, um_ref, t8_ref, win_ref, wst_ref, ar_ref, ai_ref, d_ref,
                       y_ref, sre_ref, sim_ref):
    batch = sre_ref.shape[0]
    n_rows = u_ref.shape[0] // CHUNK
    nc = n_rows // batch
    ucat = jnp.concatenate([u_ref[pl.ds(j, n_rows, stride=CHUNK), :] for j in range(CHUNK)],
                           axis=1)
    ub = ucat.astype(BF16)
    win = win_ref[0]
    g = jnp.dot(ub, win, preferred_element_type=F32)

    n_mc = um_ref.shape[0] // CHUNK
    um = jnp.concatenate([um_ref[pl.ds(j, n_mc, stride=CHUNK), :] for j in range(CHUNK)], axis=1)
    gm = jnp.dot(um.astype(BF16), win, preferred_element_type=F32)
    a_r, a_i = ar_ref[0, 0:1, :], ai_ref[0, 0:1, :]
    s0r, s0i = gm[0:1, :STATE_TILE], gm[0:1, STATE_TILE:]
    for c in range(1, n_mc):
        s0r, s0i = (a_r * s0r - a_i * s0i + gm[c:c + 1, :STATE_TILE],
                    a_r * s0i + a_i * s0r + gm[c:c + 1, STATE_TILE:])

    row = lax.broadcasted_iota(jnp.int32, (nc, 1), 0)
    first = row == 0
    prev_r, prev_i = [], []
    for b in range(batch):
        gr = g[b * nc:(b + 1) * nc, :STATE_TILE]
        gi = g[b * nc:(b + 1) * nc, STATE_TILE:]
        gr = gr + jnp.where(first, a_r * s0r - a_i * s0i, 0.0)
        gi = gi + jnp.where(first, a_r * s0i + a_i * s0r, 0.0)
        for k in range(SCAN_LEVELS):
            d = 1 << k
            if d >= nc:
                break
            k_r, k_i = ar_ref[0, k:k + 1, :], ai_ref[0, k:k + 1, :]
            if d < CHUNK:
                keep = row >= d
                sh_r = jnp.where(keep, pltpu.roll(gr, d, 0), 0.0)
                sh_i = jnp.where(keep, pltpu.roll(gi, d, 0), 0.0)
            else:
                zero = jnp.zeros((d, STATE_TILE), F32)
                sh_r = jnp.concatenate([zero, gr[:nc - d]], axis=0)
                sh_i = jnp.concatenate([zero, gi[:nc - d]], axis=0)
            gr, gi = gr + (k_r * sh_r - k_i * sh_i), gi + (k_r * sh_i + k_i * sh_r)
        sre_ref[b, 0] = gr[nc - 1:nc, :]
        sim_ref[b, 0] = gi[nc - 1:nc, :]
        prev_r.append(jnp.where(first, s0r, pltpu.roll(gr, 1, 0)))
        prev_i.append(jnp.where(first, s0i, pltpu.roll(gi, 1, 0)))
    sp = jnp.concatenate([jnp.concatenate(prev_r, axis=0), jnp.concatenate(prev_i, axis=0)],
                         axis=1).astype(BF16)
    y2 = (jnp.dot(ub, t8_ref[0], preferred_element_type=F32)
          + lax.dot_general(sp, wst_ref[0], NT_DIMS, preferred_element_type=F32))
    dsk = d_ref[...]
    for j in range(CHUNK):
        sl = slice(j * LANES, (j + 1) * LANES)
        y_ref[pl.ds(j, n_rows, stride=CHUNK), :] = _gelu_tanh(y2[:, sl] + dsk * ucat[:, sl])


def _prompt_ssm(h_p, h_s, t8, win, wst, a_r, a_i, d_skip, batch):
    ub = COL_U // LANES
    rows = h_p.shape[0]
    wspec = pl.BlockSpec((1, CHUNK_WIDTH, CHUNK_WIDTH), lambda gt: (gt, 0, 0))
    aspec = pl.BlockSpec((1, SCAN_LEVELS, STATE_TILE), lambda gt: (gt, 0, 0))
    sspec = pl.BlockSpec((batch, 1, 1, STATE_TILE), lambda gt: (0, gt, 0, 0))
    return pl.pallas_call(
        _prompt_ssm_kernel,
        grid=(N_GROUP_TILES,),
        in_specs=[
            pl.BlockSpec((rows, LANES), lambda gt: (0, ub + gt)),
            pl.BlockSpec((N_META, LANES), lambda gt: (h_s.shape[0] // N_META - 1, ub + gt)),
            wspec, wspec, wspec, aspec, aspec,
            pl.BlockSpec((1, LANES), lambda gt: (0, gt)),
        ],
        out_specs=[pl.BlockSpec((rows, LANES), lambda gt: (0, gt)), sspec, sspec],
        out_shape=[
            jax.ShapeDtypeStruct((rows, SSM_WIDTH), F32),
            jax.ShapeDtypeStruct((batch, N_GROUP_TILES, 1, STATE_TILE), F32),
            jax.ShapeDtypeStruct((batch, N_GROUP_TILES, 1, STATE_TILE), F32),
        ],
        compiler_params=pltpu.CompilerParams(
            dimension_semantics=("arbitrary",), vmem_limit_bytes=VMEM_LIMIT),
        name="prompt_ssm",
    )(h_p, h_s, t8, win, wst, a_r, a_i, d_skip)


def _sample_ssm_kernel(u_ref, hr_ref, hi_ref, b_ref, wc_ref, lr_ref, li_ref, d_ref,
                       y_ref, sre_ref, sim_ref):
    u = u_ref[...]
    bu = jnp.dot(u.astype(BF16), b_ref[0], preferred_element_type=F32)
    l_r, l_i = lr_ref[0], li_ref[0]
    h_r, h_i = hr_ref[...], hi_ref[...]
    n_r = l_r * h_r - l_i * h_i + bu[:, :STATE_TILE]
    n_i = l_r * h_i + l_i * h_r + bu[:, STATE_TILE:]
    sre_ref[...] = n_r
    sim_ref[...] = n_i
    hn = jnp.concatenate([n_r, n_i], axis=1).astype(BF16)
    y = lax.dot_general(hn, wc_ref[0], NT_DIMS, preferred_element_type=F32) + d_ref[...] * u
    y_ref[...] = _gelu_tanh(y)


def _sample_ssm(h_s, h0r, h0i, win, wc, l_r, l_i, d_skip, n):
    ub = COL_U // LANES
    st = pl.BlockSpec((n, STATE_TILE), lambda gt: (0, gt))
    lam = pl.BlockSpec((1, 1, STATE_TILE), lambda gt: (gt, 0, 0))
    return pl.pallas_call(
        _sample_ssm_kernel,
        grid=(N_GROUP_TILES,),
        in_specs=[
            pl.BlockSpec((n, LANES), lambda gt: (0, ub + gt)),
            st, st,
            pl.BlockSpec((1, LANES, 2 * STATE_TILE), lambda gt: (gt, CHUNK - 1, 0)),
            pl.BlockSpec((1, LANES, 2 * STATE_TILE), lambda gt: (gt, 0, 0)),
            lam, lam,
            pl.BlockSpec((1, LANES), lambda gt: (0, gt)),
        ],
        out_specs=[pl.BlockSpec((n, LANES), lambda gt: (0, gt)), st, st],
        out_shape=[
            jax.ShapeDtypeStruct((n, SSM_WIDTH), F32),
            jax.ShapeDtypeStruct((n, N_SSM_GROUPS * SSM_STATE), F32),
            jax.ShapeDtypeStruct((n, N_SSM_GROUPS * SSM_STATE), F32),
        ],
        compiler_params=pltpu.CompilerParams(
            dimension_semantics=("arbitrary",), vmem_limit_bytes=VMEM_LIMIT),
        name="sample_ssm",
    )(h_s, h0r, h0i, win, wc, l_r, l_i, d_skip)


def _merge_kernel(o_ref, za_ref, zs_ref, ga_ref, gs_ref, y_ref, x_ref, bglu_ref,
                  wglu_ref, wa_ref, ws_ref, wo_ref, out_ref):
    y = y_ref[...]
    t = jnp.dot(y.astype(BF16), wglu_ref[...], preferred_element_type=F32) + bglu_ref[...]
    y = y * jax.nn.sigmoid(t)
    za = za_ref[...]
    zs = zs_ref[...]
    a_in = (o_ref[...] * (za * jax.nn.sigmoid(za))).astype(BF16)
    s_in = (y * (zs * jax.nn.sigmoid(zs))).astype(BF16)
    br_a = jnp.dot(a_in, wa_ref[...], preferred_element_type=F32)
    br_s = jnp.dot(s_in, ws_ref[...], preferred_element_type=F32)
    mix = (jax.nn.sigmoid(ga_ref[...]) * br_a + jax.nn.sigmoid(gs_ref[...]) * br_s).astype(BF16)
    out_ref[...] = x_ref[...] + jnp.dot(mix, wo_ref[...], preferred_element_type=F32)


def _merge(o, h, y, x, b_glu, w_glu, w_a, w_s, w_o, tm):
    m = x.shape[0]
    const = lambda shape: pl.BlockSpec(shape, lambda i: (0, 0), pipeline_mode=pl.Buffered(1))
    return pl.pallas_call(
        _merge_kernel,
        grid=(m // tm,),
        in_specs=[
            pl.BlockSpec((tm, ATTN_WIDTH), lambda i: (i, 0)),
            pl.BlockSpec((tm, ATTN_WIDTH), lambda i: (i, COL_ZA // ATTN_WIDTH)),
            pl.BlockSpec((tm, SSM_WIDTH), lambda i: (i, COL_ZS // SSM_WIDTH)),
            pl.BlockSpec((tm, D_MODEL), lambda i: (i, COL_GA // D_MODEL)),
            pl.BlockSpec((tm, D_MODEL), lambda i: (i, COL_GS // D_MODEL)),
            pl.BlockSpec((tm, SSM_WIDTH), lambda i: (i, 0)),
            pl.BlockSpec((tm, D_MODEL), lambda i: (i, 0)),
            const((1, SSM_WIDTH)),
            const((SSM_WIDTH, SSM_WIDTH)),
            const((ATTN_WIDTH, D_MODEL)),
            const((SSM_WIDTH, D_MODEL)),
            const((D_MODEL, D_MODEL)),
        ],
        out_specs=pl.BlockSpec((tm, D_MODEL), lambda i: (i, 0)),
        out_shape=jax.ShapeDtypeStruct((m, D_MODEL), F32),
        compiler_params=pltpu.CompilerParams(
            dimension_semantics=("arbitrary",), vmem_limit_bytes=VMEM_LIMIT),
        name="merge",
    )(o, h, h, h, h, y, x, b_glu, w_glu, w_a, w_s, w_o)


def _rope_tables(pos):
    half = HEAD_DIM // 2
    inv_freq = (ROPE_THETA ** (-np.arange(half, dtype=np.float32) / half)).astype(np.float32)
    ang = np.asarray(pos, np.float32)[:, None] * inv_freq[None, :]
    cos, sin = np.cos(ang).astype(np.float32), np.sin(ang).astype(np.float32)
    zero = np.zeros_like(sin)
    reps = LANES // HEAD_DIM
    cos_t = np.tile(np.concatenate([cos, cos], axis=1), (1, reps))
    sin_a = np.tile(np.concatenate([-sin, zero], axis=1), (1, reps))
    sin_b = np.tile(np.concatenate([zero, sin], axis=1), (1, reps))
    return jnp.asarray(cos_t), jnp.asarray(sin_a), jnp.asarray(sin_b)


def _split_bf16(x):
    hi = x.astype(BF16)
    return hi, (x - hi.astype(F32)).astype(BF16)


def _dot_nt_f32(a, b):
    ah, al = _split_bf16(a)
    bh, bl = _split_bf16(b)
    dot = lambda x, y: lax.dot_general(x, y, NT_DIMS, preferred_element_type=F32)
    return dot(ah, bh) + dot(ah, bl) + dot(al, bh)


def _ssm_tables_kernel(are_ref, aim_ref, ldt_ref, bre_ref, bim_ref, cre_ref, cim_ref,
                       win_ref, wst_ref, t8_ref, wc_ref, ar_ref, ai_ref, lr_ref, li_ref):
    a_r, a_i = are_ref[0], aim_ref[0]
    dt = jnp.exp(ldt_ref[0])
    mag = jnp.exp(a_r * dt)
    l_r, l_i = mag * jnp.cos(a_i * dt), mag * jnp.sin(a_i * dt)
    den = a_r * a_r + a_i * a_i
    f_r = ((l_r - 1.0) * a_r + l_i * a_i) / den
    f_i = (l_i * a_r - (l_r - 1.0) * a_i) / den
    shape = (LANES, STATE_TILE)
    own = (lax.broadcasted_iota(jnp.int32, shape, 0) // SSM_GROUP
           == lax.broadcasted_iota(jnp.int32, shape, 1) // SSM_STATE)
    sq = (LANES, LANES)
    same_group = (lax.broadcasted_iota(jnp.int32, sq, 0) // SSM_GROUP
                  == lax.broadcasted_iota(jnp.int32, sq, 1) // SSM_GROUP)
    lo = lax.broadcasted_iota(jnp.int32, (1, LANES), 1) < SSM_STATE
    n_rep = STATE_TILE // LANES

    def expand(x):
        return jnp.where(own, jnp.concatenate([x] * n_rep, axis=1), 0.0)

    def compact(v):
        e = jnp.where(own, jnp.broadcast_to(v, shape), 0.0)
        s = e[:, :LANES]
        for k in range(1, n_rep):
            s = s + e[:, k * LANES:(k + 1) * LANES]
        return s + pltpu.roll(s, SSM_STATE, 1)

    lc_r, lc_i = compact(l_r), compact(l_i)
    b_r, b_i = bre_ref[0], bim_ref[0]
    c_r, c_i = cre_ref[0], cim_ref[0]
    bb_r, bb_i = _cmul(compact(f_r), compact(f_i), b_r, b_i)
    wc_ref[0] = jnp.concatenate([expand(c_r), expand(-c_i)], axis=1).astype(BF16)
    c_cat = jnp.where(lo, c_r, -c_i)
    q_r, q_i = jnp.ones(sq, F32), jnp.zeros(sq, F32)
    kern = []
    for n in range(CHUNK):
        x_r, x_i = _cmul(q_r, q_i, bb_r, bb_i)
        win_ref[0, (CHUNK - 1 - n) * LANES:(CHUNK - n) * LANES, :] = jnp.concatenate(
            [expand(x_r), expand(x_i)], axis=1).astype(BF16)
        k_n = _dot_nt_f32(jnp.where(lo, x_r, x_i), c_cat)
        kern.append(jnp.where(same_group, k_n, 0.0).astype(BF16))
        q_r, q_i = _cmul(q_r, q_i, lc_r, lc_i)
        s_r, s_i = _cmul(c_r, c_i, q_r, q_i)
        wst_ref[0, n * LANES:(n + 1) * LANES, :] = jnp.concatenate(
            [expand(s_r), expand(-s_i)], axis=1).astype(BF16)
    zero = jnp.zeros(sq, BF16)
    for j in range(CHUNK):
        for jp in range(CHUNK):
            t8_ref[0, j * LANES:(j + 1) * LANES, jp * LANES:(jp + 1) * LANES] = (
                kern[jp - j] if jp >= j else zero)
    p_r, p_i = l_r, l_i
    for _ in range(CHUNK.bit_length() - 1):
        p_r, p_i = _cmul(p_r, p_i, p_r, p_i)
    for k in range(SCAN_LEVELS):
        ar_ref[0, k:k + 1, :] = p_r
        ai_ref[0, k:k + 1, :] = p_i
        p_r, p_i = _cmul(p_r, p_i, p_r, p_i)
    lr_ref[0] = l_r
    li_ref[0] = l_i


def _ssm_tables(a_re, a_im, log_dt, b_re, b_im, c_re, c_im):
    gt = N_GROUP_TILES
    lane_vec = lambda v: v.reshape(gt, 1, STATE_TILE)
    dup = lambda v: jnp.concatenate([v, v], axis=-1)
    rows = GROUPS_PER_TILE * SSM_GROUP
    b_t = lambda v: dup(v.transpose(0, 2, 1).reshape(gt, rows, SSM_STATE))
    c_t = lambda v: dup(v.reshape(gt, rows, SSM_STATE))
    vec = pl.BlockSpec((1, 1, STATE_TILE), lambda g: (g, 0, 0))
    par = pl.BlockSpec((1, rows, LANES), lambda g: (g, 0, 0))
    big = pl.BlockSpec((1, CHUNK_WIDTH, CHUNK_WIDTH), lambda g: (g, 0, 0))
    wcs = pl.BlockSpec((1, LANES, 2 * STATE_TILE), lambda g: (g, 0, 0))
    scan = pl.BlockSpec((1, SCAN_LEVELS, STATE_TILE), lambda g: (g, 0, 0))
    return pl.pallas_call(
        _ssm_tables_kernel,
        grid=(gt,),
        in_specs=[vec, vec, vec, par, par, par, par],
        out_specs=[big, big, big, wcs, scan, scan, vec, vec],
        out_shape=[
            jax.ShapeDtypeStruct((gt, CHUNK_WIDTH, 2 * STATE_TILE), BF16),
            jax.ShapeDtypeStruct((gt, CHUNK_WIDTH, 2 * STATE_TILE), BF16),
            jax.ShapeDtypeStruct((gt, CHUNK_WIDTH, CHUNK_WIDTH), BF16),
            jax.ShapeDtypeStruct((gt, LANES, 2 * STATE_TILE), BF16),
            jax.ShapeDtypeStruct((gt, SCAN_LEVELS, STATE_TILE), F32),
            jax.ShapeDtypeStruct((gt, SCAN_LEVELS, STATE_TILE), F32),
            jax.ShapeDtypeStruct((gt, 1, STATE_TILE), F32),
            jax.ShapeDtypeStruct((gt, 1, STATE_TILE), F32),
        ],
        compiler_params=pltpu.CompilerParams(
            dimension_semantics=("arbitrary",), vmem_limit_bytes=VMEM_LIMIT),
        name="ssm_tables",
    )(lane_vec(a_re), lane_vec(a_im), lane_vec(jnp.repeat(log_dt, SSM_STATE)),
      b_t(b_re), b_t(b_im), c_t(c_re), c_t(c_im))


def kernel(x_prompt, x_sample, cache_win_k, cache_win_v, cache_meta_k, cache_meta_v,
           state_ssm_re, state_ssm_im, meta_tokens, norm_gain, w_in, q_norm_gain, k_norm_gain,
           sinks, a_re, a_im, log_dt, b_re, b_im, c_re, c_im, d_skip, w_glu, b_glu,
           w_attn_out, w_ssm_out, w_out):
    depth = w_in.shape[0]
    assert depth == 1, "single-layer trunk"
    batch, seq = x_prompt.shape[:2]
    n_s = x_sample.shape[0]
    assert x_sample.shape[1] == 1 and seq % (CHUNK * BLOCK) == 0
    l = 0

    w = w_in[l]
    qg = jnp.tile(q_norm_gain[l], TN // HEAD_DIM)[None]
    kg = jnp.tile(k_norm_gain[l], KV_WIDTH // HEAD_DIM)[None]
    lane = np.arange(LANES)
    e128 = np.where((lane[:, None] // HEAD_DIM) == (lane[None, :] // HEAD_DIM),
                    1.0 / HEAD_DIM, 0.0).astype(np.float32)
    e = jnp.asarray(np.concatenate([e128, e128], axis=0), BF16)
    gain = norm_gain[l][None]

    pos_p = N_META + np.arange(seq)
    pos_small = np.concatenate([np.full((n_s,), PAST_LEN), np.arange(N_META)])
    x_small = jnp.concatenate([x_sample[:, 0, :], meta_tokens.astype(x_prompt.dtype)], axis=0)
    xp = x_prompt.reshape(batch * seq, D_MODEL)
    xn_s = _rmsnorm(x_small, gain, tm=n_s + N_META)
    h_p, h_s = _inproj(xp, gain, xn_s, w, qg, kg, e, _rope_tables(pos_p),
                       _rope_tables(pos_small), tm=1024, halves=2)

    win, wst, t8, wc, a_r, a_i, l_r, l_i = _ssm_tables(
        a_re[l], a_im[l], log_dt[l], b_re[l], b_im[l], c_re[l], c_im[l])
    dsk = d_skip[l][None]

    vm_t = h_s[n_s:, COL_KV + KV_WIDTH:IN_WIDTH].T
    o_p = _prompt_attention(sinks[l], h_p, h_s, vm_t, batch, seq)
    y_p, p_re, p_im = _prompt_ssm(h_p, h_s, t8, win, wst, a_r, a_i, dsk, batch)
    w_glu_bf = w_glu[l].astype(BF16)
    w_a_bf = w_attn_out[l].astype(BF16)
    w_s_bf = w_ssm_out[l].astype(BF16)
    w_o_bf = w_out[l].astype(BF16)
    bglu = b_glu[l][None]
    y_prompt = _merge(o_p, h_p, y_p, xp, bglu,
                      w_glu_bf, w_a_bf, w_s_bf, w_o_bf, tm=256).reshape(batch, seq, D_MODEL)

    q_s = h_s[:n_s, COL_Q:COL_Q + ATTN_WIDTH].reshape(n_s, N_Q_HEADS, 1, HEAD_DIM)
    sel = (np.arange(N_Q_HEADS)[:, None] // (N_Q_HEADS // N_KV_HEADS)
           == np.arange(N_KV_HEADS)[None, :]).astype(np.float32)
    qp = (q_s * sel[None, :, :, None]).reshape(n_s, N_Q_HEADS, KV_WIDTH)
    bt = 8
    to_t = lambda c: c.transpose(0, 2, 3, 1).reshape(n_s, KV_WIDTH, WINDOW)
    from_t = lambda c: c.reshape(n_s, N_KV_HEADS, HEAD_DIM, WINDOW).transpose(0, 3, 1, 2)[None]
    new_t = lambda c0: h_s[:n_s, c0:c0 + KV_WIDTH].reshape(n_s // bt, bt, KV_WIDTH).transpose(0, 2, 1)
    o_s, s_win_k, s_win_v = _sample_attention(
        qp, new_t(COL_KV), new_t(COL_KV + KV_WIDTH), to_t(cache_win_k[l]), to_t(cache_win_v[l]),
        cache_meta_k[l].reshape(n_s, N_META, KV_WIDTH), cache_meta_v[l].reshape(n_s, N_META, KV_WIDTH),
        sinks[l][:, None], bt=bt)
    o_s = o_s[:, :, :HEAD_DIM].reshape(n_s, ATTN_WIDTH)
    y_s, s_re, s_im = _sample_ssm(
        h_s, state_ssm_re[l].reshape(n_s, -1), state_ssm_im[l].reshape(n_s, -1),
        win, wc, l_r, l_i, dsk, n_s)
    y_sample = _merge(o_s, h_s, y_s, x_sample[:, 0, :], bglu,
                      w_glu_bf, w_a_bf, w_s_bf, w_o_bf, tm=n_s).reshape(n_s, 1, D_MODEL)

    kv_p = h_p.reshape(batch, seq, IN_WIDTH)[:, seq - WINDOW:, COL_KV:]
    kv_p = kv_p.reshape(batch, WINDOW, 2, N_KV_HEADS, HEAD_DIM)
    p_win_k = kv_p[:, :, 0][None]
    p_win_v = kv_p[:, :, 1][None]
    kv_m = h_s[n_s:, COL_KV:IN_WIDTH].reshape(N_META, 2, N_KV_HEADS, HEAD_DIM)
    p_meta_k = jnp.broadcast_to(kv_m[None, :, 0], (batch, N_META, N_KV_HEADS, HEAD_DIM))[None]
    p_meta_v = jnp.broadcast_to(kv_m[None, :, 1], (batch, N_META, N_KV_HEADS, HEAD_DIM))[None]
    p_ssm_re = p_re.reshape(batch, N_SSM_GROUPS, SSM_STATE)[None]
    p_ssm_im = p_im.reshape(batch, N_SSM_GROUPS, SSM_STATE)[None]
    st_shape = (1, n_s, N_SSM_GROUPS, SSM_STATE)
    return (y_prompt, y_sample, p_win_k, p_win_v, p_meta_k, p_meta_v, p_ssm_re, p_ssm_im,
            from_t(s_win_k), from_t(s_win_v), s_re.reshape(st_shape), s_im.reshape(st_shape))
```

```python
import jax
import jax.numpy as jnp
import numpy as np
from jax import lax
from jax.experimental import pallas as pl
from jax.experimental.pallas import tpu as pltpu

F32 = jnp.float32
BF16 = jnp.bfloat16

D_MODEL = 2048
N_META = 16
HEAD_DIM = 64
N_Q_HEADS = 16
N_KV_HEADS = 4
ATTN_WIDTH = N_Q_HEADS * HEAD_DIM
KV_WIDTH = N_KV_HEADS * HEAD_DIM
WINDOW = 128
BLOCK = 128
ROPE_THETA = 10000.0
SSM_WIDTH = D_MODEL // 2
SSM_GROUP = 16
N_SSM_GROUPS = SSM_WIDTH // SSM_GROUP
SSM_STATE = 64
EPS = 1e-6
PAST_LEN = 8192

LANES = 128
CHUNK = 8
GROUPS_PER_TILE = LANES // SSM_GROUP
N_GROUP_TILES = N_SSM_GROUPS // GROUPS_PER_TILE
CHUNK_WIDTH = CHUNK * LANES
STATE_TILE = GROUPS_PER_TILE * SSM_STATE
SCAN_LEVELS = 8

COL_Q = 0
COL_ZA = 1024
COL_U = 2048
COL_ZS = 3072
COL_GA = 4096
COL_GS = 6144
COL_KV = 8192
IN_WIDTH = 8704
TN = 512
KV_TILE = COL_KV // TN
VMEM_LIMIT = 56 * 1024 * 1024
NEG = -1e30
NT_DIMS = (((1,), (1,)), ((), ()))


def _cmul(ar, ai, br, bi):
    return ar * br - ai * bi, ar * bi + ai * br


def _rmsnorm_kernel(x_ref, gain_ref, o_ref):
    x = x_ref[...]
    r = lax.rsqrt(jnp.mean(x * x, axis=-1, keepdims=True) + EPS)
    o_ref[...] = (x * r * gain_ref[...]).astype(BF16)


def _rmsnorm(x, gain, tm):
    m = x.shape[0]
    return pl.pallas_call(
        _rmsnorm_kernel,
        grid=(m // tm,),
        in_specs=[pl.BlockSpec((tm, D_MODEL), lambda i: (i, 0)),
                  pl.BlockSpec((1, D_MODEL), lambda i: (0, 0))],
        out_specs=pl.BlockSpec((tm, D_MODEL), lambda i: (i, 0)),
        out_shape=jax.ShapeDtypeStruct((m, D_MODEL), BF16),
        compiler_params=pltpu.CompilerParams(
            dimension_semantics=("arbitrary",), vmem_limit_bytes=VMEM_LIMIT),
        name="rmsnorm",
    )(x, gain)


def _inproj_epilogue(j, o_ref, qg_ref, kg_ref, e_ref, cos_ref, sa_ref, sb_ref):
    def head_norm_rope(n_chunks, gain_ref, scale):
        cos, sa, sb = cos_ref[...], sa_ref[...], sb_ref[...]
        for c in range(n_chunks):
            cs = slice(c * LANES, (c + 1) * LANES)
            ac = o_ref[:, cs]
            sq = ac * ac
            hi = sq.astype(BF16)
            lo = (sq - hi.astype(F32)).astype(BF16)
            ms = jnp.dot(jnp.concatenate([hi, lo], axis=1), e_ref[...],
                         preferred_element_type=F32)
            xc = ac * lax.rsqrt(ms + EPS) * gain_ref[:, cs]
            o_ref[:, cs] = (xc * cos + pltpu.roll(xc, LANES - HEAD_DIM // 2, 1) * sa
                            + pltpu.roll(xc, HEAD_DIM // 2, 1) * sb) * scale

    @pl.when(j < COL_ZA // TN)
    def _():
        head_norm_rope(TN // LANES, qg_ref, HEAD_DIM ** -0.5)

    @pl.when(j == KV_TILE)
    def _():
        head_norm_rope(KV_WIDTH // LANES, kg_ref, 1.0)


def _inproj_kernel(x_ref, gain_ref, xs_ref, w_ref, qg_ref, kg_ref, e_ref, cos_ref, sa_ref, sb_ref,
                   cos_s_ref, sa_s_ref, sb_s_ref, o_ref, os_ref, xn_ref, wbf_ref):
    h, j, i = pl.program_id(0), pl.program_id(1), pl.program_id(2)
    tm = o_ref.shape[0]
    rows = pl.ds(pl.multiple_of(i * tm, tm), tm)

    @pl.when(j == 0)
    def _():
        x = x_ref[...]
        r = lax.rsqrt(jnp.mean(x * x, axis=-1, keepdims=True) + EPS)
        xn_ref[rows, :] = (x * r * gain_ref[...]).astype(BF16)

    @pl.when(i == 0)
    def _():
        wbf_ref[...] = w_ref[...].astype(BF16)

    o_ref[...] = jnp.dot(xn_ref[rows, :], wbf_ref[...], preferred_element_type=F32)
    _inproj_epilogue(j, o_ref, qg_ref, kg_ref, e_ref, cos_ref, sa_ref, sb_ref)

    @pl.when(jnp.logical_and(h == 0, jnp.logical_and(j == 0, i == 0)))
    def _():
        os_ref[...] = jnp.zeros(os_ref.shape, F32)

    @pl.when(jnp.logical_and(h == pl.num_programs(0) - 1, i == pl.num_programs(2) - 1))
    def _():
        os_ref[...] = jnp.dot(xs_ref[...], wbf_ref[...], preferred_element_type=F32)
        _inproj_epilogue(j, os_ref, qg_ref, kg_ref, e_ref, cos_s_ref, sa_s_ref, sb_s_ref)


def _w_tile(j):
    n_q = COL_ZA // TN
    return jnp.where(j < n_q, j, jnp.where(j == KV_TILE, n_q, j + 1))


def _inproj(x, gain, xs, w, qg, kg, e, tabs_p, tabs_s, tm, halves):
    m, ms = x.shape[0], xs.shape[0]
    rows_half = m // halves
    tiles = rows_half // tm
    n_tab = tabs_p[0].shape[0] // tm
    n_col = IN_WIDTH // TN
    const = lambda shape: pl.BlockSpec(shape, lambda h, j, i: (0, 0))
    tab_p = pl.BlockSpec((tm, LANES), lambda h, j, i: ((h * tiles + i) % n_tab, 0))
    return pl.pallas_call(
        _inproj_kernel,
        grid=(halves, IN_WIDTH // TN, tiles),
        in_specs=[
            pl.BlockSpec((tm, D_MODEL),
                         lambda h, j, i: (h * tiles + jnp.where(j == 0, i, tiles - 1), 0)),
            const((1, D_MODEL)),
            const((ms, D_MODEL)),
            pl.BlockSpec((D_MODEL, TN), lambda h, j, i: (0, _w_tile(j))),
            const((1, TN)), const((1, KV_WIDTH)), const((2 * LANES, LANES)),
            tab_p, tab_p, tab_p,
            const((ms, LANES)), const((ms, LANES)), const((ms, LANES)),
        ],
        out_specs=[pl.BlockSpec((tm, TN), lambda h, j, i: (h * tiles + i, j)),
                   pl.BlockSpec((ms, TN),
                                lambda h, j, i: (0, jnp.where(h == halves - 1, j, n_col)))],
        out_shape=[jax.ShapeDtypeStruct((m, IN_WIDTH), F32),
                   jax.ShapeDtypeStruct((ms, IN_WIDTH + TN), F32)],
        scratch_shapes=[pltpu.VMEM((rows_half, D_MODEL), BF16), pltpu.VMEM((D_MODEL, TN), BF16)],
        compiler_params=pltpu.CompilerParams(
            dimension_semantics=("arbitrary", "arbitrary", "arbitrary"),
            vmem_limit_bytes=VMEM_LIMIT),
        name="inproj",
    )(x, gain, xs, w, qg, kg, e, *tabs_p, *tabs_s)


def _dup_head(x, g, lo):
    r = pltpu.roll(x, HEAD_DIM, 1)
    return jnp.where(lo, x, r) if g % 2 == 0 else jnp.where(lo, r, x)


def _prompt_attn_kernel(sinks_ref, q_ref, kvc_ref, kvp_ref, kvm_ref, vmt_ref, bias_ref, o_ref):
    lo = lax.broadcasted_iota(jnp.int32, (1, LANES), 1) < HEAD_DIM
    head_lane = lax.broadcasted_iota(jnp.int32, (1, 4 * BLOCK), 1) // BLOCK
    bias = bias_ref[0]
    v_t = jnp.concatenate([kvp_ref[:, KV_WIDTH:].T, kvc_ref[:, KV_WIDTH:].T], axis=1)
    for g in range(N_KV_HEADS):
        ks = slice((g // 2) * LANES, (g // 2 + 1) * LANES)
        k_all = jnp.concatenate([_dup_head(kvp_ref[:, ks], g, lo), _dup_head(kvc_ref[:, ks], g, lo),
                                 _dup_head(kvm_ref[:, ks], g, lo)], axis=0).astype(BF16)
        qa = q_ref[:, (2 * g) * LANES:(2 * g + 1) * LANES]
        qb = q_ref[:, (2 * g + 1) * LANES:(2 * g + 2) * LANES]
        q4 = jnp.concatenate([jnp.where(lo, qa, 0.0), jnp.where(lo, 0.0, qa),
                              jnp.where(lo, qb, 0.0), jnp.where(lo, 0.0, qb)],
                             axis=0).astype(BF16)
        s = lax.dot_general(k_all, q4, NT_DIMS, preferred_element_type=F32)
        s_w = s[:2 * BLOCK] + bias
        s_m = s[2 * BLOCK:]
        sink = jnp.where(head_lane == 0, sinks_ref[4 * g],
                         jnp.where(head_lane == 1, sinks_ref[4 * g + 1],
                                   jnp.where(head_lane == 2, sinks_ref[4 * g + 2],
                                             sinks_ref[4 * g + 3])))
        m = jnp.maximum(jnp.maximum(jnp.max(s_w, axis=0, keepdims=True),
                                    jnp.max(s_m, axis=0, keepdims=True)), sink)
        p_w = jnp.exp(s_w - m)
        p_m = jnp.exp(s_m - m)
        den = (jnp.sum(p_w, axis=0, keepdims=True) + jnp.sum(p_m, axis=0, keepdims=True)
               + jnp.exp(sink - m))
        inv = 1.0 / den
        p = jnp.concatenate([p_w * inv, p_m * inv], axis=0).astype(BF16)
        hs = slice(g * HEAD_DIM, (g + 1) * HEAD_DIM)
        vt_g = jnp.concatenate([v_t[hs], vmt_ref[hs, :]], axis=1).astype(BF16)
        o_t = jnp.dot(vt_g, p, preferred_element_type=F32)
        for pair in range(2):
            two = jnp.concatenate([o_t[:, (2 * pair) * BLOCK:(2 * pair + 1) * BLOCK],
                                   o_t[:, (2 * pair + 1) * BLOCK:(2 * pair + 2) * BLOCK]], axis=0)
            o_ref[:, (2 * g + pair) * LANES:(2 * g + pair + 1) * LANES] = two.T


def _prompt_attention(sinks, h_p, h_s, vm_t, batch, seq):
    nb = seq // BLOCK
    kvb = COL_KV // (2 * KV_WIDTH)
    kj = np.arange(2 * BLOCK)[:, None]
    qi = np.arange(4 * BLOCK)[None, :] % BLOCK
    cur = (kj >= BLOCK) & (kj - BLOCK <= qi)
    prev = (kj < BLOCK) & (kj > qi)
    bias = jnp.asarray(np.where(np.stack([cur, cur | prev]), 0.0, NEG), F32)
    return pl.pallas_call(
        _prompt_attn_kernel,
        grid=(batch, nb),
        in_specs=[
            pl.BlockSpec(memory_space=pltpu.SMEM),
            pl.BlockSpec((BLOCK, ATTN_WIDTH), lambda b, i: (b * nb + i, 0)),
            pl.BlockSpec((BLOCK, 2 * KV_WIDTH), lambda b, i: (b * nb + i, kvb)),
            pl.BlockSpec((BLOCK, 2 * KV_WIDTH),
                         lambda b, i: (b * nb + jnp.maximum(i - 1, 0), kvb)),
            pl.BlockSpec((N_META, 2 * KV_WIDTH), lambda b, i: (h_s.shape[0] // N_META - 1, kvb)),
            pl.BlockSpec((KV_WIDTH, N_META), lambda b, i: (0, 0)),
            pl.BlockSpec((1, 2 * BLOCK, 4 * BLOCK), lambda b, i: (jnp.minimum(i, 1), 0, 0)),
        ],
        out_specs=pl.BlockSpec((BLOCK, ATTN_WIDTH), lambda b, i: (b * nb + i, 0)),
        out_shape=jax.ShapeDtypeStruct((batch * seq, ATTN_WIDTH), F32),
        compiler_params=pltpu.CompilerParams(
            dimension_semantics=("arbitrary", "arbitrary"), vmem_limit_bytes=VMEM_LIMIT),
        name="prompt_attn",
    )(sinks, h_p, h_p, h_p, h_s, vm_t, bias)


def _sample_attn_kernel(qp_ref, kn_ref, vn_ref, ck_ref, cv_ref, mk_ref, mv_ref, sink_ref,
                        o_ref, ok_ref, ov_ref):
    bt = ck_ref.shape[0]
    last = lax.broadcasted_iota(jnp.int32, (1, WINDOW), 1) == WINDOW - 1
    kn = kn_ref[0]
    vn = vn_ref[0]
    ck = pltpu.roll(ck_ref[...], WINDOW - 1, 2)
    cv = pltpu.roll(cv_ref[...], WINDOW - 1, 2)
    k_new = jnp.stack([jnp.where(last, kn[:, b:b + 1], ck[b]) for b in range(bt)])
    v_new = jnp.stack([jnp.where(last, vn[:, b:b + 1], cv[b]) for b in range(bt)])
    ok_ref[...] = k_new
    ov_ref[...] = v_new
    qp = qp_ref[...].astype(BF16)
    k_w = k_new.astype(BF16)
    v_w = v_new.astype(BF16)
    k_m = mk_ref[...].astype(BF16)
    v_m = mv_ref[...].astype(BF16)
    s_w = jnp.einsum("bhc,bcw->bhw", qp, k_w, preferred_element_type=F32)
    s_m = jnp.einsum("bhc,bkc->bhk", qp, k_m, preferred_element_type=F32)
    sink = sink_ref[...][None]
    m = jnp.maximum(jnp.maximum(jnp.max(s_w, axis=-1, keepdims=True),
                                jnp.max(s_m, axis=-1, keepdims=True)), sink)
    p_w = jnp.exp(s_w - m)
    p_m = jnp.exp(s_m - m)
    den = (jnp.sum(p_w, axis=-1, keepdims=True) + jnp.sum(p_m, axis=-1, keepdims=True)
           + jnp.exp(sink - m))
    inv = 1.0 / den
    o = (jnp.einsum("bhw,bcw->bhc", (p_w * inv).astype(BF16), v_w, preferred_element_type=F32)
         + jnp.einsum("bhk,bkc->bhc", (p_m * inv).astype(BF16), v_m,
                      preferred_element_type=F32))
    hh = lax.broadcasted_iota(jnp.int32, (1, N_Q_HEADS, KV_WIDTH), 1) // (N_Q_HEADS // N_KV_HEADS)
    cc = lax.broadcasted_iota(jnp.int32, (1, N_Q_HEADS, KV_WIDTH), 2) // HEAD_DIM
    o = jnp.where(hh == cc, o, 0.0)
    o = o[:, :, :LANES] + o[:, :, LANES:]
    o_ref[...] = o + pltpu.roll(o, HEAD_DIM, 2)


def _sample_attention(qp, kn_t, vn_t, cache_k, cache_v, meta_k, meta_v, sink_col, bt):
    n = cache_k.shape[0]
    win_spec = pl.BlockSpec((bt, KV_WIDTH, WINDOW), lambda t: (t, 0, 0))
    new_spec = pl.BlockSpec((1, KV_WIDTH, bt), lambda t: (t, 0, 0))
    meta_spec = pl.BlockSpec((bt, N_META, KV_WIDTH), lambda t: (t, 0, 0))
    return pl.pallas_call(
        _sample_attn_kernel,
        grid=(n // bt,),
        in_specs=[
            pl.BlockSpec((bt, N_Q_HEADS, KV_WIDTH), lambda t: (t, 0, 0)),
            new_spec, new_spec, win_spec, win_spec, meta_spec, meta_spec,
            pl.BlockSpec((N_Q_HEADS, 1), lambda t: (0, 0)),
        ],
        out_specs=[
            pl.BlockSpec((bt, N_Q_HEADS, LANES), lambda t: (t, 0, 0)),
            win_spec, win_spec,
        ],
        out_shape=[
            jax.ShapeDtypeStruct((n, N_Q_HEADS, LANES), F32),
            jax.ShapeDtypeStruct(cache_k.shape, F32),
            jax.ShapeDtypeStruct(cache_v.shape, F32),
        ],
        compiler_params=pltpu.CompilerParams(
            dimension_semantics=("arbitrary",), vmem_limit_bytes=VMEM_LIMIT),
        name="sample_attn",
    )(qp, kn_t, vn_t, cache_k, cache_v, meta_k, meta_v, sink_col)


def _gelu_tanh(x):
    return 0.5 * x * (1.0 + jnp.tanh(0.7978845608028654 * (x + 0.044715 * (x * x * x))))


def _ssm_kernel(u_ref, um_ref, us_ref, hr_ref, hi_ref, t8_ref, win_ref, wst_ref, wc_ref,
                ar_ref, ai_ref, lr_ref, li_ref, d_ref,
                y_ref, sre_ref, sim_ref, ys_ref, nr_ref, ni_ref):
    dsk = d_ref[...]
    us = us_ref[...]
    bu = jnp.dot(us.astype(BF16), win_ref[0, (CHUNK - 1) * LANES:, :],
                 preferred_element_type=F32)
    l_r, l_i = lr_ref[0], li_ref[0]
    h_r, h_i = hr_ref[...], hi_ref[...]
    n_r = l_r * h_r - l_i * h_i + bu[:, :STATE_TILE]
    n_i = l_r * h_i + l_i * h_r + bu[:, STATE_TILE:]
    nr_ref[...] = n_r
    ni_ref[...] = n_i
    hn = jnp.concatenate([n_r, n_i], axis=1).astype(BF16)
    ys_ref[...] = _gelu_tanh(
        lax.dot_general(hn, wc_ref[0], NT_DIMS, preferred_element_type=F32) + dsk * us)

    batch = sre_ref.shape[0]
    n_rows = u_ref.shape[0] // CHUNK
    nc = n_rows // batch
    ucat = jnp.concatenate([u_ref[pl.ds(j, n_rows, stride=CHUNK), :] for j in range(CHUNK)],
                           axis=1)
    ub = ucat.astype(BF16)
    win = win_ref[0]
    g = jnp.dot(ub, win, preferred_element_type=F32)

    n_mc = um_ref.shape[0] // CHUNK
    um = jnp.concatenate([um_ref[pl.ds(j, n_mc, stride=CHUNK), :] for j in range(CHUNK)], axis=1)
    gm = jnp.dot(um.astype(BF16), win, preferred_element_type=F32)
    a_r, a_i = ar_ref[0, 0:1, :], ai_ref[0, 0:1, :]
    s0r, s0i = gm[0:1, :STATE_TILE], gm[0:1, STATE_TILE:]
    for c in range(1, n_mc):
        s0r, s0i = (a_r * s0r - a_i * s0i + gm[c:c + 1, :STATE_TILE],
                    a_r * s0i + a_i * s0r + gm[c:c + 1, STATE_TILE:])

    row = lax.broadcasted_iota(jnp.int32, (nc, 1), 0)
    first = row == 0
    prev_r, prev_i = [], []
    for b in range(batch):
        gr = g[b * nc:(b + 1) * nc, :STATE_TILE]
        gi = g[b * nc:(b + 1) * nc, STATE_TILE:]
        gr = gr + jnp.where(first, a_r * s0r - a_i * s0i, 0.0)
        gi = gi + jnp.where(first, a_r * s0i + a_i * s0r, 0.0)
        for k in range(SCAN_LEVELS):
            d = 1 << k
            if d >= nc:
                break
            k_r, k_i = ar_ref[0, k:k + 1, :], ai_ref[0, k:k + 1, :]
            if d < CHUNK:
                keep = row >= d
                sh_r = jnp.where(keep, pltpu.roll(gr, d, 0), 0.0)
                sh_i = jnp.where(keep, pltpu.roll(gi, d, 0), 0.0)
            else:
                zero = jnp.zeros((d, STATE_TILE), F32)
                sh_r = jnp.concatenate([zero, gr[:nc - d]], axis=0)
                sh_i = jnp.concatenate([zero, gi[:nc - d]], axis=0)
            gr, gi = gr + (k_r * sh_r - k_i * sh_i), gi + (k_r * sh_i + k_i * sh_r)
        sre_ref[b, 0] = gr[nc - 1:nc, :]
        sim_ref[b, 0] = gi[nc - 1:nc, :]
        prev_r.append(jnp.where(first, s0r, pltpu.roll(gr, 1, 0)))
        prev_i.append(jnp.where(first, s0i, pltpu.roll(gi, 1, 0)))
    sp = jnp.concatenate([jnp.concatenate(prev_r, axis=0), jnp.concatenate(prev_i, axis=0)],
                         axis=1).astype(BF16)
    y2 = (jnp.dot(ub, t8_ref[0], preferred_element_type=F32)
          + lax.dot_general(sp, wst_ref[0], NT_DIMS, preferred_element_type=F32))
    for j in range(CHUNK):
        sl = slice(j * LANES, (j + 1) * LANES)
        y_ref[pl.ds(j, n_rows, stride=CHUNK), :] = _gelu_tanh(y2[:, sl] + dsk * ucat[:, sl])


def _ssm(h_p, h_s, h0r, h0i, t8, win, wst, wc, a_r, a_i, l_r, l_i, d_skip, batch, n_s):
    ub = COL_U // LANES
    rows = h_p.shape[0]
    wspec = pl.BlockSpec((1, CHUNK_WIDTH, CHUNK_WIDTH), lambda gt: (gt, 0, 0))
    aspec = pl.BlockSpec((1, SCAN_LEVELS, STATE_TILE), lambda gt: (gt, 0, 0))
    sspec = pl.BlockSpec((batch, 1, 1, STATE_TILE), lambda gt: (0, gt, 0, 0))
    st = pl.BlockSpec((n_s, STATE_TILE), lambda gt: (0, gt))
    lam = pl.BlockSpec((1, 1, STATE_TILE), lambda gt: (gt, 0, 0))
    return pl.pallas_call(
        _ssm_kernel,
        grid=(N_GROUP_TILES,),
        in_specs=[
            pl.BlockSpec((rows, LANES), lambda gt: (0, ub + gt)),
            pl.BlockSpec((N_META, LANES), lambda gt: (h_s.shape[0] // N_META - 1, ub + gt)),
            pl.BlockSpec((n_s, LANES), lambda gt: (0, ub + gt)),
            st, st,
            wspec, wspec, wspec,
            pl.BlockSpec((1, LANES, 2 * STATE_TILE), lambda gt: (gt, 0, 0)),
            aspec, aspec, lam, lam,
            pl.BlockSpec((1, LANES), lambda gt: (0, gt)),
        ],
        out_specs=[pl.BlockSpec((rows, LANES), lambda gt: (0, gt)), sspec, sspec,
                   pl.BlockSpec((n_s, LANES), lambda gt: (0, gt)), st, st],
        out_shape=[
            jax.ShapeDtypeStruct((rows, SSM_WIDTH), F32),
            jax.ShapeDtypeStruct((batch, N_GROUP_TILES, 1, STATE_TILE), F32),
            jax.ShapeDtypeStruct((batch, N_GROUP_TILES, 1, STATE_TILE), F32),
            jax.ShapeDtypeStruct((n_s, SSM_WIDTH), F32),
            jax.ShapeDtypeStruct((n_s, N_SSM_GROUPS * SSM_STATE), F32),
            jax.ShapeDtypeStruct((n_s, N_SSM_GROUPS * SSM_STATE), F32),
        ],
        compiler_params=pltpu.CompilerParams(
            dimension_semantics=("arbitrary",), vmem_limit_bytes=VMEM_LIMIT),
        name="ssm",
    )(h_p, h_s, h_s, h0r, h0i, t8, win, wst, wc, a_r, a_i, l_r, l_i, d_skip)


def _merge_kernel(o_ref, za_ref, zs_ref, ga_ref, gs_ref, y_ref, x_ref,
                  o2_ref, za2_ref, zs2_ref, ga2_ref, gs2_ref, y2_ref, x2_ref,
                  bglu_ref, wglu_ref, wa_ref, ws_ref, wo_ref, out_ref, out2_ref):
    def merge(o_r, za_r, zs_r, ga_r, gs_r, y_r, x_r, out_r):
        y = y_r[...]
        t = jnp.dot(y.astype(BF16), wglu_ref[...], preferred_element_type=F32) + bglu_ref[...]
        y = y * jax.nn.sigmoid(t)
        za = za_r[...]
        zs = zs_r[...]
        a_in = (o_r[...] * (za * jax.nn.sigmoid(za))).astype(BF16)
        s_in = (y * (zs * jax.nn.sigmoid(zs))).astype(BF16)
        br_a = jnp.dot(a_in, wa_ref[...], preferred_element_type=F32)
        br_s = jnp.dot(s_in, ws_ref[...], preferred_element_type=F32)
        mix = (jax.nn.sigmoid(ga_r[...]) * br_a + jax.nn.sigmoid(gs_r[...]) * br_s).astype(BF16)
        out_r[...] = x_r[...] + jnp.dot(mix, wo_ref[...], preferred_element_type=F32)

    merge(o_ref, za_ref, zs_ref, ga_ref, gs_ref, y_ref, x_ref, out_ref)

    @pl.when(pl.program_id(0) == pl.num_programs(0) - 1)
    def _():
        merge(o2_ref, za2_ref, zs2_ref, ga2_ref, gs2_ref, y2_ref, x2_ref, out2_ref)


def _merge(o, h, y, x, o2, h2, y2, x2, b_glu, w_glu, w_a, w_s, w_o, tm):
    m, m2 = x.shape[0], x2.shape[0]
    const = lambda shape: pl.BlockSpec(shape, lambda i: (0, 0), pipeline_mode=pl.Buffered(1))
    rows = lambda width, col: pl.BlockSpec((tm, width), lambda i: (i, col))
    rows2 = lambda width, col: pl.BlockSpec((m2, width), lambda i: (0, col),
                                            pipeline_mode=pl.Buffered(1))
    operands = lambda spec: [
        spec(ATTN_WIDTH, 0), spec(ATTN_WIDTH, COL_ZA // ATTN_WIDTH),
        spec(SSM_WIDTH, COL_ZS // SSM_WIDTH), spec(D_MODEL, COL_GA // D_MODEL),
        spec(D_MODEL, COL_GS // D_MODEL), spec(SSM_WIDTH, 0), spec(D_MODEL, 0)]
    return pl.pallas_call(
        _merge_kernel,
        grid=(m // tm,),
        in_specs=operands(rows) + operands(rows2) + [
            const((1, SSM_WIDTH)),
            const((SSM_WIDTH, SSM_WIDTH)),
            const((ATTN_WIDTH, D_MODEL)),
            const((SSM_WIDTH, D_MODEL)),
            const((D_MODEL, D_MODEL)),
        ],
        out_specs=[pl.BlockSpec((tm, D_MODEL), lambda i: (i, 0)),
                   pl.BlockSpec((m2, D_MODEL), lambda i: (0, 0))],
        out_shape=[jax.ShapeDtypeStruct((m, D_MODEL), F32),
                   jax.ShapeDtypeStruct((m2, D_MODEL), F32)],
        compiler_params=pltpu.CompilerParams(
            dimension_semantics=("arbitrary",), vmem_limit_bytes=VMEM_LIMIT),
        name="merge",
    )(o, h, h, h, h, y, x, o2, h2, h2, h2, h2, y2, x2, b_glu, w_glu, w_a, w_s, w_o)


def _rope_tables(pos):
    half = HEAD_DIM // 2
    inv_freq = (ROPE_THETA ** (-np.arange(half, dtype=np.float32) / half)).astype(np.float32)
    ang = np.asarray(pos, np.float32)[:, None] * inv_freq[None, :]
    cos, sin = np.cos(ang).astype(np.float32), np.sin(ang).astype(np.float32)
    zero = np.zeros_like(sin)
    reps = LANES // HEAD_DIM
    cos_t = np.tile(np.concatenate([cos, cos], axis=1), (1, reps))
    sin_a = np.tile(np.concatenate([-sin, zero], axis=1), (1, reps))
    sin_b = np.tile(np.concatenate([zero, sin], axis=1), (1, reps))
    return jnp.asarray(cos_t), jnp.asarray(sin_a), jnp.asarray(sin_b)


def _split_bf16(x):
    hi = x.astype(BF16)
    return hi, (x - hi.astype(F32)).astype(BF16)


def _dot_nt_f32(a, b):
    ah, al = _split_bf16(a)
    bh, bl = _split_bf16(b)
    dot = lambda x, y: lax.dot_general(x, y, NT_DIMS, preferred_element_type=F32)
    return dot(ah, bh) + dot(ah, bl) + dot(al, bh)


def _ssm_tables_kernel(are_ref, aim_ref, ldt_ref, bre_ref, bim_ref, cre_ref, cim_ref,
                       win_ref, wst_ref, t8_ref, wc_ref, ar_ref, ai_ref, lr_ref, li_ref):
    a_r, a_i = are_ref[0], aim_ref[0]
    dt = jnp.exp(ldt_ref[0])
    mag = jnp.exp(a_r * dt)
    l_r, l_i = mag * jnp.cos(a_i * dt), mag * jnp.sin(a_i * dt)
    den = a_r * a_r + a_i * a_i
    f_r = ((l_r - 1.0) * a_r + l_i * a_i) / den
    f_i = (l_i * a_r - (l_r - 1.0) * a_i) / den
    shape = (LANES, STATE_TILE)
    own = (lax.broadcasted_iota(jnp.int32, shape, 0) // SSM_GROUP
           == lax.broadcasted_iota(jnp.int32, shape, 1) // SSM_STATE)
    sq = (LANES, LANES)
    same_group = (lax.broadcasted_iota(jnp.int32, sq, 0) // SSM_GROUP
                  == lax.broadcasted_iota(jnp.int32, sq, 1) // SSM_GROUP)
    lo = lax.broadcasted_iota(jnp.int32, (1, LANES), 1) < SSM_STATE
    n_rep = STATE_TILE // LANES

    def expand(x):
        return jnp.where(own, jnp.concatenate([x] * n_rep, axis=1), 0.0)

    def compact(v):
        e = jnp.where(own, jnp.broadcast_to(v, shape), 0.0)
        s = e[:, :LANES]
        for k in range(1, n_rep):
            s = s + e[:, k * LANES:(k + 1) * LANES]
        return s + pltpu.roll(s, SSM_STATE, 1)

    lc_r, lc_i = compact(l_r), compact(l_i)
    b_r, b_i = bre_ref[0], bim_ref[0]
    c_r, c_i = cre_ref[0], cim_ref[0]
    bb_r, bb_i = _cmul(compact(f_r), compact(f_i), b_r, b_i)
    wc_ref[0] = jnp.concatenate([expand(c_r), expand(-c_i)], axis=1).astype(BF16)
    c_cat = jnp.where(lo, c_r, -c_i)
    q_r, q_i = jnp.ones(sq, F32), jnp.zeros(sq, F32)
    kern = []
    for n in range(CHUNK):
        x_r, x_i = _cmul(q_r, q_i, bb_r, bb_i)
        win_ref[0, (CHUNK - 1 - n) * LANES:(CHUNK - n) * LANES, :] = jnp.concatenate(
            [expand(x_r), expand(x_i)], axis=1).astype(BF16)
        k_n = _dot_nt_f32(jnp.where(lo, x_r, x_i), c_cat)
        kern.append(jnp.where(same_group, k_n, 0.0).astype(BF16))
        q_r, q_i = _cmul(q_r, q_i, lc_r, lc_i)
        s_r, s_i = _cmul(c_r, c_i, q_r, q_i)
        wst_ref[0, n * LANES:(n + 1) * LANES, :] = jnp.concatenate(
            [expand(s_r), expand(-s_i)], axis=1).astype(BF16)
    zero = jnp.zeros(sq, BF16)
    for j in range(CHUNK):
        for jp in range(CHUNK):
            t8_ref[0, j * LANES:(j + 1) * LANES, jp * LANES:(jp + 1) * LANES] = (
                kern[jp - j] if jp >= j else zero)
    p_r, p_i = l_r, l_i
    for _ in range(CHUNK.bit_length() - 1):
        p_r, p_i = _cmul(p_r, p_i, p_r, p_i)
    for k in range(SCAN_LEVELS):
        ar_ref[0, k:k + 1, :] = p_r
        ai_ref[0, k:k + 1, :] = p_i
        p_r, p_i = _cmul(p_r, p_i, p_r, p_i)
    lr_ref[0] = l_r
    li_ref[0] = l_i


def _ssm_tables(a_re, a_im, log_dt, b_re, b_im, c_re, c_im):
    gt = N_GROUP_TILES
    lane_vec = lambda v: v.reshape(gt, 1, STATE_TILE)
    dup = lambda v: jnp.concatenate([v, v], axis=-1)
    rows = GROUPS_PER_TILE * SSM_GROUP
    b_t = lambda v: dup(v.transpose(0, 2, 1).reshape(gt, rows, SSM_STATE))
    c_t = lambda v: dup(v.reshape(gt, rows, SSM_STATE))
    vec = pl.BlockSpec((1, 1, STATE_TILE), lambda g: (g, 0, 0))
    par = pl.BlockSpec((1, rows, LANES), lambda g: (g, 0, 0))
    big = pl.BlockSpec((1, CHUNK_WIDTH, CHUNK_WIDTH), lambda g: (g, 0, 0))
    wcs = pl.BlockSpec((1, LANES, 2 * STATE_TILE), lambda g: (g, 0, 0))
    scan = pl.BlockSpec((1, SCAN_LEVELS, STATE_TILE), lambda g: (g, 0, 0))
    return pl.pallas_call(
        _ssm_tables_kernel,
        grid=(gt,),
        in_specs=[vec, vec, vec, par, par, par, par],
        out_specs=[big, big, big, wcs, scan, scan, vec, vec],
        out_shape=[
            jax.ShapeDtypeStruct((gt, CHUNK_WIDTH, 2 * STATE_TILE), BF16),
            jax.ShapeDtypeStruct((gt, CHUNK_WIDTH, 2 * STATE_TILE), BF16),
            jax.ShapeDtypeStruct((gt, CHUNK_WIDTH, CHUNK_WIDTH), BF16),
            jax.ShapeDtypeStruct((gt, LANES, 2 * STATE_TILE), BF16),
            jax.ShapeDtypeStruct((gt, SCAN_LEVELS, STATE_TILE), F32),
            jax.ShapeDtypeStruct((gt, SCAN_LEVELS, STATE_TILE), F32),
            jax.ShapeDtypeStruct((gt, 1, STATE_TILE), F32),
            jax.ShapeDtypeStruct((gt, 1, STATE_TILE), F32),
        ],
        compiler_params=pltpu.CompilerParams(
            dimension_semantics=("arbitrary",), vmem_limit_bytes=VMEM_LIMIT),
        name="ssm_tables",
    )(lane_vec(a_re), lane_vec(a_im), lane_vec(jnp.repeat(log_dt, SSM_STATE)),
      b_t(b_re), b_t(b_im), c_t(c_re), c_t(c_im))


def kernel(x_prompt, x_sample, cache_win_k, cache_win_v, cache_meta_k, cache_meta_v,
           state_ssm_re, state_ssm_im, meta_tokens, norm_gain, w_in, q_norm_gain, k_norm_gain,
           sinks, a_re, a_im, log_dt, b_re, b_im, c_re, c_im, d_skip, w_glu, b_glu,
           w_attn_out, w_ssm_out, w_out):
    depth = w_in.shape[0]
    assert depth == 1, "single-layer trunk"
    batch, seq = x_prompt.shape[:2]
    n_s = x_sample.shape[0]
    assert x_sample.shape[1] == 1 and seq % (CHUNK * BLOCK) == 0
    l = 0

    w = w_in[l]
    qg = jnp.tile(q_norm_gain[l], TN // HEAD_DIM)[None]
    kg = jnp.tile(k_norm_gain[l], KV_WIDTH // HEAD_DIM)[None]
    lane = np.arange(LANES)
    e128 = np.where((lane[:, None] // HEAD_DIM) == (lane[None, :] // HEAD_DIM),
                    1.0 / HEAD_DIM, 0.0).astype(np.float32)
    e = jnp.asarray(np.concatenate([e128, e128], axis=0), BF16)
    gain = norm_gain[l][None]

    pos_p = N_META + np.arange(seq)
    pos_small = np.concatenate([np.full((n_s,), PAST_LEN), np.arange(N_META)])
    x_small = jnp.concatenate([x_sample[:, 0, :], meta_tokens.astype(x_prompt.dtype)], axis=0)
    xp = x_prompt.reshape(batch * seq, D_MODEL)
    xn_s = _rmsnorm(x_small, gain, tm=n_s + N_META)
    h_p, h_s = _inproj(xp, gain, xn_s, w, qg, kg, e, _rope_tables(pos_p),
                       _rope_tables(pos_small), tm=1024, halves=2)

    win, wst, t8, wc, a_r, a_i, l_r, l_i = _ssm_tables(
        a_re[l], a_im[l], log_dt[l], b_re[l], b_im[l], c_re[l], c_im[l])
    dsk = d_skip[l][None]

    vm_t = h_s[n_s:, COL_KV + KV_WIDTH:IN_WIDTH].T
    o_p = _prompt_attention(sinks[l], h_p, h_s, vm_t, batch, seq)
    q_s = h_s[:n_s, COL_Q:COL_Q + ATTN_WIDTH].reshape(n_s, N_Q_HEADS, 1, HEAD_DIM)
    sel = (np.arange(N_Q_HEADS)[:, None] // (N_Q_HEADS // N_KV_HEADS)
           == np.arange(N_KV_HEADS)[None, :]).astype(np.float32)
    qp = (q_s * sel[None, :, :, None]).reshape(n_s, N_Q_HEADS, KV_WIDTH)
    bt = 8
    to_t = lambda c: c.transpose(0, 2, 3, 1).reshape(n_s, KV_WIDTH, WINDOW)
    from_t = lambda c: c.reshape(n_s, N_KV_HEADS, HEAD_DIM, WINDOW).transpose(0, 3, 1, 2)[None]
    new_t = lambda c0: h_s[:n_s, c0:c0 + KV_WIDTH].reshape(n_s // bt, bt, KV_WIDTH).transpose(0, 2, 1)
    o_s, s_win_k, s_win_v = _sample_attention(
        qp, new_t(COL_KV), new_t(COL_KV + KV_WIDTH), to_t(cache_win_k[l]), to_t(cache_win_v[l]),
        cache_meta_k[l].reshape(n_s, N_META, KV_WIDTH), cache_meta_v[l].reshape(n_s, N_META, KV_WIDTH),
        sinks[l][:, None], bt=bt)
    o_s = o_s[:, :, :HEAD_DIM].reshape(n_s, ATTN_WIDTH)

    y_p, p_re, p_im, y_s, s_re, s_im = _ssm(
        h_p, h_s, state_ssm_re[l].reshape(n_s, -1), state_ssm_im[l].reshape(n_s, -1),
        t8, win, wst, wc, a_r, a_i, l_r, l_i, dsk, batch, n_s)
    y_prompt, y_sample = _merge(
        o_p, h_p, y_p, xp, o_s, h_s, y_s, x_sample[:, 0, :], b_glu[l][None],
        w_glu[l].astype(BF16), w_attn_out[l].astype(BF16), w_ssm_out[l].astype(BF16),
        w_out[l].astype(BF16), tm=256)
    y_prompt = y_prompt.reshape(batch, seq, D_MODEL)
    y_sample = y_sample.reshape(n_s, 1, D_MODEL)

    kv_p = h_p.reshape(batch, seq, IN_WIDTH)[:, seq - WINDOW:, COL_KV:]
    kv_p = kv_p.reshape(batch, WINDOW, 2, N_KV_HEADS, HEAD_DIM)
    p_win_k = kv_p[:, :, 0][None]
    p_win_v = kv_p[:, :, 1][None]
    kv_m = h_s[n_s:, COL_KV:IN_WIDTH].reshape(N_META, 2, N_KV_HEADS, HEAD_DIM)
    p_meta_k = jnp.broadcast_to(kv_m[None, :, 0], (batch, N_META, N_KV_HEADS, HEAD_DIM))[None]
    p_meta_v = jnp.broadcast_to(kv_m[None, :, 1], (batch, N_META, N_KV_HEADS, HEAD_DIM))[None]
    p_ssm_re = p_re.reshape(batch, N_SSM_GROUPS, SSM_STATE)[None]
    p_ssm_im = p_im.reshape(batch, N_SSM_GROUPS, SSM_STATE)[None]
    st_shape = (1, n_s, N_SSM_GROUPS, SSM_STATE)
    return (y_prompt, y_sample, p_win_k, p_win_v, p_meta_k, p_meta_v, p_ssm_re, p_ssm_im,
            from_t(s_win_k), from_t(s_win_v), s_re.reshape(st_shape), s_im.reshape(st_shape))
```

```python
import functools

import jax
import jax.numpy as jnp
import numpy as np
from jax import lax
from jax.experimental import pallas as pl
from jax.experimental.pallas import tpu as pltpu

F32 = jnp.float32
BF16 = jnp.bfloat16

D_MODEL = 2048
N_META = 16
HEAD_DIM = 64
N_Q_HEADS = 16
N_KV_HEADS = 4
ATTN_WIDTH = N_Q_HEADS * HEAD_DIM
KV_WIDTH = N_KV_HEADS * HEAD_DIM
WINDOW = 128
BLOCK = 128
ROPE_THETA = 10000.0
SSM_WIDTH = D_MODEL // 2
SSM_GROUP = 16
N_SSM_GROUPS = SSM_WIDTH // SSM_GROUP
SSM_STATE = 64
EPS = 1e-6
PAST_LEN = 8192

LANES = 128
CHUNK = 8
GROUPS_PER_TILE = LANES // SSM_GROUP
N_GROUP_TILES = N_SSM_GROUPS // GROUPS_PER_TILE
CHUNK_WIDTH = CHUNK * LANES
STATE_TILE = GROUPS_PER_TILE * SSM_STATE
SCAN_LEVELS = 8

COL_Q = 0
COL_ZA = 1024
COL_U = 2048
COL_ZS = 3072
COL_GA = 4096
COL_GS = 6144
COL_KV = 8192
IN_WIDTH = 8704
TN = 512
CAST_ROWS = 16
KV_TILE = COL_KV // TN
VMEM_LIMIT = 56 * 1024 * 1024
NEG = -1e30
NT_DIMS = (((1,), (1,)), ((), ()))


def _cmul(ar, ai, br, bi):
    return ar * br - ai * bi, ar * bi + ai * br


def _rmsnorm_kernel(x_ref, gain_ref, o_ref):
    x = x_ref[...]
    r = lax.rsqrt(jnp.mean(x * x, axis=-1, keepdims=True) + EPS)
    o_ref[...] = (x * r * gain_ref[...]).astype(BF16)


def _rmsnorm(x, gain, tm):
    m = x.shape[0]
    return pl.pallas_call(
        _rmsnorm_kernel,
        grid=(m // tm,),
        in_specs=[pl.BlockSpec((tm, D_MODEL), lambda i: (i, 0)),
                  pl.BlockSpec((1, D_MODEL), lambda i: (0, 0))],
        out_specs=pl.BlockSpec((tm, D_MODEL), lambda i: (i, 0)),
        out_shape=jax.ShapeDtypeStruct((m, D_MODEL), BF16),
        compiler_params=pltpu.CompilerParams(
            dimension_semantics=("arbitrary",), vmem_limit_bytes=VMEM_LIMIT),
        name="rmsnorm",
    )(x, gain)


def _inproj_epilogue(j, o_ref, qg_ref, kg_ref, e_ref, cos_ref, sa_ref, sb_ref):
    def head_norm_rope(n_chunks, gain_ref, scale):
        cos, sa, sb = cos_ref[...], sa_ref[...], sb_ref[...]
        for c in range(n_chunks):
            cs = slice(c * LANES, (c + 1) * LANES)
            ac = o_ref[:, cs]
            sq = ac * ac
            hi = sq.astype(BF16)
            lo = (sq - hi.astype(F32)).astype(BF16)
            ms = jnp.dot(jnp.concatenate([hi, lo], axis=1), e_ref[...],
                         preferred_element_type=F32)
            xc = ac * lax.rsqrt(ms + EPS) * gain_ref[:, cs]
            o_ref[:, cs] = (xc * cos + pltpu.roll(xc, LANES - HEAD_DIM // 2, 1) * sa
                            + pltpu.roll(xc, HEAD_DIM // 2, 1) * sb) * scale

    @pl.when(j < COL_ZA // TN)
    def _():
        head_norm_rope(TN // LANES, qg_ref, HEAD_DIM ** -0.5)

    @pl.when(j == KV_TILE)
    def _():
        head_norm_rope(KV_WIDTH // LANES, kg_ref, 1.0)


def _inproj_kernel(x_ref, gain_ref, xs_ref, w_ref, qg_ref, kg_ref, e_ref, cos_ref, sa_ref, sb_ref,
                   cos_s_ref, sa_s_ref, sb_s_ref, *rest, mw_slabs):
    n_mw = (len(rest) - 4) // 2
    mw_refs, (o_ref, os_ref) = rest[:n_mw], rest[n_mw:n_mw + 2]
    mw_out_refs, (xn_ref, wbf_ref) = rest[n_mw + 2:2 * n_mw + 2], rest[2 * n_mw + 2:]
    h, j, i = pl.program_id(0), pl.program_id(1), pl.program_id(2)
    tm = o_ref.shape[0]
    rows = pl.ds(pl.multiple_of(i * tm, tm), tm)

    @pl.when(j == 0)
    def _():
        x = x_ref[...]
        r = lax.rsqrt(jnp.mean(x * x, axis=-1, keepdims=True) + EPS)
        xn_ref[rows, :] = (x * r * gain_ref[...]).astype(BF16)

    @pl.when(i == 0)
    def _():
        wbf_ref[...] = w_ref[...].astype(BF16)

    o_ref[...] = jnp.dot(xn_ref[rows, :], wbf_ref[...], preferred_element_type=F32)
    _inproj_epilogue(j, o_ref, qg_ref, kg_ref, e_ref, cos_ref, sa_ref, sb_ref)

    step = (h * pl.num_programs(1) + j) * pl.num_programs(2) + i
    for src, dst, n_slabs in zip(mw_refs, mw_out_refs, mw_slabs):
        @pl.when(step < n_slabs)
        def _():
            dst[...] = src[...].astype(BF16)

    @pl.when(jnp.logical_and(h == 0, jnp.logical_and(j == 0, i == 0)))
    def _():
        os_ref[...] = jnp.zeros(os_ref.shape, F32)

    @pl.when(jnp.logical_and(h == pl.num_programs(0) - 1, i == pl.num_programs(2) - 1))
    def _():
        os_ref[...] = jnp.dot(xs_ref[...], wbf_ref[...], preferred_element_type=F32)
        _inproj_epilogue(j, os_ref, qg_ref, kg_ref, e_ref, cos_s_ref, sa_s_ref, sb_s_ref)


def _w_tile(j):
    n_q = COL_ZA // TN
    return jnp.where(j < n_q, j, jnp.where(j == KV_TILE, n_q, j + 1))


def _inproj(x, gain, xs, w, qg, kg, e, tabs_p, tabs_s, merge_weights, tm, halves):
    m, ms = x.shape[0], xs.shape[0]
    rows_half = m // halves
    tiles = rows_half // tm
    n_tab = tabs_p[0].shape[0] // tm
    n_col = IN_WIDTH // TN
    const = lambda shape: pl.BlockSpec(shape, lambda h, j, i: (0, 0))
    tab_p = pl.BlockSpec((tm, LANES), lambda h, j, i: ((h * tiles + i) % n_tab, 0))
    n_steps = halves * n_col * tiles
    for mw in merge_weights:
        assert mw.shape[0] % CAST_ROWS == 0 and mw.shape[0] // CAST_ROWS <= n_steps
    slab = lambda mw: pl.BlockSpec(
        (CAST_ROWS, mw.shape[1]),
        lambda h, j, i: (jnp.minimum((h * n_col + j) * tiles + i, mw.shape[0] // CAST_ROWS - 1), 0))
    return pl.pallas_call(
        functools.partial(_inproj_kernel,
                          mw_slabs=tuple(mw.shape[0] // CAST_ROWS for mw in merge_weights)),
        grid=(halves, IN_WIDTH // TN, tiles),
        in_specs=[
            pl.BlockSpec((tm, D_MODEL),
                         lambda h, j, i: (h * tiles + jnp.where(j == 0, i, tiles - 1), 0)),
            const((1, D_MODEL)),
            const((ms, D_MODEL)),
            pl.BlockSpec((D_MODEL, TN), lambda h, j, i: (0, _w_tile(j))),
            const((1, TN)), const((1, KV_WIDTH)), const((2 * LANES, LANES)),
            tab_p, tab_p, tab_p,
            const((ms, LANES)), const((ms, LANES)), const((ms, LANES)),
        ] + [slab(mw) for mw in merge_weights],
        out_specs=[pl.BlockSpec((tm, TN), lambda h, j, i: (h * tiles + i, j)),
                   pl.BlockSpec((ms, TN),
                                lambda h, j, i: (0, jnp.where(h == halves - 1, j, n_col)))
                   ] + [slab(mw) for mw in merge_weights],
        out_shape=[jax.ShapeDtypeStruct((m, IN_WIDTH), F32),
                   jax.ShapeDtypeStruct((ms, IN_WIDTH + TN), F32)
                   ] + [jax.ShapeDtypeStruct(mw.shape, BF16) for mw in merge_weights],
        scratch_shapes=[pltpu.VMEM((rows_half, D_MODEL), BF16), pltpu.VMEM((D_MODEL, TN), BF16)],
        compiler_params=pltpu.CompilerParams(
            dimension_semantics=("arbitrary", "arbitrary", "arbitrary"),
            vmem_limit_bytes=VMEM_LIMIT),
        name="inproj",
    )(x, gain, xs, w, qg, kg, e, *tabs_p, *tabs_s, *merge_weights)


def _dup_head(x, g, lo):
    r = pltpu.roll(x, HEAD_DIM, 1)
    return jnp.where(lo, x, r) if g % 2 == 0 else jnp.where(lo, r, x)


def _prompt_attn_kernel(sinks_ref, q_ref, kvc_ref, kvp_ref, kvm_ref, vmt_ref, bias_ref, o_ref):
    lo = lax.broadcasted_iota(jnp.int32, (1, LANES), 1) < HEAD_DIM
    head_lane = lax.broadcasted_iota(jnp.int32, (1, 4 * BLOCK), 1) // BLOCK
    bias = bias_ref[0]
    v_t = jnp.concatenate([kvp_ref[:, KV_WIDTH:].T, kvc_ref[:, KV_WIDTH:].T], axis=1)
    for g in range(N_KV_HEADS):
        ks = slice((g // 2) * LANES, (g // 2 + 1) * LANES)
        k_all = jnp.concatenate([_dup_head(kvp_ref[:, ks], g, lo), _dup_head(kvc_ref[:, ks], g, lo),
                                 _dup_head(kvm_ref[:, ks], g, lo)], axis=0).astype(BF16)
        qa = q_ref[:, (2 * g) * LANES:(2 * g + 1) * LANES]
        qb = q_ref[:, (2 * g + 1) * LANES:(2 * g + 2) * LANES]
        q4 = jnp.concatenate([jnp.where(lo, qa, 0.0), jnp.where(lo, 0.0, qa),
                              jnp.where(lo, qb, 0.0), jnp.where(lo, 0.0, qb)],
                             axis=0).astype(BF16)
        s = lax.dot_general(k_all, q4, NT_DIMS, preferred_element_type=F32)
        s_w = s[:2 * BLOCK] + bias
        s_m = s[2 * BLOCK:]
        sink = jnp.where(head_lane == 0, sinks_ref[4 * g],
                         jnp.where(head_lane == 1, sinks_ref[4 * g + 1],
                                   jnp.where(head_lane == 2, sinks_ref[4 * g + 2],
                                             sinks_ref[4 * g + 3])))
        m = jnp.maximum(jnp.maximum(jnp.max(s_w, axis=0, keepdims=True),
                                    jnp.max(s_m, axis=0, keepdims=True)), sink)
        p_w = jnp.exp(s_w - m)
        p_m = jnp.exp(s_m - m)
        den = (jnp.sum(p_w, axis=0, keepdims=True) + jnp.sum(p_m, axis=0, keepdims=True)
               + jnp.exp(sink - m))
        inv = 1.0 / den
        p = jnp.concatenate([p_w * inv, p_m * inv], axis=0).astype(BF16)
        hs = slice(g * HEAD_DIM, (g + 1) * HEAD_DIM)
        vt_g = jnp.concatenate([v_t[hs], vmt_ref[hs, :]], axis=1).astype(BF16)
        o_t = jnp.dot(vt_g, p, preferred_element_type=F32)
        for pair in range(2):
            two = jnp.concatenate([o_t[:, (2 * pair) * BLOCK:(2 * pair + 1) * BLOCK],
                                   o_t[:, (2 * pair + 1) * BLOCK:(2 * pair + 2) * BLOCK]], axis=0)
            o_ref[:, (2 * g + pair) * LANES:(2 * g + pair + 1) * LANES] = two.T


def _prompt_attention(sinks, h_p, h_s, vm_t, batch, seq):
    nb = seq // BLOCK
    kvb = COL_KV // (2 * KV_WIDTH)
    kj = np.arange(2 * BLOCK)[:, None]
    qi = np.arange(4 * BLOCK)[None, :] % BLOCK
    cur = (kj >= BLOCK) & (kj - BLOCK <= qi)
    prev = (kj < BLOCK) & (kj > qi)
    bias = jnp.asarray(np.where(np.stack([cur, cur | prev]), 0.0, NEG), F32)
    return pl.pallas_call(
        _prompt_attn_kernel,
        grid=(batch, nb),
        in_specs=[
            pl.BlockSpec(memory_space=pltpu.SMEM),
            pl.BlockSpec((BLOCK, ATTN_WIDTH), lambda b, i: (b * nb + i, 0)),
            pl.BlockSpec((BLOCK, 2 * KV_WIDTH), lambda b, i: (b * nb + i, kvb)),
            pl.BlockSpec((BLOCK, 2 * KV_WIDTH),
                         lambda b, i: (b * nb + jnp.maximum(i - 1, 0), kvb)),
            pl.BlockSpec((N_META, 2 * KV_WIDTH), lambda b, i: (h_s.shape[0] // N_META - 1, kvb)),
            pl.BlockSpec((KV_WIDTH, N_META), lambda b, i: (0, 0)),
            pl.BlockSpec((1, 2 * BLOCK, 4 * BLOCK), lambda b, i: (jnp.minimum(i, 1), 0, 0)),
        ],
        out_specs=pl.BlockSpec((BLOCK, ATTN_WIDTH), lambda b, i: (b * nb + i, 0)),
        out_shape=jax.ShapeDtypeStruct((batch * seq, ATTN_WIDTH), F32),
        compiler_params=pltpu.CompilerParams(
            dimension_semantics=("arbitrary", "arbitrary"), vmem_limit_bytes=VMEM_LIMIT),
        name="prompt_attn",
    )(sinks, h_p, h_p, h_p, h_s, vm_t, bias)


def _sample_attn_kernel(qp_ref, kn_ref, vn_ref, ck_ref, cv_ref, mk_ref, mv_ref, sink_ref,
                        o_ref, ok_ref, ov_ref):
    bt = ck_ref.shape[0]
    last = lax.broadcasted_iota(jnp.int32, (1, WINDOW), 1) == WINDOW - 1
    kn = kn_ref[0]
    vn = vn_ref[0]
    ck = pltpu.roll(ck_ref[...], WINDOW - 1, 2)
    cv = pltpu.roll(cv_ref[...], WINDOW - 1, 2)
    k_new = jnp.stack([jnp.where(last, kn[:, b:b + 1], ck[b]) for b in range(bt)])
    v_new = jnp.stack([jnp.where(last, vn[:, b:b + 1], cv[b]) for b in range(bt)])
    ok_ref[...] = k_new
    ov_ref[...] = v_new
    qp = qp_ref[...].astype(BF16)
    k_w = k_new.astype(BF16)
    v_w = v_new.astype(BF16)
    k_m = mk_ref[...].astype(BF16)
    v_m = mv_ref[...].astype(BF16)
    s_w = jnp.einsum("bhc,bcw->bhw", qp, k_w, preferred_element_type=F32)
    s_m = jnp.einsum("bhc,bkc->bhk", qp, k_m, preferred_element_type=F32)
    sink = sink_ref[...][None]
    m = jnp.maximum(jnp.maximum(jnp.max(s_w, axis=-1, keepdims=True),
                                jnp.max(s_m, axis=-1, keepdims=True)), sink)
    p_w = jnp.exp(s_w - m)
    p_m = jnp.exp(s_m - m)
    den = (jnp.sum(p_w, axis=-1, keepdims=True) + jnp.sum(p_m, axis=-1, keepdims=True)
           + jnp.exp(sink - m))
    inv = 1.0 / den
    o = (jnp.einsum("bhw,bcw->bhc", (p_w * inv).astype(BF16), v_w, preferred_element_type=F32)
         + jnp.einsum("bhk,bkc->bhc", (p_m * inv).astype(BF16), v_m,
                      preferred_element_type=F32))
    hh = lax.broadcasted_iota(jnp.int32, (1, N_Q_HEADS, KV_WIDTH), 1) // (N_Q_HEADS // N_KV_HEADS)
    cc = lax.broadcasted_iota(jnp.int32, (1, N_Q_HEADS, KV_WIDTH), 2) // HEAD_DIM
    o = jnp.where(hh == cc, o, 0.0)
    o = o[:, :, :LANES] + o[:, :, LANES:]
    o_ref[...] = o + pltpu.roll(o, HEAD_DIM, 2)


def _sample_attention(qp, kn_t, vn_t, cache_k, cache_v, meta_k, meta_v, sink_col, bt):
    n = cache_k.shape[0]
    win_spec = pl.BlockSpec((bt, KV_WIDTH, WINDOW), lambda t: (t, 0, 0))
    new_spec = pl.BlockSpec((1, KV_WIDTH, bt), lambda t: (t, 0, 0))
    meta_spec = pl.BlockSpec((bt, N_META, KV_WIDTH), lambda t: (t, 0, 0))
    return pl.pallas_call(
        _sample_attn_kernel,
        grid=(n // bt,),
        in_specs=[
            pl.BlockSpec((bt, N_Q_HEADS, KV_WIDTH), lambda t: (t, 0, 0)),
            new_spec, new_spec, win_spec, win_spec, meta_spec, meta_spec,
            pl.BlockSpec((N_Q_HEADS, 1), lambda t: (0, 0)),
        ],
        out_specs=[
            pl.BlockSpec((bt, N_Q_HEADS, LANES), lambda t: (t, 0, 0)),
            win_spec, win_spec,
        ],
        out_shape=[
            jax.ShapeDtypeStruct((n, N_Q_HEADS, LANES), F32),
            jax.ShapeDtypeStruct(cache_k.shape, F32),
            jax.ShapeDtypeStruct(cache_v.shape, F32),
        ],
        compiler_params=pltpu.CompilerParams(
            dimension_semantics=("arbitrary",), vmem_limit_bytes=VMEM_LIMIT),
        name="sample_attn",
    )(qp, kn_t, vn_t, cache_k, cache_v, meta_k, meta_v, sink_col)


def _gelu_tanh(x):
    return 0.5 * x * (1.0 + jnp.tanh(0.7978845608028654 * (x + 0.044715 * (x * x * x))))


def _ssm_kernel(u_ref, um_ref, us_ref, hr_ref, hi_ref, kern_ref, win_ref, wst_ref, wc_ref,
                ar_ref, ai_ref, lr_ref, li_ref, d_ref,
                y_ref, sre_ref, sim_ref, ys_ref, nr_ref, ni_ref, t8_ref):
    zero = jnp.zeros((LANES, LANES), BF16)
    for j in range(CHUNK):
        for jp in range(CHUNK):
            t8_ref[j * LANES:(j + 1) * LANES, jp * LANES:(jp + 1) * LANES] = (
                kern_ref[0, jp - j] if jp >= j else zero)
    dsk = d_ref[...]
    us = us_ref[...]
    bu = jnp.dot(us.astype(BF16), win_ref[0, (CHUNK - 1) * LANES:, :],
                 preferred_element_type=F32)
    l_r, l_i = lr_ref[0], li_ref[0]
    h_r, h_i = hr_ref[...], hi_ref[...]
    n_r = l_r * h_r - l_i * h_i + bu[:, :STATE_TILE]
    n_i = l_r * h_i + l_i * h_r + bu[:, STATE_TILE:]
    nr_ref[...] = n_r
    ni_ref[...] = n_i
    hn = jnp.concatenate([n_r, n_i], axis=1).astype(BF16)
    ys_ref[...] = _gelu_tanh(
        lax.dot_general(hn, wc_ref[0], NT_DIMS, preferred_element_type=F32) + dsk * us)

    batch = sre_ref.shape[0]
    n_rows = u_ref.shape[0] // CHUNK
    nc = n_rows // batch
    ucat = jnp.concatenate([u_ref[pl.ds(j, n_rows, stride=CHUNK), :] for j in range(CHUNK)],
                           axis=1)
    ub = ucat.astype(BF16)
    win = win_ref[0]
    g = jnp.dot(ub, win, preferred_element_type=F32)

    n_mc = um_ref.shape[0] // CHUNK
    um = jnp.concatenate([um_ref[pl.ds(j, n_mc, stride=CHUNK), :] for j in range(CHUNK)], axis=1)
    gm = jnp.dot(um.astype(BF16), win, preferred_element_type=F32)
    a_r, a_i = ar_ref[0, 0:1, :], ai_ref[0, 0:1, :]
    s0r, s0i = gm[0:1, :STATE_TILE], gm[0:1, STATE_TILE:]
    for c in range(1, n_mc):
        s0r, s0i = (a_r * s0r - a_i * s0i + gm[c:c + 1, :STATE_TILE],
                    a_r * s0i + a_i * s0r + gm[c:c + 1, STATE_TILE:])

    row = lax.broadcasted_iota(jnp.int32, (nc, 1), 0)
    first = row == 0
    prev_r, prev_i = [], []
    for b in range(batch):
        gr = g[b * nc:(b + 1) * nc, :STATE_TILE]
        gi = g[b * nc:(b + 1) * nc, STATE_TILE:]
        gr = gr + jnp.where(first, a_r * s0r - a_i * s0i, 0.0)
        gi = gi + jnp.where(first, a_r * s0i + a_i * s0r, 0.0)
        for k in range(SCAN_LEVELS):
            d = 1 << k
            if d >= nc:
                break
            k_r, k_i = ar_ref[0, k:k + 1, :], ai_ref[0, k:k + 1, :]
            if d < CHUNK:
                keep = row >= d
                sh_r = jnp.where(keep, pltpu.roll(gr, d, 0), 0.0)
                sh_i = jnp.where(keep, pltpu.roll(gi, d, 0), 0.0)
            else:
                zero = jnp.zeros((d, STATE_TILE), F32)
                sh_r = jnp.concatenate([zero, gr[:nc - d]], axis=0)
                sh_i = jnp.concatenate([zero, gi[:nc - d]], axis=0)
            gr, gi = gr + (k_r * sh_r - k_i * sh_i), gi + (k_r * sh_i + k_i * sh_r)
        sre_ref[b, 0] = gr[nc - 1:nc, :]
        sim_ref[b, 0] = gi[nc - 1:nc, :]
        prev_r.append(jnp.where(first, s0r, pltpu.roll(gr, 1, 0)))
        prev_i.append(jnp.where(first, s0i, pltpu.roll(gi, 1, 0)))
    sp = jnp.concatenate([jnp.concatenate(prev_r, axis=0), jnp.concatenate(prev_i, axis=0)],
                         axis=1).astype(BF16)
    y2 = (jnp.dot(ub, t8_ref[...], preferred_element_type=F32)
          + lax.dot_general(sp, wst_ref[0], NT_DIMS, preferred_element_type=F32))
    for j in range(CHUNK):
        sl = slice(j * LANES, (j + 1) * LANES)
        y_ref[pl.ds(j, n_rows, stride=CHUNK), :] = _gelu_tanh(y2[:, sl] + dsk * ucat[:, sl])


def _ssm(h_p, h_s, h0r, h0i, kern, win, wst, wc, a_r, a_i, l_r, l_i, d_skip, batch, n_s):
    ub = COL_U // LANES
    rows = h_p.shape[0]
    wspec = pl.BlockSpec((1, CHUNK_WIDTH, CHUNK_WIDTH), lambda gt: (gt, 0, 0))
    aspec = pl.BlockSpec((1, SCAN_LEVELS, STATE_TILE), lambda gt: (gt, 0, 0))
    sspec = pl.BlockSpec((batch, 1, 1, STATE_TILE), lambda gt: (0, gt, 0, 0))
    st = pl.BlockSpec((n_s, STATE_TILE), lambda gt: (0, gt))
    lam = pl.BlockSpec((1, 1, STATE_TILE), lambda gt: (gt, 0, 0))
    return pl.pallas_call(
        _ssm_kernel,
        grid=(N_GROUP_TILES,),
        in_specs=[
            pl.BlockSpec((rows, LANES), lambda gt: (0, ub + gt)),
            pl.BlockSpec((N_META, LANES), lambda gt: (h_s.shape[0] // N_META - 1, ub + gt)),
            pl.BlockSpec((n_s, LANES), lambda gt: (0, ub + gt)),
            st, st,
            pl.BlockSpec((1, CHUNK, LANES, LANES), lambda gt: (gt, 0, 0, 0)),
            wspec, wspec,
            pl.BlockSpec((1, LANES, 2 * STATE_TILE), lambda gt: (gt, 0, 0)),
            aspec, aspec, lam, lam,
            pl.BlockSpec((1, LANES), lambda gt: (0, gt)),
        ],
        out_specs=[pl.BlockSpec((rows, LANES), lambda gt: (0, gt)), sspec, sspec,
                   pl.BlockSpec((n_s, LANES), lambda gt: (0, gt)), st, st],
        out_shape=[
            jax.ShapeDtypeStruct((rows, SSM_WIDTH), F32),
            jax.ShapeDtypeStruct((batch, N_GROUP_TILES, 1, STATE_TILE), F32),
            jax.ShapeDtypeStruct((batch, N_GROUP_TILES, 1, STATE_TILE), F32),
            jax.ShapeDtypeStruct((n_s, SSM_WIDTH), F32),
            jax.ShapeDtypeStruct((n_s, N_SSM_GROUPS * SSM_STATE), F32),
            jax.ShapeDtypeStruct((n_s, N_SSM_GROUPS * SSM_STATE), F32),
        ],
        scratch_shapes=[pltpu.VMEM((CHUNK_WIDTH, CHUNK_WIDTH), BF16)],
        compiler_params=pltpu.CompilerParams(
            dimension_semantics=("arbitrary",), vmem_limit_bytes=VMEM_LIMIT),
        name="ssm",
    )(h_p, h_s, h_s, h0r, h0i, kern, win, wst, wc, a_r, a_i, l_r, l_i, d_skip)


def _merge_kernel(o_ref, za_ref, zs_ref, ga_ref, gs_ref, y_ref, x_ref,
                  o2_ref, za2_ref, zs2_ref, ga2_ref, gs2_ref, y2_ref, x2_ref,
                  bglu_ref, wglu_ref, wa_ref, ws_ref, wo_ref, out_ref, out2_ref):
    def merge(o_r, za_r, zs_r, ga_r, gs_r, y_r, x_r, out_r):
        y = y_r[...]
        t = jnp.dot(y.astype(BF16), wglu_ref[...], preferred_element_type=F32) + bglu_ref[...]
        y = y * jax.nn.sigmoid(t)
        za = za_r[...]
        zs = zs_r[...]
        a_in = (o_r[...] * (za * jax.nn.sigmoid(za))).astype(BF16)
        s_in = (y * (zs * jax.nn.sigmoid(zs))).astype(BF16)
        br_a = jnp.dot(a_in, wa_ref[...], preferred_element_type=F32)
        br_s = jnp.dot(s_in, ws_ref[...], preferred_element_type=F32)
        mix = (jax.nn.sigmoid(ga_r[...]) * br_a + jax.nn.sigmoid(gs_r[...]) * br_s).astype(BF16)
        out_r[...] = x_r[...] + jnp.dot(mix, wo_ref[...], preferred_element_type=F32)

    merge(o_ref, za_ref, zs_ref, ga_ref, gs_ref, y_ref, x_ref, out_ref)

    @pl.when(pl.program_id(0) == pl.num_programs(0) - 1)
    def _():
        merge(o2_ref, za2_ref, zs2_ref, ga2_ref, gs2_ref, y2_ref, x2_ref, out2_ref)


def _merge(o, h, y, x, o2, h2, y2, x2, b_glu, w_glu, w_a, w_s, w_o, tm):
    m, m2 = x.shape[0], x2.shape[0]
    const = lambda shape: pl.BlockSpec(shape, lambda i: (0, 0), pipeline_mode=pl.Buffered(1))
    rows = lambda width, col: pl.BlockSpec((tm, width), lambda i: (i, col))
    rows2 = lambda width, col: pl.BlockSpec((m2, width), lambda i: (0, col),
                                            pipeline_mode=pl.Buffered(1))
    operands = lambda spec: [
        spec(ATTN_WIDTH, 0), spec(ATTN_WIDTH, COL_ZA // ATTN_WIDTH),
        spec(SSM_WIDTH, COL_ZS // SSM_WIDTH), spec(D_MODEL, COL_GA // D_MODEL),
        spec(D_MODEL, COL_GS // D_MODEL), spec(SSM_WIDTH, 0), spec(D_MODEL, 0)]
    return pl.pallas_call(
        _merge_kernel,
        grid=(m // tm,),
        in_specs=operands(rows) + operands(rows2) + [
            const((1, SSM_WIDTH)),
            const((SSM_WIDTH, SSM_WIDTH)),
            const((ATTN_WIDTH, D_MODEL)),
            const((SSM_WIDTH, D_MODEL)),
            const((D_MODEL, D_MODEL)),
        ],
        out_specs=[pl.BlockSpec((tm, D_MODEL), lambda i: (i, 0)),
                   pl.BlockSpec((m2, D_MODEL), lambda i: (0, 0))],
        out_shape=[jax.ShapeDtypeStruct((m, D_MODEL), F32),
                   jax.ShapeDtypeStruct((m2, D_MODEL), F32)],
        compiler_params=pltpu.CompilerParams(
            dimension_semantics=("arbitrary",), vmem_limit_bytes=VMEM_LIMIT),
        name="merge",
    )(o, h, h, h, h, y, x, o2, h2, h2, h2, h2, y2, x2, b_glu, w_glu, w_a, w_s, w_o)


def _rope_tables(pos):
    half = HEAD_DIM // 2
    inv_freq = (ROPE_THETA ** (-np.arange(half, dtype=np.float32) / half)).astype(np.float32)
    ang = np.asarray(pos, np.float32)[:, None] * inv_freq[None, :]
    cos, sin = np.cos(ang).astype(np.float32), np.sin(ang).astype(np.float32)
    zero = np.zeros_like(sin)
    reps = LANES // HEAD_DIM
    cos_t = np.tile(np.concatenate([cos, cos], axis=1), (1, reps))
    sin_a = np.tile(np.concatenate([-sin, zero], axis=1), (1, reps))
    sin_b = np.tile(np.concatenate([zero, sin], axis=1), (1, reps))
    return jnp.asarray(cos_t), jnp.asarray(sin_a), jnp.asarray(sin_b)


def _split_bf16(x):
    hi = x.astype(BF16)
    return hi, (x - hi.astype(F32)).astype(BF16)


def _dot_nt_f32(a, b):
    ah, al = _split_bf16(a)
    bh, bl = _split_bf16(b)
    dot = lambda x, y: lax.dot_general(x, y, NT_DIMS, preferred_element_type=F32)
    return dot(ah, bh) + dot(ah, bl) + dot(al, bh)


def _ssm_tables_kernel(are_ref, aim_ref, ldt_ref, bre_ref, bim_ref, cre_ref, cim_ref,
                       win_ref, wst_ref, kern_ref, wc_ref, ar_ref, ai_ref, lr_ref, li_ref):
    a_r, a_i = are_ref[0], aim_ref[0]
    dt = jnp.exp(ldt_ref[0])
    mag = jnp.exp(a_r * dt)
    l_r, l_i = mag * jnp.cos(a_i * dt), mag * jnp.sin(a_i * dt)
    den = a_r * a_r + a_i * a_i
    f_r = ((l_r - 1.0) * a_r + l_i * a_i) / den
    f_i = (l_i * a_r - (l_r - 1.0) * a_i) / den
    shape = (LANES, STATE_TILE)
    own = (lax.broadcasted_iota(jnp.int32, shape, 0) // SSM_GROUP
           == lax.broadcasted_iota(jnp.int32, shape, 1) // SSM_STATE)
    sq = (LANES, LANES)
    same_group = (lax.broadcasted_iota(jnp.int32, sq, 0) // SSM_GROUP
                  == lax.broadcasted_iota(jnp.int32, sq, 1) // SSM_GROUP)
    lo = lax.broadcasted_iota(jnp.int32, (1, LANES), 1) < SSM_STATE
    n_rep = STATE_TILE // LANES

    def expand(x):
        return jnp.where(own, jnp.concatenate([x] * n_rep, axis=1), 0.0)

    def compact(v):
        e = jnp.where(own, jnp.broadcast_to(v, shape), 0.0)
        s = e[:, :LANES]
        for k in range(1, n_rep):
            s = s + e[:, k * LANES:(k + 1) * LANES]
        return s + pltpu.roll(s, SSM_STATE, 1)

    lc_r, lc_i = compact(l_r), compact(l_i)
    b_r, b_i = bre_ref[0], bim_ref[0]
    c_r, c_i = cre_ref[0], cim_ref[0]
    bb_r, bb_i = _cmul(compact(f_r), compact(f_i), b_r, b_i)
    wc_ref[0] = jnp.concatenate([expand(c_r), expand(-c_i)], axis=1).astype(BF16)
    c_cat = jnp.where(lo, c_r, -c_i)
    q_r, q_i = jnp.ones(sq, F32), jnp.zeros(sq, F32)
    for n in range(CHUNK):
        x_r, x_i = _cmul(q_r, q_i, bb_r, bb_i)
        win_ref[0, (CHUNK - 1 - n) * LANES:(CHUNK - n) * LANES, :] = jnp.concatenate(
            [expand(x_r), expand(x_i)], axis=1).astype(BF16)
        k_n = _dot_nt_f32(jnp.where(lo, x_r, x_i), c_cat)
        kern_ref[0, n] = jnp.where(same_group, k_n, 0.0).astype(BF16)
        q_r, q_i = _cmul(q_r, q_i, lc_r, lc_i)
        s_r, s_i = _cmul(c_r, c_i, q_r, q_i)
        wst_ref[0, n * LANES:(n + 1) * LANES, :] = jnp.concatenate(
            [expand(s_r), expand(-s_i)], axis=1).astype(BF16)
    p_r, p_i = l_r, l_i
    for _ in range(CHUNK.bit_length() - 1):
        p_r, p_i = _cmul(p_r, p_i, p_r, p_i)
    for k in range(SCAN_LEVELS):
        ar_ref[0, k:k + 1, :] = p_r
        ai_ref[0, k:k + 1, :] = p_i
        p_r, p_i = _cmul(p_r, p_i, p_r, p_i)
    lr_ref[0] = l_r
    li_ref[0] = l_i


def _ssm_tables(a_re, a_im, log_dt, b_re, b_im, c_re, c_im):
    gt = N_GROUP_TILES
    lane_vec = lambda v: v.reshape(gt, 1, STATE_TILE)
    dup = lambda v: jnp.concatenate([v, v], axis=-1)
    rows = GROUPS_PER_TILE * SSM_GROUP
    b_t = lambda v: dup(v.transpose(0, 2, 1).reshape(gt, rows, SSM_STATE))
    c_t = lambda v: dup(v.reshape(gt, rows, SSM_STATE))
    vec = pl.BlockSpec((1, 1, STATE_TILE), lambda g: (g, 0, 0))
    par = pl.BlockSpec((1, rows, LANES), lambda g: (g, 0, 0))
    big = pl.BlockSpec((1, CHUNK_WIDTH, CHUNK_WIDTH), lambda g: (g, 0, 0))
    wcs = pl.BlockSpec((1, LANES, 2 * STATE_TILE), lambda g: (g, 0, 0))
    scan = pl.BlockSpec((1, SCAN_LEVELS, STATE_TILE), lambda g: (g, 0, 0))
    return pl.pallas_call(
        _ssm_tables_kernel,
        grid=(gt,),
        in_specs=[vec, vec, vec, par, par, par, par],
        out_specs=[big, big, pl.BlockSpec((1, CHUNK, LANES, LANES), lambda g: (g, 0, 0, 0)),
                   wcs, scan, scan, vec, vec],
        out_shape=[
            jax.ShapeDtypeStruct((gt, CHUNK_WIDTH, 2 * STATE_TILE), BF16),
            jax.ShapeDtypeStruct((gt, CHUNK_WIDTH, 2 * STATE_TILE), BF16),
            jax.ShapeDtypeStruct((gt, CHUNK, LANES, LANES), BF16),
            jax.ShapeDtypeStruct((gt, LANES, 2 * STATE_TILE), BF16),
            jax.ShapeDtypeStruct((gt, SCAN_LEVELS, STATE_TILE), F32),
            jax.ShapeDtypeStruct((gt, SCAN_LEVELS, STATE_TILE), F32),
            jax.ShapeDtypeStruct((gt, 1, STATE_TILE), F32),
            jax.ShapeDtypeStruct((gt, 1, STATE_TILE), F32),
        ],
        compiler_params=pltpu.CompilerParams(
            dimension_semantics=("arbitrary",), vmem_limit_bytes=VMEM_LIMIT),
        name="ssm_tables",
    )(lane_vec(a_re), lane_vec(a_im), lane_vec(jnp.repeat(log_dt, SSM_STATE)),
      b_t(b_re), b_t(b_im), c_t(c_re), c_t(c_im))


def kernel(x_prompt, x_sample, cache_win_k, cache_win_v, cache_meta_k, cache_meta_v,
           state_ssm_re, state_ssm_im, meta_tokens, norm_gain, w_in, q_norm_gain, k_norm_gain,
           sinks, a_re, a_im, log_dt, b_re, b_im, c_re, c_im, d_skip, w_glu, b_glu,
           w_attn_out, w_ssm_out, w_out):
    depth = w_in.shape[0]
    assert depth == 1, "single-layer trunk"
    batch, seq = x_prompt.shape[:2]
    n_s = x_sample.shape[0]
    assert x_sample.shape[1] == 1 and seq % (CHUNK * BLOCK) == 0
    l = 0

    w = w_in[l]
    qg = jnp.tile(q_norm_gain[l], TN // HEAD_DIM)[None]
    kg = jnp.tile(k_norm_gain[l], KV_WIDTH // HEAD_DIM)[None]
    lane = np.arange(LANES)
    e128 = np.where((lane[:, None] // HEAD_DIM) == (lane[None, :] // HEAD_DIM),
                    1.0 / HEAD_DIM, 0.0).astype(np.float32)
    e = jnp.asarray(np.concatenate([e128, e128], axis=0), BF16)
    gain = norm_gain[l][None]

    pos_p = N_META + np.arange(seq)
    pos_small = np.concatenate([np.full((n_s,), PAST_LEN), np.arange(N_META)])
    x_small = jnp.concatenate([x_sample[:, 0, :], meta_tokens.astype(x_prompt.dtype)], axis=0)
    xp = x_prompt.reshape(batch * seq, D_MODEL)
    xn_s = _rmsnorm(x_small, gain, tm=n_s + N_META)
    h_p, h_s, w_glu_bf, w_a_bf, w_s_bf, w_o_bf = _inproj(
        xp, gain, xn_s, w, qg, kg, e, _rope_tables(pos_p), _rope_tables(pos_small),
        (w_glu[l], w_attn_out[l], w_ssm_out[l], w_out[l]), tm=1024, halves=2)

    win, wst, kern, wc, a_r, a_i, l_r, l_i = _ssm_tables(
        a_re[l], a_im[l], log_dt[l], b_re[l], b_im[l], c_re[l], c_im[l])
    dsk = d_skip[l][None]

    vm_t = h_s[n_s:, COL_KV + KV_WIDTH:IN_WIDTH].T
    o_p = _prompt_attention(sinks[l], h_p, h_s, vm_t, batch, seq)
    q_s = h_s[:n_s, COL_Q:COL_Q + ATTN_WIDTH].reshape(n_s, N_Q_HEADS, 1, HEAD_DIM)
    sel = (np.arange(N_Q_HEADS)[:, None] // (N_Q_HEADS // N_KV_HEADS)
           == np.arange(N_KV_HEADS)[None, :]).astype(np.float32)
    qp = (q_s * sel[None, :, :, None]).reshape(n_s, N_Q_HEADS, KV_WIDTH)
    bt = 8
    to_t = lambda c: c.transpose(0, 2, 3, 1).reshape(n_s, KV_WIDTH, WINDOW)
    from_t = lambda c: c.reshape(n_s, N_KV_HEADS, HEAD_DIM, WINDOW).transpose(0, 3, 1, 2)[None]
    new_t = lambda c0: h_s[:n_s, c0:c0 + KV_WIDTH].reshape(n_s // bt, bt, KV_WIDTH).transpose(0, 2, 1)
    o_s, s_win_k, s_win_v = _sample_attention(
        qp, new_t(COL_KV), new_t(COL_KV + KV_WIDTH), to_t(cache_win_k[l]), to_t(cache_win_v[l]),
        cache_meta_k[l].reshape(n_s, N_META, KV_WIDTH), cache_meta_v[l].reshape(n_s, N_META, KV_WIDTH),
        sinks[l][:, None], bt=bt)
    o_s = o_s[:, :, :HEAD_DIM].reshape(n_s, ATTN_WIDTH)

    y_p, p_re, p_im, y_s, s_re, s_im = _ssm(
        h_p, h_s, state_ssm_re[l].reshape(n_s, -1), state_ssm_im[l].reshape(n_s, -1),
        kern, win, wst, wc, a_r, a_i, l_r, l_i, dsk, batch, n_s)
    y_prompt, y_sample = _merge(
        o_p, h_p, y_p, xp, o_s, h_s, y_s, x_sample[:, 0, :], b_glu[l][None],
        w_glu_bf, w_a_bf, w_s_bf, w_o_bf, tm=256)
    y_prompt = y_prompt.reshape(batch, seq, D_MODEL)
    y_sample = y_sample.reshape(n_s, 1, D_MODEL)

    kv_p = h_p.reshape(batch, seq, IN_WIDTH)[:, seq - WINDOW:, COL_KV:]
    kv_p = kv_p.reshape(batch, WINDOW, 2, N_KV_HEADS, HEAD_DIM)
    p_win_k = kv_p[:, :, 0][None]
    p_win_v = kv_p[:, :, 1][None]
    kv_m = h_s[n_s:, COL_KV:IN_WIDTH].reshape(N_META, 2, N_KV_HEADS, HEAD_DIM)
    p_meta_k = jnp.broadcast_to(kv_m[None, :, 0], (batch, N_META, N_KV_HEADS, HEAD_DIM))[None]
    p_meta_v = jnp.broadcast_to(kv_m[None, :, 1], (batch, N_META, N_KV_HEADS, HEAD_DIM))[None]
    p_ssm_re = p_re.reshape(batch, N_SSM_GROUPS, SSM_STATE)[None]
    p_ssm_im = p_im.reshape(batch, N_SSM_GROUPS, SSM_STATE)[None]
    st_shape = (1, n_s, N_SSM_GROUPS, SSM_STATE)
    return (y_prompt, y_sample, p_win_k, p_win_v, p_meta_k, p_meta_v, p_ssm_re, p_ssm_im,
            from_t(s_win_k), from_t(s_win_v), s_re.reshape(st_shape), s_im.reshape(st_shape))
```

```python
import functools

import jax
import jax.numpy as jnp
import numpy as np
from jax import lax
from jax.experimental import pallas as pl
from jax.experimental.pallas import tpu as pltpu

F32 = jnp.float32
BF16 = jnp.bfloat16

D_MODEL = 2048
N_META = 16
HEAD_DIM = 64
N_Q_HEADS = 16
N_KV_HEADS = 4
ATTN_WIDTH = N_Q_HEADS * HEAD_DIM
KV_WIDTH = N_KV_HEADS * HEAD_DIM
WINDOW = 128
BLOCK = 128
ROPE_THETA = 10000.0
SSM_WIDTH = D_MODEL // 2
SSM_GROUP = 16
N_SSM_GROUPS = SSM_WIDTH // SSM_GROUP
SSM_STATE = 64
EPS = 1e-6
PAST_LEN = 8192

LANES = 128
SUBLANES = 8
CHUNK = 8
GROUPS_PER_TILE = LANES // SSM_GROUP
N_GROUP_TILES = N_SSM_GROUPS // GROUPS_PER_TILE
CHUNK_WIDTH = CHUNK * LANES
STATE_TILE = GROUPS_PER_TILE * SSM_STATE
SCAN_LEVELS = 8

COL_Q = 0
COL_ZA = 1024
COL_U = 2048
COL_ZS = 3072
COL_GA = 4096
COL_GS = 6144
COL_KV = 8192
IN_WIDTH = 8704
TN = 512
CAST_ROWS = 16
KV_TILE = COL_KV // TN
VMEM_LIMIT = 56 * 1024 * 1024
NEG = -1e30
NT_DIMS = (((1,), (1,)), ((), ()))


def _cmul(ar, ai, br, bi):
    return ar * br - ai * bi, ar * bi + ai * br


def _rmsnorm_kernel(x_ref, gain_ref, o_ref):
    x = x_ref[...]
    r = lax.rsqrt(jnp.mean(x * x, axis=-1, keepdims=True) + EPS)
    o_ref[...] = (x * r * gain_ref[...]).astype(BF16)


def _rmsnorm(x, gain, tm):
    m = x.shape[0]
    return pl.pallas_call(
        _rmsnorm_kernel,
        grid=(m // tm,),
        in_specs=[pl.BlockSpec((tm, D_MODEL), lambda i: (i, 0)),
                  pl.BlockSpec((1, D_MODEL), lambda i: (0, 0))],
        out_specs=pl.BlockSpec((tm, D_MODEL), lambda i: (i, 0)),
        out_shape=jax.ShapeDtypeStruct((m, D_MODEL), BF16),
        compiler_params=pltpu.CompilerParams(
            dimension_semantics=("arbitrary",), vmem_limit_bytes=VMEM_LIMIT),
        name="rmsnorm",
    )(x, gain)


def _inproj_epilogue(j, o_ref, qg_ref, kg_ref, e_ref, cos_ref, sa_ref, sb_ref):
    def head_norm_rope(n_chunks, gain_ref, scale):
        cos, sa, sb = cos_ref[...], sa_ref[...], sb_ref[...]
        for c in range(n_chunks):
            cs = slice(c * LANES, (c + 1) * LANES)
            ac = o_ref[:, cs]
            sq = ac * ac
            hi = sq.astype(BF16)
            lo = (sq - hi.astype(F32)).astype(BF16)
            ms = jnp.dot(jnp.concatenate([hi, lo], axis=1), e_ref[...],
                         preferred_element_type=F32)
            xc = ac * lax.rsqrt(ms + EPS) * gain_ref[:, cs]
            o_ref[:, cs] = (xc * cos + pltpu.roll(xc, LANES - HEAD_DIM // 2, 1) * sa
                            + pltpu.roll(xc, HEAD_DIM // 2, 1) * sb) * scale

    @pl.when(j < COL_ZA // TN)
    def _():
        head_norm_rope(TN // LANES, qg_ref, HEAD_DIM ** -0.5)

    @pl.when(j == KV_TILE)
    def _():
        head_norm_rope(KV_WIDTH // LANES, kg_ref, 1.0)


def _inproj_kernel(x_ref, gain_ref, xs_ref, w_ref, qg_ref, kg_ref, e_ref, cos_ref, sa_ref, sb_ref,
                   cos_s_ref, sa_s_ref, sb_s_ref, *rest, mw_slabs):
    n_mw = (len(rest) - 4) // 2
    mw_refs, (o_ref, os_ref) = rest[:n_mw], rest[n_mw:n_mw + 2]
    mw_out_refs, (xn_ref, wbf_ref) = rest[n_mw + 2:2 * n_mw + 2], rest[2 * n_mw + 2:]
    h, j, i = pl.program_id(0), pl.program_id(1), pl.program_id(2)
    tm = o_ref.shape[0]
    rows = pl.ds(pl.multiple_of(i * tm, tm), tm)

    @pl.when(j == 0)
    def _():
        x = x_ref[...]
        r = lax.rsqrt(jnp.mean(x * x, axis=-1, keepdims=True) + EPS)
        xn_ref[rows, :] = (x * r * gain_ref[...]).astype(BF16)

    @pl.when(i == 0)
    def _():
        wbf_ref[...] = w_ref[...].astype(BF16)

    o_ref[...] = jnp.dot(xn_ref[rows, :], wbf_ref[...], preferred_element_type=F32)
    _inproj_epilogue(j, o_ref, qg_ref, kg_ref, e_ref, cos_ref, sa_ref, sb_ref)

    step = (h * pl.num_programs(1) + j) * pl.num_programs(2) + i
    for src, dst, n_slabs in zip(mw_refs, mw_out_refs, mw_slabs):
        @pl.when(step < n_slabs)
        def _():
            dst[...] = src[...].astype(BF16)

    @pl.when(jnp.logical_and(h == 0, jnp.logical_and(j == 0, i == 0)))
    def _():
        os_ref[...] = jnp.zeros(os_ref.shape, F32)

    @pl.when(jnp.logical_and(h == pl.num_programs(0) - 1, i == pl.num_programs(2) - 1))
    def _():
        os_ref[...] = jnp.dot(xs_ref[...], wbf_ref[...], preferred_element_type=F32)
        _inproj_epilogue(j, os_ref, qg_ref, kg_ref, e_ref, cos_s_ref, sa_s_ref, sb_s_ref)


def _w_tile(j):
    n_q = COL_ZA // TN
    return jnp.where(j < n_q, j, jnp.where(j == KV_TILE, n_q, j + 1))


def _inproj(x, gain, xs, w, qg, kg, e, tabs_p, tabs_s, merge_weights, tm, halves):
    m, ms = x.shape[0], xs.shape[0]
    rows_half = m // halves
    tiles = rows_half // tm
    n_tab = tabs_p[0].shape[0] // tm
    n_col = IN_WIDTH // TN
    const = lambda shape: pl.BlockSpec(shape, lambda h, j, i: (0, 0))
    tab_p = pl.BlockSpec((tm, LANES), lambda h, j, i: ((h * tiles + i) % n_tab, 0))
    n_steps = halves * n_col * tiles
    for mw in merge_weights:
        assert mw.shape[0] % CAST_ROWS == 0 and mw.shape[0] // CAST_ROWS <= n_steps
    slab = lambda mw: pl.BlockSpec(
        (CAST_ROWS, mw.shape[1]),
        lambda h, j, i: (jnp.minimum((h * n_col + j) * tiles + i, mw.shape[0] // CAST_ROWS - 1), 0))
    return pl.pallas_call(
        functools.partial(_inproj_kernel,
                          mw_slabs=tuple(mw.shape[0] // CAST_ROWS for mw in merge_weights)),
        grid=(halves, IN_WIDTH // TN, tiles),
        in_specs=[
            pl.BlockSpec((tm, D_MODEL),
                         lambda h, j, i: (h * tiles + jnp.where(j == 0, i, tiles - 1), 0)),
            const((1, D_MODEL)),
            const((ms, D_MODEL)),
            pl.BlockSpec((D_MODEL, TN), lambda h, j, i: (0, _w_tile(j))),
            const((1, TN)), const((1, KV_WIDTH)), const((2 * LANES, LANES)),
            tab_p, tab_p, tab_p,
            const((ms, LANES)), const((ms, LANES)), const((ms, LANES)),
        ] + [slab(mw) for mw in merge_weights],
        out_specs=[pl.BlockSpec((tm, TN), lambda h, j, i: (h * tiles + i, j)),
                   pl.BlockSpec((ms, TN),
                                lambda h, j, i: (0, jnp.where(h == halves - 1, j, n_col)))
                   ] + [slab(mw) for mw in merge_weights],
        out_shape=[jax.ShapeDtypeStruct((m, IN_WIDTH), F32),
                   jax.ShapeDtypeStruct((ms, IN_WIDTH + TN), F32)
                   ] + [jax.ShapeDtypeStruct(mw.shape, BF16) for mw in merge_weights],
        scratch_shapes=[pltpu.VMEM((rows_half, D_MODEL), BF16), pltpu.VMEM((D_MODEL, TN), BF16)],
        compiler_params=pltpu.CompilerParams(
            dimension_semantics=("arbitrary", "arbitrary", "arbitrary"),
            vmem_limit_bytes=VMEM_LIMIT),
        name="inproj",
    )(x, gain, xs, w, qg, kg, e, *tabs_p, *tabs_s, *merge_weights)


def _dup_head(x, g, lo):
    r = pltpu.roll(x, HEAD_DIM, 1)
    return jnp.where(lo, x, r) if g % 2 == 0 else jnp.where(lo, r, x)


def _prompt_attn_kernel(sinks_ref, q_ref, kvc_ref, kvp_ref, kvm_ref, vmt_ref, bias_ref, o_ref):
    lo = lax.broadcasted_iota(jnp.int32, (1, LANES), 1) < HEAD_DIM
    head_lane = lax.broadcasted_iota(jnp.int32, (1, 4 * BLOCK), 1) // BLOCK
    bias = bias_ref[0]
    v_t = jnp.concatenate([kvp_ref[:, KV_WIDTH:].T, kvc_ref[:, KV_WIDTH:].T], axis=1)
    for g in range(N_KV_HEADS):
        ks = slice((g // 2) * LANES, (g // 2 + 1) * LANES)
        k_all = jnp.concatenate([_dup_head(kvp_ref[:, ks], g, lo), _dup_head(kvc_ref[:, ks], g, lo),
                                 _dup_head(kvm_ref[:, ks], g, lo)], axis=0).astype(BF16)
        qa = q_ref[:, (2 * g) * LANES:(2 * g + 1) * LANES]
        qb = q_ref[:, (2 * g + 1) * LANES:(2 * g + 2) * LANES]
        q4 = jnp.concatenate([jnp.where(lo, qa, 0.0), jnp.where(lo, 0.0, qa),
                              jnp.where(lo, qb, 0.0), jnp.where(lo, 0.0, qb)],
                             axis=0).astype(BF16)
        s = lax.dot_general(k_all, q4, NT_DIMS, preferred_element_type=F32)
        s_w = s[:2 * BLOCK] + bias
        s_m = s[2 * BLOCK:]
        sink = jnp.where(head_lane == 0, sinks_ref[4 * g],
                         jnp.where(head_lane == 1, sinks_ref[4 * g + 1],
                                   jnp.where(head_lane == 2, sinks_ref[4 * g + 2],
                                             sinks_ref[4 * g + 3])))
        m = jnp.maximum(jnp.maximum(jnp.max(s_w, axis=0, keepdims=True),
                                    jnp.max(s_m, axis=0, keepdims=True)), sink)
        p_w = jnp.exp(s_w - m)
        p_m = jnp.exp(s_m - m)
        den = (jnp.sum(p_w, axis=0, keepdims=True) + jnp.sum(p_m, axis=0, keepdims=True)
               + jnp.exp(sink - m))
        inv = 1.0 / den
        p = jnp.concatenate([p_w * inv, p_m * inv], axis=0).astype(BF16)
        hs = slice(g * HEAD_DIM, (g + 1) * HEAD_DIM)
        vt_g = jnp.concatenate([v_t[hs], vmt_ref[hs, :]], axis=1).astype(BF16)
        o_t = jnp.dot(vt_g, p, preferred_element_type=F32)
        for pair in range(2):
            two = jnp.concatenate([o_t[:, (2 * pair) * BLOCK:(2 * pair + 1) * BLOCK],
                                   o_t[:, (2 * pair + 1) * BLOCK:(2 * pair + 2) * BLOCK]], axis=0)
            o_ref[:, (2 * g + pair) * LANES:(2 * g + pair + 1) * LANES] = two.T


def _prompt_attention(sinks, h_p, h_s, vm_t, batch, seq):
    nb = seq // BLOCK
    kvb = COL_KV // (2 * KV_WIDTH)
    kj = np.arange(2 * BLOCK)[:, None]
    qi = np.arange(4 * BLOCK)[None, :] % BLOCK
    cur = (kj >= BLOCK) & (kj - BLOCK <= qi)
    prev = (kj < BLOCK) & (kj > qi)
    bias = jnp.asarray(np.where(np.stack([cur, cur | prev]), 0.0, NEG), F32)
    return pl.pallas_call(
        _prompt_attn_kernel,
        grid=(batch, nb),
        in_specs=[
            pl.BlockSpec(memory_space=pltpu.SMEM),
            pl.BlockSpec((BLOCK, ATTN_WIDTH), lambda b, i: (b * nb + i, 0)),
            pl.BlockSpec((BLOCK, 2 * KV_WIDTH), lambda b, i: (b * nb + i, kvb)),
            pl.BlockSpec((BLOCK, 2 * KV_WIDTH),
                         lambda b, i: (b * nb + jnp.maximum(i - 1, 0), kvb)),
            pl.BlockSpec((N_META, 2 * KV_WIDTH), lambda b, i: (h_s.shape[0] // N_META - 1, kvb)),
            pl.BlockSpec((KV_WIDTH, N_META), lambda b, i: (0, 0)),
            pl.BlockSpec((1, 2 * BLOCK, 4 * BLOCK), lambda b, i: (jnp.minimum(i, 1), 0, 0)),
        ],
        out_specs=pl.BlockSpec((BLOCK, ATTN_WIDTH), lambda b, i: (b * nb + i, 0)),
        out_shape=jax.ShapeDtypeStruct((batch * seq, ATTN_WIDTH), F32),
        compiler_params=pltpu.CompilerParams(
            dimension_semantics=("arbitrary", "arbitrary"), vmem_limit_bytes=VMEM_LIMIT),
        name="prompt_attn",
    )(sinks, h_p, h_p, h_p, h_s, vm_t, bias)


def _sample_attn_kernel(qp_ref, kn_ref, vn_ref, ck_ref, cv_ref, mk_ref, mv_ref, sink_ref,
                        o_ref, ok_ref, ov_ref):
    bt = ck_ref.shape[0]
    last = lax.broadcasted_iota(jnp.int32, (1, WINDOW), 1) == WINDOW - 1
    kn = kn_ref[0]
    vn = vn_ref[0]
    ck = pltpu.roll(ck_ref[...], WINDOW - 1, 2)
    cv = pltpu.roll(cv_ref[...], WINDOW - 1, 2)
    k_new = jnp.stack([jnp.where(last, kn[:, b:b + 1], ck[b]) for b in range(bt)])
    v_new = jnp.stack([jnp.where(last, vn[:, b:b + 1], cv[b]) for b in range(bt)])
    ok_ref[...] = k_new
    ov_ref[...] = v_new
    qp = qp_ref[...].astype(BF16)
    k_w = k_new.astype(BF16)
    v_w = v_new.astype(BF16)
    k_m = mk_ref[...].astype(BF16)
    v_m = mv_ref[...].astype(BF16)
    s_w = jnp.einsum("bhc,bcw->bhw", qp, k_w, preferred_element_type=F32)
    s_m = jnp.einsum("bhc,bkc->bhk", qp, k_m, preferred_element_type=F32)
    sink = sink_ref[...][None]
    m = jnp.maximum(jnp.maximum(jnp.max(s_w, axis=-1, keepdims=True),
                                jnp.max(s_m, axis=-1, keepdims=True)), sink)
    p_w = jnp.exp(s_w - m)
    p_m = jnp.exp(s_m - m)
    den = (jnp.sum(p_w, axis=-1, keepdims=True) + jnp.sum(p_m, axis=-1, keepdims=True)
           + jnp.exp(sink - m))
    inv = 1.0 / den
    o = (jnp.einsum("bhw,bcw->bhc", (p_w * inv).astype(BF16), v_w, preferred_element_type=F32)
         + jnp.einsum("bhk,bkc->bhc", (p_m * inv).astype(BF16), v_m,
                      preferred_element_type=F32))
    hh = lax.broadcasted_iota(jnp.int32, (1, N_Q_HEADS, KV_WIDTH), 1) // (N_Q_HEADS // N_KV_HEADS)
    cc = lax.broadcasted_iota(jnp.int32, (1, N_Q_HEADS, KV_WIDTH), 2) // HEAD_DIM
    o = jnp.where(hh == cc, o, 0.0)
    o = o[:, :, :LANES] + o[:, :, LANES:]
    o_ref[...] = o + pltpu.roll(o, HEAD_DIM, 2)


def _sample_attention(qp, kn_t, vn_t, cache_k, cache_v, meta_k, meta_v, sink_col, bt):
    n = cache_k.shape[0]
    win_spec = pl.BlockSpec((bt, KV_WIDTH, WINDOW), lambda t: (t, 0, 0))
    new_spec = pl.BlockSpec((1, KV_WIDTH, bt), lambda t: (t, 0, 0))
    meta_spec = pl.BlockSpec((bt, N_META, KV_WIDTH), lambda t: (t, 0, 0))
    return pl.pallas_call(
        _sample_attn_kernel,
        grid=(n // bt,),
        in_specs=[
            pl.BlockSpec((bt, N_Q_HEADS, KV_WIDTH), lambda t: (t, 0, 0)),
            new_spec, new_spec, win_spec, win_spec, meta_spec, meta_spec,
            pl.BlockSpec((N_Q_HEADS, 1), lambda t: (0, 0)),
        ],
        out_specs=[
            pl.BlockSpec((bt, N_Q_HEADS, LANES), lambda t: (t, 0, 0)),
            win_spec, win_spec,
        ],
        out_shape=[
            jax.ShapeDtypeStruct((n, N_Q_HEADS, LANES), F32),
            jax.ShapeDtypeStruct(cache_k.shape, F32),
            jax.ShapeDtypeStruct(cache_v.shape, F32),
        ],
        compiler_params=pltpu.CompilerParams(
            dimension_semantics=("arbitrary",), vmem_limit_bytes=VMEM_LIMIT),
        name="sample_attn",
    )(qp, kn_t, vn_t, cache_k, cache_v, meta_k, meta_v, sink_col)


def _gelu_tanh(x):
    return 0.5 * x * (1.0 + jnp.tanh(0.7978845608028654 * (x + 0.044715 * (x * x * x))))


def _ssm_kernel(u_ref, um_ref, us_ref, hr_ref, hi_ref, t8_ref, win_ref, wst_ref, wc_ref,
                ar_ref, ai_ref, lr_ref, li_ref, pr_ref, pi_ref, d_ref,
                y_ref, sre_ref, sim_ref, ys_ref, nr_ref, ni_ref, xr_ref, xi_ref):
    dsk = d_ref[...]
    us = us_ref[...]
    bu = jnp.dot(us.astype(BF16), win_ref[0, (CHUNK - 1) * LANES:, :],
                 preferred_element_type=F32)
    l_r, l_i = lr_ref[0], li_ref[0]
    h_r, h_i = hr_ref[...], hi_ref[...]
    n_r = l_r * h_r - l_i * h_i + bu[:, :STATE_TILE]
    n_i = l_r * h_i + l_i * h_r + bu[:, STATE_TILE:]
    nr_ref[...] = n_r
    ni_ref[...] = n_i
    hn = jnp.concatenate([n_r, n_i], axis=1).astype(BF16)
    ys_ref[...] = _gelu_tanh(
        lax.dot_general(hn, wc_ref[0], NT_DIMS, preferred_element_type=F32) + dsk * us)

    batch = sre_ref.shape[0]
    n_rows = u_ref.shape[0] // CHUNK
    nc = n_rows // batch
    ucat = jnp.concatenate([u_ref[pl.ds(j, n_rows, stride=CHUNK), :] for j in range(CHUNK)],
                           axis=1)
    ub = ucat.astype(BF16)
    win = win_ref[0]
    g = jnp.dot(ub, win, preferred_element_type=F32)

    n_mc = um_ref.shape[0] // CHUNK
    um = jnp.concatenate([um_ref[pl.ds(j, n_mc, stride=CHUNK), :] for j in range(CHUNK)], axis=1)
    gm = jnp.dot(um.astype(BF16), win, preferred_element_type=F32)
    a_r, a_i = ar_ref[0, 0:1, :], ai_ref[0, 0:1, :]
    s0r, s0i = gm[0:1, :STATE_TILE], gm[0:1, STATE_TILE:]
    for c in range(1, n_mc):
        s0r, s0i = (a_r * s0r - a_i * s0i + gm[c:c + 1, :STATE_TILE],
                    a_r * s0i + a_i * s0r + gm[c:c + 1, STATE_TILE:])

    n_t = nc // SUBLANES
    row = lax.broadcasted_iota(jnp.int32, (nc, 1), 0)
    first = row == 0
    in_tile = row % SUBLANES
    t_row = lax.broadcasted_iota(jnp.int32, (n_t, 1), 0)
    first_t = t_row == 0
    n_local = SUBLANES.bit_length() - 1
    w_r, w_i = pr_ref[0], pi_ref[0]
    prev_r, prev_i = [], []

    def tile_totals(scr_ref, x):
        parts = []
        for c in range(STATE_TILE // LANES):
            scr_ref[c] = x[:, c * LANES:(c + 1) * LANES]
            parts.append(scr_ref[c, pl.ds(SUBLANES - 1, n_t, stride=SUBLANES), :])
        return jnp.concatenate(parts, axis=1)

    def shifted(x, d, keep):
        if d < SUBLANES:
            return jnp.where(keep, pltpu.roll(x, d, 0), 0.0)
        return jnp.concatenate([jnp.zeros((d, STATE_TILE), F32), x[:x.shape[0] - d]], axis=0)

    for b in range(batch):
        gr = g[b * nc:(b + 1) * nc, :STATE_TILE]
        gi = g[b * nc:(b + 1) * nc, STATE_TILE:]
        for k in range(n_local):
            d = 1 << k
            k_r, k_i = ar_ref[0, k:k + 1, :], ai_ref[0, k:k + 1, :]
            sh_r, sh_i = shifted(gr, d, in_tile >= d), shifted(gi, d, in_tile >= d)
            gr, gi = gr + (k_r * sh_r - k_i * sh_i), gi + (k_r * sh_i + k_i * sh_r)
        tot_r, tot_i = tile_totals(xr_ref, gr), tile_totals(xi_ref, gi)
        k_r, k_i = ar_ref[0, n_local:n_local + 1, :], ai_ref[0, n_local:n_local + 1, :]
        tot_r = tot_r + jnp.where(first_t, k_r * s0r - k_i * s0i, 0.0)
        tot_i = tot_i + jnp.where(first_t, k_r * s0i + k_i * s0r, 0.0)
        for k in range(SCAN_LEVELS - n_local):
            d = 1 << k
            if d >= n_t:
                break
            k_r = ar_ref[0, n_local + k:n_local + k + 1, :]
            k_i = ai_ref[0, n_local + k:n_local + k + 1, :]
            sh_r, sh_i = shifted(tot_r, d, t_row >= d), shifted(tot_i, d, t_row >= d)
            tot_r, tot_i = tot_r + (k_r * sh_r - k_i * sh_i), tot_i + (k_r * sh_i + k_i * sh_r)
        sre_ref[b, 0] = tot_r[n_t - 1:n_t, :]
        sim_ref[b, 0] = tot_i[n_t - 1:n_t, :]
        c_r = jnp.where(first_t, s0r, pltpu.roll(tot_r, 1, 0))
        c_i = jnp.where(first_t, s0i, pltpu.roll(tot_i, 1, 0))
        full_r, full_i = [], []
        for t in range(n_t):
            rs = slice(t * SUBLANES, (t + 1) * SUBLANES)
            e_r, e_i = c_r[t:t + 1, :], c_i[t:t + 1, :]
            full_r.append(gr[rs] + (w_r * e_r - w_i * e_i))
            full_i.append(gi[rs] + (w_r * e_i + w_i * e_r))
        gr, gi = jnp.concatenate(full_r, axis=0), jnp.concatenate(full_i, axis=0)
        prev_r.append(jnp.where(first, s0r, pltpu.roll(gr, 1, 0)))
        prev_i.append(jnp.where(first, s0i, pltpu.roll(gi, 1, 0)))
    sp = jnp.concatenate([jnp.concatenate(prev_r, axis=0), jnp.concatenate(prev_i, axis=0)],
                         axis=1).astype(BF16)
    y2 = (jnp.dot(ub, t8_ref[0], preferred_element_type=F32)
          + lax.dot_general(sp, wst_ref[0], NT_DIMS, preferred_element_type=F32))
    for j in range(CHUNK):
        sl = slice(j * LANES, (j + 1) * LANES)
        y_ref[pl.ds(j, n_rows, stride=CHUNK), :] = _gelu_tanh(y2[:, sl] + dsk * ucat[:, sl])


def _ssm(h_p, h_s, h0r, h0i, t8, win, wst, wc, a_r, a_i, l_r, l_i, p_r, p_i, d_skip, batch, n_s):
    ub = COL_U // LANES
    rows = h_p.shape[0]
    wspec = pl.BlockSpec((1, CHUNK_WIDTH, CHUNK_WIDTH), lambda gt: (gt, 0, 0))
    aspec = pl.BlockSpec((1, SCAN_LEVELS, STATE_TILE), lambda gt: (gt, 0, 0))
    sspec = pl.BlockSpec((batch, 1, 1, STATE_TILE), lambda gt: (0, gt, 0, 0))
    st = pl.BlockSpec((n_s, STATE_TILE), lambda gt: (0, gt))
    lam = pl.BlockSpec((1, 1, STATE_TILE), lambda gt: (gt, 0, 0))
    return pl.pallas_call(
        _ssm_kernel,
        grid=(N_GROUP_TILES,),
        in_specs=[
            pl.BlockSpec((rows, LANES), lambda gt: (0, ub + gt)),
            pl.BlockSpec((N_META, LANES), lambda gt: (h_s.shape[0] // N_META - 1, ub + gt)),
            pl.BlockSpec((n_s, LANES), lambda gt: (0, ub + gt)),
            st, st,
            wspec, wspec, wspec,
            pl.BlockSpec((1, LANES, 2 * STATE_TILE), lambda gt: (gt, 0, 0)),
            aspec, aspec, lam, lam,
            pl.BlockSpec((1, SUBLANES, STATE_TILE), lambda gt: (gt, 0, 0)),
            pl.BlockSpec((1, SUBLANES, STATE_TILE), lambda gt: (gt, 0, 0)),
            pl.BlockSpec((1, LANES), lambda gt: (0, gt)),
        ],
        out_specs=[pl.BlockSpec((rows, LANES), lambda gt: (0, gt)), sspec, sspec,
                   pl.BlockSpec((n_s, LANES), lambda gt: (0, gt)), st, st],
        out_shape=[
            jax.ShapeDtypeStruct((rows, SSM_WIDTH), F32),
            jax.ShapeDtypeStruct((batch, N_GROUP_TILES, 1, STATE_TILE), F32),
            jax.ShapeDtypeStruct((batch, N_GROUP_TILES, 1, STATE_TILE), F32),
            jax.ShapeDtypeStruct((n_s, SSM_WIDTH), F32),
            jax.ShapeDtypeStruct((n_s, N_SSM_GROUPS * SSM_STATE), F32),
            jax.ShapeDtypeStruct((n_s, N_SSM_GROUPS * SSM_STATE), F32),
        ],
        scratch_shapes=[pltpu.VMEM((STATE_TILE // LANES, rows // CHUNK // batch, LANES), F32)] * 2,
        compiler_params=pltpu.CompilerParams(
            dimension_semantics=("arbitrary",), vmem_limit_bytes=VMEM_LIMIT),
        name="ssm",
    )(h_p, h_s, h_s, h0r, h0i, t8, win, wst, wc, a_r, a_i, l_r, l_i, p_r, p_i, d_skip)


def _merge_kernel(o_ref, za_ref, zs_ref, ga_ref, gs_ref, y_ref, x_ref,
                  o2_ref, za2_ref, zs2_ref, ga2_ref, gs2_ref, y2_ref, x2_ref,
                  bglu_ref, wglu_ref, wa_ref, ws_ref, wo_ref, out_ref, out2_ref):
    def merge(o_r, za_r, zs_r, ga_r, gs_r, y_r, x_r, out_r):
        y = y_r[...]
        t = jnp.dot(y.astype(BF16), wglu_ref[...], preferred_element_type=F32) + bglu_ref[...]
        y = y * jax.nn.sigmoid(t)
        za = za_r[...]
        zs = zs_r[...]
        a_in = (o_r[...] * (za * jax.nn.sigmoid(za))).astype(BF16)
        s_in = (y * (zs * jax.nn.sigmoid(zs))).astype(BF16)
        br_a = jnp.dot(a_in, wa_ref[...], preferred_element_type=F32)
        br_s = jnp.dot(s_in, ws_ref[...], preferred_element_type=F32)
        mix = (jax.nn.sigmoid(ga_r[...]) * br_a + jax.nn.sigmoid(gs_r[...]) * br_s).astype(BF16)
        out_r[...] = x_r[...] + jnp.dot(mix, wo_ref[...], preferred_element_type=F32)

    merge(o_ref, za_ref, zs_ref, ga_ref, gs_ref, y_ref, x_ref, out_ref)

    @pl.when(pl.program_id(0) == pl.num_programs(0) - 1)
    def _():
        merge(o2_ref, za2_ref, zs2_ref, ga2_ref, gs2_ref, y2_ref, x2_ref, out2_ref)


def _merge(o, h, y, x, o2, h2, y2, x2, b_glu, w_glu, w_a, w_s, w_o, tm):
    m, m2 = x.shape[0], x2.shape[0]
    const = lambda shape: pl.BlockSpec(shape, lambda i: (0, 0), pipeline_mode=pl.Buffered(1))
    rows = lambda width, col: pl.BlockSpec((tm, width), lambda i: (i, col))
    rows2 = lambda width, col: pl.BlockSpec((m2, width), lambda i: (0, col),
                                            pipeline_mode=pl.Buffered(1))
    operands = lambda spec: [
        spec(ATTN_WIDTH, 0), spec(ATTN_WIDTH, COL_ZA // ATTN_WIDTH),
        spec(SSM_WIDTH, COL_ZS // SSM_WIDTH), spec(D_MODEL, COL_GA // D_MODEL),
        spec(D_MODEL, COL_GS // D_MODEL), spec(SSM_WIDTH, 0), spec(D_MODEL, 0)]
    return pl.pallas_call(
        _merge_kernel,
        grid=(m // tm,),
        in_specs=operands(rows) + operands(rows2) + [
            const((1, SSM_WIDTH)),
            const((SSM_WIDTH, SSM_WIDTH)),
            const((ATTN_WIDTH, D_MODEL)),
            const((SSM_WIDTH, D_MODEL)),
            const((D_MODEL, D_MODEL)),
        ],
        out_specs=[pl.BlockSpec((tm, D_MODEL), lambda i: (i, 0)),
                   pl.BlockSpec((m2, D_MODEL), lambda i: (0, 0))],
        out_shape=[jax.ShapeDtypeStruct((m, D_MODEL), F32),
                   jax.ShapeDtypeStruct((m2, D_MODEL), F32)],
        compiler_params=pltpu.CompilerParams(
            dimension_semantics=("arbitrary",), vmem_limit_bytes=VMEM_LIMIT),
        name="merge",
    )(o, h, h, h, h, y, x, o2, h2, h2, h2, h2, y2, x2, b_glu, w_glu, w_a, w_s, w_o)


def _rope_tables(pos):
    half = HEAD_DIM // 2
    inv_freq = (ROPE_THETA ** (-np.arange(half, dtype=np.float32) / half)).astype(np.float32)
    ang = np.asarray(pos, np.float32)[:, None] * inv_freq[None, :]
    cos, sin = np.cos(ang).astype(np.float32), np.sin(ang).astype(np.float32)
    zero = np.zeros_like(sin)
    reps = LANES // HEAD_DIM
    cos_t = np.tile(np.concatenate([cos, cos], axis=1), (1, reps))
    sin_a = np.tile(np.concatenate([-sin, zero], axis=1), (1, reps))
    sin_b = np.tile(np.concatenate([zero, sin], axis=1), (1, reps))
    return jnp.asarray(cos_t), jnp.asarray(sin_a), jnp.asarray(sin_b)


def _split_bf16(x):
    hi = x.astype(BF16)
    return hi, (x - hi.astype(F32)).astype(BF16)


def _dot_nt_f32(a, b):
    ah, al = _split_bf16(a)
    bh, bl = _split_bf16(b)
    dot = lambda x, y: lax.dot_general(x, y, NT_DIMS, preferred_element_type=F32)
    return dot(ah, bh) + dot(ah, bl) + dot(al, bh)


def _ssm_tables_kernel(are_ref, aim_ref, ldt_ref, bre_ref, bim_ref, cre_ref, cim_ref,
                       win_ref, wst_ref, t8_ref, wc_ref, ar_ref, ai_ref, lr_ref, li_ref,
                       pr_ref, pi_ref):
    a_r, a_i = are_ref[0], aim_ref[0]
    dt = jnp.exp(ldt_ref[0])
    mag = jnp.exp(a_r * dt)
    l_r, l_i = mag * jnp.cos(a_i * dt), mag * jnp.sin(a_i * dt)
    den = a_r * a_r + a_i * a_i
    f_r = ((l_r - 1.0) * a_r + l_i * a_i) / den
    f_i = (l_i * a_r - (l_r - 1.0) * a_i) / den
    shape = (LANES, STATE_TILE)
    own = (lax.broadcasted_iota(jnp.int32, shape, 0) // SSM_GROUP
           == lax.broadcasted_iota(jnp.int32, shape, 1) // SSM_STATE)
    sq = (LANES, LANES)
    same_group = (lax.broadcasted_iota(jnp.int32, sq, 0) // SSM_GROUP
                  == lax.broadcasted_iota(jnp.int32, sq, 1) // SSM_GROUP)
    lo = lax.broadcasted_iota(jnp.int32, (1, LANES), 1) < SSM_STATE
    n_rep = STATE_TILE // LANES

    def expand(x):
        return jnp.where(own, jnp.concatenate([x] * n_rep, axis=1), 0.0)

    def compact(v):
        e = jnp.where(own, jnp.broadcast_to(v, shape), 0.0)
        s = e[:, :LANES]
        for k in range(1, n_rep):
            s = s + e[:, k * LANES:(k + 1) * LANES]
        return s + pltpu.roll(s, SSM_STATE, 1)

    lc_r, lc_i = compact(l_r), compact(l_i)
    b_r, b_i = bre_ref[0], bim_ref[0]
    c_r, c_i = cre_ref[0], cim_ref[0]
    bb_r, bb_i = _cmul(compact(f_r), compact(f_i), b_r, b_i)
    wc_ref[0] = jnp.concatenate([expand(c_r), expand(-c_i)], axis=1).astype(BF16)
    c_cat = jnp.where(lo, c_r, -c_i)
    q_r, q_i = jnp.ones(sq, F32), jnp.zeros(sq, F32)
    for n in range(CHUNK):
        x_r, x_i = _cmul(q_r, q_i, bb_r, bb_i)
        win_ref[0, (CHUNK - 1 - n) * LANES:(CHUNK - n) * LANES, :] = jnp.concatenate(
            [expand(x_r), expand(x_i)], axis=1).astype(BF16)
        k_n = _dot_nt_f32(jnp.where(lo, x_r, x_i), c_cat)
        k_n = jnp.where(same_group, k_n, 0.0).astype(BF16)
        for j in range(CHUNK - n):
            t8_ref[0, j * LANES:(j + 1) * LANES, (j + n) * LANES:(j + n + 1) * LANES] = k_n
            if n > 0:
                t8_ref[0, (j + n) * LANES:(j + n + 1) * LANES, j * LANES:(j + 1) * LANES] = (
                    jnp.zeros(sq, BF16))
        q_r, q_i = _cmul(q_r, q_i, lc_r, lc_i)
        s_r, s_i = _cmul(c_r, c_i, q_r, q_i)
        wst_ref[0, n * LANES:(n + 1) * LANES, :] = jnp.concatenate(
            [expand(s_r), expand(-s_i)], axis=1).astype(BF16)
    p_r, p_i = l_r, l_i
    for _ in range(CHUNK.bit_length() - 1):
        p_r, p_i = _cmul(p_r, p_i, p_r, p_i)
    t_r, t_i = p_r, p_i
    for i in range(SUBLANES):
        pr_ref[0, i:i + 1, :] = t_r
        pi_ref[0, i:i + 1, :] = t_i
        t_r, t_i = _cmul(t_r, t_i, p_r, p_i)
    for k in range(SCAN_LEVELS):
        ar_ref[0, k:k + 1, :] = p_r
        ai_ref[0, k:k + 1, :] = p_i
        p_r, p_i = _cmul(p_r, p_i, p_r, p_i)
    lr_ref[0] = l_r
    li_ref[0] = l_i


def _ssm_tables(a_re, a_im, log_dt, b_re, b_im, c_re, c_im):
    gt = N_GROUP_TILES
    lane_vec = lambda v: v.reshape(gt, 1, STATE_TILE)
    dup = lambda v: jnp.concatenate([v, v], axis=-1)
    rows = GROUPS_PER_TILE * SSM_GROUP
    b_t = lambda v: dup(v.transpose(0, 2, 1).reshape(gt, rows, SSM_STATE))
    c_t = lambda v: dup(v.reshape(gt, rows, SSM_STATE))
    vec = pl.BlockSpec((1, 1, STATE_TILE), lambda g: (g, 0, 0))
    par = pl.BlockSpec((1, rows, LANES), lambda g: (g, 0, 0))
    big = pl.BlockSpec((1, CHUNK_WIDTH, CHUNK_WIDTH), lambda g: (g, 0, 0))
    wcs = pl.BlockSpec((1, LANES, 2 * STATE_TILE), lambda g: (g, 0, 0))
    scan = pl.BlockSpec((1, SCAN_LEVELS, STATE_TILE), lambda g: (g, 0, 0))
    tile = pl.BlockSpec((1, SUBLANES, STATE_TILE), lambda g: (g, 0, 0))
    return pl.pallas_call(
        _ssm_tables_kernel,
        grid=(gt,),
        in_specs=[vec, vec, vec, par, par, par, par],
        out_specs=[big, big, big, wcs, scan, scan, vec, vec, tile, tile],
        out_shape=[
            jax.ShapeDtypeStruct((gt, CHUNK_WIDTH, 2 * STATE_TILE), BF16),
            jax.ShapeDtypeStruct((gt, CHUNK_WIDTH, 2 * STATE_TILE), BF16),
            jax.ShapeDtypeStruct((gt, CHUNK_WIDTH, CHUNK_WIDTH), BF16),
            jax.ShapeDtypeStruct((gt, LANES, 2 * STATE_TILE), BF16),
            jax.ShapeDtypeStruct((gt, SCAN_LEVELS, STATE_TILE), F32),
            jax.ShapeDtypeStruct((gt, SCAN_LEVELS, STATE_TILE), F32),
            jax.ShapeDtypeStruct((gt, 1, STATE_TILE), F32),
            jax.ShapeDtypeStruct((gt, 1, STATE_TILE), F32),
            jax.ShapeDtypeStruct((gt, SUBLANES, STATE_TILE), F32),
            jax.ShapeDtypeStruct((gt, SUBLANES, STATE_TILE), F32),
        ],
        compiler_params=pltpu.CompilerParams(
            dimension_semantics=("arbitrary",), vmem_limit_bytes=VMEM_LIMIT),
        name="ssm_tables",
    )(lane_vec(a_re), lane_vec(a_im), lane_vec(jnp.repeat(log_dt, SSM_STATE)),
      b_t(b_re), b_t(b_im), c_t(c_re), c_t(c_im))


def kernel(x_prompt, x_sample, cache_win_k, cache_win_v, cache_meta_k, cache_meta_v,
           state_ssm_re, state_ssm_im, meta_tokens, norm_gain, w_in, q_norm_gain, k_norm_gain,
           sinks, a_re, a_im, log_dt, b_re, b_im, c_re, c_im, d_skip, w_glu, b_glu,
           w_attn_out, w_ssm_out, w_out):
    depth = w_in.shape[0]
    assert depth == 1, "single-layer trunk"
    batch, seq = x_prompt.shape[:2]
    n_s = x_sample.shape[0]
    assert x_sample.shape[1] == 1 and seq % (CHUNK * BLOCK) == 0
    l = 0

    w = w_in[l]
    qg = jnp.tile(q_norm_gain[l], TN // HEAD_DIM)[None]
    kg = jnp.tile(k_norm_gain[l], KV_WIDTH // HEAD_DIM)[None]
    lane = np.arange(LANES)
    e128 = np.where((lane[:, None] // HEAD_DIM) == (lane[None, :] // HEAD_DIM),
                    1.0 / HEAD_DIM, 0.0).astype(np.float32)
    e = jnp.asarray(np.concatenate([e128, e128], axis=0), BF16)
    gain = norm_gain[l][None]

    pos_p = N_META + np.arange(seq)
    pos_small = np.concatenate([np.full((n_s,), PAST_LEN), np.arange(N_META)])
    x_small = jnp.concatenate([x_sample[:, 0, :], meta_tokens.astype(x_prompt.dtype)], axis=0)
    xp = x_prompt.reshape(batch * seq, D_MODEL)
    xn_s = _rmsnorm(x_small, gain, tm=n_s + N_META)
    h_p, h_s, w_glu_bf, w_a_bf, w_s_bf, w_o_bf = _inproj(
        xp, gain, xn_s, w, qg, kg, e, _rope_tables(pos_p), _rope_tables(pos_small),
        (w_glu[l], w_attn_out[l], w_ssm_out[l], w_out[l]), tm=1024, halves=2)

    win, wst, t8, wc, a_r, a_i, l_r, l_i, p_r, p_i = _ssm_tables(
        a_re[l], a_im[l], log_dt[l], b_re[l], b_im[l], c_re[l], c_im[l])
    dsk = d_skip[l][None]

    vm_t = h_s[n_s:, COL_KV + KV_WIDTH:IN_WIDTH].T
    o_p = _prompt_attention(sinks[l], h_p, h_s, vm_t, batch, seq)
    q_s = h_s[:n_s, COL_Q:COL_Q + ATTN_WIDTH].reshape(n_s, N_Q_HEADS, 1, HEAD_DIM)
    sel = (np.arange(N_Q_HEADS)[:, None] // (N_Q_HEADS // N_KV_HEADS)
           == np.arange(N_KV_HEADS)[None, :]).astype(np.float32)
    qp = (q_s * sel[None, :, :, None]).reshape(n_s, N_Q_HEADS, KV_WIDTH)
    bt = 8
    to_t = lambda c: c.transpose(0, 2, 3, 1).reshape(n_s, KV_WIDTH, WINDOW)
    from_t = lambda c: c.reshape(n_s, N_KV_HEADS, HEAD_DIM, WINDOW).transpose(0, 3, 1, 2)[None]
    new_t = lambda c0: h_s[:n_s, c0:c0 + KV_WIDTH].reshape(n_s // bt, bt, KV_WIDTH).transpose(0, 2, 1)
    o_s, s_win_k, s_win_v = _sample_attention(
        qp, new_t(COL_KV), new_t(COL_KV + KV_WIDTH), to_t(cache_win_k[l]), to_t(cache_win_v[l]),
        cache_meta_k[l].reshape(n_s, N_META, KV_WIDTH), cache_meta_v[l].reshape(n_s, N_META, KV_WIDTH),
        sinks[l][:, None], bt=bt)
    o_s = o_s[:, :, :HEAD_DIM].reshape(n_s, ATTN_WIDTH)

    y_p, p_re, p_im, y_s, s_re, s_im = _ssm(
        h_p, h_s, state_ssm_re[l].reshape(n_s, -1), state_ssm_im[l].reshape(n_s, -1),
        t8, win, wst, wc, a_r, a_i, l_r, l_i, p_r, p_i, dsk, batch, n_s)
    y_prompt, y_sample = _merge(
        o_p, h_p, y_p, xp, o_s, h_s, y_s, x_sample[:, 0, :], b_glu[l][None],
        w_glu_bf, w_a_bf, w_s_bf, w_o_bf, tm=256)
    y_prompt = y_prompt.reshape(batch, seq, D_MODEL)
    y_sample = y_sample.reshape(n_s, 1, D_MODEL)

    kv_p = h_p.reshape(batch, seq, IN_WIDTH)[:, seq - WINDOW:, COL_KV:]
    kv_p = kv_p.reshape(batch, WINDOW, 2, N_KV_HEADS, HEAD_DIM)
    p_win_k = kv_p[:, :, 0][None]
    p_win_v = kv_p[:, :, 1][None]
    kv_m = h_s[n_s:, COL_KV:IN_WIDTH].reshape(N_META, 2, N_KV_HEADS, HEAD_DIM)
    p_meta_k = jnp.broadcast_to(kv_m[None, :, 0], (batch, N_META, N_KV_HEADS, HEAD_DIM))[None]
    p_meta_v = jnp.broadcast_to(kv_m[None, :, 1], (batch, N_META, N_KV_HEADS, HEAD_DIM))[None]
    p_ssm_re = p_re.reshape(batch, N_SSM_GROUPS, SSM_STATE)[None]
    p_ssm_im = p_im.reshape(batch, N_SSM_GROUPS, SSM_STATE)[None]
    st_shape = (1, n_s, N_SSM_GROUPS, SSM_STATE)
    return (y_prompt, y_sample, p_win_k, p_win_v, p_meta_k, p_meta_v, p_ssm_re, p_ssm_im,
            from_t(s_win_k), from_t(s_win_v), s_re.reshape(st_shape), s_im.reshape(st_shape))
```

```python
import functools

import jax
import jax.numpy as jnp
import numpy as np
from jax import lax
from jax.experimental import pallas as pl
from jax.experimental.pallas import tpu as pltpu

F32 = jnp.float32
BF16 = jnp.bfloat16

D_MODEL = 2048
N_META = 16
HEAD_DIM = 64
N_Q_HEADS = 16
N_KV_HEADS = 4
ATTN_WIDTH = N_Q_HEADS * HEAD_DIM
KV_WIDTH = N_KV_HEADS * HEAD_DIM
WINDOW = 128
BLOCK = 128
ROPE_THETA = 10000.0
SSM_WIDTH = D_MODEL // 2
SSM_GROUP = 16
N_SSM_GROUPS = SSM_WIDTH // SSM_GROUP
SSM_STATE = 64
EPS = 1e-6
PAST_LEN = 8192

LANES = 128
SUBLANES = 8
CHUNK = 8
GROUPS_PER_TILE = LANES // SSM_GROUP
N_GROUP_TILES = N_SSM_GROUPS // GROUPS_PER_TILE
CHUNK_WIDTH = CHUNK * LANES
STATE_TILE = GROUPS_PER_TILE * SSM_STATE
SCAN_LEVELS = 8

COL_Q = 0
COL_ZA = 1024
COL_U = 2048
COL_ZS = 3072
COL_GA = 4096
COL_GS = 6144
COL_KV = 8192
IN_WIDTH = 8704
TN = 512
CAST_ROWS = 16
KV_TILE = COL_KV // TN
VMEM_LIMIT = 56 * 1024 * 1024
NEG = -1e30
NT_DIMS = (((1,), (1,)), ((), ()))


def _cmul(ar, ai, br, bi):
    return ar * br - ai * bi, ar * bi + ai * br


def _rmsnorm_kernel(x_ref, gain_ref, o_ref):
    x = x_ref[...]
    r = lax.rsqrt(jnp.mean(x * x, axis=-1, keepdims=True) + EPS)
    o_ref[...] = (x * r * gain_ref[...]).astype(BF16)


def _rmsnorm(x, gain, tm):
    m = x.shape[0]
    return pl.pallas_call(
        _rmsnorm_kernel,
        grid=(m // tm,),
        in_specs=[pl.BlockSpec((tm, D_MODEL), lambda i: (i, 0)),
                  pl.BlockSpec((1, D_MODEL), lambda i: (0, 0))],
        out_specs=pl.BlockSpec((tm, D_MODEL), lambda i: (i, 0)),
        out_shape=jax.ShapeDtypeStruct((m, D_MODEL), BF16),
        compiler_params=pltpu.CompilerParams(
            dimension_semantics=("arbitrary",), vmem_limit_bytes=VMEM_LIMIT),
        name="rmsnorm",
    )(x, gain)


def _inproj_epilogue(j, o_ref, qg_ref, kg_ref, e_ref, cos_ref, sa_ref, sb_ref):
    def head_norm_rope(n_chunks, gain_ref, scale):
        cos, sa, sb = cos_ref[...], sa_ref[...], sb_ref[...]
        for c in range(n_chunks):
            cs = slice(c * LANES, (c + 1) * LANES)
            ac = o_ref[:, cs]
            sq = ac * ac
            hi = sq.astype(BF16)
            lo = (sq - hi.astype(F32)).astype(BF16)
            ms = jnp.dot(jnp.concatenate([hi, lo], axis=1), e_ref[...],
                         preferred_element_type=F32)
            xc = ac * lax.rsqrt(ms + EPS) * gain_ref[:, cs]
            o_ref[:, cs] = (xc * cos + pltpu.roll(xc, LANES - HEAD_DIM // 2, 1) * sa
                            + pltpu.roll(xc, HEAD_DIM // 2, 1) * sb) * scale

    @pl.when(j < COL_ZA // TN)
    def _():
        head_norm_rope(TN // LANES, qg_ref, HEAD_DIM ** -0.5)

    @pl.when(j == KV_TILE)
    def _():
        head_norm_rope(KV_WIDTH // LANES, kg_ref, 1.0)


def _inproj_kernel(x_ref, gain_ref, xs_ref, w_ref, qg_ref, kg_ref, e_ref, cos_ref, sa_ref, sb_ref,
                   cos_s_ref, sa_s_ref, sb_s_ref, *rest, mw_slabs):
    n_mw = (len(rest) - 4) // 2
    mw_refs, (o_ref, os_ref) = rest[:n_mw], rest[n_mw:n_mw + 2]
    mw_out_refs, (xn_ref, wbf_ref) = rest[n_mw + 2:2 * n_mw + 2], rest[2 * n_mw + 2:]
    h, j, i = pl.program_id(0), pl.program_id(1), pl.program_id(2)
    tm = o_ref.shape[0]
    rows = pl.ds(pl.multiple_of(i * tm, tm), tm)

    @pl.when(j == 0)
    def _():
        x = x_ref[...]
        r = lax.rsqrt(jnp.mean(x * x, axis=-1, keepdims=True) + EPS)
        xn_ref[rows, :] = (x * r * gain_ref[...]).astype(BF16)

    @pl.when(i == 0)
    def _():
        wbf_ref[...] = w_ref[...].astype(BF16)

    o_ref[...] = jnp.dot(xn_ref[rows, :], wbf_ref[...], preferred_element_type=F32)
    _inproj_epilogue(j, o_ref, qg_ref, kg_ref, e_ref, cos_ref, sa_ref, sb_ref)

    step = (h * pl.num_programs(1) + j) * pl.num_programs(2) + i
    for src, dst, n_slabs in zip(mw_refs, mw_out_refs, mw_slabs):
        @pl.when(step < n_slabs)
        def _():
            dst[...] = src[...].astype(BF16)

    @pl.when(jnp.logical_and(h == 0, jnp.logical_and(j == 0, i == 0)))
    def _():
        os_ref[...] = jnp.zeros(os_ref.shape, F32)

    @pl.when(jnp.logical_and(h == pl.num_programs(0) - 1, i == pl.num_programs(2) - 1))
    def _():
        os_ref[...] = jnp.dot(xs_ref[...], wbf_ref[...], preferred_element_type=F32)
        _inproj_epilogue(j, os_ref, qg_ref, kg_ref, e_ref, cos_s_ref, sa_s_ref, sb_s_ref)


def _w_tile(j):
    n_q = COL_ZA // TN
    return jnp.where(j < n_q, j, jnp.where(j == KV_TILE, n_q, j + 1))


def _inproj(x, gain, xs, w, qg, kg, e, tabs_p, tabs_s, merge_weights, tm, halves):
    m, ms = x.shape[0], xs.shape[0]
    rows_half = m // halves
    tiles = rows_half // tm
    n_tab = tabs_p[0].shape[0] // tm
    n_col = IN_WIDTH // TN
    const = lambda shape: pl.BlockSpec(shape, lambda h, j, i: (0, 0))
    tab_p = pl.BlockSpec((tm, LANES), lambda h, j, i: ((h * tiles + i) % n_tab, 0))
    n_steps = halves * n_col * tiles
    for mw in merge_weights:
        assert mw.shape[0] % CAST_ROWS == 0 and mw.shape[0] // CAST_ROWS <= n_steps
    slab = lambda mw: pl.BlockSpec(
        (CAST_ROWS, mw.shape[1]),
        lambda h, j, i: (jnp.minimum((h * n_col + j) * tiles + i, mw.shape[0] // CAST_ROWS - 1), 0))
    return pl.pallas_call(
        functools.partial(_inproj_kernel,
                          mw_slabs=tuple(mw.shape[0] // CAST_ROWS for mw in merge_weights)),
        grid=(halves, IN_WIDTH // TN, tiles),
        in_specs=[
            pl.BlockSpec((tm, D_MODEL),
                         lambda h, j, i: (h * tiles + jnp.where(j == 0, i, tiles - 1), 0)),
            const((1, D_MODEL)),
            const((ms, D_MODEL)),
            pl.BlockSpec((D_MODEL, TN), lambda h, j, i: (0, _w_tile(j))),
            const((1, TN)), const((1, KV_WIDTH)), const((2 * LANES, LANES)),
            tab_p, tab_p, tab_p,
            const((ms, LANES)), const((ms, LANES)), const((ms, LANES)),
        ] + [slab(mw) for mw in merge_weights],
        out_specs=[pl.BlockSpec((tm, TN), lambda h, j, i: (h * tiles + i, j)),
                   pl.BlockSpec((ms, TN),
                                lambda h, j, i: (0, jnp.where(h == halves - 1, j, n_col)))
                   ] + [slab(mw) for mw in merge_weights],
        out_shape=[jax.ShapeDtypeStruct((m, IN_WIDTH), F32),
                   jax.ShapeDtypeStruct((ms, IN_WIDTH + TN), F32)
                   ] + [jax.ShapeDtypeStruct(mw.shape, BF16) for mw in merge_weights],
        scratch_shapes=[pltpu.VMEM((rows_half, D_MODEL), BF16), pltpu.VMEM((D_MODEL, TN), BF16)],
        compiler_params=pltpu.CompilerParams(
            dimension_semantics=("arbitrary", "arbitrary", "arbitrary"),
            vmem_limit_bytes=VMEM_LIMIT),
        name="inproj",
    )(x, gain, xs, w, qg, kg, e, *tabs_p, *tabs_s, *merge_weights)


def _dup_head(x, g, lo):
    r = pltpu.roll(x, HEAD_DIM, 1)
    return jnp.where(lo, x, r) if g % 2 == 0 else jnp.where(lo, r, x)


def _prompt_attn_kernel(sinks_ref, q_ref, kvc_ref, kvp_ref, kvm_ref, vmt_ref, bias_ref, o_ref):
    lo = lax.broadcasted_iota(jnp.int32, (1, LANES), 1) < HEAD_DIM
    head_lane = lax.broadcasted_iota(jnp.int32, (1, 4 * BLOCK), 1) // BLOCK
    bias = bias_ref[0]
    v_t = jnp.concatenate([kvp_ref[:, KV_WIDTH:].T, kvc_ref[:, KV_WIDTH:].T], axis=1)
    scores = []
    for g in range(N_KV_HEADS):
        ks = slice((g // 2) * LANES, (g // 2 + 1) * LANES)
        k_all = jnp.concatenate([_dup_head(kvp_ref[:, ks], g, lo), _dup_head(kvc_ref[:, ks], g, lo),
                                 _dup_head(kvm_ref[:, ks], g, lo)], axis=0).astype(BF16)
        qa = q_ref[:, (2 * g) * LANES:(2 * g + 1) * LANES]
        qb = q_ref[:, (2 * g + 1) * LANES:(2 * g + 2) * LANES]
        q4 = jnp.concatenate([jnp.where(lo, qa, 0.0), jnp.where(lo, 0.0, qa),
                              jnp.where(lo, qb, 0.0), jnp.where(lo, 0.0, qb)],
                             axis=0).astype(BF16)
        scores.append(lax.dot_general(k_all, q4, NT_DIMS, preferred_element_type=F32))
    for g in range(N_KV_HEADS):
        s = scores[g]
        s_w = s[:2 * BLOCK] + bias
        s_m = s[2 * BLOCK:]
        sink = jnp.where(head_lane == 0, sinks_ref[4 * g],
                         jnp.where(head_lane == 1, sinks_ref[4 * g + 1],
                                   jnp.where(head_lane == 2, sinks_ref[4 * g + 2],
                                             sinks_ref[4 * g + 3])))
        m = jnp.maximum(jnp.maximum(jnp.max(s_w, axis=0, keepdims=True),
                                    jnp.max(s_m, axis=0, keepdims=True)), sink)
        p_w = jnp.exp(s_w - m)
        p_m = jnp.exp(s_m - m)
        den = (jnp.sum(p_w, axis=0, keepdims=True) + jnp.sum(p_m, axis=0, keepdims=True)
               + jnp.exp(sink - m))
        inv = 1.0 / den
        p = jnp.concatenate([p_w * inv, p_m * inv], axis=0).astype(BF16)
        hs = slice(g * HEAD_DIM, (g + 1) * HEAD_DIM)
        vt_g = jnp.concatenate([v_t[hs], vmt_ref[hs, :]], axis=1).astype(BF16)
        o_t = jnp.dot(vt_g, p, preferred_element_type=F32)
        for pair in range(2):
            two = jnp.concatenate([o_t[:, (2 * pair) * BLOCK:(2 * pair + 1) * BLOCK],
                                   o_t[:, (2 * pair + 1) * BLOCK:(2 * pair + 2) * BLOCK]], axis=0)
            o_ref[:, (2 * g + pair) * LANES:(2 * g + pair + 1) * LANES] = two.T


def _prompt_attention(sinks, h_p, h_s, vm_t, batch, seq):
    nb = seq // BLOCK
    kvb = COL_KV // (2 * KV_WIDTH)
    kj = np.arange(2 * BLOCK)[:, None]
    qi = np.arange(4 * BLOCK)[None, :] % BLOCK
    cur = (kj >= BLOCK) & (kj - BLOCK <= qi)
    prev = (kj < BLOCK) & (kj > qi)
    bias = jnp.asarray(np.where(np.stack([cur, cur | prev]), 0.0, NEG), F32)
    return pl.pallas_call(
        _prompt_attn_kernel,
        grid=(batch, nb),
        in_specs=[
            pl.BlockSpec(memory_space=pltpu.SMEM),
            pl.BlockSpec((BLOCK, ATTN_WIDTH), lambda b, i: (b * nb + i, 0)),
            pl.BlockSpec((BLOCK, 2 * KV_WIDTH), lambda b, i: (b * nb + i, kvb)),
            pl.BlockSpec((BLOCK, 2 * KV_WIDTH),
                         lambda b, i: (b * nb + jnp.maximum(i - 1, 0), kvb)),
            pl.BlockSpec((N_META, 2 * KV_WIDTH), lambda b, i: (h_s.shape[0] // N_META - 1, kvb)),
            pl.BlockSpec((KV_WIDTH, N_META), lambda b, i: (0, 0)),
            pl.BlockSpec((1, 2 * BLOCK, 4 * BLOCK), lambda b, i: (jnp.minimum(i, 1), 0, 0)),
        ],
        out_specs=pl.BlockSpec((BLOCK, ATTN_WIDTH), lambda b, i: (b * nb + i, 0)),
        out_shape=jax.ShapeDtypeStruct((batch * seq, ATTN_WIDTH), F32),
        compiler_params=pltpu.CompilerParams(
            dimension_semantics=("arbitrary", "arbitrary"), vmem_limit_bytes=VMEM_LIMIT),
        name="prompt_attn",
    )(sinks, h_p, h_p, h_p, h_s, vm_t, bias)


def _sample_attn_kernel(qp_ref, kn_ref, vn_ref, ck_ref, cv_ref, mk_ref, mv_ref, sink_ref,
                        o_ref, ok_ref, ov_ref):
    bt = ck_ref.shape[0]
    last = lax.broadcasted_iota(jnp.int32, (1, WINDOW), 1) == WINDOW - 1
    kn = kn_ref[0]
    vn = vn_ref[0]
    ck = pltpu.roll(ck_ref[...], WINDOW - 1, 2)
    cv = pltpu.roll(cv_ref[...], WINDOW - 1, 2)
    k_new = jnp.stack([jnp.where(last, kn[:, b:b + 1], ck[b]) for b in range(bt)])
    v_new = jnp.stack([jnp.where(last, vn[:, b:b + 1], cv[b]) for b in range(bt)])
    ok_ref[...] = k_new
    ov_ref[...] = v_new
    qp = qp_ref[...].astype(BF16)
    k_w = k_new.astype(BF16)
    v_w = v_new.astype(BF16)
    k_m = mk_ref[...].astype(BF16)
    v_m = mv_ref[...].astype(BF16)
    s_w = jnp.einsum("bhc,bcw->bhw", qp, k_w, preferred_element_type=F32)
    s_m = jnp.einsum("bhc,bkc->bhk", qp, k_m, preferred_element_type=F32)
    sink = sink_ref[...][None]
    m = jnp.maximum(jnp.maximum(jnp.max(s_w, axis=-1, keepdims=True),
                                jnp.max(s_m, axis=-1, keepdims=True)), sink)
    p_w = jnp.exp(s_w - m)
    p_m = jnp.exp(s_m - m)
    den = (jnp.sum(p_w, axis=-1, keepdims=True) + jnp.sum(p_m, axis=-1, keepdims=True)
           + jnp.exp(sink - m))
    inv = 1.0 / den
    o = (jnp.einsum("bhw,bcw->bhc", (p_w * inv).astype(BF16), v_w, preferred_element_type=F32)
         + jnp.einsum("bhk,bkc->bhc", (p_m * inv).astype(BF16), v_m,
                      preferred_element_type=F32))
    hh = lax.broadcasted_iota(jnp.int32, (1, N_Q_HEADS, KV_WIDTH), 1) // (N_Q_HEADS // N_KV_HEADS)
    cc = lax.broadcasted_iota(jnp.int32, (1, N_Q_HEADS, KV_WIDTH), 2) // HEAD_DIM
    o = jnp.where(hh == cc, o, 0.0)
    o = o[:, :, :LANES] + o[:, :, LANES:]
    o_ref[...] = o + pltpu.roll(o, HEAD_DIM, 2)


def _sample_attention(qp, kn_t, vn_t, cache_k, cache_v, meta_k, meta_v, sink_col, bt):
    n = cache_k.shape[0]
    win_spec = pl.BlockSpec((bt, KV_WIDTH, WINDOW), lambda t: (t, 0, 0))
    new_spec = pl.BlockSpec((1, KV_WIDTH, bt), lambda t: (t, 0, 0))
    meta_spec = pl.BlockSpec((bt, N_META, KV_WIDTH), lambda t: (t, 0, 0))
    return pl.pallas_call(
        _sample_attn_kernel,
        grid=(n // bt,),
        in_specs=[
            pl.BlockSpec((bt, N_Q_HEADS, KV_WIDTH), lambda t: (t, 0, 0)),
            new_spec, new_spec, win_spec, win_spec, meta_spec, meta_spec,
            pl.BlockSpec((N_Q_HEADS, 1), lambda t: (0, 0)),
        ],
        out_specs=[
            pl.BlockSpec((bt, N_Q_HEADS, LANES), lambda t: (t, 0, 0)),
            win_spec, win_spec,
        ],
        out_shape=[
            jax.ShapeDtypeStruct((n, N_Q_HEADS, LANES), F32),
            jax.ShapeDtypeStruct(cache_k.shape, F32),
            jax.ShapeDtypeStruct(cache_v.shape, F32),
        ],
        compiler_params=pltpu.CompilerParams(
            dimension_semantics=("arbitrary",), vmem_limit_bytes=VMEM_LIMIT),
        name="sample_attn",
    )(qp, kn_t, vn_t, cache_k, cache_v, meta_k, meta_v, sink_col)


def _gelu_tanh(x):
    return 0.5 * x * (1.0 + jnp.tanh(0.7978845608028654 * (x + 0.044715 * (x * x * x))))


def _ssm_kernel(u_ref, um_ref, us_ref, hr_ref, hi_ref, t8_ref, win_ref, wst_ref, wc_ref,
                ar_ref, ai_ref, lr_ref, li_ref, pr_ref, pi_ref, d_ref,
                y_ref, sre_ref, sim_ref, ys_ref, nr_ref, ni_ref, xr_ref, xi_ref):
    dsk = d_ref[...]
    us = us_ref[...]
    bu = jnp.dot(us.astype(BF16), win_ref[0, (CHUNK - 1) * LANES:, :],
                 preferred_element_type=F32)
    l_r, l_i = lr_ref[0], li_ref[0]
    h_r, h_i = hr_ref[...], hi_ref[...]
    n_r = l_r * h_r - l_i * h_i + bu[:, :STATE_TILE]
    n_i = l_r * h_i + l_i * h_r + bu[:, STATE_TILE:]
    nr_ref[...] = n_r
    ni_ref[...] = n_i
    hn = jnp.concatenate([n_r, n_i], axis=1).astype(BF16)
    ys_ref[...] = _gelu_tanh(
        lax.dot_general(hn, wc_ref[0], NT_DIMS, preferred_element_type=F32) + dsk * us)

    batch = sre_ref.shape[0]
    n_rows = u_ref.shape[0] // CHUNK
    nc = n_rows // batch
    ucat = jnp.concatenate([u_ref[pl.ds(j, n_rows, stride=CHUNK), :] for j in range(CHUNK)],
                           axis=1)
    ub = ucat.astype(BF16)
    win = win_ref[0]
    g = jnp.dot(ub, win, preferred_element_type=F32)

    n_mc = um_ref.shape[0] // CHUNK
    um = jnp.concatenate([um_ref[pl.ds(j, n_mc, stride=CHUNK), :] for j in range(CHUNK)], axis=1)
    gm = jnp.dot(um.astype(BF16), win, preferred_element_type=F32)
    a_r, a_i = ar_ref[0, 0:1, :], ai_ref[0, 0:1, :]
    s0r, s0i = gm[0:1, :STATE_TILE], gm[0:1, STATE_TILE:]
    for c in range(1, n_mc):
        s0r, s0i = (a_r * s0r - a_i * s0i + gm[c:c + 1, :STATE_TILE],
                    a_r * s0i + a_i * s0r + gm[c:c + 1, STATE_TILE:])

    n_t = nc // SUBLANES
    row = lax.broadcasted_iota(jnp.int32, (nc, 1), 0)
    first = row == 0
    in_tile = row % SUBLANES
    t_row = lax.broadcasted_iota(jnp.int32, (n_t, 1), 0)
    first_t = t_row == 0
    n_local = SUBLANES.bit_length() - 1
    w_r, w_i = pr_ref[0], pi_ref[0]
    prev_r, prev_i = [], []

    def tile_totals(scr_ref, x):
        parts = []
        for c in range(STATE_TILE // LANES):
            scr_ref[c] = x[:, c * LANES:(c + 1) * LANES]
            parts.append(scr_ref[c, pl.ds(SUBLANES - 1, n_t, stride=SUBLANES), :])
        return jnp.concatenate(parts, axis=1)

    def shifted(x, d, keep):
        if d < SUBLANES:
            return jnp.where(keep, pltpu.roll(x, d, 0), 0.0)
        return jnp.concatenate([jnp.zeros((d, STATE_TILE), F32), x[:x.shape[0] - d]], axis=0)

    for b in range(batch):
        gr = g[b * nc:(b + 1) * nc, :STATE_TILE]
        gi = g[b * nc:(b + 1) * nc, STATE_TILE:]
        for k in range(n_local):
            d = 1 << k
            k_r, k_i = ar_ref[0, k:k + 1, :], ai_ref[0, k:k + 1, :]
            sh_r, sh_i = shifted(gr, d, in_tile >= d), shifted(gi, d, in_tile >= d)
            gr, gi = gr + (k_r * sh_r - k_i * sh_i), gi + (k_r * sh_i + k_i * sh_r)
        tot_r, tot_i = tile_totals(xr_ref, gr), tile_totals(xi_ref, gi)
        k_r, k_i = ar_ref[0, n_local:n_local + 1, :], ai_ref[0, n_local:n_local + 1, :]
        tot_r = tot_r + jnp.where(first_t, k_r * s0r - k_i * s0i, 0.0)
        tot_i = tot_i + jnp.where(first_t, k_r * s0i + k_i * s0r, 0.0)
        for k in range(SCAN_LEVELS - n_local):
            d = 1 << k
            if d >= n_t:
                break
            k_r = ar_ref[0, n_local + k:n_local + k + 1, :]
            k_i = ai_ref[0, n_local + k:n_local + k + 1, :]
            sh_r, sh_i = shifted(tot_r, d, t_row >= d), shifted(tot_i, d, t_row >= d)
            tot_r, tot_i = tot_r + (k_r * sh_r - k_i * sh_i), tot_i + (k_r * sh_i + k_i * sh_r)
        sre_ref[b, 0] = tot_r[n_t - 1:n_t, :]
        sim_ref[b, 0] = tot_i[n_t - 1:n_t, :]
        c_r = jnp.where(first_t, s0r, pltpu.roll(tot_r, 1, 0))
        c_i = jnp.where(first_t, s0i, pltpu.roll(tot_i, 1, 0))
        full_r, full_i = [], []
        for t in range(n_t):
            rs = slice(t * SUBLANES, (t + 1) * SUBLANES)
            e_r, e_i = c_r[t:t + 1, :], c_i[t:t + 1, :]
            full_r.append(gr[rs] + (w_r * e_r - w_i * e_i))
            full_i.append(gi[rs] + (w_r * e_i + w_i * e_r))
        gr, gi = jnp.concatenate(full_r, axis=0), jnp.concatenate(full_i, axis=0)
        prev_r.append(jnp.where(first, s0r, pltpu.roll(gr, 1, 0)))
        prev_i.append(jnp.where(first, s0i, pltpu.roll(gi, 1, 0)))
    sp = jnp.concatenate([jnp.concatenate(prev_r, axis=0), jnp.concatenate(prev_i, axis=0)],
                         axis=1).astype(BF16)
    y2 = (jnp.dot(ub, t8_ref[0], preferred_element_type=F32)
          + lax.dot_general(sp, wst_ref[0], NT_DIMS, preferred_element_type=F32))
    for j in range(CHUNK):
        sl = slice(j * LANES, (j + 1) * LANES)
        y_ref[pl.ds(j, n_rows, stride=CHUNK), :] = _gelu_tanh(y2[:, sl] + dsk * ucat[:, sl])


def _ssm(h_p, h_s, h0r, h0i, t8, win, wst, wc, a_r, a_i, l_r, l_i, p_r, p_i, d_skip, batch, n_s):
    ub = COL_U // LANES
    rows = h_p.shape[0]
    wspec = pl.BlockSpec((1, CHUNK_WIDTH, CHUNK_WIDTH), lambda gt: (gt, 0, 0))
    aspec = pl.BlockSpec((1, SCAN_LEVELS, STATE_TILE), lambda gt: (gt, 0, 0))
    sspec = pl.BlockSpec((batch, 1, 1, STATE_TILE), lambda gt: (0, gt, 0, 0))
    st = pl.BlockSpec((n_s, STATE_TILE), lambda gt: (0, gt))
    lam = pl.BlockSpec((1, 1, STATE_TILE), lambda gt: (gt, 0, 0))
    return pl.pallas_call(
        _ssm_kernel,
        grid=(N_GROUP_TILES,),
        in_specs=[
            pl.BlockSpec((rows, LANES), lambda gt: (0, ub + gt)),
            pl.BlockSpec((N_META, LANES), lambda gt: (h_s.shape[0] // N_META - 1, ub + gt)),
            pl.BlockSpec((n_s, LANES), lambda gt: (0, ub + gt)),
            st, st,
            wspec, wspec, wspec,
            pl.BlockSpec((1, LANES, 2 * STATE_TILE), lambda gt: (gt, 0, 0)),
            aspec, aspec, lam, lam,
            pl.BlockSpec((1, SUBLANES, STATE_TILE), lambda gt: (gt, 0, 0)),
            pl.BlockSpec((1, SUBLANES, STATE_TILE), lambda gt: (gt, 0, 0)),
            pl.BlockSpec((1, LANES), lambda gt: (0, gt)),
        ],
        out_specs=[pl.BlockSpec((rows, LANES), lambda gt: (0, gt)), sspec, sspec,
                   pl.BlockSpec((n_s, LANES), lambda gt: (0, gt)), st, st],
        out_shape=[
            jax.ShapeDtypeStruct((rows, SSM_WIDTH), F32),
            jax.ShapeDtypeStruct((batch, N_GROUP_TILES, 1, STATE_TILE), F32),
            jax.ShapeDtypeStruct((batch, N_GROUP_TILES, 1, STATE_TILE), F32),
            jax.ShapeDtypeStruct((n_s, SSM_WIDTH), F32),
            jax.ShapeDtypeStruct((n_s, N_SSM_GROUPS * SSM_STATE), F32),
            jax.ShapeDtypeStruct((n_s, N_SSM_GROUPS * SSM_STATE), F32),
        ],
        scratch_shapes=[pltpu.VMEM((STATE_TILE // LANES, rows // CHUNK // batch, LANES), F32)] * 2,
        compiler_params=pltpu.CompilerParams(
            dimension_semantics=("arbitrary",), vmem_limit_bytes=VMEM_LIMIT),
        name="ssm",
    )(h_p, h_s, h_s, h0r, h0i, t8, win, wst, wc, a_r, a_i, l_r, l_i, p_r, p_i, d_skip)


def _merge_kernel(o_ref, za_ref, zs_ref, ga_ref, gs_ref, y_ref, x_ref,
                  o2_ref, za2_ref, zs2_ref, ga2_ref, gs2_ref, y2_ref, x2_ref,
                  bglu_ref, wglu_ref, wa_ref, ws_ref, wo_ref, out_ref, out2_ref):
    def merge(o_r, za_r, zs_r, ga_r, gs_r, y_r, x_r, out_r):
        y = y_r[...]
        t = jnp.dot(y.astype(BF16), wglu_ref[...], preferred_element_type=F32) + bglu_ref[...]
        y = y * jax.nn.sigmoid(t)
        za = za_r[...]
        zs = zs_r[...]
        a_in = (o_r[...] * (za * jax.nn.sigmoid(za))).astype(BF16)
        s_in = (y * (zs * jax.nn.sigmoid(zs))).astype(BF16)
        br_a = jnp.dot(a_in, wa_ref[...], preferred_element_type=F32)
        br_s = jnp.dot(s_in, ws_ref[...], preferred_element_type=F32)
        mix = (jax.nn.sigmoid(ga_r[...]) * br_a + jax.nn.sigmoid(gs_r[...]) * br_s).astype(BF16)
        out_r[...] = x_r[...] + jnp.dot(mix, wo_ref[...], preferred_element_type=F32)

    merge(o_ref, za_ref, zs_ref, ga_ref, gs_ref, y_ref, x_ref, out_ref)

    @pl.when(pl.program_id(0) == pl.num_programs(0) - 1)
    def _():
        merge(o2_ref, za2_ref, zs2_ref, ga2_ref, gs2_ref, y2_ref, x2_ref, out2_ref)


def _merge(o, h, y, x, o2, h2, y2, x2, b_glu, w_glu, w_a, w_s, w_o, tm):
    m, m2 = x.shape[0], x2.shape[0]
    const = lambda shape: pl.BlockSpec(shape, lambda i: (0, 0), pipeline_mode=pl.Buffered(1))
    rows = lambda width, col: pl.BlockSpec((tm, width), lambda i: (i, col))
    rows2 = lambda width, col: pl.BlockSpec((m2, width), lambda i: (0, col),
                                            pipeline_mode=pl.Buffered(1))
    operands = lambda spec: [
        spec(ATTN_WIDTH, 0), spec(ATTN_WIDTH, COL_ZA // ATTN_WIDTH),
        spec(SSM_WIDTH, COL_ZS // SSM_WIDTH), spec(D_MODEL, COL_GA // D_MODEL),
        spec(D_MODEL, COL_GS // D_MODEL), spec(SSM_WIDTH, 0), spec(D_MODEL, 0)]
    return pl.pallas_call(
        _merge_kernel,
        grid=(m // tm,),
        in_specs=operands(rows) + operands(rows2) + [
            const((1, SSM_WIDTH)),
            const((SSM_WIDTH, SSM_WIDTH)),
            const((ATTN_WIDTH, D_MODEL)),
            const((SSM_WIDTH, D_MODEL)),
            const((D_MODEL, D_MODEL)),
        ],
        out_specs=[pl.BlockSpec((tm, D_MODEL), lambda i: (i, 0)),
                   pl.BlockSpec((m2, D_MODEL), lambda i: (0, 0))],
        out_shape=[jax.ShapeDtypeStruct((m, D_MODEL), F32),
                   jax.ShapeDtypeStruct((m2, D_MODEL), F32)],
        compiler_params=pltpu.CompilerParams(
            dimension_semantics=("arbitrary",), vmem_limit_bytes=VMEM_LIMIT),
        name="merge",
    )(o, h, h, h, h, y, x, o2, h2, h2, h2, h2, y2, x2, b_glu, w_glu, w_a, w_s, w_o)


def _rope_tables(pos):
    half = HEAD_DIM // 2
    inv_freq = (ROPE_THETA ** (-np.arange(half, dtype=np.float32) / half)).astype(np.float32)
    ang = np.asarray(pos, np.float32)[:, None] * inv_freq[None, :]
    cos, sin = np.cos(ang).astype(np.float32), np.sin(ang).astype(np.float32)
    zero = np.zeros_like(sin)
    reps = LANES // HEAD_DIM
    cos_t = np.tile(np.concatenate([cos, cos], axis=1), (1, reps))
    sin_a = np.tile(np.concatenate([-sin, zero], axis=1), (1, reps))
    sin_b = np.tile(np.concatenate([zero, sin], axis=1), (1, reps))
    return jnp.asarray(cos_t), jnp.asarray(sin_a), jnp.asarray(sin_b)


def _split_bf16(x):
    hi = x.astype(BF16)
    return hi, (x - hi.astype(F32)).astype(BF16)


def _dot_nt_f32(a, b):
    ah, al = _split_bf16(a)
    bh, bl = _split_bf16(b)
    dot = lambda x, y: lax.dot_general(x, y, NT_DIMS, preferred_element_type=F32)
    return dot(ah, bh) + dot(ah, bl) + dot(al, bh)


def _ssm_tables_kernel(are_ref, aim_ref, ldt_ref, bre_ref, bim_ref, cre_ref, cim_ref,
                       win_ref, wst_ref, t8_ref, wc_ref, ar_ref, ai_ref, lr_ref, li_ref,
                       pr_ref, pi_ref):
    a_r, a_i = are_ref[0], aim_ref[0]
    dt = jnp.exp(ldt_ref[0])
    mag = jnp.exp(a_r * dt)
    l_r, l_i = mag * jnp.cos(a_i * dt), mag * jnp.sin(a_i * dt)
    den = a_r * a_r + a_i * a_i
    f_r = ((l_r - 1.0) * a_r + l_i * a_i) / den
    f_i = (l_i * a_r - (l_r - 1.0) * a_i) / den
    shape = (LANES, STATE_TILE)
    own = (lax.broadcasted_iota(jnp.int32, shape, 0) // SSM_GROUP
           == lax.broadcasted_iota(jnp.int32, shape, 1) // SSM_STATE)
    sq = (LANES, LANES)
    same_group = (lax.broadcasted_iota(jnp.int32, sq, 0) // SSM_GROUP
                  == lax.broadcasted_iota(jnp.int32, sq, 1) // SSM_GROUP)
    lo = lax.broadcasted_iota(jnp.int32, (1, LANES), 1) < SSM_STATE
    n_rep = STATE_TILE // LANES

    def expand(x):
        return jnp.where(own, jnp.concatenate([x] * n_rep, axis=1), 0.0)

    def compact(v):
        e = jnp.where(own, jnp.broadcast_to(v, shape), 0.0)
        s = e[:, :LANES]
        for k in range(1, n_rep):
            s = s + e[:, k * LANES:(k + 1) * LANES]
        return s + pltpu.roll(s, SSM_STATE, 1)

    lc_r, lc_i = compact(l_r), compact(l_i)
    b_r, b_i = bre_ref[0], bim_ref[0]
    c_r, c_i = cre_ref[0], cim_ref[0]
    bb_r, bb_i = _cmul(compact(f_r), compact(f_i), b_r, b_i)
    wc_ref[0] = jnp.concatenate([expand(c_r), expand(-c_i)], axis=1).astype(BF16)
    c_cat = jnp.where(lo, c_r, -c_i)
    q_r, q_i = jnp.ones(sq, F32), jnp.zeros(sq, F32)
    for n in range(CHUNK):
        x_r, x_i = _cmul(q_r, q_i, bb_r, bb_i)
        win_ref[0, (CHUNK - 1 - n) * LANES:(CHUNK - n) * LANES, :] = jnp.concatenate(
            [expand(x_r), expand(x_i)], axis=1).astype(BF16)
        k_n = _dot_nt_f32(jnp.where(lo, x_r, x_i), c_cat)
        k_n = jnp.where(same_group, k_n, 0.0).astype(BF16)
        for j in range(CHUNK - n):
            t8_ref[0, j * LANES:(j + 1) * LANES, (j + n) * LANES:(j + n + 1) * LANES] = k_n
            if n > 0:
                t8_ref[0, (j + n) * LANES:(j + n + 1) * LANES, j * LANES:(j + 1) * LANES] = (
                    jnp.zeros(sq, BF16))
        q_r, q_i = _cmul(q_r, q_i, lc_r, lc_i)
        s_r, s_i = _cmul(c_r, c_i, q_r, q_i)
        wst_ref[0, n * LANES:(n + 1) * LANES, :] = jnp.concatenate(
            [expand(s_r), expand(-s_i)], axis=1).astype(BF16)
    p_r, p_i = l_r, l_i
    for _ in range(CHUNK.bit_length() - 1):
        p_r, p_i = _cmul(p_r, p_i, p_r, p_i)
    t_r, t_i = p_r, p_i
    for i in range(SUBLANES):
        pr_ref[0, i:i + 1, :] = t_r
        pi_ref[0, i:i + 1, :] = t_i
        t_r, t_i = _cmul(t_r, t_i, p_r, p_i)
    for k in range(SCAN_LEVELS):
        ar_ref[0, k:k + 1, :] = p_r
        ai_ref[0, k:k + 1, :] = p_i
        p_r, p_i = _cmul(p_r, p_i, p_r, p_i)
    lr_ref[0] = l_r
    li_ref[0] = l_i


def _ssm_tables(a_re, a_im, log_dt, b_re, b_im, c_re, c_im):
    gt = N_GROUP_TILES
    lane_vec = lambda v: v.reshape(gt, 1, STATE_TILE)
    dup = lambda v: jnp.concatenate([v, v], axis=-1)
    rows = GROUPS_PER_TILE * SSM_GROUP
    b_t = lambda v: dup(v.transpose(0, 2, 1).reshape(gt, rows, SSM_STATE))
    c_t = lambda v: dup(v.reshape(gt, rows, SSM_STATE))
    vec = pl.BlockSpec((1, 1, STATE_TILE), lambda g: (g, 0, 0))
    par = pl.BlockSpec((1, rows, LANES), lambda g: (g, 0, 0))
    big = pl.BlockSpec((1, CHUNK_WIDTH, CHUNK_WIDTH), lambda g: (g, 0, 0))
    wcs = pl.BlockSpec((1, LANES, 2 * STATE_TILE), lambda g: (g, 0, 0))
    scan = pl.BlockSpec((1, SCAN_LEVELS, STATE_TILE), lambda g: (g, 0, 0))
    tile = pl.BlockSpec((1, SUBLANES, STATE_TILE), lambda g: (g, 0, 0))
    return pl.pallas_call(
        _ssm_tables_kernel,
        grid=(gt,),
        in_specs=[vec, vec, vec, par, par, par, par],
        out_specs=[big, big, big, wcs, scan, scan, vec, vec, tile, tile],
        out_shape=[
            jax.ShapeDtypeStruct((gt, CHUNK_WIDTH, 2 * STATE_TILE), BF16),
            jax.ShapeDtypeStruct((gt, CHUNK_WIDTH, 2 * STATE_TILE), BF16),
            jax.ShapeDtypeStruct((gt, CHUNK_WIDTH, CHUNK_WIDTH), BF16),
            jax.ShapeDtypeStruct((gt, LANES, 2 * STATE_TILE), BF16),
            jax.ShapeDtypeStruct((gt, SCAN_LEVELS, STATE_TILE), F32),
            jax.ShapeDtypeStruct((gt, SCAN_LEVELS, STATE_TILE), F32),
            jax.ShapeDtypeStruct((gt, 1, STATE_TILE), F32),
            jax.ShapeDtypeStruct((gt, 1, STATE_TILE), F32),
            jax.ShapeDtypeStruct((gt, SUBLANES, STATE_TILE), F32),
            jax.ShapeDtypeStruct((gt, SUBLANES, STATE_TILE), F32),
        ],
        compiler_params=pltpu.CompilerParams(
            dimension_semantics=("arbitrary",), vmem_limit_bytes=VMEM_LIMIT),
        name="ssm_tables",
    )(lane_vec(a_re), lane_vec(a_im), lane_vec(jnp.repeat(log_dt, SSM_STATE)),
      b_t(b_re), b_t(b_im), c_t(c_re), c_t(c_im))


def kernel(x_prompt, x_sample, cache_win_k, cache_win_v, cache_meta_k, cache_meta_v,
           state_ssm_re, state_ssm_im, meta_tokens, norm_gain, w_in, q_norm_gain, k_norm_gain,
           sinks, a_re, a_im, log_dt, b_re, b_im, c_re, c_im, d_skip, w_glu, b_glu,
           w_attn_out, w_ssm_out, w_out):
    depth = w_in.shape[0]
    assert depth == 1, "single-layer trunk"
    batch, seq = x_prompt.shape[:2]
    n_s = x_sample.shape[0]
    assert x_sample.shape[1] == 1 and seq % (CHUNK * BLOCK) == 0
    l = 0

    w = w_in[l]
    qg = jnp.tile(q_norm_gain[l], TN // HEAD_DIM)[None]
    kg = jnp.tile(k_norm_gain[l], KV_WIDTH // HEAD_DIM)[None]
    lane = np.arange(LANES)
    e128 = np.where((lane[:, None] // HEAD_DIM) == (lane[None, :] // HEAD_DIM),
                    1.0 / HEAD_DIM, 0.0).astype(np.float32)
    e = jnp.asarray(np.concatenate([e128, e128], axis=0), BF16)
    gain = norm_gain[l][None]

    pos_p = N_META + np.arange(seq)
    pos_small = np.concatenate([np.full((n_s,), PAST_LEN), np.arange(N_META)])
    x_small = jnp.concatenate([x_sample[:, 0, :], meta_tokens.astype(x_prompt.dtype)], axis=0)
    xp = x_prompt.reshape(batch * seq, D_MODEL)
    xn_s = _rmsnorm(x_small, gain, tm=n_s + N_META)
    h_p, h_s, w_glu_bf, w_a_bf, w_s_bf, w_o_bf = _inproj(
        xp, gain, xn_s, w, qg, kg, e, _rope_tables(pos_p), _rope_tables(pos_small),
        (w_glu[l], w_attn_out[l], w_ssm_out[l], w_out[l]), tm=1024, halves=2)

    win, wst, t8, wc, a_r, a_i, l_r, l_i, p_r, p_i = _ssm_tables(
        a_re[l], a_im[l], log_dt[l], b_re[l], b_im[l], c_re[l], c_im[l])
    dsk = d_skip[l][None]

    vm_t = h_s[n_s:, COL_KV + KV_WIDTH:IN_WIDTH].T
    o_p = _prompt_attention(sinks[l], h_p, h_s, vm_t, batch, seq)
    q_s = h_s[:n_s, COL_Q:COL_Q + ATTN_WIDTH].reshape(n_s, N_Q_HEADS, 1, HEAD_DIM)
    sel = (np.arange(N_Q_HEADS)[:, None] // (N_Q_HEADS // N_KV_HEADS)
           == np.arange(N_KV_HEADS)[None, :]).astype(np.float32)
    qp = (q_s * sel[None, :, :, None]).reshape(n_s, N_Q_HEADS, KV_WIDTH)
    bt = 8
    to_t = lambda c: c.transpose(0, 2, 3, 1).reshape(n_s, KV_WIDTH, WINDOW)
    from_t = lambda c: c.reshape(n_s, N_KV_HEADS, HEAD_DIM, WINDOW).transpose(0, 3, 1, 2)[None]
    new_t = lambda c0: h_s[:n_s, c0:c0 + KV_WIDTH].reshape(n_s // bt, bt, KV_WIDTH).transpose(0, 2, 1)
    o_s, s_win_k, s_win_v = _sample_attention(
        qp, new_t(COL_KV), new_t(COL_KV + KV_WIDTH), to_t(cache_win_k[l]), to_t(cache_win_v[l]),
        cache_meta_k[l].reshape(n_s, N_META, KV_WIDTH), cache_meta_v[l].reshape(n_s, N_META, KV_WIDTH),
        sinks[l][:, None], bt=bt)
    o_s = o_s[:, :, :HEAD_DIM].reshape(n_s, ATTN_WIDTH)

    y_p, p_re, p_im, y_s, s_re, s_im = _ssm(
        h_p, h_s, state_ssm_re[l].reshape(n_s, -1), state_ssm_im[l].reshape(n_s, -1),
        t8, win, wst, wc, a_r, a_i, l_r, l_i, p_r, p_i, dsk, batch, n_s)
    y_prompt, y_sample = _merge(
        o_p, h_p, y_p, xp, o_s, h_s, y_s, x_sample[:, 0, :], b_glu[l][None],
        w_glu_bf, w_a_bf, w_s_bf, w_o_bf, tm=256)
    y_prompt = y_prompt.reshape(batch, seq, D_MODEL)
    y_sample = y_sample.reshape(n_s, 1, D_MODEL)

    kv_p = h_p.reshape(batch, seq, IN_WIDTH)[:, seq - WINDOW:, COL_KV:]
    kv_p = kv_p.reshape(batch, WINDOW, 2, N_KV_HEADS, HEAD_DIM)
    p_win_k = kv_p[:, :, 0][None]
    p_win_v = kv_p[:, :, 1][None]
    kv_m = h_s[n_s:, COL_KV:IN_WIDTH].reshape(N_META, 2, N_KV_HEADS, HEAD_DIM)
    p_meta_k = jnp.broadcast_to(kv_m[None, :, 0], (batch, N_META, N_KV_HEADS, HEAD_DIM))[None]
    p_meta_v = jnp.broadcast_to(kv_m[None, :, 1], (batch, N_META, N_KV_HEADS, HEAD_DIM))[None]
    p_ssm_re = p_re.reshape(batch, N_SSM_GROUPS, SSM_STATE)[None]
    p_ssm_im = p_im.reshape(batch, N_SSM_GROUPS, SSM_STATE)[None]
    st_shape = (1, n_s, N_SSM_GROUPS, SSM_STATE)
    return (y_prompt, y_sample, p_win_k, p_win_v, p_meta_k, p_meta_v, p_ssm_re, p_ssm_im,
            from_t(s_win_k), from_t(s_win_v), s_re.reshape(st_shape), s_im.reshape(st_shape))
```

```python
import functools

import jax
import jax.numpy as jnp
import numpy as np
from jax import lax
from jax.experimental import pallas as pl
from jax.experimental.pallas import tpu as pltpu

F32 = jnp.float32
BF16 = jnp.bfloat16

D_MODEL = 2048
N_META = 16
HEAD_DIM = 64
N_Q_HEADS = 16
N_KV_HEADS = 4
ATTN_WIDTH = N_Q_HEADS * HEAD_DIM
KV_WIDTH = N_KV_HEADS * HEAD_DIM
WINDOW = 128
BLOCK = 128
ROPE_THETA = 10000.0
SSM_WIDTH = D_MODEL // 2
SSM_GROUP = 16
N_SSM_GROUPS = SSM_WIDTH // SSM_GROUP
SSM_STATE = 64
EPS = 1e-6
PAST_LEN = 8192

LANES = 128
SUBLANES = 8
CHUNK = 8
GROUPS_PER_TILE = LANES // SSM_GROUP
N_GROUP_TILES = N_SSM_GROUPS // GROUPS_PER_TILE
CHUNK_WIDTH = CHUNK * LANES
STATE_TILE = GROUPS_PER_TILE * SSM_STATE
SCAN_LEVELS = 8

COL_Q = 0
COL_ZA = 1024
COL_U = 2048
COL_ZS = 3072
COL_GA = 4096
COL_GS = 6144
COL_KV = 8192
IN_WIDTH = 8704
TN = 512
CAST_ROWS = 16
KV_TILE = COL_KV // TN
VMEM_LIMIT = 56 * 1024 * 1024
NEG = -1e30
NT_DIMS = (((1,), (1,)), ((), ()))


def _cmul(ar, ai, br, bi):
    return ar * br - ai * bi, ar * bi + ai * br


def _rmsnorm_kernel(x_ref, gain_ref, o_ref):
    x = x_ref[...]
    r = lax.rsqrt(jnp.mean(x * x, axis=-1, keepdims=True) + EPS)
    o_ref[...] = (x * r * gain_ref[...]).astype(BF16)


def _rmsnorm(x, gain, tm):
    m = x.shape[0]
    return pl.pallas_call(
        _rmsnorm_kernel,
        grid=(m // tm,),
        in_specs=[pl.BlockSpec((tm, D_MODEL), lambda i: (i, 0)),
                  pl.BlockSpec((1, D_MODEL), lambda i: (0, 0))],
        out_specs=pl.BlockSpec((tm, D_MODEL), lambda i: (i, 0)),
        out_shape=jax.ShapeDtypeStruct((m, D_MODEL), BF16),
        compiler_params=pltpu.CompilerParams(
            dimension_semantics=("arbitrary",), vmem_limit_bytes=VMEM_LIMIT),
        name="rmsnorm",
    )(x, gain)


def _inproj_epilogue(j, o_ref, qg_ref, kg_ref, e_ref, cos_ref, sa_ref, sb_ref):
    def head_norm_rope(n_chunks, gain_ref, scale):
        cos, sa, sb = cos_ref[...], sa_ref[...], sb_ref[...]
        for c in range(n_chunks):
            cs = slice(c * LANES, (c + 1) * LANES)
            ac = o_ref[:, cs]
            sq = ac * ac
            hi = sq.astype(BF16)
            lo = (sq - hi.astype(F32)).astype(BF16)
            ms = jnp.dot(jnp.concatenate([hi, lo], axis=1), e_ref[...],
                         preferred_element_type=F32)
            xc = ac * lax.rsqrt(ms + EPS) * gain_ref[:, cs]
            o_ref[:, cs] = (xc * cos + pltpu.roll(xc, LANES - HEAD_DIM // 2, 1) * sa
                            + pltpu.roll(xc, HEAD_DIM // 2, 1) * sb) * scale

    @pl.when(j < COL_ZA // TN)
    def _():
        head_norm_rope(TN // LANES, qg_ref, HEAD_DIM ** -0.5)

    @pl.when(j == KV_TILE)
    def _():
        head_norm_rope(KV_WIDTH // LANES, kg_ref, 1.0)


def _inproj_kernel(x_ref, gain_ref, xs_ref, w_ref, qg_ref, kg_ref, e_ref, cos_ref, sa_ref, sb_ref,
                   cos_s_ref, sa_s_ref, sb_s_ref, *rest, mw_slabs):
    n_mw = (len(rest) - 4) // 2
    mw_refs, (o_ref, os_ref) = rest[:n_mw], rest[n_mw:n_mw + 2]
    mw_out_refs, (xn_ref, wbf_ref) = rest[n_mw + 2:2 * n_mw + 2], rest[2 * n_mw + 2:]
    h, j, i = pl.program_id(0), pl.program_id(1), pl.program_id(2)
    tm = o_ref.shape[0]
    rows = pl.ds(pl.multiple_of(i * tm, tm), tm)

    @pl.when(j == 0)
    def _():
        x = x_ref[...]
        r = lax.rsqrt(jnp.mean(x * x, axis=-1, keepdims=True) + EPS)
        xn_ref[rows, :] = (x * r * gain_ref[...]).astype(BF16)

    @pl.when(i == 0)
    def _():
        wbf_ref[...] = w_ref[...].astype(BF16)

    o_ref[...] = jnp.dot(xn_ref[rows, :], wbf_ref[...], preferred_element_type=F32)
    _inproj_epilogue(j, o_ref, qg_ref, kg_ref, e_ref, cos_ref, sa_ref, sb_ref)

    step = (h * pl.num_programs(1) + j) * pl.num_programs(2) + i
    for src, dst, n_slabs in zip(mw_refs, mw_out_refs, mw_slabs):
        @pl.when(step < n_slabs)
        def _():
            dst[...] = src[...].astype(BF16)

    @pl.when(jnp.logical_and(h == 0, jnp.logical_and(j == 0, i == 0)))
    def _():
        os_ref[...] = jnp.zeros(os_ref.shape, F32)

    @pl.when(jnp.logical_and(h == pl.num_programs(0) - 1, i == pl.num_programs(2) - 1))
    def _():
        os_ref[...] = jnp.dot(xs_ref[...], wbf_ref[...], preferred_element_type=F32)
        _inproj_epilogue(j, os_ref, qg_ref, kg_ref, e_ref, cos_s_ref, sa_s_ref, sb_s_ref)


def _w_tile(j):
    n_q = COL_ZA // TN
    return jnp.where(j < n_q, j, jnp.where(j == KV_TILE, n_q, j + 1))


def _inproj(x, gain, xs, w, qg, kg, e, tabs_p, tabs_s, merge_weights, tm, halves):
    m, ms = x.shape[0], xs.shape[0]
    rows_half = m // halves
    tiles = rows_half // tm
    n_tab = tabs_p[0].shape[0] // tm
    n_col = IN_WIDTH // TN
    const = lambda shape: pl.BlockSpec(shape, lambda h, j, i: (0, 0))
    tab_p = pl.BlockSpec((tm, LANES), lambda h, j, i: ((h * tiles + i) % n_tab, 0))
    n_steps = halves * n_col * tiles
    for mw in merge_weights:
        assert mw.shape[0] % CAST_ROWS == 0 and mw.shape[0] // CAST_ROWS <= n_steps
    slab = lambda mw: pl.BlockSpec(
        (CAST_ROWS, mw.shape[1]),
        lambda h, j, i: (jnp.minimum((h * n_col + j) * tiles + i, mw.shape[0] // CAST_ROWS - 1), 0))
    return pl.pallas_call(
        functools.partial(_inproj_kernel,
                          mw_slabs=tuple(mw.shape[0] // CAST_ROWS for mw in merge_weights)),
        grid=(halves, IN_WIDTH // TN, tiles),
        in_specs=[
            pl.BlockSpec((tm, D_MODEL),
                         lambda h, j, i: (h * tiles + jnp.where(j == 0, i, tiles - 1), 0)),
            const((1, D_MODEL)),
            const((ms, D_MODEL)),
            pl.BlockSpec((D_MODEL, TN), lambda h, j, i: (0, _w_tile(j))),
            const((1, TN)), const((1, KV_WIDTH)), const((2 * LANES, LANES)),
            tab_p, tab_p, tab_p,
            const((ms, LANES)), const((ms, LANES)), const((ms, LANES)),
        ] + [slab(mw) for mw in merge_weights],
        out_specs=[pl.BlockSpec((tm, TN), lambda h, j, i: (h * tiles + i, j)),
                   pl.BlockSpec((ms, TN),
                                lambda h, j, i: (0, jnp.where(h == halves - 1, j, n_col)))
                   ] + [slab(mw) for mw in merge_weights],
        out_shape=[jax.ShapeDtypeStruct((m, IN_WIDTH), F32),
                   jax.ShapeDtypeStruct((ms, IN_WIDTH + TN), F32)
                   ] + [jax.ShapeDtypeStruct(mw.shape, BF16) for mw in merge_weights],
        scratch_shapes=[pltpu.VMEM((rows_half, D_MODEL), BF16), pltpu.VMEM((D_MODEL, TN), BF16)],
        compiler_params=pltpu.CompilerParams(
            dimension_semantics=("arbitrary", "arbitrary", "arbitrary"),
            vmem_limit_bytes=VMEM_LIMIT),
        name="inproj",
    )(x, gain, xs, w, qg, kg, e, *tabs_p, *tabs_s, *merge_weights)


def _dup_head(x, g, lo):
    r = pltpu.roll(x, HEAD_DIM, 1)
    return jnp.where(lo, x, r) if g % 2 == 0 else jnp.where(lo, r, x)


def _prompt_attn_kernel(sinks_ref, q_ref, kvc_ref, kvp_ref, kvm_ref, vmt_ref, bias_ref, o_ref):
    lo = lax.broadcasted_iota(jnp.int32, (1, LANES), 1) < HEAD_DIM
    head_lane = lax.broadcasted_iota(jnp.int32, (1, 4 * BLOCK), 1) // BLOCK
    bias = bias_ref[0]
    scores = []
    for g in range(N_KV_HEADS):
        ks = slice((g // 2) * LANES, (g // 2 + 1) * LANES)
        k_all = jnp.concatenate([_dup_head(kvp_ref[:, ks], g, lo), _dup_head(kvc_ref[:, ks], g, lo),
                                 _dup_head(kvm_ref[:, ks], g, lo)], axis=0).astype(BF16)
        qa = q_ref[:, (2 * g) * LANES:(2 * g + 1) * LANES]
        qb = q_ref[:, (2 * g + 1) * LANES:(2 * g + 2) * LANES]
        q4 = jnp.concatenate([jnp.where(lo, qa, 0.0), jnp.where(lo, 0.0, qa),
                              jnp.where(lo, qb, 0.0), jnp.where(lo, 0.0, qb)],
                             axis=0).astype(BF16)
        scores.append(lax.dot_general(k_all, q4, NT_DIMS, preferred_element_type=F32))
    v_t = jnp.concatenate([kvp_ref[:, KV_WIDTH:].T, kvc_ref[:, KV_WIDTH:].T], axis=1)
    for g in range(N_KV_HEADS):
        s = scores[g]
        s_w = s[:2 * BLOCK] + bias
        s_m = s[2 * BLOCK:]
        sink = jnp.where(head_lane == 0, sinks_ref[4 * g],
                         jnp.where(head_lane == 1, sinks_ref[4 * g + 1],
                                   jnp.where(head_lane == 2, sinks_ref[4 * g + 2],
                                             sinks_ref[4 * g + 3])))
        m = jnp.maximum(jnp.maximum(jnp.max(s_w, axis=0, keepdims=True),
                                    jnp.max(s_m, axis=0, keepdims=True)), sink)
        p_w = jnp.exp(s_w - m)
        p_m = jnp.exp(s_m - m)
        den = (jnp.sum(p_w, axis=0, keepdims=True) + jnp.sum(p_m, axis=0, keepdims=True)
               + jnp.exp(sink - m))
        inv = 1.0 / den
        p = jnp.concatenate([p_w * inv, p_m * inv], axis=0).astype(BF16)
        hs = slice(g * HEAD_DIM, (g + 1) * HEAD_DIM)
        vt_g = jnp.concatenate([v_t[hs], vmt_ref[hs, :]], axis=1).astype(BF16)
        o_t = jnp.dot(vt_g, p, preferred_element_type=F32)
        for pair in range(2):
            two = jnp.concatenate([o_t[:, (2 * pair) * BLOCK:(2 * pair + 1) * BLOCK],
                                   o_t[:, (2 * pair + 1) * BLOCK:(2 * pair + 2) * BLOCK]], axis=0)
            o_ref[:, (2 * g + pair) * LANES:(2 * g + pair + 1) * LANES] = two.T


def _prompt_attention(sinks, h_p, h_s, vm_t, batch, seq):
    nb = seq // BLOCK
    kvb = COL_KV // (2 * KV_WIDTH)
    kj = np.arange(2 * BLOCK)[:, None]
    qi = np.arange(4 * BLOCK)[None, :] % BLOCK
    cur = (kj >= BLOCK) & (kj - BLOCK <= qi)
    prev = (kj < BLOCK) & (kj > qi)
    bias = jnp.asarray(np.where(np.stack([cur, cur | prev]), 0.0, NEG), F32)
    return pl.pallas_call(
        _prompt_attn_kernel,
        grid=(batch, nb),
        in_specs=[
            pl.BlockSpec(memory_space=pltpu.SMEM),
            pl.BlockSpec((BLOCK, ATTN_WIDTH), lambda b, i: (b * nb + i, 0)),
            pl.BlockSpec((BLOCK, 2 * KV_WIDTH), lambda b, i: (b * nb + i, kvb)),
            pl.BlockSpec((BLOCK, 2 * KV_WIDTH),
                         lambda b, i: (b * nb + jnp.maximum(i - 1, 0), kvb)),
            pl.BlockSpec((N_META, 2 * KV_WIDTH), lambda b, i: (h_s.shape[0] // N_META - 1, kvb)),
            pl.BlockSpec((KV_WIDTH, N_META), lambda b, i: (0, 0)),
            pl.BlockSpec((1, 2 * BLOCK, 4 * BLOCK), lambda b, i: (jnp.minimum(i, 1), 0, 0)),
        ],
        out_specs=pl.BlockSpec((BLOCK, ATTN_WIDTH), lambda b, i: (b * nb + i, 0)),
        out_shape=jax.ShapeDtypeStruct((batch * seq, ATTN_WIDTH), F32),
        compiler_params=pltpu.CompilerParams(
            dimension_semantics=("arbitrary", "arbitrary"), vmem_limit_bytes=VMEM_LIMIT),
        name="prompt_attn",
    )(sinks, h_p, h_p, h_p, h_s, vm_t, bias)


def _sample_attn_kernel(qp_ref, kn_ref, vn_ref, ck_ref, cv_ref, mk_ref, mv_ref, sink_ref,
                        o_ref, ok_ref, ov_ref):
    bt = ck_ref.shape[0]
    last = lax.broadcasted_iota(jnp.int32, (1, WINDOW), 1) == WINDOW - 1
    kn = kn_ref[0]
    vn = vn_ref[0]
    ck = pltpu.roll(ck_ref[...], WINDOW - 1, 2)
    cv = pltpu.roll(cv_ref[...], WINDOW - 1, 2)
    k_new = jnp.stack([jnp.where(last, kn[:, b:b + 1], ck[b]) for b in range(bt)])
    v_new = jnp.stack([jnp.where(last, vn[:, b:b + 1], cv[b]) for b in range(bt)])
    ok_ref[...] = k_new
    ov_ref[...] = v_new
    qp = qp_ref[...].astype(BF16)
    k_w = k_new.astype(BF16)
    v_w = v_new.astype(BF16)
    k_m = mk_ref[...].astype(BF16)
    v_m = mv_ref[...].astype(BF16)
    s_w = jnp.einsum("bhc,bcw->bhw", qp, k_w, preferred_element_type=F32)
    s_m = jnp.einsum("bhc,bkc->bhk", qp, k_m, preferred_element_type=F32)
    sink = sink_ref[...][None]
    m = jnp.maximum(jnp.maximum(jnp.max(s_w, axis=-1, keepdims=True),
                                jnp.max(s_m, axis=-1, keepdims=True)), sink)
    p_w = jnp.exp(s_w - m)
    p_m = jnp.exp(s_m - m)
    den = (jnp.sum(p_w, axis=-1, keepdims=True) + jnp.sum(p_m, axis=-1, keepdims=True)
           + jnp.exp(sink - m))
    inv = 1.0 / den
    o = (jnp.einsum("bhw,bcw->bhc", (p_w * inv).astype(BF16), v_w, preferred_element_type=F32)
         + jnp.einsum("bhk,bkc->bhc", (p_m * inv).astype(BF16), v_m,
                      preferred_element_type=F32))
    hh = lax.broadcasted_iota(jnp.int32, (1, N_Q_HEADS, KV_WIDTH), 1) // (N_Q_HEADS // N_KV_HEADS)
    cc = lax.broadcasted_iota(jnp.int32, (1, N_Q_HEADS, KV_WIDTH), 2) // HEAD_DIM
    o = jnp.where(hh == cc, o, 0.0)
    o = o[:, :, :LANES] + o[:, :, LANES:]
    o_ref[...] = o + pltpu.roll(o, HEAD_DIM, 2)


def _sample_attention(qp, kn_t, vn_t, cache_k, cache_v, meta_k, meta_v, sink_col, bt):
    n = cache_k.shape[0]
    win_spec = pl.BlockSpec((bt, KV_WIDTH, WINDOW), lambda t: (t, 0, 0))
    new_spec = pl.BlockSpec((1, KV_WIDTH, bt), lambda t: (t, 0, 0))
    meta_spec = pl.BlockSpec((bt, N_META, KV_WIDTH), lambda t: (t, 0, 0))
    return pl.pallas_call(
        _sample_attn_kernel,
        grid=(n // bt,),
        in_specs=[
            pl.BlockSpec((bt, N_Q_HEADS, KV_WIDTH), lambda t: (t, 0, 0)),
            new_spec, new_spec, win_spec, win_spec, meta_spec, meta_spec,
            pl.BlockSpec((N_Q_HEADS, 1), lambda t: (0, 0)),
        ],
        out_specs=[
            pl.BlockSpec((bt, N_Q_HEADS, LANES), lambda t: (t, 0, 0)),
            win_spec, win_spec,
        ],
        out_shape=[
            jax.ShapeDtypeStruct((n, N_Q_HEADS, LANES), F32),
            jax.ShapeDtypeStruct(cache_k.shape, F32),
            jax.ShapeDtypeStruct(cache_v.shape, F32),
        ],
        compiler_params=pltpu.CompilerParams(
            dimension_semantics=("arbitrary",), vmem_limit_bytes=VMEM_LIMIT),
        name="sample_attn",
    )(qp, kn_t, vn_t, cache_k, cache_v, meta_k, meta_v, sink_col)


def _gelu_tanh(x):
    return 0.5 * x * (1.0 + jnp.tanh(0.7978845608028654 * (x + 0.044715 * (x * x * x))))


def _ssm_kernel(u_ref, um_ref, us_ref, hr_ref, hi_ref, t8_ref, win_ref, wst_ref, wc_ref,
                ar_ref, ai_ref, lr_ref, li_ref, pr_ref, pi_ref, d_ref,
                y_ref, sre_ref, sim_ref, ys_ref, nr_ref, ni_ref, xr_ref, xi_ref):
    dsk = d_ref[...]
    us = us_ref[...]
    bu = jnp.dot(us.astype(BF16), win_ref[0, (CHUNK - 1) * LANES:, :],
                 preferred_element_type=F32)
    l_r, l_i = lr_ref[0], li_ref[0]
    h_r, h_i = hr_ref[...], hi_ref[...]
    n_r = l_r * h_r - l_i * h_i + bu[:, :STATE_TILE]
    n_i = l_r * h_i + l_i * h_r + bu[:, STATE_TILE:]
    nr_ref[...] = n_r
    ni_ref[...] = n_i
    hn = jnp.concatenate([n_r, n_i], axis=1).astype(BF16)
    ys_ref[...] = _gelu_tanh(
        lax.dot_general(hn, wc_ref[0], NT_DIMS, preferred_element_type=F32) + dsk * us)

    batch = sre_ref.shape[0]
    nc = u_ref.shape[0] // CHUNK // batch
    win = win_ref[0]
    dsk_row = jnp.concatenate([dsk] * CHUNK, axis=1)

    def chunks_of(b):
        r0 = b * nc * CHUNK
        return jnp.concatenate(
            [u_ref[pl.ds(r0 + j, nc, stride=CHUNK), :] for j in range(CHUNK)], axis=1)

    n_mc = um_ref.shape[0] // CHUNK
    um = jnp.concatenate([um_ref[pl.ds(j, n_mc, stride=CHUNK), :] for j in range(CHUNK)], axis=1)
    gm = jnp.dot(um.astype(BF16), win, preferred_element_type=F32)
    a_r, a_i = ar_ref[0, 0:1, :], ai_ref[0, 0:1, :]
    s0r, s0i = gm[0:1, :STATE_TILE], gm[0:1, STATE_TILE:]
    for c in range(1, n_mc):
        s0r, s0i = (a_r * s0r - a_i * s0i + gm[c:c + 1, :STATE_TILE],
                    a_r * s0i + a_i * s0r + gm[c:c + 1, STATE_TILE:])

    n_t = nc // SUBLANES
    row = lax.broadcasted_iota(jnp.int32, (nc, 1), 0)
    first = row == 0
    in_tile = row % SUBLANES
    t_row = lax.broadcasted_iota(jnp.int32, (n_t, 1), 0)
    first_t = t_row == 0
    n_local = SUBLANES.bit_length() - 1
    w_r, w_i = pr_ref[0], pi_ref[0]

    def tile_totals(scr_ref, x):
        parts = []
        for c in range(STATE_TILE // LANES):
            scr_ref[c] = x[:, c * LANES:(c + 1) * LANES]
            parts.append(scr_ref[c, pl.ds(SUBLANES - 1, n_t, stride=SUBLANES), :])
        return jnp.concatenate(parts, axis=1)

    def shifted(x, d, keep):
        if d < SUBLANES:
            return jnp.where(keep, pltpu.roll(x, d, 0), 0.0)
        return jnp.concatenate([jnp.zeros((d, STATE_TILE), F32), x[:x.shape[0] - d]], axis=0)

    ucat = chunks_of(0)
    ub = ucat.astype(BF16)
    g = jnp.dot(ub, win, preferred_element_type=F32)
    for b in range(batch):
        y_intra = jnp.dot(ub, t8_ref[0], preferred_element_type=F32) + dsk_row * ucat
        if b + 1 < batch:
            ucat_next = chunks_of(b + 1)
            ub_next = ucat_next.astype(BF16)
            g_next = jnp.dot(ub_next, win, preferred_element_type=F32)
        gr, gi = g[:, :STATE_TILE], g[:, STATE_TILE:]
        for k in range(n_local):
            d = 1 << k
            k_r, k_i = ar_ref[0, k:k + 1, :], ai_ref[0, k:k + 1, :]
            sh_r, sh_i = shifted(gr, d, in_tile >= d), shifted(gi, d, in_tile >= d)
            gr, gi = gr + (k_r * sh_r - k_i * sh_i), gi + (k_r * sh_i + k_i * sh_r)
        tot_r, tot_i = tile_totals(xr_ref, gr), tile_totals(xi_ref, gi)
        k_r, k_i = ar_ref[0, n_local:n_local + 1, :], ai_ref[0, n_local:n_local + 1, :]
        tot_r = tot_r + jnp.where(first_t, k_r * s0r - k_i * s0i, 0.0)
        tot_i = tot_i + jnp.where(first_t, k_r * s0i + k_i * s0r, 0.0)
        for k in range(SCAN_LEVELS - n_local):
            d = 1 << k
            if d >= n_t:
                break
            k_r = ar_ref[0, n_local + k:n_local + k + 1, :]
            k_i = ai_ref[0, n_local + k:n_local + k + 1, :]
            sh_r, sh_i = shifted(tot_r, d, t_row >= d), shifted(tot_i, d, t_row >= d)
            tot_r, tot_i = tot_r + (k_r * sh_r - k_i * sh_i), tot_i + (k_r * sh_i + k_i * sh_r)
        sre_ref[b, 0] = tot_r[n_t - 1:n_t, :]
        sim_ref[b, 0] = tot_i[n_t - 1:n_t, :]
        c_r = jnp.where(first_t, s0r, pltpu.roll(tot_r, 1, 0))
        c_i = jnp.where(first_t, s0i, pltpu.roll(tot_i, 1, 0))
        full_r, full_i = [], []
        for t in range(n_t):
            rs = slice(t * SUBLANES, (t + 1) * SUBLANES)
            e_r, e_i = c_r[t:t + 1, :], c_i[t:t + 1, :]
            full_r.append(gr[rs] + (w_r * e_r - w_i * e_i))
            full_i.append(gi[rs] + (w_r * e_i + w_i * e_r))
        gr, gi = jnp.concatenate(full_r, axis=0), jnp.concatenate(full_i, axis=0)
        sp = jnp.concatenate([jnp.where(first, s0r, pltpu.roll(gr, 1, 0)),
                              jnp.where(first, s0i, pltpu.roll(gi, 1, 0))],
                             axis=1).astype(BF16)
        y2 = y_intra + lax.dot_general(sp, wst_ref[0], NT_DIMS, preferred_element_type=F32)
        for j in range(CHUNK):
            sl = slice(j * LANES, (j + 1) * LANES)
            y_ref[pl.ds(b * nc * CHUNK + j, nc, stride=CHUNK), :] = _gelu_tanh(y2[:, sl])
        if b + 1 < batch:
            ucat, ub, g = ucat_next, ub_next, g_next


def _ssm(h_p, h_s, h0r, h0i, t8, win, wst, wc, a_r, a_i, l_r, l_i, p_r, p_i, d_skip, batch, n_s):
    ub = COL_U // LANES
    rows = h_p.shape[0]
    wspec = pl.BlockSpec((1, CHUNK_WIDTH, CHUNK_WIDTH), lambda gt: (gt, 0, 0))
    aspec = pl.BlockSpec((1, SCAN_LEVELS, STATE_TILE), lambda gt: (gt, 0, 0))
    sspec = pl.BlockSpec((batch, 1, 1, STATE_TILE), lambda gt: (0, gt, 0, 0))
    st = pl.BlockSpec((n_s, STATE_TILE), lambda gt: (0, gt))
    lam = pl.BlockSpec((1, 1, STATE_TILE), lambda gt: (gt, 0, 0))
    return pl.pallas_call(
        _ssm_kernel,
        grid=(N_GROUP_TILES,),
        in_specs=[
            pl.BlockSpec((rows, LANES), lambda gt: (0, ub + gt)),
            pl.BlockSpec((N_META, LANES), lambda gt: (h_s.shape[0] // N_META - 1, ub + gt)),
            pl.BlockSpec((n_s, LANES), lambda gt: (0, ub + gt)),
            st, st,
            wspec, wspec, wspec,
            pl.BlockSpec((1, LANES, 2 * STATE_TILE), lambda gt: (gt, 0, 0)),
            aspec, aspec, lam, lam,
            pl.BlockSpec((1, SUBLANES, STATE_TILE), lambda gt: (gt, 0, 0)),
            pl.BlockSpec((1, SUBLANES, STATE_TILE), lambda gt: (gt, 0, 0)),
            pl.BlockSpec((1, LANES), lambda gt: (0, gt)),
        ],
        out_specs=[pl.BlockSpec((rows, LANES), lambda gt: (0, gt)), sspec, sspec,
                   pl.BlockSpec((n_s, LANES), lambda gt: (0, gt)), st, st],
        out_shape=[
            jax.ShapeDtypeStruct((rows, SSM_WIDTH), F32),
            jax.ShapeDtypeStruct((batch, N_GROUP_TILES, 1, STATE_TILE), F32),
            jax.ShapeDtypeStruct((batch, N_GROUP_TILES, 1, STATE_TILE), F32),
            jax.ShapeDtypeStruct((n_s, SSM_WIDTH), F32),
            jax.ShapeDtypeStruct((n_s, N_SSM_GROUPS * SSM_STATE), F32),
            jax.ShapeDtypeStruct((n_s, N_SSM_GROUPS * SSM_STATE), F32),
        ],
        scratch_shapes=[pltpu.VMEM((STATE_TILE // LANES, rows // CHUNK // batch, LANES), F32)] * 2,
        compiler_params=pltpu.CompilerParams(
            dimension_semantics=("arbitrary",), vmem_limit_bytes=VMEM_LIMIT),
        name="ssm",
    )(h_p, h_s, h_s, h0r, h0i, t8, win, wst, wc, a_r, a_i, l_r, l_i, p_r, p_i, d_skip)


def _merge_kernel(o_ref, za_ref, zs_ref, ga_ref, gs_ref, y_ref, x_ref,
                  o2_ref, za2_ref, zs2_ref, ga2_ref, gs2_ref, y2_ref, x2_ref,
                  bglu_ref, wglu_ref, wa_ref, ws_ref, wo_ref, out_ref, out2_ref):
    def merge(o_r, za_r, zs_r, ga_r, gs_r, y_r, x_r, out_r):
        y = y_r[...]
        t = jnp.dot(y.astype(BF16), wglu_ref[...], preferred_element_type=F32) + bglu_ref[...]
        y = y * jax.nn.sigmoid(t)
        za = za_r[...]
        zs = zs_r[...]
        a_in = (o_r[...] * (za * jax.nn.sigmoid(za))).astype(BF16)
        s_in = (y * (zs * jax.nn.sigmoid(zs))).astype(BF16)
        br_a = jnp.dot(a_in, wa_ref[...], preferred_element_type=F32)
        br_s = jnp.dot(s_in, ws_ref[...], preferred_element_type=F32)
        mix = (jax.nn.sigmoid(ga_r[...]) * br_a + jax.nn.sigmoid(gs_r[...]) * br_s).astype(BF16)
        out_r[...] = x_r[...] + jnp.dot(mix, wo_ref[...], preferred_element_type=F32)

    merge(o_ref, za_ref, zs_ref, ga_ref, gs_ref, y_ref, x_ref, out_ref)

    @pl.when(pl.program_id(0) == pl.num_programs(0) - 1)
    def _():
        merge(o2_ref, za2_ref, zs2_ref, ga2_ref, gs2_ref, y2_ref, x2_ref, out2_ref)


def _merge(o, h, y, x, o2, h2, y2, x2, b_glu, w_glu, w_a, w_s, w_o, tm):
    m, m2 = x.shape[0], x2.shape[0]
    const = lambda shape: pl.BlockSpec(shape, lambda i: (0, 0), pipeline_mode=pl.Buffered(1))
    rows = lambda width, col: pl.BlockSpec((tm, width), lambda i: (i, col))
    rows2 = lambda width, col: pl.BlockSpec((m2, width), lambda i: (0, col),
                                            pipeline_mode=pl.Buffered(1))
    operands = lambda spec: [
        spec(ATTN_WIDTH, 0), spec(ATTN_WIDTH, COL_ZA // ATTN_WIDTH),
        spec(SSM_WIDTH, COL_ZS // SSM_WIDTH), spec(D_MODEL, COL_GA // D_MODEL),
        spec(D_MODEL, COL_GS // D_MODEL), spec(SSM_WIDTH, 0), spec(D_MODEL, 0)]
    return pl.pallas_call(
        _merge_kernel,
        grid=(m // tm,),
        in_specs=operands(rows) + operands(rows2) + [
            const((1, SSM_WIDTH)),
            const((SSM_WIDTH, SSM_WIDTH)),
            const((ATTN_WIDTH, D_MODEL)),
            const((SSM_WIDTH, D_MODEL)),
            const((D_MODEL, D_MODEL)),
        ],
        out_specs=[pl.BlockSpec((tm, D_MODEL), lambda i: (i, 0)),
                   pl.BlockSpec((m2, D_MODEL), lambda i: (0, 0))],
        out_shape=[jax.ShapeDtypeStruct((m, D_MODEL), F32),
                   jax.ShapeDtypeStruct((m2, D_MODEL), F32)],
        compiler_params=pltpu.CompilerParams(
            dimension_semantics=("arbitrary",), vmem_limit_bytes=VMEM_LIMIT),
        name="merge",
    )(o, h, h, h, h, y, x, o2, h2, h2, h2, h2, y2, x2, b_glu, w_glu, w_a, w_s, w_o)


def _rope_tables(pos):
    half = HEAD_DIM // 2
    inv_freq = (ROPE_THETA ** (-np.arange(half, dtype=np.float32) / half)).astype(np.float32)
    ang = np.asarray(pos, np.float32)[:, None] * inv_freq[None, :]
    cos, sin = np.cos(ang).astype(np.float32), np.sin(ang).astype(np.float32)
    zero = np.zeros_like(sin)
    reps = LANES // HEAD_DIM
    cos_t = np.tile(np.concatenate([cos, cos], axis=1), (1, reps))
    sin_a = np.tile(np.concatenate([-sin, zero], axis=1), (1, reps))
    sin_b = np.tile(np.concatenate([zero, sin], axis=1), (1, reps))
    return jnp.asarray(cos_t), jnp.asarray(sin_a), jnp.asarray(sin_b)


def _split_bf16(x):
    hi = x.astype(BF16)
    return hi, (x - hi.astype(F32)).astype(BF16)


def _dot_nt_f32(a, b):
    ah, al = _split_bf16(a)
    bh, bl = _split_bf16(b)
    dot = lambda x, y: lax.dot_general(x, y, NT_DIMS, preferred_element_type=F32)
    return dot(ah, bh) + dot(ah, bl) + dot(al, bh)


def _ssm_tables_kernel(are_ref, aim_ref, ldt_ref, bre_ref, bim_ref, cre_ref, cim_ref,
                       win_ref, wst_ref, t8_ref, wc_ref, ar_ref, ai_ref, lr_ref, li_ref,
                       pr_ref, pi_ref):
    a_r, a_i = are_ref[0], aim_ref[0]
    dt = jnp.exp(ldt_ref[0])
    mag = jnp.exp(a_r * dt)
    l_r, l_i = mag * jnp.cos(a_i * dt), mag * jnp.sin(a_i * dt)
    den = a_r * a_r + a_i * a_i
    f_r = ((l_r - 1.0) * a_r + l_i * a_i) / den
    f_i = (l_i * a_r - (l_r - 1.0) * a_i) / den
    shape = (LANES, STATE_TILE)
    own = (lax.broadcasted_iota(jnp.int32, shape, 0) // SSM_GROUP
           == lax.broadcasted_iota(jnp.int32, shape, 1) // SSM_STATE)
    sq = (LANES, LANES)
    same_group = (lax.broadcasted_iota(jnp.int32, sq, 0) // SSM_GROUP
                  == lax.broadcasted_iota(jnp.int32, sq, 1) // SSM_GROUP)
    lo = lax.broadcasted_iota(jnp.int32, (1, LANES), 1) < SSM_STATE
    n_rep = STATE_TILE // LANES

    def expand(x):
        return jnp.where(own, jnp.concatenate([x] * n_rep, axis=1), 0.0)

    def compact(v):
        e = jnp.where(own, jnp.broadcast_to(v, shape), 0.0)
        s = e[:, :LANES]
        for k in range(1, n_rep):
            s = s + e[:, k * LANES:(k + 1) * LANES]
        return s + pltpu.roll(s, SSM_STATE, 1)

    lc_r, lc_i = compact(l_r), compact(l_i)
    b_r, b_i = bre_ref[0], bim_ref[0]
    c_r, c_i = cre_ref[0], cim_ref[0]
    bb_r, bb_i = _cmul(compact(f_r), compact(f_i), b_r, b_i)
    wc_ref[0] = jnp.concatenate([expand(c_r), expand(-c_i)], axis=1).astype(BF16)
    c_cat = jnp.where(lo, c_r, -c_i)
    q_r, q_i = jnp.ones(sq, F32), jnp.zeros(sq, F32)
    for n in range(CHUNK):
        x_r, x_i = _cmul(q_r, q_i, bb_r, bb_i)
        win_ref[0, (CHUNK - 1 - n) * LANES:(CHUNK - n) * LANES, :] = jnp.concatenate(
            [expand(x_r), expand(x_i)], axis=1).astype(BF16)
        k_n = _dot_nt_f32(jnp.where(lo, x_r, x_i), c_cat)
        k_n = jnp.where(same_group, k_n, 0.0).astype(BF16)
        for j in range(CHUNK - n):
            t8_ref[0, j * LANES:(j + 1) * LANES, (j + n) * LANES:(j + n + 1) * LANES] = k_n
            if n > 0:
                t8_ref[0, (j + n) * LANES:(j + n + 1) * LANES, j * LANES:(j + 1) * LANES] = (
                    jnp.zeros(sq, BF16))
        q_r, q_i = _cmul(q_r, q_i, lc_r, lc_i)
        s_r, s_i = _cmul(c_r, c_i, q_r, q_i)
        wst_ref[0, n * LANES:(n + 1) * LANES, :] = jnp.concatenate(
            [expand(s_r), expand(-s_i)], axis=1).astype(BF16)
    p_r, p_i = l_r, l_i
    for _ in range(CHUNK.bit_length() - 1):
        p_r, p_i = _cmul(p_r, p_i, p_r, p_i)
    t_r, t_i = p_r, p_i
    for i in range(SUBLANES):
        pr_ref[0, i:i + 1, :] = t_r
        pi_ref[0, i:i + 1, :] = t_i
        t_r, t_i = _cmul(t_r, t_i, p_r, p_i)
    for k in range(SCAN_LEVELS):
        ar_ref[0, k:k + 1, :] = p_r
        ai_ref[0, k:k + 1, :] = p_i
        p_r, p_i = _cmul(p_r, p_i, p_r, p_i)
    lr_ref[0] = l_r
    li_ref[0] = l_i


def _ssm_tables(a_re, a_im, log_dt, b_re, b_im, c_re, c_im):
    gt = N_GROUP_TILES
    lane_vec = lambda v: v.reshape(gt, 1, STATE_TILE)
    dup = lambda v: jnp.concatenate([v, v], axis=-1)
    rows = GROUPS_PER_TILE * SSM_GROUP
    b_t = lambda v: dup(v.transpose(0, 2, 1).reshape(gt, rows, SSM_STATE))
    c_t = lambda v: dup(v.reshape(gt, rows, SSM_STATE))
    vec = pl.BlockSpec((1, 1, STATE_TILE), lambda g: (g, 0, 0))
    par = pl.BlockSpec((1, rows, LANES), lambda g: (g, 0, 0))
    big = pl.BlockSpec((1, CHUNK_WIDTH, CHUNK_WIDTH), lambda g: (g, 0, 0))
    wcs = pl.BlockSpec((1, LANES, 2 * STATE_TILE), lambda g: (g, 0, 0))
    scan = pl.BlockSpec((1, SCAN_LEVELS, STATE_TILE), lambda g: (g, 0, 0))
    tile = pl.BlockSpec((1, SUBLANES, STATE_TILE), lambda g: (g, 0, 0))
    return pl.pallas_call(
        _ssm_tables_kernel,
        grid=(gt,),
        in_specs=[vec, vec, vec, par, par, par, par],
        out_specs=[big, big, big, wcs, scan, scan, vec, vec, tile, tile],
        out_shape=[
            jax.ShapeDtypeStruct((gt, CHUNK_WIDTH, 2 * STATE_TILE), BF16),
            jax.ShapeDtypeStruct((gt, CHUNK_WIDTH, 2 * STATE_TILE), BF16),
            jax.ShapeDtypeStruct((gt, CHUNK_WIDTH, CHUNK_WIDTH), BF16),
            jax.ShapeDtypeStruct((gt, LANES, 2 * STATE_TILE), BF16),
            jax.ShapeDtypeStruct((gt, SCAN_LEVELS, STATE_TILE), F32),
            jax.ShapeDtypeStruct((gt, SCAN_LEVELS, STATE_TILE), F32),
            jax.ShapeDtypeStruct((gt, 1, STATE_TILE), F32),
            jax.ShapeDtypeStruct((gt, 1, STATE_TILE), F32),
            jax.ShapeDtypeStruct((gt, SUBLANES, STATE_TILE), F32),
            jax.ShapeDtypeStruct((gt, SUBLANES, STATE_TILE), F32),
        ],
        compiler_params=pltpu.CompilerParams(
            dimension_semantics=("arbitrary",), vmem_limit_bytes=VMEM_LIMIT),
        name="ssm_tables",
    )(lane_vec(a_re), lane_vec(a_im), lane_vec(jnp.repeat(log_dt, SSM_STATE)),
      b_t(b_re), b_t(b_im), c_t(c_re), c_t(c_im))


def kernel(x_prompt, x_sample, cache_win_k, cache_win_v, cache_meta_k, cache_meta_v,
           state_ssm_re, state_ssm_im, meta_tokens, norm_gain, w_in, q_norm_gain, k_norm_gain,
           sinks, a_re, a_im, log_dt, b_re, b_im, c_re, c_im, d_skip, w_glu, b_glu,
           w_attn_out, w_ssm_out, w_out):
    depth = w_in.shape[0]
    assert depth == 1, "single-layer trunk"
    batch, seq = x_prompt.shape[:2]
    n_s = x_sample.shape[0]
    assert x_sample.shape[1] == 1 and seq % (CHUNK * BLOCK) == 0
    l = 0

    w = w_in[l]
    qg = jnp.tile(q_norm_gain[l], TN // HEAD_DIM)[None]
    kg = jnp.tile(k_norm_gain[l], KV_WIDTH // HEAD_DIM)[None]
    lane = np.arange(LANES)
    e128 = np.where((lane[:, None] // HEAD_DIM) == (lane[None, :] // HEAD_DIM),
                    1.0 / HEAD_DIM, 0.0).astype(np.float32)
    e = jnp.asarray(np.concatenate([e128, e128], axis=0), BF16)
    gain = norm_gain[l][None]

    pos_p = N_META + np.arange(seq)
    pos_small = np.concatenate([np.full((n_s,), PAST_LEN), np.arange(N_META)])
    x_small = jnp.concatenate([x_sample[:, 0, :], meta_tokens.astype(x_prompt.dtype)], axis=0)
    xp = x_prompt.reshape(batch * seq, D_MODEL)
    xn_s = _rmsnorm(x_small, gain, tm=n_s + N_META)
    h_p, h_s, w_glu_bf, w_a_bf, w_s_bf, w_o_bf = _inproj(
        xp, gain, xn_s, w, qg, kg, e, _rope_tables(pos_p), _rope_tables(pos_small),
        (w_glu[l], w_attn_out[l], w_ssm_out[l], w_out[l]), tm=1024, halves=2)

    win, wst, t8, wc, a_r, a_i, l_r, l_i, p_r, p_i = _ssm_tables(
        a_re[l], a_im[l], log_dt[l], b_re[l], b_im[l], c_re[l], c_im[l])
    dsk = d_skip[l][None]

    vm_t = h_s[n_s:, COL_KV + KV_WIDTH:IN_WIDTH].T
    o_p = _prompt_attention(sinks[l], h_p, h_s, vm_t, batch, seq)
    q_s = h_s[:n_s, COL_Q:COL_Q + ATTN_WIDTH].reshape(n_s, N_Q_HEADS, 1, HEAD_DIM)
    sel = (np.arange(N_Q_HEADS)[:, None] // (N_Q_HEADS // N_KV_HEADS)
           == np.arange(N_KV_HEADS)[None, :]).astype(np.float32)
    qp = (q_s * sel[None, :, :, None]).reshape(n_s, N_Q_HEADS, KV_WIDTH)
    bt = 8
    to_t = lambda c: c.transpose(0, 2, 3, 1).reshape(n_s, KV_WIDTH, WINDOW)
    from_t = lambda c: c.reshape(n_s, N_KV_HEADS, HEAD_DIM, WINDOW).transpose(0, 3, 1, 2)[None]
    new_t = lambda c0: h_s[:n_s, c0:c0 + KV_WIDTH].reshape(n_s // bt, bt, KV_WIDTH).transpose(0, 2, 1)
    o_s, s_win_k, s_win_v = _sample_attention(
        qp, new_t(COL_KV), new_t(COL_KV + KV_WIDTH), to_t(cache_win_k[l]), to_t(cache_win_v[l]),
        cache_meta_k[l].reshape(n_s, N_META, KV_WIDTH), cache_meta_v[l].reshape(n_s, N_META, KV_WIDTH),
        sinks[l][:, None], bt=bt)
    o_s = o_s[:, :, :HEAD_DIM].reshape(n_s, ATTN_WIDTH)

    y_p, p_re, p_im, y_s, s_re, s_im = _ssm(
        h_p, h_s, state_ssm_re[l].reshape(n_s, -1), state_ssm_im[l].reshape(n_s, -1),
        t8, win, wst, wc, a_r, a_i, l_r, l_i, p_r, p_i, dsk, batch, n_s)
    y_prompt, y_sample = _merge(
        o_p, h_p, y_p, xp, o_s, h_s, y_s, x_sample[:, 0, :], b_glu[l][None],
        w_glu_bf, w_a_bf, w_s_bf, w_o_bf, tm=256)
    y_prompt = y_prompt.reshape(batch, seq, D_MODEL)
    y_sample = y_sample.reshape(n_s, 1, D_MODEL)

    kv_p = h_p.reshape(batch, seq, IN_WIDTH)[:, seq - WINDOW:, COL_KV:]
    kv_p = kv_p.reshape(batch, WINDOW, 2, N_KV_HEADS, HEAD_DIM)
    p_win_k = kv_p[:, :, 0][None]
    p_win_v = kv_p[:, :, 1][None]
    kv_m = h_s[n_s:, COL_KV:IN_WIDTH].reshape(N_META, 2, N_KV_HEADS, HEAD_DIM)
    p_meta_k = jnp.broadcast_to(kv_m[None, :, 0], (batch, N_META, N_KV_HEADS, HEAD_DIM))[None]
    p_meta_v = jnp.broadcast_to(kv_m[None, :, 1], (batch, N_META, N_KV_HEADS, HEAD_DIM))[None]
    p_ssm_re = p_re.reshape(batch, N_SSM_GROUPS, SSM_STATE)[None]
    p_ssm_im = p_im.reshape(batch, N_SSM_GROUPS, SSM_STATE)[None]
    st_shape = (1, n_s, N_SSM_GROUPS, SSM_STATE)
    return (y_prompt, y_sample, p_win_k, p_win_v, p_meta_k, p_meta_v, p_ssm_re, p_ssm_im,
            from_t(s_win_k), from_t(s_win_v), s_re.reshape(st_shape), s_im.reshape(st_shape))
```

```python
import functools

import jax
import jax.numpy as jnp
import numpy as np
from jax import lax
from jax.experimental import pallas as pl
from jax.experimental.pallas import tpu as pltpu

F32 = jnp.float32
BF16 = jnp.bfloat16

D_MODEL = 2048
N_META = 16
HEAD_DIM = 64
N_Q_HEADS = 16
N_KV_HEADS = 4
ATTN_WIDTH = N_Q_HEADS * HEAD_DIM
KV_WIDTH = N_KV_HEADS * HEAD_DIM
WINDOW = 128
BLOCK = 128
ROPE_THETA = 10000.0
SSM_WIDTH = D_MODEL // 2
SSM_GROUP = 16
N_SSM_GROUPS = SSM_WIDTH // SSM_GROUP
SSM_STATE = 64
EPS = 1e-6
PAST_LEN = 8192

LANES = 128
SUBLANES = 8
CHUNK = 8
GROUPS_PER_TILE = LANES // SSM_GROUP
N_GROUP_TILES = N_SSM_GROUPS // GROUPS_PER_TILE
CHUNK_WIDTH = CHUNK * LANES
STATE_TILE = GROUPS_PER_TILE * SSM_STATE
SCAN_LEVELS = 8

COL_Q = 0
COL_ZA = 1024
COL_U = 2048
COL_ZS = 3072
COL_GA = 4096
COL_GS = 6144
COL_KV = 8192
IN_WIDTH = 8704
TN = 512
CAST_ROWS = 16
KV_TILE = COL_KV // TN
VMEM_LIMIT = 56 * 1024 * 1024
NEG = -1e30
NT_DIMS = (((1,), (1,)), ((), ()))


def _cmul(ar, ai, br, bi):
    return ar * br - ai * bi, ar * bi + ai * br


def _rmsnorm_kernel(x_ref, gain_ref, o_ref):
    x = x_ref[...]
    r = lax.rsqrt(jnp.mean(x * x, axis=-1, keepdims=True) + EPS)
    o_ref[...] = (x * r * gain_ref[...]).astype(BF16)


def _rmsnorm(x, gain, tm):
    m = x.shape[0]
    return pl.pallas_call(
        _rmsnorm_kernel,
        grid=(m // tm,),
        in_specs=[pl.BlockSpec((tm, D_MODEL), lambda i: (i, 0)),
                  pl.BlockSpec((1, D_MODEL), lambda i: (0, 0))],
        out_specs=pl.BlockSpec((tm, D_MODEL), lambda i: (i, 0)),
        out_shape=jax.ShapeDtypeStruct((m, D_MODEL), BF16),
        compiler_params=pltpu.CompilerParams(
            dimension_semantics=("arbitrary",), vmem_limit_bytes=VMEM_LIMIT),
        name="rmsnorm",
    )(x, gain)


def _inproj_epilogue(j, o_ref, qg_ref, kg_ref, e_ref, cos_ref, sa_ref, sb_ref):
    def head_norm_rope(n_chunks, gain_ref, scale):
        cos, sa, sb = cos_ref[...], sa_ref[...], sb_ref[...]
        for c in range(n_chunks):
            cs = slice(c * LANES, (c + 1) * LANES)
            ac = o_ref[:, cs]
            sq = ac * ac
            hi = sq.astype(BF16)
            lo = (sq - hi.astype(F32)).astype(BF16)
            ms = jnp.dot(jnp.concatenate([hi, lo], axis=1), e_ref[...],
                         preferred_element_type=F32)
            xc = ac * lax.rsqrt(ms + EPS) * gain_ref[:, cs]
            o_ref[:, cs] = (xc * cos + pltpu.roll(xc, LANES - HEAD_DIM // 2, 1) * sa
                            + pltpu.roll(xc, HEAD_DIM // 2, 1) * sb) * scale

    @pl.when(j < COL_ZA // TN)
    def _():
        head_norm_rope(TN // LANES, qg_ref, HEAD_DIM ** -0.5)

    @pl.when(j == KV_TILE)
    def _():
        head_norm_rope(KV_WIDTH // LANES, kg_ref, 1.0)


def _inproj_kernel(x_ref, gain_ref, xs_ref, w_ref, qg_ref, kg_ref, e_ref, cos_ref, sa_ref, sb_ref,
                   cos_s_ref, sa_s_ref, sb_s_ref, *rest, mw_slabs):
    n_mw = (len(rest) - 4) // 2
    mw_refs, (o_ref, os_ref) = rest[:n_mw], rest[n_mw:n_mw + 2]
    mw_out_refs, (xn_ref, wbf_ref) = rest[n_mw + 2:2 * n_mw + 2], rest[2 * n_mw + 2:]
    h, j, i = pl.program_id(0), pl.program_id(1), pl.program_id(2)
    tm = o_ref.shape[0]
    rows = pl.ds(pl.multiple_of(i * tm, tm), tm)

    @pl.when(j == 0)
    def _():
        x = x_ref[...]
        r = lax.rsqrt(jnp.mean(x * x, axis=-1, keepdims=True) + EPS)
        xn_ref[rows, :] = (x * r * gain_ref[...]).astype(BF16)

    @pl.when(i == 0)
    def _():
        wbf_ref[...] = w_ref[...].astype(BF16)

    o_ref[...] = jnp.dot(xn_ref[rows, :], wbf_ref[...], preferred_element_type=F32)
    _inproj_epilogue(j, o_ref, qg_ref, kg_ref, e_ref, cos_ref, sa_ref, sb_ref)

    step = (h * pl.num_programs(1) + j) * pl.num_programs(2) + i
    for src, dst, n_slabs in zip(mw_refs, mw_out_refs, mw_slabs):
        @pl.when(step < n_slabs)
        def _():
            dst[...] = src[...].astype(BF16)

    @pl.when(jnp.logical_and(h == 0, jnp.logical_and(j == 0, i == 0)))
    def _():
        os_ref[...] = jnp.zeros(os_ref.shape, F32)

    @pl.when(jnp.logical_and(h == pl.num_programs(0) - 1, i == pl.num_programs(2) - 1))
    def _():
        os_ref[...] = jnp.dot(xs_ref[...], wbf_ref[...], preferred_element_type=F32)
        _inproj_epilogue(j, os_ref, qg_ref, kg_ref, e_ref, cos_s_ref, sa_s_ref, sb_s_ref)


def _w_tile(j):
    n_q = COL_ZA // TN
    return jnp.where(j < n_q, j, jnp.where(j == KV_TILE, n_q, j + 1))


def _inproj(x, gain, xs, w, qg, kg, e, tabs_p, tabs_s, merge_weights, tm, halves):
    m, ms = x.shape[0], xs.shape[0]
    rows_half = m // halves
    tiles = rows_half // tm
    n_tab = tabs_p[0].shape[0] // tm
    n_col = IN_WIDTH // TN
    const = lambda shape: pl.BlockSpec(shape, lambda h, j, i: (0, 0))
    tab_p = pl.BlockSpec((tm, LANES), lambda h, j, i: ((h * tiles + i) % n_tab, 0))
    n_steps = halves * n_col * tiles
    for mw in merge_weights:
        assert mw.shape[0] % CAST_ROWS == 0 and mw.shape[0] // CAST_ROWS <= n_steps
    slab = lambda mw: pl.BlockSpec(
        (CAST_ROWS, mw.shape[1]),
        lambda h, j, i: (jnp.minimum((h * n_col + j) * tiles + i, mw.shape[0] // CAST_ROWS - 1), 0))
    return pl.pallas_call(
        functools.partial(_inproj_kernel,
                          mw_slabs=tuple(mw.shape[0] // CAST_ROWS for mw in merge_weights)),
        grid=(halves, IN_WIDTH // TN, tiles),
        in_specs=[
            pl.BlockSpec((tm, D_MODEL),
                         lambda h, j, i: (h * tiles + jnp.where(j == 0, i, tiles - 1), 0)),
            const((1, D_MODEL)),
            const((ms, D_MODEL)),
            pl.BlockSpec((D_MODEL, TN), lambda h, j, i: (0, _w_tile(j))),
            const((1, TN)), const((1, KV_WIDTH)), const((2 * LANES, LANES)),
            tab_p, tab_p, tab_p,
            const((ms, LANES)), const((ms, LANES)), const((ms, LANES)),
        ] + [slab(mw) for mw in merge_weights],
        out_specs=[pl.BlockSpec((tm, TN), lambda h, j, i: (h * tiles + i, j)),
                   pl.BlockSpec((ms, TN),
                                lambda h, j, i: (0, jnp.where(h == halves - 1, j, n_col)))
                   ] + [slab(mw) for mw in merge_weights],
        out_shape=[jax.ShapeDtypeStruct((m, IN_WIDTH), F32),
                   jax.ShapeDtypeStruct((ms, IN_WIDTH + TN), F32)
                   ] + [jax.ShapeDtypeStruct(mw.shape, BF16) for mw in merge_weights],
        scratch_shapes=[pltpu.VMEM((rows_half, D_MODEL), BF16), pltpu.VMEM((D_MODEL, TN), BF16)],
        compiler_params=pltpu.CompilerParams(
            dimension_semantics=("arbitrary", "arbitrary", "arbitrary"),
            vmem_limit_bytes=VMEM_LIMIT),
        name="inproj",
    )(x, gain, xs, w, qg, kg, e, *tabs_p, *tabs_s, *merge_weights)


def _dup_head(x, g, lo):
    r = pltpu.roll(x, HEAD_DIM, 1)
    return jnp.where(lo, x, r) if g % 2 == 0 else jnp.where(lo, r, x)


def _prompt_attn_kernel(sinks_ref, q_ref, kvc_ref, kvp_ref, kvm_ref, vmt_ref, bias_ref, o_ref):
    lo = lax.broadcasted_iota(jnp.int32, (1, LANES), 1) < HEAD_DIM
    head_lane = lax.broadcasted_iota(jnp.int32, (1, 4 * BLOCK), 1) // BLOCK
    bias = bias_ref[0]
    scores = []
    for g in range(N_KV_HEADS):
        ks = slice((g // 2) * LANES, (g // 2 + 1) * LANES)
        k_all = jnp.concatenate([_dup_head(kvp_ref[:, ks], g, lo), _dup_head(kvc_ref[:, ks], g, lo),
                                 _dup_head(kvm_ref[:, ks], g, lo)], axis=0).astype(BF16)
        qa = q_ref[:, (2 * g) * LANES:(2 * g + 1) * LANES]
        qb = q_ref[:, (2 * g + 1) * LANES:(2 * g + 2) * LANES]
        q4 = jnp.concatenate([jnp.where(lo, qa, 0.0), jnp.where(lo, 0.0, qa),
                              jnp.where(lo, qb, 0.0), jnp.where(lo, 0.0, qb)],
                             axis=0).astype(BF16)
        scores.append(lax.dot_general(k_all, q4, NT_DIMS, preferred_element_type=F32))
    v_t = jnp.concatenate([kvp_ref[:, KV_WIDTH:].T, kvc_ref[:, KV_WIDTH:].T], axis=1)
    for g in range(N_KV_HEADS):
        s = scores[g]
        s_w = s[:2 * BLOCK] + bias
        s_m = s[2 * BLOCK:]
        sink = jnp.where(head_lane == 0, sinks_ref[4 * g],
                         jnp.where(head_lane == 1, sinks_ref[4 * g + 1],
                                   jnp.where(head_lane == 2, sinks_ref[4 * g + 2],
                                             sinks_ref[4 * g + 3])))
        m = jnp.maximum(jnp.maximum(jnp.max(s_w, axis=0, keepdims=True),
                                    jnp.max(s_m, axis=0, keepdims=True)), sink)
        p_w = jnp.exp(s_w - m)
        p_m = jnp.exp(s_m - m)
        den = (jnp.sum(p_w, axis=0, keepdims=True) + jnp.sum(p_m, axis=0, keepdims=True)
               + jnp.exp(sink - m))
        inv = 1.0 / den
        p = jnp.concatenate([p_w * inv, p_m * inv], axis=0).astype(BF16)
        hs = slice(g * HEAD_DIM, (g + 1) * HEAD_DIM)
        vt_g = jnp.concatenate([v_t[hs], vmt_ref[hs, :]], axis=1).astype(BF16)
        o_t = jnp.dot(vt_g, p, preferred_element_type=F32)
        for pair in range(2):
            two = jnp.concatenate([o_t[:, (2 * pair) * BLOCK:(2 * pair + 1) * BLOCK],
                                   o_t[:, (2 * pair + 1) * BLOCK:(2 * pair + 2) * BLOCK]], axis=0)
            o_ref[:, (2 * g + pair) * LANES:(2 * g + pair + 1) * LANES] = two.T


def _prompt_attention(sinks, h_p, h_s, vm_t, batch, seq):
    nb = seq // BLOCK
    kvb = COL_KV // (2 * KV_WIDTH)
    kj = np.arange(2 * BLOCK)[:, None]
    qi = np.arange(4 * BLOCK)[None, :] % BLOCK
    cur = (kj >= BLOCK) & (kj - BLOCK <= qi)
    prev = (kj < BLOCK) & (kj > qi)
    bias = jnp.asarray(np.where(np.stack([cur, cur | prev]), 0.0, NEG), F32)
    return pl.pallas_call(
        _prompt_attn_kernel,
        grid=(batch, nb),
        in_specs=[
            pl.BlockSpec(memory_space=pltpu.SMEM),
            pl.BlockSpec((BLOCK, ATTN_WIDTH), lambda b, i: (b * nb + i, 0)),
            pl.BlockSpec((BLOCK, 2 * KV_WIDTH), lambda b, i: (b * nb + i, kvb)),
            pl.BlockSpec((BLOCK, 2 * KV_WIDTH),
                         lambda b, i: (b * nb + jnp.maximum(i - 1, 0), kvb)),
            pl.BlockSpec((N_META, 2 * KV_WIDTH), lambda b, i: (h_s.shape[0] // N_META - 1, kvb)),
            pl.BlockSpec((KV_WIDTH, N_META), lambda b, i: (0, 0)),
            pl.BlockSpec((1, 2 * BLOCK, 4 * BLOCK), lambda b, i: (jnp.minimum(i, 1), 0, 0)),
        ],
        out_specs=pl.BlockSpec((BLOCK, ATTN_WIDTH), lambda b, i: (b * nb + i, 0)),
        out_shape=jax.ShapeDtypeStruct((batch * seq, ATTN_WIDTH), F32),
        compiler_params=pltpu.CompilerParams(
            dimension_semantics=("arbitrary", "arbitrary"), vmem_limit_bytes=VMEM_LIMIT),
        name="prompt_attn",
    )(sinks, h_p, h_p, h_p, h_s, vm_t, bias)


def _sample_attn_kernel(qp_ref, kn_ref, vn_ref, knr_ref, vnr_ref, ck_ref, cv_ref, mk_ref, mv_ref,
                        sink_ref, o_ref, ok_ref, ov_ref):
    bt = ck_ref.shape[0]
    slot = lax.broadcasted_iota(jnp.int32, (1, 1, WINDOW), 2)
    round_bf16 = lambda a: a.astype(BF16).astype(F32)
    qp = qp_ref[...].astype(BF16)
    ck = ck_ref[...]
    cv = cv_ref[...]
    k_m = mk_ref[...].astype(BF16)
    v_m = mv_ref[...].astype(BF16)
    s_w = jnp.einsum("bhc,bcw->bhw", qp, ck.astype(BF16), preferred_element_type=F32)
    s_w = jnp.where(slot == 0, NEG, s_w)
    s_m = jnp.einsum("bhc,bkc->bhk", qp, k_m, preferred_element_type=F32)
    s_n = jnp.sum(qp.astype(F32) * round_bf16(knr_ref[...]), axis=-1, keepdims=True)
    sink = sink_ref[...][None]
    m = jnp.maximum(jnp.maximum(jnp.max(s_w, axis=-1, keepdims=True),
                                jnp.max(s_m, axis=-1, keepdims=True)),
                    jnp.maximum(s_n, sink))
    p_w = jnp.exp(s_w - m)
    p_m = jnp.exp(s_m - m)
    p_n = jnp.exp(s_n - m)
    den = (jnp.sum(p_w, axis=-1, keepdims=True) + jnp.sum(p_m, axis=-1, keepdims=True)
           + p_n + jnp.exp(sink - m))
    inv = 1.0 / den
    o = (jnp.einsum("bhw,bcw->bhc", (p_w * inv).astype(BF16), cv.astype(BF16),
                    preferred_element_type=F32)
         + jnp.einsum("bhk,bkc->bhc", (p_m * inv).astype(BF16), v_m,
                      preferred_element_type=F32)
         + round_bf16(p_n * inv) * round_bf16(vnr_ref[...]))

    last = lax.broadcasted_iota(jnp.int32, (1, WINDOW), 1) == WINDOW - 1
    kn = kn_ref[0]
    vn = vn_ref[0]
    ck = pltpu.roll(ck, WINDOW - 1, 2)
    cv = pltpu.roll(cv, WINDOW - 1, 2)
    ok_ref[...] = jnp.stack([jnp.where(last, kn[:, b:b + 1], ck[b]) for b in range(bt)])
    ov_ref[...] = jnp.stack([jnp.where(last, vn[:, b:b + 1], cv[b]) for b in range(bt)])
    hh = lax.broadcasted_iota(jnp.int32, (1, N_Q_HEADS, KV_WIDTH), 1) // (N_Q_HEADS // N_KV_HEADS)
    cc = lax.broadcasted_iota(jnp.int32, (1, N_Q_HEADS, KV_WIDTH), 2) // HEAD_DIM
    o = jnp.where(hh == cc, o, 0.0)
    o = o[:, :, :LANES] + o[:, :, LANES:]
    o_ref[...] = o + pltpu.roll(o, HEAD_DIM, 2)


def _sample_attention(qp, kn_t, vn_t, h_s3, cache_k, cache_v, meta_k, meta_v, sink_col, bt):
    n = cache_k.shape[0]
    kvb = COL_KV // KV_WIDTH
    win_spec = pl.BlockSpec((bt, KV_WIDTH, WINDOW), lambda t: (t, 0, 0))
    new_spec = pl.BlockSpec((1, KV_WIDTH, bt), lambda t: (t, 0, 0))
    meta_spec = pl.BlockSpec((bt, N_META, KV_WIDTH), lambda t: (t, 0, 0))
    return pl.pallas_call(
        _sample_attn_kernel,
        grid=(n // bt,),
        in_specs=[
            pl.BlockSpec((bt, N_Q_HEADS, KV_WIDTH), lambda t: (t, 0, 0)),
            new_spec, new_spec,
            pl.BlockSpec((bt, 1, KV_WIDTH), lambda t: (t, 0, kvb)),
            pl.BlockSpec((bt, 1, KV_WIDTH), lambda t: (t, 0, kvb + 1)),
            win_spec, win_spec, meta_spec, meta_spec,
            pl.BlockSpec((N_Q_HEADS, 1), lambda t: (0, 0)),
        ],
        out_specs=[
            pl.BlockSpec((bt, N_Q_HEADS, LANES), lambda t: (t, 0, 0)),
            win_spec, win_spec,
        ],
        out_shape=[
            jax.ShapeDtypeStruct((n, N_Q_HEADS, LANES), F32),
            jax.ShapeDtypeStruct(cache_k.shape, F32),
            jax.ShapeDtypeStruct(cache_v.shape, F32),
        ],
        compiler_params=pltpu.CompilerParams(
            dimension_semantics=("arbitrary",), vmem_limit_bytes=VMEM_LIMIT),
        name="sample_attn",
    )(qp, kn_t, vn_t, h_s3, h_s3, cache_k, cache_v, meta_k, meta_v, sink_col)


def _gelu_tanh(x):
    return 0.5 * x * (1.0 + jnp.tanh(0.7978845608028654 * (x + 0.044715 * (x * x * x))))


def _ssm_kernel(u_ref, um_ref, us_ref, hr_ref, hi_ref, t8_ref, win_ref, wst_ref, wc_ref,
                ar_ref, ai_ref, lr_ref, li_ref, pr_ref, pi_ref, d_ref,
                y_ref, sre_ref, sim_ref, ys_ref, nr_ref, ni_ref, xr_ref, xi_ref):
    dsk = d_ref[...]
    us = us_ref[...]
    bu = jnp.dot(us.astype(BF16), win_ref[0, (CHUNK - 1) * LANES:, :],
                 preferred_element_type=F32)
    l_r, l_i = lr_ref[0], li_ref[0]
    h_r, h_i = hr_ref[...], hi_ref[...]
    n_r = l_r * h_r - l_i * h_i + bu[:, :STATE_TILE]
    n_i = l_r * h_i + l_i * h_r + bu[:, STATE_TILE:]
    nr_ref[...] = n_r
    ni_ref[...] = n_i
    hn = jnp.concatenate([n_r, n_i], axis=1).astype(BF16)
    ys_ref[...] = _gelu_tanh(
        lax.dot_general(hn, wc_ref[0], NT_DIMS, preferred_element_type=F32) + dsk * us)

    batch = sre_ref.shape[0]
    nc = u_ref.shape[0] // CHUNK // batch
    win = win_ref[0]
    dsk_row = jnp.concatenate([dsk] * CHUNK, axis=1)

    def chunks_of(b):
        r0 = b * nc * CHUNK
        return jnp.concatenate(
            [u_ref[pl.ds(r0 + j, nc, stride=CHUNK), :] for j in range(CHUNK)], axis=1)

    n_mc = um_ref.shape[0] // CHUNK
    um = jnp.concatenate([um_ref[pl.ds(j, n_mc, stride=CHUNK), :] for j in range(CHUNK)], axis=1)
    gm = jnp.dot(um.astype(BF16), win, preferred_element_type=F32)
    a_r, a_i = ar_ref[0, 0:1, :], ai_ref[0, 0:1, :]
    s0r, s0i = gm[0:1, :STATE_TILE], gm[0:1, STATE_TILE:]
    for c in range(1, n_mc):
        s0r, s0i = (a_r * s0r - a_i * s0i + gm[c:c + 1, :STATE_TILE],
                    a_r * s0i + a_i * s0r + gm[c:c + 1, STATE_TILE:])

    n_t = nc // SUBLANES
    row = lax.broadcasted_iota(jnp.int32, (nc, 1), 0)
    first = row == 0
    in_tile = row % SUBLANES
    t_row = lax.broadcasted_iota(jnp.int32, (n_t, 1), 0)
    first_t = t_row == 0
    n_local = SUBLANES.bit_length() - 1
    w_r, w_i = pr_ref[0], pi_ref[0]

    def tile_totals(scr_ref, x):
        parts = []
        for c in range(STATE_TILE // LANES):
            scr_ref[c] = x[:, c * LANES:(c + 1) * LANES]
            parts.append(scr_ref[c, pl.ds(SUBLANES - 1, n_t, stride=SUBLANES), :])
        return jnp.concatenate(parts, axis=1)

    def shifted(x, d, keep):
        if d < SUBLANES:
            return jnp.where(keep, pltpu.roll(x, d, 0), 0.0)
        return jnp.concatenate([jnp.zeros((d, STATE_TILE), F32), x[:x.shape[0] - d]], axis=0)

    ucat = chunks_of(0)
    ub = ucat.astype(BF16)
    g = jnp.dot(ub, win, preferred_element_type=F32)
    for b in range(batch):
        y_intra = jnp.dot(ub, t8_ref[0], preferred_element_type=F32) + dsk_row * ucat
        if b + 1 < batch:
            ucat_next = chunks_of(b + 1)
            ub_next = ucat_next.astype(BF16)
            g_next = jnp.dot(ub_next, win, preferred_element_type=F32)
        gr, gi = g[:, :STATE_TILE], g[:, STATE_TILE:]
        for k in range(n_local):
            d = 1 << k
            k_r, k_i = ar_ref[0, k:k + 1, :], ai_ref[0, k:k + 1, :]
            sh_r, sh_i = shifted(gr, d, in_tile >= d), shifted(gi, d, in_tile >= d)
            gr, gi = gr + (k_r * sh_r - k_i * sh_i), gi + (k_r * sh_i + k_i * sh_r)
        tot_r, tot_i = tile_totals(xr_ref, gr), tile_totals(xi_ref, gi)
        k_r, k_i = ar_ref[0, n_local:n_local + 1, :], ai_ref[0, n_local:n_local + 1, :]
        tot_r = tot_r + jnp.where(first_t, k_r * s0r - k_i * s0i, 0.0)
        tot_i = tot_i + jnp.where(first_t, k_r * s0i + k_i * s0r, 0.0)
        for k in range(SCAN_LEVELS - n_local):
            d = 1 << k
            if d >= n_t:
                break
            k_r = ar_ref[0, n_local + k:n_local + k + 1, :]
            k_i = ai_ref[0, n_local + k:n_local + k + 1, :]
            sh_r, sh_i = shifted(tot_r, d, t_row >= d), shifted(tot_i, d, t_row >= d)
            tot_r, tot_i = tot_r + (k_r * sh_r - k_i * sh_i), tot_i + (k_r * sh_i + k_i * sh_r)
        sre_ref[b, 0] = tot_r[n_t - 1:n_t, :]
        sim_ref[b, 0] = tot_i[n_t - 1:n_t, :]
        c_r = jnp.where(first_t, s0r, pltpu.roll(tot_r, 1, 0))
        c_i = jnp.where(first_t, s0i, pltpu.roll(tot_i, 1, 0))
        full_r, full_i = [], []
        for t in range(n_t):
            rs = slice(t * SUBLANES, (t + 1) * SUBLANES)
            e_r, e_i = c_r[t:t + 1, :], c_i[t:t + 1, :]
            full_r.append(gr[rs] + (w_r * e_r - w_i * e_i))
            full_i.append(gi[rs] + (w_r * e_i + w_i * e_r))
        gr, gi = jnp.concatenate(full_r, axis=0), jnp.concatenate(full_i, axis=0)
        sp = jnp.concatenate([jnp.where(first, s0r, pltpu.roll(gr, 1, 0)),
                              jnp.where(first, s0i, pltpu.roll(gi, 1, 0))],
                             axis=1).astype(BF16)
        y2 = y_intra + lax.dot_general(sp, wst_ref[0], NT_DIMS, preferred_element_type=F32)
        for j in range(CHUNK):
            sl = slice(j * LANES, (j + 1) * LANES)
            y_ref[pl.ds(b * nc * CHUNK + j, nc, stride=CHUNK), :] = _gelu_tanh(y2[:, sl])
        if b + 1 < batch:
            ucat, ub, g = ucat_next, ub_next, g_next


def _ssm(h_p, h_s, h0r, h0i, t8, win, wst, wc, a_r, a_i, l_r, l_i, p_r, p_i, d_skip, batch, n_s):
    ub = COL_U // LANES
    rows = h_p.shape[0]
    wspec = pl.BlockSpec((1, CHUNK_WIDTH, CHUNK_WIDTH), lambda gt: (gt, 0, 0))
    aspec = pl.BlockSpec((1, SCAN_LEVELS, STATE_TILE), lambda gt: (gt, 0, 0))
    sspec = pl.BlockSpec((batch, 1, 1, STATE_TILE), lambda gt: (0, gt, 0, 0))
    st = pl.BlockSpec((n_s, STATE_TILE), lambda gt: (0, gt))
    lam = pl.BlockSpec((1, 1, STATE_TILE), lambda gt: (gt, 0, 0))
    return pl.pallas_call(
        _ssm_kernel,
        grid=(N_GROUP_TILES,),
        in_specs=[
            pl.BlockSpec((rows, LANES), lambda gt: (0, ub + gt)),
            pl.BlockSpec((N_META, LANES), lambda gt: (h_s.shape[0] // N_META - 1, ub + gt)),
            pl.BlockSpec((n_s, LANES), lambda gt: (0, ub + gt)),
            st, st,
            wspec, wspec, wspec,
            pl.BlockSpec((1, LANES, 2 * STATE_TILE), lambda gt: (gt, 0, 0)),
            aspec, aspec, lam, lam,
            pl.BlockSpec((1, SUBLANES, STATE_TILE), lambda gt: (gt, 0, 0)),
            pl.BlockSpec((1, SUBLANES, STATE_TILE), lambda gt: (gt, 0, 0)),
            pl.BlockSpec((1, LANES), lambda gt: (0, gt)),
        ],
        out_specs=[pl.BlockSpec((rows, LANES), lambda gt: (0, gt)), sspec, sspec,
                   pl.BlockSpec((n_s, LANES), lambda gt: (0, gt)), st, st],
        out_shape=[
            jax.ShapeDtypeStruct((rows, SSM_WIDTH), F32),
            jax.ShapeDtypeStruct((batch, N_GROUP_TILES, 1, STATE_TILE), F32),
            jax.ShapeDtypeStruct((batch, N_GROUP_TILES, 1, STATE_TILE), F32),
            jax.ShapeDtypeStruct((n_s, SSM_WIDTH), F32),
            jax.ShapeDtypeStruct((n_s, N_SSM_GROUPS * SSM_STATE), F32),
            jax.ShapeDtypeStruct((n_s, N_SSM_GROUPS * SSM_STATE), F32),
        ],
        scratch_shapes=[pltpu.VMEM((STATE_TILE // LANES, rows // CHUNK // batch, LANES), F32)] * 2,
        compiler_params=pltpu.CompilerParams(
            dimension_semantics=("arbitrary",), vmem_limit_bytes=VMEM_LIMIT),
        name="ssm",
    )(h_p, h_s, h_s, h0r, h0i, t8, win, wst, wc, a_r, a_i, l_r, l_i, p_r, p_i, d_skip)


def _merge_kernel(o_ref, za_ref, zs_ref, ga_ref, gs_ref, y_ref, x_ref,
                  o2_ref, za2_ref, zs2_ref, ga2_ref, gs2_ref, y2_ref, x2_ref,
                  bglu_ref, wglu_ref, wa_ref, ws_ref, wo_ref, out_ref, out2_ref):
    def merge(o_r, za_r, zs_r, ga_r, gs_r, y_r, x_r, out_r):
        za = za_r[...]
        a_in = (o_r[...] * (za * jax.nn.sigmoid(za))).astype(BF16)
        gated_a = jax.nn.sigmoid(ga_r[...]) * jnp.dot(a_in, wa_ref[...],
                                                      preferred_element_type=F32)
        y = y_r[...]
        t = jnp.dot(y.astype(BF16), wglu_ref[...], preferred_element_type=F32) + bglu_ref[...]
        y = y * jax.nn.sigmoid(t)
        zs = zs_r[...]
        s_in = (y * (zs * jax.nn.sigmoid(zs))).astype(BF16)
        br_s = jnp.dot(s_in, ws_ref[...], preferred_element_type=F32)
        mix = (gated_a + jax.nn.sigmoid(gs_r[...]) * br_s).astype(BF16)
        out_r[...] = x_r[...] + jnp.dot(mix, wo_ref[...], preferred_element_type=F32)

    merge(o_ref, za_ref, zs_ref, ga_ref, gs_ref, y_ref, x_ref, out_ref)

    @pl.when(pl.program_id(0) == pl.num_programs(0) - 1)
    def _():
        merge(o2_ref, za2_ref, zs2_ref, ga2_ref, gs2_ref, y2_ref, x2_ref, out2_ref)


def _merge(o, h, y, x, o2, h2, y2, x2, b_glu, w_glu, w_a, w_s, w_o, tm):
    m, m2 = x.shape[0], x2.shape[0]
    const = lambda shape: pl.BlockSpec(shape, lambda i: (0, 0), pipeline_mode=pl.Buffered(1))
    rows = lambda width, col: pl.BlockSpec((tm, width), lambda i: (i, col))
    rows2 = lambda width, col: pl.BlockSpec((m2, width), lambda i: (0, col),
                                            pipeline_mode=pl.Buffered(1))
    operands = lambda spec: [
        spec(ATTN_WIDTH, 0), spec(ATTN_WIDTH, COL_ZA // ATTN_WIDTH),
        spec(SSM_WIDTH, COL_ZS // SSM_WIDTH), spec(D_MODEL, COL_GA // D_MODEL),
        spec(D_MODEL, COL_GS // D_MODEL), spec(SSM_WIDTH, 0), spec(D_MODEL, 0)]
    return pl.pallas_call(
        _merge_kernel,
        grid=(m // tm,),
        in_specs=operands(rows) + operands(rows2) + [
            const((1, SSM_WIDTH)),
            const((SSM_WIDTH, SSM_WIDTH)),
            const((ATTN_WIDTH, D_MODEL)),
            const((SSM_WIDTH, D_MODEL)),
            const((D_MODEL, D_MODEL)),
        ],
        out_specs=[pl.BlockSpec((tm, D_MODEL), lambda i: (i, 0)),
                   pl.BlockSpec((m2, D_MODEL), lambda i: (0, 0))],
        out_shape=[jax.ShapeDtypeStruct((m, D_MODEL), F32),
                   jax.ShapeDtypeStruct((m2, D_MODEL), F32)],
        compiler_params=pltpu.CompilerParams(
            dimension_semantics=("arbitrary",), vmem_limit_bytes=VMEM_LIMIT),
        name="merge",
    )(o, h, h, h, h, y, x, o2, h2, h2, h2, h2, y2, x2, b_glu, w_glu, w_a, w_s, w_o)


def _rope_tables(pos):
    half = HEAD_DIM // 2
    inv_freq = (ROPE_THETA ** (-np.arange(half, dtype=np.float32) / half)).astype(np.float32)
    ang = np.asarray(pos, np.float32)[:, None] * inv_freq[None, :]
    cos, sin = np.cos(ang).astype(np.float32), np.sin(ang).astype(np.float32)
    zero = np.zeros_like(sin)
    reps = LANES // HEAD_DIM
    cos_t = np.tile(np.concatenate([cos, cos], axis=1), (1, reps))
    sin_a = np.tile(np.concatenate([-sin, zero], axis=1), (1, reps))
    sin_b = np.tile(np.concatenate([zero, sin], axis=1), (1, reps))
    return jnp.asarray(cos_t), jnp.asarray(sin_a), jnp.asarray(sin_b)


def _split_bf16(x):
    hi = x.astype(BF16)
    return hi, (x - hi.astype(F32)).astype(BF16)


def _dot_nt_f32(a, b):
    ah, al = _split_bf16(a)
    bh, bl = _split_bf16(b)
    dot = lambda x, y: lax.dot_general(x, y, NT_DIMS, preferred_element_type=F32)
    return dot(ah, bh) + dot(ah, bl) + dot(al, bh)


def _ssm_tables_kernel(are_ref, aim_ref, ldt_ref, bre_ref, bim_ref, cre_ref, cim_ref,
                       win_ref, wst_ref, t8_ref, wc_ref, ar_ref, ai_ref, lr_ref, li_ref,
                       pr_ref, pi_ref):
    a_r, a_i = are_ref[0], aim_ref[0]
    dt = jnp.exp(ldt_ref[0])
    mag = jnp.exp(a_r * dt)
    l_r, l_i = mag * jnp.cos(a_i * dt), mag * jnp.sin(a_i * dt)
    den = a_r * a_r + a_i * a_i
    f_r = ((l_r - 1.0) * a_r + l_i * a_i) / den
    f_i = (l_i * a_r - (l_r - 1.0) * a_i) / den
    shape = (LANES, STATE_TILE)
    own = (lax.broadcasted_iota(jnp.int32, shape, 0) // SSM_GROUP
           == lax.broadcasted_iota(jnp.int32, shape, 1) // SSM_STATE)
    sq = (LANES, LANES)
    same_group = (lax.broadcasted_iota(jnp.int32, sq, 0) // SSM_GROUP
                  == lax.broadcasted_iota(jnp.int32, sq, 1) // SSM_GROUP)
    lo = lax.broadcasted_iota(jnp.int32, (1, LANES), 1) < SSM_STATE
    n_rep = STATE_TILE // LANES

    def expand(x):
        return jnp.where(own, jnp.concatenate([x] * n_rep, axis=1), 0.0)

    def compact(v):
        e = jnp.where(own, jnp.broadcast_to(v, shape), 0.0)
        s = e[:, :LANES]
        for k in range(1, n_rep):
            s = s + e[:, k * LANES:(k + 1) * LANES]
        return s + pltpu.roll(s, SSM_STATE, 1)

    lc_r, lc_i = compact(l_r), compact(l_i)
    b_r, b_i = bre_ref[0], bim_ref[0]
    c_r, c_i = cre_ref[0], cim_ref[0]
    bb_r, bb_i = _cmul(compact(f_r), compact(f_i), b_r, b_i)
    wc_ref[0] = jnp.concatenate([expand(c_r), expand(-c_i)], axis=1).astype(BF16)
    c_cat = jnp.where(lo, c_r, -c_i)
    q_r, q_i = jnp.ones(sq, F32), jnp.zeros(sq, F32)
    for n in range(CHUNK):
        x_r, x_i = _cmul(q_r, q_i, bb_r, bb_i)
        win_ref[0, (CHUNK - 1 - n) * LANES:(CHUNK - n) * LANES, :] = jnp.concatenate(
            [expand(x_r), expand(x_i)], axis=1).astype(BF16)
        k_n = _dot_nt_f32(jnp.where(lo, x_r, x_i), c_cat)
        k_n = jnp.where(same_group, k_n, 0.0).astype(BF16)
        for j in range(CHUNK - n):
            t8_ref[0, j * LANES:(j + 1) * LANES, (j + n) * LANES:(j + n + 1) * LANES] = k_n
            if n > 0:
                t8_ref[0, (j + n) * LANES:(j + n + 1) * LANES, j * LANES:(j + 1) * LANES] = (
                    jnp.zeros(sq, BF16))
        q_r, q_i = _cmul(q_r, q_i, lc_r, lc_i)
        s_r, s_i = _cmul(c_r, c_i, q_r, q_i)
        wst_ref[0, n * LANES:(n + 1) * LANES, :] = jnp.concatenate(
            [expand(s_r), expand(-s_i)], axis=1).astype(BF16)
    p_r, p_i = l_r, l_i
    for _ in range(CHUNK.bit_length() - 1):
        p_r, p_i = _cmul(p_r, p_i, p_r, p_i)
    t_r, t_i = p_r, p_i
    for i in range(SUBLANES):
        pr_ref[0, i:i + 1, :] = t_r
        pi_ref[0, i:i + 1, :] = t_i
        t_r, t_i = _cmul(t_r, t_i, p_r, p_i)
    for k in range(SCAN_LEVELS):
        ar_ref[0, k:k + 1, :] = p_r
        ai_ref[0, k:k + 1, :] = p_i
        p_r, p_i = _cmul(p_r, p_i, p_r, p_i)
    lr_ref[0] = l_r
    li_ref[0] = l_i


def _ssm_tables(a_re, a_im, log_dt, b_re, b_im, c_re, c_im):
    gt = N_GROUP_TILES
    lane_vec = lambda v: v.reshape(gt, 1, STATE_TILE)
    dup = lambda v: jnp.concatenate([v, v], axis=-1)
    rows = GROUPS_PER_TILE * SSM_GROUP
    b_t = lambda v: dup(v.transpose(0, 2, 1).reshape(gt, rows, SSM_STATE))
    c_t = lambda v: dup(v.reshape(gt, rows, SSM_STATE))
    vec = pl.BlockSpec((1, 1, STATE_TILE), lambda g: (g, 0, 0))
    par = pl.BlockSpec((1, rows, LANES), lambda g: (g, 0, 0))
    big = pl.BlockSpec((1, CHUNK_WIDTH, CHUNK_WIDTH), lambda g: (g, 0, 0))
    wcs = pl.BlockSpec((1, LANES, 2 * STATE_TILE), lambda g: (g, 0, 0))
    scan = pl.BlockSpec((1, SCAN_LEVELS, STATE_TILE), lambda g: (g, 0, 0))
    tile = pl.BlockSpec((1, SUBLANES, STATE_TILE), lambda g: (g, 0, 0))
    return pl.pallas_call(
        _ssm_tables_kernel,
        grid=(gt,),
        in_specs=[vec, vec, vec, par, par, par, par],
        out_specs=[big, big, big, wcs, scan, scan, vec, vec, tile, tile],
        out_shape=[
            jax.ShapeDtypeStruct((gt, CHUNK_WIDTH, 2 * STATE_TILE), BF16),
            jax.ShapeDtypeStruct((gt, CHUNK_WIDTH, 2 * STATE_TILE), BF16),
            jax.ShapeDtypeStruct((gt, CHUNK_WIDTH, CHUNK_WIDTH), BF16),
            jax.ShapeDtypeStruct((gt, LANES, 2 * STATE_TILE), BF16),
            jax.ShapeDtypeStruct((gt, SCAN_LEVELS, STATE_TILE), F32),
            jax.ShapeDtypeStruct((gt, SCAN_LEVELS, STATE_TILE), F32),
            jax.ShapeDtypeStruct((gt, 1, STATE_TILE), F32),
            jax.ShapeDtypeStruct((gt, 1, STATE_TILE), F32),
            jax.ShapeDtypeStruct((gt, SUBLANES, STATE_TILE), F32),
            jax.ShapeDtypeStruct((gt, SUBLANES, STATE_TILE), F32),
        ],
        compiler_params=pltpu.CompilerParams(
            dimension_semantics=("arbitrary",), vmem_limit_bytes=VMEM_LIMIT),
        name="ssm_tables",
    )(lane_vec(a_re), lane_vec(a_im), lane_vec(jnp.repeat(log_dt, SSM_STATE)),
      b_t(b_re), b_t(b_im), c_t(c_re), c_t(c_im))


def kernel(x_prompt, x_sample, cache_win_k, cache_win_v, cache_meta_k, cache_meta_v,
           state_ssm_re, state_ssm_im, meta_tokens, norm_gain, w_in, q_norm_gain, k_norm_gain,
           sinks, a_re, a_im, log_dt, b_re, b_im, c_re, c_im, d_skip, w_glu, b_glu,
           w_attn_out, w_ssm_out, w_out):
    depth = w_in.shape[0]
    assert depth == 1, "single-layer trunk"
    batch, seq = x_prompt.shape[:2]
    n_s = x_sample.shape[0]
    assert x_sample.shape[1] == 1 and seq % (CHUNK * BLOCK) == 0
    l = 0

    w = w_in[l]
    qg = jnp.tile(q_norm_gain[l], TN // HEAD_DIM)[None]
    kg = jnp.tile(k_norm_gain[l], KV_WIDTH // HEAD_DIM)[None]
    lane = np.arange(LANES)
    e128 = np.where((lane[:, None] // HEAD_DIM) == (lane[None, :] // HEAD_DIM),
                    1.0 / HEAD_DIM, 0.0).astype(np.float32)
    e = jnp.asarray(np.concatenate([e128, e128], axis=0), BF16)
    gain = norm_gain[l][None]

    pos_p = N_META + np.arange(seq)
    pos_small = np.concatenate([np.full((n_s,), PAST_LEN), np.arange(N_META)])
    x_small = jnp.concatenate([x_sample[:, 0, :], meta_tokens.astype(x_prompt.dtype)], axis=0)
    xp = x_prompt.reshape(batch * seq, D_MODEL)
    xn_s = _rmsnorm(x_small, gain, tm=n_s + N_META)
    h_p, h_s, w_glu_bf, w_a_bf, w_s_bf, w_o_bf = _inproj(
        xp, gain, xn_s, w, qg, kg, e, _rope_tables(pos_p), _rope_tables(pos_small),
        (w_glu[l], w_attn_out[l], w_ssm_out[l], w_out[l]), tm=1024, halves=2)

    win, wst, t8, wc, a_r, a_i, l_r, l_i, p_r, p_i = _ssm_tables(
        a_re[l], a_im[l], log_dt[l], b_re[l], b_im[l], c_re[l], c_im[l])
    dsk = d_skip[l][None]

    vm_t = h_s[n_s:, COL_KV + KV_WIDTH:IN_WIDTH].T
    o_p = _prompt_attention(sinks[l], h_p, h_s, vm_t, batch, seq)
    q_s = h_s[:n_s, COL_Q:COL_Q + ATTN_WIDTH].reshape(n_s, N_Q_HEADS, 1, HEAD_DIM)
    sel = (np.arange(N_Q_HEADS)[:, None] // (N_Q_HEADS // N_KV_HEADS)
           == np.arange(N_KV_HEADS)[None, :]).astype(np.float32)
    qp = (q_s * sel[None, :, :, None]).reshape(n_s, N_Q_HEADS, KV_WIDTH)
    bt = 8
    to_t = lambda c: c.transpose(0, 2, 3, 1).reshape(n_s, KV_WIDTH, WINDOW)
    from_t = lambda c: c.reshape(n_s, N_KV_HEADS, HEAD_DIM, WINDOW).transpose(0, 3, 1, 2)[None]
    new_t = lambda c0: h_s[:n_s, c0:c0 + KV_WIDTH].reshape(n_s // bt, bt, KV_WIDTH).transpose(0, 2, 1)
    o_s, s_win_k, s_win_v = _sample_attention(
        qp, new_t(COL_KV), new_t(COL_KV + KV_WIDTH), h_s.reshape(h_s.shape[0], 1, h_s.shape[1]),
        to_t(cache_win_k[l]), to_t(cache_win_v[l]),
        cache_meta_k[l].reshape(n_s, N_META, KV_WIDTH), cache_meta_v[l].reshape(n_s, N_META, KV_WIDTH),
        sinks[l][:, None], bt=bt)
    o_s = o_s[:, :, :HEAD_DIM].reshape(n_s, ATTN_WIDTH)

    y_p, p_re, p_im, y_s, s_re, s_im = _ssm(
        h_p, h_s, state_ssm_re[l].reshape(n_s, -1), state_ssm_im[l].reshape(n_s, -1),
        t8, win, wst, wc, a_r, a_i, l_r, l_i, p_r, p_i, dsk, batch, n_s)
    y_prompt, y_sample = _merge(
        o_p, h_p, y_p, xp, o_s, h_s, y_s, x_sample[:, 0, :], b_glu[l][None],
        w_glu_bf, w_a_bf, w_s_bf, w_o_bf, tm=256)
    y_prompt = y_prompt.reshape(batch, seq, D_MODEL)
    y_sample = y_sample.reshape(n_s, 1, D_MODEL)

    kv_p = h_p.reshape(batch, seq, IN_WIDTH)[:, seq - WINDOW:, COL_KV:]
    kv_p = kv_p.reshape(batch, WINDOW, 2, N_KV_HEADS, HEAD_DIM)
    p_win_k = kv_p[:, :, 0][None]
    p_win_v = kv_p[:, :, 1][None]
    kv_m = h_s[n_s:, COL_KV:IN_WIDTH].reshape(N_META, 2, N_KV_HEADS, HEAD_DIM)
    p_meta_k = jnp.broadcast_to(kv_m[None, :, 0], (batch, N_META, N_KV_HEADS, HEAD_DIM))[None]
    p_meta_v = jnp.broadcast_to(kv_m[None, :, 1], (batch, N_META, N_KV_HEADS, HEAD_DIM))[None]
    p_ssm_re = p_re.reshape(batch, N_SSM_GROUPS, SSM_STATE)[None]
    p_ssm_im = p_im.reshape(batch, N_SSM_GROUPS, SSM_STATE)[None]
    st_shape = (1, n_s, N_SSM_GROUPS, SSM_STATE)
    return (y_prompt, y_sample, p_win_k, p_win_v, p_meta_k, p_meta_v, p_ssm_re, p_ssm_im,
            from_t(s_win_k), from_t(s_win_v), s_re.reshape(st_shape), s_im.reshape(st_shape))
```

```python
import functools

import jax
import jax.numpy as jnp
import numpy as np
from jax import lax
from jax.experimental import pallas as pl
from jax.experimental.pallas import tpu as pltpu

F32 = jnp.float32
BF16 = jnp.bfloat16

D_MODEL = 2048
N_META = 16
HEAD_DIM = 64
N_Q_HEADS = 16
N_KV_HEADS = 4
ATTN_WIDTH = N_Q_HEADS * HEAD_DIM
KV_WIDTH = N_KV_HEADS * HEAD_DIM
WINDOW = 128
BLOCK = 128
ROPE_THETA = 10000.0
SSM_WIDTH = D_MODEL // 2
SSM_GROUP = 16
N_SSM_GROUPS = SSM_WIDTH // SSM_GROUP
SSM_STATE = 64
EPS = 1e-6
PAST_LEN = 8192

LANES = 128
SUBLANES = 8
CHUNK = 8
GROUPS_PER_TILE = LANES // SSM_GROUP
N_GROUP_TILES = N_SSM_GROUPS // GROUPS_PER_TILE
CHUNK_WIDTH = CHUNK * LANES
STATE_TILE = GROUPS_PER_TILE * SSM_STATE
SCAN_LEVELS = 8

COL_Q = 0
COL_ZA = 1024
COL_U = 2048
COL_ZS = 3072
COL_GA = 4096
COL_GS = 6144
COL_KV = 8192
IN_WIDTH = 8704
TN = 512
CAST_ROWS = 16
KV_TILE = COL_KV // TN
VMEM_LIMIT = 56 * 1024 * 1024
NEG = -1e30
NT_DIMS = (((1,), (1,)), ((), ()))


def _cmul(ar, ai, br, bi):
    return ar * br - ai * bi, ar * bi + ai * br


def _rmsnorm_kernel(x_ref, gain_ref, o_ref):
    x = x_ref[...]
    r = lax.rsqrt(jnp.mean(x * x, axis=-1, keepdims=True) + EPS)
    o_ref[...] = (x * r * gain_ref[...]).astype(BF16)


def _rmsnorm(x, gain, tm):
    m = x.shape[0]
    return pl.pallas_call(
        _rmsnorm_kernel,
        grid=(m // tm,),
        in_specs=[pl.BlockSpec((tm, D_MODEL), lambda i: (i, 0)),
                  pl.BlockSpec((1, D_MODEL), lambda i: (0, 0))],
        out_specs=pl.BlockSpec((tm, D_MODEL), lambda i: (i, 0)),
        out_shape=jax.ShapeDtypeStruct((m, D_MODEL), BF16),
        compiler_params=pltpu.CompilerParams(
            dimension_semantics=("arbitrary",), vmem_limit_bytes=VMEM_LIMIT),
        name="rmsnorm",
    )(x, gain)


def _inproj_epilogue(j, o_ref, qg_ref, kg_ref, e_ref, cos_ref, sa_ref, sb_ref):
    def head_norm_rope(n_chunks, gain_ref, scale):
        cos, sa, sb = cos_ref[...], sa_ref[...], sb_ref[...]
        for c in range(n_chunks):
            cs = slice(c * LANES, (c + 1) * LANES)
            ac = o_ref[:, cs]
            sq = ac * ac
            hi = sq.astype(BF16)
            lo = (sq - hi.astype(F32)).astype(BF16)
            ms = jnp.dot(jnp.concatenate([hi, lo], axis=1), e_ref[...],
                         preferred_element_type=F32)
            xc = ac * lax.rsqrt(ms + EPS) * gain_ref[:, cs]
            o_ref[:, cs] = (xc * cos + pltpu.roll(xc, LANES - HEAD_DIM // 2, 1) * sa
                            + pltpu.roll(xc, HEAD_DIM // 2, 1) * sb) * scale

    @pl.when(j < COL_ZA // TN)
    def _():
        head_norm_rope(TN // LANES, qg_ref, HEAD_DIM ** -0.5)

    @pl.when(j == KV_TILE)
    def _():
        head_norm_rope(KV_WIDTH // LANES, kg_ref, 1.0)


def _inproj_kernel(x_ref, gain_ref, xs_ref, w_ref, qg_ref, kg_ref, e_ref, cos_ref, sa_ref, sb_ref,
                   cos_s_ref, sa_s_ref, sb_s_ref, *rest, mw_slabs):
    n_mw = (len(rest) - 4) // 2
    mw_refs, (o_ref, os_ref) = rest[:n_mw], rest[n_mw:n_mw + 2]
    mw_out_refs, (xn_ref, wbf_ref) = rest[n_mw + 2:2 * n_mw + 2], rest[2 * n_mw + 2:]
    h, j, i = pl.program_id(0), pl.program_id(1), pl.program_id(2)
    tm = o_ref.shape[0]
    rows = pl.ds(pl.multiple_of(i * tm, tm), tm)

    @pl.when(j == 0)
    def _():
        x = x_ref[...]
        r = lax.rsqrt(jnp.mean(x * x, axis=-1, keepdims=True) + EPS)
        xn_ref[rows, :] = (x * r * gain_ref[...]).astype(BF16)

    @pl.when(i == 0)
    def _():
        wbf_ref[...] = w_ref[...].astype(BF16)

    o_ref[...] = jnp.dot(xn_ref[rows, :], wbf_ref[...], preferred_element_type=F32)
    _inproj_epilogue(j, o_ref, qg_ref, kg_ref, e_ref, cos_ref, sa_ref, sb_ref)

    step = (h * pl.num_programs(1) + j) * pl.num_programs(2) + i
    for src, dst, n_slabs in zip(mw_refs, mw_out_refs, mw_slabs):
        @pl.when(step < n_slabs)
        def _():
            dst[...] = src[...].astype(BF16)

    @pl.when(jnp.logical_and(h == 0, jnp.logical_and(j == 0, i == 0)))
    def _():
        os_ref[...] = jnp.zeros(os_ref.shape, F32)

    @pl.when(jnp.logical_and(h == pl.num_programs(0) - 1, i == pl.num_programs(2) - 1))
    def _():
        os_ref[...] = jnp.dot(xs_ref[...], wbf_ref[...], preferred_element_type=F32)
        _inproj_epilogue(j, os_ref, qg_ref, kg_ref, e_ref, cos_s_ref, sa_s_ref, sb_s_ref)


def _w_tile(j):
    n_q = COL_ZA // TN
    return jnp.where(j < n_q, j, jnp.where(j == KV_TILE, n_q, j + 1))


def _inproj(x, gain, xs, w, qg, kg, e, tabs_p, tabs_s, merge_weights, tm, halves):
    m, ms = x.shape[0], xs.shape[0]
    rows_half = m // halves
    tiles = rows_half // tm
    n_tab = tabs_p[0].shape[0] // tm
    n_col = IN_WIDTH // TN
    const = lambda shape: pl.BlockSpec(shape, lambda h, j, i: (0, 0))
    tab_p = pl.BlockSpec((tm, LANES), lambda h, j, i: ((h * tiles + i) % n_tab, 0))
    n_steps = halves * n_col * tiles
    for mw in merge_weights:
        assert mw.shape[0] % CAST_ROWS == 0 and mw.shape[0] // CAST_ROWS <= n_steps
    slab = lambda mw: pl.BlockSpec(
        (CAST_ROWS, mw.shape[1]),
        lambda h, j, i: (jnp.minimum((h * n_col + j) * tiles + i, mw.shape[0] // CAST_ROWS - 1), 0))
    return pl.pallas_call(
        functools.partial(_inproj_kernel,
                          mw_slabs=tuple(mw.shape[0] // CAST_ROWS for mw in merge_weights)),
        grid=(halves, IN_WIDTH // TN, tiles),
        in_specs=[
            pl.BlockSpec((tm, D_MODEL),
                         lambda h, j, i: (h * tiles + jnp.where(j == 0, i, tiles - 1), 0)),
            const((1, D_MODEL)),
            const((ms, D_MODEL)),
            pl.BlockSpec((D_MODEL, TN), lambda h, j, i: (0, _w_tile(j))),
            const((1, TN)), const((1, KV_WIDTH)), const((2 * LANES, LANES)),
            tab_p, tab_p, tab_p,
            const((ms, LANES)), const((ms, LANES)), const((ms, LANES)),
        ] + [slab(mw) for mw in merge_weights],
        out_specs=[pl.BlockSpec((tm, TN), lambda h, j, i: (h * tiles + i, j)),
                   pl.BlockSpec((ms, TN),
                                lambda h, j, i: (0, jnp.where(h == halves - 1, j, n_col)))
                   ] + [slab(mw) for mw in merge_weights],
        out_shape=[jax.ShapeDtypeStruct((m, IN_WIDTH), F32),
                   jax.ShapeDtypeStruct((ms, IN_WIDTH + TN), F32)
                   ] + [jax.ShapeDtypeStruct(mw.shape, BF16) for mw in merge_weights],
        scratch_shapes=[pltpu.VMEM((rows_half, D_MODEL), BF16), pltpu.VMEM((D_MODEL, TN), BF16)],
        compiler_params=pltpu.CompilerParams(
            dimension_semantics=("arbitrary", "arbitrary", "arbitrary"),
            vmem_limit_bytes=VMEM_LIMIT),
        name="inproj",
    )(x, gain, xs, w, qg, kg, e, *tabs_p, *tabs_s, *merge_weights)


def _dup_head(x, g, lo):
    r = pltpu.roll(x, HEAD_DIM, 1)
    return jnp.where(lo, x, r) if g % 2 == 0 else jnp.where(lo, r, x)


def _prompt_attn_kernel(sinks_ref, q_ref, kvc_ref, kvp_ref, kvm_ref, vmt_ref, bias_ref, o_ref):
    lo = lax.broadcasted_iota(jnp.int32, (1, LANES), 1) < HEAD_DIM
    head_lane = lax.broadcasted_iota(jnp.int32, (1, 4 * BLOCK), 1) // BLOCK
    bias = bias_ref[0]
    scores = []
    for g in range(N_KV_HEADS):
        ks = slice((g // 2) * LANES, (g // 2 + 1) * LANES)
        k_all = jnp.concatenate([_dup_head(kvp_ref[:, ks], g, lo), _dup_head(kvc_ref[:, ks], g, lo),
                                 _dup_head(kvm_ref[:, ks], g, lo)], axis=0).astype(BF16)
        qa = q_ref[:, (2 * g) * LANES:(2 * g + 1) * LANES]
        qb = q_ref[:, (2 * g + 1) * LANES:(2 * g + 2) * LANES]
        q4 = jnp.concatenate([jnp.where(lo, qa, 0.0), jnp.where(lo, 0.0, qa),
                              jnp.where(lo, qb, 0.0), jnp.where(lo, 0.0, qb)],
                             axis=0).astype(BF16)
        scores.append(lax.dot_general(k_all, q4, NT_DIMS, preferred_element_type=F32))
    v_t = jnp.concatenate([kvp_ref[:, KV_WIDTH:].T, kvc_ref[:, KV_WIDTH:].T], axis=1)
    for g in range(N_KV_HEADS):
        s = scores[g]
        s_w = s[:2 * BLOCK] + bias
        s_m = s[2 * BLOCK:]
        sink = jnp.where(head_lane == 0, sinks_ref[4 * g],
                         jnp.where(head_lane == 1, sinks_ref[4 * g + 1],
                                   jnp.where(head_lane == 2, sinks_ref[4 * g + 2],
                                             sinks_ref[4 * g + 3])))
        m = jnp.maximum(jnp.maximum(jnp.max(s_w, axis=0, keepdims=True),
                                    jnp.max(s_m, axis=0, keepdims=True)), sink)
        p_w = jnp.exp(s_w - m)
        p_m = jnp.exp(s_m - m)
        den = (jnp.sum(p_w, axis=0, keepdims=True) + jnp.sum(p_m, axis=0, keepdims=True)
               + jnp.exp(sink - m))
        inv = 1.0 / den
        p = jnp.concatenate([p_w * inv, p_m * inv], axis=0).astype(BF16)
        hs = slice(g * HEAD_DIM, (g + 1) * HEAD_DIM)
        vt_g = jnp.concatenate([v_t[hs], vmt_ref[hs, :]], axis=1).astype(BF16)
        o_t = jnp.dot(vt_g, p, preferred_element_type=F32)
        for pair in range(2):
            two = jnp.concatenate([o_t[:, (2 * pair) * BLOCK:(2 * pair + 1) * BLOCK],
                                   o_t[:, (2 * pair + 1) * BLOCK:(2 * pair + 2) * BLOCK]], axis=0)
            o_ref[:, (2 * g + pair) * LANES:(2 * g + pair + 1) * LANES] = two.T


def _prompt_attention(sinks, h_p, h_s, vm_t, batch, seq):
    nb = seq // BLOCK
    kvb = COL_KV // (2 * KV_WIDTH)
    kj = np.arange(2 * BLOCK)[:, None]
    qi = np.arange(4 * BLOCK)[None, :] % BLOCK
    cur = (kj >= BLOCK) & (kj - BLOCK <= qi)
    prev = (kj < BLOCK) & (kj > qi)
    bias = jnp.asarray(np.where(np.stack([cur, cur | prev]), 0.0, NEG), F32)
    return pl.pallas_call(
        _prompt_attn_kernel,
        grid=(batch, nb),
        in_specs=[
            pl.BlockSpec(memory_space=pltpu.SMEM),
            pl.BlockSpec((BLOCK, ATTN_WIDTH), lambda b, i: (b * nb + i, 0)),
            pl.BlockSpec((BLOCK, 2 * KV_WIDTH), lambda b, i: (b * nb + i, kvb)),
            pl.BlockSpec((BLOCK, 2 * KV_WIDTH),
                         lambda b, i: (b * nb + jnp.maximum(i - 1, 0), kvb)),
            pl.BlockSpec((N_META, 2 * KV_WIDTH), lambda b, i: (h_s.shape[0] // N_META - 1, kvb)),
            pl.BlockSpec((KV_WIDTH, N_META), lambda b, i: (0, 0)),
            pl.BlockSpec((1, 2 * BLOCK, 4 * BLOCK), lambda b, i: (jnp.minimum(i, 1), 0, 0)),
        ],
        out_specs=pl.BlockSpec((BLOCK, ATTN_WIDTH), lambda b, i: (b * nb + i, 0)),
        out_shape=jax.ShapeDtypeStruct((batch * seq, ATTN_WIDTH), F32),
        compiler_params=pltpu.CompilerParams(
            dimension_semantics=("arbitrary", "arbitrary"), vmem_limit_bytes=VMEM_LIMIT),
        name="prompt_attn",
    )(sinks, h_p, h_p, h_p, h_s, vm_t, bias)


def _sample_attn_kernel(qp_ref, kn_ref, vn_ref, knr_ref, vnr_ref, ck_ref, cv_ref, mk_ref, mv_ref,
                        sink_ref, o_ref, ok_ref, ov_ref):
    bt = ck_ref.shape[0]
    slot = lax.broadcasted_iota(jnp.int32, (1, 1, WINDOW), 2)
    round_bf16 = lambda a: a.astype(BF16).astype(F32)
    qp = qp_ref[...].astype(BF16)
    ck = ck_ref[...]
    cv = cv_ref[...]
    k_m = mk_ref[...].astype(BF16)
    v_m = mv_ref[...].astype(BF16)
    s_w = jnp.einsum("bhc,bcw->bhw", qp, ck.astype(BF16), preferred_element_type=F32)
    s_w = jnp.where(slot == 0, NEG, s_w)
    s_m = jnp.einsum("bhc,bkc->bhk", qp, k_m, preferred_element_type=F32)
    knr, vnr = round_bf16(knr_ref[...]), round_bf16(vnr_ref[...])
    qf = qp.astype(F32)
    s_n = jnp.stack([jnp.sum(qf[b] * knr[b:b + 1, :], axis=-1, keepdims=True)
                     for b in range(bt)])
    sink = sink_ref[...][None]
    m = jnp.maximum(jnp.maximum(jnp.max(s_w, axis=-1, keepdims=True),
                                jnp.max(s_m, axis=-1, keepdims=True)),
                    jnp.maximum(s_n, sink))
    p_w = jnp.exp(s_w - m)
    p_m = jnp.exp(s_m - m)
    p_n = jnp.exp(s_n - m)
    den = (jnp.sum(p_w, axis=-1, keepdims=True) + jnp.sum(p_m, axis=-1, keepdims=True)
           + p_n + jnp.exp(sink - m))
    inv = 1.0 / den
    o = (jnp.einsum("bhw,bcw->bhc", (p_w * inv).astype(BF16), cv.astype(BF16),
                    preferred_element_type=F32)
         + jnp.einsum("bhk,bkc->bhc", (p_m * inv).astype(BF16), v_m,
                      preferred_element_type=F32)
         + jnp.stack([round_bf16(p_n[b] * inv[b]) * vnr[b:b + 1, :] for b in range(bt)]))

    last = lax.broadcasted_iota(jnp.int32, (1, WINDOW), 1) == WINDOW - 1
    kn = kn_ref[0]
    vn = vn_ref[0]
    ck = pltpu.roll(ck, WINDOW - 1, 2)
    cv = pltpu.roll(cv, WINDOW - 1, 2)
    ok_ref[...] = jnp.stack([jnp.where(last, kn[:, b:b + 1], ck[b]) for b in range(bt)])
    ov_ref[...] = jnp.stack([jnp.where(last, vn[:, b:b + 1], cv[b]) for b in range(bt)])
    hh = lax.broadcasted_iota(jnp.int32, (1, N_Q_HEADS, KV_WIDTH), 1) // (N_Q_HEADS // N_KV_HEADS)
    cc = lax.broadcasted_iota(jnp.int32, (1, N_Q_HEADS, KV_WIDTH), 2) // HEAD_DIM
    o = jnp.where(hh == cc, o, 0.0)
    o = o[:, :, :LANES] + o[:, :, LANES:]
    o_ref[...] = o + pltpu.roll(o, HEAD_DIM, 2)


def _sample_attention(qp, kn_t, vn_t, h_s, cache_k, cache_v, meta_k, meta_v, sink_col, bt):
    n = cache_k.shape[0]
    kvb = COL_KV // KV_WIDTH
    win_spec = pl.BlockSpec((bt, KV_WIDTH, WINDOW), lambda t: (t, 0, 0))
    new_spec = pl.BlockSpec((1, KV_WIDTH, bt), lambda t: (t, 0, 0))
    meta_spec = pl.BlockSpec((bt, N_META, KV_WIDTH), lambda t: (t, 0, 0))
    return pl.pallas_call(
        _sample_attn_kernel,
        grid=(n // bt,),
        in_specs=[
            pl.BlockSpec((bt, N_Q_HEADS, KV_WIDTH), lambda t: (t, 0, 0)),
            new_spec, new_spec,
            pl.BlockSpec((bt, KV_WIDTH), lambda t: (t, kvb)),
            pl.BlockSpec((bt, KV_WIDTH), lambda t: (t, kvb + 1)),
            win_spec, win_spec, meta_spec, meta_spec,
            pl.BlockSpec((N_Q_HEADS, 1), lambda t: (0, 0)),
        ],
        out_specs=[
            pl.BlockSpec((bt, N_Q_HEADS, LANES), lambda t: (t, 0, 0)),
            win_spec, win_spec,
        ],
        out_shape=[
            jax.ShapeDtypeStruct((n, N_Q_HEADS, LANES), F32),
            jax.ShapeDtypeStruct(cache_k.shape, F32),
            jax.ShapeDtypeStruct(cache_v.shape, F32),
        ],
        compiler_params=pltpu.CompilerParams(
            dimension_semantics=("arbitrary",), vmem_limit_bytes=VMEM_LIMIT),
        name="sample_attn",
    )(qp, kn_t, vn_t, h_s, h_s, cache_k, cache_v, meta_k, meta_v, sink_col)


def _gelu_tanh(x):
    return 0.5 * x * (1.0 + jnp.tanh(0.7978845608028654 * (x + 0.044715 * (x * x * x))))


def _ssm_kernel(u_ref, um_ref, us_ref, hr_ref, hi_ref, t8_ref, win_ref, wst_ref, wc_ref,
                ar_ref, ai_ref, lr_ref, li_ref, pr_ref, pi_ref, d_ref,
                y_ref, sre_ref, sim_ref, ys_ref, nr_ref, ni_ref, xr_ref, xi_ref):
    dsk = d_ref[...]
    us = us_ref[...]
    bu = jnp.dot(us.astype(BF16), win_ref[0, (CHUNK - 1) * LANES:, :],
                 preferred_element_type=F32)
    l_r, l_i = lr_ref[0], li_ref[0]
    h_r, h_i = hr_ref[...], hi_ref[...]
    n_r = l_r * h_r - l_i * h_i + bu[:, :STATE_TILE]
    n_i = l_r * h_i + l_i * h_r + bu[:, STATE_TILE:]
    nr_ref[...] = n_r
    ni_ref[...] = n_i
    hn = jnp.concatenate([n_r, n_i], axis=1).astype(BF16)
    ys_ref[...] = _gelu_tanh(
        lax.dot_general(hn, wc_ref[0], NT_DIMS, preferred_element_type=F32) + dsk * us)

    batch = sre_ref.shape[0]
    nc = u_ref.shape[0] // CHUNK // batch
    win = win_ref[0]
    dsk_row = jnp.concatenate([dsk] * CHUNK, axis=1)

    def chunks_of(b):
        r0 = b * nc * CHUNK
        return jnp.concatenate(
            [u_ref[pl.ds(r0 + j, nc, stride=CHUNK), :] for j in range(CHUNK)], axis=1)

    n_mc = um_ref.shape[0] // CHUNK
    um = jnp.concatenate([um_ref[pl.ds(j, n_mc, stride=CHUNK), :] for j in range(CHUNK)], axis=1)
    gm = jnp.dot(um.astype(BF16), win, preferred_element_type=F32)
    a_r, a_i = ar_ref[0, 0:1, :], ai_ref[0, 0:1, :]
    s0r, s0i = gm[0:1, :STATE_TILE], gm[0:1, STATE_TILE:]
    for c in range(1, n_mc):
        s0r, s0i = (a_r * s0r - a_i * s0i + gm[c:c + 1, :STATE_TILE],
                    a_r * s0i + a_i * s0r + gm[c:c + 1, STATE_TILE:])

    n_t = nc // SUBLANES
    row = lax.broadcasted_iota(jnp.int32, (nc, 1), 0)
    first = row == 0
    in_tile = row % SUBLANES
    t_row = lax.broadcasted_iota(jnp.int32, (n_t, 1), 0)
    first_t = t_row == 0
    n_local = SUBLANES.bit_length() - 1
    w_r, w_i = pr_ref[0], pi_ref[0]

    def tile_totals(scr_ref, x):
        parts = []
        for c in range(STATE_TILE // LANES):
            scr_ref[c] = x[:, c * LANES:(c + 1) * LANES]
            parts.append(scr_ref[c, pl.ds(SUBLANES - 1, n_t, stride=SUBLANES), :])
        return jnp.concatenate(parts, axis=1)

    def shifted(x, d, keep):
        if d < SUBLANES:
            return jnp.where(keep, pltpu.roll(x, d, 0), 0.0)
        return jnp.concatenate([jnp.zeros((d, STATE_TILE), F32), x[:x.shape[0] - d]], axis=0)

    ucat = chunks_of(0)
    ub = ucat.astype(BF16)
    g = jnp.dot(ub, win, preferred_element_type=F32)
    for b in range(batch):
        y_intra = jnp.dot(ub, t8_ref[0], preferred_element_type=F32) + dsk_row * ucat
        if b + 1 < batch:
            ucat_next = chunks_of(b + 1)
            ub_next = ucat_next.astype(BF16)
            g_next = jnp.dot(ub_next, win, preferred_element_type=F32)
        gr, gi = g[:, :STATE_TILE], g[:, STATE_TILE:]
        for k in range(n_local):
            d = 1 << k
            k_r, k_i = ar_ref[0, k:k + 1, :], ai_ref[0, k:k + 1, :]
            sh_r, sh_i = shifted(gr, d, in_tile >= d), shifted(gi, d, in_tile >= d)
            gr, gi = gr + (k_r * sh_r - k_i * sh_i), gi + (k_r * sh_i + k_i * sh_r)
        tot_r, tot_i = tile_totals(xr_ref, gr), tile_totals(xi_ref, gi)
        k_r, k_i = ar_ref[0, n_local:n_local + 1, :], ai_ref[0, n_local:n_local + 1, :]
        tot_r = tot_r + jnp.where(first_t, k_r * s0r - k_i * s0i, 0.0)
        tot_i = tot_i + jnp.where(first_t, k_r * s0i + k_i * s0r, 0.0)
        for k in range(SCAN_LEVELS - n_local):
            d = 1 << k
            if d >= n_t:
                break
            k_r = ar_ref[0, n_local + k:n_local + k + 1, :]
            k_i = ai_ref[0, n_local + k:n_local + k + 1, :]
            sh_r, sh_i = shifted(tot_r, d, t_row >= d), shifted(tot_i, d, t_row >= d)
            tot_r, tot_i = tot_r + (k_r * sh_r - k_i * sh_i), tot_i + (k_r * sh_i + k_i * sh_r)
        sre_ref[b, 0] = tot_r[n_t - 1:n_t, :]
        sim_ref[b, 0] = tot_i[n_t - 1:n_t, :]
        c_r = jnp.where(first_t, s0r, pltpu.roll(tot_r, 1, 0))
        c_i = jnp.where(first_t, s0i, pltpu.roll(tot_i, 1, 0))
        full_r, full_i = [], []
        for t in range(n_t):
            rs = slice(t * SUBLANES, (t + 1) * SUBLANES)
            e_r, e_i = c_r[t:t + 1, :], c_i[t:t + 1, :]
            full_r.append(gr[rs] + (w_r * e_r - w_i * e_i))
            full_i.append(gi[rs] + (w_r * e_i + w_i * e_r))
        gr, gi = jnp.concatenate(full_r, axis=0), jnp.concatenate(full_i, axis=0)
        sp = jnp.concatenate([jnp.where(first, s0r, pltpu.roll(gr, 1, 0)),
                              jnp.where(first, s0i, pltpu.roll(gi, 1, 0))],
                             axis=1).astype(BF16)
        y2 = y_intra + lax.dot_general(sp, wst_ref[0], NT_DIMS, preferred_element_type=F32)
        for j in range(CHUNK):
            sl = slice(j * LANES, (j + 1) * LANES)
            y_ref[pl.ds(b * nc * CHUNK + j, nc, stride=CHUNK), :] = _gelu_tanh(y2[:, sl])
        if b + 1 < batch:
            ucat, ub, g = ucat_next, ub_next, g_next


def _ssm(h_p, h_s, h0r, h0i, t8, win, wst, wc, a_r, a_i, l_r, l_i, p_r, p_i, d_skip, batch, n_s):
    ub = COL_U // LANES
    rows = h_p.shape[0]
    wspec = pl.BlockSpec((1, CHUNK_WIDTH, CHUNK_WIDTH), lambda gt: (gt, 0, 0))
    aspec = pl.BlockSpec((1, SCAN_LEVELS, STATE_TILE), lambda gt: (gt, 0, 0))
    sspec = pl.BlockSpec((batch, 1, 1, STATE_TILE), lambda gt: (0, gt, 0, 0))
    st = pl.BlockSpec((n_s, STATE_TILE), lambda gt: (0, gt))
    lam = pl.BlockSpec((1, 1, STATE_TILE), lambda gt: (gt, 0, 0))
    return pl.pallas_call(
        _ssm_kernel,
        grid=(N_GROUP_TILES,),
        in_specs=[
            pl.BlockSpec((rows, LANES), lambda gt: (0, ub + gt)),
            pl.BlockSpec((N_META, LANES), lambda gt: (h_s.shape[0] // N_META - 1, ub + gt)),
            pl.BlockSpec((n_s, LANES), lambda gt: (0, ub + gt)),
            st, st,
            wspec, wspec, wspec,
            pl.BlockSpec((1, LANES, 2 * STATE_TILE), lambda gt: (gt, 0, 0)),
            aspec, aspec, lam, lam,
            pl.BlockSpec((1, SUBLANES, STATE_TILE), lambda gt: (gt, 0, 0)),
            pl.BlockSpec((1, SUBLANES, STATE_TILE), lambda gt: (gt, 0, 0)),
            pl.BlockSpec((1, LANES), lambda gt: (0, gt)),
        ],
        out_specs=[pl.BlockSpec((rows, LANES), lambda gt: (0, gt)), sspec, sspec,
                   pl.BlockSpec((n_s, LANES), lambda gt: (0, gt)), st, st],
        out_shape=[
            jax.ShapeDtypeStruct((rows, SSM_WIDTH), F32),
            jax.ShapeDtypeStruct((batch, N_GROUP_TILES, 1, STATE_TILE), F32),
            jax.ShapeDtypeStruct((batch, N_GROUP_TILES, 1, STATE_TILE), F32),
            jax.ShapeDtypeStruct((n_s, SSM_WIDTH), F32),
            jax.ShapeDtypeStruct((n_s, N_SSM_GROUPS * SSM_STATE), F32),
            jax.ShapeDtypeStruct((n_s, N_SSM_GROUPS * SSM_STATE), F32),
        ],
        scratch_shapes=[pltpu.VMEM((STATE_TILE // LANES, rows // CHUNK // batch, LANES), F32)] * 2,
        compiler_params=pltpu.CompilerParams(
            dimension_semantics=("arbitrary",), vmem_limit_bytes=VMEM_LIMIT),
        name="ssm",
    )(h_p, h_s, h_s, h0r, h0i, t8, win, wst, wc, a_r, a_i, l_r, l_i, p_r, p_i, d_skip)


def _merge_kernel(o_ref, za_ref, zs_ref, ga_ref, gs_ref, y_ref, x_ref,
                  o2_ref, za2_ref, zs2_ref, ga2_ref, gs2_ref, y2_ref, x2_ref,
                  bglu_ref, wglu_ref, wa_ref, ws_ref, wo_ref, out_ref, out2_ref):
    def merge(o_r, za_r, zs_r, ga_r, gs_r, y_r, x_r, out_r):
        za = za_r[...]
        a_in = (o_r[...] * (za * jax.nn.sigmoid(za))).astype(BF16)
        gated_a = jax.nn.sigmoid(ga_r[...]) * jnp.dot(a_in, wa_ref[...],
                                                      preferred_element_type=F32)
        y = y_r[...]
        t = jnp.dot(y.astype(BF16), wglu_ref[...], preferred_element_type=F32) + bglu_ref[...]
        y = y * jax.nn.sigmoid(t)
        zs = zs_r[...]
        s_in = (y * (zs * jax.nn.sigmoid(zs))).astype(BF16)
        br_s = jnp.dot(s_in, ws_ref[...], preferred_element_type=F32)
        mix = (gated_a + jax.nn.sigmoid(gs_r[...]) * br_s).astype(BF16)
        out_r[...] = x_r[...] + jnp.dot(mix, wo_ref[...], preferred_element_type=F32)

    merge(o_ref, za_ref, zs_ref, ga_ref, gs_ref, y_ref, x_ref, out_ref)

    @pl.when(pl.program_id(0) == pl.num_programs(0) - 1)
    def _():
        merge(o2_ref, za2_ref, zs2_ref, ga2_ref, gs2_ref, y2_ref, x2_ref, out2_ref)


def _merge(o, h, y, x, o2, h2, y2, x2, b_glu, w_glu, w_a, w_s, w_o, tm):
    m, m2 = x.shape[0], x2.shape[0]
    const = lambda shape: pl.BlockSpec(shape, lambda i: (0, 0), pipeline_mode=pl.Buffered(1))
    rows = lambda width, col: pl.BlockSpec((tm, width), lambda i: (i, col))
    rows2 = lambda width, col: pl.BlockSpec((m2, width), lambda i: (0, col),
                                            pipeline_mode=pl.Buffered(1))
    operands = lambda spec: [
        spec(ATTN_WIDTH, 0), spec(ATTN_WIDTH, COL_ZA // ATTN_WIDTH),
        spec(SSM_WIDTH, COL_ZS // SSM_WIDTH), spec(D_MODEL, COL_GA // D_MODEL),
        spec(D_MODEL, COL_GS // D_MODEL), spec(SSM_WIDTH, 0), spec(D_MODEL, 0)]
    return pl.pallas_call(
        _merge_kernel,
        grid=(m // tm,),
        in_specs=operands(rows) + operands(rows2) + [
            const((1, SSM_WIDTH)),
            const((SSM_WIDTH, SSM_WIDTH)),
            const((ATTN_WIDTH, D_MODEL)),
            const((SSM_WIDTH, D_MODEL)),
            const((D_MODEL, D_MODEL)),
        ],
        out_specs=[pl.BlockSpec((tm, D_MODEL), lambda i: (i, 0)),
                   pl.BlockSpec((m2, D_MODEL), lambda i: (0, 0))],
        out_shape=[jax.ShapeDtypeStruct((m, D_MODEL), F32),
                   jax.ShapeDtypeStruct((m2, D_MODEL), F32)],
        compiler_params=pltpu.CompilerParams(
            dimension_semantics=("arbitrary",), vmem_limit_bytes=VMEM_LIMIT),
        name="merge",
    )(o, h, h, h, h, y, x, o2, h2, h2, h2, h2, y2, x2, b_glu, w_glu, w_a, w_s, w_o)


def _rope_tables(pos):
    half = HEAD_DIM // 2
    inv_freq = (ROPE_THETA ** (-np.arange(half, dtype=np.float32) / half)).astype(np.float32)
    ang = np.asarray(pos, np.float32)[:, None] * inv_freq[None, :]
    cos, sin = np.cos(ang).astype(np.float32), np.sin(ang).astype(np.float32)
    zero = np.zeros_like(sin)
    reps = LANES // HEAD_DIM
    cos_t = np.tile(np.concatenate([cos, cos], axis=1), (1, reps))
    sin_a = np.tile(np.concatenate([-sin, zero], axis=1), (1, reps))
    sin_b = np.tile(np.concatenate([zero, sin], axis=1), (1, reps))
    return jnp.asarray(cos_t), jnp.asarray(sin_a), jnp.asarray(sin_b)


def _split_bf16(x):
    hi = x.astype(BF16)
    return hi, (x - hi.astype(F32)).astype(BF16)


def _dot_nt_f32(a, b):
    ah, al = _split_bf16(a)
    bh, bl = _split_bf16(b)
    dot = lambda x, y: lax.dot_general(x, y, NT_DIMS, preferred_element_type=F32)
    return dot(ah, bh) + dot(ah, bl) + dot(al, bh)


def _ssm_tables_kernel(are_ref, aim_ref, ldt_ref, bre_ref, bim_ref, cre_ref, cim_ref,
                       win_ref, wst_ref, t8_ref, wc_ref, ar_ref, ai_ref, lr_ref, li_ref,
                       pr_ref, pi_ref):
    a_r, a_i = are_ref[0], aim_ref[0]
    dt = jnp.exp(ldt_ref[0])
    mag = jnp.exp(a_r * dt)
    l_r, l_i = mag * jnp.cos(a_i * dt), mag * jnp.sin(a_i * dt)
    den = a_r * a_r + a_i * a_i
    f_r = ((l_r - 1.0) * a_r + l_i * a_i) / den
    f_i = (l_i * a_r - (l_r - 1.0) * a_i) / den
    shape = (LANES, STATE_TILE)
    own = (lax.broadcasted_iota(jnp.int32, shape, 0) // SSM_GROUP
           == lax.broadcasted_iota(jnp.int32, shape, 1) // SSM_STATE)
    sq = (LANES, LANES)
    same_group = (lax.broadcasted_iota(jnp.int32, sq, 0) // SSM_GROUP
                  == lax.broadcasted_iota(jnp.int32, sq, 1) // SSM_GROUP)
    lo = lax.broadcasted_iota(jnp.int32, (1, LANES), 1) < SSM_STATE
    n_rep = STATE_TILE // LANES

    def expand(x):
        return jnp.where(own, jnp.concatenate([x] * n_rep, axis=1), 0.0)

    def compact(v):
        e = jnp.where(own, jnp.broadcast_to(v, shape), 0.0)
        s = e[:, :LANES]
        for k in range(1, n_rep):
            s = s + e[:, k * LANES:(k + 1) * LANES]
        return s + pltpu.roll(s, SSM_STATE, 1)

    lc_r, lc_i = compact(l_r), compact(l_i)
    b_r, b_i = bre_ref[0], bim_ref[0]
    c_r, c_i = cre_ref[0], cim_ref[0]
    bb_r, bb_i = _cmul(compact(f_r), compact(f_i), b_r, b_i)
    wc_ref[0] = jnp.concatenate([expand(c_r), expand(-c_i)], axis=1).astype(BF16)
    c_cat = jnp.where(lo, c_r, -c_i)
    q_r, q_i = jnp.ones(sq, F32), jnp.zeros(sq, F32)
    for n in range(CHUNK):
        x_r, x_i = _cmul(q_r, q_i, bb_r, bb_i)
        win_ref[0, (CHUNK - 1 - n) * LANES:(CHUNK - n) * LANES, :] = jnp.concatenate(
            [expand(x_r), expand(x_i)], axis=1).astype(BF16)
        k_n = _dot_nt_f32(jnp.where(lo, x_r, x_i), c_cat)
        k_n = jnp.where(same_group, k_n, 0.0).astype(BF16)
        for j in range(CHUNK - n):
            t8_ref[0, j * LANES:(j + 1) * LANES, (j + n) * LANES:(j + n + 1) * LANES] = k_n
            if n > 0:
                t8_ref[0, (j + n) * LANES:(j + n + 1) * LANES, j * LANES:(j + 1) * LANES] = (
                    jnp.zeros(sq, BF16))
        q_r, q_i = _cmul(q_r, q_i, lc_r, lc_i)
        s_r, s_i = _cmul(c_r, c_i, q_r, q_i)
        wst_ref[0, n * LANES:(n + 1) * LANES, :] = jnp.concatenate(
            [expand(s_r), expand(-s_i)], axis=1).astype(BF16)
    p_r, p_i = l_r, l_i
    for _ in range(CHUNK.bit_length() - 1):
        p_r, p_i = _cmul(p_r, p_i, p_r, p_i)
    t_r, t_i = p_r, p_i
    for i in range(SUBLANES):
        pr_ref[0, i:i + 1, :] = t_r
        pi_ref[0, i:i + 1, :] = t_i
        t_r, t_i = _cmul(t_r, t_i, p_r, p_i)
    for k in range(SCAN_LEVELS):
        ar_ref[0, k:k + 1, :] = p_r
        ai_ref[0, k:k + 1, :] = p_i
        p_r, p_i = _cmul(p_r, p_i, p_r, p_i)
    lr_ref[0] = l_r
    li_ref[0] = l_i


def _ssm_tables(a_re, a_im, log_dt, b_re, b_im, c_re, c_im):
    gt = N_GROUP_TILES
    lane_vec = lambda v: v.reshape(gt, 1, STATE_TILE)
    dup = lambda v: jnp.concatenate([v, v], axis=-1)
    rows = GROUPS_PER_TILE * SSM_GROUP
    b_t = lambda v: dup(v.transpose(0, 2, 1).reshape(gt, rows, SSM_STATE))
    c_t = lambda v: dup(v.reshape(gt, rows, SSM_STATE))
    vec = pl.BlockSpec((1, 1, STATE_TILE), lambda g: (g, 0, 0))
    par = pl.BlockSpec((1, rows, LANES), lambda g: (g, 0, 0))
    big = pl.BlockSpec((1, CHUNK_WIDTH, CHUNK_WIDTH), lambda g: (g, 0, 0))
    wcs = pl.BlockSpec((1, LANES, 2 * STATE_TILE), lambda g: (g, 0, 0))
    scan = pl.BlockSpec((1, SCAN_LEVELS, STATE_TILE), lambda g: (g, 0, 0))
    tile = pl.BlockSpec((1, SUBLANES, STATE_TILE), lambda g: (g, 0, 0))
    return pl.pallas_call(
        _ssm_tables_kernel,
        grid=(gt,),
        in_specs=[vec, vec, vec, par, par, par, par],
        out_specs=[big, big, big, wcs, scan, scan, vec, vec, tile, tile],
        out_shape=[
            jax.ShapeDtypeStruct((gt, CHUNK_WIDTH, 2 * STATE_TILE), BF16),
            jax.ShapeDtypeStruct((gt, CHUNK_WIDTH, 2 * STATE_TILE), BF16),
            jax.ShapeDtypeStruct((gt, CHUNK_WIDTH, CHUNK_WIDTH), BF16),
            jax.ShapeDtypeStruct((gt, LANES, 2 * STATE_TILE), BF16),
            jax.ShapeDtypeStruct((gt, SCAN_LEVELS, STATE_TILE), F32),
            jax.ShapeDtypeStruct((gt, SCAN_LEVELS, STATE_TILE), F32),
            jax.ShapeDtypeStruct((gt, 1, STATE_TILE), F32),
            jax.ShapeDtypeStruct((gt, 1, STATE_TILE), F32),
            jax.ShapeDtypeStruct((gt, SUBLANES, STATE_TILE), F32),
            jax.ShapeDtypeStruct((gt, SUBLANES, STATE_TILE), F32),
        ],
        compiler_params=pltpu.CompilerParams(
            dimension_semantics=("arbitrary",), vmem_limit_bytes=VMEM_LIMIT),
        name="ssm_tables",
    )(lane_vec(a_re), lane_vec(a_im), lane_vec(jnp.repeat(log_dt, SSM_STATE)),
      b_t(b_re), b_t(b_im), c_t(c_re), c_t(c_im))


def kernel(x_prompt, x_sample, cache_win_k, cache_win_v, cache_meta_k, cache_meta_v,
           state_ssm_re, state_ssm_im, meta_tokens, norm_gain, w_in, q_norm_gain, k_norm_gain,
           sinks, a_re, a_im, log_dt, b_re, b_im, c_re, c_im, d_skip, w_glu, b_glu,
           w_attn_out, w_ssm_out, w_out):
    depth = w_in.shape[0]
    assert depth == 1, "single-layer trunk"
    batch, seq = x_prompt.shape[:2]
    n_s = x_sample.shape[0]
    assert x_sample.shape[1] == 1 and seq % (CHUNK * BLOCK) == 0
    l = 0

    w = w_in[l]
    qg = jnp.tile(q_norm_gain[l], TN // HEAD_DIM)[None]
    kg = jnp.tile(k_norm_gain[l], KV_WIDTH // HEAD_DIM)[None]
    lane = np.arange(LANES)
    e128 = np.where((lane[:, None] // HEAD_DIM) == (lane[None, :] // HEAD_DIM),
                    1.0 / HEAD_DIM, 0.0).astype(np.float32)
    e = jnp.asarray(np.concatenate([e128, e128], axis=0), BF16)
    gain = norm_gain[l][None]

    pos_p = N_META + np.arange(seq)
    pos_small = np.concatenate([np.full((n_s,), PAST_LEN), np.arange(N_META)])
    x_small = jnp.concatenate([x_sample[:, 0, :], meta_tokens.astype(x_prompt.dtype)], axis=0)
    xp = x_prompt.reshape(batch * seq, D_MODEL)
    xn_s = _rmsnorm(x_small, gain, tm=n_s + N_META)
    h_p, h_s, w_glu_bf, w_a_bf, w_s_bf, w_o_bf = _inproj(
        xp, gain, xn_s, w, qg, kg, e, _rope_tables(pos_p), _rope_tables(pos_small),
        (w_glu[l], w_attn_out[l], w_ssm_out[l], w_out[l]), tm=1024, halves=2)

    win, wst, t8, wc, a_r, a_i, l_r, l_i, p_r, p_i = _ssm_tables(
        a_re[l], a_im[l], log_dt[l], b_re[l], b_im[l], c_re[l], c_im[l])
    dsk = d_skip[l][None]

    vm_t = h_s[n_s:, COL_KV + KV_WIDTH:IN_WIDTH].T
    o_p = _prompt_attention(sinks[l], h_p, h_s, vm_t, batch, seq)
    q_s = h_s[:n_s, COL_Q:COL_Q + ATTN_WIDTH].reshape(n_s, N_Q_HEADS, 1, HEAD_DIM)
    sel = (np.arange(N_Q_HEADS)[:, None] // (N_Q_HEADS // N_KV_HEADS)
           == np.arange(N_KV_HEADS)[None, :]).astype(np.float32)
    qp = (q_s * sel[None, :, :, None]).reshape(n_s, N_Q_HEADS, KV_WIDTH)
    bt = 16
    to_t = lambda c: c.transpose(0, 2, 3, 1).reshape(n_s, KV_WIDTH, WINDOW)
    from_t = lambda c: c.reshape(n_s, N_KV_HEADS, HEAD_DIM, WINDOW).transpose(0, 3, 1, 2)[None]
    new_t = lambda c0: h_s[:n_s, c0:c0 + KV_WIDTH].reshape(n_s // bt, bt, KV_WIDTH).transpose(0, 2, 1)
    o_s, s_win_k, s_win_v = _sample_attention(
        qp, new_t(COL_KV), new_t(COL_KV + KV_WIDTH), h_s, to_t(cache_win_k[l]), to_t(cache_win_v[l]),
        cache_meta_k[l].reshape(n_s, N_META, KV_WIDTH), cache_meta_v[l].reshape(n_s, N_META, KV_WIDTH),
        sinks[l][:, None], bt=bt)
    o_s = o_s[:, :, :HEAD_DIM].reshape(n_s, ATTN_WIDTH)

    y_p, p_re, p_im, y_s, s_re, s_im = _ssm(
        h_p, h_s, state_ssm_re[l].reshape(n_s, -1), state_ssm_im[l].reshape(n_s, -1),
        t8, win, wst, wc, a_r, a_i, l_r, l_i, p_r, p_i, dsk, batch, n_s)
    y_prompt, y_sample = _merge(
        o_p, h_p, y_p, xp, o_s, h_s, y_s, x_sample[:, 0, :], b_glu[l][None],
        w_glu_bf, w_a_bf, w_s_bf, w_o_bf, tm=256)
    y_prompt = y_prompt.reshape(batch, seq, D_MODEL)
    y_sample = y_sample.reshape(n_s, 1, D_MODEL)

    kv_p = h_p.reshape(batch, seq, IN_WIDTH)[:, seq - WINDOW:, COL_KV:]
    kv_p = kv_p.reshape(batch, WINDOW, 2, N_KV_HEADS, HEAD_DIM)
    p_win_k = kv_p[:, :, 0][None]
    p_win_v = kv_p[:, :, 1][None]
    kv_m = h_s[n_s:, COL_KV:IN_WIDTH].reshape(N_META, 2, N_KV_HEADS, HEAD_DIM)
    p_meta_k = jnp.broadcast_to(kv_m[None, :, 0], (batch, N_META, N_KV_HEADS, HEAD_DIM))[None]
    p_meta_v = jnp.broadcast_to(kv_m[None, :, 1], (batch, N_META, N_KV_HEADS, HEAD_DIM))[None]
    p_ssm_re = p_re.reshape(batch, N_SSM_GROUPS, SSM_STATE)[None]
    p_ssm_im = p_im.reshape(batch, N_SSM_GROUPS, SSM_STATE)[None]
    st_shape = (1, n_s, N_SSM_GROUPS, SSM_STATE)
    return (y_prompt, y_sample, p_win_k, p_win_v, p_meta_k, p_meta_v, p_ssm_re, p_ssm_im,
            from_t(s_win_k), from_t(s_win_v), s_re.reshape(st_shape), s_im.reshape(st_shape))
```

```python
import functools

import jax
import jax.numpy as jnp
import numpy as np
from jax import lax
from jax.experimental import pallas as pl
from jax.experimental.pallas import tpu as pltpu

F32 = jnp.float32
BF16 = jnp.bfloat16

D_MODEL = 2048
N_META = 16
HEAD_DIM = 64
N_Q_HEADS = 16
N_KV_HEADS = 4
ATTN_WIDTH = N_Q_HEADS * HEAD_DIM
KV_WIDTH = N_KV_HEADS * HEAD_DIM
WINDOW = 128
BLOCK = 128
ROPE_THETA = 10000.0
SSM_WIDTH = D_MODEL // 2
SSM_GROUP = 16
N_SSM_GROUPS = SSM_WIDTH // SSM_GROUP
SSM_STATE = 64
EPS = 1e-6
PAST_LEN = 8192

LANES = 128
SUBLANES = 8
CHUNK = 8
GROUPS_PER_TILE = LANES // SSM_GROUP
N_GROUP_TILES = N_SSM_GROUPS // GROUPS_PER_TILE
CHUNK_WIDTH = CHUNK * LANES
STATE_TILE = GROUPS_PER_TILE * SSM_STATE
SCAN_LEVELS = 8

COL_Q = 0
COL_ZA = 1024
COL_U = 2048
COL_ZS = 3072
COL_GA = 4096
COL_GS = 6144
COL_KV = 8192
IN_WIDTH = 8704
TN = 512
CAST_ROWS = 16
INPROJ_ROWS = 1024
INPROJ_HALVES = 2
MERGE_ROWS = 256
SAMPLE_ATTN_BATCH = 16
ATTN_BLOCKS_PER_STEP = 4
KV_TILE = COL_KV // TN
VMEM_LIMIT = 56 * 1024 * 1024
NEG = -1e30
NT_DIMS = (((1,), (1,)), ((), ()))


def _cmul(ar, ai, br, bi):
    return ar * br - ai * bi, ar * bi + ai * br


def _rmsnorm_kernel(x_ref, gain_ref, o_ref):
    x = x_ref[...]
    r = lax.rsqrt(jnp.mean(x * x, axis=-1, keepdims=True) + EPS)
    o_ref[...] = (x * r * gain_ref[...]).astype(BF16)


def _rmsnorm(x, gain, tm):
    m = x.shape[0]
    return pl.pallas_call(
        _rmsnorm_kernel,
        grid=(m // tm,),
        in_specs=[pl.BlockSpec((tm, D_MODEL), lambda i: (i, 0)),
                  pl.BlockSpec((1, D_MODEL), lambda i: (0, 0))],
        out_specs=pl.BlockSpec((tm, D_MODEL), lambda i: (i, 0)),
        out_shape=jax.ShapeDtypeStruct((m, D_MODEL), BF16),
        compiler_params=pltpu.CompilerParams(
            dimension_semantics=("arbitrary",), vmem_limit_bytes=VMEM_LIMIT),
        name="rmsnorm",
    )(x, gain)


def _inproj_epilogue(j, o_ref, qg_ref, kg_ref, e_ref, cos_ref, sa_ref, sb_ref):
    def head_norm_rope(n_chunks, gain_ref, scale):
        cos, sa, sb = cos_ref[...], sa_ref[...], sb_ref[...]
        for c in range(n_chunks):
            cs = slice(c * LANES, (c + 1) * LANES)
            ac = o_ref[:, cs]
            sq = ac * ac
            hi = sq.astype(BF16)
            lo = (sq - hi.astype(F32)).astype(BF16)
            ms = jnp.dot(jnp.concatenate([hi, lo], axis=1), e_ref[...],
                         preferred_element_type=F32)
            xc = ac * lax.rsqrt(ms + EPS) * gain_ref[:, cs]
            o_ref[:, cs] = (xc * cos + pltpu.roll(xc, LANES - HEAD_DIM // 2, 1) * sa
                            + pltpu.roll(xc, HEAD_DIM // 2, 1) * sb) * scale

    @pl.when(j < COL_ZA // TN)
    def _():
        head_norm_rope(TN // LANES, qg_ref, HEAD_DIM ** -0.5)

    @pl.when(j == KV_TILE)
    def _():
        head_norm_rope(KV_WIDTH // LANES, kg_ref, 1.0)


def _inproj_kernel(x_ref, gain_ref, xs_ref, w_ref, qg_ref, kg_ref, e_ref, cos_ref, sa_ref, sb_ref,
                   cos_s_ref, sa_s_ref, sb_s_ref, *rest, mw_slabs):
    n_mw = (len(rest) - 4) // 2
    mw_refs, (o_ref, os_ref) = rest[:n_mw], rest[n_mw:n_mw + 2]
    mw_out_refs, (xn_ref, wbf_ref) = rest[n_mw + 2:2 * n_mw + 2], rest[2 * n_mw + 2:]
    h, j, i = pl.program_id(0), pl.program_id(1), pl.program_id(2)
    tm = o_ref.shape[0]
    rows = pl.ds(pl.multiple_of(i * tm, tm), tm)

    @pl.when(j == 0)
    def _():
        x = x_ref[...]
        r = lax.rsqrt(jnp.mean(x * x, axis=-1, keepdims=True) + EPS)
        xn_ref[rows, :] = (x * r * gain_ref[...]).astype(BF16)

    @pl.when(i == 0)
    def _():
        wbf_ref[...] = w_ref[...].astype(BF16)

    o_ref[...] = jnp.dot(xn_ref[rows, :], wbf_ref[...], preferred_element_type=F32)
    _inproj_epilogue(j, o_ref, qg_ref, kg_ref, e_ref, cos_ref, sa_ref, sb_ref)

    step = (h * pl.num_programs(1) + j) * pl.num_programs(2) + i
    for src, dst, n_slabs in zip(mw_refs, mw_out_refs, mw_slabs):
        @pl.when(step < n_slabs)
        def _():
            dst[...] = src[...].astype(BF16)

    @pl.when(jnp.logical_and(h == 0, jnp.logical_and(j == 0, i == 0)))
    def _():
        os_ref[...] = jnp.zeros(os_ref.shape, F32)

    @pl.when(jnp.logical_and(h == pl.num_programs(0) - 1, i == pl.num_programs(2) - 1))
    def _():
        os_ref[...] = jnp.dot(xs_ref[...], wbf_ref[...], preferred_element_type=F32)
        _inproj_epilogue(j, os_ref, qg_ref, kg_ref, e_ref, cos_s_ref, sa_s_ref, sb_s_ref)


def _w_tile(j):
    n_q = COL_ZA // TN
    return jnp.where(j < n_q, j, jnp.where(j == KV_TILE, n_q, j + 1))


def _inproj(x, gain, xs, w, qg, kg, e, tabs_p, tabs_s, merge_weights, tm, halves):
    m, ms = x.shape[0], xs.shape[0]
    rows_half = m // halves
    tiles = rows_half // tm
    n_tab = tabs_p[0].shape[0] // tm
    n_col = IN_WIDTH // TN
    const = lambda shape: pl.BlockSpec(shape, lambda h, j, i: (0, 0))
    tab_p = pl.BlockSpec((tm, LANES), lambda h, j, i: ((h * tiles + i) % n_tab, 0))
    n_steps = halves * n_col * tiles
    for mw in merge_weights:
        assert mw.shape[0] % CAST_ROWS == 0 and mw.shape[0] // CAST_ROWS <= n_steps
    slab = lambda mw: pl.BlockSpec(
        (CAST_ROWS, mw.shape[1]),
        lambda h, j, i: (jnp.minimum((h * n_col + j) * tiles + i, mw.shape[0] // CAST_ROWS - 1), 0))
    return pl.pallas_call(
        functools.partial(_inproj_kernel,
                          mw_slabs=tuple(mw.shape[0] // CAST_ROWS for mw in merge_weights)),
        grid=(halves, IN_WIDTH // TN, tiles),
        in_specs=[
            pl.BlockSpec((tm, D_MODEL),
                         lambda h, j, i: (h * tiles + jnp.where(j == 0, i, tiles - 1), 0)),
            const((1, D_MODEL)),
            const((ms, D_MODEL)),
            pl.BlockSpec((D_MODEL, TN), lambda h, j, i: (0, _w_tile(j))),
            const((1, TN)), const((1, KV_WIDTH)), const((2 * LANES, LANES)),
            tab_p, tab_p, tab_p,
            const((ms, LANES)), const((ms, LANES)), const((ms, LANES)),
        ] + [slab(mw) for mw in merge_weights],
        out_specs=[pl.BlockSpec((tm, TN), lambda h, j, i: (h * tiles + i, j)),
                   pl.BlockSpec((ms, TN),
                                lambda h, j, i: (0, jnp.where(h == halves - 1, j, n_col)))
                   ] + [slab(mw) for mw in merge_weights],
        out_shape=[jax.ShapeDtypeStruct((m, IN_WIDTH), F32),
                   jax.ShapeDtypeStruct((ms, IN_WIDTH + TN), F32)
                   ] + [jax.ShapeDtypeStruct(mw.shape, BF16) for mw in merge_weights],
        scratch_shapes=[pltpu.VMEM((rows_half, D_MODEL), BF16), pltpu.VMEM((D_MODEL, TN), BF16)],
        compiler_params=pltpu.CompilerParams(
            dimension_semantics=("arbitrary", "arbitrary", "arbitrary"),
            vmem_limit_bytes=VMEM_LIMIT),
        name="inproj",
    )(x, gain, xs, w, qg, kg, e, *tabs_p, *tabs_s, *merge_weights)


def _dup_head(x, g, lo):
    r = pltpu.roll(x, HEAD_DIM, 1)
    return jnp.where(lo, x, r) if g % 2 == 0 else jnp.where(lo, r, x)


def _prompt_attn_kernel(sinks_ref, q_ref, kvc_ref, kvp_ref, kvm_ref, vmt_ref, bias_ref, o_ref):
    n_blk = q_ref.shape[0] // BLOCK
    lo = lax.broadcasted_iota(jnp.int32, (1, LANES), 1) < HEAD_DIM
    head_lane = lax.broadcasted_iota(jnp.int32, (1, 4 * BLOCK), 1) // BLOCK

    def prev_rows(blk, cols):
        if blk == 0:
            return kvp_ref[:, cols]
        return kvc_ref[(blk - 1) * BLOCK:blk * BLOCK, cols]

    scores = []
    for blk in range(n_blk):
        rows = slice(blk * BLOCK, (blk + 1) * BLOCK)
        for g in range(N_KV_HEADS):
            ks = slice((g // 2) * LANES, (g // 2 + 1) * LANES)
            k_all = jnp.concatenate(
                [_dup_head(prev_rows(blk, ks), g, lo), _dup_head(kvc_ref[rows, ks], g, lo),
                 _dup_head(kvm_ref[:, ks], g, lo)], axis=0).astype(BF16)
            qa = q_ref[rows, (2 * g) * LANES:(2 * g + 1) * LANES]
            qb = q_ref[rows, (2 * g + 1) * LANES:(2 * g + 2) * LANES]
            q4 = jnp.concatenate([jnp.where(lo, qa, 0.0), jnp.where(lo, 0.0, qa),
                                  jnp.where(lo, qb, 0.0), jnp.where(lo, 0.0, qb)],
                                 axis=0).astype(BF16)
            scores.append(lax.dot_general(k_all, q4, NT_DIMS, preferred_element_type=F32))
    for blk in range(n_blk):
        rows = slice(blk * BLOCK, (blk + 1) * BLOCK)
        bias = bias_ref[jnp.minimum(pl.program_id(1), 1)] if blk == 0 else bias_ref[1]
        vs = slice(KV_WIDTH, 2 * KV_WIDTH)
        v_t = jnp.concatenate([prev_rows(blk, vs).T, kvc_ref[rows, vs].T], axis=1)
        for g in range(N_KV_HEADS):
            s = scores[blk * N_KV_HEADS + g]
            s_w = s[:2 * BLOCK] + bias
            s_m = s[2 * BLOCK:]
            sink = jnp.where(head_lane == 0, sinks_ref[4 * g],
                             jnp.where(head_lane == 1, sinks_ref[4 * g + 1],
                                       jnp.where(head_lane == 2, sinks_ref[4 * g + 2],
                                                 sinks_ref[4 * g + 3])))
            m = jnp.maximum(jnp.maximum(jnp.max(s_w, axis=0, keepdims=True),
                                        jnp.max(s_m, axis=0, keepdims=True)), sink)
            p_w = jnp.exp(s_w - m)
            p_m = jnp.exp(s_m - m)
            den = (jnp.sum(p_w, axis=0, keepdims=True) + jnp.sum(p_m, axis=0, keepdims=True)
                   + jnp.exp(sink - m))
            inv = 1.0 / den
            p = jnp.concatenate([p_w * inv, p_m * inv], axis=0).astype(BF16)
            hs = slice(g * HEAD_DIM, (g + 1) * HEAD_DIM)
            vt_g = jnp.concatenate([v_t[hs], vmt_ref[hs, :]], axis=1).astype(BF16)
            o_t = jnp.dot(vt_g, p, preferred_element_type=F32)
            for pair in range(2):
                two = jnp.concatenate(
                    [o_t[:, (2 * pair) * BLOCK:(2 * pair + 1) * BLOCK],
                     o_t[:, (2 * pair + 1) * BLOCK:(2 * pair + 2) * BLOCK]], axis=0)
                o_ref[rows, (2 * g + pair) * LANES:(2 * g + pair + 1) * LANES] = two.T


def _prompt_attention(sinks, h_p, h_s, vm_t, batch, seq):
    nb = seq // BLOCK
    kvb = COL_KV // (2 * KV_WIDTH)
    kj = np.arange(2 * BLOCK)[:, None]
    qi = np.arange(4 * BLOCK)[None, :] % BLOCK
    cur = (kj >= BLOCK) & (kj - BLOCK <= qi)
    prev = (kj < BLOCK) & (kj > qi)
    bias = jnp.asarray(np.where(np.stack([cur, cur | prev]), 0.0, NEG), F32)
    per = ATTN_BLOCKS_PER_STEP
    ns = nb // per
    return pl.pallas_call(
        _prompt_attn_kernel,
        grid=(batch, ns),
        in_specs=[
            pl.BlockSpec(memory_space=pltpu.SMEM),
            pl.BlockSpec((per * BLOCK, ATTN_WIDTH), lambda b, i: (b * ns + i, 0)),
            pl.BlockSpec((per * BLOCK, 2 * KV_WIDTH), lambda b, i: (b * ns + i, kvb)),
            pl.BlockSpec((BLOCK, 2 * KV_WIDTH),
                         lambda b, i: (b * nb + jnp.maximum(i * per - 1, 0), kvb)),
            pl.BlockSpec((N_META, 2 * KV_WIDTH), lambda b, i: (h_s.shape[0] // N_META - 1, kvb)),
            pl.BlockSpec((KV_WIDTH, N_META), lambda b, i: (0, 0)),
            pl.BlockSpec((2, 2 * BLOCK, 4 * BLOCK), lambda b, i: (0, 0, 0)),
        ],
        out_specs=pl.BlockSpec((per * BLOCK, ATTN_WIDTH), lambda b, i: (b * ns + i, 0)),
        out_shape=jax.ShapeDtypeStruct((batch * seq, ATTN_WIDTH), F32),
        compiler_params=pltpu.CompilerParams(
            dimension_semantics=("arbitrary", "arbitrary"), vmem_limit_bytes=VMEM_LIMIT),
        name="prompt_attn",
    )(sinks, h_p, h_p, h_p, h_s, vm_t, bias)


def _sample_attn_kernel(qp_ref, kn_ref, vn_ref, knr_ref, vnr_ref, ck_ref, cv_ref, mk_ref, mv_ref,
                        sink_ref, o_ref, ok_ref, ov_ref):
    bt = ck_ref.shape[0]
    slot = lax.broadcasted_iota(jnp.int32, (1, 1, WINDOW), 2)
    round_bf16 = lambda a: a.astype(BF16).astype(F32)
    qp = qp_ref[...].astype(BF16)
    ck = ck_ref[...]
    cv = cv_ref[...]
    k_m = mk_ref[...].astype(BF16)
    v_m = mv_ref[...].astype(BF16)
    s_w = jnp.einsum("bhc,bcw->bhw", qp, ck.astype(BF16), preferred_element_type=F32)
    s_w = jnp.where(slot == 0, NEG, s_w)
    s_m = jnp.einsum("bhc,bkc->bhk", qp, k_m, preferred_element_type=F32)
    knr, vnr = round_bf16(knr_ref[...]), round_bf16(vnr_ref[...])
    qf = qp.astype(F32)
    s_n = jnp.stack([jnp.sum(qf[b] * knr[b:b + 1, :], axis=-1, keepdims=True)
                     for b in range(bt)])
    sink = sink_ref[...][None]
    m = jnp.maximum(jnp.maximum(jnp.max(s_w, axis=-1, keepdims=True),
                                jnp.max(s_m, axis=-1, keepdims=True)),
                    jnp.maximum(s_n, sink))
    p_w = jnp.exp(s_w - m)
    p_m = jnp.exp(s_m - m)
    p_n = jnp.exp(s_n - m)
    den = (jnp.sum(p_w, axis=-1, keepdims=True) + jnp.sum(p_m, axis=-1, keepdims=True)
           + p_n + jnp.exp(sink - m))
    inv = 1.0 / den
    o = (jnp.einsum("bhw,bcw->bhc", (p_w * inv).astype(BF16), cv.astype(BF16),
                    preferred_element_type=F32)
         + jnp.einsum("bhk,bkc->bhc", (p_m * inv).astype(BF16), v_m,
                      preferred_element_type=F32)
         + jnp.stack([round_bf16(p_n[b] * inv[b]) * vnr[b:b + 1, :] for b in range(bt)]))

    last = lax.broadcasted_iota(jnp.int32, (1, WINDOW), 1) == WINDOW - 1
    kn = kn_ref[0]
    vn = vn_ref[0]
    ck = pltpu.roll(ck, WINDOW - 1, 2)
    cv = pltpu.roll(cv, WINDOW - 1, 2)
    ok_ref[...] = jnp.stack([jnp.where(last, kn[:, b:b + 1], ck[b]) for b in range(bt)])
    ov_ref[...] = jnp.stack([jnp.where(last, vn[:, b:b + 1], cv[b]) for b in range(bt)])
    hh = lax.broadcasted_iota(jnp.int32, (1, N_Q_HEADS, KV_WIDTH), 1) // (N_Q_HEADS // N_KV_HEADS)
    cc = lax.broadcasted_iota(jnp.int32, (1, N_Q_HEADS, KV_WIDTH), 2) // HEAD_DIM
    o = jnp.where(hh == cc, o, 0.0)
    o = o[:, :, :LANES] + o[:, :, LANES:]
    o_ref[...] = o + pltpu.roll(o, HEAD_DIM, 2)


def _sample_attention(qp, kn_t, vn_t, h_s, cache_k, cache_v, meta_k, meta_v, sink_col, bt):
    n = cache_k.shape[0]
    kvb = COL_KV // KV_WIDTH
    win_spec = pl.BlockSpec((bt, KV_WIDTH, WINDOW), lambda t: (t, 0, 0))
    new_spec = pl.BlockSpec((1, KV_WIDTH, bt), lambda t: (t, 0, 0))
    meta_spec = pl.BlockSpec((bt, N_META, KV_WIDTH), lambda t: (t, 0, 0))
    return pl.pallas_call(
        _sample_attn_kernel,
        grid=(n // bt,),
        in_specs=[
            pl.BlockSpec((bt, N_Q_HEADS, KV_WIDTH), lambda t: (t, 0, 0)),
            new_spec, new_spec,
            pl.BlockSpec((bt, KV_WIDTH), lambda t: (t, kvb)),
            pl.BlockSpec((bt, KV_WIDTH), lambda t: (t, kvb + 1)),
            win_spec, win_spec, meta_spec, meta_spec,
            pl.BlockSpec((N_Q_HEADS, 1), lambda t: (0, 0)),
        ],
        out_specs=[
            pl.BlockSpec((bt, N_Q_HEADS, LANES), lambda t: (t, 0, 0)),
            win_spec, win_spec,
        ],
        out_shape=[
            jax.ShapeDtypeStruct((n, N_Q_HEADS, LANES), F32),
            jax.ShapeDtypeStruct(cache_k.shape, F32),
            jax.ShapeDtypeStruct(cache_v.shape, F32),
        ],
        compiler_params=pltpu.CompilerParams(
            dimension_semantics=("arbitrary",), vmem_limit_bytes=VMEM_LIMIT),
        name="sample_attn",
    )(qp, kn_t, vn_t, h_s, h_s, cache_k, cache_v, meta_k, meta_v, sink_col)


def _gelu_tanh(x):
    return 0.5 * x * (1.0 + jnp.tanh(0.7978845608028654 * (x + 0.044715 * (x * x * x))))


def _ssm_kernel(u_ref, um_ref, us_ref, hr_ref, hi_ref, t8_ref, win_ref, wst_ref, wc_ref,
                ar_ref, ai_ref, lr_ref, li_ref, pr_ref, pi_ref, d_ref,
                y_ref, sre_ref, sim_ref, ys_ref, nr_ref, ni_ref, xr_ref, xi_ref):
    dsk = d_ref[...]
    us = us_ref[...]
    bu = jnp.dot(us.astype(BF16), win_ref[0, (CHUNK - 1) * LANES:, :],
                 preferred_element_type=F32)
    l_r, l_i = lr_ref[0], li_ref[0]
    h_r, h_i = hr_ref[...], hi_ref[...]
    n_r = l_r * h_r - l_i * h_i + bu[:, :STATE_TILE]
    n_i = l_r * h_i + l_i * h_r + bu[:, STATE_TILE:]
    nr_ref[...] = n_r
    ni_ref[...] = n_i
    hn = jnp.concatenate([n_r, n_i], axis=1).astype(BF16)
    ys_ref[...] = _gelu_tanh(
        lax.dot_general(hn, wc_ref[0], NT_DIMS, preferred_element_type=F32) + dsk * us)

    batch = sre_ref.shape[0]
    nc = u_ref.shape[0] // CHUNK // batch
    win = win_ref[0]
    dsk_row = jnp.concatenate([dsk] * CHUNK, axis=1)

    def chunks_of(b):
        r0 = b * nc * CHUNK
        return jnp.concatenate(
            [u_ref[pl.ds(r0 + j, nc, stride=CHUNK), :] for j in range(CHUNK)], axis=1)

    n_mc = um_ref.shape[0] // CHUNK
    um = jnp.concatenate([um_ref[pl.ds(j, n_mc, stride=CHUNK), :] for j in range(CHUNK)], axis=1)
    gm = jnp.dot(um.astype(BF16), win, preferred_element_type=F32)
    a_r, a_i = ar_ref[0, 0:1, :], ai_ref[0, 0:1, :]
    s0r, s0i = gm[0:1, :STATE_TILE], gm[0:1, STATE_TILE:]
    for c in range(1, n_mc):
        s0r, s0i = (a_r * s0r - a_i * s0i + gm[c:c + 1, :STATE_TILE],
                    a_r * s0i + a_i * s0r + gm[c:c + 1, STATE_TILE:])

    n_t = nc // SUBLANES
    row = lax.broadcasted_iota(jnp.int32, (nc, 1), 0)
    first = row == 0
    in_tile = row % SUBLANES
    t_row = lax.broadcasted_iota(jnp.int32, (n_t, 1), 0)
    first_t = t_row == 0
    n_local = SUBLANES.bit_length() - 1
    w_r, w_i = pr_ref[0], pi_ref[0]

    def tile_totals(scr_ref, x):
        parts = []
        for c in range(STATE_TILE // LANES):
            scr_ref[c] = x[:, c * LANES:(c + 1) * LANES]
            parts.append(scr_ref[c, pl.ds(SUBLANES - 1, n_t, stride=SUBLANES), :])
        return jnp.concatenate(parts, axis=1)

    def shifted(x, d, keep):
        if d < SUBLANES:
            return jnp.where(keep, pltpu.roll(x, d, 0), 0.0)
        return jnp.concatenate([jnp.zeros((d, STATE_TILE), F32), x[:x.shape[0] - d]], axis=0)

    ucat = chunks_of(0)
    ub = ucat.astype(BF16)
    g = jnp.dot(ub, win, preferred_element_type=F32)
    for b in range(batch):
        y_intra = jnp.dot(ub, t8_ref[0], preferred_element_type=F32) + dsk_row * ucat
        if b + 1 < batch:
            ucat_next = chunks_of(b + 1)
            ub_next = ucat_next.astype(BF16)
            g_next = jnp.dot(ub_next, win, preferred_element_type=F32)
        gr, gi = g[:, :STATE_TILE], g[:, STATE_TILE:]
        for k in range(n_local):
            d = 1 << k
            k_r, k_i = ar_ref[0, k:k + 1, :], ai_ref[0, k:k + 1, :]
            sh_r, sh_i = shifted(gr, d, in_tile >= d), shifted(gi, d, in_tile >= d)
            gr, gi = gr + (k_r * sh_r - k_i * sh_i), gi + (k_r * sh_i + k_i * sh_r)
        tot_r, tot_i = tile_totals(xr_ref, gr), tile_totals(xi_ref, gi)
        k_r, k_i = ar_ref[0, n_local:n_local + 1, :], ai_ref[0, n_local:n_local + 1, :]
        tot_r = tot_r + jnp.where(first_t, k_r * s0r - k_i * s0i, 0.0)
        tot_i = tot_i + jnp.where(first_t, k_r * s0i + k_i * s0r, 0.0)
        for k in range(SCAN_LEVELS - n_local):
            d = 1 << k
            if d >= n_t:
                break
            k_r = ar_ref[0, n_local + k:n_local + k + 1, :]
            k_i = ai_ref[0, n_local + k:n_local + k + 1, :]
            sh_r, sh_i = shifted(tot_r, d, t_row >= d), shifted(tot_i, d, t_row >= d)
            tot_r, tot_i = tot_r + (k_r * sh_r - k_i * sh_i), tot_i + (k_r * sh_i + k_i * sh_r)
        sre_ref[b, 0] = tot_r[n_t - 1:n_t, :]
        sim_ref[b, 0] = tot_i[n_t - 1:n_t, :]
        c_r = jnp.where(first_t, s0r, pltpu.roll(tot_r, 1, 0))
        c_i = jnp.where(first_t, s0i, pltpu.roll(tot_i, 1, 0))
        full_r, full_i = [], []
        for t in range(n_t):
            rs = slice(t * SUBLANES, (t + 1) * SUBLANES)
            e_r, e_i = c_r[t:t + 1, :], c_i[t:t + 1, :]
            full_r.append(gr[rs] + (w_r * e_r - w_i * e_i))
            full_i.append(gi[rs] + (w_r * e_i + w_i * e_r))
        gr, gi = jnp.concatenate(full_r, axis=0), jnp.concatenate(full_i, axis=0)
        sp = jnp.concatenate([jnp.where(first, s0r, pltpu.roll(gr, 1, 0)),
                              jnp.where(first, s0i, pltpu.roll(gi, 1, 0))],
                             axis=1).astype(BF16)
        y2 = y_intra + lax.dot_general(sp, wst_ref[0], NT_DIMS, preferred_element_type=F32)
        for j in range(CHUNK):
            sl = slice(j * LANES, (j + 1) * LANES)
            y_ref[pl.ds(b * nc * CHUNK + j, nc, stride=CHUNK), :] = _gelu_tanh(y2[:, sl])
        if b + 1 < batch:
            ucat, ub, g = ucat_next, ub_next, g_next


def _ssm(h_p, h_s, h0r, h0i, t8, win, wst, wc, a_r, a_i, l_r, l_i, p_r, p_i, d_skip, batch, n_s):
    ub = COL_U // LANES
    rows = h_p.shape[0]
    wspec = pl.BlockSpec((1, CHUNK_WIDTH, CHUNK_WIDTH), lambda gt: (gt, 0, 0))
    aspec = pl.BlockSpec((1, SCAN_LEVELS, STATE_TILE), lambda gt: (gt, 0, 0))
    sspec = pl.BlockSpec((batch, 1, 1, STATE_TILE), lambda gt: (0, gt, 0, 0))
    st = pl.BlockSpec((n_s, STATE_TILE), lambda gt: (0, gt))
    lam = pl.BlockSpec((1, 1, STATE_TILE), lambda gt: (gt, 0, 0))
    return pl.pallas_call(
        _ssm_kernel,
        grid=(N_GROUP_TILES,),
        in_specs=[
            pl.BlockSpec((rows, LANES), lambda gt: (0, ub + gt)),
            pl.BlockSpec((N_META, LANES), lambda gt: (h_s.shape[0] // N_META - 1, ub + gt)),
            pl.BlockSpec((n_s, LANES), lambda gt: (0, ub + gt)),
            st, st,
            wspec, wspec, wspec,
            pl.BlockSpec((1, LANES, 2 * STATE_TILE), lambda gt: (gt, 0, 0)),
            aspec, aspec, lam, lam,
            pl.BlockSpec((1, SUBLANES, STATE_TILE), lambda gt: (gt, 0, 0)),
            pl.BlockSpec((1, SUBLANES, STATE_TILE), lambda gt: (gt, 0, 0)),
            pl.BlockSpec((1, LANES), lambda gt: (0, gt)),
        ],
        out_specs=[pl.BlockSpec((rows, LANES), lambda gt: (0, gt)), sspec, sspec,
                   pl.BlockSpec((n_s, LANES), lambda gt: (0, gt)), st, st],
        out_shape=[
            jax.ShapeDtypeStruct((rows, SSM_WIDTH), F32),
            jax.ShapeDtypeStruct((batch, N_GROUP_TILES, 1, STATE_TILE), F32),
            jax.ShapeDtypeStruct((batch, N_GROUP_TILES, 1, STATE_TILE), F32),
            jax.ShapeDtypeStruct((n_s, SSM_WIDTH), F32),
            jax.ShapeDtypeStruct((n_s, N_SSM_GROUPS * SSM_STATE), F32),
            jax.ShapeDtypeStruct((n_s, N_SSM_GROUPS * SSM_STATE), F32),
        ],
        scratch_shapes=[pltpu.VMEM((STATE_TILE // LANES, rows // CHUNK // batch, LANES), F32)] * 2,
        compiler_params=pltpu.CompilerParams(
            dimension_semantics=("arbitrary",), vmem_limit_bytes=VMEM_LIMIT),
        name="ssm",
    )(h_p, h_s, h_s, h0r, h0i, t8, win, wst, wc, a_r, a_i, l_r, l_i, p_r, p_i, d_skip)


def _merge_kernel(o_ref, za_ref, zs_ref, ga_ref, gs_ref, y_ref, x_ref,
                  o2_ref, za2_ref, zs2_ref, ga2_ref, gs2_ref, y2_ref, x2_ref,
                  bglu_ref, wglu_ref, wa_ref, ws_ref, wo_ref, out_ref, out2_ref):
    def merge(o_r, za_r, zs_r, ga_r, gs_r, y_r, x_r, out_r):
        za = za_r[...]
        a_in = (o_r[...] * (za * jax.nn.sigmoid(za))).astype(BF16)
        gated_a = jax.nn.sigmoid(ga_r[...]) * jnp.dot(a_in, wa_ref[...],
                                                      preferred_element_type=F32)
        y = y_r[...]
        t = jnp.dot(y.astype(BF16), wglu_ref[...], preferred_element_type=F32) + bglu_ref[...]
        y = y * jax.nn.sigmoid(t)
        zs = zs_r[...]
        s_in = (y * (zs * jax.nn.sigmoid(zs))).astype(BF16)
        br_s = jnp.dot(s_in, ws_ref[...], preferred_element_type=F32)
        mix = (gated_a + jax.nn.sigmoid(gs_r[...]) * br_s).astype(BF16)
        out_r[...] = x_r[...] + jnp.dot(mix, wo_ref[...], preferred_element_type=F32)

    merge(o_ref, za_ref, zs_ref, ga_ref, gs_ref, y_ref, x_ref, out_ref)

    @pl.when(pl.program_id(0) == pl.num_programs(0) - 1)
    def _():
        merge(o2_ref, za2_ref, zs2_ref, ga2_ref, gs2_ref, y2_ref, x2_ref, out2_ref)


def _merge(o, h, y, x, o2, h2, y2, x2, b_glu, w_glu, w_a, w_s, w_o, tm):
    m, m2 = x.shape[0], x2.shape[0]
    const = lambda shape: pl.BlockSpec(shape, lambda i: (0, 0), pipeline_mode=pl.Buffered(1))
    rows = lambda width, col: pl.BlockSpec((tm, width), lambda i: (i, col))
    rows2 = lambda width, col: pl.BlockSpec((m2, width), lambda i: (0, col),
                                            pipeline_mode=pl.Buffered(1))
    operands = lambda spec: [
        spec(ATTN_WIDTH, 0), spec(ATTN_WIDTH, COL_ZA // ATTN_WIDTH),
        spec(SSM_WIDTH, COL_ZS // SSM_WIDTH), spec(D_MODEL, COL_GA // D_MODEL),
        spec(D_MODEL, COL_GS // D_MODEL), spec(SSM_WIDTH, 0), spec(D_MODEL, 0)]
    return pl.pallas_call(
        _merge_kernel,
        grid=(m // tm,),
        in_specs=operands(rows) + operands(rows2) + [
            const((1, SSM_WIDTH)),
            const((SSM_WIDTH, SSM_WIDTH)),
            const((ATTN_WIDTH, D_MODEL)),
            const((SSM_WIDTH, D_MODEL)),
            const((D_MODEL, D_MODEL)),
        ],
        out_specs=[pl.BlockSpec((tm, D_MODEL), lambda i: (i, 0)),
                   pl.BlockSpec((m2, D_MODEL), lambda i: (0, 0))],
        out_shape=[jax.ShapeDtypeStruct((m, D_MODEL), F32),
                   jax.ShapeDtypeStruct((m2, D_MODEL), F32)],
        compiler_params=pltpu.CompilerParams(
            dimension_semantics=("arbitrary",), vmem_limit_bytes=VMEM_LIMIT),
        name="merge",
    )(o, h, h, h, h, y, x, o2, h2, h2, h2, h2, y2, x2, b_glu, w_glu, w_a, w_s, w_o)


def _rope_tables(pos):
    half = HEAD_DIM // 2
    inv_freq = (ROPE_THETA ** (-np.arange(half, dtype=np.float32) / half)).astype(np.float32)
    ang = np.asarray(pos, np.float32)[:, None] * inv_freq[None, :]
    cos, sin = np.cos(ang).astype(np.float32), np.sin(ang).astype(np.float32)
    zero = np.zeros_like(sin)
    reps = LANES // HEAD_DIM
    cos_t = np.tile(np.concatenate([cos, cos], axis=1), (1, reps))
    sin_a = np.tile(np.concatenate([-sin, zero], axis=1), (1, reps))
    sin_b = np.tile(np.concatenate([zero, sin], axis=1), (1, reps))
    return jnp.asarray(cos_t), jnp.asarray(sin_a), jnp.asarray(sin_b)


def _split_bf16(x):
    hi = x.astype(BF16)
    return hi, (x - hi.astype(F32)).astype(BF16)


def _dot_nt_f32(a, b):
    ah, al = _split_bf16(a)
    bh, bl = _split_bf16(b)
    dot = lambda x, y: lax.dot_general(x, y, NT_DIMS, preferred_element_type=F32)
    return dot(ah, bh) + dot(ah, bl) + dot(al, bh)


def _ssm_tables_kernel(are_ref, aim_ref, ldt_ref, bre_ref, bim_ref, cre_ref, cim_ref,
                       win_ref, wst_ref, t8_ref, wc_ref, ar_ref, ai_ref, lr_ref, li_ref,
                       pr_ref, pi_ref):
    a_r, a_i = are_ref[0], aim_ref[0]
    dt = jnp.exp(ldt_ref[0])
    mag = jnp.exp(a_r * dt)
    l_r, l_i = mag * jnp.cos(a_i * dt), mag * jnp.sin(a_i * dt)
    den = a_r * a_r + a_i * a_i
    f_r = ((l_r - 1.0) * a_r + l_i * a_i) / den
    f_i = (l_i * a_r - (l_r - 1.0) * a_i) / den
    shape = (LANES, STATE_TILE)
    own = (lax.broadcasted_iota(jnp.int32, shape, 0) // SSM_GROUP
           == lax.broadcasted_iota(jnp.int32, shape, 1) // SSM_STATE)
    sq = (LANES, LANES)
    same_group = (lax.broadcasted_iota(jnp.int32, sq, 0) // SSM_GROUP
                  == lax.broadcasted_iota(jnp.int32, sq, 1) // SSM_GROUP)
    lo = lax.broadcasted_iota(jnp.int32, (1, LANES), 1) < SSM_STATE
    n_rep = STATE_TILE // LANES

    def expand(x):
        return jnp.where(own, jnp.concatenate([x] * n_rep, axis=1), 0.0)

    def compact(v):
        e = jnp.where(own, jnp.broadcast_to(v, shape), 0.0)
        s = e[:, :LANES]
        for k in range(1, n_rep):
            s = s + e[:, k * LANES:(k + 1) * LANES]
        return s + pltpu.roll(s, SSM_STATE, 1)

    lc_r, lc_i = compact(l_r), compact(l_i)
    b_r, b_i = bre_ref[0], bim_ref[0]
    c_r, c_i = cre_ref[0], cim_ref[0]
    bb_r, bb_i = _cmul(compact(f_r), compact(f_i), b_r, b_i)
    wc_ref[0] = jnp.concatenate([expand(c_r), expand(-c_i)], axis=1).astype(BF16)
    c_cat = jnp.where(lo, c_r, -c_i)
    q_r, q_i = jnp.ones(sq, F32), jnp.zeros(sq, F32)
    for n in range(CHUNK):
        x_r, x_i = _cmul(q_r, q_i, bb_r, bb_i)
        win_ref[0, (CHUNK - 1 - n) * LANES:(CHUNK - n) * LANES, :] = jnp.concatenate(
            [expand(x_r), expand(x_i)], axis=1).astype(BF16)
        k_n = _dot_nt_f32(jnp.where(lo, x_r, x_i), c_cat)
        k_n = jnp.where(same_group, k_n, 0.0).astype(BF16)
        for j in range(CHUNK - n):
            t8_ref[0, j * LANES:(j + 1) * LANES, (j + n) * LANES:(j + n + 1) * LANES] = k_n
            if n > 0:
                t8_ref[0, (j + n) * LANES:(j + n + 1) * LANES, j * LANES:(j + 1) * LANES] = (
                    jnp.zeros(sq, BF16))
        q_r, q_i = _cmul(q_r, q_i, lc_r, lc_i)
        s_r, s_i = _cmul(c_r, c_i, q_r, q_i)
        wst_ref[0, n * LANES:(n + 1) * LANES, :] = jnp.concatenate(
            [expand(s_r), expand(-s_i)], axis=1).astype(BF16)
    p_r, p_i = l_r, l_i
    for _ in range(CHUNK.bit_length() - 1):
        p_r, p_i = _cmul(p_r, p_i, p_r, p_i)
    t_r, t_i = p_r, p_i
    for i in range(SUBLANES):
        pr_ref[0, i:i + 1, :] = t_r
        pi_ref[0, i:i + 1, :] = t_i
        t_r, t_i = _cmul(t_r, t_i, p_r, p_i)
    for k in range(SCAN_LEVELS):
        ar_ref[0, k:k + 1, :] = p_r
        ai_ref[0, k:k + 1, :] = p_i
        p_r, p_i = _cmul(p_r, p_i, p_r, p_i)
    lr_ref[0] = l_r
    li_ref[0] = l_i


def _ssm_tables(a_re, a_im, log_dt, b_re, b_im, c_re, c_im):
    gt = N_GROUP_TILES
    lane_vec = lambda v: v.reshape(gt, 1, STATE_TILE)
    dup = lambda v: jnp.concatenate([v, v], axis=-1)
    rows = GROUPS_PER_TILE * SSM_GROUP
    b_t = lambda v: dup(v.transpose(0, 2, 1).reshape(gt, rows, SSM_STATE))
    c_t = lambda v: dup(v.reshape(gt, rows, SSM_STATE))
    vec = pl.BlockSpec((1, 1, STATE_TILE), lambda g: (g, 0, 0))
    par = pl.BlockSpec((1, rows, LANES), lambda g: (g, 0, 0))
    big = pl.BlockSpec((1, CHUNK_WIDTH, CHUNK_WIDTH), lambda g: (g, 0, 0))
    wcs = pl.BlockSpec((1, LANES, 2 * STATE_TILE), lambda g: (g, 0, 0))
    scan = pl.BlockSpec((1, SCAN_LEVELS, STATE_TILE), lambda g: (g, 0, 0))
    tile = pl.BlockSpec((1, SUBLANES, STATE_TILE), lambda g: (g, 0, 0))
    return pl.pallas_call(
        _ssm_tables_kernel,
        grid=(gt,),
        in_specs=[vec, vec, vec, par, par, par, par],
        out_specs=[big, big, big, wcs, scan, scan, vec, vec, tile, tile],
        out_shape=[
            jax.ShapeDtypeStruct((gt, CHUNK_WIDTH, 2 * STATE_TILE), BF16),
            jax.ShapeDtypeStruct((gt, CHUNK_WIDTH, 2 * STATE_TILE), BF16),
            jax.ShapeDtypeStruct((gt, CHUNK_WIDTH, CHUNK_WIDTH), BF16),
            jax.ShapeDtypeStruct((gt, LANES, 2 * STATE_TILE), BF16),
            jax.ShapeDtypeStruct((gt, SCAN_LEVELS, STATE_TILE), F32),
            jax.ShapeDtypeStruct((gt, SCAN_LEVELS, STATE_TILE), F32),
            jax.ShapeDtypeStruct((gt, 1, STATE_TILE), F32),
            jax.ShapeDtypeStruct((gt, 1, STATE_TILE), F32),
            jax.ShapeDtypeStruct((gt, SUBLANES, STATE_TILE), F32),
            jax.ShapeDtypeStruct((gt, SUBLANES, STATE_TILE), F32),
        ],
        compiler_params=pltpu.CompilerParams(
            dimension_semantics=("arbitrary",), vmem_limit_bytes=VMEM_LIMIT),
        name="ssm_tables",
    )(lane_vec(a_re), lane_vec(a_im), lane_vec(jnp.repeat(log_dt, SSM_STATE)),
      b_t(b_re), b_t(b_im), c_t(c_re), c_t(c_im))


def kernel(x_prompt, x_sample, cache_win_k, cache_win_v, cache_meta_k, cache_meta_v,
           state_ssm_re, state_ssm_im, meta_tokens, norm_gain, w_in, q_norm_gain, k_norm_gain,
           sinks, a_re, a_im, log_dt, b_re, b_im, c_re, c_im, d_skip, w_glu, b_glu,
           w_attn_out, w_ssm_out, w_out):
    depth = w_in.shape[0]
    assert depth == 1, "single-layer trunk"
    batch, seq = x_prompt.shape[:2]
    n_s = x_sample.shape[0]
    assert x_sample.shape[1] == 1 and seq % (CHUNK * BLOCK) == 0
    l = 0

    w = w_in[l]
    qg = jnp.tile(q_norm_gain[l], TN // HEAD_DIM)[None]
    kg = jnp.tile(k_norm_gain[l], KV_WIDTH // HEAD_DIM)[None]
    lane = np.arange(LANES)
    e128 = np.where((lane[:, None] // HEAD_DIM) == (lane[None, :] // HEAD_DIM),
                    1.0 / HEAD_DIM, 0.0).astype(np.float32)
    e = jnp.asarray(np.concatenate([e128, e128], axis=0), BF16)
    gain = norm_gain[l][None]

    pos_p = N_META + np.arange(seq)
    pos_small = np.concatenate([np.full((n_s,), PAST_LEN), np.arange(N_META)])
    x_small = jnp.concatenate([x_sample[:, 0, :], meta_tokens.astype(x_prompt.dtype)], axis=0)
    xp = x_prompt.reshape(batch * seq, D_MODEL)
    xn_s = _rmsnorm(x_small, gain, tm=n_s + N_META)
    h_p, h_s, w_glu_bf, w_a_bf, w_s_bf, w_o_bf = _inproj(
        xp, gain, xn_s, w, qg, kg, e, _rope_tables(pos_p), _rope_tables(pos_small),
        (w_glu[l], w_attn_out[l], w_ssm_out[l], w_out[l]), tm=INPROJ_ROWS, halves=INPROJ_HALVES)

    win, wst, t8, wc, a_r, a_i, l_r, l_i, p_r, p_i = _ssm_tables(
        a_re[l], a_im[l], log_dt[l], b_re[l], b_im[l], c_re[l], c_im[l])
    dsk = d_skip[l][None]

    vm_t = h_s[n_s:, COL_KV + KV_WIDTH:IN_WIDTH].T
    o_p = _prompt_attention(sinks[l], h_p, h_s, vm_t, batch, seq)
    q_s = h_s[:n_s, COL_Q:COL_Q + ATTN_WIDTH].reshape(n_s, N_Q_HEADS, 1, HEAD_DIM)
    sel = (np.arange(N_Q_HEADS)[:, None] // (N_Q_HEADS // N_KV_HEADS)
           == np.arange(N_KV_HEADS)[None, :]).astype(np.float32)
    qp = (q_s * sel[None, :, :, None]).reshape(n_s, N_Q_HEADS, KV_WIDTH)
    bt = SAMPLE_ATTN_BATCH
    to_t = lambda c: c.transpose(0, 2, 3, 1).reshape(n_s, KV_WIDTH, WINDOW)
    from_t = lambda c: c.reshape(n_s, N_KV_HEADS, HEAD_DIM, WINDOW).transpose(0, 3, 1, 2)[None]
    new_t = lambda c0: h_s[:n_s, c0:c0 + KV_WIDTH].reshape(n_s // bt, bt, KV_WIDTH).transpose(0, 2, 1)
    o_s, s_win_k, s_win_v = _sample_attention(
        qp, new_t(COL_KV), new_t(COL_KV + KV_WIDTH), h_s, to_t(cache_win_k[l]), to_t(cache_win_v[l]),
        cache_meta_k[l].reshape(n_s, N_META, KV_WIDTH), cache_meta_v[l].reshape(n_s, N_META, KV_WIDTH),
        sinks[l][:, None], bt=bt)
    o_s = o_s[:, :, :HEAD_DIM].reshape(n_s, ATTN_WIDTH)

    y_p, p_re, p_im, y_s, s_re, s_im = _ssm(
        h_p, h_s, state_ssm_re[l].reshape(n_s, -1), state_ssm_im[l].reshape(n_s, -1),
        t8, win, wst, wc, a_r, a_i, l_r, l_i, p_r, p_i, dsk, batch, n_s)
    y_prompt, y_sample = _merge(
        o_p, h_p, y_p, xp, o_s, h_s, y_s, x_sample[:, 0, :], b_glu[l][None],
        w_glu_bf, w_a_bf, w_s_bf, w_o_bf, tm=MERGE_ROWS)
    y_prompt = y_prompt.reshape(batch, seq, D_MODEL)
    y_sample = y_sample.reshape(n_s, 1, D_MODEL)

    kv_p = h_p.reshape(batch, seq, IN_WIDTH)[:, seq - WINDOW:, COL_KV:]
    kv_p = kv_p.reshape(batch, WINDOW, 2, N_KV_HEADS, HEAD_DIM)
    p_win_k = kv_p[:, :, 0][None]
    p_win_v = kv_p[:, :, 1][None]
    kv_m = h_s[n_s:, COL_KV:IN_WIDTH].reshape(N_META, 2, N_KV_HEADS, HEAD_DIM)
    p_meta_k = jnp.broadcast_to(kv_m[None, :, 0], (batch, N_META, N_KV_HEADS, HEAD_DIM))[None]
    p_meta_v = jnp.broadcast_to(kv_m[None, :, 1], (batch, N_META, N_KV_HEADS, HEAD_DIM))[None]
    p_ssm_re = p_re.reshape(batch, N_SSM_GROUPS, SSM_STATE)[None]
    p_ssm_im = p_im.reshape(batch, N_SSM_GROUPS, SSM_STATE)[None]
    st_shape = (1, n_s, N_SSM_GROUPS, SSM_STATE)
    return (y_prompt, y_sample, p_win_k, p_win_v, p_meta_k, p_meta_v, p_ssm_re, p_ssm_im,
            from_t(s_win_k), from_t(s_win_v), s_re.reshape(st_shape), s_im.reshape(st_shape))
```

```python
import functools

import jax
import jax.numpy as jnp
import numpy as np
from jax import lax
from jax.experimental import pallas as pl
from jax.experimental.pallas import tpu as pltpu

F32 = jnp.float32
BF16 = jnp.bfloat16

D_MODEL = 2048
N_META = 16
HEAD_DIM = 64
N_Q_HEADS = 16
N_KV_HEADS = 4
ATTN_WIDTH = N_Q_HEADS * HEAD_DIM
KV_WIDTH = N_KV_HEADS * HEAD_DIM
WINDOW = 128
BLOCK = 128
ROPE_THETA = 10000.0
SSM_WIDTH = D_MODEL // 2
SSM_GROUP = 16
N_SSM_GROUPS = SSM_WIDTH // SSM_GROUP
SSM_STATE = 64
EPS = 1e-6
PAST_LEN = 8192

LANES = 128
SUBLANES = 8
CHUNK = 8
GROUPS_PER_TILE = LANES // SSM_GROUP
N_GROUP_TILES = N_SSM_GROUPS // GROUPS_PER_TILE
CHUNK_WIDTH = CHUNK * LANES
STATE_TILE = GROUPS_PER_TILE * SSM_STATE
SCAN_LEVELS = 8

COL_Q = 0
COL_ZA = 1024
COL_U = 2048
COL_ZS = 3072
COL_GA = 4096
COL_GS = 6144
COL_KV = 8192
IN_WIDTH = 8704
TN = 512
CAST_ROWS = 16
INPROJ_ROWS = 1024
INPROJ_HALVES = 2
MERGE_ROWS = 256
SAMPLE_ATTN_BATCH = 16
ATTN_BLOCKS_PER_STEP = 4
KV_TILE = COL_KV // TN
VMEM_LIMIT = 56 * 1024 * 1024
NEG = -1e30
LOG2E = 1.4426950408889634
Q_SCALE = HEAD_DIM ** -0.5 * LOG2E
NT_DIMS = (((1,), (1,)), ((), ()))


def _cmul(ar, ai, br, bi):
    return ar * br - ai * bi, ar * bi + ai * br


def _rmsnorm_kernel(x_ref, gain_ref, o_ref):
    x = x_ref[...]
    r = lax.rsqrt(jnp.mean(x * x, axis=-1, keepdims=True) + EPS)
    o_ref[...] = (x * r * gain_ref[...]).astype(BF16)


def _rmsnorm(x, gain, tm):
    m = x.shape[0]
    return pl.pallas_call(
        _rmsnorm_kernel,
        grid=(m // tm,),
        in_specs=[pl.BlockSpec((tm, D_MODEL), lambda i: (i, 0)),
                  pl.BlockSpec((1, D_MODEL), lambda i: (0, 0))],
        out_specs=pl.BlockSpec((tm, D_MODEL), lambda i: (i, 0)),
        out_shape=jax.ShapeDtypeStruct((m, D_MODEL), BF16),
        compiler_params=pltpu.CompilerParams(
            dimension_semantics=("arbitrary",), vmem_limit_bytes=VMEM_LIMIT),
        name="rmsnorm",
    )(x, gain)


def _inproj_epilogue(j, o_ref, qg_ref, kg_ref, e_ref, rope_ref):
    def head_norm_rope(n_chunks, gain_ref, scale):
        cos, sa, sb = (rope_ref[:, t * LANES:(t + 1) * LANES] for t in range(3))
        for c in range(n_chunks):
            cs = slice(c * LANES, (c + 1) * LANES)
            ac = o_ref[:, cs]
            sq = ac * ac
            hi = sq.astype(BF16)
            lo = (sq - hi.astype(F32)).astype(BF16)
            ms = jnp.dot(jnp.concatenate([hi, lo], axis=1), e_ref[...],
                         preferred_element_type=F32)
            xc = ac * lax.rsqrt(ms + EPS) * gain_ref[:, cs]
            o_ref[:, cs] = (xc * cos + pltpu.roll(xc, LANES - HEAD_DIM // 2, 1) * sa
                            + pltpu.roll(xc, HEAD_DIM // 2, 1) * sb) * scale

    @pl.when(j < COL_ZA // TN)
    def _():
        head_norm_rope(TN // LANES, qg_ref, Q_SCALE)

    @pl.when(j == KV_TILE)
    def _():
        head_norm_rope(KV_WIDTH // LANES, kg_ref, 1.0)


def _inproj_kernel(x_ref, gain_ref, xs_ref, w_ref, qg_ref, kg_ref, e_ref, rope_ref, rope_s_ref,
                   *rest, mw_slabs):
    n_mw = (len(rest) - 4) // 2
    mw_refs, (o_ref, os_ref) = rest[:n_mw], rest[n_mw:n_mw + 2]
    mw_out_refs, (xn_ref, wbf_ref) = rest[n_mw + 2:2 * n_mw + 2], rest[2 * n_mw + 2:]
    h, j, i = pl.program_id(0), pl.program_id(1), pl.program_id(2)
    tm = o_ref.shape[0]
    rows = pl.ds(pl.multiple_of(i * tm, tm), tm)

    @pl.when(j == 0)
    def _():
        x = x_ref[...]
        r = lax.rsqrt(jnp.mean(x * x, axis=-1, keepdims=True) + EPS)
        xn_ref[rows, :] = (x * r * gain_ref[...]).astype(BF16)

    @pl.when(i == 0)
    def _():
        wbf_ref[...] = w_ref[...].astype(BF16)

    o_ref[...] = jnp.dot(xn_ref[rows, :], wbf_ref[...], preferred_element_type=F32)
    _inproj_epilogue(j, o_ref, qg_ref, kg_ref, e_ref, rope_ref)

    step = (h * pl.num_programs(1) + j) * pl.num_programs(2) + i
    for src, dst, n_slabs in zip(mw_refs, mw_out_refs, mw_slabs):
        @pl.when(step < n_slabs)
        def _():
            dst[...] = src[...].astype(BF16)

    @pl.when(jnp.logical_and(h == 0, jnp.logical_and(j == 0, i == 0)))
    def _():
        os_ref[...] = jnp.zeros(os_ref.shape, F32)

    @pl.when(jnp.logical_and(h == pl.num_programs(0) - 1, i == pl.num_programs(2) - 1))
    def _():
        os_ref[...] = jnp.dot(xs_ref[...], wbf_ref[...], preferred_element_type=F32)
        _inproj_epilogue(j, os_ref, qg_ref, kg_ref, e_ref, rope_s_ref)


def _w_tile(j):
    n_q = COL_ZA // TN
    return jnp.where(j < n_q, j, jnp.where(j == KV_TILE, n_q, j + 1))


def _inproj(x, gain, xs, w, qg, kg, e, rope_p, rope_s, merge_weights, tm, halves):
    m, ms = x.shape[0], xs.shape[0]
    rows_half = m // halves
    tiles = rows_half // tm
    n_tab = rope_p.shape[0] // tm
    n_col = IN_WIDTH // TN
    const = lambda shape: pl.BlockSpec(shape, lambda h, j, i: (0, 0))
    n_steps = halves * n_col * tiles
    for mw in merge_weights:
        assert mw.shape[0] % CAST_ROWS == 0 and mw.shape[0] // CAST_ROWS <= n_steps
    slab = lambda mw: pl.BlockSpec(
        (CAST_ROWS, mw.shape[1]),
        lambda h, j, i: (jnp.minimum((h * n_col + j) * tiles + i, mw.shape[0] // CAST_ROWS - 1), 0))
    return pl.pallas_call(
        functools.partial(_inproj_kernel,
                          mw_slabs=tuple(mw.shape[0] // CAST_ROWS for mw in merge_weights)),
        grid=(halves, IN_WIDTH // TN, tiles),
        in_specs=[
            pl.BlockSpec((tm, D_MODEL),
                         lambda h, j, i: (h * tiles + jnp.where(j == 0, i, tiles - 1), 0)),
            const((1, D_MODEL)),
            const((ms, D_MODEL)),
            pl.BlockSpec((D_MODEL, TN), lambda h, j, i: (0, _w_tile(j))),
            const((1, TN)), const((1, KV_WIDTH)), const((2 * LANES, LANES)),
            pl.BlockSpec((tm, 3 * LANES), lambda h, j, i: ((h * tiles + i) % n_tab, 0)),
            const((ms, 3 * LANES)),
        ] + [slab(mw) for mw in merge_weights],
        out_specs=[pl.BlockSpec((tm, TN), lambda h, j, i: (h * tiles + i, j)),
                   pl.BlockSpec((ms, TN),
                                lambda h, j, i: (0, jnp.where(h == halves - 1, j, n_col)))
                   ] + [slab(mw) for mw in merge_weights],
        out_shape=[jax.ShapeDtypeStruct((m, IN_WIDTH), F32),
                   jax.ShapeDtypeStruct((ms, IN_WIDTH + TN), F32)
                   ] + [jax.ShapeDtypeStruct(mw.shape, BF16) for mw in merge_weights],
        scratch_shapes=[pltpu.VMEM((rows_half, D_MODEL), BF16), pltpu.VMEM((D_MODEL, TN), BF16)],
        compiler_params=pltpu.CompilerParams(
            dimension_semantics=("arbitrary", "arbitrary", "arbitrary"),
            vmem_limit_bytes=VMEM_LIMIT),
        name="inproj",
    )(x, gain, xs, w, qg, kg, e, rope_p, rope_s, *merge_weights)


def _dup_head(x, g, lo):
    r = pltpu.roll(x, HEAD_DIM, 1)
    return jnp.where(lo, x, r) if g % 2 == 0 else jnp.where(lo, r, x)


def _prompt_attn_kernel(sinks_ref, q_ref, kvc_ref, kvp_ref, kvm_ref, vmt_ref, bias_ref, o_ref):
    n_blk = q_ref.shape[0] // BLOCK
    lo = lax.broadcasted_iota(jnp.int32, (1, LANES), 1) < HEAD_DIM
    head_lane = lax.broadcasted_iota(jnp.int32, (1, 4 * BLOCK), 1) // BLOCK

    def prev_rows(blk, cols):
        if blk == 0:
            return kvp_ref[:, cols]
        return kvc_ref[(blk - 1) * BLOCK:blk * BLOCK, cols]

    scores = []
    for blk in range(n_blk):
        rows = slice(blk * BLOCK, (blk + 1) * BLOCK)
        for g in range(N_KV_HEADS):
            ks = slice((g // 2) * LANES, (g // 2 + 1) * LANES)
            k_all = jnp.concatenate(
                [_dup_head(prev_rows(blk, ks), g, lo), _dup_head(kvc_ref[rows, ks], g, lo),
                 _dup_head(kvm_ref[:, ks], g, lo)], axis=0).astype(BF16)
            qa = q_ref[rows, (2 * g) * LANES:(2 * g + 1) * LANES]
            qb = q_ref[rows, (2 * g + 1) * LANES:(2 * g + 2) * LANES]
            q4 = jnp.concatenate([jnp.where(lo, qa, 0.0), jnp.where(lo, 0.0, qa),
                                  jnp.where(lo, qb, 0.0), jnp.where(lo, 0.0, qb)],
                                 axis=0).astype(BF16)
            scores.append(lax.dot_general(k_all, q4, NT_DIMS, preferred_element_type=F32))
    for blk in range(n_blk):
        rows = slice(blk * BLOCK, (blk + 1) * BLOCK)
        bias = bias_ref[jnp.minimum(pl.program_id(1), 1)] if blk == 0 else bias_ref[1]
        vs = slice(KV_WIDTH, 2 * KV_WIDTH)
        v_t = jnp.concatenate([prev_rows(blk, vs).T, kvc_ref[rows, vs].T], axis=1)
        for g in range(N_KV_HEADS):
            s = scores[blk * N_KV_HEADS + g]
            s_w = s[:2 * BLOCK] + bias
            s_m = s[2 * BLOCK:]
            sink = LOG2E * jnp.where(head_lane == 0, sinks_ref[4 * g],
                                     jnp.where(head_lane == 1, sinks_ref[4 * g + 1],
                                               jnp.where(head_lane == 2, sinks_ref[4 * g + 2],
                                                         sinks_ref[4 * g + 3])))
            m = jnp.maximum(jnp.maximum(jnp.max(s_w, axis=0, keepdims=True),
                                        jnp.max(s_m, axis=0, keepdims=True)), sink)
            p_w = jnp.exp2(s_w - m)
            p_m = jnp.exp2(s_m - m)
            den = (jnp.sum(p_w, axis=0, keepdims=True) + jnp.sum(p_m, axis=0, keepdims=True)
                   + jnp.exp2(sink - m))
            inv = 1.0 / den
            p = jnp.concatenate([p_w * inv, p_m * inv], axis=0).astype(BF16)
            hs = slice(g * HEAD_DIM, (g + 1) * HEAD_DIM)
            vt_g = jnp.concatenate([v_t[hs], vmt_ref[hs, :]], axis=1).astype(BF16)
            o_t = jnp.dot(vt_g, p, preferred_element_type=F32)
            for pair in range(2):
                two = jnp.concatenate(
                    [o_t[:, (2 * pair) * BLOCK:(2 * pair + 1) * BLOCK],
                     o_t[:, (2 * pair + 1) * BLOCK:(2 * pair + 2) * BLOCK]], axis=0)
                o_ref[rows, (2 * g + pair) * LANES:(2 * g + pair + 1) * LANES] = two.T


def _prompt_attention(sinks, h_p, h_s, vm_t, batch, seq):
    nb = seq // BLOCK
    kvb = COL_KV // (2 * KV_WIDTH)
    kj = np.arange(2 * BLOCK)[:, None]
    qi = np.arange(4 * BLOCK)[None, :] % BLOCK
    cur = (kj >= BLOCK) & (kj - BLOCK <= qi)
    prev = (kj < BLOCK) & (kj > qi)
    bias = jnp.asarray(np.where(np.stack([cur, cur | prev]), 0.0, NEG), F32)
    per = ATTN_BLOCKS_PER_STEP
    ns = nb // per
    return pl.pallas_call(
        _prompt_attn_kernel,
        grid=(batch, ns),
        in_specs=[
            pl.BlockSpec(memory_space=pltpu.SMEM),
            pl.BlockSpec((per * BLOCK, ATTN_WIDTH), lambda b, i: (b * ns + i, 0)),
            pl.BlockSpec((per * BLOCK, 2 * KV_WIDTH), lambda b, i: (b * ns + i, kvb)),
            pl.BlockSpec((BLOCK, 2 * KV_WIDTH),
                         lambda b, i: (b * nb + jnp.maximum(i * per - 1, 0), kvb)),
            pl.BlockSpec((N_META, 2 * KV_WIDTH), lambda b, i: (h_s.shape[0] // N_META - 1, kvb)),
            pl.BlockSpec((KV_WIDTH, N_META), lambda b, i: (0, 0)),
            pl.BlockSpec((2, 2 * BLOCK, 4 * BLOCK), lambda b, i: (0, 0, 0)),
        ],
        out_specs=pl.BlockSpec((per * BLOCK, ATTN_WIDTH), lambda b, i: (b * ns + i, 0)),
        out_shape=jax.ShapeDtypeStruct((batch * seq, ATTN_WIDTH), F32),
        compiler_params=pltpu.CompilerParams(
            dimension_semantics=("arbitrary", "arbitrary"), vmem_limit_bytes=VMEM_LIMIT),
        name="prompt_attn",
    )(sinks, h_p, h_p, h_p, h_s, vm_t, bias)


def _sample_attn_kernel(qp_ref, kn_ref, vn_ref, knr_ref, vnr_ref, ck_ref, cv_ref, mk_ref, mv_ref,
                        sink_ref, o_ref, ok_ref, ov_ref):
    bt = ck_ref.shape[0]
    slot = lax.broadcasted_iota(jnp.int32, (1, 1, WINDOW), 2)
    round_bf16 = lambda a: a.astype(BF16).astype(F32)
    qp = qp_ref[...].astype(BF16)
    ck = ck_ref[...]
    cv = cv_ref[...]
    k_m = mk_ref[...].astype(BF16)
    v_m = mv_ref[...].astype(BF16)
    s_w = jnp.einsum("bhc,bcw->bhw", qp, ck.astype(BF16), preferred_element_type=F32)
    s_w = jnp.where(slot == 0, NEG, s_w)
    s_m = jnp.einsum("bhc,bkc->bhk", qp, k_m, preferred_element_type=F32)
    knr, vnr = round_bf16(knr_ref[...]), round_bf16(vnr_ref[...])
    qf = qp.astype(F32)
    s_n = jnp.stack([jnp.sum(qf[b] * knr[b:b + 1, :], axis=-1, keepdims=True)
                     for b in range(bt)])
    sink = LOG2E * sink_ref[...][None]
    m = jnp.maximum(jnp.maximum(jnp.max(s_w, axis=-1, keepdims=True),
                                jnp.max(s_m, axis=-1, keepdims=True)),
                    jnp.maximum(s_n, sink))
    p_w = jnp.exp2(s_w - m)
    p_m = jnp.exp2(s_m - m)
    p_n = jnp.exp2(s_n - m)
    den = (jnp.sum(p_w, axis=-1, keepdims=True) + jnp.sum(p_m, axis=-1, keepdims=True)
           + p_n + jnp.exp2(sink - m))
    inv = 1.0 / den
    o = (jnp.einsum("bhw,bcw->bhc", (p_w * inv).astype(BF16), cv.astype(BF16),
                    preferred_element_type=F32)
         + jnp.einsum("bhk,bkc->bhc", (p_m * inv).astype(BF16), v_m,
                      preferred_element_type=F32)
         + jnp.stack([round_bf16(p_n[b] * inv[b]) * vnr[b:b + 1, :] for b in range(bt)]))

    last = lax.broadcasted_iota(jnp.int32, (1, WINDOW), 1) == WINDOW - 1
    kn = kn_ref[0]
    vn = vn_ref[0]
    ck = pltpu.roll(ck, WINDOW - 1, 2)
    cv = pltpu.roll(cv, WINDOW - 1, 2)
    ok_ref[...] = jnp.stack([jnp.where(last, kn[:, b:b + 1], ck[b]) for b in range(bt)])
    ov_ref[...] = jnp.stack([jnp.where(last, vn[:, b:b + 1], cv[b]) for b in range(bt)])
    hh = lax.broadcasted_iota(jnp.int32, (1, N_Q_HEADS, KV_WIDTH), 1) // (N_Q_HEADS // N_KV_HEADS)
    cc = lax.broadcasted_iota(jnp.int32, (1, N_Q_HEADS, KV_WIDTH), 2) // HEAD_DIM
    o = jnp.where(hh == cc, o, 0.0)
    o = o[:, :, :LANES] + o[:, :, LANES:]
    o_ref[...] = o + pltpu.roll(o, HEAD_DIM, 2)


def _sample_attention(qp, kn_t, vn_t, h_s, cache_k, cache_v, meta_k, meta_v, sink_col, bt):
    n = cache_k.shape[0]
    kvb = COL_KV // KV_WIDTH
    win_spec = pl.BlockSpec((bt, KV_WIDTH, WINDOW), lambda t: (t, 0, 0))
    new_spec = pl.BlockSpec((1, KV_WIDTH, bt), lambda t: (t, 0, 0))
    meta_spec = pl.BlockSpec((bt, N_META, KV_WIDTH), lambda t: (t, 0, 0))
    return pl.pallas_call(
        _sample_attn_kernel,
        grid=(n // bt,),
        in_specs=[
            pl.BlockSpec((bt, N_Q_HEADS, KV_WIDTH), lambda t: (t, 0, 0)),
            new_spec, new_spec,
            pl.BlockSpec((bt, KV_WIDTH), lambda t: (t, kvb)),
            pl.BlockSpec((bt, KV_WIDTH), lambda t: (t, kvb + 1)),
            win_spec, win_spec, meta_spec, meta_spec,
            pl.BlockSpec((N_Q_HEADS, 1), lambda t: (0, 0)),
        ],
        out_specs=[
            pl.BlockSpec((bt, N_Q_HEADS, LANES), lambda t: (t, 0, 0)),
            win_spec, win_spec,
        ],
        out_shape=[
            jax.ShapeDtypeStruct((n, N_Q_HEADS, LANES), F32),
            jax.ShapeDtypeStruct(cache_k.shape, F32),
            jax.ShapeDtypeStruct(cache_v.shape, F32),
        ],
        compiler_params=pltpu.CompilerParams(
            dimension_semantics=("arbitrary",), vmem_limit_bytes=VMEM_LIMIT),
        name="sample_attn",
    )(qp, kn_t, vn_t, h_s, h_s, cache_k, cache_v, meta_k, meta_v, sink_col)


def _gelu_tanh(x):
    return 0.5 * x * (1.0 + jnp.tanh(0.7978845608028654 * (x + 0.044715 * (x * x * x))))


def _ssm_kernel(u_ref, um_ref, us_ref, hr_ref, hi_ref, t8_ref, win_ref, wst_ref, wc_ref,
                ar_ref, ai_ref, lr_ref, li_ref, pr_ref, pi_ref, d_ref,
                y_ref, sre_ref, sim_ref, ys_ref, nr_ref, ni_ref, xr_ref, xi_ref):
    dsk = d_ref[...]
    us = us_ref[...]
    bu = jnp.dot(us.astype(BF16), win_ref[0, (CHUNK - 1) * LANES:, :],
                 preferred_element_type=F32)
    l_r, l_i = lr_ref[0], li_ref[0]
    h_r, h_i = hr_ref[...], hi_ref[...]
    n_r = l_r * h_r - l_i * h_i + bu[:, :STATE_TILE]
    n_i = l_r * h_i + l_i * h_r + bu[:, STATE_TILE:]
    nr_ref[...] = n_r
    ni_ref[...] = n_i
    hn = jnp.concatenate([n_r, n_i], axis=1).astype(BF16)
    ys_ref[...] = _gelu_tanh(
        lax.dot_general(hn, wc_ref[0], NT_DIMS, preferred_element_type=F32) + dsk * us)

    batch = sre_ref.shape[0]
    nc = u_ref.shape[0] // CHUNK // batch
    win = win_ref[0]
    dsk_row = jnp.concatenate([dsk] * CHUNK, axis=1)

    def chunks_of(b):
        r0 = b * nc * CHUNK
        return jnp.concatenate(
            [u_ref[pl.ds(r0 + j, nc, stride=CHUNK), :] for j in range(CHUNK)], axis=1)

    n_mc = um_ref.shape[0] // CHUNK
    um = jnp.concatenate([um_ref[pl.ds(j, n_mc, stride=CHUNK), :] for j in range(CHUNK)], axis=1)
    gm = jnp.dot(um.astype(BF16), win, preferred_element_type=F32)
    a_r, a_i = ar_ref[0, 0:1, :], ai_ref[0, 0:1, :]
    s0r, s0i = gm[0:1, :STATE_TILE], gm[0:1, STATE_TILE:]
    for c in range(1, n_mc):
        s0r, s0i = (a_r * s0r - a_i * s0i + gm[c:c + 1, :STATE_TILE],
                    a_r * s0i + a_i * s0r + gm[c:c + 1, STATE_TILE:])

    n_t = nc // SUBLANES
    row = lax.broadcasted_iota(jnp.int32, (nc, 1), 0)
    first = row == 0
    in_tile = row % SUBLANES
    t_row = lax.broadcasted_iota(jnp.int32, (n_t, 1), 0)
    first_t = t_row == 0
    n_local = SUBLANES.bit_length() - 1
    w_r, w_i = pr_ref[0], pi_ref[0]

    def tile_totals(scr_ref, x):
        parts = []
        for c in range(STATE_TILE // LANES):
            scr_ref[c] = x[:, c * LANES:(c + 1) * LANES]
            parts.append(scr_ref[c, pl.ds(SUBLANES - 1, n_t, stride=SUBLANES), :])
        return jnp.concatenate(parts, axis=1)

    def shifted(x, d, keep):
        if d < SUBLANES:
            return jnp.where(keep, pltpu.roll(x, d, 0), 0.0)
        return jnp.concatenate([jnp.zeros((d, STATE_TILE), F32), x[:x.shape[0] - d]], axis=0)

    ucat = chunks_of(0)
    ub = ucat.astype(BF16)
    g = jnp.dot(ub, win, preferred_element_type=F32)
    for b in range(batch):
        y_intra = jnp.dot(ub, t8_ref[0], preferred_element_type=F32) + dsk_row * ucat
        if b + 1 < batch:
            ucat_next = chunks_of(b + 1)
            ub_next = ucat_next.astype(BF16)
            g_next = jnp.dot(ub_next, win, preferred_element_type=F32)
        gr, gi = g[:, :STATE_TILE], g[:, STATE_TILE:]
        for k in range(n_local):
            d = 1 << k
            k_r, k_i = ar_ref[0, k:k + 1, :], ai_ref[0, k:k + 1, :]
            sh_r, sh_i = shifted(gr, d, in_tile >= d), shifted(gi, d, in_tile >= d)
            gr, gi = gr + (k_r * sh_r - k_i * sh_i), gi + (k_r * sh_i + k_i * sh_r)
        tot_r, tot_i = tile_totals(xr_ref, gr), tile_totals(xi_ref, gi)
        k_r, k_i = ar_ref[0, n_local:n_local + 1, :], ai_ref[0, n_local:n_local + 1, :]
        tot_r = tot_r + jnp.where(first_t, k_r * s0r - k_i * s0i, 0.0)
        tot_i = tot_i + jnp.where(first_t, k_r * s0i + k_i * s0r, 0.0)
        for k in range(SCAN_LEVELS - n_local):
            d = 1 << k
            if d >= n_t:
                break
            k_r = ar_ref[0, n_local + k:n_local + k + 1, :]
            k_i = ai_ref[0, n_local + k:n_local + k + 1, :]
            sh_r, sh_i = shifted(tot_r, d, t_row >= d), shifted(tot_i, d, t_row >= d)
            tot_r, tot_i = tot_r + (k_r * sh_r - k_i * sh_i), tot_i + (k_r * sh_i + k_i * sh_r)
        sre_ref[b, 0] = tot_r[n_t - 1:n_t, :]
        sim_ref[b, 0] = tot_i[n_t - 1:n_t, :]
        c_r = jnp.where(first_t, s0r, pltpu.roll(tot_r, 1, 0))
        c_i = jnp.where(first_t, s0i, pltpu.roll(tot_i, 1, 0))
        full_r, full_i = [], []
        for t in range(n_t):
            rs = slice(t * SUBLANES, (t + 1) * SUBLANES)
            e_r, e_i = c_r[t:t + 1, :], c_i[t:t + 1, :]
            full_r.append(gr[rs] + (w_r * e_r - w_i * e_i))
            full_i.append(gi[rs] + (w_r * e_i + w_i * e_r))
        gr, gi = jnp.concatenate(full_r, axis=0), jnp.concatenate(full_i, axis=0)
        sp = jnp.concatenate([jnp.where(first, s0r, pltpu.roll(gr, 1, 0)),
                              jnp.where(first, s0i, pltpu.roll(gi, 1, 0))],
                             axis=1).astype(BF16)
        y2 = y_intra + lax.dot_general(sp, wst_ref[0], NT_DIMS, preferred_element_type=F32)
        for j in range(CHUNK):
            sl = slice(j * LANES, (j + 1) * LANES)
            y_ref[pl.ds(b * nc * CHUNK + j, nc, stride=CHUNK), :] = _gelu_tanh(y2[:, sl])
        if b + 1 < batch:
            ucat, ub, g = ucat_next, ub_next, g_next


def _ssm(h_p, h_s, h0r, h0i, t8, win, wst, wc, a_r, a_i, l_r, l_i, p_r, p_i, d_skip, batch, n_s):
    ub = COL_U // LANES
    rows = h_p.shape[0]
    wspec = pl.BlockSpec((1, CHUNK_WIDTH, CHUNK_WIDTH), lambda gt: (gt, 0, 0))
    aspec = pl.BlockSpec((1, SCAN_LEVELS, STATE_TILE), lambda gt: (gt, 0, 0))
    sspec = pl.BlockSpec((batch, 1, 1, STATE_TILE), lambda gt: (0, gt, 0, 0))
    st = pl.BlockSpec((n_s, STATE_TILE), lambda gt: (0, gt))
    lam = pl.BlockSpec((1, 1, STATE_TILE), lambda gt: (gt, 0, 0))
    return pl.pallas_call(
        _ssm_kernel,
        grid=(N_GROUP_TILES,),
        in_specs=[
            pl.BlockSpec((rows, LANES), lambda gt: (0, ub + gt)),
            pl.BlockSpec((N_META, LANES), lambda gt: (h_s.shape[0] // N_META - 1, ub + gt)),
            pl.BlockSpec((n_s, LANES), lambda gt: (0, ub + gt)),
            st, st,
            wspec, wspec, wspec,
            pl.BlockSpec((1, LANES, 2 * STATE_TILE), lambda gt: (gt, 0, 0)),
            aspec, aspec, lam, lam,
            pl.BlockSpec((1, SUBLANES, STATE_TILE), lambda gt: (gt, 0, 0)),
            pl.BlockSpec((1, SUBLANES, STATE_TILE), lambda gt: (gt, 0, 0)),
            pl.BlockSpec((1, LANES), lambda gt: (0, gt)),
        ],
        out_specs=[pl.BlockSpec((rows, LANES), lambda gt: (0, gt)), sspec, sspec,
                   pl.BlockSpec((n_s, LANES), lambda gt: (0, gt)), st, st],
        out_shape=[
            jax.ShapeDtypeStruct((rows, SSM_WIDTH), F32),
            jax.ShapeDtypeStruct((batch, N_GROUP_TILES, 1, STATE_TILE), F32),
            jax.ShapeDtypeStruct((batch, N_GROUP_TILES, 1, STATE_TILE), F32),
            jax.ShapeDtypeStruct((n_s, SSM_WIDTH), F32),
            jax.ShapeDtypeStruct((n_s, N_SSM_GROUPS * SSM_STATE), F32),
            jax.ShapeDtypeStruct((n_s, N_SSM_GROUPS * SSM_STATE), F32),
        ],
        scratch_shapes=[pltpu.VMEM((STATE_TILE // LANES, rows // CHUNK // batch, LANES), F32)] * 2,
        compiler_params=pltpu.CompilerParams(
            dimension_semantics=("arbitrary",), vmem_limit_bytes=VMEM_LIMIT),
        name="ssm",
    )(h_p, h_s, h_s, h0r, h0i, t8, win, wst, wc, a_r, a_i, l_r, l_i, p_r, p_i, d_skip)


def _merge_kernel(o_ref, za_ref, zs_ref, ga_ref, gs_ref, y_ref, x_ref,
                  o2_ref, za2_ref, zs2_ref, ga2_ref, gs2_ref, y2_ref, x2_ref,
                  bglu_ref, wglu_ref, wa_ref, ws_ref, wo_ref, out_ref, out2_ref):
    def merge(o_r, za_r, zs_r, ga_r, gs_r, y_r, x_r, out_r):
        za = za_r[...]
        a_in = (o_r[...] * (za * jax.nn.sigmoid(za))).astype(BF16)
        gated_a = jax.nn.sigmoid(ga_r[...]) * jnp.dot(a_in, wa_ref[...],
                                                      preferred_element_type=F32)
        y = y_r[...]
        t = jnp.dot(y.astype(BF16), wglu_ref[...], preferred_element_type=F32) + bglu_ref[...]
        y = y * jax.nn.sigmoid(t)
        zs = zs_r[...]
        s_in = (y * (zs * jax.nn.sigmoid(zs))).astype(BF16)
        br_s = jnp.dot(s_in, ws_ref[...], preferred_element_type=F32)
        mix = (gated_a + jax.nn.sigmoid(gs_r[...]) * br_s).astype(BF16)
        out_r[...] = x_r[...] + jnp.dot(mix, wo_ref[...], preferred_element_type=F32)

    merge(o_ref, za_ref, zs_ref, ga_ref, gs_ref, y_ref, x_ref, out_ref)

    @pl.when(pl.program_id(0) == pl.num_programs(0) - 1)
    def _():
        merge(o2_ref, za2_ref, zs2_ref, ga2_ref, gs2_ref, y2_ref, x2_ref, out2_ref)


def _merge(o, h, y, x, o2, h2, y2, x2, b_glu, w_glu, w_a, w_s, w_o, tm):
    m, m2 = x.shape[0], x2.shape[0]
    const = lambda shape: pl.BlockSpec(shape, lambda i: (0, 0), pipeline_mode=pl.Buffered(1))
    rows = lambda width, col: pl.BlockSpec((tm, width), lambda i: (i, col))
    rows2 = lambda width, col: pl.BlockSpec((m2, width), lambda i: (0, col),
                                            pipeline_mode=pl.Buffered(1))
    operands = lambda spec: [
        spec(ATTN_WIDTH, 0), spec(ATTN_WIDTH, COL_ZA // ATTN_WIDTH),
        spec(SSM_WIDTH, COL_ZS // SSM_WIDTH), spec(D_MODEL, COL_GA // D_MODEL),
        spec(D_MODEL, COL_GS // D_MODEL), spec(SSM_WIDTH, 0), spec(D_MODEL, 0)]
    return pl.pallas_call(
        _merge_kernel,
        grid=(m // tm,),
        in_specs=operands(rows) + operands(rows2) + [
            const((1, SSM_WIDTH)),
            const((SSM_WIDTH, SSM_WIDTH)),
            const((ATTN_WIDTH, D_MODEL)),
            const((SSM_WIDTH, D_MODEL)),
            const((D_MODEL, D_MODEL)),
        ],
        out_specs=[pl.BlockSpec((tm, D_MODEL), lambda i: (i, 0)),
                   pl.BlockSpec((m2, D_MODEL), lambda i: (0, 0))],
        out_shape=[jax.ShapeDtypeStruct((m, D_MODEL), F32),
                   jax.ShapeDtypeStruct((m2, D_MODEL), F32)],
        compiler_params=pltpu.CompilerParams(
            dimension_semantics=("arbitrary",), vmem_limit_bytes=VMEM_LIMIT),
        name="merge",
    )(o, h, h, h, h, y, x, o2, h2, h2, h2, h2, y2, x2, b_glu, w_glu, w_a, w_s, w_o)


def _rope_tables(pos):
    half = HEAD_DIM // 2
    inv_freq = (ROPE_THETA ** (-np.arange(half, dtype=np.float32) / half)).astype(np.float32)
    ang = np.asarray(pos, np.float32)[:, None] * inv_freq[None, :]
    cos, sin = np.cos(ang).astype(np.float32), np.sin(ang).astype(np.float32)
    zero = np.zeros_like(sin)
    reps = LANES // HEAD_DIM
    cos_t = np.tile(np.concatenate([cos, cos], axis=1), (1, reps))
    sin_a = np.tile(np.concatenate([-sin, zero], axis=1), (1, reps))
    sin_b = np.tile(np.concatenate([zero, sin], axis=1), (1, reps))
    return jnp.asarray(np.concatenate([cos_t, sin_a, sin_b], axis=1))


def _split_bf16(x):
    hi = x.astype(BF16)
    return hi, (x - hi.astype(F32)).astype(BF16)


def _dot_nt_f32(a, b):
    ah, al = _split_bf16(a)
    bh, bl = _split_bf16(b)
    dot = lambda x, y: lax.dot_general(x, y, NT_DIMS, preferred_element_type=F32)
    return dot(ah, bh) + dot(ah, bl) + dot(al, bh)


def _ssm_tables_kernel(are_ref, aim_ref, ldt_ref, bre_ref, bim_ref, cre_ref, cim_ref,
                       win_ref, wst_ref, t8_ref, wc_ref, ar_ref, ai_ref, lr_ref, li_ref,
                       pr_ref, pi_ref):
    a_r, a_i = are_ref[0], aim_ref[0]
    dt = jnp.exp(ldt_ref[0])
    mag = jnp.exp(a_r * dt)
    l_r, l_i = mag * jnp.cos(a_i * dt), mag * jnp.sin(a_i * dt)
    den = a_r * a_r + a_i * a_i
    f_r = ((l_r - 1.0) * a_r + l_i * a_i) / den
    f_i = (l_i * a_r - (l_r - 1.0) * a_i) / den
    shape = (LANES, STATE_TILE)
    own = (lax.broadcasted_iota(jnp.int32, shape, 0) // SSM_GROUP
           == lax.broadcasted_iota(jnp.int32, shape, 1) // SSM_STATE)
    sq = (LANES, LANES)
    same_group = (lax.broadcasted_iota(jnp.int32, sq, 0) // SSM_GROUP
                  == lax.broadcasted_iota(jnp.int32, sq, 1) // SSM_GROUP)
    lo = lax.broadcasted_iota(jnp.int32, (1, LANES), 1) < SSM_STATE
    n_rep = STATE_TILE // LANES

    def expand(x):
        return jnp.where(own, jnp.concatenate([x] * n_rep, axis=1), 0.0)

    def compact(v):
        e = jnp.where(own, jnp.broadcast_to(v, shape), 0.0)
        s = e[:, :LANES]
        for k in range(1, n_rep):
            s = s + e[:, k * LANES:(k + 1) * LANES]
        return s + pltpu.roll(s, SSM_STATE, 1)

    lc_r, lc_i = compact(l_r), compact(l_i)
    b_r, b_i = bre_ref[0], bim_ref[0]
    c_r, c_i = cre_ref[0], cim_ref[0]
    bb_r, bb_i = _cmul(compact(f_r), compact(f_i), b_r, b_i)
    wc_ref[0] = jnp.concatenate([expand(c_r), expand(-c_i)], axis=1).astype(BF16)
    c_cat = jnp.where(lo, c_r, -c_i)
    q_r, q_i = jnp.ones(sq, F32), jnp.zeros(sq, F32)
    for n in range(CHUNK):
        x_r, x_i = _cmul(q_r, q_i, bb_r, bb_i)
        win_ref[0, (CHUNK - 1 - n) * LANES:(CHUNK - n) * LANES, :] = jnp.concatenate(
            [expand(x_r), expand(x_i)], axis=1).astype(BF16)
        k_n = _dot_nt_f32(jnp.where(lo, x_r, x_i), c_cat)
        k_n = jnp.where(same_group, k_n, 0.0).astype(BF16)
        for j in range(CHUNK - n):
            t8_ref[0, j * LANES:(j + 1) * LANES, (j + n) * LANES:(j + n + 1) * LANES] = k_n
            if n > 0:
                t8_ref[0, (j + n) * LANES:(j + n + 1) * LANES, j * LANES:(j + 1) * LANES] = (
                    jnp.zeros(sq, BF16))
        q_r, q_i = _cmul(q_r, q_i, lc_r, lc_i)
        s_r, s_i = _cmul(c_r, c_i, q_r, q_i)
        wst_ref[0, n * LANES:(n + 1) * LANES, :] = jnp.concatenate(
            [expand(s_r), expand(-s_i)], axis=1).astype(BF16)
    p_r, p_i = l_r, l_i
    for _ in range(CHUNK.bit_length() - 1):
        p_r, p_i = _cmul(p_r, p_i, p_r, p_i)
    t_r, t_i = p_r, p_i
    for i in range(SUBLANES):
        pr_ref[0, i:i + 1, :] = t_r
        pi_ref[0, i:i + 1, :] = t_i
        t_r, t_i = _cmul(t_r, t_i, p_r, p_i)
    for k in range(SCAN_LEVELS):
        ar_ref[0, k:k + 1, :] = p_r
        ai_ref[0, k:k + 1, :] = p_i
        p_r, p_i = _cmul(p_r, p_i, p_r, p_i)
    lr_ref[0] = l_r
    li_ref[0] = l_i


def _ssm_tables(a_re, a_im, log_dt, b_re, b_im, c_re, c_im):
    gt = N_GROUP_TILES
    lane_vec = lambda v: v.reshape(gt, 1, STATE_TILE)
    dup = lambda v: jnp.concatenate([v, v], axis=-1)
    rows = GROUPS_PER_TILE * SSM_GROUP
    b_t = lambda v: dup(v.transpose(0, 2, 1).reshape(gt, rows, SSM_STATE))
    c_t = lambda v: dup(v.reshape(gt, rows, SSM_STATE))
    vec = pl.BlockSpec((1, 1, STATE_TILE), lambda g: (g, 0, 0))
    par = pl.BlockSpec((1, rows, LANES), lambda g: (g, 0, 0))
    big = pl.BlockSpec((1, CHUNK_WIDTH, CHUNK_WIDTH), lambda g: (g, 0, 0))
    wcs = pl.BlockSpec((1, LANES, 2 * STATE_TILE), lambda g: (g, 0, 0))
    scan = pl.BlockSpec((1, SCAN_LEVELS, STATE_TILE), lambda g: (g, 0, 0))
    tile = pl.BlockSpec((1, SUBLANES, STATE_TILE), lambda g: (g, 0, 0))
    return pl.pallas_call(
        _ssm_tables_kernel,
        grid=(gt,),
        in_specs=[vec, vec, vec, par, par, par, par],
        out_specs=[big, big, big, wcs, scan, scan, vec, vec, tile, tile],
        out_shape=[
            jax.ShapeDtypeStruct((gt, CHUNK_WIDTH, 2 * STATE_TILE), BF16),
            jax.ShapeDtypeStruct((gt, CHUNK_WIDTH, 2 * STATE_TILE), BF16),
            jax.ShapeDtypeStruct((gt, CHUNK_WIDTH, CHUNK_WIDTH), BF16),
            jax.ShapeDtypeStruct((gt, LANES, 2 * STATE_TILE), BF16),
            jax.ShapeDtypeStruct((gt, SCAN_LEVELS, STATE_TILE), F32),
            jax.ShapeDtypeStruct((gt, SCAN_LEVELS, STATE_TILE), F32),
            jax.ShapeDtypeStruct((gt, 1, STATE_TILE), F32),
            jax.ShapeDtypeStruct((gt, 1, STATE_TILE), F32),
            jax.ShapeDtypeStruct((gt, SUBLANES, STATE_TILE), F32),
            jax.ShapeDtypeStruct((gt, SUBLANES, STATE_TILE), F32),
        ],
        compiler_params=pltpu.CompilerParams(
            dimension_semantics=("arbitrary",), vmem_limit_bytes=VMEM_LIMIT),
        name="ssm_tables",
    )(lane_vec(a_re), lane_vec(a_im), lane_vec(jnp.repeat(log_dt, SSM_STATE)),
      b_t(b_re), b_t(b_im), c_t(c_re), c_t(c_im))


def kernel(x_prompt, x_sample, cache_win_k, cache_win_v, cache_meta_k, cache_meta_v,
           state_ssm_re, state_ssm_im, meta_tokens, norm_gain, w_in, q_norm_gain, k_norm_gain,
           sinks, a_re, a_im, log_dt, b_re, b_im, c_re, c_im, d_skip, w_glu, b_glu,
           w_attn_out, w_ssm_out, w_out):
    depth = w_in.shape[0]
    assert depth == 1, "single-layer trunk"
    batch, seq = x_prompt.shape[:2]
    n_s = x_sample.shape[0]
    assert x_sample.shape[1] == 1 and seq % (CHUNK * BLOCK) == 0
    l = 0

    w = w_in[l]
    qg = jnp.tile(q_norm_gain[l], TN // HEAD_DIM)[None]
    kg = jnp.tile(k_norm_gain[l], KV_WIDTH // HEAD_DIM)[None]
    lane = np.arange(LANES)
    e128 = np.where((lane[:, None] // HEAD_DIM) == (lane[None, :] // HEAD_DIM),
                    1.0 / HEAD_DIM, 0.0).astype(np.float32)
    e = jnp.asarray(np.concatenate([e128, e128], axis=0), BF16)
    gain = norm_gain[l][None]

    pos_p = N_META + np.arange(seq)
    pos_small = np.concatenate([np.full((n_s,), PAST_LEN), np.arange(N_META)])
    x_small = jnp.concatenate([x_sample[:, 0, :], meta_tokens.astype(x_prompt.dtype)], axis=0)
    xp = x_prompt.reshape(batch * seq, D_MODEL)
    xn_s = _rmsnorm(x_small, gain, tm=n_s + N_META)
    h_p, h_s, w_glu_bf, w_a_bf, w_s_bf, w_o_bf = _inproj(
        xp, gain, xn_s, w, qg, kg, e, _rope_tables(pos_p), _rope_tables(pos_small),
        (w_glu[l], w_attn_out[l], w_ssm_out[l], w_out[l]), tm=INPROJ_ROWS, halves=INPROJ_HALVES)

    win, wst, t8, wc, a_r, a_i, l_r, l_i, p_r, p_i = _ssm_tables(
        a_re[l], a_im[l], log_dt[l], b_re[l], b_im[l], c_re[l], c_im[l])
    dsk = d_skip[l][None]

    vm_t = h_s[n_s:, COL_KV + KV_WIDTH:IN_WIDTH].T
    o_p = _prompt_attention(sinks[l], h_p, h_s, vm_t, batch, seq)
    q_s = h_s[:n_s, COL_Q:COL_Q + ATTN_WIDTH].reshape(n_s, N_Q_HEADS, 1, HEAD_DIM)
    sel = (np.arange(N_Q_HEADS)[:, None] // (N_Q_HEADS // N_KV_HEADS)
           == np.arange(N_KV_HEADS)[None, :]).astype(np.float32)
    qp = (q_s * sel[None, :, :, None]).reshape(n_s, N_Q_HEADS, KV_WIDTH)
    bt = SAMPLE_ATTN_BATCH
    to_t = lambda c: c.transpose(0, 2, 3, 1).reshape(n_s, KV_WIDTH, WINDOW)
    from_t = lambda c: c.reshape(n_s, N_KV_HEADS, HEAD_DIM, WINDOW).transpose(0, 3, 1, 2)[None]
    new_t = lambda c0: h_s[:n_s, c0:c0 + KV_WIDTH].reshape(n_s // bt, bt, KV_WIDTH).transpose(0, 2, 1)
    o_s, s_win_k, s_win_v = _sample_attention(
        qp, new_t(COL_KV), new_t(COL_KV + KV_WIDTH), h_s, to_t(cache_win_k[l]), to_t(cache_win_v[l]),
        cache_meta_k[l].reshape(n_s, N_META, KV_WIDTH), cache_meta_v[l].reshape(n_s, N_META, KV_WIDTH),
        sinks[l][:, None], bt=bt)
    o_s = o_s[:, :, :HEAD_DIM].reshape(n_s, ATTN_WIDTH)

    y_p, p_re, p_im, y_s, s_re, s_im = _ssm(
        h_p, h_s, state_ssm_re[l].reshape(n_s, -1), state_ssm_im[l].reshape(n_s, -1),
        t8, win, wst, wc, a_r, a_i, l_r, l_i, p_r, p_i, dsk, batch, n_s)
    y_prompt, y_sample = _merge(
        o_p, h_p, y_p, xp, o_s, h_s, y_s, x_sample[:, 0, :], b_glu[l][None],
        w_glu_bf, w_a_bf, w_s_bf, w_o_bf, tm=MERGE_ROWS)
    y_prompt = y_prompt.reshape(batch, seq, D_MODEL)
    y_sample = y_sample.reshape(n_s, 1, D_MODEL)

    kv_p = h_p.reshape(batch, seq, IN_WIDTH)[:, seq - WINDOW:, COL_KV:]
    kv_p = kv_p.reshape(batch, WINDOW, 2, N_KV_HEADS, HEAD_DIM)
    p_win_k = kv_p[:, :, 0][None]
    p_win_v = kv_p[:, :, 1][None]
    kv_m = h_s[n_s:, COL_KV:IN_WIDTH].reshape(N_META, 2, N_KV_HEADS, HEAD_DIM)
    p_meta_k = jnp.broadcast_to(kv_m[None, :, 0], (batch, N_META, N_KV_HEADS, HEAD_DIM))[None]
    p_meta_v = jnp.broadcast_to(kv_m[None, :, 1], (batch, N_META, N_KV_HEADS, HEAD_DIM))[None]
    p_ssm_re = p_re.reshape(batch, N_SSM_GROUPS, SSM_STATE)[None]
    p_ssm_im = p_im.reshape(batch, N_SSM_GROUPS, SSM_STATE)[None]
    st_shape = (1, n_s, N_SSM_GROUPS, SSM_STATE)
    return (y_prompt, y_sample, p_win_k, p_win_v, p_meta_k, p_meta_v, p_ssm_re, p_ssm_im,
            from_t(s_win_k), from_t(s_win_v), s_re.reshape(st_shape), s_im.reshape(st_shape))
```

```python
import functools

import jax
import jax.numpy as jnp
import numpy as np
from jax import lax
from jax.experimental import pallas as pl
from jax.experimental.pallas import tpu as pltpu

F32 = jnp.float32
BF16 = jnp.bfloat16

D_MODEL = 2048
N_META = 16
HEAD_DIM = 64
N_Q_HEADS = 16
N_KV_HEADS = 4
ATTN_WIDTH = N_Q_HEADS * HEAD_DIM
KV_WIDTH = N_KV_HEADS * HEAD_DIM
WINDOW = 128
BLOCK = 128
ROPE_THETA = 10000.0
SSM_WIDTH = D_MODEL // 2
SSM_GROUP = 16
N_SSM_GROUPS = SSM_WIDTH // SSM_GROUP
SSM_STATE = 64
EPS = 1e-6
PAST_LEN = 8192

LANES = 128
SUBLANES = 8
CHUNK = 8
GROUPS_PER_TILE = LANES // SSM_GROUP
N_GROUP_TILES = N_SSM_GROUPS // GROUPS_PER_TILE
CHUNK_WIDTH = CHUNK * LANES
STATE_TILE = GROUPS_PER_TILE * SSM_STATE
SCAN_LEVELS = 8

COL_Q = 0
COL_ZA = 1024
COL_U = 2048
COL_ZS = 3072
COL_GA = 4096
COL_GS = 6144
COL_KV = 8192
IN_WIDTH = 8704
TN = 512
CAST_ROWS = 16
INPROJ_ROWS = 1024
INPROJ_HALVES = 2
MERGE_ROWS = 256
SAMPLE_ATTN_BATCH = 16
ATTN_BLOCKS_PER_STEP = 8
KV_TILE = COL_KV // TN
VMEM_LIMIT = 56 * 1024 * 1024
NEG = -1e30
LOG2E = 1.4426950408889634
Q_SCALE = HEAD_DIM ** -0.5 * LOG2E
NT_DIMS = (((1,), (1,)), ((), ()))


def _cmul(ar, ai, br, bi):
    return ar * br - ai * bi, ar * bi + ai * br


def _rmsnorm_kernel(x_ref, gain_ref, o_ref):
    x = x_ref[...]
    r = lax.rsqrt(jnp.mean(x * x, axis=-1, keepdims=True) + EPS)
    o_ref[...] = (x * r * gain_ref[...]).astype(BF16)


def _rmsnorm(x, gain, tm):
    m = x.shape[0]
    return pl.pallas_call(
        _rmsnorm_kernel,
        grid=(m // tm,),
        in_specs=[pl.BlockSpec((tm, D_MODEL), lambda i: (i, 0)),
                  pl.BlockSpec((1, D_MODEL), lambda i: (0, 0))],
        out_specs=pl.BlockSpec((tm, D_MODEL), lambda i: (i, 0)),
        out_shape=jax.ShapeDtypeStruct((m, D_MODEL), BF16),
        compiler_params=pltpu.CompilerParams(
            dimension_semantics=("arbitrary",), vmem_limit_bytes=VMEM_LIMIT),
        name="rmsnorm",
    )(x, gain)


def _inproj_epilogue(j, o_ref, qg_ref, kg_ref, e_ref, rope_ref):
    def head_norm_rope(n_chunks, gain_ref, scale):
        cos, sa, sb = (rope_ref[:, t * LANES:(t + 1) * LANES] for t in range(3))
        for c in range(n_chunks):
            cs = slice(c * LANES, (c + 1) * LANES)
            ac = o_ref[:, cs]
            sq = ac * ac
            hi = sq.astype(BF16)
            lo = (sq - hi.astype(F32)).astype(BF16)
            ms = jnp.dot(jnp.concatenate([hi, lo], axis=1), e_ref[...],
                         preferred_element_type=F32)
            xc = ac * lax.rsqrt(ms + EPS) * gain_ref[:, cs]
            o_ref[:, cs] = (xc * cos + pltpu.roll(xc, LANES - HEAD_DIM // 2, 1) * sa
                            + pltpu.roll(xc, HEAD_DIM // 2, 1) * sb) * scale

    @pl.when(j < COL_ZA // TN)
    def _():
        head_norm_rope(TN // LANES, qg_ref, Q_SCALE)

    @pl.when(j == KV_TILE)
    def _():
        head_norm_rope(KV_WIDTH // LANES, kg_ref, 1.0)


def _inproj_kernel(x_ref, gain_ref, xs_ref, w_ref, qg_ref, kg_ref, e_ref, rope_ref, rope_s_ref,
                   *rest, mw_slabs):
    n_mw = (len(rest) - 4) // 2
    mw_refs, (o_ref, os_ref) = rest[:n_mw], rest[n_mw:n_mw + 2]
    mw_out_refs, (xn_ref, wbf_ref) = rest[n_mw + 2:2 * n_mw + 2], rest[2 * n_mw + 2:]
    h, j, i = pl.program_id(0), pl.program_id(1), pl.program_id(2)
    tm = o_ref.shape[0]
    rows = pl.ds(pl.multiple_of(i * tm, tm), tm)

    @pl.when(j == 0)
    def _():
        x = x_ref[...]
        r = lax.rsqrt(jnp.mean(x * x, axis=-1, keepdims=True) + EPS)
        xn_ref[rows, :] = (x * r * gain_ref[...]).astype(BF16)

    @pl.when(i == 0)
    def _():
        wbf_ref[...] = w_ref[...].astype(BF16)

    o_ref[...] = jnp.dot(xn_ref[rows, :], wbf_ref[...], preferred_element_type=F32)
    _inproj_epilogue(j, o_ref, qg_ref, kg_ref, e_ref, rope_ref)

    step = (h * pl.num_programs(1) + j) * pl.num_programs(2) + i
    for src, dst, n_slabs in zip(mw_refs, mw_out_refs, mw_slabs):
        @pl.when(step < n_slabs)
        def _():
            dst[...] = src[...].astype(BF16)

    @pl.when(jnp.logical_and(h == 0, jnp.logical_and(j == 0, i == 0)))
    def _():
        os_ref[...] = jnp.zeros(os_ref.shape, F32)

    @pl.when(jnp.logical_and(h == pl.num_programs(0) - 1, i == pl.num_programs(2) - 1))
    def _():
        os_ref[...] = jnp.dot(xs_ref[...], wbf_ref[...], preferred_element_type=F32)
        _inproj_epilogue(j, os_ref, qg_ref, kg_ref, e_ref, rope_s_ref)


def _w_tile(j):
    n_q = COL_ZA // TN
    return jnp.where(j < n_q, j, jnp.where(j == KV_TILE, n_q, j + 1))


def _inproj(x, gain, xs, w, qg, kg, e, rope_p, rope_s, merge_weights, tm, halves):
    m, ms = x.shape[0], xs.shape[0]
    rows_half = m // halves
    tiles = rows_half // tm
    n_tab = rope_p.shape[0] // tm
    n_col = IN_WIDTH // TN
    const = lambda shape: pl.BlockSpec(shape, lambda h, j, i: (0, 0))
    n_steps = halves * n_col * tiles
    for mw in merge_weights:
        assert mw.shape[0] % CAST_ROWS == 0 and mw.shape[0] // CAST_ROWS <= n_steps
    slab = lambda mw: pl.BlockSpec(
        (CAST_ROWS, mw.shape[1]),
        lambda h, j, i: (jnp.minimum((h * n_col + j) * tiles + i, mw.shape[0] // CAST_ROWS - 1), 0))
    return pl.pallas_call(
        functools.partial(_inproj_kernel,
                          mw_slabs=tuple(mw.shape[0] // CAST_ROWS for mw in merge_weights)),
        grid=(halves, IN_WIDTH // TN, tiles),
        in_specs=[
            pl.BlockSpec((tm, D_MODEL),
                         lambda h, j, i: (h * tiles + jnp.where(j == 0, i, tiles - 1), 0)),
            const((1, D_MODEL)),
            const((ms, D_MODEL)),
            pl.BlockSpec((D_MODEL, TN), lambda h, j, i: (0, _w_tile(j))),
            const((1, TN)), const((1, KV_WIDTH)), const((2 * LANES, LANES)),
            pl.BlockSpec((tm, 3 * LANES), lambda h, j, i: ((h * tiles + i) % n_tab, 0)),
            const((ms, 3 * LANES)),
        ] + [slab(mw) for mw in merge_weights],
        out_specs=[pl.BlockSpec((tm, TN), lambda h, j, i: (h * tiles + i, j)),
                   pl.BlockSpec((ms, TN),
                                lambda h, j, i: (0, jnp.where(h == halves - 1, j, n_col)))
                   ] + [slab(mw) for mw in merge_weights],
        out_shape=[jax.ShapeDtypeStruct((m, IN_WIDTH), F32),
                   jax.ShapeDtypeStruct((ms, IN_WIDTH + TN), F32)
                   ] + [jax.ShapeDtypeStruct(mw.shape, BF16) for mw in merge_weights],
        scratch_shapes=[pltpu.VMEM((rows_half, D_MODEL), BF16), pltpu.VMEM((D_MODEL, TN), BF16)],
        compiler_params=pltpu.CompilerParams(
            dimension_semantics=("arbitrary", "arbitrary", "arbitrary"),
            vmem_limit_bytes=VMEM_LIMIT),
        name="inproj",
    )(x, gain, xs, w, qg, kg, e, rope_p, rope_s, *merge_weights)


def _dup_head(x, g, lo):
    r = pltpu.roll(x, HEAD_DIM, 1)
    return jnp.where(lo, x, r) if g % 2 == 0 else jnp.where(lo, r, x)


def _prompt_attn_kernel(sinks_ref, q_ref, kvc_ref, kvp_ref, kvm_ref, vmt_ref, bias_ref, o_ref):
    n_blk = q_ref.shape[0] // BLOCK
    lo = lax.broadcasted_iota(jnp.int32, (1, LANES), 1) < HEAD_DIM
    head_lane = lax.broadcasted_iota(jnp.int32, (1, 4 * BLOCK), 1) // BLOCK

    def prev_rows(blk, cols):
        if blk == 0:
            return kvp_ref[:, cols]
        return kvc_ref[(blk - 1) * BLOCK:blk * BLOCK, cols]

    scores = []
    for blk in range(n_blk):
        rows = slice(blk * BLOCK, (blk + 1) * BLOCK)
        for g in range(N_KV_HEADS):
            ks = slice((g // 2) * LANES, (g // 2 + 1) * LANES)
            k_all = jnp.concatenate(
                [_dup_head(prev_rows(blk, ks), g, lo), _dup_head(kvc_ref[rows, ks], g, lo),
                 _dup_head(kvm_ref[:, ks], g, lo)], axis=0).astype(BF16)
            qa = q_ref[rows, (2 * g) * LANES:(2 * g + 1) * LANES]
            qb = q_ref[rows, (2 * g + 1) * LANES:(2 * g + 2) * LANES]
            q4 = jnp.concatenate([jnp.where(lo, qa, 0.0), jnp.where(lo, 0.0, qa),
                                  jnp.where(lo, qb, 0.0), jnp.where(lo, 0.0, qb)],
                                 axis=0).astype(BF16)
            scores.append(lax.dot_general(k_all, q4, NT_DIMS, preferred_element_type=F32))
    for blk in range(n_blk):
        rows = slice(blk * BLOCK, (blk + 1) * BLOCK)
        bias = bias_ref[jnp.minimum(pl.program_id(1), 1)] if blk == 0 else bias_ref[1]
        vs = slice(KV_WIDTH, 2 * KV_WIDTH)
        v_t = jnp.concatenate([prev_rows(blk, vs).T, kvc_ref[rows, vs].T], axis=1)
        for g in range(N_KV_HEADS):
            s = scores[blk * N_KV_HEADS + g]
            s_w = s[:2 * BLOCK] + bias
            s_m = s[2 * BLOCK:]
            sink = LOG2E * jnp.where(head_lane == 0, sinks_ref[4 * g],
                                     jnp.where(head_lane == 1, sinks_ref[4 * g + 1],
                                               jnp.where(head_lane == 2, sinks_ref[4 * g + 2],
                                                         sinks_ref[4 * g + 3])))
            m = jnp.maximum(jnp.maximum(jnp.max(s_w, axis=0, keepdims=True),
                                        jnp.max(s_m, axis=0, keepdims=True)), sink)
            p_w = jnp.exp2(s_w - m)
            p_m = jnp.exp2(s_m - m)
            den = (jnp.sum(p_w, axis=0, keepdims=True) + jnp.sum(p_m, axis=0, keepdims=True)
                   + jnp.exp2(sink - m))
            p = jnp.concatenate([p_w, p_m], axis=0).astype(BF16)
            hs = slice(g * HEAD_DIM, (g + 1) * HEAD_DIM)
            vt_g = jnp.concatenate([v_t[hs], vmt_ref[hs, :]], axis=1).astype(BF16)
            o_t = jnp.dot(vt_g, p, preferred_element_type=F32) * (1.0 / den)
            for pair in range(2):
                two = jnp.concatenate(
                    [o_t[:, (2 * pair) * BLOCK:(2 * pair + 1) * BLOCK],
                     o_t[:, (2 * pair + 1) * BLOCK:(2 * pair + 2) * BLOCK]], axis=0)
                o_ref[rows, (2 * g + pair) * LANES:(2 * g + pair + 1) * LANES] = two.T


def _prompt_attention(sinks, h_p, h_s, vm_t, batch, seq):
    nb = seq // BLOCK
    kvb = COL_KV // (2 * KV_WIDTH)
    kj = np.arange(2 * BLOCK)[:, None]
    qi = np.arange(4 * BLOCK)[None, :] % BLOCK
    cur = (kj >= BLOCK) & (kj - BLOCK <= qi)
    prev = (kj < BLOCK) & (kj > qi)
    bias = jnp.asarray(np.where(np.stack([cur, cur | prev]), 0.0, NEG), F32)
    per = ATTN_BLOCKS_PER_STEP
    ns = nb // per
    return pl.pallas_call(
        _prompt_attn_kernel,
        grid=(batch, ns),
        in_specs=[
            pl.BlockSpec(memory_space=pltpu.SMEM),
            pl.BlockSpec((per * BLOCK, ATTN_WIDTH), lambda b, i: (b * ns + i, 0)),
            pl.BlockSpec((per * BLOCK, 2 * KV_WIDTH), lambda b, i: (b * ns + i, kvb)),
            pl.BlockSpec((BLOCK, 2 * KV_WIDTH),
                         lambda b, i: (b * nb + jnp.maximum(i * per - 1, 0), kvb)),
            pl.BlockSpec((N_META, 2 * KV_WIDTH), lambda b, i: (h_s.shape[0] // N_META - 1, kvb)),
            pl.BlockSpec((KV_WIDTH, N_META), lambda b, i: (0, 0)),
            pl.BlockSpec((2, 2 * BLOCK, 4 * BLOCK), lambda b, i: (0, 0, 0)),
        ],
        out_specs=pl.BlockSpec((per * BLOCK, ATTN_WIDTH), lambda b, i: (b * ns + i, 0)),
        out_shape=jax.ShapeDtypeStruct((batch * seq, ATTN_WIDTH), F32),
        compiler_params=pltpu.CompilerParams(
            dimension_semantics=("arbitrary", "arbitrary"), vmem_limit_bytes=VMEM_LIMIT),
        name="prompt_attn",
    )(sinks, h_p, h_p, h_p, h_s, vm_t, bias)


def _sample_attn_kernel(qp_ref, kn_ref, vn_ref, knr_ref, vnr_ref, ck_ref, cv_ref, mk_ref, mv_ref,
                        sink_ref, o_ref, ok_ref, ov_ref):
    bt = ck_ref.shape[0]
    slot = lax.broadcasted_iota(jnp.int32, (1, 1, WINDOW), 2)
    round_bf16 = lambda a: a.astype(BF16).astype(F32)
    qp = qp_ref[...].astype(BF16)
    ck = ck_ref[...]
    cv = cv_ref[...]
    k_m = mk_ref[...].astype(BF16)
    v_m = mv_ref[...].astype(BF16)
    s_w = jnp.einsum("bhc,bcw->bhw", qp, ck.astype(BF16), preferred_element_type=F32)
    s_w = jnp.where(slot == 0, NEG, s_w)
    s_m = jnp.einsum("bhc,bkc->bhk", qp, k_m, preferred_element_type=F32)
    knr, vnr = round_bf16(knr_ref[...]), round_bf16(vnr_ref[...])
    qf = qp.astype(F32)
    s_n = jnp.stack([jnp.sum(qf[b] * knr[b:b + 1, :], axis=-1, keepdims=True)
                     for b in range(bt)])
    sink = LOG2E * sink_ref[...][None]
    m = jnp.maximum(jnp.maximum(jnp.max(s_w, axis=-1, keepdims=True),
                                jnp.max(s_m, axis=-1, keepdims=True)),
                    jnp.maximum(s_n, sink))
    p_w = jnp.exp2(s_w - m)
    p_m = jnp.exp2(s_m - m)
    p_n = jnp.exp2(s_n - m)
    den = (jnp.sum(p_w, axis=-1, keepdims=True) + jnp.sum(p_m, axis=-1, keepdims=True)
           + p_n + jnp.exp2(sink - m))
    inv = 1.0 / den
    o = (jnp.einsum("bhw,bcw->bhc", (p_w * inv).astype(BF16), cv.astype(BF16),
                    preferred_element_type=F32)
         + jnp.einsum("bhk,bkc->bhc", (p_m * inv).astype(BF16), v_m,
                      preferred_element_type=F32)
         + jnp.stack([round_bf16(p_n[b] * inv[b]) * vnr[b:b + 1, :] for b in range(bt)]))

    last = lax.broadcasted_iota(jnp.int32, (1, WINDOW), 1) == WINDOW - 1
    kn = kn_ref[0]
    vn = vn_ref[0]
    ck = pltpu.roll(ck, WINDOW - 1, 2)
    cv = pltpu.roll(cv, WINDOW - 1, 2)
    ok_ref[...] = jnp.stack([jnp.where(last, kn[:, b:b + 1], ck[b]) for b in range(bt)])
    ov_ref[...] = jnp.stack([jnp.where(last, vn[:, b:b + 1], cv[b]) for b in range(bt)])
    hh = lax.broadcasted_iota(jnp.int32, (1, N_Q_HEADS, KV_WIDTH), 1) // (N_Q_HEADS // N_KV_HEADS)
    cc = lax.broadcasted_iota(jnp.int32, (1, N_Q_HEADS, KV_WIDTH), 2) // HEAD_DIM
    o = jnp.where(hh == cc, o, 0.0)
    o = o[:, :, :LANES] + o[:, :, LANES:]
    o_ref[...] = o + pltpu.roll(o, HEAD_DIM, 2)


def _sample_attention(qp, kn_t, vn_t, h_s, cache_k, cache_v, meta_k, meta_v, sink_col, bt):
    n = cache_k.shape[0]
    kvb = COL_KV // KV_WIDTH
    win_spec = pl.BlockSpec((bt, KV_WIDTH, WINDOW), lambda t: (t, 0, 0))
    new_spec = pl.BlockSpec((1, KV_WIDTH, bt), lambda t: (t, 0, 0))
    meta_spec = pl.BlockSpec((bt, N_META, KV_WIDTH), lambda t: (t, 0, 0))
    return pl.pallas_call(
        _sample_attn_kernel,
        grid=(n // bt,),
        in_specs=[
            pl.BlockSpec((bt, N_Q_HEADS, KV_WIDTH), lambda t: (t, 0, 0)),
            new_spec, new_spec,
            pl.BlockSpec((bt, KV_WIDTH), lambda t: (t, kvb)),
            pl.BlockSpec((bt, KV_WIDTH), lambda t: (t, kvb + 1)),
            win_spec, win_spec, meta_spec, meta_spec,
            pl.BlockSpec((N_Q_HEADS, 1), lambda t: (0, 0)),
        ],
        out_specs=[
            pl.BlockSpec((bt, N_Q_HEADS, LANES), lambda t: (t, 0, 0)),
            win_spec, win_spec,
        ],
        out_shape=[
            jax.ShapeDtypeStruct((n, N_Q_HEADS, LANES), F32),
            jax.ShapeDtypeStruct(cache_k.shape, F32),
            jax.ShapeDtypeStruct(cache_v.shape, F32),
        ],
        compiler_params=pltpu.CompilerParams(
            dimension_semantics=("arbitrary",), vmem_limit_bytes=VMEM_LIMIT),
        name="sample_attn",
    )(qp, kn_t, vn_t, h_s, h_s, cache_k, cache_v, meta_k, meta_v, sink_col)


def _gelu_tanh(x):
    return 0.5 * x * (1.0 + jnp.tanh(0.7978845608028654 * (x + 0.044715 * (x * x * x))))


def _ssm_kernel(u_ref, um_ref, us_ref, hr_ref, hi_ref, t8_ref, win_ref, wst_ref, wc_ref,
                ar_ref, ai_ref, lr_ref, li_ref, pr_ref, pi_ref, d_ref,
                y_ref, sre_ref, sim_ref, ys_ref, nr_ref, ni_ref, xr_ref, xi_ref):
    dsk = d_ref[...]
    us = us_ref[...]
    bu = jnp.dot(us.astype(BF16), win_ref[0, (CHUNK - 1) * LANES:, :],
                 preferred_element_type=F32)
    l_r, l_i = lr_ref[0], li_ref[0]
    h_r, h_i = hr_ref[...], hi_ref[...]
    n_r = l_r * h_r - l_i * h_i + bu[:, :STATE_TILE]
    n_i = l_r * h_i + l_i * h_r + bu[:, STATE_TILE:]
    nr_ref[...] = n_r
    ni_ref[...] = n_i
    hn = jnp.concatenate([n_r, n_i], axis=1).astype(BF16)
    ys_ref[...] = _gelu_tanh(
        lax.dot_general(hn, wc_ref[0], NT_DIMS, preferred_element_type=F32) + dsk * us)

    batch = sre_ref.shape[0]
    nc = u_ref.shape[0] // CHUNK // batch
    win = win_ref[0]
    dsk_row = jnp.concatenate([dsk] * CHUNK, axis=1)

    def chunks_of(b):
        r0 = b * nc * CHUNK
        return jnp.concatenate(
            [u_ref[pl.ds(r0 + j, nc, stride=CHUNK), :] for j in range(CHUNK)], axis=1)

    n_mc = um_ref.shape[0] // CHUNK
    um = jnp.concatenate([um_ref[pl.ds(j, n_mc, stride=CHUNK), :] for j in range(CHUNK)], axis=1)
    gm = jnp.dot(um.astype(BF16), win, preferred_element_type=F32)
    a_r, a_i = ar_ref[0, 0:1, :], ai_ref[0, 0:1, :]
    s0r, s0i = gm[0:1, :STATE_TILE], gm[0:1, STATE_TILE:]
    for c in range(1, n_mc):
        s0r, s0i = (a_r * s0r - a_i * s0i + gm[c:c + 1, :STATE_TILE],
                    a_r * s0i + a_i * s0r + gm[c:c + 1, STATE_TILE:])

    n_t = nc // SUBLANES
    row = lax.broadcasted_iota(jnp.int32, (nc, 1), 0)
    first = row == 0
    in_tile = row % SUBLANES
    t_row = lax.broadcasted_iota(jnp.int32, (n_t, 1), 0)
    first_t = t_row == 0
    n_local = SUBLANES.bit_length() - 1
    w_r, w_i = pr_ref[0], pi_ref[0]

    def tile_totals(scr_ref, x):
        parts = []
        for c in range(STATE_TILE // LANES):
            scr_ref[c] = x[:, c * LANES:(c + 1) * LANES]
            parts.append(scr_ref[c, pl.ds(SUBLANES - 1, n_t, stride=SUBLANES), :])
        return jnp.concatenate(parts, axis=1)

    def shifted(x, d, keep):
        if d < SUBLANES:
            return jnp.where(keep, pltpu.roll(x, d, 0), 0.0)
        return jnp.concatenate([jnp.zeros((d, STATE_TILE), F32), x[:x.shape[0] - d]], axis=0)

    ucat = chunks_of(0)
    ub = ucat.astype(BF16)
    g = jnp.dot(ub, win, preferred_element_type=F32)
    for b in range(batch):
        y_intra = jnp.dot(ub, t8_ref[0], preferred_element_type=F32) + dsk_row * ucat
        if b + 1 < batch:
            ucat_next = chunks_of(b + 1)
            ub_next = ucat_next.astype(BF16)
            g_next = jnp.dot(ub_next, win, preferred_element_type=F32)
        gr, gi = g[:, :STATE_TILE], g[:, STATE_TILE:]
        for k in range(n_local):
            d = 1 << k
            k_r, k_i = ar_ref[0, k:k + 1, :], ai_ref[0, k:k + 1, :]
            sh_r, sh_i = shifted(gr, d, in_tile >= d), shifted(gi, d, in_tile >= d)
            gr, gi = gr + (k_r * sh_r - k_i * sh_i), gi + (k_r * sh_i + k_i * sh_r)
        tot_r, tot_i = tile_totals(xr_ref, gr), tile_totals(xi_ref, gi)
        k_r, k_i = ar_ref[0, n_local:n_local + 1, :], ai_ref[0, n_local:n_local + 1, :]
        tot_r = tot_r + jnp.where(first_t, k_r * s0r - k_i * s0i, 0.0)
        tot_i = tot_i + jnp.where(first_t, k_r * s0i + k_i * s0r, 0.0)
        for k in range(SCAN_LEVELS - n_local):
            d = 1 << k
            if d >= n_t:
                break
            k_r = ar_ref[0, n_local + k:n_local + k + 1, :]
            k_i = ai_ref[0, n_local + k:n_local + k + 1, :]
            sh_r, sh_i = shifted(tot_r, d, t_row >= d), shifted(tot_i, d, t_row >= d)
            tot_r, tot_i = tot_r + (k_r * sh_r - k_i * sh_i), tot_i + (k_r * sh_i + k_i * sh_r)
        sre_ref[b, 0] = tot_r[n_t - 1:n_t, :]
        sim_ref[b, 0] = tot_i[n_t - 1:n_t, :]
        c_r = jnp.where(first_t, s0r, pltpu.roll(tot_r, 1, 0))
        c_i = jnp.where(first_t, s0i, pltpu.roll(tot_i, 1, 0))
        full_r, full_i = [], []
        for t in range(n_t):
            rs = slice(t * SUBLANES, (t + 1) * SUBLANES)
            e_r, e_i = c_r[t:t + 1, :], c_i[t:t + 1, :]
            full_r.append(gr[rs] + (w_r * e_r - w_i * e_i))
            full_i.append(gi[rs] + (w_r * e_i + w_i * e_r))
        gr, gi = jnp.concatenate(full_r, axis=0), jnp.concatenate(full_i, axis=0)
        sp = jnp.concatenate([jnp.where(first, s0r, pltpu.roll(gr, 1, 0)),
                              jnp.where(first, s0i, pltpu.roll(gi, 1, 0))],
                             axis=1).astype(BF16)
        y2 = y_intra + lax.dot_general(sp, wst_ref[0], NT_DIMS, preferred_element_type=F32)
        for j in range(CHUNK):
            sl = slice(j * LANES, (j + 1) * LANES)
            y_ref[pl.ds(b * nc * CHUNK + j, nc, stride=CHUNK), :] = _gelu_tanh(y2[:, sl])
        if b + 1 < batch:
            ucat, ub, g = ucat_next, ub_next, g_next


def _ssm(h_p, h_s, h0r, h0i, t8, win, wst, wc, a_r, a_i, l_r, l_i, p_r, p_i, d_skip, batch, n_s):
    ub = COL_U // LANES
    rows = h_p.shape[0]
    wspec = pl.BlockSpec((1, CHUNK_WIDTH, CHUNK_WIDTH), lambda gt: (gt, 0, 0))
    aspec = pl.BlockSpec((1, SCAN_LEVELS, STATE_TILE), lambda gt: (gt, 0, 0))
    sspec = pl.BlockSpec((batch, 1, 1, STATE_TILE), lambda gt: (0, gt, 0, 0))
    st = pl.BlockSpec((n_s, STATE_TILE), lambda gt: (0, gt))
    lam = pl.BlockSpec((1, 1, STATE_TILE), lambda gt: (gt, 0, 0))
    return pl.pallas_call(
        _ssm_kernel,
        grid=(N_GROUP_TILES,),
        in_specs=[
            pl.BlockSpec((rows, LANES), lambda gt: (0, ub + gt)),
            pl.BlockSpec((N_META, LANES), lambda gt: (h_s.shape[0] // N_META - 1, ub + gt)),
            pl.BlockSpec((n_s, LANES), lambda gt: (0, ub + gt)),
            st, st,
            wspec, wspec, wspec,
            pl.BlockSpec((1, LANES, 2 * STATE_TILE), lambda gt: (gt, 0, 0)),
            aspec, aspec, lam, lam,
            pl.BlockSpec((1, SUBLANES, STATE_TILE), lambda gt: (gt, 0, 0)),
            pl.BlockSpec((1, SUBLANES, STATE_TILE), lambda gt: (gt, 0, 0)),
            pl.BlockSpec((1, LANES), lambda gt: (0, gt)),
        ],
        out_specs=[pl.BlockSpec((rows, LANES), lambda gt: (0, gt)), sspec, sspec,
                   pl.BlockSpec((n_s, LANES), lambda gt: (0, gt)), st, st],
        out_shape=[
            jax.ShapeDtypeStruct((rows, SSM_WIDTH), F32),
            jax.ShapeDtypeStruct((batch, N_GROUP_TILES, 1, STATE_TILE), F32),
            jax.ShapeDtypeStruct((batch, N_GROUP_TILES, 1, STATE_TILE), F32),
            jax.ShapeDtypeStruct((n_s, SSM_WIDTH), F32),
            jax.ShapeDtypeStruct((n_s, N_SSM_GROUPS * SSM_STATE), F32),
            jax.ShapeDtypeStruct((n_s, N_SSM_GROUPS * SSM_STATE), F32),
        ],
        scratch_shapes=[pltpu.VMEM((STATE_TILE // LANES, rows // CHUNK // batch, LANES), F32)] * 2,
        compiler_params=pltpu.CompilerParams(
            dimension_semantics=("arbitrary",), vmem_limit_bytes=VMEM_LIMIT),
        name="ssm",
    )(h_p, h_s, h_s, h0r, h0i, t8, win, wst, wc, a_r, a_i, l_r, l_i, p_r, p_i, d_skip)


def _merge_kernel(o_ref, za_ref, zs_ref, ga_ref, gs_ref, y_ref, x_ref,
                  o2_ref, za2_ref, zs2_ref, ga2_ref, gs2_ref, y2_ref, x2_ref,
                  bglu_ref, wglu_ref, wa_ref, ws_ref, wo_ref, out_ref, out2_ref):
    def merge(o_r, za_r, zs_r, ga_r, gs_r, y_r, x_r, out_r):
        za = za_r[...]
        a_in = (o_r[...] * (za * jax.nn.sigmoid(za))).astype(BF16)
        gated_a = jax.nn.sigmoid(ga_r[...]) * jnp.dot(a_in, wa_ref[...],
                                                      preferred_element_type=F32)
        y = y_r[...]
        t = jnp.dot(y.astype(BF16), wglu_ref[...], preferred_element_type=F32) + bglu_ref[...]
        y = y * jax.nn.sigmoid(t)
        zs = zs_r[...]
        s_in = (y * (zs * jax.nn.sigmoid(zs))).astype(BF16)
        br_s = jnp.dot(s_in, ws_ref[...], preferred_element_type=F32)
        mix = (gated_a + jax.nn.sigmoid(gs_r[...]) * br_s).astype(BF16)
        out_r[...] = x_r[...] + jnp.dot(mix, wo_ref[...], preferred_element_type=F32)

    merge(o_ref, za_ref, zs_ref, ga_ref, gs_ref, y_ref, x_ref, out_ref)

    @pl.when(pl.program_id(0) == pl.num_programs(0) - 1)
    def _():
        merge(o2_ref, za2_ref, zs2_ref, ga2_ref, gs2_ref, y2_ref, x2_ref, out2_ref)


def _merge(o, h, y, x, o2, h2, y2, x2, b_glu, w_glu, w_a, w_s, w_o, tm):
    m, m2 = x.shape[0], x2.shape[0]
    const = lambda shape: pl.BlockSpec(shape, lambda i: (0, 0), pipeline_mode=pl.Buffered(1))
    rows = lambda width, col: pl.BlockSpec((tm, width), lambda i: (i, col))
    rows2 = lambda width, col: pl.BlockSpec((m2, width), lambda i: (0, col),
                                            pipeline_mode=pl.Buffered(1))
    operands = lambda spec: [
        spec(ATTN_WIDTH, 0), spec(ATTN_WIDTH, COL_ZA // ATTN_WIDTH),
        spec(SSM_WIDTH, COL_ZS // SSM_WIDTH), spec(D_MODEL, COL_GA // D_MODEL),
        spec(D_MODEL, COL_GS // D_MODEL), spec(SSM_WIDTH, 0), spec(D_MODEL, 0)]
    return pl.pallas_call(
        _merge_kernel,
        grid=(m // tm,),
        in_specs=operands(rows) + operands(rows2) + [
            const((1, SSM_WIDTH)),
            const((SSM_WIDTH, SSM_WIDTH)),
            const((ATTN_WIDTH, D_MODEL)),
            const((SSM_WIDTH, D_MODEL)),
            const((D_MODEL, D_MODEL)),
        ],
        out_specs=[pl.BlockSpec((tm, D_MODEL), lambda i: (i, 0)),
                   pl.BlockSpec((m2, D_MODEL), lambda i: (0, 0))],
        out_shape=[jax.ShapeDtypeStruct((m, D_MODEL), F32),
                   jax.ShapeDtypeStruct((m2, D_MODEL), F32)],
        compiler_params=pltpu.CompilerParams(
            dimension_semantics=("arbitrary",), vmem_limit_bytes=VMEM_LIMIT),
        name="merge",
    )(o, h, h, h, h, y, x, o2, h2, h2, h2, h2, y2, x2, b_glu, w_glu, w_a, w_s, w_o)


def _rope_tables(pos):
    half = HEAD_DIM // 2
    inv_freq = (ROPE_THETA ** (-np.arange(half, dtype=np.float32) / half)).astype(np.float32)
    ang = np.asarray(pos, np.float32)[:, None] * inv_freq[None, :]
    cos, sin = np.cos(ang).astype(np.float32), np.sin(ang).astype(np.float32)
    zero = np.zeros_like(sin)
    reps = LANES // HEAD_DIM
    cos_t = np.tile(np.concatenate([cos, cos], axis=1), (1, reps))
    sin_a = np.tile(np.concatenate([-sin, zero], axis=1), (1, reps))
    sin_b = np.tile(np.concatenate([zero, sin], axis=1), (1, reps))
    return jnp.asarray(np.concatenate([cos_t, sin_a, sin_b], axis=1))


def _split_bf16(x):
    hi = x.astype(BF16)
    return hi, (x - hi.astype(F32)).astype(BF16)


def _dot_nt_f32(a, b):
    ah, al = _split_bf16(a)
    bh, bl = _split_bf16(b)
    dot = lambda x, y: lax.dot_general(x, y, NT_DIMS, preferred_element_type=F32)
    return dot(ah, bh) + dot(ah, bl) + dot(al, bh)


def _ssm_tables_kernel(are_ref, aim_ref, ldt_ref, bre_ref, bim_ref, cre_ref, cim_ref,
                       win_ref, wst_ref, t8_ref, wc_ref, ar_ref, ai_ref, lr_ref, li_ref,
                       pr_ref, pi_ref):
    a_r, a_i = are_ref[0], aim_ref[0]
    dt = jnp.exp(ldt_ref[0])
    mag = jnp.exp(a_r * dt)
    l_r, l_i = mag * jnp.cos(a_i * dt), mag * jnp.sin(a_i * dt)
    den = a_r * a_r + a_i * a_i
    f_r = ((l_r - 1.0) * a_r + l_i * a_i) / den
    f_i = (l_i * a_r - (l_r - 1.0) * a_i) / den
    shape = (LANES, STATE_TILE)
    own = (lax.broadcasted_iota(jnp.int32, shape, 0) // SSM_GROUP
           == lax.broadcasted_iota(jnp.int32, shape, 1) // SSM_STATE)
    sq = (LANES, LANES)
    same_group = (lax.broadcasted_iota(jnp.int32, sq, 0) // SSM_GROUP
                  == lax.broadcasted_iota(jnp.int32, sq, 1) // SSM_GROUP)
    lo = lax.broadcasted_iota(jnp.int32, (1, LANES), 1) < SSM_STATE
    n_rep = STATE_TILE // LANES

    def expand(x):
        return jnp.where(own, jnp.concatenate([x] * n_rep, axis=1), 0.0)

    def compact(v):
        e = jnp.where(own, jnp.broadcast_to(v, shape), 0.0)
        s = e[:, :LANES]
        for k in range(1, n_rep):
            s = s + e[:, k * LANES:(k + 1) * LANES]
        return s + pltpu.roll(s, SSM_STATE, 1)

    lc_r, lc_i = compact(l_r), compact(l_i)
    b_r, b_i = bre_ref[0], bim_ref[0]
    c_r, c_i = cre_ref[0], cim_ref[0]
    bb_r, bb_i = _cmul(compact(f_r), compact(f_i), b_r, b_i)
    wc_ref[0] = jnp.concatenate([expand(c_r), expand(-c_i)], axis=1).astype(BF16)
    c_cat = jnp.where(lo, c_r, -c_i)
    q_r, q_i = jnp.ones(sq, F32), jnp.zeros(sq, F32)
    for n in range(CHUNK):
        x_r, x_i = _cmul(q_r, q_i, bb_r, bb_i)
        win_ref[0, (CHUNK - 1 - n) * LANES:(CHUNK - n) * LANES, :] = jnp.concatenate(
            [expand(x_r), expand(x_i)], axis=1).astype(BF16)
        k_n = _dot_nt_f32(jnp.where(lo, x_r, x_i), c_cat)
        k_n = jnp.where(same_group, k_n, 0.0).astype(BF16)
        for j in range(CHUNK - n):
            t8_ref[0, j * LANES:(j + 1) * LANES, (j + n) * LANES:(j + n + 1) * LANES] = k_n
            if n > 0:
                t8_ref[0, (j + n) * LANES:(j + n + 1) * LANES, j * LANES:(j + 1) * LANES] = (
                    jnp.zeros(sq, BF16))
        q_r, q_i = _cmul(q_r, q_i, lc_r, lc_i)
        s_r, s_i = _cmul(c_r, c_i, q_r, q_i)
        wst_ref[0, n * LANES:(n + 1) * LANES, :] = jnp.concatenate(
            [expand(s_r), expand(-s_i)], axis=1).astype(BF16)
    p_r, p_i = l_r, l_i
    for _ in range(CHUNK.bit_length() - 1):
        p_r, p_i = _cmul(p_r, p_i, p_r, p_i)
    t_r, t_i = p_r, p_i
    for i in range(SUBLANES):
        pr_ref[0, i:i + 1, :] = t_r
        pi_ref[0, i:i + 1, :] = t_i
        t_r, t_i = _cmul(t_r, t_i, p_r, p_i)
    for k in range(SCAN_LEVELS):
        ar_ref[0, k:k + 1, :] = p_r
        ai_ref[0, k:k + 1, :] = p_i
        p_r, p_i = _cmul(p_r, p_i, p_r, p_i)
    lr_ref[0] = l_r
    li_ref[0] = l_i


def _ssm_tables(a_re, a_im, log_dt, b_re, b_im, c_re, c_im):
    gt = N_GROUP_TILES
    lane_vec = lambda v: v.reshape(gt, 1, STATE_TILE)
    dup = lambda v: jnp.concatenate([v, v], axis=-1)
    rows = GROUPS_PER_TILE * SSM_GROUP
    b_t = lambda v: dup(v.transpose(0, 2, 1).reshape(gt, rows, SSM_STATE))
    c_t = lambda v: dup(v.reshape(gt, rows, SSM_STATE))
    vec = pl.BlockSpec((1, 1, STATE_TILE), lambda g: (g, 0, 0))
    par = pl.BlockSpec((1, rows, LANES), lambda g: (g, 0, 0))
    big = pl.BlockSpec((1, CHUNK_WIDTH, CHUNK_WIDTH), lambda g: (g, 0, 0))
    wcs = pl.BlockSpec((1, LANES, 2 * STATE_TILE), lambda g: (g, 0, 0))
    scan = pl.BlockSpec((1, SCAN_LEVELS, STATE_TILE), lambda g: (g, 0, 0))
    tile = pl.BlockSpec((1, SUBLANES, STATE_TILE), lambda g: (g, 0, 0))
    return pl.pallas_call(
        _ssm_tables_kernel,
        grid=(gt,),
        in_specs=[vec, vec, vec, par, par, par, par],
        out_specs=[big, big, big, wcs, scan, scan, vec, vec, tile, tile],
        out_shape=[
            jax.ShapeDtypeStruct((gt, CHUNK_WIDTH, 2 * STATE_TILE), BF16),
            jax.ShapeDtypeStruct((gt, CHUNK_WIDTH, 2 * STATE_TILE), BF16),
            jax.ShapeDtypeStruct((gt, CHUNK_WIDTH, CHUNK_WIDTH), BF16),
            jax.ShapeDtypeStruct((gt, LANES, 2 * STATE_TILE), BF16),
            jax.ShapeDtypeStruct((gt, SCAN_LEVELS, STATE_TILE), F32),
            jax.ShapeDtypeStruct((gt, SCAN_LEVELS, STATE_TILE), F32),
            jax.ShapeDtypeStruct((gt, 1, STATE_TILE), F32),
            jax.ShapeDtypeStruct((gt, 1, STATE_TILE), F32),
            jax.ShapeDtypeStruct((gt, SUBLANES, STATE_TILE), F32),
            jax.ShapeDtypeStruct((gt, SUBLANES, STATE_TILE), F32),
        ],
        compiler_params=pltpu.CompilerParams(
            dimension_semantics=("arbitrary",), vmem_limit_bytes=VMEM_LIMIT),
        name="ssm_tables",
    )(lane_vec(a_re), lane_vec(a_im), lane_vec(jnp.repeat(log_dt, SSM_STATE)),
      b_t(b_re), b_t(b_im), c_t(c_re), c_t(c_im))


def kernel(x_prompt, x_sample, cache_win_k, cache_win_v, cache_meta_k, cache_meta_v,
           state_ssm_re, state_ssm_im, meta_tokens, norm_gain, w_in, q_norm_gain, k_norm_gain,
           sinks, a_re, a_im, log_dt, b_re, b_im, c_re, c_im, d_skip, w_glu, b_glu,
           w_attn_out, w_ssm_out, w_out):
    depth = w_in.shape[0]
    assert depth == 1, "single-layer trunk"
    batch, seq = x_prompt.shape[:2]
    n_s = x_sample.shape[0]
    assert x_sample.shape[1] == 1 and seq % (CHUNK * BLOCK) == 0
    l = 0

    w = w_in[l]
    qg = jnp.tile(q_norm_gain[l], TN // HEAD_DIM)[None]
    kg = jnp.tile(k_norm_gain[l], KV_WIDTH // HEAD_DIM)[None]
    lane = np.arange(LANES)
    e128 = np.where((lane[:, None] // HEAD_DIM) == (lane[None, :] // HEAD_DIM),
                    1.0 / HEAD_DIM, 0.0).astype(np.float32)
    e = jnp.asarray(np.concatenate([e128, e128], axis=0), BF16)
    gain = norm_gain[l][None]

    pos_p = N_META + np.arange(seq)
    pos_small = np.concatenate([np.full((n_s,), PAST_LEN), np.arange(N_META)])
    x_small = jnp.concatenate([x_sample[:, 0, :], meta_tokens.astype(x_prompt.dtype)], axis=0)
    xp = x_prompt.reshape(batch * seq, D_MODEL)
    xn_s = _rmsnorm(x_small, gain, tm=n_s + N_META)
    h_p, h_s, w_glu_bf, w_a_bf, w_s_bf, w_o_bf = _inproj(
        xp, gain, xn_s, w, qg, kg, e, _rope_tables(pos_p), _rope_tables(pos_small),
        (w_glu[l], w_attn_out[l], w_ssm_out[l], w_out[l]), tm=INPROJ_ROWS, halves=INPROJ_HALVES)

    win, wst, t8, wc, a_r, a_i, l_r, l_i, p_r, p_i = _ssm_tables(
        a_re[l], a_im[l], log_dt[l], b_re[l], b_im[l], c_re[l], c_im[l])
    dsk = d_skip[l][None]

    vm_t = h_s[n_s:, COL_KV + KV_WIDTH:IN_WIDTH].T
    o_p = _prompt_attention(sinks[l], h_p, h_s, vm_t, batch, seq)
    q_s = h_s[:n_s, COL_Q:COL_Q + ATTN_WIDTH].reshape(n_s, N_Q_HEADS, 1, HEAD_DIM)
    sel = (np.arange(N_Q_HEADS)[:, None] // (N_Q_HEADS // N_KV_HEADS)
           == np.arange(N_KV_HEADS)[None, :]).astype(np.float32)
    qp = (q_s * sel[None, :, :, None]).reshape(n_s, N_Q_HEADS, KV_WIDTH)
    bt = SAMPLE_ATTN_BATCH
    to_t = lambda c: c.transpose(0, 2, 3, 1).reshape(n_s, KV_WIDTH, WINDOW)
    from_t = lambda c: c.reshape(n_s, N_KV_HEADS, HEAD_DIM, WINDOW).transpose(0, 3, 1, 2)[None]
    new_t = lambda c0: h_s[:n_s, c0:c0 + KV_WIDTH].reshape(n_s // bt, bt, KV_WIDTH).transpose(0, 2, 1)
    o_s, s_win_k, s_win_v = _sample_attention(
        qp, new_t(COL_KV), new_t(COL_KV + KV_WIDTH), h_s, to_t(cache_win_k[l]), to_t(cache_win_v[l]),
        cache_meta_k[l].reshape(n_s, N_META, KV_WIDTH), cache_meta_v[l].reshape(n_s, N_META, KV_WIDTH),
        sinks[l][:, None], bt=bt)
    o_s = o_s[:, :, :HEAD_DIM].reshape(n_s, ATTN_WIDTH)

    y_p, p_re, p_im, y_s, s_re, s_im = _ssm(
        h_p, h_s, state_ssm_re[l].reshape(n_s, -1), state_ssm_im[l].reshape(n_s, -1),
        t8, win, wst, wc, a_r, a_i, l_r, l_i, p_r, p_i, dsk, batch, n_s)
    y_prompt, y_sample = _merge(
        o_p, h_p, y_p, xp, o_s, h_s, y_s, x_sample[:, 0, :], b_glu[l][None],
        w_glu_bf, w_a_bf, w_s_bf, w_o_bf, tm=MERGE_ROWS)
    y_prompt = y_prompt.reshape(batch, seq, D_MODEL)
    y_sample = y_sample.reshape(n_s, 1, D_MODEL)

    kv_p = h_p.reshape(batch, seq, IN_WIDTH)[:, seq - WINDOW:, COL_KV:]
    kv_p = kv_p.reshape(batch, WINDOW, 2, N_KV_HEADS, HEAD_DIM)
    p_win_k = kv_p[:, :, 0][None]
    p_win_v = kv_p[:, :, 1][None]
    kv_m = h_s[n_s:, COL_KV:IN_WIDTH].reshape(N_META, 2, N_KV_HEADS, HEAD_DIM)
    p_meta_k = jnp.broadcast_to(kv_m[None, :, 0], (batch, N_META, N_KV_HEADS, HEAD_DIM))[None]
    p_meta_v = jnp.broadcast_to(kv_m[None, :, 1], (batch, N_META, N_KV_HEADS, HEAD_DIM))[None]
    p_ssm_re = p_re.reshape(batch, N_SSM_GROUPS, SSM_STATE)[None]
    p_ssm_im = p_im.reshape(batch, N_SSM_GROUPS, SSM_STATE)[None]
    st_shape = (1, n_s, N_SSM_GROUPS, SSM_STATE)
    return (y_prompt, y_sample, p_win_k, p_win_v, p_meta_k, p_meta_v, p_ssm_re, p_ssm_im,
            from_t(s_win_k), from_t(s_win_v), s_re.reshape(st_shape), s_im.reshape(st_shape))
```

```python
import functools

import jax
import jax.numpy as jnp
import numpy as np
from jax import lax
from jax.experimental import pallas as pl
from jax.experimental.pallas import tpu as pltpu

F32 = jnp.float32
BF16 = jnp.bfloat16

D_MODEL = 2048
N_META = 16
HEAD_DIM = 64
N_Q_HEADS = 16
N_KV_HEADS = 4
ATTN_WIDTH = N_Q_HEADS * HEAD_DIM
KV_WIDTH = N_KV_HEADS * HEAD_DIM
WINDOW = 128
BLOCK = 128
ROPE_THETA = 10000.0
SSM_WIDTH = D_MODEL // 2
SSM_GROUP = 16
N_SSM_GROUPS = SSM_WIDTH // SSM_GROUP
SSM_STATE = 64
EPS = 1e-6
PAST_LEN = 8192

LANES = 128
SUBLANES = 8
CHUNK = 8
GROUPS_PER_TILE = LANES // SSM_GROUP
N_GROUP_TILES = N_SSM_GROUPS // GROUPS_PER_TILE
CHUNK_WIDTH = CHUNK * LANES
STATE_TILE = GROUPS_PER_TILE * SSM_STATE
SCAN_LEVELS = 8

COL_Q = 0
COL_ZA = 1024
COL_U = 2048
COL_ZS = 3072
COL_GA = 4096
COL_GS = 6144
COL_KV = 8192
IN_WIDTH = 8704
TN = 512
CAST_ROWS = 16
INPROJ_ROWS = 1024
INPROJ_HALVES = 2
MERGE_ROWS = 256
SAMPLE_ATTN_BATCH = 16
ATTN_BLOCKS_PER_STEP = 8
KV_TILE = COL_KV // TN
VMEM_LIMIT = 56 * 1024 * 1024
NEG = -1e30
LOG2E = 1.4426950408889634
Q_SCALE = HEAD_DIM ** -0.5 * LOG2E
NT_DIMS = (((1,), (1,)), ((), ()))


def _cmul(ar, ai, br, bi):
    return ar * br - ai * bi, ar * bi + ai * br


def _rmsnorm_kernel(x_ref, gain_ref, o_ref):
    x = x_ref[...]
    r = lax.rsqrt(jnp.mean(x * x, axis=-1, keepdims=True) + EPS)
    o_ref[...] = (x * r * gain_ref[...]).astype(BF16)


def _rmsnorm(x, gain, tm):
    m = x.shape[0]
    return pl.pallas_call(
        _rmsnorm_kernel,
        grid=(m // tm,),
        in_specs=[pl.BlockSpec((tm, D_MODEL), lambda i: (i, 0)),
                  pl.BlockSpec((1, D_MODEL), lambda i: (0, 0))],
        out_specs=pl.BlockSpec((tm, D_MODEL), lambda i: (i, 0)),
        out_shape=jax.ShapeDtypeStruct((m, D_MODEL), BF16),
        compiler_params=pltpu.CompilerParams(
            dimension_semantics=("arbitrary",), vmem_limit_bytes=VMEM_LIMIT),
        name="rmsnorm",
    )(x, gain)


def _inproj_epilogue(j, o_ref, qg_ref, kg_ref, e_ref, rope_ref):
    def head_norm_rope(n_chunks, gain_ref, scale):
        cos, sa, sb = (rope_ref[:, t * LANES:(t + 1) * LANES] for t in range(3))
        for c in range(n_chunks):
            cs = slice(c * LANES, (c + 1) * LANES)
            ac = o_ref[:, cs]
            sq = ac * ac
            hi = sq.astype(BF16)
            lo = (sq - hi.astype(F32)).astype(BF16)
            ms = jnp.dot(jnp.concatenate([hi, lo], axis=1), e_ref[...],
                         preferred_element_type=F32)
            xc = ac * lax.rsqrt(ms + EPS) * gain_ref[:, cs]
            o_ref[:, cs] = (xc * cos + pltpu.roll(xc, LANES - HEAD_DIM // 2, 1) * sa
                            + pltpu.roll(xc, HEAD_DIM // 2, 1) * sb) * scale

    @pl.when(j < COL_ZA // TN)
    def _():
        head_norm_rope(TN // LANES, qg_ref, Q_SCALE)

    @pl.when(j == KV_TILE)
    def _():
        head_norm_rope(KV_WIDTH // LANES, kg_ref, 1.0)


def _inproj_kernel(x_ref, gain_ref, xs_ref, w_ref, qg_ref, kg_ref, e_ref, rope_ref, rope_s_ref,
                   *rest, mw_slabs):
    n_mw = (len(rest) - 4) // 2
    mw_refs, (o_ref, os_ref) = rest[:n_mw], rest[n_mw:n_mw + 2]
    mw_out_refs, (xn_ref, wbf_ref) = rest[n_mw + 2:2 * n_mw + 2], rest[2 * n_mw + 2:]
    h, j, i = pl.program_id(0), pl.program_id(1), pl.program_id(2)
    tm = o_ref.shape[0]
    rows = pl.ds(pl.multiple_of(i * tm, tm), tm)

    @pl.when(j == 0)
    def _():
        x = x_ref[...]
        r = lax.rsqrt(jnp.mean(x * x, axis=-1, keepdims=True) + EPS)
        xn_ref[rows, :] = (x * r * gain_ref[...]).astype(BF16)

    @pl.when(i == 0)
    def _():
        wbf_ref[...] = w_ref[...].astype(BF16)

    o_ref[...] = jnp.dot(xn_ref[rows, :], wbf_ref[...], preferred_element_type=F32)
    _inproj_epilogue(j, o_ref, qg_ref, kg_ref, e_ref, rope_ref)

    step = (h * pl.num_programs(1) + j) * pl.num_programs(2) + i
    for src, dst, n_slabs in zip(mw_refs, mw_out_refs, mw_slabs):
        @pl.when(step < n_slabs)
        def _():
            dst[...] = src[...].astype(BF16)

    @pl.when(jnp.logical_and(h == 0, jnp.logical_and(j == 0, i == 0)))
    def _():
        os_ref[...] = jnp.zeros(os_ref.shape, F32)

    @pl.when(jnp.logical_and(h == pl.num_programs(0) - 1, i == pl.num_programs(2) - 1))
    def _():
        os_ref[...] = jnp.dot(xs_ref[...], wbf_ref[...], preferred_element_type=F32)
        _inproj_epilogue(j, os_ref, qg_ref, kg_ref, e_ref, rope_s_ref)


def _w_tile(j):
    n_q = COL_ZA // TN
    return jnp.where(j < n_q, j, jnp.where(j == KV_TILE, n_q, j + 1))


def _inproj(x, gain, xs, w, qg, kg, e, rope_p, rope_s, merge_weights, tm, halves):
    m, ms = x.shape[0], xs.shape[0]
    rows_half = m // halves
    tiles = rows_half // tm
    n_tab = rope_p.shape[0] // tm
    n_col = IN_WIDTH // TN
    const = lambda shape: pl.BlockSpec(shape, lambda h, j, i: (0, 0))
    n_steps = halves * n_col * tiles
    for mw in merge_weights:
        assert mw.shape[0] % CAST_ROWS == 0 and mw.shape[0] // CAST_ROWS <= n_steps
    slab = lambda mw: pl.BlockSpec(
        (CAST_ROWS, mw.shape[1]),
        lambda h, j, i: (jnp.minimum((h * n_col + j) * tiles + i, mw.shape[0] // CAST_ROWS - 1), 0))
    return pl.pallas_call(
        functools.partial(_inproj_kernel,
                          mw_slabs=tuple(mw.shape[0] // CAST_ROWS for mw in merge_weights)),
        grid=(halves, IN_WIDTH // TN, tiles),
        in_specs=[
            pl.BlockSpec((tm, D_MODEL),
                         lambda h, j, i: (h * tiles + jnp.where(j == 0, i, tiles - 1), 0)),
            const((1, D_MODEL)),
            const((ms, D_MODEL)),
            pl.BlockSpec((D_MODEL, TN), lambda h, j, i: (0, _w_tile(j))),
            const((1, TN)), const((1, KV_WIDTH)), const((2 * LANES, LANES)),
            pl.BlockSpec((tm, 3 * LANES), lambda h, j, i: ((h * tiles + i) % n_tab, 0)),
            const((ms, 3 * LANES)),
        ] + [slab(mw) for mw in merge_weights],
        out_specs=[pl.BlockSpec((tm, TN), lambda h, j, i: (h * tiles + i, j)),
                   pl.BlockSpec((ms, TN),
                                lambda h, j, i: (0, jnp.where(h == halves - 1, j, n_col)))
                   ] + [slab(mw) for mw in merge_weights],
        out_shape=[jax.ShapeDtypeStruct((m, IN_WIDTH), F32),
                   jax.ShapeDtypeStruct((ms, IN_WIDTH + TN), F32)
                   ] + [jax.ShapeDtypeStruct(mw.shape, BF16) for mw in merge_weights],
        scratch_shapes=[pltpu.VMEM((rows_half, D_MODEL), BF16), pltpu.VMEM((D_MODEL, TN), BF16)],
        compiler_params=pltpu.CompilerParams(
            dimension_semantics=("arbitrary", "arbitrary", "arbitrary"),
            vmem_limit_bytes=VMEM_LIMIT),
        name="inproj",
    )(x, gain, xs, w, qg, kg, e, rope_p, rope_s, *merge_weights)


def _dup_head(x, g, lo):
    r = pltpu.roll(x, HEAD_DIM, 1)
    return jnp.where(lo, x, r) if g % 2 == 0 else jnp.where(lo, r, x)


def _prompt_attn_kernel(sinks_ref, q_ref, kvc_ref, kvp_ref, kvm_ref, vmt_ref, bias_ref, o_ref):
    n_blk = q_ref.shape[0] // BLOCK
    lo = lax.broadcasted_iota(jnp.int32, (1, LANES), 1) < HEAD_DIM
    head_lane = lax.broadcasted_iota(jnp.int32, (1, 4 * BLOCK), 1) // BLOCK

    def prev_rows(blk, cols):
        if blk == 0:
            return kvp_ref[:, cols]
        return kvc_ref[(blk - 1) * BLOCK:blk * BLOCK, cols]

    scores = []
    for blk in range(n_blk):
        rows = slice(blk * BLOCK, (blk + 1) * BLOCK)
        for g in range(N_KV_HEADS):
            ks = slice((g // 2) * LANES, (g // 2 + 1) * LANES)
            k_all = jnp.concatenate(
                [_dup_head(prev_rows(blk, ks), g, lo), _dup_head(kvc_ref[rows, ks], g, lo),
                 _dup_head(kvm_ref[:, ks], g, lo)], axis=0).astype(BF16)
            qa = q_ref[rows, (2 * g) * LANES:(2 * g + 1) * LANES]
            qb = q_ref[rows, (2 * g + 1) * LANES:(2 * g + 2) * LANES]
            q4 = jnp.concatenate([jnp.where(lo, qa, 0.0), jnp.where(lo, 0.0, qa),
                                  jnp.where(lo, qb, 0.0), jnp.where(lo, 0.0, qb)],
                                 axis=0).astype(BF16)
            scores.append(lax.dot_general(k_all, q4, NT_DIMS, preferred_element_type=F32))
    for blk in range(n_blk):
        rows = slice(blk * BLOCK, (blk + 1) * BLOCK)
        bias = bias_ref[jnp.minimum(pl.program_id(1), 1)] if blk == 0 else bias_ref[1]
        vs = slice(KV_WIDTH, 2 * KV_WIDTH)
        v_t = jnp.concatenate([prev_rows(blk, vs).T, kvc_ref[rows, vs].T], axis=1)
        for g in range(N_KV_HEADS):
            s = scores[blk * N_KV_HEADS + g]
            s_w = s[:2 * BLOCK] + bias
            s_m = s[2 * BLOCK:]
            sink = LOG2E * jnp.where(head_lane == 0, sinks_ref[4 * g],
                                     jnp.where(head_lane == 1, sinks_ref[4 * g + 1],
                                               jnp.where(head_lane == 2, sinks_ref[4 * g + 2],
                                                         sinks_ref[4 * g + 3])))
            m = jnp.maximum(jnp.maximum(jnp.max(s_w, axis=0, keepdims=True),
                                        jnp.max(s_m, axis=0, keepdims=True)), sink)
            p_w = jnp.exp2(s_w - m)
            p_m = jnp.exp2(s_m - m)
            den = (jnp.sum(p_w, axis=0, keepdims=True) + jnp.sum(p_m, axis=0, keepdims=True)
                   + jnp.exp2(sink - m))
            p = jnp.concatenate([p_w, p_m], axis=0).astype(BF16)
            hs = slice(g * HEAD_DIM, (g + 1) * HEAD_DIM)
            vt_g = jnp.concatenate([v_t[hs], vmt_ref[hs, :]], axis=1).astype(BF16)
            o_t = jnp.dot(vt_g, p, preferred_element_type=F32) * (1.0 / den)
            for pair in range(2):
                two = jnp.concatenate(
                    [o_t[:, (2 * pair) * BLOCK:(2 * pair + 1) * BLOCK],
                     o_t[:, (2 * pair + 1) * BLOCK:(2 * pair + 2) * BLOCK]], axis=0)
                o_ref[rows, (2 * g + pair) * LANES:(2 * g + pair + 1) * LANES] = two.T


def _prompt_attention(sinks, h_p, h_s, vm_t, batch, seq):
    nb = seq // BLOCK
    kvb = COL_KV // (2 * KV_WIDTH)
    kj = np.arange(2 * BLOCK)[:, None]
    qi = np.arange(4 * BLOCK)[None, :] % BLOCK
    cur = (kj >= BLOCK) & (kj - BLOCK <= qi)
    prev = (kj < BLOCK) & (kj > qi)
    bias = jnp.asarray(np.where(np.stack([cur, cur | prev]), 0.0, NEG), F32)
    per = ATTN_BLOCKS_PER_STEP
    ns = nb // per
    return pl.pallas_call(
        _prompt_attn_kernel,
        grid=(batch, ns),
        in_specs=[
            pl.BlockSpec(memory_space=pltpu.SMEM),
            pl.BlockSpec((per * BLOCK, ATTN_WIDTH), lambda b, i: (b * ns + i, 0)),
            pl.BlockSpec((per * BLOCK, 2 * KV_WIDTH), lambda b, i: (b * ns + i, kvb)),
            pl.BlockSpec((BLOCK, 2 * KV_WIDTH),
                         lambda b, i: (b * nb + jnp.maximum(i * per - 1, 0), kvb)),
            pl.BlockSpec((N_META, 2 * KV_WIDTH), lambda b, i: (h_s.shape[0] // N_META - 1, kvb)),
            pl.BlockSpec((KV_WIDTH, N_META), lambda b, i: (0, 0)),
            pl.BlockSpec((2, 2 * BLOCK, 4 * BLOCK), lambda b, i: (0, 0, 0)),
        ],
        out_specs=pl.BlockSpec((per * BLOCK, ATTN_WIDTH), lambda b, i: (b * ns + i, 0)),
        out_shape=jax.ShapeDtypeStruct((batch * seq, ATTN_WIDTH), F32),
        compiler_params=pltpu.CompilerParams(
            dimension_semantics=("arbitrary", "arbitrary"), vmem_limit_bytes=VMEM_LIMIT),
        name="prompt_attn",
    )(sinks, h_p, h_p, h_p, h_s, vm_t, bias)


def _sample_attn_kernel(qp_ref, kn_ref, vn_ref, knr_ref, vnr_ref, ck_ref, cv_ref, mk_ref, mv_ref,
                        sink_ref, o_ref, ok_ref, ov_ref):
    bt = ck_ref.shape[0]
    slot = lax.broadcasted_iota(jnp.int32, (1, 1, WINDOW), 2)
    round_bf16 = lambda a: a.astype(BF16).astype(F32)
    qp = qp_ref[...].astype(BF16)
    ck = ck_ref[...]
    cv = cv_ref[...]
    k_m = mk_ref[...].astype(BF16)
    v_m = mv_ref[...].astype(BF16)
    s_w = jnp.einsum("bhc,bcw->bhw", qp, ck.astype(BF16), preferred_element_type=F32)
    s_w = jnp.where(slot == 0, NEG, s_w)
    s_m = jnp.einsum("bhc,bkc->bhk", qp, k_m, preferred_element_type=F32)
    knr, vnr = round_bf16(knr_ref[...]), round_bf16(vnr_ref[...])
    qf = qp.astype(F32)
    s_n = jnp.stack([jnp.sum(qf[b] * knr[b:b + 1, :], axis=-1, keepdims=True)
                     for b in range(bt)])
    sink = LOG2E * sink_ref[...][None]
    m = jnp.maximum(jnp.maximum(jnp.max(s_w, axis=-1, keepdims=True),
                                jnp.max(s_m, axis=-1, keepdims=True)),
                    jnp.maximum(s_n, sink))
    p_w = jnp.exp2(s_w - m)
    p_m = jnp.exp2(s_m - m)
    p_n = jnp.exp2(s_n - m)
    den = (jnp.sum(p_w, axis=-1, keepdims=True) + jnp.sum(p_m, axis=-1, keepdims=True)
           + p_n + jnp.exp2(sink - m))
    inv = 1.0 / den
    o = (jnp.einsum("bhw,bcw->bhc", (p_w * inv).astype(BF16), cv.astype(BF16),
                    preferred_element_type=F32)
         + jnp.einsum("bhk,bkc->bhc", (p_m * inv).astype(BF16), v_m,
                      preferred_element_type=F32)
         + jnp.stack([round_bf16(p_n[b] * inv[b]) * vnr[b:b + 1, :] for b in range(bt)]))

    last = lax.broadcasted_iota(jnp.int32, (1, WINDOW), 1) == WINDOW - 1
    kn = kn_ref[0]
    vn = vn_ref[0]
    ck = pltpu.roll(ck, WINDOW - 1, 2)
    cv = pltpu.roll(cv, WINDOW - 1, 2)
    ok_ref[...] = jnp.stack([jnp.where(last, kn[:, b:b + 1], ck[b]) for b in range(bt)])
    ov_ref[...] = jnp.stack([jnp.where(last, vn[:, b:b + 1], cv[b]) for b in range(bt)])
    hh = lax.broadcasted_iota(jnp.int32, (1, N_Q_HEADS, KV_WIDTH), 1) // (N_Q_HEADS // N_KV_HEADS)
    cc = lax.broadcasted_iota(jnp.int32, (1, N_Q_HEADS, KV_WIDTH), 2) // HEAD_DIM
    o = jnp.where(hh == cc, o, 0.0)
    o = o[:, :, :LANES] + o[:, :, LANES:]
    o_ref[...] = o + pltpu.roll(o, HEAD_DIM, 2)


def _sample_attention(qp, kn_t, vn_t, h_s, cache_k, cache_v, meta_k, meta_v, sink_col, bt):
    n = cache_k.shape[0]
    kvb = COL_KV // KV_WIDTH
    win_spec = pl.BlockSpec((bt, KV_WIDTH, WINDOW), lambda t: (t, 0, 0))
    new_spec = pl.BlockSpec((1, KV_WIDTH, bt), lambda t: (t, 0, 0))
    meta_spec = pl.BlockSpec((bt, N_META, KV_WIDTH), lambda t: (t, 0, 0))
    return pl.pallas_call(
        _sample_attn_kernel,
        grid=(n // bt,),
        in_specs=[
            pl.BlockSpec((bt, N_Q_HEADS, KV_WIDTH), lambda t: (t, 0, 0)),
            new_spec, new_spec,
            pl.BlockSpec((bt, KV_WIDTH), lambda t: (t, kvb)),
            pl.BlockSpec((bt, KV_WIDTH), lambda t: (t, kvb + 1)),
            win_spec, win_spec, meta_spec, meta_spec,
            pl.BlockSpec((N_Q_HEADS, 1), lambda t: (0, 0)),
        ],
        out_specs=[
            pl.BlockSpec((bt, N_Q_HEADS, LANES), lambda t: (t, 0, 0)),
            win_spec, win_spec,
        ],
        out_shape=[
            jax.ShapeDtypeStruct((n, N_Q_HEADS, LANES), F32),
            jax.ShapeDtypeStruct(cache_k.shape, F32),
            jax.ShapeDtypeStruct(cache_v.shape, F32),
        ],
        compiler_params=pltpu.CompilerParams(
            dimension_semantics=("arbitrary",), vmem_limit_bytes=VMEM_LIMIT),
        name="sample_attn",
    )(qp, kn_t, vn_t, h_s, h_s, cache_k, cache_v, meta_k, meta_v, sink_col)


def _gelu_tanh(x):
    c = 0.7978845608028654
    half = 0.5 * x
    return half + half * jnp.tanh(x * (c + (c * 0.044715) * (x * x)))


def _ssm_kernel(u_ref, um_ref, us_ref, hr_ref, hi_ref, t8_ref, win_ref, wst_ref, wc_ref,
                ar_ref, ai_ref, lr_ref, li_ref, pr_ref, pi_ref, d_ref,
                y_ref, sre_ref, sim_ref, ys_ref, nr_ref, ni_ref, xr_ref, xi_ref):
    dsk = d_ref[...]
    us = us_ref[...]
    bu = jnp.dot(us.astype(BF16), win_ref[0, (CHUNK - 1) * LANES:, :],
                 preferred_element_type=F32)
    l_r, l_i = lr_ref[0], li_ref[0]
    h_r, h_i = hr_ref[...], hi_ref[...]
    n_r = l_r * h_r - l_i * h_i + bu[:, :STATE_TILE]
    n_i = l_r * h_i + l_i * h_r + bu[:, STATE_TILE:]
    nr_ref[...] = n_r
    ni_ref[...] = n_i
    hn = jnp.concatenate([n_r, n_i], axis=1).astype(BF16)
    ys_ref[...] = _gelu_tanh(
        lax.dot_general(hn, wc_ref[0], NT_DIMS, preferred_element_type=F32) + dsk * us)

    batch = sre_ref.shape[0]
    nc = u_ref.shape[0] // CHUNK // batch
    win = win_ref[0]
    dsk_row = jnp.concatenate([dsk] * CHUNK, axis=1)

    def chunks_of(b):
        r0 = b * nc * CHUNK
        return jnp.concatenate(
            [u_ref[pl.ds(r0 + j, nc, stride=CHUNK), :] for j in range(CHUNK)], axis=1)

    n_mc = um_ref.shape[0] // CHUNK
    um = jnp.concatenate([um_ref[pl.ds(j, n_mc, stride=CHUNK), :] for j in range(CHUNK)], axis=1)
    gm = jnp.dot(um.astype(BF16), win, preferred_element_type=F32)
    a_r, a_i = ar_ref[0, 0:1, :], ai_ref[0, 0:1, :]
    s0r, s0i = gm[0:1, :STATE_TILE], gm[0:1, STATE_TILE:]
    for c in range(1, n_mc):
        s0r, s0i = (a_r * s0r - a_i * s0i + gm[c:c + 1, :STATE_TILE],
                    a_r * s0i + a_i * s0r + gm[c:c + 1, STATE_TILE:])

    n_t = nc // SUBLANES
    row = lax.broadcasted_iota(jnp.int32, (nc, 1), 0)
    first = row == 0
    in_tile = row % SUBLANES
    t_row = lax.broadcasted_iota(jnp.int32, (n_t, 1), 0)
    first_t = t_row == 0
    n_local = SUBLANES.bit_length() - 1
    w_r, w_i = pr_ref[0], pi_ref[0]

    def tile_totals(scr_ref, x):
        parts = []
        for c in range(STATE_TILE // LANES):
            scr_ref[c] = x[:, c * LANES:(c + 1) * LANES]
            parts.append(scr_ref[c, pl.ds(SUBLANES - 1, n_t, stride=SUBLANES), :])
        return jnp.concatenate(parts, axis=1)

    def shifted(x, d, keep):
        if d < SUBLANES:
            return jnp.where(keep, pltpu.roll(x, d, 0), 0.0)
        return jnp.concatenate([jnp.zeros((d, STATE_TILE), F32), x[:x.shape[0] - d]], axis=0)

    ucat = chunks_of(0)
    ub = ucat.astype(BF16)
    g = jnp.dot(ub, win, preferred_element_type=F32)
    for b in range(batch):
        y_intra = jnp.concatenate(
            [jnp.dot(ub[:, :(2 * m + 2) * LANES],
                     t8_ref[0, :(2 * m + 2) * LANES, 2 * m * LANES:(2 * m + 2) * LANES],
                     preferred_element_type=F32) for m in range(CHUNK // 2)],
            axis=1) + dsk_row * ucat
        if b + 1 < batch:
            ucat_next = chunks_of(b + 1)
            ub_next = ucat_next.astype(BF16)
            g_next = jnp.dot(ub_next, win, preferred_element_type=F32)
        gr, gi = g[:, :STATE_TILE], g[:, STATE_TILE:]
        for k in range(n_local):
            d = 1 << k
            k_r, k_i = ar_ref[0, k:k + 1, :], ai_ref[0, k:k + 1, :]
            sh_r, sh_i = shifted(gr, d, in_tile >= d), shifted(gi, d, in_tile >= d)
            gr, gi = gr + (k_r * sh_r - k_i * sh_i), gi + (k_r * sh_i + k_i * sh_r)
        tot_r, tot_i = tile_totals(xr_ref, gr), tile_totals(xi_ref, gi)
        k_r, k_i = ar_ref[0, n_local:n_local + 1, :], ai_ref[0, n_local:n_local + 1, :]
        tot_r = tot_r + jnp.where(first_t, k_r * s0r - k_i * s0i, 0.0)
        tot_i = tot_i + jnp.where(first_t, k_r * s0i + k_i * s0r, 0.0)
        for k in range(SCAN_LEVELS - n_local):
            d = 1 << k
            if d >= n_t:
                break
            k_r = ar_ref[0, n_local + k:n_local + k + 1, :]
            k_i = ai_ref[0, n_local + k:n_local + k + 1, :]
            sh_r, sh_i = shifted(tot_r, d, t_row >= d), shifted(tot_i, d, t_row >= d)
            tot_r, tot_i = tot_r + (k_r * sh_r - k_i * sh_i), tot_i + (k_r * sh_i + k_i * sh_r)
        sre_ref[b, 0] = tot_r[n_t - 1:n_t, :]
        sim_ref[b, 0] = tot_i[n_t - 1:n_t, :]
        c_r = jnp.where(first_t, s0r, pltpu.roll(tot_r, 1, 0))
        c_i = jnp.where(first_t, s0i, pltpu.roll(tot_i, 1, 0))
        full_r, full_i = [], []
        for t in range(n_t):
            rs = slice(t * SUBLANES, (t + 1) * SUBLANES)
            e_r, e_i = c_r[t:t + 1, :], c_i[t:t + 1, :]
            full_r.append(gr[rs] + (w_r * e_r - w_i * e_i))
            full_i.append(gi[rs] + (w_r * e_i + w_i * e_r))
        gr, gi = jnp.concatenate(full_r, axis=0), jnp.concatenate(full_i, axis=0)
        sp = jnp.concatenate([jnp.where(first, s0r, pltpu.roll(gr, 1, 0)),
                              jnp.where(first, s0i, pltpu.roll(gi, 1, 0))],
                             axis=1).astype(BF16)
        y2 = y_intra + lax.dot_general(sp, wst_ref[0], NT_DIMS, preferred_element_type=F32)
        for j in range(CHUNK):
            sl = slice(j * LANES, (j + 1) * LANES)
            y_ref[pl.ds(b * nc * CHUNK + j, nc, stride=CHUNK), :] = _gelu_tanh(y2[:, sl])
        if b + 1 < batch:
            ucat, ub, g = ucat_next, ub_next, g_next


def _ssm(h_p, h_s, h0r, h0i, t8, win, wst, wc, a_r, a_i, l_r, l_i, p_r, p_i, d_skip, batch, n_s):
    ub = COL_U // LANES
    rows = h_p.shape[0]
    wspec = pl.BlockSpec((1, CHUNK_WIDTH, CHUNK_WIDTH), lambda gt: (gt, 0, 0))
    aspec = pl.BlockSpec((1, SCAN_LEVELS, STATE_TILE), lambda gt: (gt, 0, 0))
    sspec = pl.BlockSpec((batch, 1, 1, STATE_TILE), lambda gt: (0, gt, 0, 0))
    st = pl.BlockSpec((n_s, STATE_TILE), lambda gt: (0, gt))
    lam = pl.BlockSpec((1, 1, STATE_TILE), lambda gt: (gt, 0, 0))
    return pl.pallas_call(
        _ssm_kernel,
        grid=(N_GROUP_TILES,),
        in_specs=[
            pl.BlockSpec((rows, LANES), lambda gt: (0, ub + gt)),
            pl.BlockSpec((N_META, LANES), lambda gt: (h_s.shape[0] // N_META - 1, ub + gt)),
            pl.BlockSpec((n_s, LANES), lambda gt: (0, ub + gt)),
            st, st,
            wspec, wspec, wspec,
            pl.BlockSpec((1, LANES, 2 * STATE_TILE), lambda gt: (gt, 0, 0)),
            aspec, aspec, lam, lam,
            pl.BlockSpec((1, SUBLANES, STATE_TILE), lambda gt: (gt, 0, 0)),
            pl.BlockSpec((1, SUBLANES, STATE_TILE), lambda gt: (gt, 0, 0)),
            pl.BlockSpec((1, LANES), lambda gt: (0, gt)),
        ],
        out_specs=[pl.BlockSpec((rows, LANES), lambda gt: (0, gt)), sspec, sspec,
                   pl.BlockSpec((n_s, LANES), lambda gt: (0, gt)), st, st],
        out_shape=[
            jax.ShapeDtypeStruct((rows, SSM_WIDTH), F32),
            jax.ShapeDtypeStruct((batch, N_GROUP_TILES, 1, STATE_TILE), F32),
            jax.ShapeDtypeStruct((batch, N_GROUP_TILES, 1, STATE_TILE), F32),
            jax.ShapeDtypeStruct((n_s, SSM_WIDTH), F32),
            jax.ShapeDtypeStruct((n_s, N_SSM_GROUPS * SSM_STATE), F32),
            jax.ShapeDtypeStruct((n_s, N_SSM_GROUPS * SSM_STATE), F32),
        ],
        scratch_shapes=[pltpu.VMEM((STATE_TILE // LANES, rows // CHUNK // batch, LANES), F32)] * 2,
        compiler_params=pltpu.CompilerParams(
            dimension_semantics=("arbitrary",), vmem_limit_bytes=VMEM_LIMIT),
        name="ssm",
    )(h_p, h_s, h_s, h0r, h0i, t8, win, wst, wc, a_r, a_i, l_r, l_i, p_r, p_i, d_skip)


def _merge_kernel(o_ref, za_ref, zs_ref, ga_ref, gs_ref, y_ref, x_ref,
                  o2_ref, za2_ref, zs2_ref, ga2_ref, gs2_ref, y2_ref, x2_ref,
                  bglu_ref, wglu_ref, wa_ref, ws_ref, wo_ref, out_ref, out2_ref):
    def merge(o_r, za_r, zs_r, ga_r, gs_r, y_r, x_r, out_r):
        za = za_r[...]
        a_in = (o_r[...] * (za * jax.nn.sigmoid(za))).astype(BF16)
        gated_a = jax.nn.sigmoid(ga_r[...]) * jnp.dot(a_in, wa_ref[...],
                                                      preferred_element_type=F32)
        y = y_r[...]
        t = jnp.dot(y.astype(BF16), wglu_ref[...], preferred_element_type=F32) + bglu_ref[...]
        y = y * jax.nn.sigmoid(t)
        zs = zs_r[...]
        s_in = (y * (zs * jax.nn.sigmoid(zs))).astype(BF16)
        br_s = jnp.dot(s_in, ws_ref[...], preferred_element_type=F32)
        mix = (gated_a + jax.nn.sigmoid(gs_r[...]) * br_s).astype(BF16)
        out_r[...] = x_r[...] + jnp.dot(mix, wo_ref[...], preferred_element_type=F32)

    merge(o_ref, za_ref, zs_ref, ga_ref, gs_ref, y_ref, x_ref, out_ref)

    @pl.when(pl.program_id(0) == pl.num_programs(0) - 1)
    def _():
        merge(o2_ref, za2_ref, zs2_ref, ga2_ref, gs2_ref, y2_ref, x2_ref, out2_ref)


def _merge(o, h, y, x, o2, h2, y2, x2, b_glu, w_glu, w_a, w_s, w_o, tm):
    m, m2 = x.shape[0], x2.shape[0]
    const = lambda shape: pl.BlockSpec(shape, lambda i: (0, 0), pipeline_mode=pl.Buffered(1))
    rows = lambda width, col: pl.BlockSpec((tm, width), lambda i: (i, col))
    rows2 = lambda width, col: pl.BlockSpec((m2, width), lambda i: (0, col),
                                            pipeline_mode=pl.Buffered(1))
    operands = lambda spec: [
        spec(ATTN_WIDTH, 0), spec(ATTN_WIDTH, COL_ZA // ATTN_WIDTH),
        spec(SSM_WIDTH, COL_ZS // SSM_WIDTH), spec(D_MODEL, COL_GA // D_MODEL),
        spec(D_MODEL, COL_GS // D_MODEL), spec(SSM_WIDTH, 0), spec(D_MODEL, 0)]
    return pl.pallas_call(
        _merge_kernel,
        grid=(m // tm,),
        in_specs=operands(rows) + operands(rows2) + [
            const((1, SSM_WIDTH)),
            const((SSM_WIDTH, SSM_WIDTH)),
            const((ATTN_WIDTH, D_MODEL)),
            const((SSM_WIDTH, D_MODEL)),
            const((D_MODEL, D_MODEL)),
        ],
        out_specs=[pl.BlockSpec((tm, D_MODEL), lambda i: (i, 0)),
                   pl.BlockSpec((m2, D_MODEL), lambda i: (0, 0))],
        out_shape=[jax.ShapeDtypeStruct((m, D_MODEL), F32),
                   jax.ShapeDtypeStruct((m2, D_MODEL), F32)],
        compiler_params=pltpu.CompilerParams(
            dimension_semantics=("arbitrary",), vmem_limit_bytes=VMEM_LIMIT),
        name="merge",
    )(o, h, h, h, h, y, x, o2, h2, h2, h2, h2, y2, x2, b_glu, w_glu, w_a, w_s, w_o)


def _rope_tables(pos):
    half = HEAD_DIM // 2
    inv_freq = (ROPE_THETA ** (-np.arange(half, dtype=np.float32) / half)).astype(np.float32)
    ang = np.asarray(pos, np.float32)[:, None] * inv_freq[None, :]
    cos, sin = np.cos(ang).astype(np.float32), np.sin(ang).astype(np.float32)
    zero = np.zeros_like(sin)
    reps = LANES // HEAD_DIM
    cos_t = np.tile(np.concatenate([cos, cos], axis=1), (1, reps))
    sin_a = np.tile(np.concatenate([-sin, zero], axis=1), (1, reps))
    sin_b = np.tile(np.concatenate([zero, sin], axis=1), (1, reps))
    return jnp.asarray(np.concatenate([cos_t, sin_a, sin_b], axis=1))


def _split_bf16(x):
    hi = x.astype(BF16)
    return hi, (x - hi.astype(F32)).astype(BF16)


def _dot_nt_f32(a, b):
    ah, al = _split_bf16(a)
    bh, bl = _split_bf16(b)
    dot = lambda x, y: lax.dot_general(x, y, NT_DIMS, preferred_element_type=F32)
    return dot(ah, bh) + dot(ah, bl) + dot(al, bh)


def _ssm_tables_kernel(are_ref, aim_ref, ldt_ref, bre_ref, bim_ref, cre_ref, cim_ref,
                       win_ref, wst_ref, t8_ref, wc_ref, ar_ref, ai_ref, lr_ref, li_ref,
                       pr_ref, pi_ref):
    a_r, a_i = are_ref[0], aim_ref[0]
    dt = jnp.exp(ldt_ref[0])
    mag = jnp.exp(a_r * dt)
    l_r, l_i = mag * jnp.cos(a_i * dt), mag * jnp.sin(a_i * dt)
    den = a_r * a_r + a_i * a_i
    f_r = ((l_r - 1.0) * a_r + l_i * a_i) / den
    f_i = (l_i * a_r - (l_r - 1.0) * a_i) / den
    shape = (LANES, STATE_TILE)
    own = (lax.broadcasted_iota(jnp.int32, shape, 0) // SSM_GROUP
           == lax.broadcasted_iota(jnp.int32, shape, 1) // SSM_STATE)
    sq = (LANES, LANES)
    same_group = (lax.broadcasted_iota(jnp.int32, sq, 0) // SSM_GROUP
                  == lax.broadcasted_iota(jnp.int32, sq, 1) // SSM_GROUP)
    lo = lax.broadcasted_iota(jnp.int32, (1, LANES), 1) < SSM_STATE
    n_rep = STATE_TILE // LANES

    def expand(x):
        return jnp.where(own, jnp.concatenate([x] * n_rep, axis=1), 0.0)

    def compact(v):
        e = jnp.where(own, jnp.broadcast_to(v, shape), 0.0)
        s = e[:, :LANES]
        for k in range(1, n_rep):
            s = s + e[:, k * LANES:(k + 1) * LANES]
        return s + pltpu.roll(s, SSM_STATE, 1)

    lc_r, lc_i = compact(l_r), compact(l_i)
    b_r, b_i = bre_ref[0], bim_ref[0]
    c_r, c_i = cre_ref[0], cim_ref[0]
    bb_r, bb_i = _cmul(compact(f_r), compact(f_i), b_r, b_i)
    wc_ref[0] = jnp.concatenate([expand(c_r), expand(-c_i)], axis=1).astype(BF16)
    c_cat = jnp.where(lo, c_r, -c_i)
    q_r, q_i = jnp.ones(sq, F32), jnp.zeros(sq, F32)
    for n in range(CHUNK):
        x_r, x_i = _cmul(q_r, q_i, bb_r, bb_i)
        win_ref[0, (CHUNK - 1 - n) * LANES:(CHUNK - n) * LANES, :] = jnp.concatenate(
            [expand(x_r), expand(x_i)], axis=1).astype(BF16)
        k_n = _dot_nt_f32(jnp.where(lo, x_r, x_i), c_cat)
        k_n = jnp.where(same_group, k_n, 0.0).astype(BF16)
        for j in range(CHUNK - n):
            t8_ref[0, j * LANES:(j + 1) * LANES, (j + n) * LANES:(j + n + 1) * LANES] = k_n
            if n > 0:
                t8_ref[0, (j + n) * LANES:(j + n + 1) * LANES, j * LANES:(j + 1) * LANES] = (
                    jnp.zeros(sq, BF16))
        q_r, q_i = _cmul(q_r, q_i, lc_r, lc_i)
        s_r, s_i = _cmul(c_r, c_i, q_r, q_i)
        wst_ref[0, n * LANES:(n + 1) * LANES, :] = jnp.concatenate(
            [expand(s_r), expand(-s_i)], axis=1).astype(BF16)
    p_r, p_i = l_r, l_i
    for _ in range(CHUNK.bit_length() - 1):
        p_r, p_i = _cmul(p_r, p_i, p_r, p_i)
    t_r, t_i = p_r, p_i
    for i in range(SUBLANES):
        pr_ref[0, i:i + 1, :] = t_r
        pi_ref[0, i:i + 1, :] = t_i
        t_r, t_i = _cmul(t_r, t_i, p_r, p_i)
    for k in range(SCAN_LEVELS):
        ar_ref[0, k:k + 1, :] = p_r
        ai_ref[0, k:k + 1, :] = p_i
        p_r, p_i = _cmul(p_r, p_i, p_r, p_i)
    lr_ref[0] = l_r
    li_ref[0] = l_i


def _ssm_tables(a_re, a_im, log_dt, b_re, b_im, c_re, c_im):
    gt = N_GROUP_TILES
    lane_vec = lambda v: v.reshape(gt, 1, STATE_TILE)
    dup = lambda v: jnp.concatenate([v, v], axis=-1)
    rows = GROUPS_PER_TILE * SSM_GROUP
    b_t = lambda v: dup(v.transpose(0, 2, 1).reshape(gt, rows, SSM_STATE))
    c_t = lambda v: dup(v.reshape(gt, rows, SSM_STATE))
    vec = pl.BlockSpec((1, 1, STATE_TILE), lambda g: (g, 0, 0))
    par = pl.BlockSpec((1, rows, LANES), lambda g: (g, 0, 0))
    big = pl.BlockSpec((1, CHUNK_WIDTH, CHUNK_WIDTH), lambda g: (g, 0, 0))
    wcs = pl.BlockSpec((1, LANES, 2 * STATE_TILE), lambda g: (g, 0, 0))
    scan = pl.BlockSpec((1, SCAN_LEVELS, STATE_TILE), lambda g: (g, 0, 0))
    tile = pl.BlockSpec((1, SUBLANES, STATE_TILE), lambda g: (g, 0, 0))
    return pl.pallas_call(
        _ssm_tables_kernel,
        grid=(gt,),
        in_specs=[vec, vec, vec, par, par, par, par],
        out_specs=[big, big, big, wcs, scan, scan, vec, vec, tile, tile],
        out_shape=[
            jax.ShapeDtypeStruct((gt, CHUNK_WIDTH, 2 * STATE_TILE), BF16),
            jax.ShapeDtypeStruct((gt, CHUNK_WIDTH, 2 * STATE_TILE), BF16),
            jax.ShapeDtypeStruct((gt, CHUNK_WIDTH, CHUNK_WIDTH), BF16),
            jax.ShapeDtypeStruct((gt, LANES, 2 * STATE_TILE), BF16),
            jax.ShapeDtypeStruct((gt, SCAN_LEVELS, STATE_TILE), F32),
            jax.ShapeDtypeStruct((gt, SCAN_LEVELS, STATE_TILE), F32),
            jax.ShapeDtypeStruct((gt, 1, STATE_TILE), F32),
            jax.ShapeDtypeStruct((gt, 1, STATE_TILE), F32),
            jax.ShapeDtypeStruct((gt, SUBLANES, STATE_TILE), F32),
            jax.ShapeDtypeStruct((gt, SUBLANES, STATE_TILE), F32),
        ],
        compiler_params=pltpu.CompilerParams(
            dimension_semantics=("arbitrary",), vmem_limit_bytes=VMEM_LIMIT),
        name="ssm_tables",
    )(lane_vec(a_re), lane_vec(a_im), lane_vec(jnp.repeat(log_dt, SSM_STATE)),
      b_t(b_re), b_t(b_im), c_t(c_re), c_t(c_im))


def kernel(x_prompt, x_sample, cache_win_k, cache_win_v, cache_meta_k, cache_meta_v,
           state_ssm_re, state_ssm_im, meta_tokens, norm_gain, w_in, q_norm_gain, k_norm_gain,
           sinks, a_re, a_im, log_dt, b_re, b_im, c_re, c_im, d_skip, w_glu, b_glu,
           w_attn_out, w_ssm_out, w_out):
    depth = w_in.shape[0]
    assert depth == 1, "single-layer trunk"
    batch, seq = x_prompt.shape[:2]
    n_s = x_sample.shape[0]
    assert x_sample.shape[1] == 1 and seq % (CHUNK * BLOCK) == 0
    l = 0

    w = w_in[l]
    qg = jnp.tile(q_norm_gain[l], TN // HEAD_DIM)[None]
    kg = jnp.tile(k_norm_gain[l], KV_WIDTH // HEAD_DIM)[None]
    lane = np.arange(LANES)
    e128 = np.where((lane[:, None] // HEAD_DIM) == (lane[None, :] // HEAD_DIM),
                    1.0 / HEAD_DIM, 0.0).astype(np.float32)
    e = jnp.asarray(np.concatenate([e128, e128], axis=0), BF16)
    gain = norm_gain[l][None]

    pos_p = N_META + np.arange(seq)
    pos_small = np.concatenate([np.full((n_s,), PAST_LEN), np.arange(N_META)])
    x_small = jnp.concatenate([x_sample[:, 0, :], meta_tokens.astype(x_prompt.dtype)], axis=0)
    xp = x_prompt.reshape(batch * seq, D_MODEL)
    xn_s = _rmsnorm(x_small, gain, tm=n_s + N_META)
    h_p, h_s, w_glu_bf, w_a_bf, w_s_bf, w_o_bf = _inproj(
        xp, gain, xn_s, w, qg, kg, e, _rope_tables(pos_p), _rope_tables(pos_small),
        (w_glu[l], w_attn_out[l], w_ssm_out[l], w_out[l]), tm=INPROJ_ROWS, halves=INPROJ_HALVES)

    win, wst, t8, wc, a_r, a_i, l_r, l_i, p_r, p_i = _ssm_tables(
        a_re[l], a_im[l], log_dt[l], b_re[l], b_im[l], c_re[l], c_im[l])
    dsk = d_skip[l][None]

    vm_t = h_s[n_s:, COL_KV + KV_WIDTH:IN_WIDTH].T
    o_p = _prompt_attention(sinks[l], h_p, h_s, vm_t, batch, seq)
    q_s = h_s[:n_s, COL_Q:COL_Q + ATTN_WIDTH].reshape(n_s, N_Q_HEADS, 1, HEAD_DIM)
    sel = (np.arange(N_Q_HEADS)[:, None] // (N_Q_HEADS // N_KV_HEADS)
           == np.arange(N_KV_HEADS)[None, :]).astype(np.float32)
    qp = (q_s * sel[None, :, :, None]).reshape(n_s, N_Q_HEADS, KV_WIDTH)
    bt = SAMPLE_ATTN_BATCH
    to_t = lambda c: c.transpose(0, 2, 3, 1).reshape(n_s, KV_WIDTH, WINDOW)
    from_t = lambda c: c.reshape(n_s, N_KV_HEADS, HEAD_DIM, WINDOW).transpose(0, 3, 1, 2)[None]
    new_t = lambda c0: h_s[:n_s, c0:c0 + KV_WIDTH].reshape(n_s // bt, bt, KV_WIDTH).transpose(0, 2, 1)
    o_s, s_win_k, s_win_v = _sample_attention(
        qp, new_t(COL_KV), new_t(COL_KV + KV_WIDTH), h_s, to_t(cache_win_k[l]), to_t(cache_win_v[l]),
        cache_meta_k[l].reshape(n_s, N_META, KV_WIDTH), cache_meta_v[l].reshape(n_s, N_META, KV_WIDTH),
        sinks[l][:, None], bt=bt)
    o_s = o_s[:, :, :HEAD_DIM].reshape(n_s, ATTN_WIDTH)

    y_p, p_re, p_im, y_s, s_re, s_im = _ssm(
        h_p, h_s, state_ssm_re[l].reshape(n_s, -1), state_ssm_im[l].reshape(n_s, -1),
        t8, win, wst, wc, a_r, a_i, l_r, l_i, p_r, p_i, dsk, batch, n_s)
    y_prompt, y_sample = _merge(
        o_p, h_p, y_p, xp, o_s, h_s, y_s, x_sample[:, 0, :], b_glu[l][None],
        w_glu_bf, w_a_bf, w_s_bf, w_o_bf, tm=MERGE_ROWS)
    y_prompt = y_prompt.reshape(batch, seq, D_MODEL)
    y_sample = y_sample.reshape(n_s, 1, D_MODEL)

    kv_p = h_p.reshape(batch, seq, IN_WIDTH)[:, seq - WINDOW:, COL_KV:]
    kv_p = kv_p.reshape(batch, WINDOW, 2, N_KV_HEADS, HEAD_DIM)
    p_win_k = kv_p[:, :, 0][None]
    p_win_v = kv_p[:, :, 1][None]
    kv_m = h_s[n_s:, COL_KV:IN_WIDTH].reshape(N_META, 2, N_KV_HEADS, HEAD_DIM)
    p_meta_k = jnp.broadcast_to(kv_m[None, :, 0], (batch, N_META, N_KV_HEADS, HEAD_DIM))[None]
    p_meta_v = jnp.broadcast_to(kv_m[None, :, 1], (batch, N_META, N_KV_HEADS, HEAD_DIM))[None]
    p_ssm_re = p_re.reshape(batch, N_SSM_GROUPS, SSM_STATE)[None]
    p_ssm_im = p_im.reshape(batch, N_SSM_GROUPS, SSM_STATE)[None]
    st_shape = (1, n_s, N_SSM_GROUPS, SSM_STATE)
    return (y_prompt, y_sample, p_win_k, p_win_v, p_meta_k, p_meta_v, p_ssm_re, p_ssm_im,
            from_t(s_win_k), from_t(s_win_v), s_re.reshape(st_shape), s_im.reshape(st_shape))
```

```python
import functools

import jax
import jax.numpy as jnp
import numpy as np
from jax import lax
from jax.experimental import pallas as pl
from jax.experimental.pallas import tpu as pltpu

F32 = jnp.float32
BF16 = jnp.bfloat16

D_MODEL = 2048
N_META = 16
HEAD_DIM = 64
N_Q_HEADS = 16
N_KV_HEADS = 4
ATTN_WIDTH = N_Q_HEADS * HEAD_DIM
KV_WIDTH = N_KV_HEADS * HEAD_DIM
WINDOW = 128
BLOCK = 128
ROPE_THETA = 10000.0
SSM_WIDTH = D_MODEL // 2
SSM_GROUP = 16
N_SSM_GROUPS = SSM_WIDTH // SSM_GROUP
SSM_STATE = 64
EPS = 1e-6
PAST_LEN = 8192

LANES = 128
SUBLANES = 8
CHUNK = 8
GROUPS_PER_TILE = LANES // SSM_GROUP
N_GROUP_TILES = N_SSM_GROUPS // GROUPS_PER_TILE
CHUNK_WIDTH = CHUNK * LANES
STATE_TILE = GROUPS_PER_TILE * SSM_STATE
SCAN_LEVELS = 8

COL_Q = 0
COL_ZA = 1024
COL_U = 2048
COL_ZS = 3072
COL_GA = 4096
COL_GS = 6144
COL_KV = 8192
IN_WIDTH = 8704
TN = 512
CAST_ROWS = 16
INPROJ_ROWS = 1024
INPROJ_HALVES = 2
MERGE_ROWS = 256
SAMPLE_ATTN_BATCH = 16
ATTN_BLOCKS_PER_STEP = 8
KV_TILE = COL_KV // TN
VMEM_LIMIT = 56 * 1024 * 1024
NEG = -1e30
LOG2E = 1.4426950408889634
Q_SCALE = HEAD_DIM ** -0.5 * LOG2E
NT_DIMS = (((1,), (1,)), ((), ()))


def _cmul(ar, ai, br, bi):
    return ar * br - ai * bi, ar * bi + ai * br


def _rmsnorm_kernel(x_ref, gain_ref, o_ref):
    x = x_ref[...]
    r = lax.rsqrt(jnp.mean(x * x, axis=-1, keepdims=True) + EPS)
    o_ref[...] = (x * r * gain_ref[...]).astype(BF16)


def _rmsnorm(x, gain, tm):
    m = x.shape[0]
    return pl.pallas_call(
        _rmsnorm_kernel,
        grid=(m // tm,),
        in_specs=[pl.BlockSpec((tm, D_MODEL), lambda i: (i, 0)),
                  pl.BlockSpec((1, D_MODEL), lambda i: (0, 0))],
        out_specs=pl.BlockSpec((tm, D_MODEL), lambda i: (i, 0)),
        out_shape=jax.ShapeDtypeStruct((m, D_MODEL), BF16),
        compiler_params=pltpu.CompilerParams(
            dimension_semantics=("arbitrary",), vmem_limit_bytes=VMEM_LIMIT),
        name="rmsnorm",
    )(x, gain)


def _inproj_epilogue(j, o_ref, qg_ref, kg_ref, e_ref, rope_ref):
    def head_norm_rope(n_chunks, gain_ref, scale):
        cos, sa, sb = (rope_ref[:, t * LANES:(t + 1) * LANES] for t in range(3))
        for c in range(n_chunks):
            cs = slice(c * LANES, (c + 1) * LANES)
            ac = o_ref[:, cs]
            sq = ac * ac
            hi = sq.astype(BF16)
            lo = (sq - hi.astype(F32)).astype(BF16)
            ms = jnp.dot(jnp.concatenate([hi, lo], axis=1), e_ref[...],
                         preferred_element_type=F32)
            xc = ac * lax.rsqrt(ms + EPS) * gain_ref[:, cs]
            o_ref[:, cs] = (xc * cos + pltpu.roll(xc, LANES - HEAD_DIM // 2, 1) * sa
                            + pltpu.roll(xc, HEAD_DIM // 2, 1) * sb) * scale

    @pl.when(j < COL_ZA // TN)
    def _():
        head_norm_rope(TN // LANES, qg_ref, Q_SCALE)

    @pl.when(j == KV_TILE)
    def _():
        head_norm_rope(KV_WIDTH // LANES, kg_ref, 1.0)


def _inproj_kernel(x_ref, gain_ref, xs_ref, w_ref, qg_ref, kg_ref, e_ref, rope_ref, rope_s_ref,
                   *rest, mw_slabs):
    n_mw = (len(rest) - 4) // 2
    mw_refs, (o_ref, os_ref) = rest[:n_mw], rest[n_mw:n_mw + 2]
    mw_out_refs, (xn_ref, wbf_ref) = rest[n_mw + 2:2 * n_mw + 2], rest[2 * n_mw + 2:]
    h, j, i = pl.program_id(0), pl.program_id(1), pl.program_id(2)
    tm = o_ref.shape[0]
    rows = pl.ds(pl.multiple_of(i * tm, tm), tm)

    @pl.when(j == 0)
    def _():
        x = x_ref[...]
        r = lax.rsqrt(jnp.mean(x * x, axis=-1, keepdims=True) + EPS)
        xn_ref[rows, :] = (x * r * gain_ref[...]).astype(BF16)

    @pl.when(i == 0)
    def _():
        wbf_ref[...] = w_ref[...].astype(BF16)

    o_ref[...] = jnp.dot(xn_ref[rows, :], wbf_ref[...], preferred_element_type=F32)
    _inproj_epilogue(j, o_ref, qg_ref, kg_ref, e_ref, rope_ref)

    step = (h * pl.num_programs(1) + j) * pl.num_programs(2) + i
    for src, dst, n_slabs in zip(mw_refs, mw_out_refs, mw_slabs):
        @pl.when(step < n_slabs)
        def _():
            dst[...] = src[...].astype(BF16)

    @pl.when(jnp.logical_and(h == 0, jnp.logical_and(j == 0, i == 0)))
    def _():
        os_ref[...] = jnp.zeros(os_ref.shape, F32)

    @pl.when(jnp.logical_and(h == pl.num_programs(0) - 1, i == pl.num_programs(2) - 1))
    def _():
        os_ref[...] = jnp.dot(xs_ref[...], wbf_ref[...], preferred_element_type=F32)
        _inproj_epilogue(j, os_ref, qg_ref, kg_ref, e_ref, rope_s_ref)


def _w_tile(j):
    n_q = COL_ZA // TN
    return jnp.where(j < n_q, j, jnp.where(j == KV_TILE, n_q, j + 1))


def _inproj(x, gain, xs, w, qg, kg, e, rope_p, rope_s, merge_weights, tm, halves):
    m, ms = x.shape[0], xs.shape[0]
    rows_half = m // halves
    tiles = rows_half // tm
    n_tab = rope_p.shape[0] // tm
    n_col = IN_WIDTH // TN
    const = lambda shape: pl.BlockSpec(shape, lambda h, j, i: (0, 0))
    n_steps = halves * n_col * tiles
    for mw in merge_weights:
        assert mw.shape[0] % CAST_ROWS == 0 and mw.shape[0] // CAST_ROWS <= n_steps
    slab = lambda mw: pl.BlockSpec(
        (CAST_ROWS, mw.shape[1]),
        lambda h, j, i: (jnp.minimum((h * n_col + j) * tiles + i, mw.shape[0] // CAST_ROWS - 1), 0))
    return pl.pallas_call(
        functools.partial(_inproj_kernel,
                          mw_slabs=tuple(mw.shape[0] // CAST_ROWS for mw in merge_weights)),
        grid=(halves, IN_WIDTH // TN, tiles),
        in_specs=[
            pl.BlockSpec((tm, D_MODEL),
                         lambda h, j, i: (h * tiles + jnp.where(j == 0, i, tiles - 1), 0)),
            const((1, D_MODEL)),
            const((ms, D_MODEL)),
            pl.BlockSpec((D_MODEL, TN), lambda h, j, i: (0, _w_tile(j))),
            const((1, TN)), const((1, KV_WIDTH)), const((2 * LANES, LANES)),
            pl.BlockSpec((tm, 3 * LANES), lambda h, j, i: ((h * tiles + i) % n_tab, 0)),
            const((ms, 3 * LANES)),
        ] + [slab(mw) for mw in merge_weights],
        out_specs=[pl.BlockSpec((tm, TN), lambda h, j, i: (h * tiles + i, j)),
                   pl.BlockSpec((ms, TN),
                                lambda h, j, i: (0, jnp.where(h == halves - 1, j, n_col)))
                   ] + [slab(mw) for mw in merge_weights],
        out_shape=[jax.ShapeDtypeStruct((m, IN_WIDTH), F32),
                   jax.ShapeDtypeStruct((ms, IN_WIDTH + TN), F32)
                   ] + [jax.ShapeDtypeStruct(mw.shape, BF16) for mw in merge_weights],
        scratch_shapes=[pltpu.VMEM((rows_half, D_MODEL), BF16), pltpu.VMEM((D_MODEL, TN), BF16)],
        compiler_params=pltpu.CompilerParams(
            dimension_semantics=("arbitrary", "arbitrary", "arbitrary"),
            vmem_limit_bytes=VMEM_LIMIT),
        name="inproj",
    )(x, gain, xs, w, qg, kg, e, rope_p, rope_s, *merge_weights)


def _dup_head(x, g, lo):
    r = pltpu.roll(x, HEAD_DIM, 1)
    return jnp.where(lo, x, r) if g % 2 == 0 else jnp.where(lo, r, x)


def _prompt_attn_kernel(sinks_ref, q_ref, kvc_ref, kvp_ref, kvm_ref, vmt_ref, bias_ref, o_ref):
    n_blk = q_ref.shape[0] // BLOCK
    lo = lax.broadcasted_iota(jnp.int32, (1, LANES), 1) < HEAD_DIM
    head_lane = lax.broadcasted_iota(jnp.int32, (1, 4 * BLOCK), 1) // BLOCK

    def prev_rows(blk, cols):
        if blk == 0:
            return kvp_ref[:, cols]
        return kvc_ref[(blk - 1) * BLOCK:blk * BLOCK, cols]

    scores = []
    for blk in range(n_blk):
        rows = slice(blk * BLOCK, (blk + 1) * BLOCK)
        for g in range(N_KV_HEADS):
            ks = slice((g // 2) * LANES, (g // 2 + 1) * LANES)
            k_all = jnp.concatenate(
                [_dup_head(prev_rows(blk, ks), g, lo), _dup_head(kvc_ref[rows, ks], g, lo),
                 _dup_head(kvm_ref[:, ks], g, lo)], axis=0).astype(BF16)
            qa = q_ref[rows, (2 * g) * LANES:(2 * g + 1) * LANES]
            qb = q_ref[rows, (2 * g + 1) * LANES:(2 * g + 2) * LANES]
            q4 = jnp.concatenate([jnp.where(lo, qa, 0.0), jnp.where(lo, 0.0, qa),
                                  jnp.where(lo, qb, 0.0), jnp.where(lo, 0.0, qb)],
                                 axis=0).astype(BF16)
            scores.append(lax.dot_general(k_all, q4, NT_DIMS, preferred_element_type=F32))
    for blk in range(n_blk):
        rows = slice(blk * BLOCK, (blk + 1) * BLOCK)
        bias = bias_ref[jnp.minimum(pl.program_id(1), 1)] if blk == 0 else bias_ref[1]
        vs = slice(KV_WIDTH, 2 * KV_WIDTH)
        v_t = jnp.concatenate([prev_rows(blk, vs).T, kvc_ref[rows, vs].T], axis=1)
        for g in range(N_KV_HEADS):
            s = scores[blk * N_KV_HEADS + g]
            s_w = s[:2 * BLOCK] + bias
            s_m = s[2 * BLOCK:]
            sink = LOG2E * jnp.where(head_lane == 0, sinks_ref[4 * g],
                                     jnp.where(head_lane == 1, sinks_ref[4 * g + 1],
                                               jnp.where(head_lane == 2, sinks_ref[4 * g + 2],
                                                         sinks_ref[4 * g + 3])))
            m = jnp.maximum(jnp.maximum(jnp.max(s_w, axis=0, keepdims=True),
                                        jnp.max(s_m, axis=0, keepdims=True)), sink)
            p_w = jnp.exp2(s_w - m)
            p_m = jnp.exp2(s_m - m)
            den = (jnp.sum(p_w, axis=0, keepdims=True) + jnp.sum(p_m, axis=0, keepdims=True)
                   + jnp.exp2(sink - m))
            p = jnp.concatenate([p_w, p_m], axis=0).astype(BF16)
            hs = slice(g * HEAD_DIM, (g + 1) * HEAD_DIM)
            vt_g = jnp.concatenate([v_t[hs], vmt_ref[hs, :]], axis=1).astype(BF16)
            o_t = jnp.dot(vt_g, p, preferred_element_type=F32) * (1.0 / den)
            for pair in range(2):
                two = jnp.concatenate(
                    [o_t[:, (2 * pair) * BLOCK:(2 * pair + 1) * BLOCK],
                     o_t[:, (2 * pair + 1) * BLOCK:(2 * pair + 2) * BLOCK]], axis=0)
                o_ref[rows, (2 * g + pair) * LANES:(2 * g + pair + 1) * LANES] = two.T


def _prompt_attention(sinks, h_p, h_s, vm_t, batch, seq):
    nb = seq // BLOCK
    kvb = COL_KV // (2 * KV_WIDTH)
    kj = np.arange(2 * BLOCK)[:, None]
    qi = np.arange(4 * BLOCK)[None, :] % BLOCK
    cur = (kj >= BLOCK) & (kj - BLOCK <= qi)
    prev = (kj < BLOCK) & (kj > qi)
    bias = jnp.asarray(np.where(np.stack([cur, cur | prev]), 0.0, NEG), F32)
    per = ATTN_BLOCKS_PER_STEP
    ns = nb // per
    return pl.pallas_call(
        _prompt_attn_kernel,
        grid=(batch, ns),
        in_specs=[
            pl.BlockSpec(memory_space=pltpu.SMEM),
            pl.BlockSpec((per * BLOCK, ATTN_WIDTH), lambda b, i: (b * ns + i, 0)),
            pl.BlockSpec((per * BLOCK, 2 * KV_WIDTH), lambda b, i: (b * ns + i, kvb)),
            pl.BlockSpec((BLOCK, 2 * KV_WIDTH),
                         lambda b, i: (b * nb + jnp.maximum(i * per - 1, 0), kvb)),
            pl.BlockSpec((N_META, 2 * KV_WIDTH), lambda b, i: (h_s.shape[0] // N_META - 1, kvb)),
            pl.BlockSpec((KV_WIDTH, N_META), lambda b, i: (0, 0)),
            pl.BlockSpec((2, 2 * BLOCK, 4 * BLOCK), lambda b, i: (0, 0, 0)),
        ],
        out_specs=pl.BlockSpec((per * BLOCK, ATTN_WIDTH), lambda b, i: (b * ns + i, 0)),
        out_shape=jax.ShapeDtypeStruct((batch * seq, ATTN_WIDTH), F32),
        compiler_params=pltpu.CompilerParams(
            dimension_semantics=("arbitrary", "arbitrary"), vmem_limit_bytes=VMEM_LIMIT),
        name="prompt_attn",
    )(sinks, h_p, h_p, h_p, h_s, vm_t, bias)


def _sample_attn_kernel(qp_ref, kn_ref, vn_ref, knr_ref, vnr_ref, ck_ref, cv_ref, mk_ref, mv_ref,
                        sink_ref, o_ref, ok_ref, ov_ref):
    bt = ck_ref.shape[0]
    slot = lax.broadcasted_iota(jnp.int32, (1, 1, WINDOW), 2)
    round_bf16 = lambda a: a.astype(BF16).astype(F32)
    qp = qp_ref[...].astype(BF16)
    ck = ck_ref[...]
    cv = cv_ref[...]
    k_m = mk_ref[...].astype(BF16)
    v_m = mv_ref[...].astype(BF16)
    s_w = jnp.einsum("bhc,bcw->bhw", qp, ck.astype(BF16), preferred_element_type=F32)
    s_w = jnp.where(slot == 0, NEG, s_w)
    s_m = jnp.einsum("bhc,bkc->bhk", qp, k_m, preferred_element_type=F32)
    knr, vnr = round_bf16(knr_ref[...]), round_bf16(vnr_ref[...])
    qf = qp.astype(F32)
    s_n = jnp.stack([jnp.sum(qf[b] * knr[b:b + 1, :], axis=-1, keepdims=True)
                     for b in range(bt)])
    sink = LOG2E * sink_ref[...][None]
    m = jnp.maximum(jnp.maximum(jnp.max(s_w, axis=-1, keepdims=True),
                                jnp.max(s_m, axis=-1, keepdims=True)),
                    jnp.maximum(s_n, sink))
    p_w = jnp.exp2(s_w - m)
    p_m = jnp.exp2(s_m - m)
    p_n = jnp.exp2(s_n - m)
    den = (jnp.sum(p_w, axis=-1, keepdims=True) + jnp.sum(p_m, axis=-1, keepdims=True)
           + p_n + jnp.exp2(sink - m))
    inv = 1.0 / den
    o = (jnp.einsum("bhw,bcw->bhc", (p_w * inv).astype(BF16), cv.astype(BF16),
                    preferred_element_type=F32)
         + jnp.einsum("bhk,bkc->bhc", (p_m * inv).astype(BF16), v_m,
                      preferred_element_type=F32)
         + jnp.stack([round_bf16(p_n[b] * inv[b]) * vnr[b:b + 1, :] for b in range(bt)]))

    last = lax.broadcasted_iota(jnp.int32, (1, WINDOW), 1) == WINDOW - 1
    kn = kn_ref[0]
    vn = vn_ref[0]
    ck = pltpu.roll(ck, WINDOW - 1, 2)
    cv = pltpu.roll(cv, WINDOW - 1, 2)
    ok_ref[...] = jnp.stack([jnp.where(last, kn[:, b:b + 1], ck[b]) for b in range(bt)])
    ov_ref[...] = jnp.stack([jnp.where(last, vn[:, b:b + 1], cv[b]) for b in range(bt)])
    hh = lax.broadcasted_iota(jnp.int32, (1, N_Q_HEADS, KV_WIDTH), 1) // (N_Q_HEADS // N_KV_HEADS)
    cc = lax.broadcasted_iota(jnp.int32, (1, N_Q_HEADS, KV_WIDTH), 2) // HEAD_DIM
    o = jnp.where(hh == cc, o, 0.0)
    o = o[:, :, :LANES] + o[:, :, LANES:]
    o_ref[...] = o + pltpu.roll(o, HEAD_DIM, 2)


def _sample_attention(qp, kn_t, vn_t, h_s, cache_k, cache_v, meta_k, meta_v, sink_col, bt):
    n = cache_k.shape[0]
    kvb = COL_KV // KV_WIDTH
    win_spec = pl.BlockSpec((bt, KV_WIDTH, WINDOW), lambda t: (t, 0, 0))
    new_spec = pl.BlockSpec((1, KV_WIDTH, bt), lambda t: (t, 0, 0))
    meta_spec = pl.BlockSpec((bt, N_META, KV_WIDTH), lambda t: (t, 0, 0))
    return pl.pallas_call(
        _sample_attn_kernel,
        grid=(n // bt,),
        in_specs=[
            pl.BlockSpec((bt, N_Q_HEADS, KV_WIDTH), lambda t: (t, 0, 0)),
            new_spec, new_spec,
            pl.BlockSpec((bt, KV_WIDTH), lambda t: (t, kvb)),
            pl.BlockSpec((bt, KV_WIDTH), lambda t: (t, kvb + 1)),
            win_spec, win_spec, meta_spec, meta_spec,
            pl.BlockSpec((N_Q_HEADS, 1), lambda t: (0, 0)),
        ],
        out_specs=[
            pl.BlockSpec((bt, N_Q_HEADS, LANES), lambda t: (t, 0, 0)),
            win_spec, win_spec,
        ],
        out_shape=[
            jax.ShapeDtypeStruct((n, N_Q_HEADS, LANES), F32),
            jax.ShapeDtypeStruct(cache_k.shape, F32),
            jax.ShapeDtypeStruct(cache_v.shape, F32),
        ],
        compiler_params=pltpu.CompilerParams(
            dimension_semantics=("arbitrary",), vmem_limit_bytes=VMEM_LIMIT),
        name="sample_attn",
    )(qp, kn_t, vn_t, h_s, h_s, cache_k, cache_v, meta_k, meta_v, sink_col)


def _gelu_tanh(x):
    c = 0.7978845608028654
    half = 0.5 * x
    return half + half * jnp.tanh(x * (c + (c * 0.044715) * (x * x)))


def _ssm_kernel(u_ref, um_ref, us_ref, hr_ref, hi_ref, t8_ref, win_ref, wst_ref, wc_ref,
                ar_ref, ai_ref, lr_ref, li_ref, pr_ref, pi_ref, d_ref,
                y_ref, sre_ref, sim_ref, ys_ref, nr_ref, ni_ref, xr_ref, xi_ref):
    dsk = d_ref[...]
    us = us_ref[...]
    bu = jnp.dot(us.astype(BF16), win_ref[0, (CHUNK - 1) * LANES:, :],
                 preferred_element_type=F32)
    l_r, l_i = lr_ref[0], li_ref[0]
    h_r, h_i = hr_ref[...], hi_ref[...]
    n_r = l_r * h_r - l_i * h_i + bu[:, :STATE_TILE]
    n_i = l_r * h_i + l_i * h_r + bu[:, STATE_TILE:]
    nr_ref[...] = n_r
    ni_ref[...] = n_i
    hn = jnp.concatenate([n_r, n_i], axis=1).astype(BF16)
    ys_ref[...] = _gelu_tanh(
        lax.dot_general(hn, wc_ref[0], NT_DIMS, preferred_element_type=F32) + dsk * us)

    batch = sre_ref.shape[0]
    nc = u_ref.shape[0] // CHUNK // batch
    win = win_ref[0]
    dsk_row = jnp.concatenate([dsk] * CHUNK, axis=1)

    def chunks_of(b):
        r0 = b * nc * CHUNK
        return jnp.concatenate(
            [u_ref[pl.ds(r0 + j, nc, stride=CHUNK), :] for j in range(CHUNK)], axis=1)

    n_mc = um_ref.shape[0] // CHUNK
    um = jnp.concatenate([um_ref[pl.ds(j, n_mc, stride=CHUNK), :] for j in range(CHUNK)], axis=1)
    gm = jnp.dot(um.astype(BF16), win, preferred_element_type=F32)
    a_r, a_i = ar_ref[0, 0:1, :], ai_ref[0, 0:1, :]
    s0r, s0i = gm[0:1, :STATE_TILE], gm[0:1, STATE_TILE:]
    for c in range(1, n_mc):
        s0r, s0i = (a_r * s0r - a_i * s0i + gm[c:c + 1, :STATE_TILE],
                    a_r * s0i + a_i * s0r + gm[c:c + 1, STATE_TILE:])

    n_t = nc // SUBLANES
    row = lax.broadcasted_iota(jnp.int32, (nc, 1), 0)
    first = row == 0
    in_tile = row % SUBLANES
    t_row = lax.broadcasted_iota(jnp.int32, (n_t, 1), 0)
    first_t = t_row == 0
    n_local = SUBLANES.bit_length() - 1
    w_r, w_i = pr_ref[0], pi_ref[0]

    def tile_totals(scr_ref, x):
        parts = []
        for c in range(STATE_TILE // LANES):
            scr_ref[c] = x[:, c * LANES:(c + 1) * LANES]
            parts.append(scr_ref[c, pl.ds(SUBLANES - 1, n_t, stride=SUBLANES), :])
        return jnp.concatenate(parts, axis=1)

    def shifted(x, d, keep):
        if d < SUBLANES:
            return jnp.where(keep, pltpu.roll(x, d, 0), 0.0)
        return jnp.concatenate([jnp.zeros((d, STATE_TILE), F32), x[:x.shape[0] - d]], axis=0)

    ucats = [chunks_of(b) for b in range(batch)]
    ubs = [u.astype(BF16) for u in ucats]
    g_all = jnp.dot(jnp.concatenate(ubs, axis=0), win, preferred_element_type=F32)
    for b in range(batch):
        ucat, ub, g = ucats[b], ubs[b], g_all[b * nc:(b + 1) * nc]
        y_intra = jnp.concatenate(
            [jnp.dot(ub[:, :(2 * m + 2) * LANES],
                     t8_ref[0, :(2 * m + 2) * LANES, 2 * m * LANES:(2 * m + 2) * LANES],
                     preferred_element_type=F32) for m in range(CHUNK // 2)],
            axis=1) + dsk_row * ucat
        gr, gi = g[:, :STATE_TILE], g[:, STATE_TILE:]
        for k in range(n_local):
            d = 1 << k
            k_r, k_i = ar_ref[0, k:k + 1, :], ai_ref[0, k:k + 1, :]
            sh_r, sh_i = shifted(gr, d, in_tile >= d), shifted(gi, d, in_tile >= d)
            gr, gi = gr + (k_r * sh_r - k_i * sh_i), gi + (k_r * sh_i + k_i * sh_r)
        tot_r, tot_i = tile_totals(xr_ref, gr), tile_totals(xi_ref, gi)
        k_r, k_i = ar_ref[0, n_local:n_local + 1, :], ai_ref[0, n_local:n_local + 1, :]
        tot_r = tot_r + jnp.where(first_t, k_r * s0r - k_i * s0i, 0.0)
        tot_i = tot_i + jnp.where(first_t, k_r * s0i + k_i * s0r, 0.0)
        for k in range(SCAN_LEVELS - n_local):
            d = 1 << k
            if d >= n_t:
                break
            k_r = ar_ref[0, n_local + k:n_local + k + 1, :]
            k_i = ai_ref[0, n_local + k:n_local + k + 1, :]
            sh_r, sh_i = shifted(tot_r, d, t_row >= d), shifted(tot_i, d, t_row >= d)
            tot_r, tot_i = tot_r + (k_r * sh_r - k_i * sh_i), tot_i + (k_r * sh_i + k_i * sh_r)
        sre_ref[b, 0] = tot_r[n_t - 1:n_t, :]
        sim_ref[b, 0] = tot_i[n_t - 1:n_t, :]
        c_r = jnp.where(first_t, s0r, pltpu.roll(tot_r, 1, 0))
        c_i = jnp.where(first_t, s0i, pltpu.roll(tot_i, 1, 0))
        full_r, full_i = [], []
        for t in range(n_t):
            rs = slice(t * SUBLANES, (t + 1) * SUBLANES)
            e_r, e_i = c_r[t:t + 1, :], c_i[t:t + 1, :]
            full_r.append(gr[rs] + (w_r * e_r - w_i * e_i))
            full_i.append(gi[rs] + (w_r * e_i + w_i * e_r))
        gr, gi = jnp.concatenate(full_r, axis=0), jnp.concatenate(full_i, axis=0)
        sp = jnp.concatenate([jnp.where(first, s0r, pltpu.roll(gr, 1, 0)),
                              jnp.where(first, s0i, pltpu.roll(gi, 1, 0))],
                             axis=1).astype(BF16)
        y2 = y_intra + lax.dot_general(sp, wst_ref[0], NT_DIMS, preferred_element_type=F32)
        for j in range(CHUNK):
            sl = slice(j * LANES, (j + 1) * LANES)
            y_ref[pl.ds(b * nc * CHUNK + j, nc, stride=CHUNK), :] = _gelu_tanh(y2[:, sl])


def _ssm(h_p, h_s, h0r, h0i, t8, win, wst, wc, a_r, a_i, l_r, l_i, p_r, p_i, d_skip, batch, n_s):
    ub = COL_U // LANES
    rows = h_p.shape[0]
    wspec = pl.BlockSpec((1, CHUNK_WIDTH, CHUNK_WIDTH), lambda gt: (gt, 0, 0))
    aspec = pl.BlockSpec((1, SCAN_LEVELS, STATE_TILE), lambda gt: (gt, 0, 0))
    sspec = pl.BlockSpec((batch, 1, 1, STATE_TILE), lambda gt: (0, gt, 0, 0))
    st = pl.BlockSpec((n_s, STATE_TILE), lambda gt: (0, gt))
    lam = pl.BlockSpec((1, 1, STATE_TILE), lambda gt: (gt, 0, 0))
    return pl.pallas_call(
        _ssm_kernel,
        grid=(N_GROUP_TILES,),
        in_specs=[
            pl.BlockSpec((rows, LANES), lambda gt: (0, ub + gt)),
            pl.BlockSpec((N_META, LANES), lambda gt: (h_s.shape[0] // N_META - 1, ub + gt)),
            pl.BlockSpec((n_s, LANES), lambda gt: (0, ub + gt)),
            st, st,
            wspec, wspec, wspec,
            pl.BlockSpec((1, LANES, 2 * STATE_TILE), lambda gt: (gt, 0, 0)),
            aspec, aspec, lam, lam,
            pl.BlockSpec((1, SUBLANES, STATE_TILE), lambda gt: (gt, 0, 0)),
            pl.BlockSpec((1, SUBLANES, STATE_TILE), lambda gt: (gt, 0, 0)),
            pl.BlockSpec((1, LANES), lambda gt: (0, gt)),
        ],
        out_specs=[pl.BlockSpec((rows, LANES), lambda gt: (0, gt)), sspec, sspec,
                   pl.BlockSpec((n_s, LANES), lambda gt: (0, gt)), st, st],
        out_shape=[
            jax.ShapeDtypeStruct((rows, SSM_WIDTH), F32),
            jax.ShapeDtypeStruct((batch, N_GROUP_TILES, 1, STATE_TILE), F32),
            jax.ShapeDtypeStruct((batch, N_GROUP_TILES, 1, STATE_TILE), F32),
            jax.ShapeDtypeStruct((n_s, SSM_WIDTH), F32),
            jax.ShapeDtypeStruct((n_s, N_SSM_GROUPS * SSM_STATE), F32),
            jax.ShapeDtypeStruct((n_s, N_SSM_GROUPS * SSM_STATE), F32),
        ],
        scratch_shapes=[pltpu.VMEM((STATE_TILE // LANES, rows // CHUNK // batch, LANES), F32)] * 2,
        compiler_params=pltpu.CompilerParams(
            dimension_semantics=("arbitrary",), vmem_limit_bytes=VMEM_LIMIT),
        name="ssm",
    )(h_p, h_s, h_s, h0r, h0i, t8, win, wst, wc, a_r, a_i, l_r, l_i, p_r, p_i, d_skip)


def _merge_kernel(o_ref, za_ref, zs_ref, ga_ref, gs_ref, y_ref, x_ref,
                  o2_ref, za2_ref, zs2_ref, ga2_ref, gs2_ref, y2_ref, x2_ref,
                  bglu_ref, wglu_ref, wa_ref, ws_ref, wo_ref, out_ref, out2_ref):
    def merge(o_r, za_r, zs_r, ga_r, gs_r, y_r, x_r, out_r):
        za = za_r[...]
        a_in = (o_r[...] * (za * jax.nn.sigmoid(za))).astype(BF16)
        gated_a = jax.nn.sigmoid(ga_r[...]) * jnp.dot(a_in, wa_ref[...],
                                                      preferred_element_type=F32)
        y = y_r[...]
        t = jnp.dot(y.astype(BF16), wglu_ref[...], preferred_element_type=F32) + bglu_ref[...]
        y = y * jax.nn.sigmoid(t)
        zs = zs_r[...]
        s_in = (y * (zs * jax.nn.sigmoid(zs))).astype(BF16)
        br_s = jnp.dot(s_in, ws_ref[...], preferred_element_type=F32)
        mix = (gated_a + jax.nn.sigmoid(gs_r[...]) * br_s).astype(BF16)
        out_r[...] = x_r[...] + jnp.dot(mix, wo_ref[...], preferred_element_type=F32)

    merge(o_ref, za_ref, zs_ref, ga_ref, gs_ref, y_ref, x_ref, out_ref)

    @pl.when(pl.program_id(0) == pl.num_programs(0) - 1)
    def _():
        merge(o2_ref, za2_ref, zs2_ref, ga2_ref, gs2_ref, y2_ref, x2_ref, out2_ref)


def _merge(o, h, y, x, o2, h2, y2, x2, b_glu, w_glu, w_a, w_s, w_o, tm):
    m, m2 = x.shape[0], x2.shape[0]
    const = lambda shape: pl.BlockSpec(shape, lambda i: (0, 0), pipeline_mode=pl.Buffered(1))
    rows = lambda width, col: pl.BlockSpec((tm, width), lambda i: (i, col))
    rows2 = lambda width, col: pl.BlockSpec((m2, width), lambda i: (0, col),
                                            pipeline_mode=pl.Buffered(1))
    operands = lambda spec: [
        spec(ATTN_WIDTH, 0), spec(ATTN_WIDTH, COL_ZA // ATTN_WIDTH),
        spec(SSM_WIDTH, COL_ZS // SSM_WIDTH), spec(D_MODEL, COL_GA // D_MODEL),
        spec(D_MODEL, COL_GS // D_MODEL), spec(SSM_WIDTH, 0), spec(D_MODEL, 0)]
    return pl.pallas_call(
        _merge_kernel,
        grid=(m // tm,),
        in_specs=operands(rows) + operands(rows2) + [
            const((1, SSM_WIDTH)),
            const((SSM_WIDTH, SSM_WIDTH)),
            const((ATTN_WIDTH, D_MODEL)),
            const((SSM_WIDTH, D_MODEL)),
            const((D_MODEL, D_MODEL)),
        ],
        out_specs=[pl.BlockSpec((tm, D_MODEL), lambda i: (i, 0)),
                   pl.BlockSpec((m2, D_MODEL), lambda i: (0, 0))],
        out_shape=[jax.ShapeDtypeStruct((m, D_MODEL), F32),
                   jax.ShapeDtypeStruct((m2, D_MODEL), F32)],
        compiler_params=pltpu.CompilerParams(
            dimension_semantics=("arbitrary",), vmem_limit_bytes=VMEM_LIMIT),
        name="merge",
    )(o, h, h, h, h, y, x, o2, h2, h2, h2, h2, y2, x2, b_glu, w_glu, w_a, w_s, w_o)


def _rope_tables(pos):
    half = HEAD_DIM // 2
    inv_freq = (ROPE_THETA ** (-np.arange(half, dtype=np.float32) / half)).astype(np.float32)
    ang = np.asarray(pos, np.float32)[:, None] * inv_freq[None, :]
    cos, sin = np.cos(ang).astype(np.float32), np.sin(ang).astype(np.float32)
    zero = np.zeros_like(sin)
    reps = LANES // HEAD_DIM
    cos_t = np.tile(np.concatenate([cos, cos], axis=1), (1, reps))
    sin_a = np.tile(np.concatenate([-sin, zero], axis=1), (1, reps))
    sin_b = np.tile(np.concatenate([zero, sin], axis=1), (1, reps))
    return jnp.asarray(np.concatenate([cos_t, sin_a, sin_b], axis=1))


def _split_bf16(x):
    hi = x.astype(BF16)
    return hi, (x - hi.astype(F32)).astype(BF16)


def _dot_nt_f32(a, b):
    ah, al = _split_bf16(a)
    bh, bl = _split_bf16(b)
    dot = lambda x, y: lax.dot_general(x, y, NT_DIMS, preferred_element_type=F32)
    return dot(ah, bh) + dot(ah, bl) + dot(al, bh)


def _ssm_tables_kernel(are_ref, aim_ref, ldt_ref, bre_ref, bim_ref, cre_ref, cim_ref,
                       win_ref, wst_ref, t8_ref, wc_ref, ar_ref, ai_ref, lr_ref, li_ref,
                       pr_ref, pi_ref):
    a_r, a_i = are_ref[0], aim_ref[0]
    dt = jnp.exp(ldt_ref[0])
    mag = jnp.exp(a_r * dt)
    l_r, l_i = mag * jnp.cos(a_i * dt), mag * jnp.sin(a_i * dt)
    den = a_r * a_r + a_i * a_i
    f_r = ((l_r - 1.0) * a_r + l_i * a_i) / den
    f_i = (l_i * a_r - (l_r - 1.0) * a_i) / den
    shape = (LANES, STATE_TILE)
    own = (lax.broadcasted_iota(jnp.int32, shape, 0) // SSM_GROUP
           == lax.broadcasted_iota(jnp.int32, shape, 1) // SSM_STATE)
    sq = (LANES, LANES)
    same_group = (lax.broadcasted_iota(jnp.int32, sq, 0) // SSM_GROUP
                  == lax.broadcasted_iota(jnp.int32, sq, 1) // SSM_GROUP)
    lo = lax.broadcasted_iota(jnp.int32, (1, LANES), 1) < SSM_STATE
    n_rep = STATE_TILE // LANES

    def expand(x):
        return jnp.where(own, jnp.concatenate([x] * n_rep, axis=1), 0.0)

    def compact(v):
        e = jnp.where(own, jnp.broadcast_to(v, shape), 0.0)
        s = e[:, :LANES]
        for k in range(1, n_rep):
            s = s + e[:, k * LANES:(k + 1) * LANES]
        return s + pltpu.roll(s, SSM_STATE, 1)

    lc_r, lc_i = compact(l_r), compact(l_i)
    b_r, b_i = bre_ref[0], bim_ref[0]
    c_r, c_i = cre_ref[0], cim_ref[0]
    bb_r, bb_i = _cmul(compact(f_r), compact(f_i), b_r, b_i)
    wc_ref[0] = jnp.concatenate([expand(c_r), expand(-c_i)], axis=1).astype(BF16)
    c_cat = jnp.where(lo, c_r, -c_i)
    q_r, q_i = jnp.ones(sq, F32), jnp.zeros(sq, F32)
    for n in range(CHUNK):
        x_r, x_i = _cmul(q_r, q_i, bb_r, bb_i)
        win_ref[0, (CHUNK - 1 - n) * LANES:(CHUNK - n) * LANES, :] = jnp.concatenate(
            [expand(x_r), expand(x_i)], axis=1).astype(BF16)
        k_n = _dot_nt_f32(jnp.where(lo, x_r, x_i), c_cat)
        k_n = jnp.where(same_group, k_n, 0.0).astype(BF16)
        for j in range(CHUNK - n):
            t8_ref[0, j * LANES:(j + 1) * LANES, (j + n) * LANES:(j + n + 1) * LANES] = k_n
            if n > 0:
                t8_ref[0, (j + n) * LANES:(j + n + 1) * LANES, j * LANES:(j + 1) * LANES] = (
                    jnp.zeros(sq, BF16))
        q_r, q_i = _cmul(q_r, q_i, lc_r, lc_i)
        s_r, s_i = _cmul(c_r, c_i, q_r, q_i)
        wst_ref[0, n * LANES:(n + 1) * LANES, :] = jnp.concatenate(
            [expand(s_r), expand(-s_i)], axis=1).astype(BF16)
    p_r, p_i = l_r, l_i
    for _ in range(CHUNK.bit_length() - 1):
        p_r, p_i = _cmul(p_r, p_i, p_r, p_i)
    t_r, t_i = p_r, p_i
    for i in range(SUBLANES):
        pr_ref[0, i:i + 1, :] = t_r
        pi_ref[0, i:i + 1, :] = t_i
        t_r, t_i = _cmul(t_r, t_i, p_r, p_i)
    for k in range(SCAN_LEVELS):
        ar_ref[0, k:k + 1, :] = p_r
        ai_ref[0, k:k + 1, :] = p_i
        p_r, p_i = _cmul(p_r, p_i, p_r, p_i)
    lr_ref[0] = l_r
    li_ref[0] = l_i


def _ssm_tables(a_re, a_im, log_dt, b_re, b_im, c_re, c_im):
    gt = N_GROUP_TILES
    lane_vec = lambda v: v.reshape(gt, 1, STATE_TILE)
    dup = lambda v: jnp.concatenate([v, v], axis=-1)
    rows = GROUPS_PER_TILE * SSM_GROUP
    b_t = lambda v: dup(v.transpose(0, 2, 1).reshape(gt, rows, SSM_STATE))
    c_t = lambda v: dup(v.reshape(gt, rows, SSM_STATE))
    vec = pl.BlockSpec((1, 1, STATE_TILE), lambda g: (g, 0, 0))
    par = pl.BlockSpec((1, rows, LANES), lambda g: (g, 0, 0))
    big = pl.BlockSpec((1, CHUNK_WIDTH, CHUNK_WIDTH), lambda g: (g, 0, 0))
    wcs = pl.BlockSpec((1, LANES, 2 * STATE_TILE), lambda g: (g, 0, 0))
    scan = pl.BlockSpec((1, SCAN_LEVELS, STATE_TILE), lambda g: (g, 0, 0))
    tile = pl.BlockSpec((1, SUBLANES, STATE_TILE), lambda g: (g, 0, 0))
    return pl.pallas_call(
        _ssm_tables_kernel,
        grid=(gt,),
        in_specs=[vec, vec, vec, par, par, par, par],
        out_specs=[big, big, big, wcs, scan, scan, vec, vec, tile, tile],
        out_shape=[
            jax.ShapeDtypeStruct((gt, CHUNK_WIDTH, 2 * STATE_TILE), BF16),
            jax.ShapeDtypeStruct((gt, CHUNK_WIDTH, 2 * STATE_TILE), BF16),
            jax.ShapeDtypeStruct((gt, CHUNK_WIDTH, CHUNK_WIDTH), BF16),
            jax.ShapeDtypeStruct((gt, LANES, 2 * STATE_TILE), BF16),
            jax.ShapeDtypeStruct((gt, SCAN_LEVELS, STATE_TILE), F32),
            jax.ShapeDtypeStruct((gt, SCAN_LEVELS, STATE_TILE), F32),
            jax.ShapeDtypeStruct((gt, 1, STATE_TILE), F32),
            jax.ShapeDtypeStruct((gt, 1, STATE_TILE), F32),
            jax.ShapeDtypeStruct((gt, SUBLANES, STATE_TILE), F32),
            jax.ShapeDtypeStruct((gt, SUBLANES, STATE_TILE), F32),
        ],
        compiler_params=pltpu.CompilerParams(
            dimension_semantics=("arbitrary",), vmem_limit_bytes=VMEM_LIMIT),
        name="ssm_tables",
    )(lane_vec(a_re), lane_vec(a_im), lane_vec(jnp.repeat(log_dt, SSM_STATE)),
      b_t(b_re), b_t(b_im), c_t(c_re), c_t(c_im))


def kernel(x_prompt, x_sample, cache_win_k, cache_win_v, cache_meta_k, cache_meta_v,
           state_ssm_re, state_ssm_im, meta_tokens, norm_gain, w_in, q_norm_gain, k_norm_gain,
           sinks, a_re, a_im, log_dt, b_re, b_im, c_re, c_im, d_skip, w_glu, b_glu,
           w_attn_out, w_ssm_out, w_out):
    depth = w_in.shape[0]
    assert depth == 1, "single-layer trunk"
    batch, seq = x_prompt.shape[:2]
    n_s = x_sample.shape[0]
    assert x_sample.shape[1] == 1 and seq % (CHUNK * BLOCK) == 0
    l = 0

    w = w_in[l]
    qg = jnp.tile(q_norm_gain[l], TN // HEAD_DIM)[None]
    kg = jnp.tile(k_norm_gain[l], KV_WIDTH // HEAD_DIM)[None]
    lane = np.arange(LANES)
    e128 = np.where((lane[:, None] // HEAD_DIM) == (lane[None, :] // HEAD_DIM),
                    1.0 / HEAD_DIM, 0.0).astype(np.float32)
    e = jnp.asarray(np.concatenate([e128, e128], axis=0), BF16)
    gain = norm_gain[l][None]

    pos_p = N_META + np.arange(seq)
    pos_small = np.concatenate([np.full((n_s,), PAST_LEN), np.arange(N_META)])
    x_small = jnp.concatenate([x_sample[:, 0, :], meta_tokens.astype(x_prompt.dtype)], axis=0)
    xp = x_prompt.reshape(batch * seq, D_MODEL)
    xn_s = _rmsnorm(x_small, gain, tm=n_s + N_META)
    h_p, h_s, w_glu_bf, w_a_bf, w_s_bf, w_o_bf = _inproj(
        xp, gain, xn_s, w, qg, kg, e, _rope_tables(pos_p), _rope_tables(pos_small),
        (w_glu[l], w_attn_out[l], w_ssm_out[l], w_out[l]), tm=INPROJ_ROWS, halves=INPROJ_HALVES)

    win, wst, t8, wc, a_r, a_i, l_r, l_i, p_r, p_i = _ssm_tables(
        a_re[l], a_im[l], log_dt[l], b_re[l], b_im[l], c_re[l], c_im[l])
    dsk = d_skip[l][None]

    vm_t = h_s[n_s:, COL_KV + KV_WIDTH:IN_WIDTH].T
    o_p = _prompt_attention(sinks[l], h_p, h_s, vm_t, batch, seq)
    q_s = h_s[:n_s, COL_Q:COL_Q + ATTN_WIDTH].reshape(n_s, N_Q_HEADS, 1, HEAD_DIM)
    sel = (np.arange(N_Q_HEADS)[:, None] // (N_Q_HEADS // N_KV_HEADS)
           == np.arange(N_KV_HEADS)[None, :]).astype(np.float32)
    qp = (q_s * sel[None, :, :, None]).reshape(n_s, N_Q_HEADS, KV_WIDTH)
    bt = SAMPLE_ATTN_BATCH
    to_t = lambda c: c.transpose(0, 2, 3, 1).reshape(n_s, KV_WIDTH, WINDOW)
    from_t = lambda c: c.reshape(n_s, N_KV_HEADS, HEAD_DIM, WINDOW).transpose(0, 3, 1, 2)[None]
    new_t = lambda c0: h_s[:n_s, c0:c0 + KV_WIDTH].reshape(n_s // bt, bt, KV_WIDTH).transpose(0, 2, 1)
    o_s, s_win_k, s_win_v = _sample_attention(
        qp, new_t(COL_KV), new_t(COL_KV + KV_WIDTH), h_s, to_t(cache_win_k[l]), to_t(cache_win_v[l]),
        cache_meta_k[l].reshape(n_s, N_META, KV_WIDTH), cache_meta_v[l].reshape(n_s, N_META, KV_WIDTH),
        sinks[l][:, None], bt=bt)
    o_s = o_s[:, :, :HEAD_DIM].reshape(n_s, ATTN_WIDTH)

    y_p, p_re, p_im, y_s, s_re, s_im = _ssm(
        h_p, h_s, state_ssm_re[l].reshape(n_s, -1), state_ssm_im[l].reshape(n_s, -1),
        t8, win, wst, wc, a_r, a_i, l_r, l_i, p_r, p_i, dsk, batch, n_s)
    y_prompt, y_sample = _merge(
        o_p, h_p, y_p, xp, o_s, h_s, y_s, x_sample[:, 0, :], b_glu[l][None],
        w_glu_bf, w_a_bf, w_s_bf, w_o_bf, tm=MERGE_ROWS)
    y_prompt = y_prompt.reshape(batch, seq, D_MODEL)
    y_sample = y_sample.reshape(n_s, 1, D_MODEL)

    kv_p = h_p.reshape(batch, seq, IN_WIDTH)[:, seq - WINDOW:, COL_KV:]
    kv_p = kv_p.reshape(batch, WINDOW, 2, N_KV_HEADS, HEAD_DIM)
    p_win_k = kv_p[:, :, 0][None]
    p_win_v = kv_p[:, :, 1][None]
    kv_m = h_s[n_s:, COL_KV:IN_WIDTH].reshape(N_META, 2, N_KV_HEADS, HEAD_DIM)
    p_meta_k = jnp.broadcast_to(kv_m[None, :, 0], (batch, N_META, N_KV_HEADS, HEAD_DIM))[None]
    p_meta_v = jnp.broadcast_to(kv_m[None, :, 1], (batch, N_META, N_KV_HEADS, HEAD_DIM))[None]
    p_ssm_re = p_re.reshape(batch, N_SSM_GROUPS, SSM_STATE)[None]
    p_ssm_im = p_im.reshape(batch, N_SSM_GROUPS, SSM_STATE)[None]
    st_shape = (1, n_s, N_SSM_GROUPS, SSM_STATE)
    return (y_prompt, y_sample, p_win_k, p_win_v, p_meta_k, p_meta_v, p_ssm_re, p_ssm_im,
            from_t(s_win_k), from_t(s_win_v), s_re.reshape(st_shape), s_im.reshape(st_shape))
```

```python
import functools

import jax
import jax.numpy as jnp
import numpy as np
from jax import lax
from jax.experimental import pallas as pl
from jax.experimental.pallas import tpu as pltpu

F32 = jnp.float32
BF16 = jnp.bfloat16

D_MODEL = 2048
N_META = 16
HEAD_DIM = 64
N_Q_HEADS = 16
N_KV_HEADS = 4
ATTN_WIDTH = N_Q_HEADS * HEAD_DIM
KV_WIDTH = N_KV_HEADS * HEAD_DIM
WINDOW = 128
BLOCK = 128
ROPE_THETA = 10000.0
SSM_WIDTH = D_MODEL // 2
SSM_GROUP = 16
N_SSM_GROUPS = SSM_WIDTH // SSM_GROUP
SSM_STATE = 64
EPS = 1e-6
PAST_LEN = 8192

LANES = 128
SUBLANES = 8
CHUNK = 8
GROUPS_PER_TILE = LANES // SSM_GROUP
N_GROUP_TILES = N_SSM_GROUPS // GROUPS_PER_TILE
CHUNK_WIDTH = CHUNK * LANES
STATE_TILE = GROUPS_PER_TILE * SSM_STATE
SCAN_LEVELS = 8

COL_Q = 0
COL_ZA = 1024
COL_U = 2048
COL_ZS = 3072
COL_GA = 4096
COL_GS = 6144
COL_KV = 8192
IN_WIDTH = 8704
TN = 512
CAST_ROWS = 16
INPROJ_ROWS = 1024
INPROJ_HALVES = 2
MERGE_ROWS = 256
SAMPLE_ATTN_BATCH = 16
ATTN_BLOCKS_PER_STEP = 8
KV_TILE = COL_KV // TN
VMEM_LIMIT = 56 * 1024 * 1024
NEG = -1e30
LOG2E = 1.4426950408889634
Q_SCALE = HEAD_DIM ** -0.5 * LOG2E
NT_DIMS = (((1,), (1,)), ((), ()))


def _cmul(ar, ai, br, bi):
    return ar * br - ai * bi, ar * bi + ai * br


def _rmsnorm_kernel(x_ref, gain_ref, o_ref):
    x = x_ref[...]
    r = lax.rsqrt(jnp.mean(x * x, axis=-1, keepdims=True) + EPS)
    o_ref[...] = (x * r * gain_ref[...]).astype(BF16)


def _rmsnorm(x, gain, tm):
    m = x.shape[0]
    return pl.pallas_call(
        _rmsnorm_kernel,
        grid=(m // tm,),
        in_specs=[pl.BlockSpec((tm, D_MODEL), lambda i: (i, 0)),
                  pl.BlockSpec((1, D_MODEL), lambda i: (0, 0))],
        out_specs=pl.BlockSpec((tm, D_MODEL), lambda i: (i, 0)),
        out_shape=jax.ShapeDtypeStruct((m, D_MODEL), BF16),
        compiler_params=pltpu.CompilerParams(
            dimension_semantics=("arbitrary",), vmem_limit_bytes=VMEM_LIMIT),
        name="rmsnorm",
    )(x, gain)


def _inproj_epilogue(j, o_ref, qg_ref, kg_ref, e_ref, rope_ref):
    def head_norm_rope(n_chunks, gain_ref, scale):
        cos, sa, sb = (rope_ref[:, t * LANES:(t + 1) * LANES] for t in range(3))
        for c in range(n_chunks):
            cs = slice(c * LANES, (c + 1) * LANES)
            ac = o_ref[:, cs]
            sq = ac * ac
            hi = sq.astype(BF16)
            lo = (sq - hi.astype(F32)).astype(BF16)
            ms = jnp.dot(jnp.concatenate([hi, lo], axis=1), e_ref[...],
                         preferred_element_type=F32)
            xc = ac * lax.rsqrt(ms + EPS) * gain_ref[:, cs]
            o_ref[:, cs] = (xc * cos + pltpu.roll(xc, LANES - HEAD_DIM // 2, 1) * sa
                            + pltpu.roll(xc, HEAD_DIM // 2, 1) * sb) * scale

    @pl.when(j < COL_ZA // TN)
    def _():
        head_norm_rope(TN // LANES, qg_ref, Q_SCALE)

    @pl.when(j == KV_TILE)
    def _():
        head_norm_rope(KV_WIDTH // LANES, kg_ref, 1.0)


def _inproj_kernel(x_ref, gain_ref, xs_ref, w_ref, qg_ref, kg_ref, e_ref, rope_ref, rope_s_ref,
                   *rest, mw_slabs):
    n_mw = (len(rest) - 4) // 2
    mw_refs, (o_ref, os_ref) = rest[:n_mw], rest[n_mw:n_mw + 2]
    mw_out_refs, (xn_ref, wbf_ref) = rest[n_mw + 2:2 * n_mw + 2], rest[2 * n_mw + 2:]
    h, j, i = pl.program_id(0), pl.program_id(1), pl.program_id(2)
    tm = o_ref.shape[0]
    rows = pl.ds(pl.multiple_of(i * tm, tm), tm)

    @pl.when(j == 0)
    def _():
        x = x_ref[...]
        r = lax.rsqrt(jnp.mean(x * x, axis=-1, keepdims=True) + EPS)
        xn_ref[rows, :] = (x * r * gain_ref[...]).astype(BF16)

    @pl.when(i == 0)
    def _():
        wbf_ref[...] = w_ref[...].astype(BF16)

    o_ref[...] = jnp.dot(xn_ref[rows, :], wbf_ref[...], preferred_element_type=F32)
    _inproj_epilogue(j, o_ref, qg_ref, kg_ref, e_ref, rope_ref)

    step = (h * pl.num_programs(1) + j) * pl.num_programs(2) + i
    for src, dst, n_slabs in zip(mw_refs, mw_out_refs, mw_slabs):
        @pl.when(step < n_slabs)
        def _():
            dst[...] = src[...].astype(BF16)

    @pl.when(jnp.logical_and(h == 0, jnp.logical_and(j == 0, i == 0)))
    def _():
        os_ref[...] = jnp.zeros(os_ref.shape, F32)

    @pl.when(jnp.logical_and(h == pl.num_programs(0) - 1, i == pl.num_programs(2) - 1))
    def _():
        os_ref[...] = jnp.dot(xs_ref[...], wbf_ref[...], preferred_element_type=F32)
        _inproj_epilogue(j, os_ref, qg_ref, kg_ref, e_ref, rope_s_ref)


def _w_tile(j):
    n_q = COL_ZA // TN
    return jnp.where(j < n_q, j, jnp.where(j == KV_TILE, n_q, j + 1))


def _inproj(x, gain, xs, w, qg, kg, e, rope_p, rope_s, merge_weights, tm, halves):
    m, ms = x.shape[0], xs.shape[0]
    rows_half = m // halves
    tiles = rows_half // tm
    n_tab = rope_p.shape[0] // tm
    n_col = IN_WIDTH // TN
    const = lambda shape: pl.BlockSpec(shape, lambda h, j, i: (0, 0))
    n_steps = halves * n_col * tiles
    for mw in merge_weights:
        assert mw.shape[0] % CAST_ROWS == 0 and mw.shape[0] // CAST_ROWS <= n_steps
    slab = lambda mw: pl.BlockSpec(
        (CAST_ROWS, mw.shape[1]),
        lambda h, j, i: (jnp.minimum((h * n_col + j) * tiles + i, mw.shape[0] // CAST_ROWS - 1), 0))
    return pl.pallas_call(
        functools.partial(_inproj_kernel,
                          mw_slabs=tuple(mw.shape[0] // CAST_ROWS for mw in merge_weights)),
        grid=(halves, IN_WIDTH // TN, tiles),
        in_specs=[
            pl.BlockSpec((tm, D_MODEL),
                         lambda h, j, i: (h * tiles + jnp.where(j == 0, i, tiles - 1), 0)),
            const((1, D_MODEL)),
            const((ms, D_MODEL)),
            pl.BlockSpec((D_MODEL, TN), lambda h, j, i: (0, _w_tile(j))),
            const((1, TN)), const((1, KV_WIDTH)), const((2 * LANES, LANES)),
            pl.BlockSpec((tm, 3 * LANES), lambda h, j, i: ((h * tiles + i) % n_tab, 0)),
            const((ms, 3 * LANES)),
        ] + [slab(mw) for mw in merge_weights],
        out_specs=[pl.BlockSpec((tm, TN), lambda h, j, i: (h * tiles + i, j)),
                   pl.BlockSpec((ms, TN),
                                lambda h, j, i: (0, jnp.where(h == halves - 1, j, n_col)))
                   ] + [slab(mw) for mw in merge_weights],
        out_shape=[jax.ShapeDtypeStruct((m, IN_WIDTH), F32),
                   jax.ShapeDtypeStruct((ms, IN_WIDTH + TN), F32)
                   ] + [jax.ShapeDtypeStruct(mw.shape, BF16) for mw in merge_weights],
        scratch_shapes=[pltpu.VMEM((rows_half, D_MODEL), BF16), pltpu.VMEM((D_MODEL, TN), BF16)],
        compiler_params=pltpu.CompilerParams(
            dimension_semantics=("arbitrary", "arbitrary", "arbitrary"),
            vmem_limit_bytes=VMEM_LIMIT),
        name="inproj",
    )(x, gain, xs, w, qg, kg, e, rope_p, rope_s, *merge_weights)


def _dup_head(x, g, lo):
    r = pltpu.roll(x, HEAD_DIM, 1)
    return jnp.where(lo, x, r) if g % 2 == 0 else jnp.where(lo, r, x)


def _prompt_attn_kernel(sinks_ref, q_ref, kvc_ref, kvp_ref, kvm_ref, vmt_ref, bias_ref, o_ref):
    n_blk = q_ref.shape[0] // BLOCK
    lo = lax.broadcasted_iota(jnp.int32, (1, LANES), 1) < HEAD_DIM
    head_lane = lax.broadcasted_iota(jnp.int32, (1, 4 * BLOCK), 1) // BLOCK

    def prev_rows(blk, cols):
        if blk == 0:
            return kvp_ref[:, cols]
        return kvc_ref[(blk - 1) * BLOCK:blk * BLOCK, cols]

    scores = []
    for blk in range(n_blk):
        rows = slice(blk * BLOCK, (blk + 1) * BLOCK)
        for g in range(N_KV_HEADS):
            ks = slice((g // 2) * LANES, (g // 2 + 1) * LANES)
            k_all = jnp.concatenate(
                [_dup_head(prev_rows(blk, ks), g, lo), _dup_head(kvc_ref[rows, ks], g, lo),
                 _dup_head(kvm_ref[:, ks], g, lo)], axis=0).astype(BF16)
            qa = q_ref[rows, (2 * g) * LANES:(2 * g + 1) * LANES]
            qb = q_ref[rows, (2 * g + 1) * LANES:(2 * g + 2) * LANES]
            q4 = jnp.concatenate([jnp.where(lo, qa, 0.0), jnp.where(lo, 0.0, qa),
                                  jnp.where(lo, qb, 0.0), jnp.where(lo, 0.0, qb)],
                                 axis=0).astype(BF16)
            scores.append(lax.dot_general(k_all, q4, NT_DIMS, preferred_element_type=F32))
    for blk in range(n_blk):
        rows = slice(blk * BLOCK, (blk + 1) * BLOCK)
        bias = bias_ref[jnp.minimum(pl.program_id(1), 1)] if blk == 0 else bias_ref[1]
        vs = slice(KV_WIDTH, 2 * KV_WIDTH)
        v_t = jnp.concatenate([prev_rows(blk, vs).T, kvc_ref[rows, vs].T], axis=1)
        for g in range(N_KV_HEADS):
            s = scores[blk * N_KV_HEADS + g]
            s_w = s[:2 * BLOCK] + bias
            s_m = s[2 * BLOCK:]
            sink = LOG2E * jnp.where(head_lane == 0, sinks_ref[4 * g],
                                     jnp.where(head_lane == 1, sinks_ref[4 * g + 1],
                                               jnp.where(head_lane == 2, sinks_ref[4 * g + 2],
                                                         sinks_ref[4 * g + 3])))
            m = jnp.maximum(jnp.maximum(jnp.max(s_w, axis=0, keepdims=True),
                                        jnp.max(s_m, axis=0, keepdims=True)), sink)
            p_w = jnp.exp2(s_w - m)
            p_m = jnp.exp2(s_m - m)
            den = (jnp.sum(p_w, axis=0, keepdims=True) + jnp.sum(p_m, axis=0, keepdims=True)
                   + jnp.exp2(sink - m))
            p = jnp.concatenate([p_w, p_m], axis=0).astype(BF16)
            hs = slice(g * HEAD_DIM, (g + 1) * HEAD_DIM)
            vt_g = jnp.concatenate([v_t[hs], vmt_ref[hs, :]], axis=1).astype(BF16)
            o_t = jnp.dot(vt_g, p, preferred_element_type=F32) * (1.0 / den)
            for pair in range(2):
                two = jnp.concatenate(
                    [o_t[:, (2 * pair) * BLOCK:(2 * pair + 1) * BLOCK],
                     o_t[:, (2 * pair + 1) * BLOCK:(2 * pair + 2) * BLOCK]], axis=0)
                o_ref[rows, (2 * g + pair) * LANES:(2 * g + pair + 1) * LANES] = two.T


def _prompt_attention(sinks, h_p, h_s, vm_t, batch, seq):
    nb = seq // BLOCK
    kvb = COL_KV // (2 * KV_WIDTH)
    kj = np.arange(2 * BLOCK)[:, None]
    qi = np.arange(4 * BLOCK)[None, :] % BLOCK
    cur = (kj >= BLOCK) & (kj - BLOCK <= qi)
    prev = (kj < BLOCK) & (kj > qi)
    bias = jnp.asarray(np.where(np.stack([cur, cur | prev]), 0.0, NEG), F32)
    per = ATTN_BLOCKS_PER_STEP
    ns = nb // per
    return pl.pallas_call(
        _prompt_attn_kernel,
        grid=(batch, ns),
        in_specs=[
            pl.BlockSpec(memory_space=pltpu.SMEM),
            pl.BlockSpec((per * BLOCK, ATTN_WIDTH), lambda b, i: (b * ns + i, 0)),
            pl.BlockSpec((per * BLOCK, 2 * KV_WIDTH), lambda b, i: (b * ns + i, kvb)),
            pl.BlockSpec((BLOCK, 2 * KV_WIDTH),
                         lambda b, i: (b * nb + jnp.maximum(i * per - 1, 0), kvb)),
            pl.BlockSpec((N_META, 2 * KV_WIDTH), lambda b, i: (h_s.shape[0] // N_META - 1, kvb)),
            pl.BlockSpec((KV_WIDTH, N_META), lambda b, i: (0, 0)),
            pl.BlockSpec((2, 2 * BLOCK, 4 * BLOCK), lambda b, i: (0, 0, 0)),
        ],
        out_specs=pl.BlockSpec((per * BLOCK, ATTN_WIDTH), lambda b, i: (b * ns + i, 0)),
        out_shape=jax.ShapeDtypeStruct((batch * seq, ATTN_WIDTH), F32),
        compiler_params=pltpu.CompilerParams(
            dimension_semantics=("arbitrary", "arbitrary"), vmem_limit_bytes=VMEM_LIMIT),
        name="prompt_attn",
    )(sinks, h_p, h_p, h_p, h_s, vm_t, bias)


def _sample_attn_kernel(qp_ref, kn_ref, vn_ref, knr_ref, vnr_ref, ck_ref, cv_ref, mk_ref, mv_ref,
                        sink_ref, o_ref, ok_ref, ov_ref):
    bt = ck_ref.shape[0]
    slot = lax.broadcasted_iota(jnp.int32, (1, 1, WINDOW), 2)
    round_bf16 = lambda a: a.astype(BF16).astype(F32)
    qp = qp_ref[...].astype(BF16)
    ck = ck_ref[...]
    cv = cv_ref[...]
    k_m = mk_ref[...].astype(BF16)
    v_m = mv_ref[...].astype(BF16)
    s_w = jnp.einsum("bhc,bcw->bhw", qp, ck.astype(BF16), preferred_element_type=F32)
    s_w = jnp.where(slot == 0, NEG, s_w)
    s_m = jnp.einsum("bhc,bkc->bhk", qp, k_m, preferred_element_type=F32)
    knr, vnr = round_bf16(knr_ref[...]), round_bf16(vnr_ref[...])
    qf = qp.astype(F32)
    s_n = jnp.stack([jnp.sum(qf[b] * knr[b:b + 1, :], axis=-1, keepdims=True)
                     for b in range(bt)])
    sink = LOG2E * sink_ref[...][None]
    m = jnp.maximum(jnp.maximum(jnp.max(s_w, axis=-1, keepdims=True),
                                jnp.max(s_m, axis=-1, keepdims=True)),
                    jnp.maximum(s_n, sink))
    p_w = jnp.exp2(s_w - m)
    p_m = jnp.exp2(s_m - m)
    p_n = jnp.exp2(s_n - m)
    den = (jnp.sum(p_w, axis=-1, keepdims=True) + jnp.sum(p_m, axis=-1, keepdims=True)
           + p_n + jnp.exp2(sink - m))
    inv = 1.0 / den
    o = (jnp.einsum("bhw,bcw->bhc", (p_w * inv).astype(BF16), cv.astype(BF16),
                    preferred_element_type=F32)
         + jnp.einsum("bhk,bkc->bhc", (p_m * inv).astype(BF16), v_m,
                      preferred_element_type=F32)
         + jnp.stack([round_bf16(p_n[b] * inv[b]) * vnr[b:b + 1, :] for b in range(bt)]))

    last = lax.broadcasted_iota(jnp.int32, (1, WINDOW), 1) == WINDOW - 1
    kn = kn_ref[0]
    vn = vn_ref[0]
    ck = pltpu.roll(ck, WINDOW - 1, 2)
    cv = pltpu.roll(cv, WINDOW - 1, 2)
    ok_ref[...] = jnp.stack([jnp.where(last, kn[:, b:b + 1], ck[b]) for b in range(bt)])
    ov_ref[...] = jnp.stack([jnp.where(last, vn[:, b:b + 1], cv[b]) for b in range(bt)])
    hh = lax.broadcasted_iota(jnp.int32, (1, N_Q_HEADS, KV_WIDTH), 1) // (N_Q_HEADS // N_KV_HEADS)
    cc = lax.broadcasted_iota(jnp.int32, (1, N_Q_HEADS, KV_WIDTH), 2) // HEAD_DIM
    o = jnp.where(hh == cc, o, 0.0)
    o = o[:, :, :LANES] + o[:, :, LANES:]
    o_ref[...] = o + pltpu.roll(o, HEAD_DIM, 2)


def _sample_attention(qp, kn_t, vn_t, h_s, cache_k, cache_v, meta_k, meta_v, sink_col, bt):
    n = cache_k.shape[0]
    kvb = COL_KV // KV_WIDTH
    win_spec = pl.BlockSpec((bt, KV_WIDTH, WINDOW), lambda t: (t, 0, 0))
    new_spec = pl.BlockSpec((1, KV_WIDTH, bt), lambda t: (t, 0, 0))
    meta_spec = pl.BlockSpec((bt, N_META, KV_WIDTH), lambda t: (t, 0, 0))
    return pl.pallas_call(
        _sample_attn_kernel,
        grid=(n // bt,),
        in_specs=[
            pl.BlockSpec((bt, N_Q_HEADS, KV_WIDTH), lambda t: (t, 0, 0)),
            new_spec, new_spec,
            pl.BlockSpec((bt, KV_WIDTH), lambda t: (t, kvb)),
            pl.BlockSpec((bt, KV_WIDTH), lambda t: (t, kvb + 1)),
            win_spec, win_spec, meta_spec, meta_spec,
            pl.BlockSpec((N_Q_HEADS, 1), lambda t: (0, 0)),
        ],
        out_specs=[
            pl.BlockSpec((bt, N_Q_HEADS, LANES), lambda t: (t, 0, 0)),
            win_spec, win_spec,
        ],
        out_shape=[
            jax.ShapeDtypeStruct((n, N_Q_HEADS, LANES), F32),
            jax.ShapeDtypeStruct(cache_k.shape, F32),
            jax.ShapeDtypeStruct(cache_v.shape, F32),
        ],
        compiler_params=pltpu.CompilerParams(
            dimension_semantics=("arbitrary",), vmem_limit_bytes=VMEM_LIMIT),
        name="sample_attn",
    )(qp, kn_t, vn_t, h_s, h_s, cache_k, cache_v, meta_k, meta_v, sink_col)


def _gelu_tanh(x):
    c = 0.7978845608028654
    half = 0.5 * x
    return half + half * jnp.tanh(x * (c + (c * 0.044715) * (x * x)))


def _ssm_kernel(u_ref, um_ref, us_ref, hr_ref, hi_ref, t8_ref, win_ref, wst_ref, wc_ref,
                ar_ref, ai_ref, lr_ref, li_ref, pr_ref, pi_ref, d_ref,
                y_ref, sre_ref, sim_ref, ys_ref, nr_ref, ni_ref, xr_ref, xi_ref):
    dsk = d_ref[...]
    us = us_ref[...]
    bu = jnp.dot(us.astype(BF16), win_ref[0, (CHUNK - 1) * LANES:, :],
                 preferred_element_type=F32)
    l_r, l_i = lr_ref[0], li_ref[0]
    h_r, h_i = hr_ref[...], hi_ref[...]
    n_r = l_r * h_r - l_i * h_i + bu[:, :STATE_TILE]
    n_i = l_r * h_i + l_i * h_r + bu[:, STATE_TILE:]
    nr_ref[...] = n_r
    ni_ref[...] = n_i
    hn = jnp.concatenate([n_r, n_i], axis=1).astype(BF16)
    ys_ref[...] = _gelu_tanh(
        lax.dot_general(hn, wc_ref[0], NT_DIMS, preferred_element_type=F32) + dsk * us)

    batch = sre_ref.shape[0]
    nc = u_ref.shape[0] // CHUNK // batch
    win = win_ref[0]
    dsk_row = jnp.concatenate([dsk] * CHUNK, axis=1)

    def chunks_of(b):
        r0 = b * nc * CHUNK
        return jnp.concatenate(
            [u_ref[pl.ds(r0 + j, nc, stride=CHUNK), :] for j in range(CHUNK)], axis=1)

    n_mc = um_ref.shape[0] // CHUNK
    um = jnp.concatenate([um_ref[pl.ds(j, n_mc, stride=CHUNK), :] for j in range(CHUNK)], axis=1)
    gm = jnp.dot(um.astype(BF16), win, preferred_element_type=F32)
    a_r, a_i = ar_ref[0, 0:1, :], ai_ref[0, 0:1, :]
    s0r, s0i = gm[0:1, :STATE_TILE], gm[0:1, STATE_TILE:]
    for c in range(1, n_mc):
        s0r, s0i = (a_r * s0r - a_i * s0i + gm[c:c + 1, :STATE_TILE],
                    a_r * s0i + a_i * s0r + gm[c:c + 1, STATE_TILE:])

    n_t = nc // SUBLANES
    row = lax.broadcasted_iota(jnp.int32, (nc, 1), 0)
    first = row == 0
    in_tile = row % SUBLANES
    t_row = lax.broadcasted_iota(jnp.int32, (n_t, 1), 0)
    first_t = t_row == 0
    n_local = SUBLANES.bit_length() - 1
    w_r, w_i = pr_ref[0], pi_ref[0]

    def tile_totals(scr_ref, x):
        parts = []
        for c in range(STATE_TILE // LANES):
            scr_ref[c] = x[:, c * LANES:(c + 1) * LANES]
            parts.append(scr_ref[c, pl.ds(SUBLANES - 1, n_t, stride=SUBLANES), :])
        return jnp.concatenate(parts, axis=1)

    def shifted(x, d, keep):
        if d < SUBLANES:
            return jnp.where(keep, pltpu.roll(x, d, 0), 0.0)
        return jnp.concatenate([jnp.zeros((d, STATE_TILE), F32), x[:x.shape[0] - d]], axis=0)

    ucat_all = jnp.concatenate([chunks_of(b) for b in range(batch)], axis=0)
    ub_all = ucat_all.astype(BF16)
    g_all = jnp.dot(ub_all, win, preferred_element_type=F32)
    y_intra_all = jnp.concatenate(
        [jnp.dot(ub_all[:, :(2 * m + 2) * LANES],
                 t8_ref[0, :(2 * m + 2) * LANES, 2 * m * LANES:(2 * m + 2) * LANES],
                 preferred_element_type=F32) for m in range(CHUNK // 2)],
        axis=1) + dsk_row * ucat_all
    for b in range(batch):
        g, y_intra = g_all[b * nc:(b + 1) * nc], y_intra_all[b * nc:(b + 1) * nc]
        gr, gi = g[:, :STATE_TILE], g[:, STATE_TILE:]
        for k in range(n_local):
            d = 1 << k
            k_r, k_i = ar_ref[0, k:k + 1, :], ai_ref[0, k:k + 1, :]
            sh_r, sh_i = shifted(gr, d, in_tile >= d), shifted(gi, d, in_tile >= d)
            gr, gi = gr + (k_r * sh_r - k_i * sh_i), gi + (k_r * sh_i + k_i * sh_r)
        tot_r, tot_i = tile_totals(xr_ref, gr), tile_totals(xi_ref, gi)
        k_r, k_i = ar_ref[0, n_local:n_local + 1, :], ai_ref[0, n_local:n_local + 1, :]
        tot_r = tot_r + jnp.where(first_t, k_r * s0r - k_i * s0i, 0.0)
        tot_i = tot_i + jnp.where(first_t, k_r * s0i + k_i * s0r, 0.0)
        for k in range(SCAN_LEVELS - n_local):
            d = 1 << k
            if d >= n_t:
                break
            k_r = ar_ref[0, n_local + k:n_local + k + 1, :]
            k_i = ai_ref[0, n_local + k:n_local + k + 1, :]
            sh_r, sh_i = shifted(tot_r, d, t_row >= d), shifted(tot_i, d, t_row >= d)
            tot_r, tot_i = tot_r + (k_r * sh_r - k_i * sh_i), tot_i + (k_r * sh_i + k_i * sh_r)
        sre_ref[b, 0] = tot_r[n_t - 1:n_t, :]
        sim_ref[b, 0] = tot_i[n_t - 1:n_t, :]
        c_r = jnp.where(first_t, s0r, pltpu.roll(tot_r, 1, 0))
        c_i = jnp.where(first_t, s0i, pltpu.roll(tot_i, 1, 0))
        full_r, full_i = [], []
        for t in range(n_t):
            rs = slice(t * SUBLANES, (t + 1) * SUBLANES)
            e_r, e_i = c_r[t:t + 1, :], c_i[t:t + 1, :]
            full_r.append(gr[rs] + (w_r * e_r - w_i * e_i))
            full_i.append(gi[rs] + (w_r * e_i + w_i * e_r))
        gr, gi = jnp.concatenate(full_r, axis=0), jnp.concatenate(full_i, axis=0)
        sp = jnp.concatenate([jnp.where(first, s0r, pltpu.roll(gr, 1, 0)),
                              jnp.where(first, s0i, pltpu.roll(gi, 1, 0))],
                             axis=1).astype(BF16)
        y2 = y_intra + lax.dot_general(sp, wst_ref[0], NT_DIMS, preferred_element_type=F32)
        for j in range(CHUNK):
            sl = slice(j * LANES, (j + 1) * LANES)
            y_ref[pl.ds(b * nc * CHUNK + j, nc, stride=CHUNK), :] = _gelu_tanh(y2[:, sl])


def _ssm(h_p, h_s, h0r, h0i, t8, win, wst, wc, a_r, a_i, l_r, l_i, p_r, p_i, d_skip, batch, n_s):
    ub = COL_U // LANES
    rows = h_p.shape[0]
    wspec = pl.BlockSpec((1, CHUNK_WIDTH, CHUNK_WIDTH), lambda gt: (gt, 0, 0))
    aspec = pl.BlockSpec((1, SCAN_LEVELS, STATE_TILE), lambda gt: (gt, 0, 0))
    sspec = pl.BlockSpec((batch, 1, 1, STATE_TILE), lambda gt: (0, gt, 0, 0))
    st = pl.BlockSpec((n_s, STATE_TILE), lambda gt: (0, gt))
    lam = pl.BlockSpec((1, 1, STATE_TILE), lambda gt: (gt, 0, 0))
    return pl.pallas_call(
        _ssm_kernel,
        grid=(N_GROUP_TILES,),
        in_specs=[
            pl.BlockSpec((rows, LANES), lambda gt: (0, ub + gt)),
            pl.BlockSpec((N_META, LANES), lambda gt: (h_s.shape[0] // N_META - 1, ub + gt)),
            pl.BlockSpec((n_s, LANES), lambda gt: (0, ub + gt)),
            st, st,
            wspec, wspec, wspec,
            pl.BlockSpec((1, LANES, 2 * STATE_TILE), lambda gt: (gt, 0, 0)),
            aspec, aspec, lam, lam,
            pl.BlockSpec((1, SUBLANES, STATE_TILE), lambda gt: (gt, 0, 0)),
            pl.BlockSpec((1, SUBLANES, STATE_TILE), lambda gt: (gt, 0, 0)),
            pl.BlockSpec((1, LANES), lambda gt: (0, gt)),
        ],
        out_specs=[pl.BlockSpec((rows, LANES), lambda gt: (0, gt)), sspec, sspec,
                   pl.BlockSpec((n_s, LANES), lambda gt: (0, gt)), st, st],
        out_shape=[
            jax.ShapeDtypeStruct((rows, SSM_WIDTH), F32),
            jax.ShapeDtypeStruct((batch, N_GROUP_TILES, 1, STATE_TILE), F32),
            jax.ShapeDtypeStruct((batch, N_GROUP_TILES, 1, STATE_TILE), F32),
            jax.ShapeDtypeStruct((n_s, SSM_WIDTH), F32),
            jax.ShapeDtypeStruct((n_s, N_SSM_GROUPS * SSM_STATE), F32),
            jax.ShapeDtypeStruct((n_s, N_SSM_GROUPS * SSM_STATE), F32),
        ],
        scratch_shapes=[pltpu.VMEM((STATE_TILE // LANES, rows // CHUNK // batch, LANES), F32)] * 2,
        compiler_params=pltpu.CompilerParams(
            dimension_semantics=("arbitrary",), vmem_limit_bytes=VMEM_LIMIT),
        name="ssm",
    )(h_p, h_s, h_s, h0r, h0i, t8, win, wst, wc, a_r, a_i, l_r, l_i, p_r, p_i, d_skip)


def _merge_kernel(o_ref, za_ref, zs_ref, ga_ref, gs_ref, y_ref, x_ref,
                  o2_ref, za2_ref, zs2_ref, ga2_ref, gs2_ref, y2_ref, x2_ref,
                  bglu_ref, wglu_ref, wa_ref, ws_ref, wo_ref, out_ref, out2_ref):
    def merge(o_r, za_r, zs_r, ga_r, gs_r, y_r, x_r, out_r):
        za = za_r[...]
        a_in = (o_r[...] * (za * jax.nn.sigmoid(za))).astype(BF16)
        gated_a = jax.nn.sigmoid(ga_r[...]) * jnp.dot(a_in, wa_ref[...],
                                                      preferred_element_type=F32)
        y = y_r[...]
        t = jnp.dot(y.astype(BF16), wglu_ref[...], preferred_element_type=F32) + bglu_ref[...]
        y = y * jax.nn.sigmoid(t)
        zs = zs_r[...]
        s_in = (y * (zs * jax.nn.sigmoid(zs))).astype(BF16)
        br_s = jnp.dot(s_in, ws_ref[...], preferred_element_type=F32)
        mix = (gated_a + jax.nn.sigmoid(gs_r[...]) * br_s).astype(BF16)
        out_r[...] = x_r[...] + jnp.dot(mix, wo_ref[...], preferred_element_type=F32)

    merge(o_ref, za_ref, zs_ref, ga_ref, gs_ref, y_ref, x_ref, out_ref)

    @pl.when(pl.program_id(0) == pl.num_programs(0) - 1)
    def _():
        merge(o2_ref, za2_ref, zs2_ref, ga2_ref, gs2_ref, y2_ref, x2_ref, out2_ref)


def _merge(o, h, y, x, o2, h2, y2, x2, b_glu, w_glu, w_a, w_s, w_o, tm):
    m, m2 = x.shape[0], x2.shape[0]
    const = lambda shape: pl.BlockSpec(shape, lambda i: (0, 0), pipeline_mode=pl.Buffered(1))
    rows = lambda width, col: pl.BlockSpec((tm, width), lambda i: (i, col))
    rows2 = lambda width, col: pl.BlockSpec((m2, width), lambda i: (0, col),
                                            pipeline_mode=pl.Buffered(1))
    operands = lambda spec: [
        spec(ATTN_WIDTH, 0), spec(ATTN_WIDTH, COL_ZA // ATTN_WIDTH),
        spec(SSM_WIDTH, COL_ZS // SSM_WIDTH), spec(D_MODEL, COL_GA // D_MODEL),
        spec(D_MODEL, COL_GS // D_MODEL), spec(SSM_WIDTH, 0), spec(D_MODEL, 0)]
    return pl.pallas_call(
        _merge_kernel,
        grid=(m // tm,),
        in_specs=operands(rows) + operands(rows2) + [
            const((1, SSM_WIDTH)),
            const((SSM_WIDTH, SSM_WIDTH)),
            const((ATTN_WIDTH, D_MODEL)),
            const((SSM_WIDTH, D_MODEL)),
            const((D_MODEL, D_MODEL)),
        ],
        out_specs=[pl.BlockSpec((tm, D_MODEL), lambda i: (i, 0)),
                   pl.BlockSpec((m2, D_MODEL), lambda i: (0, 0))],
        out_shape=[jax.ShapeDtypeStruct((m, D_MODEL), F32),
                   jax.ShapeDtypeStruct((m2, D_MODEL), F32)],
        compiler_params=pltpu.CompilerParams(
            dimension_semantics=("arbitrary",), vmem_limit_bytes=VMEM_LIMIT),
        name="merge",
    )(o, h, h, h, h, y, x, o2, h2, h2, h2, h2, y2, x2, b_glu, w_glu, w_a, w_s, w_o)


def _rope_tables(pos):
    half = HEAD_DIM // 2
    inv_freq = (ROPE_THETA ** (-np.arange(half, dtype=np.float32) / half)).astype(np.float32)
    ang = np.asarray(pos, np.float32)[:, None] * inv_freq[None, :]
    cos, sin = np.cos(ang).astype(np.float32), np.sin(ang).astype(np.float32)
    zero = np.zeros_like(sin)
    reps = LANES // HEAD_DIM
    cos_t = np.tile(np.concatenate([cos, cos], axis=1), (1, reps))
    sin_a = np.tile(np.concatenate([-sin, zero], axis=1), (1, reps))
    sin_b = np.tile(np.concatenate([zero, sin], axis=1), (1, reps))
    return jnp.asarray(np.concatenate([cos_t, sin_a, sin_b], axis=1))


def _split_bf16(x):
    hi = x.astype(BF16)
    return hi, (x - hi.astype(F32)).astype(BF16)


def _dot_nt_f32(a, b):
    ah, al = _split_bf16(a)
    bh, bl = _split_bf16(b)
    dot = lambda x, y: lax.dot_general(x, y, NT_DIMS, preferred_element_type=F32)
    return dot(ah, bh) + dot(ah, bl) + dot(al, bh)


def _ssm_tables_kernel(are_ref, aim_ref, ldt_ref, bre_ref, bim_ref, cre_ref, cim_ref,
                       win_ref, wst_ref, t8_ref, wc_ref, ar_ref, ai_ref, lr_ref, li_ref,
                       pr_ref, pi_ref):
    a_r, a_i = are_ref[0], aim_ref[0]
    dt = jnp.exp(ldt_ref[0])
    mag = jnp.exp(a_r * dt)
    l_r, l_i = mag * jnp.cos(a_i * dt), mag * jnp.sin(a_i * dt)
    den = a_r * a_r + a_i * a_i
    f_r = ((l_r - 1.0) * a_r + l_i * a_i) / den
    f_i = (l_i * a_r - (l_r - 1.0) * a_i) / den
    shape = (LANES, STATE_TILE)
    own = (lax.broadcasted_iota(jnp.int32, shape, 0) // SSM_GROUP
           == lax.broadcasted_iota(jnp.int32, shape, 1) // SSM_STATE)
    sq = (LANES, LANES)
    same_group = (lax.broadcasted_iota(jnp.int32, sq, 0) // SSM_GROUP
                  == lax.broadcasted_iota(jnp.int32, sq, 1) // SSM_GROUP)
    lo = lax.broadcasted_iota(jnp.int32, (1, LANES), 1) < SSM_STATE
    n_rep = STATE_TILE // LANES

    def expand(x):
        return jnp.where(own, jnp.concatenate([x] * n_rep, axis=1), 0.0)

    def compact(v):
        e = jnp.where(own, jnp.broadcast_to(v, shape), 0.0)
        s = e[:, :LANES]
        for k in range(1, n_rep):
            s = s + e[:, k * LANES:(k + 1) * LANES]
        return s + pltpu.roll(s, SSM_STATE, 1)

    lc_r, lc_i = compact(l_r), compact(l_i)
    b_r, b_i = bre_ref[0], bim_ref[0]
    c_r, c_i = cre_ref[0], cim_ref[0]
    bb_r, bb_i = _cmul(compact(f_r), compact(f_i), b_r, b_i)
    wc_ref[0] = jnp.concatenate([expand(c_r), expand(-c_i)], axis=1).astype(BF16)
    c_cat = jnp.where(lo, c_r, -c_i)
    q_r, q_i = jnp.ones(sq, F32), jnp.zeros(sq, F32)
    for n in range(CHUNK):
        x_r, x_i = _cmul(q_r, q_i, bb_r, bb_i)
        win_ref[0, (CHUNK - 1 - n) * LANES:(CHUNK - n) * LANES, :] = jnp.concatenate(
            [expand(x_r), expand(x_i)], axis=1).astype(BF16)
        k_n = _dot_nt_f32(jnp.where(lo, x_r, x_i), c_cat)
        k_n = jnp.where(same_group, k_n, 0.0).astype(BF16)
        for j in range(CHUNK - n):
            t8_ref[0, j * LANES:(j + 1) * LANES, (j + n) * LANES:(j + n + 1) * LANES] = k_n
            if n > 0:
                t8_ref[0, (j + n) * LANES:(j + n + 1) * LANES, j * LANES:(j + 1) * LANES] = (
                    jnp.zeros(sq, BF16))
        q_r, q_i = _cmul(q_r, q_i, lc_r, lc_i)
        s_r, s_i = _cmul(c_r, c_i, q_r, q_i)
        wst_ref[0, n * LANES:(n + 1) * LANES, :] = jnp.concatenate(
            [expand(s_r), expand(-s_i)], axis=1).astype(BF16)
    p_r, p_i = l_r, l_i
    for _ in range(CHUNK.bit_length() - 1):
        p_r, p_i = _cmul(p_r, p_i, p_r, p_i)
    t_r, t_i = p_r, p_i
    for i in range(SUBLANES):
        pr_ref[0, i:i + 1, :] = t_r
        pi_ref[0, i:i + 1, :] = t_i
        t_r, t_i = _cmul(t_r, t_i, p_r, p_i)
    for k in range(SCAN_LEVELS):
        ar_ref[0, k:k + 1, :] = p_r
        ai_ref[0, k:k + 1, :] = p_i
        p_r, p_i = _cmul(p_r, p_i, p_r, p_i)
    lr_ref[0] = l_r
    li_ref[0] = l_i


def _ssm_tables(a_re, a_im, log_dt, b_re, b_im, c_re, c_im):
    gt = N_GROUP_TILES
    lane_vec = lambda v: v.reshape(gt, 1, STATE_TILE)
    dup = lambda v: jnp.concatenate([v, v], axis=-1)
    rows = GROUPS_PER_TILE * SSM_GROUP
    b_t = lambda v: dup(v.transpose(0, 2, 1).reshape(gt, rows, SSM_STATE))
    c_t = lambda v: dup(v.reshape(gt, rows, SSM_STATE))
    vec = pl.BlockSpec((1, 1, STATE_TILE), lambda g: (g, 0, 0))
    par = pl.BlockSpec((1, rows, LANES), lambda g: (g, 0, 0))
    big = pl.BlockSpec((1, CHUNK_WIDTH, CHUNK_WIDTH), lambda g: (g, 0, 0))
    wcs = pl.BlockSpec((1, LANES, 2 * STATE_TILE), lambda g: (g, 0, 0))
    scan = pl.BlockSpec((1, SCAN_LEVELS, STATE_TILE), lambda g: (g, 0, 0))
    tile = pl.BlockSpec((1, SUBLANES, STATE_TILE), lambda g: (g, 0, 0))
    return pl.pallas_call(
        _ssm_tables_kernel,
        grid=(gt,),
        in_specs=[vec, vec, vec, par, par, par, par],
        out_specs=[big, big, big, wcs, scan, scan, vec, vec, tile, tile],
        out_shape=[
            jax.ShapeDtypeStruct((gt, CHUNK_WIDTH, 2 * STATE_TILE), BF16),
            jax.ShapeDtypeStruct((gt, CHUNK_WIDTH, 2 * STATE_TILE), BF16),
            jax.ShapeDtypeStruct((gt, CHUNK_WIDTH, CHUNK_WIDTH), BF16),
            jax.ShapeDtypeStruct((gt, LANES, 2 * STATE_TILE), BF16),
            jax.ShapeDtypeStruct((gt, SCAN_LEVELS, STATE_TILE), F32),
            jax.ShapeDtypeStruct((gt, SCAN_LEVELS, STATE_TILE), F32),
            jax.ShapeDtypeStruct((gt, 1, STATE_TILE), F32),
            jax.ShapeDtypeStruct((gt, 1, STATE_TILE), F32),
            jax.ShapeDtypeStruct((gt, SUBLANES, STATE_TILE), F32),
            jax.ShapeDtypeStruct((gt, SUBLANES, STATE_TILE), F32),
        ],
        compiler_params=pltpu.CompilerParams(
            dimension_semantics=("arbitrary",), vmem_limit_bytes=VMEM_LIMIT),
        name="ssm_tables",
    )(lane_vec(a_re), lane_vec(a_im), lane_vec(jnp.repeat(log_dt, SSM_STATE)),
      b_t(b_re), b_t(b_im), c_t(c_re), c_t(c_im))


def kernel(x_prompt, x_sample, cache_win_k, cache_win_v, cache_meta_k, cache_meta_v,
           state_ssm_re, state_ssm_im, meta_tokens, norm_gain, w_in, q_norm_gain, k_norm_gain,
           sinks, a_re, a_im, log_dt, b_re, b_im, c_re, c_im, d_skip, w_glu, b_glu,
           w_attn_out, w_ssm_out, w_out):
    depth = w_in.shape[0]
    assert depth == 1, "single-layer trunk"
    batch, seq = x_prompt.shape[:2]
    n_s = x_sample.shape[0]
    assert x_sample.shape[1] == 1 and seq % (CHUNK * BLOCK) == 0
    l = 0

    w = w_in[l]
    qg = jnp.tile(q_norm_gain[l], TN // HEAD_DIM)[None]
    kg = jnp.tile(k_norm_gain[l], KV_WIDTH // HEAD_DIM)[None]
    lane = np.arange(LANES)
    e128 = np.where((lane[:, None] // HEAD_DIM) == (lane[None, :] // HEAD_DIM),
                    1.0 / HEAD_DIM, 0.0).astype(np.float32)
    e = jnp.asarray(np.concatenate([e128, e128], axis=0), BF16)
    gain = norm_gain[l][None]

    pos_p = N_META + np.arange(seq)
    pos_small = np.concatenate([np.full((n_s,), PAST_LEN), np.arange(N_META)])
    x_small = jnp.concatenate([x_sample[:, 0, :], meta_tokens.astype(x_prompt.dtype)], axis=0)
    xp = x_prompt.reshape(batch * seq, D_MODEL)
    xn_s = _rmsnorm(x_small, gain, tm=n_s + N_META)
    h_p, h_s, w_glu_bf, w_a_bf, w_s_bf, w_o_bf = _inproj(
        xp, gain, xn_s, w, qg, kg, e, _rope_tables(pos_p), _rope_tables(pos_small),
        (w_glu[l], w_attn_out[l], w_ssm_out[l], w_out[l]), tm=INPROJ_ROWS, halves=INPROJ_HALVES)

    win, wst, t8, wc, a_r, a_i, l_r, l_i, p_r, p_i = _ssm_tables(
        a_re[l], a_im[l], log_dt[l], b_re[l], b_im[l], c_re[l], c_im[l])
    dsk = d_skip[l][None]

    vm_t = h_s[n_s:, COL_KV + KV_WIDTH:IN_WIDTH].T
    o_p = _prompt_attention(sinks[l], h_p, h_s, vm_t, batch, seq)
    q_s = h_s[:n_s, COL_Q:COL_Q + ATTN_WIDTH].reshape(n_s, N_Q_HEADS, 1, HEAD_DIM)
    sel = (np.arange(N_Q_HEADS)[:, None] // (N_Q_HEADS // N_KV_HEADS)
           == np.arange(N_KV_HEADS)[None, :]).astype(np.float32)
    qp = (q_s * sel[None, :, :, None]).reshape(n_s, N_Q_HEADS, KV_WIDTH)
    bt = SAMPLE_ATTN_BATCH
    to_t = lambda c: c.transpose(0, 2, 3, 1).reshape(n_s, KV_WIDTH, WINDOW)
    from_t = lambda c: c.reshape(n_s, N_KV_HEADS, HEAD_DIM, WINDOW).transpose(0, 3, 1, 2)[None]
    new_t = lambda c0: h_s[:n_s, c0:c0 + KV_WIDTH].reshape(n_s // bt, bt, KV_WIDTH).transpose(0, 2, 1)
    o_s, s_win_k, s_win_v = _sample_attention(
        qp, new_t(COL_KV), new_t(COL_KV + KV_WIDTH), h_s, to_t(cache_win_k[l]), to_t(cache_win_v[l]),
        cache_meta_k[l].reshape(n_s, N_META, KV_WIDTH), cache_meta_v[l].reshape(n_s, N_META, KV_WIDTH),
        sinks[l][:, None], bt=bt)
    o_s = o_s[:, :, :HEAD_DIM].reshape(n_s, ATTN_WIDTH)

    y_p, p_re, p_im, y_s, s_re, s_im = _ssm(
        h_p, h_s, state_ssm_re[l].reshape(n_s, -1), state_ssm_im[l].reshape(n_s, -1),
        t8, win, wst, wc, a_r, a_i, l_r, l_i, p_r, p_i, dsk, batch, n_s)
    y_prompt, y_sample = _merge(
        o_p, h_p, y_p, xp, o_s, h_s, y_s, x_sample[:, 0, :], b_glu[l][None],
        w_glu_bf, w_a_bf, w_s_bf, w_o_bf, tm=MERGE_ROWS)
    y_prompt = y_prompt.reshape(batch, seq, D_MODEL)
    y_sample = y_sample.reshape(n_s, 1, D_MODEL)

    kv_p = h_p.reshape(batch, seq, IN_WIDTH)[:, seq - WINDOW:, COL_KV:]
    kv_p = kv_p.reshape(batch, WINDOW, 2, N_KV_HEADS, HEAD_DIM)
    p_win_k = kv_p[:, :, 0][None]
    p_win_v = kv_p[:, :, 1][None]
    kv_m = h_s[n_s:, COL_KV:IN_WIDTH].reshape(N_META, 2, N_KV_HEADS, HEAD_DIM)
    p_meta_k = jnp.broadcast_to(kv_m[None, :, 0], (batch, N_META, N_KV_HEADS, HEAD_DIM))[None]
    p_meta_v = jnp.broadcast_to(kv_m[None, :, 1], (batch, N_META, N_KV_HEADS, HEAD_DIM))[None]
    p_ssm_re = p_re.reshape(batch, N_SSM_GROUPS, SSM_STATE)[None]
    p_ssm_im = p_im.reshape(batch, N_SSM_GROUPS, SSM_STATE)[None]
    st_shape = (1, n_s, N_SSM_GROUPS, SSM_STATE)
    return (y_prompt, y_sample, p_win_k, p_win_v, p_meta_k, p_meta_v, p_ssm_re, p_ssm_im,
            from_t(s_win_k), from_t(s_win_v), s_re.reshape(st_shape), s_im.reshape(st_shape))
```

```python
import functools

import jax
import jax.numpy as jnp
import numpy as np
from jax import lax
from jax.experimental import pallas as pl
from jax.experimental.pallas import tpu as pltpu

F32 = jnp.float32
BF16 = jnp.bfloat16

D_MODEL = 2048
N_META = 16
HEAD_DIM = 64
N_Q_HEADS = 16
N_KV_HEADS = 4
ATTN_WIDTH = N_Q_HEADS * HEAD_DIM
KV_WIDTH = N_KV_HEADS * HEAD_DIM
WINDOW = 128
BLOCK = 128
ROPE_THETA = 10000.0
SSM_WIDTH = D_MODEL // 2
SSM_GROUP = 16
N_SSM_GROUPS = SSM_WIDTH // SSM_GROUP
SSM_STATE = 64
EPS = 1e-6
PAST_LEN = 8192

LANES = 128
SUBLANES = 8
CHUNK = 8
GROUPS_PER_TILE = LANES // SSM_GROUP
N_GROUP_TILES = N_SSM_GROUPS // GROUPS_PER_TILE
CHUNK_WIDTH = CHUNK * LANES
STATE_TILE = GROUPS_PER_TILE * SSM_STATE
SCAN_LEVELS = 8

COL_Q = 0
COL_ZA = 1024
COL_U = 2048
COL_ZS = 3072
COL_GA = 4096
COL_GS = 6144
COL_KV = 8192
IN_WIDTH = 8704
TN = 512
CAST_ROWS = 16
INPROJ_ROWS = 1024
INPROJ_HALVES = 2
MERGE_ROWS = 256
SAMPLE_ATTN_BATCH = 32
ATTN_BLOCKS_PER_STEP = 8
KV_TILE = COL_KV // TN
VMEM_LIMIT = 56 * 1024 * 1024
NEG = -1e30
LOG2E = 1.4426950408889634
Q_SCALE = HEAD_DIM ** -0.5 * LOG2E
NT_DIMS = (((1,), (1,)), ((), ()))


def _cmul(ar, ai, br, bi):
    return ar * br - ai * bi, ar * bi + ai * br


def _rmsnorm_kernel(x_ref, gain_ref, o_ref):
    x = x_ref[...]
    r = lax.rsqrt(jnp.mean(x * x, axis=-1, keepdims=True) + EPS)
    o_ref[...] = (x * r * gain_ref[...]).astype(BF16)


def _rmsnorm(x, gain, tm):
    m = x.shape[0]
    return pl.pallas_call(
        _rmsnorm_kernel,
        grid=(m // tm,),
        in_specs=[pl.BlockSpec((tm, D_MODEL), lambda i: (i, 0)),
                  pl.BlockSpec((1, D_MODEL), lambda i: (0, 0))],
        out_specs=pl.BlockSpec((tm, D_MODEL), lambda i: (i, 0)),
        out_shape=jax.ShapeDtypeStruct((m, D_MODEL), BF16),
        compiler_params=pltpu.CompilerParams(
            dimension_semantics=("arbitrary",), vmem_limit_bytes=VMEM_LIMIT),
        name="rmsnorm",
    )(x, gain)


def _inproj_epilogue(j, o_ref, qg_ref, kg_ref, e_ref, rope_ref):
    def head_norm_rope(n_chunks, gain_ref, scale):
        cos, sa, sb = (rope_ref[:, t * LANES:(t + 1) * LANES] for t in range(3))
        for c in range(n_chunks):
            cs = slice(c * LANES, (c + 1) * LANES)
            ac = o_ref[:, cs]
            sq = ac * ac
            hi = sq.astype(BF16)
            lo = (sq - hi.astype(F32)).astype(BF16)
            ms = jnp.dot(jnp.concatenate([hi, lo], axis=1), e_ref[...],
                         preferred_element_type=F32)
            xc = ac * lax.rsqrt(ms + EPS) * gain_ref[:, cs]
            o_ref[:, cs] = (xc * cos + pltpu.roll(xc, LANES - HEAD_DIM // 2, 1) * sa
                            + pltpu.roll(xc, HEAD_DIM // 2, 1) * sb) * scale

    @pl.when(j < COL_ZA // TN)
    def _():
        head_norm_rope(TN // LANES, qg_ref, Q_SCALE)

    @pl.when(j == KV_TILE)
    def _():
        head_norm_rope(KV_WIDTH // LANES, kg_ref, 1.0)


def _inproj_kernel(x_ref, gain_ref, xs_ref, w_ref, qg_ref, kg_ref, e_ref, rope_ref, rope_s_ref,
                   *rest, mw_slabs):
    n_mw = (len(rest) - 4) // 2
    mw_refs, (o_ref, os_ref) = rest[:n_mw], rest[n_mw:n_mw + 2]
    mw_out_refs, (xn_ref, wbf_ref) = rest[n_mw + 2:2 * n_mw + 2], rest[2 * n_mw + 2:]
    h, j, i = pl.program_id(0), pl.program_id(1), pl.program_id(2)
    tm = o_ref.shape[0]
    rows = pl.ds(pl.multiple_of(i * tm, tm), tm)

    @pl.when(j == 0)
    def _():
        x = x_ref[...]
        r = lax.rsqrt(jnp.mean(x * x, axis=-1, keepdims=True) + EPS)
        xn_ref[rows, :] = (x * r * gain_ref[...]).astype(BF16)

    @pl.when(i == 0)
    def _():
        wbf_ref[...] = w_ref[...].astype(BF16)

    o_ref[...] = jnp.dot(xn_ref[rows, :], wbf_ref[...], preferred_element_type=F32)
    _inproj_epilogue(j, o_ref, qg_ref, kg_ref, e_ref, rope_ref)

    step = (h * pl.num_programs(1) + j) * pl.num_programs(2) + i
    for src, dst, n_slabs in zip(mw_refs, mw_out_refs, mw_slabs):
        @pl.when(step < n_slabs)
        def _():
            dst[...] = src[...].astype(BF16)

    @pl.when(jnp.logical_and(h == 0, jnp.logical_and(j == 0, i == 0)))
    def _():
        os_ref[...] = jnp.zeros(os_ref.shape, F32)

    @pl.when(jnp.logical_and(h == pl.num_programs(0) - 1, i == pl.num_programs(2) - 1))
    def _():
        os_ref[...] = jnp.dot(xs_ref[...], wbf_ref[...], preferred_element_type=F32)
        _inproj_epilogue(j, os_ref, qg_ref, kg_ref, e_ref, rope_s_ref)


def _w_tile(j):
    n_q = COL_ZA // TN
    return jnp.where(j < n_q, j, jnp.where(j == KV_TILE, n_q, j + 1))


def _inproj(x, gain, xs, w, qg, kg, e, rope_p, rope_s, merge_weights, tm, halves):
    m, ms = x.shape[0], xs.shape[0]
    rows_half = m // halves
    tiles = rows_half // tm
    n_tab = rope_p.shape[0] // tm
    n_col = IN_WIDTH // TN
    const = lambda shape: pl.BlockSpec(shape, lambda h, j, i: (0, 0))
    n_steps = halves * n_col * tiles
    for mw in merge_weights:
        assert mw.shape[0] % CAST_ROWS == 0 and mw.shape[0] // CAST_ROWS <= n_steps
    slab = lambda mw: pl.BlockSpec(
        (CAST_ROWS, mw.shape[1]),
        lambda h, j, i: (jnp.minimum((h * n_col + j) * tiles + i, mw.shape[0] // CAST_ROWS - 1), 0))
    return pl.pallas_call(
        functools.partial(_inproj_kernel,
                          mw_slabs=tuple(mw.shape[0] // CAST_ROWS for mw in merge_weights)),
        grid=(halves, IN_WIDTH // TN, tiles),
        in_specs=[
            pl.BlockSpec((tm, D_MODEL),
                         lambda h, j, i: (h * tiles + jnp.where(j == 0, i, tiles - 1), 0)),
            const((1, D_MODEL)),
            const((ms, D_MODEL)),
            pl.BlockSpec((D_MODEL, TN), lambda h, j, i: (0, _w_tile(j))),
            const((1, TN)), const((1, KV_WIDTH)), const((2 * LANES, LANES)),
            pl.BlockSpec((tm, 3 * LANES), lambda h, j, i: ((h * tiles + i) % n_tab, 0)),
            const((ms, 3 * LANES)),
        ] + [slab(mw) for mw in merge_weights],
        out_specs=[pl.BlockSpec((tm, TN), lambda h, j, i: (h * tiles + i, j)),
                   pl.BlockSpec((ms, TN),
                                lambda h, j, i: (0, jnp.where(h == halves - 1, j, n_col)))
                   ] + [slab(mw) for mw in merge_weights],
        out_shape=[jax.ShapeDtypeStruct((m, IN_WIDTH), F32),
                   jax.ShapeDtypeStruct((ms, IN_WIDTH + TN), F32)
                   ] + [jax.ShapeDtypeStruct(mw.shape, BF16) for mw in merge_weights],
        scratch_shapes=[pltpu.VMEM((rows_half, D_MODEL), BF16), pltpu.VMEM((D_MODEL, TN), BF16)],
        compiler_params=pltpu.CompilerParams(
            dimension_semantics=("arbitrary", "arbitrary", "arbitrary"),
            vmem_limit_bytes=VMEM_LIMIT),
        name="inproj",
    )(x, gain, xs, w, qg, kg, e, rope_p, rope_s, *merge_weights)


def _dup_head(x, g, lo):
    r = pltpu.roll(x, HEAD_DIM, 1)
    return jnp.where(lo, x, r) if g % 2 == 0 else jnp.where(lo, r, x)


def _prompt_attn_kernel(sinks_ref, q_ref, kvc_ref, kvp_ref, kvm_ref, vmt_ref, bias_ref, o_ref):
    n_blk = q_ref.shape[0] // BLOCK
    lo = lax.broadcasted_iota(jnp.int32, (1, LANES), 1) < HEAD_DIM
    head_lane = lax.broadcasted_iota(jnp.int32, (1, 4 * BLOCK), 1) // BLOCK

    def prev_rows(blk, cols):
        if blk == 0:
            return kvp_ref[:, cols]
        return kvc_ref[(blk - 1) * BLOCK:blk * BLOCK, cols]

    scores = []
    for blk in range(n_blk):
        rows = slice(blk * BLOCK, (blk + 1) * BLOCK)
        for g in range(N_KV_HEADS):
            ks = slice((g // 2) * LANES, (g // 2 + 1) * LANES)
            k_all = jnp.concatenate(
                [_dup_head(prev_rows(blk, ks), g, lo), _dup_head(kvc_ref[rows, ks], g, lo),
                 _dup_head(kvm_ref[:, ks], g, lo)], axis=0).astype(BF16)
            qa = q_ref[rows, (2 * g) * LANES:(2 * g + 1) * LANES]
            qb = q_ref[rows, (2 * g + 1) * LANES:(2 * g + 2) * LANES]
            q4 = jnp.concatenate([jnp.where(lo, qa, 0.0), jnp.where(lo, 0.0, qa),
                                  jnp.where(lo, qb, 0.0), jnp.where(lo, 0.0, qb)],
                                 axis=0).astype(BF16)
            scores.append(lax.dot_general(k_all, q4, NT_DIMS, preferred_element_type=F32))
    for blk in range(n_blk):
        rows = slice(blk * BLOCK, (blk + 1) * BLOCK)
        bias = bias_ref[jnp.minimum(pl.program_id(1), 1)] if blk == 0 else bias_ref[1]
        vs = slice(KV_WIDTH, 2 * KV_WIDTH)
        v_t = jnp.concatenate([prev_rows(blk, vs).T, kvc_ref[rows, vs].T], axis=1)
        for g in range(N_KV_HEADS):
            s = scores[blk * N_KV_HEADS + g]
            s_w = s[:2 * BLOCK] + bias
            s_m = s[2 * BLOCK:]
            sink = LOG2E * jnp.where(head_lane == 0, sinks_ref[4 * g],
                                     jnp.where(head_lane == 1, sinks_ref[4 * g + 1],
                                               jnp.where(head_lane == 2, sinks_ref[4 * g + 2],
                                                         sinks_ref[4 * g + 3])))
            m = jnp.maximum(jnp.maximum(jnp.max(s_w, axis=0, keepdims=True),
                                        jnp.max(s_m, axis=0, keepdims=True)), sink)
            p_w = jnp.exp2(s_w - m)
            p_m = jnp.exp2(s_m - m)
            den = (jnp.sum(p_w, axis=0, keepdims=True) + jnp.sum(p_m, axis=0, keepdims=True)
                   + jnp.exp2(sink - m))
            p = jnp.concatenate([p_w, p_m], axis=0).astype(BF16)
            hs = slice(g * HEAD_DIM, (g + 1) * HEAD_DIM)
            vt_g = jnp.concatenate([v_t[hs], vmt_ref[hs, :]], axis=1).astype(BF16)
            o_t = jnp.dot(vt_g, p, preferred_element_type=F32) * (1.0 / den)
            for pair in range(2):
                two = jnp.concatenate(
                    [o_t[:, (2 * pair) * BLOCK:(2 * pair + 1) * BLOCK],
                     o_t[:, (2 * pair + 1) * BLOCK:(2 * pair + 2) * BLOCK]], axis=0)
                o_ref[rows, (2 * g + pair) * LANES:(2 * g + pair + 1) * LANES] = two.T


def _prompt_attention(sinks, h_p, h_s, vm_t, batch, seq):
    nb = seq // BLOCK
    kvb = COL_KV // (2 * KV_WIDTH)
    kj = np.arange(2 * BLOCK)[:, None]
    qi = np.arange(4 * BLOCK)[None, :] % BLOCK
    cur = (kj >= BLOCK) & (kj - BLOCK <= qi)
    prev = (kj < BLOCK) & (kj > qi)
    bias = jnp.asarray(np.where(np.stack([cur, cur | prev]), 0.0, NEG), F32)
    per = ATTN_BLOCKS_PER_STEP
    ns = nb // per
    return pl.pallas_call(
        _prompt_attn_kernel,
        grid=(batch, ns),
        in_specs=[
            pl.BlockSpec(memory_space=pltpu.SMEM),
            pl.BlockSpec((per * BLOCK, ATTN_WIDTH), lambda b, i: (b * ns + i, 0)),
            pl.BlockSpec((per * BLOCK, 2 * KV_WIDTH), lambda b, i: (b * ns + i, kvb)),
            pl.BlockSpec((BLOCK, 2 * KV_WIDTH),
                         lambda b, i: (b * nb + jnp.maximum(i * per - 1, 0), kvb)),
            pl.BlockSpec((N_META, 2 * KV_WIDTH), lambda b, i: (h_s.shape[0] // N_META - 1, kvb)),
            pl.BlockSpec((KV_WIDTH, N_META), lambda b, i: (0, 0)),
            pl.BlockSpec((2, 2 * BLOCK, 4 * BLOCK), lambda b, i: (0, 0, 0)),
        ],
        out_specs=pl.BlockSpec((per * BLOCK, ATTN_WIDTH), lambda b, i: (b * ns + i, 0)),
        out_shape=jax.ShapeDtypeStruct((batch * seq, ATTN_WIDTH), F32),
        compiler_params=pltpu.CompilerParams(
            dimension_semantics=("arbitrary", "arbitrary"), vmem_limit_bytes=VMEM_LIMIT),
        name="prompt_attn",
    )(sinks, h_p, h_p, h_p, h_s, vm_t, bias)


def _sample_attn_kernel(qp_ref, kn_ref, vn_ref, knr_ref, vnr_ref, ck_ref, cv_ref, mk_ref, mv_ref,
                        sink_ref, o_ref, ok_ref, ov_ref):
    bt = ck_ref.shape[0]
    slot = lax.broadcasted_iota(jnp.int32, (1, 1, WINDOW), 2)
    round_bf16 = lambda a: a.astype(BF16).astype(F32)
    qp = qp_ref[...].astype(BF16)
    ck = ck_ref[...]
    cv = cv_ref[...]
    k_m = mk_ref[...].astype(BF16)
    v_m = mv_ref[...].astype(BF16)
    s_w = jnp.einsum("bhc,bcw->bhw", qp, ck.astype(BF16), preferred_element_type=F32)
    s_w = jnp.where(slot == 0, NEG, s_w)
    s_m = jnp.einsum("bhc,bkc->bhk", qp, k_m, preferred_element_type=F32)
    knr, vnr = round_bf16(knr_ref[...]), round_bf16(vnr_ref[...])
    qf = qp.astype(F32)
    s_n = jnp.stack([jnp.sum(qf[b] * knr[b:b + 1, :], axis=-1, keepdims=True)
                     for b in range(bt)])
    sink = LOG2E * sink_ref[...][None]
    m = jnp.maximum(jnp.maximum(jnp.max(s_w, axis=-1, keepdims=True),
                                jnp.max(s_m, axis=-1, keepdims=True)),
                    jnp.maximum(s_n, sink))
    p_w = jnp.exp2(s_w - m)
    p_m = jnp.exp2(s_m - m)
    p_n = jnp.exp2(s_n - m)
    den = (jnp.sum(p_w, axis=-1, keepdims=True) + jnp.sum(p_m, axis=-1, keepdims=True)
           + p_n + jnp.exp2(sink - m))
    inv = 1.0 / den
    o = (jnp.einsum("bhw,bcw->bhc", (p_w * inv).astype(BF16), cv.astype(BF16),
                    preferred_element_type=F32)
         + jnp.einsum("bhk,bkc->bhc", (p_m * inv).astype(BF16), v_m,
                      preferred_element_type=F32)
         + jnp.stack([round_bf16(p_n[b] * inv[b]) * vnr[b:b + 1, :] for b in range(bt)]))

    last = lax.broadcasted_iota(jnp.int32, (1, WINDOW), 1) == WINDOW - 1
    kn = kn_ref[0]
    vn = vn_ref[0]
    ck = pltpu.roll(ck, WINDOW - 1, 2)
    cv = pltpu.roll(cv, WINDOW - 1, 2)
    ok_ref[...] = jnp.stack([jnp.where(last, kn[:, b:b + 1], ck[b]) for b in range(bt)])
    ov_ref[...] = jnp.stack([jnp.where(last, vn[:, b:b + 1], cv[b]) for b in range(bt)])
    hh = lax.broadcasted_iota(jnp.int32, (1, N_Q_HEADS, KV_WIDTH), 1) // (N_Q_HEADS // N_KV_HEADS)
    cc = lax.broadcasted_iota(jnp.int32, (1, N_Q_HEADS, KV_WIDTH), 2) // HEAD_DIM
    o = jnp.where(hh == cc, o, 0.0)
    o = o[:, :, :LANES] + o[:, :, LANES:]
    o_ref[...] = o + pltpu.roll(o, HEAD_DIM, 2)


def _sample_attention(qp, kn_t, vn_t, h_s, cache_k, cache_v, meta_k, meta_v, sink_col, bt):
    n = cache_k.shape[0]
    kvb = COL_KV // KV_WIDTH
    win_spec = pl.BlockSpec((bt, KV_WIDTH, WINDOW), lambda t: (t, 0, 0))
    new_spec = pl.BlockSpec((1, KV_WIDTH, bt), lambda t: (t, 0, 0))
    meta_spec = pl.BlockSpec((bt, N_META, KV_WIDTH), lambda t: (t, 0, 0))
    return pl.pallas_call(
        _sample_attn_kernel,
        grid=(n // bt,),
        in_specs=[
            pl.BlockSpec((bt, N_Q_HEADS, KV_WIDTH), lambda t: (t, 0, 0)),
            new_spec, new_spec,
            pl.BlockSpec((bt, KV_WIDTH), lambda t: (t, kvb)),
            pl.BlockSpec((bt, KV_WIDTH), lambda t: (t, kvb + 1)),
            win_spec, win_spec, meta_spec, meta_spec,
            pl.BlockSpec((N_Q_HEADS, 1), lambda t: (0, 0)),
        ],
        out_specs=[
            pl.BlockSpec((bt, N_Q_HEADS, LANES), lambda t: (t, 0, 0)),
            win_spec, win_spec,
        ],
        out_shape=[
            jax.ShapeDtypeStruct((n, N_Q_HEADS, LANES), F32),
            jax.ShapeDtypeStruct(cache_k.shape, F32),
            jax.ShapeDtypeStruct(cache_v.shape, F32),
        ],
        compiler_params=pltpu.CompilerParams(
            dimension_semantics=("arbitrary",), vmem_limit_bytes=VMEM_LIMIT),
        name="sample_attn",
    )(qp, kn_t, vn_t, h_s, h_s, cache_k, cache_v, meta_k, meta_v, sink_col)


def _gelu_tanh(x):
    c = 0.7978845608028654
    half = 0.5 * x
    return half + half * jnp.tanh(x * (c + (c * 0.044715) * (x * x)))


def _ssm_kernel(u_ref, um_ref, us_ref, hr_ref, hi_ref, t8_ref, win_ref, wst_ref, wc_ref,
                ar_ref, ai_ref, lr_ref, li_ref, pr_ref, pi_ref, d_ref,
                y_ref, sre_ref, sim_ref, ys_ref, nr_ref, ni_ref, xr_ref, xi_ref):
    dsk = d_ref[...]
    us = us_ref[...]
    bu = jnp.dot(us.astype(BF16), win_ref[0, (CHUNK - 1) * LANES:, :],
                 preferred_element_type=F32)
    l_r, l_i = lr_ref[0], li_ref[0]
    h_r, h_i = hr_ref[...], hi_ref[...]
    n_r = l_r * h_r - l_i * h_i + bu[:, :STATE_TILE]
    n_i = l_r * h_i + l_i * h_r + bu[:, STATE_TILE:]
    nr_ref[...] = n_r
    ni_ref[...] = n_i
    hn = jnp.concatenate([n_r, n_i], axis=1).astype(BF16)
    ys_ref[...] = _gelu_tanh(
        lax.dot_general(hn, wc_ref[0], NT_DIMS, preferred_element_type=F32) + dsk * us)

    batch = sre_ref.shape[0]
    nc = u_ref.shape[0] // CHUNK // batch
    win = win_ref[0]
    dsk_row = jnp.concatenate([dsk] * CHUNK, axis=1)

    def chunks_of(b):
        r0 = b * nc * CHUNK
        return jnp.concatenate(
            [u_ref[pl.ds(r0 + j, nc, stride=CHUNK), :] for j in range(CHUNK)], axis=1)

    n_mc = um_ref.shape[0] // CHUNK
    um = jnp.concatenate([um_ref[pl.ds(j, n_mc, stride=CHUNK), :] for j in range(CHUNK)], axis=1)
    gm = jnp.dot(um.astype(BF16), win, preferred_element_type=F32)
    a_r, a_i = ar_ref[0, 0:1, :], ai_ref[0, 0:1, :]
    s0r, s0i = gm[0:1, :STATE_TILE], gm[0:1, STATE_TILE:]
    for c in range(1, n_mc):
        s0r, s0i = (a_r * s0r - a_i * s0i + gm[c:c + 1, :STATE_TILE],
                    a_r * s0i + a_i * s0r + gm[c:c + 1, STATE_TILE:])

    n_t = nc // SUBLANES
    row = lax.broadcasted_iota(jnp.int32, (nc, 1), 0)
    first = row == 0
    in_tile = row % SUBLANES
    t_row = lax.broadcasted_iota(jnp.int32, (n_t, 1), 0)
    first_t = t_row == 0
    n_local = SUBLANES.bit_length() - 1
    w_r, w_i = pr_ref[0], pi_ref[0]

    def tile_totals(scr_ref, x):
        parts = []
        for c in range(STATE_TILE // LANES):
            scr_ref[c] = x[:, c * LANES:(c + 1) * LANES]
            parts.append(scr_ref[c, pl.ds(SUBLANES - 1, n_t, stride=SUBLANES), :])
        return jnp.concatenate(parts, axis=1)

    def shifted(x, d, keep):
        if d < SUBLANES:
            return jnp.where(keep, pltpu.roll(x, d, 0), 0.0)
        return jnp.concatenate([jnp.zeros((d, STATE_TILE), F32), x[:x.shape[0] - d]], axis=0)

    ucat_all = jnp.concatenate([chunks_of(b) for b in range(batch)], axis=0)
    ub_all = ucat_all.astype(BF16)
    g_all = jnp.dot(ub_all, win, preferred_element_type=F32)
    y_intra_all = jnp.concatenate(
        [jnp.dot(ub_all[:, :(2 * m + 2) * LANES],
                 t8_ref[0, :(2 * m + 2) * LANES, 2 * m * LANES:(2 * m + 2) * LANES],
                 preferred_element_type=F32) for m in range(CHUNK // 2)],
        axis=1) + dsk_row * ucat_all
    for b in range(batch):
        g, y_intra = g_all[b * nc:(b + 1) * nc], y_intra_all[b * nc:(b + 1) * nc]
        gr, gi = g[:, :STATE_TILE], g[:, STATE_TILE:]
        for k in range(n_local):
            d = 1 << k
            k_r, k_i = ar_ref[0, k:k + 1, :], ai_ref[0, k:k + 1, :]
            sh_r, sh_i = shifted(gr, d, in_tile >= d), shifted(gi, d, in_tile >= d)
            gr, gi = gr + (k_r * sh_r - k_i * sh_i), gi + (k_r * sh_i + k_i * sh_r)
        tot_r, tot_i = tile_totals(xr_ref, gr), tile_totals(xi_ref, gi)
        k_r, k_i = ar_ref[0, n_local:n_local + 1, :], ai_ref[0, n_local:n_local + 1, :]
        tot_r = tot_r + jnp.where(first_t, k_r * s0r - k_i * s0i, 0.0)
        tot_i = tot_i + jnp.where(first_t, k_r * s0i + k_i * s0r, 0.0)
        for k in range(SCAN_LEVELS - n_local):
            d = 1 << k
            if d >= n_t:
                break
            k_r = ar_ref[0, n_local + k:n_local + k + 1, :]
            k_i = ai_ref[0, n_local + k:n_local + k + 1, :]
            sh_r, sh_i = shifted(tot_r, d, t_row >= d), shifted(tot_i, d, t_row >= d)
            tot_r, tot_i = tot_r + (k_r * sh_r - k_i * sh_i), tot_i + (k_r * sh_i + k_i * sh_r)
        sre_ref[b, 0] = tot_r[n_t - 1:n_t, :]
        sim_ref[b, 0] = tot_i[n_t - 1:n_t, :]
        c_r = jnp.where(first_t, s0r, pltpu.roll(tot_r, 1, 0))
        c_i = jnp.where(first_t, s0i, pltpu.roll(tot_i, 1, 0))
        full_r, full_i = [], []
        for t in range(n_t):
            rs = slice(t * SUBLANES, (t + 1) * SUBLANES)
            e_r, e_i = c_r[t:t + 1, :], c_i[t:t + 1, :]
            full_r.append(gr[rs] + (w_r * e_r - w_i * e_i))
            full_i.append(gi[rs] + (w_r * e_i + w_i * e_r))
        gr, gi = jnp.concatenate(full_r, axis=0), jnp.concatenate(full_i, axis=0)
        sp = jnp.concatenate([jnp.where(first, s0r, pltpu.roll(gr, 1, 0)),
                              jnp.where(first, s0i, pltpu.roll(gi, 1, 0))],
                             axis=1).astype(BF16)
        y2 = y_intra + lax.dot_general(sp, wst_ref[0], NT_DIMS, preferred_element_type=F32)
        for j in range(CHUNK):
            sl = slice(j * LANES, (j + 1) * LANES)
            y_ref[pl.ds(b * nc * CHUNK + j, nc, stride=CHUNK), :] = _gelu_tanh(y2[:, sl])


def _ssm(h_p, h_s, h0r, h0i, t8, win, wst, wc, a_r, a_i, l_r, l_i, p_r, p_i, d_skip, batch, n_s):
    ub = COL_U // LANES
    rows = h_p.shape[0]
    wspec = pl.BlockSpec((1, CHUNK_WIDTH, CHUNK_WIDTH), lambda gt: (gt, 0, 0))
    aspec = pl.BlockSpec((1, SCAN_LEVELS, STATE_TILE), lambda gt: (gt, 0, 0))
    sspec = pl.BlockSpec((batch, 1, 1, STATE_TILE), lambda gt: (0, gt, 0, 0))
    st = pl.BlockSpec((n_s, STATE_TILE), lambda gt: (0, gt))
    lam = pl.BlockSpec((1, 1, STATE_TILE), lambda gt: (gt, 0, 0))
    return pl.pallas_call(
        _ssm_kernel,
        grid=(N_GROUP_TILES,),
        in_specs=[
            pl.BlockSpec((rows, LANES), lambda gt: (0, ub + gt)),
            pl.BlockSpec((N_META, LANES), lambda gt: (h_s.shape[0] // N_META - 1, ub + gt)),
            pl.BlockSpec((n_s, LANES), lambda gt: (0, ub + gt)),
            st, st,
            wspec, wspec, wspec,
            pl.BlockSpec((1, LANES, 2 * STATE_TILE), lambda gt: (gt, 0, 0)),
            aspec, aspec, lam, lam,
            pl.BlockSpec((1, SUBLANES, STATE_TILE), lambda gt: (gt, 0, 0)),
            pl.BlockSpec((1, SUBLANES, STATE_TILE), lambda gt: (gt, 0, 0)),
            pl.BlockSpec((1, LANES), lambda gt: (0, gt)),
        ],
        out_specs=[pl.BlockSpec((rows, LANES), lambda gt: (0, gt)), sspec, sspec,
                   pl.BlockSpec((n_s, LANES), lambda gt: (0, gt)), st, st],
        out_shape=[
            jax.ShapeDtypeStruct((rows, SSM_WIDTH), F32),
            jax.ShapeDtypeStruct((batch, N_GROUP_TILES, 1, STATE_TILE), F32),
            jax.ShapeDtypeStruct((batch, N_GROUP_TILES, 1, STATE_TILE), F32),
            jax.ShapeDtypeStruct((n_s, SSM_WIDTH), F32),
            jax.ShapeDtypeStruct((n_s, N_SSM_GROUPS * SSM_STATE), F32),
            jax.ShapeDtypeStruct((n_s, N_SSM_GROUPS * SSM_STATE), F32),
        ],
        scratch_shapes=[pltpu.VMEM((STATE_TILE // LANES, rows // CHUNK // batch, LANES), F32)] * 2,
        compiler_params=pltpu.CompilerParams(
            dimension_semantics=("arbitrary",), vmem_limit_bytes=VMEM_LIMIT),
        name="ssm",
    )(h_p, h_s, h_s, h0r, h0i, t8, win, wst, wc, a_r, a_i, l_r, l_i, p_r, p_i, d_skip)


def _merge_kernel(o_ref, za_ref, zs_ref, ga_ref, gs_ref, y_ref, x_ref,
                  o2_ref, za2_ref, zs2_ref, ga2_ref, gs2_ref, y2_ref, x2_ref,
                  bglu_ref, wglu_ref, wa_ref, ws_ref, wo_ref, out_ref, out2_ref):
    def merge(o_r, za_r, zs_r, ga_r, gs_r, y_r, x_r, out_r):
        za = za_r[...]
        a_in = (o_r[...] * (za * jax.nn.sigmoid(za))).astype(BF16)
        gated_a = jax.nn.sigmoid(ga_r[...]) * jnp.dot(a_in, wa_ref[...],
                                                      preferred_element_type=F32)
        y = y_r[...]
        t = jnp.dot(y.astype(BF16), wglu_ref[...], preferred_element_type=F32) + bglu_ref[...]
        y = y * jax.nn.sigmoid(t)
        zs = zs_r[...]
        s_in = (y * (zs * jax.nn.sigmoid(zs))).astype(BF16)
        br_s = jnp.dot(s_in, ws_ref[...], preferred_element_type=F32)
        mix = (gated_a + jax.nn.sigmoid(gs_r[...]) * br_s).astype(BF16)
        out_r[...] = x_r[...] + jnp.dot(mix, wo_ref[...], preferred_element_type=F32)

    merge(o_ref, za_ref, zs_ref, ga_ref, gs_ref, y_ref, x_ref, out_ref)

    @pl.when(pl.program_id(0) == pl.num_programs(0) - 1)
    def _():
        merge(o2_ref, za2_ref, zs2_ref, ga2_ref, gs2_ref, y2_ref, x2_ref, out2_ref)


def _merge(o, h, y, x, o2, h2, y2, x2, b_glu, w_glu, w_a, w_s, w_o, tm):
    m, m2 = x.shape[0], x2.shape[0]
    const = lambda shape: pl.BlockSpec(shape, lambda i: (0, 0), pipeline_mode=pl.Buffered(1))
    rows = lambda width, col: pl.BlockSpec((tm, width), lambda i: (i, col))
    rows2 = lambda width, col: pl.BlockSpec((m2, width), lambda i: (0, col),
                                            pipeline_mode=pl.Buffered(1))
    operands = lambda spec: [
        spec(ATTN_WIDTH, 0), spec(ATTN_WIDTH, COL_ZA // ATTN_WIDTH),
        spec(SSM_WIDTH, COL_ZS // SSM_WIDTH), spec(D_MODEL, COL_GA // D_MODEL),
        spec(D_MODEL, COL_GS // D_MODEL), spec(SSM_WIDTH, 0), spec(D_MODEL, 0)]
    return pl.pallas_call(
        _merge_kernel,
        grid=(m // tm,),
        in_specs=operands(rows) + operands(rows2) + [
            const((1, SSM_WIDTH)),
            const((SSM_WIDTH, SSM_WIDTH)),
            const((ATTN_WIDTH, D_MODEL)),
            const((SSM_WIDTH, D_MODEL)),
            const((D_MODEL, D_MODEL)),
        ],
        out_specs=[pl.BlockSpec((tm, D_MODEL), lambda i: (i, 0)),
                   pl.BlockSpec((m2, D_MODEL), lambda i: (0, 0))],
        out_shape=[jax.ShapeDtypeStruct((m, D_MODEL), F32),
                   jax.ShapeDtypeStruct((m2, D_MODEL), F32)],
        compiler_params=pltpu.CompilerParams(
            dimension_semantics=("arbitrary",), vmem_limit_bytes=VMEM_LIMIT),
        name="merge",
    )(o, h, h, h, h, y, x, o2, h2, h2, h2, h2, y2, x2, b_glu, w_glu, w_a, w_s, w_o)


def _rope_tables(pos):
    half = HEAD_DIM // 2
    inv_freq = (ROPE_THETA ** (-np.arange(half, dtype=np.float32) / half)).astype(np.float32)
    ang = np.asarray(pos, np.float32)[:, None] * inv_freq[None, :]
    cos, sin = np.cos(ang).astype(np.float32), np.sin(ang).astype(np.float32)
    zero = np.zeros_like(sin)
    reps = LANES // HEAD_DIM
    cos_t = np.tile(np.concatenate([cos, cos], axis=1), (1, reps))
    sin_a = np.tile(np.concatenate([-sin, zero], axis=1), (1, reps))
    sin_b = np.tile(np.concatenate([zero, sin], axis=1), (1, reps))
    return jnp.asarray(np.concatenate([cos_t, sin_a, sin_b], axis=1))


def _split_bf16(x):
    hi = x.astype(BF16)
    return hi, (x - hi.astype(F32)).astype(BF16)


def _dot_nt_f32(a, b):
    ah, al = _split_bf16(a)
    bh, bl = _split_bf16(b)
    dot = lambda x, y: lax.dot_general(x, y, NT_DIMS, preferred_element_type=F32)
    return dot(ah, bh) + dot(ah, bl) + dot(al, bh)


def _ssm_tables_kernel(are_ref, aim_ref, ldt_ref, bre_ref, bim_ref, cre_ref, cim_ref,
                       win_ref, wst_ref, t8_ref, wc_ref, ar_ref, ai_ref, lr_ref, li_ref,
                       pr_ref, pi_ref):
    a_r, a_i = are_ref[0], aim_ref[0]
    dt = jnp.exp(ldt_ref[0])
    mag = jnp.exp(a_r * dt)
    l_r, l_i = mag * jnp.cos(a_i * dt), mag * jnp.sin(a_i * dt)
    den = a_r * a_r + a_i * a_i
    f_r = ((l_r - 1.0) * a_r + l_i * a_i) / den
    f_i = (l_i * a_r - (l_r - 1.0) * a_i) / den
    shape = (LANES, STATE_TILE)
    own = (lax.broadcasted_iota(jnp.int32, shape, 0) // SSM_GROUP
           == lax.broadcasted_iota(jnp.int32, shape, 1) // SSM_STATE)
    sq = (LANES, LANES)
    same_group = (lax.broadcasted_iota(jnp.int32, sq, 0) // SSM_GROUP
                  == lax.broadcasted_iota(jnp.int32, sq, 1) // SSM_GROUP)
    lo = lax.broadcasted_iota(jnp.int32, (1, LANES), 1) < SSM_STATE
    n_rep = STATE_TILE // LANES

    def expand(x):
        return jnp.where(own, jnp.concatenate([x] * n_rep, axis=1), 0.0)

    def compact(v):
        e = jnp.where(own, jnp.broadcast_to(v, shape), 0.0)
        s = e[:, :LANES]
        for k in range(1, n_rep):
            s = s + e[:, k * LANES:(k + 1) * LANES]
        return s + pltpu.roll(s, SSM_STATE, 1)

    lc_r, lc_i = compact(l_r), compact(l_i)
    b_r, b_i = bre_ref[0], bim_ref[0]
    c_r, c_i = cre_ref[0], cim_ref[0]
    bb_r, bb_i = _cmul(compact(f_r), compact(f_i), b_r, b_i)
    wc_ref[0] = jnp.concatenate([expand(c_r), expand(-c_i)], axis=1).astype(BF16)
    c_cat = jnp.where(lo, c_r, -c_i)
    q_r, q_i = jnp.ones(sq, F32), jnp.zeros(sq, F32)
    for n in range(CHUNK):
        x_r, x_i = _cmul(q_r, q_i, bb_r, bb_i)
        win_ref[0, (CHUNK - 1 - n) * LANES:(CHUNK - n) * LANES, :] = jnp.concatenate(
            [expand(x_r), expand(x_i)], axis=1).astype(BF16)
        k_n = _dot_nt_f32(jnp.where(lo, x_r, x_i), c_cat)
        k_n = jnp.where(same_group, k_n, 0.0).astype(BF16)
        for j in range(CHUNK - n):
            t8_ref[0, j * LANES:(j + 1) * LANES, (j + n) * LANES:(j + n + 1) * LANES] = k_n
            if n > 0:
                t8_ref[0, (j + n) * LANES:(j + n + 1) * LANES, j * LANES:(j + 1) * LANES] = (
                    jnp.zeros(sq, BF16))
        q_r, q_i = _cmul(q_r, q_i, lc_r, lc_i)
        s_r, s_i = _cmul(c_r, c_i, q_r, q_i)
        wst_ref[0, n * LANES:(n + 1) * LANES, :] = jnp.concatenate(
            [expand(s_r), expand(-s_i)], axis=1).astype(BF16)
    p_r, p_i = l_r, l_i
    for _ in range(CHUNK.bit_length() - 1):
        p_r, p_i = _cmul(p_r, p_i, p_r, p_i)
    t_r, t_i = p_r, p_i
    for i in range(SUBLANES):
        pr_ref[0, i:i + 1, :] = t_r
        pi_ref[0, i:i + 1, :] = t_i
        t_r, t_i = _cmul(t_r, t_i, p_r, p_i)
    for k in range(SCAN_LEVELS):
        ar_ref[0, k:k + 1, :] = p_r
        ai_ref[0, k:k + 1, :] = p_i
        p_r, p_i = _cmul(p_r, p_i, p_r, p_i)
    lr_ref[0] = l_r
    li_ref[0] = l_i


def _ssm_tables(a_re, a_im, log_dt, b_re, b_im, c_re, c_im):
    gt = N_GROUP_TILES
    lane_vec = lambda v: v.reshape(gt, 1, STATE_TILE)
    dup = lambda v: jnp.concatenate([v, v], axis=-1)
    rows = GROUPS_PER_TILE * SSM_GROUP
    b_t = lambda v: dup(v.transpose(0, 2, 1).reshape(gt, rows, SSM_STATE))
    c_t = lambda v: dup(v.reshape(gt, rows, SSM_STATE))
    vec = pl.BlockSpec((1, 1, STATE_TILE), lambda g: (g, 0, 0))
    par = pl.BlockSpec((1, rows, LANES), lambda g: (g, 0, 0))
    big = pl.BlockSpec((1, CHUNK_WIDTH, CHUNK_WIDTH), lambda g: (g, 0, 0))
    wcs = pl.BlockSpec((1, LANES, 2 * STATE_TILE), lambda g: (g, 0, 0))
    scan = pl.BlockSpec((1, SCAN_LEVELS, STATE_TILE), lambda g: (g, 0, 0))
    tile = pl.BlockSpec((1, SUBLANES, STATE_TILE), lambda g: (g, 0, 0))
    return pl.pallas_call(
        _ssm_tables_kernel,
        grid=(gt,),
        in_specs=[vec, vec, vec, par, par, par, par],
        out_specs=[big, big, big, wcs, scan, scan, vec, vec, tile, tile],
        out_shape=[
            jax.ShapeDtypeStruct((gt, CHUNK_WIDTH, 2 * STATE_TILE), BF16),
            jax.ShapeDtypeStruct((gt, CHUNK_WIDTH, 2 * STATE_TILE), BF16),
            jax.ShapeDtypeStruct((gt, CHUNK_WIDTH, CHUNK_WIDTH), BF16),
            jax.ShapeDtypeStruct((gt, LANES, 2 * STATE_TILE), BF16),
            jax.ShapeDtypeStruct((gt, SCAN_LEVELS, STATE_TILE), F32),
            jax.ShapeDtypeStruct((gt, SCAN_LEVELS, STATE_TILE), F32),
            jax.ShapeDtypeStruct((gt, 1, STATE_TILE), F32),
            jax.ShapeDtypeStruct((gt, 1, STATE_TILE), F32),
            jax.ShapeDtypeStruct((gt, SUBLANES, STATE_TILE), F32),
            jax.ShapeDtypeStruct((gt, SUBLANES, STATE_TILE), F32),
        ],
        compiler_params=pltpu.CompilerParams(
            dimension_semantics=("arbitrary",), vmem_limit_bytes=VMEM_LIMIT),
        name="ssm_tables",
    )(lane_vec(a_re), lane_vec(a_im), lane_vec(jnp.repeat(log_dt, SSM_STATE)),
      b_t(b_re), b_t(b_im), c_t(c_re), c_t(c_im))


def kernel(x_prompt, x_sample, cache_win_k, cache_win_v, cache_meta_k, cache_meta_v,
           state_ssm_re, state_ssm_im, meta_tokens, norm_gain, w_in, q_norm_gain, k_norm_gain,
           sinks, a_re, a_im, log_dt, b_re, b_im, c_re, c_im, d_skip, w_glu, b_glu,
           w_attn_out, w_ssm_out, w_out):
    depth = w_in.shape[0]
    assert depth == 1, "single-layer trunk"
    batch, seq = x_prompt.shape[:2]
    n_s = x_sample.shape[0]
    assert x_sample.shape[1] == 1 and seq % (CHUNK * BLOCK) == 0
    l = 0

    w = w_in[l]
    qg = jnp.tile(q_norm_gain[l], TN // HEAD_DIM)[None]
    kg = jnp.tile(k_norm_gain[l], KV_WIDTH // HEAD_DIM)[None]
    lane = np.arange(LANES)
    e128 = np.where((lane[:, None] // HEAD_DIM) == (lane[None, :] // HEAD_DIM),
                    1.0 / HEAD_DIM, 0.0).astype(np.float32)
    e = jnp.asarray(np.concatenate([e128, e128], axis=0), BF16)
    gain = norm_gain[l][None]

    pos_p = N_META + np.arange(seq)
    pos_small = np.concatenate([np.full((n_s,), PAST_LEN), np.arange(N_META)])
    x_small = jnp.concatenate([x_sample[:, 0, :], meta_tokens.astype(x_prompt.dtype)], axis=0)
    xp = x_prompt.reshape(batch * seq, D_MODEL)
    xn_s = _rmsnorm(x_small, gain, tm=n_s + N_META)
    h_p, h_s, w_glu_bf, w_a_bf, w_s_bf, w_o_bf = _inproj(
        xp, gain, xn_s, w, qg, kg, e, _rope_tables(pos_p), _rope_tables(pos_small),
        (w_glu[l], w_attn_out[l], w_ssm_out[l], w_out[l]), tm=INPROJ_ROWS, halves=INPROJ_HALVES)

    win, wst, t8, wc, a_r, a_i, l_r, l_i, p_r, p_i = _ssm_tables(
        a_re[l], a_im[l], log_dt[l], b_re[l], b_im[l], c_re[l], c_im[l])
    dsk = d_skip[l][None]

    vm_t = h_s[n_s:, COL_KV + KV_WIDTH:IN_WIDTH].T
    o_p = _prompt_attention(sinks[l], h_p, h_s, vm_t, batch, seq)
    q_s = h_s[:n_s, COL_Q:COL_Q + ATTN_WIDTH].reshape(n_s, N_Q_HEADS, 1, HEAD_DIM)
    sel = (np.arange(N_Q_HEADS)[:, None] // (N_Q_HEADS // N_KV_HEADS)
           == np.arange(N_KV_HEADS)[None, :]).astype(np.float32)
    qp = (q_s * sel[None, :, :, None]).reshape(n_s, N_Q_HEADS, KV_WIDTH)
    bt = SAMPLE_ATTN_BATCH
    to_t = lambda c: c.transpose(0, 2, 3, 1).reshape(n_s, KV_WIDTH, WINDOW)
    from_t = lambda c: c.reshape(n_s, N_KV_HEADS, HEAD_DIM, WINDOW).transpose(0, 3, 1, 2)[None]
    new_t = lambda c0: h_s[:n_s, c0:c0 + KV_WIDTH].reshape(n_s // bt, bt, KV_WIDTH).transpose(0, 2, 1)
    o_s, s_win_k, s_win_v = _sample_attention(
        qp, new_t(COL_KV), new_t(COL_KV + KV_WIDTH), h_s, to_t(cache_win_k[l]), to_t(cache_win_v[l]),
        cache_meta_k[l].reshape(n_s, N_META, KV_WIDTH), cache_meta_v[l].reshape(n_s, N_META, KV_WIDTH),
        sinks[l][:, None], bt=bt)
    o_s = o_s[:, :, :HEAD_DIM].reshape(n_s, ATTN_WIDTH)

    y_p, p_re, p_im, y_s, s_re, s_im = _ssm(
        h_p, h_s, state_ssm_re[l].reshape(n_s, -1), state_ssm_im[l].reshape(n_s, -1),
        t8, win, wst, wc, a_r, a_i, l_r, l_i, p_r, p_i, dsk, batch, n_s)
    y_prompt, y_sample = _merge(
        o_p, h_p, y_p, xp, o_s, h_s, y_s, x_sample[:, 0, :], b_glu[l][None],
        w_glu_bf, w_a_bf, w_s_bf, w_o_bf, tm=MERGE_ROWS)
    y_prompt = y_prompt.reshape(batch, seq, D_MODEL)
    y_sample = y_sample.reshape(n_s, 1, D_MODEL)

    kv_p = h_p.reshape(batch, seq, IN_WIDTH)[:, seq - WINDOW:, COL_KV:]
    kv_p = kv_p.reshape(batch, WINDOW, 2, N_KV_HEADS, HEAD_DIM)
    p_win_k = kv_p[:, :, 0][None]
    p_win_v = kv_p[:, :, 1][None]
    kv_m = h_s[n_s:, COL_KV:IN_WIDTH].reshape(N_META, 2, N_KV_HEADS, HEAD_DIM)
    p_meta_k = jnp.broadcast_to(kv_m[None, :, 0], (batch, N_META, N_KV_HEADS, HEAD_DIM))[None]
    p_meta_v = jnp.broadcast_to(kv_m[None, :, 1], (batch, N_META, N_KV_HEADS, HEAD_DIM))[None]
    p_ssm_re = p_re.reshape(batch, N_SSM_GROUPS, SSM_STATE)[None]
    p_ssm_im = p_im.reshape(batch, N_SSM_GROUPS, SSM_STATE)[None]
    st_shape = (1, n_s, N_SSM_GROUPS, SSM_STATE)
    return (y_prompt, y_sample, p_win_k, p_win_v, p_meta_k, p_meta_v, p_ssm_re, p_ssm_im,
            from_t(s_win_k), from_t(s_win_v), s_re.reshape(st_shape), s_im.reshape(st_shape))
```

```python
import functools

import jax
import jax.numpy as jnp
import numpy as np
from jax import lax
from jax.experimental import pallas as pl
from jax.experimental.pallas import tpu as pltpu

F32 = jnp.float32
BF16 = jnp.bfloat16

D_MODEL = 2048
N_META = 16
HEAD_DIM = 64
N_Q_HEADS = 16
N_KV_HEADS = 4
ATTN_WIDTH = N_Q_HEADS * HEAD_DIM
KV_WIDTH = N_KV_HEADS * HEAD_DIM
WINDOW = 128
BLOCK = 128
ROPE_THETA = 10000.0
SSM_WIDTH = D_MODEL // 2
SSM_GROUP = 16
N_SSM_GROUPS = SSM_WIDTH // SSM_GROUP
SSM_STATE = 64
EPS = 1e-6
PAST_LEN = 8192

LANES = 128
SUBLANES = 8
CHUNK = 8
GROUPS_PER_TILE = LANES // SSM_GROUP
N_GROUP_TILES = N_SSM_GROUPS // GROUPS_PER_TILE
CHUNK_WIDTH = CHUNK * LANES
STATE_TILE = GROUPS_PER_TILE * SSM_STATE
SCAN_LEVELS = 8

COL_Q = 0
COL_ZA = 1024
COL_U = 2048
COL_ZS = 3072
COL_GA = 4096
COL_GS = 6144
COL_KV = 8192
IN_WIDTH = 8704
TN = 512
CAST_ROWS = 16
INPROJ_ROWS = 1024
INPROJ_HALVES = 2
MERGE_ROWS = 256
SAMPLE_ATTN_BATCH = 32
ATTN_BLOCKS_PER_STEP = 8
KV_TILE = COL_KV // TN
VMEM_LIMIT = 56 * 1024 * 1024
NEG = -1e30
LOG2E = 1.4426950408889634
Q_SCALE = HEAD_DIM ** -0.5 * LOG2E
NT_DIMS = (((1,), (1,)), ((), ()))


def _cmul(ar, ai, br, bi):
    return ar * br - ai * bi, ar * bi + ai * br


def _rmsnorm_kernel(x_ref, gain_ref, o_ref):
    x = x_ref[...]
    r = lax.rsqrt(jnp.mean(x * x, axis=-1, keepdims=True) + EPS)
    o_ref[...] = (x * r * gain_ref[...]).astype(BF16)


def _rmsnorm(x, gain, tm):
    m = x.shape[0]
    return pl.pallas_call(
        _rmsnorm_kernel,
        grid=(m // tm,),
        in_specs=[pl.BlockSpec((tm, D_MODEL), lambda i: (i, 0)),
                  pl.BlockSpec((1, D_MODEL), lambda i: (0, 0))],
        out_specs=pl.BlockSpec((tm, D_MODEL), lambda i: (i, 0)),
        out_shape=jax.ShapeDtypeStruct((m, D_MODEL), BF16),
        compiler_params=pltpu.CompilerParams(
            dimension_semantics=("arbitrary",), vmem_limit_bytes=VMEM_LIMIT),
        name="rmsnorm",
    )(x, gain)


def _inproj_epilogue(j, o_ref, qg_ref, kg_ref, e_ref, rope_ref):
    def head_norm_rope(n_chunks, gain_ref, scale):
        cos, sa, sb = (rope_ref[:, t * LANES:(t + 1) * LANES] for t in range(3))
        for c in range(n_chunks):
            cs = slice(c * LANES, (c + 1) * LANES)
            ac = o_ref[:, cs]
            sq = ac * ac
            hi = sq.astype(BF16)
            lo = (sq - hi.astype(F32)).astype(BF16)
            ms = jnp.dot(jnp.concatenate([hi, lo], axis=1), e_ref[...],
                         preferred_element_type=F32)
            xc = ac * lax.rsqrt(ms + EPS) * gain_ref[:, cs]
            o_ref[:, cs] = (xc * cos + pltpu.roll(xc, LANES - HEAD_DIM // 2, 1) * sa
                            + pltpu.roll(xc, HEAD_DIM // 2, 1) * sb) * scale

    @pl.when(j < COL_ZA // TN)
    def _():
        head_norm_rope(TN // LANES, qg_ref, Q_SCALE)

    @pl.when(j == KV_TILE)
    def _():
        head_norm_rope(KV_WIDTH // LANES, kg_ref, 1.0)


def _inproj_kernel(x_ref, gain_ref, xs_ref, w_ref, qg_ref, kg_ref, e_ref, rope_ref, rope_s_ref,
                   *rest, mw_slabs):
    n_mw = (len(rest) - 4) // 2
    mw_refs, (o_ref, os_ref) = rest[:n_mw], rest[n_mw:n_mw + 2]
    mw_out_refs, (xn_ref, wbf_ref) = rest[n_mw + 2:2 * n_mw + 2], rest[2 * n_mw + 2:]
    h, j, i = pl.program_id(0), pl.program_id(1), pl.program_id(2)
    tm = o_ref.shape[0]
    rows = pl.ds(pl.multiple_of(i * tm, tm), tm)

    @pl.when(j == 0)
    def _():
        x = x_ref[...]
        r = lax.rsqrt(jnp.mean(x * x, axis=-1, keepdims=True) + EPS)
        xn_ref[rows, :] = (x * r * gain_ref[...]).astype(BF16)

    @pl.when(i == 0)
    def _():
        wbf_ref[...] = w_ref[...].astype(BF16)

    o_ref[...] = jnp.dot(xn_ref[rows, :], wbf_ref[...], preferred_element_type=F32)
    _inproj_epilogue(j, o_ref, qg_ref, kg_ref, e_ref, rope_ref)

    step = (h * pl.num_programs(1) + j) * pl.num_programs(2) + i
    for src, dst, n_slabs in zip(mw_refs, mw_out_refs, mw_slabs):
        @pl.when(step < n_slabs)
        def _():
            dst[...] = src[...].astype(BF16)

    @pl.when(jnp.logical_and(h == 0, jnp.logical_and(j == 0, i == 0)))
    def _():
        os_ref[...] = jnp.zeros(os_ref.shape, F32)

    @pl.when(jnp.logical_and(h == pl.num_programs(0) - 1, i == pl.num_programs(2) - 1))
    def _():
        os_ref[...] = jnp.dot(xs_ref[...], wbf_ref[...], preferred_element_type=F32)
        _inproj_epilogue(j, os_ref, qg_ref, kg_ref, e_ref, rope_s_ref)


def _w_tile(j):
    n_q = COL_ZA // TN
    return jnp.where(j < n_q, j, jnp.where(j == KV_TILE, n_q, j + 1))


def _inproj(x, gain, xs, w, qg, kg, e, rope_p, rope_s, merge_weights, tm, halves):
    m, ms = x.shape[0], xs.shape[0]
    rows_half = m // halves
    tiles = rows_half // tm
    n_tab = rope_p.shape[0] // tm
    n_col = IN_WIDTH // TN
    const = lambda shape: pl.BlockSpec(shape, lambda h, j, i: (0, 0))
    n_steps = halves * n_col * tiles
    for mw in merge_weights:
        assert mw.shape[0] % CAST_ROWS == 0 and mw.shape[0] // CAST_ROWS <= n_steps
    slab = lambda mw: pl.BlockSpec(
        (CAST_ROWS, mw.shape[1]),
        lambda h, j, i: (jnp.minimum((h * n_col + j) * tiles + i, mw.shape[0] // CAST_ROWS - 1), 0))
    return pl.pallas_call(
        functools.partial(_inproj_kernel,
                          mw_slabs=tuple(mw.shape[0] // CAST_ROWS for mw in merge_weights)),
        grid=(halves, IN_WIDTH // TN, tiles),
        in_specs=[
            pl.BlockSpec((tm, D_MODEL),
                         lambda h, j, i: (h * tiles + jnp.where(j == 0, i, tiles - 1), 0)),
            const((1, D_MODEL)),
            const((ms, D_MODEL)),
            pl.BlockSpec((D_MODEL, TN), lambda h, j, i: (0, _w_tile(j))),
            const((1, TN)), const((1, KV_WIDTH)), const((2 * LANES, LANES)),
            pl.BlockSpec((tm, 3 * LANES), lambda h, j, i: ((h * tiles + i) % n_tab, 0)),
            const((ms, 3 * LANES)),
        ] + [slab(mw) for mw in merge_weights],
        out_specs=[pl.BlockSpec((tm, TN), lambda h, j, i: (h * tiles + i, j)),
                   pl.BlockSpec((ms, TN),
                                lambda h, j, i: (0, jnp.where(h == halves - 1, j, n_col)))
                   ] + [slab(mw) for mw in merge_weights],
        out_shape=[jax.ShapeDtypeStruct((m, IN_WIDTH), F32),
                   jax.ShapeDtypeStruct((ms, IN_WIDTH + TN), F32)
                   ] + [jax.ShapeDtypeStruct(mw.shape, BF16) for mw in merge_weights],
        scratch_shapes=[pltpu.VMEM((rows_half, D_MODEL), BF16), pltpu.VMEM((D_MODEL, TN), BF16)],
        compiler_params=pltpu.CompilerParams(
            dimension_semantics=("arbitrary", "arbitrary", "arbitrary"),
            vmem_limit_bytes=VMEM_LIMIT),
        name="inproj",
    )(x, gain, xs, w, qg, kg, e, rope_p, rope_s, *merge_weights)


def _dup_head(x, g, lo):
    r = pltpu.roll(x, HEAD_DIM, 1)
    return jnp.where(lo, x, r) if g % 2 == 0 else jnp.where(lo, r, x)


def _prompt_attn_kernel(sinks_ref, q_ref, kvc_ref, kvp_ref, kvm_ref, vmt_ref, bias_ref, o_ref):
    n_blk = q_ref.shape[0] // BLOCK
    lo = lax.broadcasted_iota(jnp.int32, (1, LANES), 1) < HEAD_DIM
    head_lane = lax.broadcasted_iota(jnp.int32, (1, 4 * BLOCK), 1) // BLOCK

    def prev_rows(blk, cols):
        if blk == 0:
            return kvp_ref[:, cols]
        return kvc_ref[(blk - 1) * BLOCK:blk * BLOCK, cols]

    scores = []
    for blk in range(n_blk):
        rows = slice(blk * BLOCK, (blk + 1) * BLOCK)
        for g in range(N_KV_HEADS):
            ks = slice((g // 2) * LANES, (g // 2 + 1) * LANES)
            k_all = jnp.concatenate(
                [_dup_head(prev_rows(blk, ks), g, lo), _dup_head(kvc_ref[rows, ks], g, lo),
                 _dup_head(kvm_ref[:, ks], g, lo)], axis=0).astype(BF16)
            qa = q_ref[rows, (2 * g) * LANES:(2 * g + 1) * LANES]
            qb = q_ref[rows, (2 * g + 1) * LANES:(2 * g + 2) * LANES]
            q4 = jnp.concatenate([jnp.where(lo, qa, 0.0), jnp.where(lo, 0.0, qa),
                                  jnp.where(lo, qb, 0.0), jnp.where(lo, 0.0, qb)],
                                 axis=0).astype(BF16)
            scores.append(lax.dot_general(k_all, q4, NT_DIMS, preferred_element_type=F32))
    for blk in range(n_blk):
        rows = slice(blk * BLOCK, (blk + 1) * BLOCK)
        bias = bias_ref[jnp.minimum(pl.program_id(1), 1)] if blk == 0 else bias_ref[1]
        vs = slice(KV_WIDTH, 2 * KV_WIDTH)
        v_t = jnp.concatenate([prev_rows(blk, vs).T, kvc_ref[rows, vs].T], axis=1)
        for g in range(N_KV_HEADS):
            s = scores[blk * N_KV_HEADS + g]
            s_w = s[:2 * BLOCK] + bias
            s_m = s[2 * BLOCK:]
            sink = LOG2E * jnp.where(head_lane == 0, sinks_ref[4 * g],
                                     jnp.where(head_lane == 1, sinks_ref[4 * g + 1],
                                               jnp.where(head_lane == 2, sinks_ref[4 * g + 2],
                                                         sinks_ref[4 * g + 3])))
            m = jnp.maximum(jnp.maximum(jnp.max(s_w, axis=0, keepdims=True),
                                        jnp.max(s_m, axis=0, keepdims=True)), sink)
            p_w = jnp.exp2(s_w - m)
            p_m = jnp.exp2(s_m - m)
            den = (jnp.sum(p_w, axis=0, keepdims=True) + jnp.sum(p_m, axis=0, keepdims=True)
                   + jnp.exp2(sink - m))
            p = jnp.concatenate([p_w, p_m], axis=0).astype(BF16)
            hs = slice(g * HEAD_DIM, (g + 1) * HEAD_DIM)
            vt_g = jnp.concatenate([v_t[hs], vmt_ref[hs, :]], axis=1).astype(BF16)
            o_t = jnp.dot(vt_g, p, preferred_element_type=F32) * (1.0 / den)
            for pair in range(2):
                two = jnp.concatenate(
                    [o_t[:, (2 * pair) * BLOCK:(2 * pair + 1) * BLOCK],
                     o_t[:, (2 * pair + 1) * BLOCK:(2 * pair + 2) * BLOCK]], axis=0)
                o_ref[rows, (2 * g + pair) * LANES:(2 * g + pair + 1) * LANES] = two.T


def _prompt_attention(sinks, h_p, h_s, vm_t, batch, seq):
    nb = seq // BLOCK
    kvb = COL_KV // (2 * KV_WIDTH)
    kj = np.arange(2 * BLOCK)[:, None]
    qi = np.arange(4 * BLOCK)[None, :] % BLOCK
    cur = (kj >= BLOCK) & (kj - BLOCK <= qi)
    prev = (kj < BLOCK) & (kj > qi)
    bias = jnp.asarray(np.where(np.stack([cur, cur | prev]), 0.0, NEG), F32)
    per = ATTN_BLOCKS_PER_STEP
    ns = nb // per
    return pl.pallas_call(
        _prompt_attn_kernel,
        grid=(batch, ns),
        in_specs=[
            pl.BlockSpec(memory_space=pltpu.SMEM),
            pl.BlockSpec((per * BLOCK, ATTN_WIDTH), lambda b, i: (b * ns + i, 0)),
            pl.BlockSpec((per * BLOCK, 2 * KV_WIDTH), lambda b, i: (b * ns + i, kvb)),
            pl.BlockSpec((BLOCK, 2 * KV_WIDTH),
                         lambda b, i: (b * nb + jnp.maximum(i * per - 1, 0), kvb)),
            pl.BlockSpec((N_META, 2 * KV_WIDTH), lambda b, i: (h_s.shape[0] // N_META - 1, kvb)),
            pl.BlockSpec((KV_WIDTH, N_META), lambda b, i: (0, 0)),
            pl.BlockSpec((2, 2 * BLOCK, 4 * BLOCK), lambda b, i: (0, 0, 0)),
        ],
        out_specs=pl.BlockSpec((per * BLOCK, ATTN_WIDTH), lambda b, i: (b * ns + i, 0)),
        out_shape=jax.ShapeDtypeStruct((batch * seq, ATTN_WIDTH), F32),
        compiler_params=pltpu.CompilerParams(
            dimension_semantics=("arbitrary", "arbitrary"), vmem_limit_bytes=VMEM_LIMIT),
        name="prompt_attn",
    )(sinks, h_p, h_p, h_p, h_s, vm_t, bias)


def _sample_attn_kernel(qp_ref, kn_ref, vn_ref, knr_ref, vnr_ref, ck_hbm, cv_hbm, mk_ref, mv_ref,
                        sink_ref, o_ref, ok_ref, ov_ref, kbuf, vbuf, sem, *, n_steps):
    bt = kbuf.shape[1]
    t = pl.program_id(0)

    def copies(step, ring):
        rows = pl.ds(step * bt, bt)
        return (pltpu.make_async_copy(ck_hbm.at[rows], kbuf.at[ring], sem.at[0, ring]),
                pltpu.make_async_copy(cv_hbm.at[rows], vbuf.at[ring], sem.at[1, ring]))

    def fetch(step, ring):
        for c in copies(step, ring):
            c.start()

    @pl.when(t == 0)
    def _():
        fetch(0, 0)
        if n_steps > 1:
            fetch(1, 1)

    @pl.when(t + 2 < n_steps)
    def _():
        fetch(t + 2, (t + 2) % 3)

    ring = t % 3
    for c in copies(t, ring):
        c.wait()
    slot = lax.broadcasted_iota(jnp.int32, (1, 1, WINDOW), 2)
    round_bf16 = lambda a: a.astype(BF16).astype(F32)
    qp = qp_ref[...].astype(BF16)
    ck = kbuf[ring]
    cv = vbuf[ring]
    k_m = mk_ref[...].astype(BF16)
    v_m = mv_ref[...].astype(BF16)
    s_w = jnp.einsum("bhc,bcw->bhw", qp, ck.astype(BF16), preferred_element_type=F32)
    s_w = jnp.where(slot == 0, NEG, s_w)
    s_m = jnp.einsum("bhc,bkc->bhk", qp, k_m, preferred_element_type=F32)
    knr, vnr = round_bf16(knr_ref[...]), round_bf16(vnr_ref[...])
    qf = qp.astype(F32)
    s_n = jnp.stack([jnp.sum(qf[b] * knr[b:b + 1, :], axis=-1, keepdims=True)
                     for b in range(bt)])
    sink = LOG2E * sink_ref[...][None]
    m = jnp.maximum(jnp.maximum(jnp.max(s_w, axis=-1, keepdims=True),
                                jnp.max(s_m, axis=-1, keepdims=True)),
                    jnp.maximum(s_n, sink))
    p_w = jnp.exp2(s_w - m)
    p_m = jnp.exp2(s_m - m)
    p_n = jnp.exp2(s_n - m)
    den = (jnp.sum(p_w, axis=-1, keepdims=True) + jnp.sum(p_m, axis=-1, keepdims=True)
           + p_n + jnp.exp2(sink - m))
    inv = 1.0 / den
    o = (jnp.einsum("bhw,bcw->bhc", (p_w * inv).astype(BF16), cv.astype(BF16),
                    preferred_element_type=F32)
         + jnp.einsum("bhk,bkc->bhc", (p_m * inv).astype(BF16), v_m,
                      preferred_element_type=F32)
         + jnp.stack([round_bf16(p_n[b] * inv[b]) * vnr[b:b + 1, :] for b in range(bt)]))

    last = lax.broadcasted_iota(jnp.int32, (1, WINDOW), 1) == WINDOW - 1
    kn = kn_ref[0]
    vn = vn_ref[0]
    ck = pltpu.roll(ck, WINDOW - 1, 2)
    cv = pltpu.roll(cv, WINDOW - 1, 2)
    ok_ref[...] = jnp.stack([jnp.where(last, kn[:, b:b + 1], ck[b]) for b in range(bt)])
    ov_ref[...] = jnp.stack([jnp.where(last, vn[:, b:b + 1], cv[b]) for b in range(bt)])
    hh = lax.broadcasted_iota(jnp.int32, (1, N_Q_HEADS, KV_WIDTH), 1) // (N_Q_HEADS // N_KV_HEADS)
    cc = lax.broadcasted_iota(jnp.int32, (1, N_Q_HEADS, KV_WIDTH), 2) // HEAD_DIM
    o = jnp.where(hh == cc, o, 0.0)
    o = o[:, :, :LANES] + o[:, :, LANES:]
    o_ref[...] = o + pltpu.roll(o, HEAD_DIM, 2)


def _sample_attention(qp, kn_t, vn_t, h_s, cache_k, cache_v, meta_k, meta_v, sink_col, bt):
    n = cache_k.shape[0]
    kvb = COL_KV // KV_WIDTH
    win_spec = pl.BlockSpec((bt, KV_WIDTH, WINDOW), lambda t: (t, 0, 0))
    new_spec = pl.BlockSpec((1, KV_WIDTH, bt), lambda t: (t, 0, 0))
    meta_spec = pl.BlockSpec((bt, N_META, KV_WIDTH), lambda t: (t, 0, 0))
    hbm_spec = pl.BlockSpec(memory_space=pl.ANY)
    ring_shape = (3, bt, KV_WIDTH, WINDOW)
    return pl.pallas_call(
        functools.partial(_sample_attn_kernel, n_steps=n // bt),
        grid=(n // bt,),
        in_specs=[
            pl.BlockSpec((bt, N_Q_HEADS, KV_WIDTH), lambda t: (t, 0, 0)),
            new_spec, new_spec,
            pl.BlockSpec((bt, KV_WIDTH), lambda t: (t, kvb)),
            pl.BlockSpec((bt, KV_WIDTH), lambda t: (t, kvb + 1)),
            hbm_spec, hbm_spec, meta_spec, meta_spec,
            pl.BlockSpec((N_Q_HEADS, 1), lambda t: (0, 0)),
        ],
        out_specs=[
            pl.BlockSpec((bt, N_Q_HEADS, LANES), lambda t: (t, 0, 0)),
            win_spec, win_spec,
        ],
        out_shape=[
            jax.ShapeDtypeStruct((n, N_Q_HEADS, LANES), F32),
            jax.ShapeDtypeStruct(cache_k.shape, F32),
            jax.ShapeDtypeStruct(cache_v.shape, F32),
        ],
        scratch_shapes=[pltpu.VMEM(ring_shape, F32), pltpu.VMEM(ring_shape, F32),
                        pltpu.SemaphoreType.DMA((2, 3))],
        compiler_params=pltpu.CompilerParams(
            dimension_semantics=("arbitrary",), vmem_limit_bytes=VMEM_LIMIT),
        name="sample_attn",
    )(qp, kn_t, vn_t, h_s, h_s, cache_k, cache_v, meta_k, meta_v, sink_col)


def _gelu_tanh(x):
    c = 0.7978845608028654
    half = 0.5 * x
    return half + half * jnp.tanh(x * (c + (c * 0.044715) * (x * x)))


def _ssm_kernel(u_ref, um_ref, us_ref, hr_ref, hi_ref, t8_ref, win_ref, wst_ref, wc_ref,
                ar_ref, ai_ref, lr_ref, li_ref, pr_ref, pi_ref, d_ref,
                y_ref, sre_ref, sim_ref, ys_ref, nr_ref, ni_ref, xr_ref, xi_ref):
    dsk = d_ref[...]
    us = us_ref[...]
    bu = jnp.dot(us.astype(BF16), win_ref[0, (CHUNK - 1) * LANES:, :],
                 preferred_element_type=F32)
    l_r, l_i = lr_ref[0], li_ref[0]
    h_r, h_i = hr_ref[...], hi_ref[...]
    n_r = l_r * h_r - l_i * h_i + bu[:, :STATE_TILE]
    n_i = l_r * h_i + l_i * h_r + bu[:, STATE_TILE:]
    nr_ref[...] = n_r
    ni_ref[...] = n_i
    hn = jnp.concatenate([n_r, n_i], axis=1).astype(BF16)
    ys_ref[...] = _gelu_tanh(
        lax.dot_general(hn, wc_ref[0], NT_DIMS, preferred_element_type=F32) + dsk * us)

    batch = sre_ref.shape[0]
    nc = u_ref.shape[0] // CHUNK // batch
    win = win_ref[0]
    dsk_row = jnp.concatenate([dsk] * CHUNK, axis=1)

    def chunks_of(b):
        r0 = b * nc * CHUNK
        return jnp.concatenate(
            [u_ref[pl.ds(r0 + j, nc, stride=CHUNK), :] for j in range(CHUNK)], axis=1)

    n_mc = um_ref.shape[0] // CHUNK
    um = jnp.concatenate([um_ref[pl.ds(j, n_mc, stride=CHUNK), :] for j in range(CHUNK)], axis=1)
    gm = jnp.dot(um.astype(BF16), win, preferred_element_type=F32)
    a_r, a_i = ar_ref[0, 0:1, :], ai_ref[0, 0:1, :]
    s0r, s0i = gm[0:1, :STATE_TILE], gm[0:1, STATE_TILE:]
    for c in range(1, n_mc):
        s0r, s0i = (a_r * s0r - a_i * s0i + gm[c:c + 1, :STATE_TILE],
                    a_r * s0i + a_i * s0r + gm[c:c + 1, STATE_TILE:])

    n_t = nc // SUBLANES
    row = lax.broadcasted_iota(jnp.int32, (nc, 1), 0)
    first = row == 0
    in_tile = row % SUBLANES
    t_row = lax.broadcasted_iota(jnp.int32, (n_t, 1), 0)
    first_t = t_row == 0
    n_local = SUBLANES.bit_length() - 1
    w_r, w_i = pr_ref[0], pi_ref[0]

    def tile_totals(scr_ref, x):
        parts = []
        for c in range(STATE_TILE // LANES):
            scr_ref[c] = x[:, c * LANES:(c + 1) * LANES]
            parts.append(scr_ref[c, pl.ds(SUBLANES - 1, n_t, stride=SUBLANES), :])
        return jnp.concatenate(parts, axis=1)

    def shifted(x, d, keep):
        if d < SUBLANES:
            return jnp.where(keep, pltpu.roll(x, d, 0), 0.0)
        return jnp.concatenate([jnp.zeros((d, STATE_TILE), F32), x[:x.shape[0] - d]], axis=0)

    ucat_all = jnp.concatenate([chunks_of(b) for b in range(batch)], axis=0)
    ub_all = ucat_all.astype(BF16)
    g_all = jnp.dot(ub_all, win, preferred_element_type=F32)
    y_intra_all = jnp.concatenate(
        [jnp.dot(ub_all[:, :(2 * m + 2) * LANES],
                 t8_ref[0, :(2 * m + 2) * LANES, 2 * m * LANES:(2 * m + 2) * LANES],
                 preferred_element_type=F32) for m in range(CHUNK // 2)],
        axis=1) + dsk_row * ucat_all
    for b in range(batch):
        g, y_intra = g_all[b * nc:(b + 1) * nc], y_intra_all[b * nc:(b + 1) * nc]
        gr, gi = g[:, :STATE_TILE], g[:, STATE_TILE:]
        for k in range(n_local):
            d = 1 << k
            k_r, k_i = ar_ref[0, k:k + 1, :], ai_ref[0, k:k + 1, :]
            sh_r, sh_i = shifted(gr, d, in_tile >= d), shifted(gi, d, in_tile >= d)
            gr, gi = gr + (k_r * sh_r - k_i * sh_i), gi + (k_r * sh_i + k_i * sh_r)
        tot_r, tot_i = tile_totals(xr_ref, gr), tile_totals(xi_ref, gi)
        k_r, k_i = ar_ref[0, n_local:n_local + 1, :], ai_ref[0, n_local:n_local + 1, :]
        tot_r = tot_r + jnp.where(first_t, k_r * s0r - k_i * s0i, 0.0)
        tot_i = tot_i + jnp.where(first_t, k_r * s0i + k_i * s0r, 0.0)
        for k in range(SCAN_LEVELS - n_local):
            d = 1 << k
            if d >= n_t:
                break
            k_r = ar_ref[0, n_local + k:n_local + k + 1, :]
            k_i = ai_ref[0, n_local + k:n_local + k + 1, :]
            sh_r, sh_i = shifted(tot_r, d, t_row >= d), shifted(tot_i, d, t_row >= d)
            tot_r, tot_i = tot_r + (k_r * sh_r - k_i * sh_i), tot_i + (k_r * sh_i + k_i * sh_r)
        sre_ref[b, 0] = tot_r[n_t - 1:n_t, :]
        sim_ref[b, 0] = tot_i[n_t - 1:n_t, :]
        c_r = jnp.where(first_t, s0r, pltpu.roll(tot_r, 1, 0))
        c_i = jnp.where(first_t, s0i, pltpu.roll(tot_i, 1, 0))
        full_r, full_i = [], []
        for t in range(n_t):
            rs = slice(t * SUBLANES, (t + 1) * SUBLANES)
            e_r, e_i = c_r[t:t + 1, :], c_i[t:t + 1, :]
            full_r.append(gr[rs] + (w_r * e_r - w_i * e_i))
            full_i.append(gi[rs] + (w_r * e_i + w_i * e_r))
        gr, gi = jnp.concatenate(full_r, axis=0), jnp.concatenate(full_i, axis=0)
        sp = jnp.concatenate([jnp.where(first, s0r, pltpu.roll(gr, 1, 0)),
                              jnp.where(first, s0i, pltpu.roll(gi, 1, 0))],
                             axis=1).astype(BF16)
        y2 = y_intra + lax.dot_general(sp, wst_ref[0], NT_DIMS, preferred_element_type=F32)
        for j in range(CHUNK):
            sl = slice(j * LANES, (j + 1) * LANES)
            y_ref[pl.ds(b * nc * CHUNK + j, nc, stride=CHUNK), :] = _gelu_tanh(y2[:, sl])


def _ssm(h_p, h_s, h0r, h0i, t8, win, wst, wc, a_r, a_i, l_r, l_i, p_r, p_i, d_skip, batch, n_s):
    ub = COL_U // LANES
    rows = h_p.shape[0]
    wspec = pl.BlockSpec((1, CHUNK_WIDTH, CHUNK_WIDTH), lambda gt: (gt, 0, 0))
    aspec = pl.BlockSpec((1, SCAN_LEVELS, STATE_TILE), lambda gt: (gt, 0, 0))
    sspec = pl.BlockSpec((batch, 1, 1, STATE_TILE), lambda gt: (0, gt, 0, 0))
    st = pl.BlockSpec((n_s, STATE_TILE), lambda gt: (0, gt))
    lam = pl.BlockSpec((1, 1, STATE_TILE), lambda gt: (gt, 0, 0))
    return pl.pallas_call(
        _ssm_kernel,
        grid=(N_GROUP_TILES,),
        in_specs=[
            pl.BlockSpec((rows, LANES), lambda gt: (0, ub + gt)),
            pl.BlockSpec((N_META, LANES), lambda gt: (h_s.shape[0] // N_META - 1, ub + gt)),
            pl.BlockSpec((n_s, LANES), lambda gt: (0, ub + gt)),
            st, st,
            wspec, wspec, wspec,
            pl.BlockSpec((1, LANES, 2 * STATE_TILE), lambda gt: (gt, 0, 0)),
            aspec, aspec, lam, lam,
            pl.BlockSpec((1, SUBLANES, STATE_TILE), lambda gt: (gt, 0, 0)),
            pl.BlockSpec((1, SUBLANES, STATE_TILE), lambda gt: (gt, 0, 0)),
            pl.BlockSpec((1, LANES), lambda gt: (0, gt)),
        ],
        out_specs=[pl.BlockSpec((rows, LANES), lambda gt: (0, gt)), sspec, sspec,
                   pl.BlockSpec((n_s, LANES), lambda gt: (0, gt)), st, st],
        out_shape=[
            jax.ShapeDtypeStruct((rows, SSM_WIDTH), F32),
            jax.ShapeDtypeStruct((batch, N_GROUP_TILES, 1, STATE_TILE), F32),
            jax.ShapeDtypeStruct((batch, N_GROUP_TILES, 1, STATE_TILE), F32),
            jax.ShapeDtypeStruct((n_s, SSM_WIDTH), F32),
            jax.ShapeDtypeStruct((n_s, N_SSM_GROUPS * SSM_STATE), F32),
            jax.ShapeDtypeStruct((n_s, N_SSM_GROUPS * SSM_STATE), F32),
        ],
        scratch_shapes=[pltpu.VMEM((STATE_TILE // LANES, rows // CHUNK // batch, LANES), F32)] * 2,
        compiler_params=pltpu.CompilerParams(
            dimension_semantics=("arbitrary",), vmem_limit_bytes=VMEM_LIMIT),
        name="ssm",
    )(h_p, h_s, h_s, h0r, h0i, t8, win, wst, wc, a_r, a_i, l_r, l_i, p_r, p_i, d_skip)


def _merge_kernel(o_ref, za_ref, zs_ref, ga_ref, gs_ref, y_ref, x_ref,
                  o2_ref, za2_ref, zs2_ref, ga2_ref, gs2_ref, y2_ref, x2_ref,
                  bglu_ref, wglu_ref, wa_ref, ws_ref, wo_ref, out_ref, out2_ref):
    def merge(o_r, za_r, zs_r, ga_r, gs_r, y_r, x_r, out_r):
        za = za_r[...]
        a_in = (o_r[...] * (za * jax.nn.sigmoid(za))).astype(BF16)
        gated_a = jax.nn.sigmoid(ga_r[...]) * jnp.dot(a_in, wa_ref[...],
                                                      preferred_element_type=F32)
        y = y_r[...]
        t = jnp.dot(y.astype(BF16), wglu_ref[...], preferred_element_type=F32) + bglu_ref[...]
        y = y * jax.nn.sigmoid(t)
        zs = zs_r[...]
        s_in = (y * (zs * jax.nn.sigmoid(zs))).astype(BF16)
        br_s = jnp.dot(s_in, ws_ref[...], preferred_element_type=F32)
        mix = (gated_a + jax.nn.sigmoid(gs_r[...]) * br_s).astype(BF16)
        out_r[...] = x_r[...] + jnp.dot(mix, wo_ref[...], preferred_element_type=F32)

    merge(o_ref, za_ref, zs_ref, ga_ref, gs_ref, y_ref, x_ref, out_ref)

    @pl.when(pl.program_id(0) == pl.num_programs(0) - 1)
    def _():
        merge(o2_ref, za2_ref, zs2_ref, ga2_ref, gs2_ref, y2_ref, x2_ref, out2_ref)


def _merge(o, h, y, x, o2, h2, y2, x2, b_glu, w_glu, w_a, w_s, w_o, tm):
    m, m2 = x.shape[0], x2.shape[0]
    const = lambda shape: pl.BlockSpec(shape, lambda i: (0, 0), pipeline_mode=pl.Buffered(1))
    rows = lambda width, col: pl.BlockSpec((tm, width), lambda i: (i, col))
    rows2 = lambda width, col: pl.BlockSpec((m2, width), lambda i: (0, col),
                                            pipeline_mode=pl.Buffered(1))
    operands = lambda spec: [
        spec(ATTN_WIDTH, 0), spec(ATTN_WIDTH, COL_ZA // ATTN_WIDTH),
        spec(SSM_WIDTH, COL_ZS // SSM_WIDTH), spec(D_MODEL, COL_GA // D_MODEL),
        spec(D_MODEL, COL_GS // D_MODEL), spec(SSM_WIDTH, 0), spec(D_MODEL, 0)]
    return pl.pallas_call(
        _merge_kernel,
        grid=(m // tm,),
        in_specs=operands(rows) + operands(rows2) + [
            const((1, SSM_WIDTH)),
            const((SSM_WIDTH, SSM_WIDTH)),
            const((ATTN_WIDTH, D_MODEL)),
            const((SSM_WIDTH, D_MODEL)),
            const((D_MODEL, D_MODEL)),
        ],
        out_specs=[pl.BlockSpec((tm, D_MODEL), lambda i: (i, 0)),
                   pl.BlockSpec((m2, D_MODEL), lambda i: (0, 0))],
        out_shape=[jax.ShapeDtypeStruct((m, D_MODEL), F32),
                   jax.ShapeDtypeStruct((m2, D_MODEL), F32)],
        compiler_params=pltpu.CompilerParams(
            dimension_semantics=("arbitrary",), vmem_limit_bytes=VMEM_LIMIT),
        name="merge",
    )(o, h, h, h, h, y, x, o2, h2, h2, h2, h2, y2, x2, b_glu, w_glu, w_a, w_s, w_o)


def _rope_tables(pos):
    half = HEAD_DIM // 2
    inv_freq = (ROPE_THETA ** (-np.arange(half, dtype=np.float32) / half)).astype(np.float32)
    ang = np.asarray(pos, np.float32)[:, None] * inv_freq[None, :]
    cos, sin = np.cos(ang).astype(np.float32), np.sin(ang).astype(np.float32)
    zero = np.zeros_like(sin)
    reps = LANES // HEAD_DIM
    cos_t = np.tile(np.concatenate([cos, cos], axis=1), (1, reps))
    sin_a = np.tile(np.concatenate([-sin, zero], axis=1), (1, reps))
    sin_b = np.tile(np.concatenate([zero, sin], axis=1), (1, reps))
    return jnp.asarray(np.concatenate([cos_t, sin_a, sin_b], axis=1))


def _split_bf16(x):
    hi = x.astype(BF16)
    return hi, (x - hi.astype(F32)).astype(BF16)


def _dot_nt_f32(a, b):
    ah, al = _split_bf16(a)
    bh, bl = _split_bf16(b)
    dot = lambda x, y: lax.dot_general(x, y, NT_DIMS, preferred_element_type=F32)
    return dot(ah, bh) + dot(ah, bl) + dot(al, bh)


def _ssm_tables_kernel(are_ref, aim_ref, ldt_ref, bre_ref, bim_ref, cre_ref, cim_ref,
                       win_ref, wst_ref, t8_ref, wc_ref, ar_ref, ai_ref, lr_ref, li_ref,
                       pr_ref, pi_ref):
    a_r, a_i = are_ref[0], aim_ref[0]
    dt = jnp.exp(ldt_ref[0])
    mag = jnp.exp(a_r * dt)
    l_r, l_i = mag * jnp.cos(a_i * dt), mag * jnp.sin(a_i * dt)
    den = a_r * a_r + a_i * a_i
    f_r = ((l_r - 1.0) * a_r + l_i * a_i) / den
    f_i = (l_i * a_r - (l_r - 1.0) * a_i) / den
    shape = (LANES, STATE_TILE)
    own = (lax.broadcasted_iota(jnp.int32, shape, 0) // SSM_GROUP
           == lax.broadcasted_iota(jnp.int32, shape, 1) // SSM_STATE)
    sq = (LANES, LANES)
    same_group = (lax.broadcasted_iota(jnp.int32, sq, 0) // SSM_GROUP
                  == lax.broadcasted_iota(jnp.int32, sq, 1) // SSM_GROUP)
    lo = lax.broadcasted_iota(jnp.int32, (1, LANES), 1) < SSM_STATE
    n_rep = STATE_TILE // LANES

    def expand(x):
        return jnp.where(own, jnp.concatenate([x] * n_rep, axis=1), 0.0)

    def compact(v):
        e = jnp.where(own, jnp.broadcast_to(v, shape), 0.0)
        s = e[:, :LANES]
        for k in range(1, n_rep):
            s = s + e[:, k * LANES:(k + 1) * LANES]
        return s + pltpu.roll(s, SSM_STATE, 1)

    lc_r, lc_i = compact(l_r), compact(l_i)
    b_r, b_i = bre_ref[0], bim_ref[0]
    c_r, c_i = cre_ref[0], cim_ref[0]
    bb_r, bb_i = _cmul(compact(f_r), compact(f_i), b_r, b_i)
    wc_ref[0] = jnp.concatenate([expand(c_r), expand(-c_i)], axis=1).astype(BF16)
    c_cat = jnp.where(lo, c_r, -c_i)
    q_r, q_i = jnp.ones(sq, F32), jnp.zeros(sq, F32)
    for n in range(CHUNK):
        x_r, x_i = _cmul(q_r, q_i, bb_r, bb_i)
        win_ref[0, (CHUNK - 1 - n) * LANES:(CHUNK - n) * LANES, :] = jnp.concatenate(
            [expand(x_r), expand(x_i)], axis=1).astype(BF16)
        k_n = _dot_nt_f32(jnp.where(lo, x_r, x_i), c_cat)
        k_n = jnp.where(same_group, k_n, 0.0).astype(BF16)
        for j in range(CHUNK - n):
            t8_ref[0, j * LANES:(j + 1) * LANES, (j + n) * LANES:(j + n + 1) * LANES] = k_n
            if n > 0:
                t8_ref[0, (j + n) * LANES:(j + n + 1) * LANES, j * LANES:(j + 1) * LANES] = (
                    jnp.zeros(sq, BF16))
        q_r, q_i = _cmul(q_r, q_i, lc_r, lc_i)
        s_r, s_i = _cmul(c_r, c_i, q_r, q_i)
        wst_ref[0, n * LANES:(n + 1) * LANES, :] = jnp.concatenate(
            [expand(s_r), expand(-s_i)], axis=1).astype(BF16)
    p_r, p_i = l_r, l_i
    for _ in range(CHUNK.bit_length() - 1):
        p_r, p_i = _cmul(p_r, p_i, p_r, p_i)
    t_r, t_i = p_r, p_i
    for i in range(SUBLANES):
        pr_ref[0, i:i + 1, :] = t_r
        pi_ref[0, i:i + 1, :] = t_i
        t_r, t_i = _cmul(t_r, t_i, p_r, p_i)
    for k in range(SCAN_LEVELS):
        ar_ref[0, k:k + 1, :] = p_r
        ai_ref[0, k:k + 1, :] = p_i
        p_r, p_i = _cmul(p_r, p_i, p_r, p_i)
    lr_ref[0] = l_r
    li_ref[0] = l_i


def _ssm_tables(a_re, a_im, log_dt, b_re, b_im, c_re, c_im):
    gt = N_GROUP_TILES
    lane_vec = lambda v: v.reshape(gt, 1, STATE_TILE)
    dup = lambda v: jnp.concatenate([v, v], axis=-1)
    rows = GROUPS_PER_TILE * SSM_GROUP
    b_t = lambda v: dup(v.transpose(0, 2, 1).reshape(gt, rows, SSM_STATE))
    c_t = lambda v: dup(v.reshape(gt, rows, SSM_STATE))
    vec = pl.BlockSpec((1, 1, STATE_TILE), lambda g: (g, 0, 0))
    par = pl.BlockSpec((1, rows, LANES), lambda g: (g, 0, 0))
    big = pl.BlockSpec((1, CHUNK_WIDTH, CHUNK_WIDTH), lambda g: (g, 0, 0))
    wcs = pl.BlockSpec((1, LANES, 2 * STATE_TILE), lambda g: (g, 0, 0))
    scan = pl.BlockSpec((1, SCAN_LEVELS, STATE_TILE), lambda g: (g, 0, 0))
    tile = pl.BlockSpec((1, SUBLANES, STATE_TILE), lambda g: (g, 0, 0))
    return pl.pallas_call(
        _ssm_tables_kernel,
        grid=(gt,),
        in_specs=[vec, vec, vec, par, par, par, par],
        out_specs=[big, big, big, wcs, scan, scan, vec, vec, tile, tile],
        out_shape=[
            jax.ShapeDtypeStruct((gt, CHUNK_WIDTH, 2 * STATE_TILE), BF16),
            jax.ShapeDtypeStruct((gt, CHUNK_WIDTH, 2 * STATE_TILE), BF16),
            jax.ShapeDtypeStruct((gt, CHUNK_WIDTH, CHUNK_WIDTH), BF16),
            jax.ShapeDtypeStruct((gt, LANES, 2 * STATE_TILE), BF16),
            jax.ShapeDtypeStruct((gt, SCAN_LEVELS, STATE_TILE), F32),
            jax.ShapeDtypeStruct((gt, SCAN_LEVELS, STATE_TILE), F32),
            jax.ShapeDtypeStruct((gt, 1, STATE_TILE), F32),
            jax.ShapeDtypeStruct((gt, 1, STATE_TILE), F32),
            jax.ShapeDtypeStruct((gt, SUBLANES, STATE_TILE), F32),
            jax.ShapeDtypeStruct((gt, SUBLANES, STATE_TILE), F32),
        ],
        compiler_params=pltpu.CompilerParams(
            dimension_semantics=("arbitrary",), vmem_limit_bytes=VMEM_LIMIT),
        name="ssm_tables",
    )(lane_vec(a_re), lane_vec(a_im), lane_vec(jnp.repeat(log_dt, SSM_STATE)),
      b_t(b_re), b_t(b_im), c_t(c_re), c_t(c_im))


def kernel(x_prompt, x_sample, cache_win_k, cache_win_v, cache_meta_k, cache_meta_v,
           state_ssm_re, state_ssm_im, meta_tokens, norm_gain, w_in, q_norm_gain, k_norm_gain,
           sinks, a_re, a_im, log_dt, b_re, b_im, c_re, c_im, d_skip, w_glu, b_glu,
           w_attn_out, w_ssm_out, w_out):
    depth = w_in.shape[0]
    assert depth == 1, "single-layer trunk"
    batch, seq = x_prompt.shape[:2]
    n_s = x_sample.shape[0]
    assert x_sample.shape[1] == 1 and seq % (CHUNK * BLOCK) == 0
    l = 0

    w = w_in[l]
    qg = jnp.tile(q_norm_gain[l], TN // HEAD_DIM)[None]
    kg = jnp.tile(k_norm_gain[l], KV_WIDTH // HEAD_DIM)[None]
    lane = np.arange(LANES)
    e128 = np.where((lane[:, None] // HEAD_DIM) == (lane[None, :] // HEAD_DIM),
                    1.0 / HEAD_DIM, 0.0).astype(np.float32)
    e = jnp.asarray(np.concatenate([e128, e128], axis=0), BF16)
    gain = norm_gain[l][None]

    pos_p = N_META + np.arange(seq)
    pos_small = np.concatenate([np.full((n_s,), PAST_LEN), np.arange(N_META)])
    x_small = jnp.concatenate([x_sample[:, 0, :], meta_tokens.astype(x_prompt.dtype)], axis=0)
    xp = x_prompt.reshape(batch * seq, D_MODEL)
    xn_s = _rmsnorm(x_small, gain, tm=n_s + N_META)
    h_p, h_s, w_glu_bf, w_a_bf, w_s_bf, w_o_bf = _inproj(
        xp, gain, xn_s, w, qg, kg, e, _rope_tables(pos_p), _rope_tables(pos_small),
        (w_glu[l], w_attn_out[l], w_ssm_out[l], w_out[l]), tm=INPROJ_ROWS, halves=INPROJ_HALVES)

    win, wst, t8, wc, a_r, a_i, l_r, l_i, p_r, p_i = _ssm_tables(
        a_re[l], a_im[l], log_dt[l], b_re[l], b_im[l], c_re[l], c_im[l])
    dsk = d_skip[l][None]

    vm_t = h_s[n_s:, COL_KV + KV_WIDTH:IN_WIDTH].T
    o_p = _prompt_attention(sinks[l], h_p, h_s, vm_t, batch, seq)
    q_s = h_s[:n_s, COL_Q:COL_Q + ATTN_WIDTH].reshape(n_s, N_Q_HEADS, 1, HEAD_DIM)
    sel = (np.arange(N_Q_HEADS)[:, None] // (N_Q_HEADS // N_KV_HEADS)
           == np.arange(N_KV_HEADS)[None, :]).astype(np.float32)
    qp = (q_s * sel[None, :, :, None]).reshape(n_s, N_Q_HEADS, KV_WIDTH)
    bt = SAMPLE_ATTN_BATCH
    to_t = lambda c: c.transpose(0, 2, 3, 1).reshape(n_s, KV_WIDTH, WINDOW)
    from_t = lambda c: c.reshape(n_s, N_KV_HEADS, HEAD_DIM, WINDOW).transpose(0, 3, 1, 2)[None]
    new_t = lambda c0: h_s[:n_s, c0:c0 + KV_WIDTH].reshape(n_s // bt, bt, KV_WIDTH).transpose(0, 2, 1)
    o_s, s_win_k, s_win_v = _sample_attention(
        qp, new_t(COL_KV), new_t(COL_KV + KV_WIDTH), h_s, to_t(cache_win_k[l]), to_t(cache_win_v[l]),
        cache_meta_k[l].reshape(n_s, N_META, KV_WIDTH), cache_meta_v[l].reshape(n_s, N_META, KV_WIDTH),
        sinks[l][:, None], bt=bt)
    o_s = o_s[:, :, :HEAD_DIM].reshape(n_s, ATTN_WIDTH)

    y_p, p_re, p_im, y_s, s_re, s_im = _ssm(
        h_p, h_s, state_ssm_re[l].reshape(n_s, -1), state_ssm_im[l].reshape(n_s, -1),
        t8, win, wst, wc, a_r, a_i, l_r, l_i, p_r, p_i, dsk, batch, n_s)
    y_prompt, y_sample = _merge(
        o_p, h_p, y_p, xp, o_s, h_s, y_s, x_sample[:, 0, :], b_glu[l][None],
        w_glu_bf, w_a_bf, w_s_bf, w_o_bf, tm=MERGE_ROWS)
    y_prompt = y_prompt.reshape(batch, seq, D_MODEL)
    y_sample = y_sample.reshape(n_s, 1, D_MODEL)

    kv_p = h_p.reshape(batch, seq, IN_WIDTH)[:, seq - WINDOW:, COL_KV:]
    kv_p = kv_p.reshape(batch, WINDOW, 2, N_KV_HEADS, HEAD_DIM)
    p_win_k = kv_p[:, :, 0][None]
    p_win_v = kv_p[:, :, 1][None]
    kv_m = h_s[n_s:, COL_KV:IN_WIDTH].reshape(N_META, 2, N_KV_HEADS, HEAD_DIM)
    p_meta_k = jnp.broadcast_to(kv_m[None, :, 0], (batch, N_META, N_KV_HEADS, HEAD_DIM))[None]
    p_meta_v = jnp.broadcast_to(kv_m[None, :, 1], (batch, N_META, N_KV_HEADS, HEAD_DIM))[None]
    p_ssm_re = p_re.reshape(batch, N_SSM_GROUPS, SSM_STATE)[None]
    p_ssm_im = p_im.reshape(batch, N_SSM_GROUPS, SSM_STATE)[None]
    st_shape = (1, n_s, N_SSM_GROUPS, SSM_STATE)
    return (y_prompt, y_sample, p_win_k, p_win_v, p_meta_k, p_meta_v, p_ssm_re, p_ssm_im,
            from_t(s_win_k), from_t(s_win_v), s_re.reshape(st_shape), s_im.reshape(st_shape))
```
